```python
import jax, jax.numpy as jnp
from jax import lax
import numpy as np

D_MODEL = 1024
BATCH = 8
SEQ = 2048
DEPTH = 2

GRID_W = 64
Q_BLOCK = 128
HEAD_DIM = 64
ROPE_THETA = 10000.0
RMS_EPS = 1e-6
LN_EPS = 1e-5
A_HEADS = 4
A_KV_HEADS = 2
B_HEADS = 4
NA_WIN_H = 8
NA_WIN_W = 16
C_HEADS = 4
C_NOPE = 64
C_ROPE = 32
C_V = 64
C_Q_RANK = 192
C_KV_RANK = 128
D_CH = 256
CONV_W = 31
N_BRANCH = 4
MIX_W = 256
D_FF = 2816
N_EXPERTS = 8
TOP_K = 2
D_FF_EXPERT = 3584
ALPHA = (2 * DEPTH) ** 0.25
BETA = (8 * DEPTH) ** -0.25
D_IN = ((A_HEADS + 2 * A_KV_HEADS) * HEAD_DIM + 3 * B_HEADS * HEAD_DIM
        + C_Q_RANK + C_KV_RANK + C_ROPE + 2 * D_CH + N_BRANCH * D_MODEL)

kernel_name = 'hybrid_gated_encoder_block'


def _split_points():
    sizes = (A_HEADS * HEAD_DIM, A_KV_HEADS * HEAD_DIM, A_KV_HEADS * HEAD_DIM,
             B_HEADS * HEAD_DIM, B_HEADS * HEAD_DIM, B_HEADS * HEAD_DIM,
             C_Q_RANK, C_KV_RANK, C_ROPE, 2 * D_CH, N_BRANCH * D_MODEL)
    return [int(v) for v in np.cumsum(sizes)[:-1]]


def layer_norm(x, g, b):
    xf = x.astype(jnp.float32)
    mu = jnp.mean(xf, axis=-1, keepdims=True)
    var = jnp.mean(jnp.square(xf - mu), axis=-1, keepdims=True)
    return ((xf - mu) * lax.rsqrt(var + LN_EPS) * g.astype(jnp.float32) + b.astype(jnp.float32)).astype(x.dtype)


def rms_norm(x, g):
    xf = x.astype(jnp.float32)
    return (xf * lax.rsqrt(jnp.mean(xf * xf, axis=-1, keepdims=True) + RMS_EPS) * g.astype(jnp.float32)).astype(x.dtype)


def axial_rope(seq, rot_dim):
    t = jnp.arange(seq)
    row = (t // GRID_W).astype(jnp.float32)
    col = (t % GRID_W).astype(jnp.float32)
    n_freq = rot_dim // 4
    inv = ROPE_THETA ** (-jnp.arange(n_freq, dtype=jnp.float32) / n_freq)
    ang = jnp.concatenate([row[:, None] * inv, col[:, None] * inv], axis=-1)
    return jnp.cos(ang), jnp.sin(ang)


def apply_rope(x, cos, sin):
    extra = x.ndim - 3
    c = cos.reshape(cos.shape[:1] + (1,) * extra + cos.shape[1:])
    s = sin.reshape(sin.shape[:1] + (1,) * extra + sin.shape[1:])
    xf = x.astype(jnp.float32)
    x1, x2 = xf[..., 0::2], xf[..., 1::2]
    out = jnp.stack([x1 * c - x2 * s, x1 * s + x2 * c], axis=-1).reshape(x.shape)
    return out.astype(x.dtype)


def block_attention(q, k, v):
    b, s, kvh, g, dk = q.shape
    nblk = s // Q_BLOCK
    scale = dk ** -0.5
    qb = q.reshape(b, nblk, Q_BLOCK, kvh, g, dk).transpose(1, 0, 2, 3, 4, 5)

    def one_block(qi):
        sc = jnp.einsum('bqhgd,bkhd->bhgqk', qi, k, preferred_element_type=jnp.float32) * scale
        p = jax.nn.softmax(sc, axis=-1)
        return jnp.einsum('bhgqk,bkhd->bqhgd', p.astype(v.dtype), v)

    out = lax.map(one_block, qb)
    return out.transpose(1, 0, 2, 3, 4, 5).reshape(b, s, kvh, g, v.shape[-1])


def neighbourhood_attention(q, k, v, rpb):
    b, s, h, d = q.shape
    rows = s // GRID_W
    wh = min(NA_WIN_H, rows)
    r = jnp.arange(rows)
    c = jnp.arange(GRID_W)
    row_start = jnp.clip(r - wh // 2, 0, rows - wh)
    key_rows = row_start[:, None] + jnp.arange(wh)[None, :]
    col_start = jnp.clip(c - NA_WIN_W // 2, 0, GRID_W - NA_WIN_W)
    col_ok = (c[None, :] >= col_start[:, None]) & (c[None, :] < col_start[:, None] + NA_WIN_W)
    qg = q.reshape(b, rows, GRID_W, h, d)
    kg = k.reshape(b, rows, GRID_W, h, d)[:, key_rows]
    vg = v.reshape(b, rows, GRID_W, h, d)[:, key_rows]
    sc = jnp.einsum('brqhd,brwkhd->bhrqwk', qg, kg, preferred_element_type=jnp.float32) * (d ** -0.5)
    dr_idx = key_rows - r[:, None] + (NA_WIN_H - 1)
    dc_idx = jnp.clip(c[None, :] - c[:, None] + (NA_WIN_W - 1), 0, 2 * NA_WIN_W - 2)
    bias = rpb[:, dr_idx[:, None, :, None], dc_idx[None, :, None, :]]
    sc = sc + bias.astype(jnp.float32)[None]
    sc = jnp.where(col_ok[:, None, :], sc, -jnp.inf)
    p = jax.nn.softmax(sc, axis=(-2, -1))
    out = jnp.einsum('bhrqwk,brwkhd->brqhd', p.astype(v.dtype), vg)
    return out.reshape(b, s, h * d)


def hybrid_mixer(x, w_in, a_q_norm, a_k_norm, b_rpb, c_q_norm, c_kv_norm, c_w_uq, c_w_ukv,
                 d_conv_w, d_conv_b, d_ln_g, d_ln_b, w_branch, w_out, rope_a, rope_c):
    b, s, _ = x.shape
    h = x @ w_in
    (a_q, a_k, a_v, b_q, b_k, b_v, c_cq, c_ckv, c_kpe, d_glu, gate_logits) = jnp.split(h, _split_points(), axis=-1)

    qa = apply_rope(rms_norm(a_q.reshape(b, s, A_HEADS, HEAD_DIM), a_q_norm), *rope_a)
    ka = apply_rope(rms_norm(a_k.reshape(b, s, A_KV_HEADS, HEAD_DIM), a_k_norm), *rope_a)
    va = a_v.reshape(b, s, A_KV_HEADS, HEAD_DIM)
    qa = qa.reshape(b, s, A_KV_HEADS, A_HEADS // A_KV_HEADS, HEAD_DIM)
    y_a = block_attention(qa, ka, va).reshape(b, s, MIX_W)

    y_b = neighbourhood_attention(b_q.reshape(b, s, B_HEADS, HEAD_DIM),
                                  b_k.reshape(b, s, B_HEADS, HEAD_DIM),
                                  b_v.reshape(b, s, B_HEADS, HEAD_DIM), b_rpb)

    q_c = (rms_norm(c_cq, c_q_norm) @ c_w_uq).reshape(b, s, C_HEADS, C_NOPE + C_ROPE)
    q_nope = q_c[..., :C_NOPE]
    q_pe = apply_rope(q_c[..., C_NOPE:], *rope_c)
    kv_c = (rms_norm(c_ckv, c_kv_norm) @ c_w_ukv).reshape(b, s, C_HEADS, C_NOPE + C_V)
    k_nope = kv_c[..., :C_NOPE]
    v_c = kv_c[..., C_NOPE:]
    k_pe = apply_rope(c_kpe, *rope_c)
    q_full = jnp.concatenate([q_nope, q_pe], axis=-1)[:, :, :, None, :]
    k_full = jnp.concatenate([k_nope, jnp.broadcast_to(k_pe[:, :, None, :], (b, s, C_HEADS, C_ROPE))], axis=-1)
    y_c = block_attention(q_full, k_full, v_c).reshape(b, s, MIX_W)

    u = d_glu[..., :D_CH] * jax.nn.sigmoid(d_glu[..., D_CH:])
    u = lax.conv_general_dilated(u, d_conv_w, window_strides=(1,), padding=[(CONV_W // 2, CONV_W // 2)],
                                 dimension_numbers=('NWC', 'WIO', 'NWC'), feature_group_count=D_CH) + d_conv_b
    y_d = jax.nn.silu(layer_norm(u, d_ln_g, d_ln_b))

    ys = jnp.stack([y_a, y_b, y_c, y_d], axis=2)
    branches = jnp.einsum('bsnc,ncd->bsnd', ys, w_branch)
    gates = jax.nn.sigmoid(gate_logits.reshape(b, s, N_BRANCH, D_MODEL))
    merged = jnp.sum(gates * branches, axis=2)
    return merged @ w_out


def swiglu(x, w1, w3, w2):
    return (jax.nn.silu(x @ w1) * (x @ w3)) @ w2


def moe_swiglu(x, router, w1, w3, w2):
    logits = jnp.einsum('bsd,de->bse', x.astype(jnp.float32), router.astype(jnp.float32))
    top_logit, top_idx = lax.top_k(logits, TOP_K)
    top_w = jax.nn.softmax(top_logit, axis=-1)
    gate = jnp.sum(jax.nn.one_hot(top_idx, N_EXPERTS, dtype=jnp.float32) * top_w[..., None], axis=-2)
    gate = gate.astype(x.dtype)
    out = jnp.zeros_like(x)
    for e in range(N_EXPERTS):
        out = out + gate[..., e:e + 1] * swiglu(x, w1[e], w3[e], w2[e])
    return out


def setup_inputs(seed: int = 0) -> dict:
    key = jax.random.key(seed)
    ks = jax.random.split(key, 28)

    def nrm(k, shape, scale):
        return jax.random.normal(k, shape, jnp.float32) * scale

    def gain(k, shape):
        return 1.0 + 0.05 * jax.random.normal(k, shape, jnp.float32)

    n_dense = (DEPTH + 1) // 2
    n_moe = DEPTH // 2
    return {
        'x': nrm(ks[0], (BATCH, SEQ, D_MODEL), 1.0),
        'ln_in_g': gain(ks[1], (D_MODEL,)),
        'ln_in_b': nrm(ks[2], (D_MODEL,), 0.02),
        'w_in': nrm(ks[3], (DEPTH, D_MODEL, D_IN), D_MODEL ** -0.5),
        'a_q_norm': gain(ks[4], (DEPTH, HEAD_DIM)),
        'a_k_norm': gain(ks[5], (DEPTH, HEAD_DIM)),
        'b_rpb': nrm(ks[6], (DEPTH, B_HEADS, 2 * NA_WIN_H - 1, 2 * NA_WIN_W - 1), 0.1),
        'c_q_norm': gain(ks[7], (DEPTH, C_Q_RANK)),
        'c_kv_norm': gain(ks[8], (DEPTH, C_KV_RANK)),
        'c_w_uq': nrm(ks[9], (DEPTH, C_Q_RANK, C_HEADS * (C_NOPE + C_ROPE)), C_Q_RANK ** -0.5),
        'c_w_ukv': nrm(ks[10], (DEPTH, C_KV_RANK, C_HEADS * (C_NOPE + C_V)), C_KV_RANK ** -0.5),
        'd_conv_w': nrm(ks[11], (DEPTH, CONV_W, 1, D_CH), CONV_W ** -0.5),
        'd_conv_b': nrm(ks[12], (DEPTH, D_CH), 0.02),
        'd_ln_g': gain(ks[13], (DEPTH, D_CH)),
        'd_ln_b': nrm(ks[14], (DEPTH, D_CH), 0.02),
        'w_branch': nrm(ks[15], (DEPTH, N_BRANCH, MIX_W, D_MODEL), MIX_W ** -0.5),
        'w_out': nrm(ks[16], (DEPTH, D_MODEL, D_MODEL), BETA * D_MODEL ** -0.5),
        'ln_mix_g': gain(ks[17], (DEPTH, D_MODEL)),
        'ln_mix_b': nrm(ks[18], (DEPTH, D_MODEL), 0.02),
        'ffn_w1': nrm(ks[19], (n_dense, D_MODEL, D_FF), D_MODEL ** -0.5),
        'ffn_w3': nrm(ks[20], (n_dense, D_MODEL, D_FF), D_MODEL ** -0.5),
        'ffn_w2': nrm(ks[21], (n_dense, D_FF, D_MODEL), BETA * D_FF ** -0.5),
        'moe_router': nrm(ks[22], (n_moe, D_MODEL, N_EXPERTS), D_MODEL ** -0.5),
        'moe_w1': nrm(ks[23], (n_moe, N_EXPERTS, D_MODEL, D_FF_EXPERT), D_MODEL ** -0.5),
        'moe_w3': nrm(ks[24], (n_moe, N_EXPERTS, D_MODEL, D_FF_EXPERT), D_MODEL ** -0.5),
        'moe_w2': nrm(ks[25], (n_moe, N_EXPERTS, D_FF_EXPERT, D_MODEL), BETA * D_FF_EXPERT ** -0.5),
        'ln_ffn_g': gain(ks[26], (DEPTH, D_MODEL)),
        'ln_ffn_b': nrm(ks[27], (DEPTH, D_MODEL), 0.02),
    }


def reference(x, ln_in_g, ln_in_b, w_in, a_q_norm, a_k_norm, b_rpb, c_q_norm, c_kv_norm, c_w_uq, c_w_ukv,
              d_conv_w, d_conv_b, d_ln_g, d_ln_b, w_branch, w_out, ln_mix_g, ln_mix_b,
              ffn_w1, ffn_w3, ffn_w2, moe_router, moe_w1, moe_w3, moe_w2, ln_ffn_g, ln_ffn_b):
    s = x.shape[1]
    rope_a = axial_rope(s, HEAD_DIM)
    rope_c = axial_rope(s, C_ROPE)
    x = layer_norm(x, ln_in_g, ln_in_b)
    for l in range(DEPTH):
        mix = hybrid_mixer(x, w_in[l], a_q_norm[l], a_k_norm[l], b_rpb[l], c_q_norm[l], c_kv_norm[l],
                           c_w_uq[l], c_w_ukv[l], d_conv_w[l], d_conv_b[l], d_ln_g[l], d_ln_b[l],
                           w_branch[l], w_out[l], rope_a, rope_c)
        x = layer_norm(ALPHA * x + mix, ln_mix_g[l], ln_mix_b[l])
        if l % 2 == 0:
            f = swiglu(x, ffn_w1[l // 2], ffn_w3[l // 2], ffn_w2[l // 2])
        else:
            f = moe_swiglu(x, moe_router[l // 2], moe_w1[l // 2], moe_w3[l // 2], moe_w2[l // 2])
        x = layer_norm(ALPHA * x + f, ln_ffn_g[l], ln_ffn_b[l])
    return x
```

```python
import functools

import numpy as np
import jax
import jax.numpy as jnp
from jax import lax
from jax.experimental import pallas as pl
from jax.experimental.pallas import tpu as pltpu

D_MODEL = 1024
DEPTH = 2
GRID_W = 64
HEAD_DIM = 64
ROPE_THETA = 10000.0
RMS_EPS = 1e-6
LN_EPS = 1e-5
A_HEADS = 4
A_KV_HEADS = 2
B_HEADS = 4
NA_WIN_H = 8
NA_WIN_W = 16
C_HEADS = 4
C_NOPE = 64
C_ROPE = 32
C_V = 64
C_Q_RANK = 192
C_KV_RANK = 128
D_CH = 256
CONV_W = 31
N_BRANCH = 4
MIX_W = 256
D_FF = 2816
N_EXPERTS = 8
TOP_K = 2
D_FF_EXPERT = 3584
ALPHA = (2 * DEPTH) ** 0.25

LANES = 128
VMEM_LIMIT = 48 * 1024 * 1024

NEG_BIG = -1e30

COL_GATES = 0
COL_AQ = 4096
COL_AKV = 4352
COL_C = 4608
COL_D = 5120
COL_BQ = 5632
COL_BK = 5888
COL_BV = 6144
D_IN_PAD = 6400

NA_QROWS = 4
NA_KROWS = 12

MOE_TILE = 512
MOE_FCHUNK = 896


def _in_proj_columns():
    o_aq, o_ak, o_av = 0, 256, 384
    o_bq, o_bk, o_bv = 512, 768, 1024
    o_cq, o_ckv, o_kpe = 1280, 1472, 1600
    o_d, o_g = 1632, 2144
    deint = np.concatenate([np.arange(0, HEAD_DIM, 2), np.arange(1, HEAD_DIM, 2)])
    cols = [o_g + np.arange(N_BRANCH * D_MODEL)]
    for g in range(2):
        for kvh in range(A_KV_HEADS):
            cols.append(o_aq + (kvh * 2 + g) * HEAD_DIM + deint)
    for kvh in range(A_KV_HEADS):
        cols.append(o_ak + kvh * HEAD_DIM + deint)
    cols.append(o_av + np.arange(A_KV_HEADS * HEAD_DIM))
    cols.append(o_cq + np.arange(C_Q_RANK))
    cols.append(np.full(64, -1))
    cols.append(o_ckv + np.arange(C_KV_RANK))
    cols.append(np.full(64, -1))
    cols.append(o_kpe + np.arange(0, C_ROPE, 2))
    cols.append(o_kpe + np.arange(1, C_ROPE, 2))
    cols.append(np.full(32, -1))
    cols.append(o_d + np.arange(2 * D_CH))
    cols.append(o_bq + np.arange(3 * B_HEADS * HEAD_DIM))
    cols = np.concatenate(cols)
    assert cols.shape[0] == D_IN_PAD
    return cols


def _take_cols(w, cols):
    wz = jnp.concatenate([w, jnp.zeros((w.shape[0], 1), w.dtype)], axis=1)
    return wz[:, np.where(cols < 0, w.shape[1], cols)]


def _rope_tables(seq):
    t = np.arange(seq)
    row = (t // GRID_W).astype(np.float32)
    col = (t % GRID_W).astype(np.float32)

    def angles(rot_dim):
        n_freq = rot_dim // 4
        inv = jnp.asarray(ROPE_THETA, jnp.float32) ** (-jnp.arange(n_freq, dtype=jnp.float32) / n_freq)
        return jnp.concatenate([jnp.asarray(row)[:, None] * inv, jnp.asarray(col)[:, None] * inv], axis=-1)

    ang_a = angles(HEAD_DIM)
    ca, sa = jnp.cos(ang_a), jnp.sin(ang_a)
    cos_a = jnp.concatenate([ca, ca, ca, ca], axis=-1)
    sin_a = jnp.concatenate([-sa, sa, -sa, sa], axis=-1)
    ang_c = angles(C_ROPE)
    cc, sc = jnp.cos(ang_c), jnp.sin(ang_c)
    one = jnp.ones((seq, 64), jnp.float32)
    zero = jnp.zeros((seq, 64), jnp.float32)
    cos_c = jnp.concatenate([one, cc, cc, one[:, :32]], axis=-1)
    sin_c = jnp.concatenate([zero, -sc, sc, zero[:, :32]], axis=-1)
    return cos_a, sin_a, cos_c, sin_c


def _na_bias_tables(rpb, rows):
    wh = min(NA_WIN_H, rows)
    tabs = []
    for r0 in (0, NA_QROWS, rows - NA_QROWS):
        ws = int(np.clip(r0 - 4, 0, rows - NA_KROWS))
        qr = r0 + np.arange(NA_QROWS)[:, None, None, None]
        qc = np.arange(GRID_W)[None, :, None, None]
        kr = ws + np.arange(NA_KROWS)[None, None, :, None]
        kc = np.arange(GRID_W)[None, None, None, :]
        row_start = np.clip(qr - wh // 2, 0, rows - wh)
        col_start = np.clip(qc - NA_WIN_W // 2, 0, GRID_W - NA_WIN_W)
        ok = (kr >= row_start) & (kr < row_start + wh) & (kc >= col_start) & (kc < col_start + NA_WIN_W)
        dr = np.clip(kr - qr + (NA_WIN_H - 1), 0, 2 * NA_WIN_H - 2)
        dc = np.clip(kc - qc + (NA_WIN_W - 1), 0, 2 * NA_WIN_W - 2)
        shape = (NA_QROWS, GRID_W, NA_KROWS, GRID_W)
        ok = np.broadcast_to(ok, shape).reshape(NA_QROWS * GRID_W, NA_KROWS * GRID_W)
        dr = np.broadcast_to(dr, shape).reshape(ok.shape)
        dc = np.broadcast_to(dc, shape).reshape(ok.shape)
        bias = rpb[:, dr, dc].astype(jnp.float32)
        tabs.append(jnp.where(jnp.asarray(ok)[None], bias, NEG_BIG))
    return jnp.stack(tabs, axis=0)


def _cparams(sem):
    return pltpu.CompilerParams(dimension_semantics=sem, vmem_limit_bytes=VMEM_LIMIT)


def _layer_norm_rows(z, g, b):
    mu = jnp.mean(z, axis=-1, keepdims=True)
    zc = z - mu
    var = jnp.mean(zc * zc, axis=-1, keepdims=True)
    return zc * lax.rsqrt(var + LN_EPS) * g + b


def _sigmoid(x):
    return 0.5 * jnp.tanh(0.5 * x) + 0.5


def _dot(a, b):
    return jnp.dot(a, b, preferred_element_type=jnp.float32)


def _dot_nt(a, b):
    return lax.dot_general(a, b, (((1,), (1,)), ((), ())), preferred_element_type=jnp.float32)


def _ln_kernel(x_ref, g_ref, b_ref, of_ref, ob_ref):
    y = _layer_norm_rows(x_ref[...], g_ref[...], b_ref[...])
    of_ref[...] = y
    ob_ref[...] = y.astype(jnp.bfloat16)


def _input_ln(x, g, b, tm=1024):
    t, d = x.shape
    return pl.pallas_call(
        _ln_kernel,
        out_shape=(jax.ShapeDtypeStruct((t, d), jnp.float32), jax.ShapeDtypeStruct((t, d), jnp.bfloat16)),
        grid=(t // tm,),
        in_specs=[pl.BlockSpec((tm, d), lambda i: (i, 0)),
                  pl.BlockSpec((1, d), lambda i: (0, 0)),
                  pl.BlockSpec((1, d), lambda i: (0, 0))],
        out_specs=(pl.BlockSpec((tm, d), lambda i: (i, 0)), pl.BlockSpec((tm, d), lambda i: (i, 0))),
        compiler_params=_cparams(("parallel",)),
    )(x, g.reshape(1, d), b.reshape(1, d))


def _mm_kernel(a_ref, b_ref, o_ref):
    o_ref[...] = _dot(a_ref[...], b_ref[...]).astype(o_ref.dtype)


def _matmul(a, b, out_dtype, tm, tn):
    m, k = a.shape
    n = b.shape[1]
    return pl.pallas_call(
        _mm_kernel,
        out_shape=jax.ShapeDtypeStruct((m, n), out_dtype),
        grid=(m // tm, n // tn),
        in_specs=[pl.BlockSpec((tm, k), lambda i, j: (i, 0)),
                  pl.BlockSpec((k, tn), lambda i, j: (0, j))],
        out_specs=pl.BlockSpec((tm, tn), lambda i, j: (i, j)),
        compiler_params=_cparams(("parallel", "parallel")),
    )(a, b)


def _softmax_pv(s, v):
    m = jnp.max(s, axis=-1, keepdims=True)
    p = jnp.exp(s - m)
    l = jnp.sum(p, axis=-1, keepdims=True)
    o = _dot(p.astype(jnp.bfloat16), v)
    return o * (1.0 / l)


def _rotate_pairs(x, cos, sin, half, lane):
    w = x.shape[-1]
    first = (lane % (2 * half)) < half
    partner = jnp.where(first, pltpu.roll(x, w - half, 1), pltpu.roll(x, half, 1))
    return x * cos + partner * sin


def _attn_a_kernel(q_ref, kv_ref, cos_ref, sin_ref, gq_ref, gk_ref, o_ref, qs_ref, ks_ref, *, tq):
    s_len = q_ref.shape[0]
    lane = lax.broadcasted_iota(jnp.int32, (s_len, LANES), 1)
    lo = lane < HEAD_DIM
    cos = cos_ref[...]
    sin = sin_ref[...]

    def prep(x, g):
        x2 = x * x
        ss_lo = jnp.sum(jnp.where(lo, x2, 0.0), axis=-1, keepdims=True)
        ss_hi = jnp.sum(jnp.where(lo, 0.0, x2), axis=-1, keepdims=True)
        ms = jnp.where(lo, ss_lo, ss_hi) * (1.0 / HEAD_DIM)
        xn = x * lax.rsqrt(ms + RMS_EPS) * g
        return _rotate_pairs(xn, cos, sin, HEAD_DIM // 2, lane)

    ks_ref[...] = prep(kv_ref[:, 0:LANES].astype(jnp.float32), gk_ref[...]).astype(jnp.bfloat16)
    scale = HEAD_DIM ** -0.5
    for g in range(2):
        qg = prep(q_ref[:, g * LANES:(g + 1) * LANES].astype(jnp.float32), gq_ref[...]) * scale
        qs_ref[:, g * LANES:(g + 1) * LANES] = qg.astype(jnp.bfloat16)

    lane_q = lax.broadcasted_iota(jnp.int32, (tq, LANES), 1)
    lo_q = lane_q < HEAD_DIM

    def body(i, carry):
        r = pl.multiple_of(i * tq, tq)
        k = ks_ref[...]
        v = kv_ref[:, LANES:2 * LANES]
        for g in range(2):
            qt = qs_ref[pl.ds(r, tq), g * LANES:(g + 1) * LANES]
            zero = jnp.zeros_like(qt)
            o_lo = _softmax_pv(_dot_nt(jnp.where(lo_q, qt, zero), k), v)
            o_hi = _softmax_pv(_dot_nt(jnp.where(lo_q, zero, qt), k), v)
            o_ref[pl.ds(r, tq), g * LANES:(g + 1) * LANES] = jnp.where(lo_q, o_lo, o_hi).astype(o_ref.dtype)
        return carry

    lax.fori_loop(0, s_len // tq, body, 0)


def _attn_a(h, cos_a, sin_a, gq, gk, batch, seq, tq=256):
    kern = functools.partial(_attn_a_kernel, tq=tq)
    return pl.pallas_call(
        kern,
        out_shape=jax.ShapeDtypeStruct((batch * seq, MIX_W), jnp.bfloat16),
        grid=(batch,),
        in_specs=[pl.BlockSpec((seq, 256), lambda b: (b, COL_AQ // 256)),
                  pl.BlockSpec((seq, 256), lambda b: (b, COL_AKV // 256)),
                  pl.BlockSpec((seq, LANES), lambda b: (0, 0)),
                  pl.BlockSpec((seq, LANES), lambda b: (0, 0)),
                  pl.BlockSpec((1, LANES), lambda b: (0, 0)),
                  pl.BlockSpec((1, LANES), lambda b: (0, 0))],
        out_specs=pl.BlockSpec((seq, MIX_W), lambda b: (b, 0)),
        scratch_shapes=[pltpu.VMEM((seq, 256), jnp.bfloat16), pltpu.VMEM((seq, LANES), jnp.bfloat16)],
        compiler_params=_cparams(("parallel",)),
    )(h, h, cos_a, sin_a, gq, gk)


def _attn_b_kernel(q_ref, k_ref, v_ref, bias_ref, o_ref, *, rows):
    nq = NA_QROWS * GRID_W
    nk = NA_KROWS * GRID_W
    lane_q = lax.broadcasted_iota(jnp.int32, (nq, LANES), 1)
    lo_q = lane_q < HEAD_DIM
    n_groups = rows // NA_QROWS
    scale = HEAD_DIM ** -0.5

    def body(gi, carry):
        ws = jnp.clip(gi * NA_QROWS - 4, 0, rows - NA_KROWS)
        cls = jnp.where(gi == 0, 0, jnp.where(gi == n_groups - 1, 2, 1))
        qr = pl.multiple_of(gi * nq, nq)
        kr = pl.multiple_of(ws * GRID_W, GRID_W)
        for blk in range(B_HEADS // 2):
            sl = slice(blk * LANES, (blk + 1) * LANES)
            qt = q_ref[pl.ds(qr, nq), sl] * scale
            kw = k_ref[pl.ds(kr, nk), sl]
            vw = v_ref[pl.ds(kr, nk), sl]
            zero = jnp.zeros_like(qt)
            s_lo = _dot_nt(jnp.where(lo_q, qt, zero), kw) + bias_ref[cls, 2 * blk]
            s_hi = _dot_nt(jnp.where(lo_q, zero, qt), kw) + bias_ref[cls, 2 * blk + 1]
            o_lo = _softmax_pv(s_lo, vw)
            o_hi = _softmax_pv(s_hi, vw)
            o_ref[pl.ds(qr, nq), sl] = jnp.where(lo_q, o_lo, o_hi).astype(o_ref.dtype)
        return carry

    lax.fori_loop(0, n_groups, body, 0)


def _attn_b(h, bias, batch, seq):
    rows = seq // GRID_W
    kern = functools.partial(_attn_b_kernel, rows=rows)
    return pl.pallas_call(
        kern,
        out_shape=jax.ShapeDtypeStruct((batch * seq, MIX_W), jnp.bfloat16),
        grid=(batch,),
        in_specs=[pl.BlockSpec((seq, 256), lambda b: (b, COL_BQ // 256)),
                  pl.BlockSpec((seq, 256), lambda b: (b, COL_BK // 256)),
                  pl.BlockSpec((seq, 256), lambda b: (b, COL_BV // 256)),
                  pl.BlockSpec(bias.shape, lambda b: (0, 0, 0, 0))],
        out_specs=pl.BlockSpec((seq, MIX_W), lambda b: (b, 0)),
        compiler_params=_cparams(("parallel",)),
    )(h, h, h, bias)


def _attn_c_kernel(hc_ref, wuq_ref, wuk_ref, wuv_ref, gq_ref, gkv_ref, cos_ref, sin_ref, o_ref,
                   qs_ref, ks_ref, vs_ref, *, tq):
    s_len = hc_ref.shape[0]
    lane = lax.broadcasted_iota(jnp.int32, (s_len, LANES), 1)
    cos = cos_ref[...]
    sin = sin_ref[...]

    cq = hc_ref[:, 0:256].astype(jnp.float32)
    ms = jnp.sum(cq * cq, axis=-1, keepdims=True) * (1.0 / C_Q_RANK)
    cqn = (cq * lax.rsqrt(ms + RMS_EPS) * gq_ref[...]).astype(jnp.bfloat16)
    ckv = hc_ref[:, 256:384].astype(jnp.float32)
    ms = jnp.mean(ckv * ckv, axis=-1, keepdims=True)
    kvn = (ckv * lax.rsqrt(ms + RMS_EPS) * gkv_ref[...]).astype(jnp.bfloat16)
    kpe = _rotate_pairs(hc_ref[:, 384:512].astype(jnp.float32), cos, sin, C_ROPE // 2, lane)
    vs_ref[...] = _dot(kvn, wuv_ref[...]).astype(jnp.bfloat16)
    scale = (C_NOPE + C_ROPE) ** -0.5
    for h in range(C_HEADS):
        sl = slice(h * LANES, (h + 1) * LANES)
        qh = _rotate_pairs(_dot(cqn, wuq_ref[:, sl]), cos, sin, C_ROPE // 2, lane) * scale
        qs_ref[:, sl] = qh.astype(jnp.bfloat16)
        ks_ref[:, sl] = (_dot(kvn, wuk_ref[:, sl]) + kpe).astype(jnp.bfloat16)

    lane_q = lax.broadcasted_iota(jnp.int32, (tq, LANES), 1)
    lo_q = lane_q < C_V

    def body(i, carry):
        r = pl.multiple_of(i * tq, tq)
        for blk in range(C_HEADS // 2):
            vb = vs_ref[:, blk * LANES:(blk + 1) * LANES]
            outs = []
            for par in range(2):
                sl = slice((2 * blk + par) * LANES, (2 * blk + par + 1) * LANES)
                outs.append(_softmax_pv(_dot_nt(qs_ref[pl.ds(r, tq), sl], ks_ref[:, sl]), vb))
            o_ref[pl.ds(r, tq), blk * LANES:(blk + 1) * LANES] = jnp.where(lo_q, outs[0], outs[1]).astype(o_ref.dtype)
        return carry

    lax.fori_loop(0, s_len // tq, body, 0)


def _attn_c(h, wuq, wuk, wuv, gq, gkv, cos_c, sin_c, batch, seq, tq=256):
    kern = functools.partial(_attn_c_kernel, tq=tq)
    const = lambda shape: pl.BlockSpec(shape, lambda b: (0,) * len(shape))
    return pl.pallas_call(
        kern,
        out_shape=jax.ShapeDtypeStruct((batch * seq, MIX_W), jnp.bfloat16),
        grid=(batch,),
        in_specs=[pl.BlockSpec((seq, 512), lambda b: (b, COL_C // 512)),
                  const(wuq.shape), const(wuk.shape), const(wuv.shape),
                  const(gq.shape), const(gkv.shape), const(cos_c.shape), const(sin_c.shape)],
        out_specs=pl.BlockSpec((seq, MIX_W), lambda b: (b, 0)),
        scratch_shapes=[pltpu.VMEM((seq, C_HEADS * LANES), jnp.bfloat16),
                        pltpu.VMEM((seq, C_HEADS * LANES), jnp.bfloat16),
                        pltpu.VMEM((seq, C_HEADS * C_V), jnp.bfloat16)],
        compiler_params=_cparams(("parallel",)),
    )(h, wuq, wuk, wuv, gq, gkv, cos_c, sin_c)


CONV_PAD = 16
CONV_ROWS = 128


def _conv_kernel(hd_ref, w_ref, cb_ref, g_ref, b_ref, o_ref, up_ref):
    s_len = hd_ref.shape[0]
    a = hd_ref[:, 0:D_CH].astype(jnp.float32)
    gate = hd_ref[:, D_CH:2 * D_CH].astype(jnp.float32)
    zeros = jnp.zeros((CONV_PAD, D_CH), jnp.float32)
    up_ref[0:CONV_PAD, :] = zeros
    up_ref[CONV_PAD + s_len:2 * CONV_PAD + s_len, :] = zeros
    up_ref[CONV_PAD:CONV_PAD + s_len, :] = a * _sigmoid(gate)
    shift = CONV_PAD - CONV_W // 2
    for c in range(s_len // CONV_ROWS):
        acc = jnp.zeros((CONV_ROWS, D_CH), jnp.float32)
        for j in range(CONV_W):
            start = c * CONV_ROWS + j + shift
            acc = acc + up_ref[start:start + CONV_ROWS, :] * w_ref[j:j + 1, :]
        u = acc + cb_ref[...]
        y = _layer_norm_rows(u, g_ref[...], b_ref[...])
        o_ref[c * CONV_ROWS:(c + 1) * CONV_ROWS, :] = (y * _sigmoid(y)).astype(o_ref.dtype)


def _conv_branch(h, w, cb, g, b, batch, seq):
    const = lambda shape: pl.BlockSpec(shape, lambda i: (0,) * len(shape))
    return pl.pallas_call(
        _conv_kernel,
        out_shape=jax.ShapeDtypeStruct((batch * seq, MIX_W), jnp.bfloat16),
        grid=(batch,),
        in_specs=[pl.BlockSpec((seq, 512), lambda i: (i, COL_D // 512)),
                  const(w.shape), const(cb.shape), const(g.shape), const(b.shape)],
        out_specs=pl.BlockSpec((seq, MIX_W), lambda i: (i, 0)),
        scratch_shapes=[pltpu.VMEM((seq + 2 * CONV_PAD, D_CH), jnp.float32)],
        compiler_params=_cparams(("parallel",)),
    )(h, w, cb, g, b)


def _merge_kernel(gl_ref, ya_ref, yb_ref, yc_ref, yd_ref, x_ref, wb_ref, wo_ref, g_ref, b_ref,
                  of_ref, ob_ref):
    merged = None
    for n, y_ref in enumerate((ya_ref, yb_ref, yc_ref, yd_ref)):
        gate = _sigmoid(gl_ref[:, n * D_MODEL:(n + 1) * D_MODEL].astype(jnp.float32))
        term = gate * _dot(y_ref[...], wb_ref[n])
        merged = term if merged is None else merged + term
    mix = _dot(merged.astype(jnp.bfloat16), wo_ref[...])
    y = _layer_norm_rows(ALPHA * x_ref[...] + mix, g_ref[...], b_ref[...])
    of_ref[...] = y
    ob_ref[...] = y.astype(jnp.bfloat16)


def _merge(h, ya, yb, yc, yd, x, wb, wo, g, b, tm=512):
    t, d = x.shape
    row = lambda w: pl.BlockSpec((tm, w), lambda i: (i, 0))
    const = lambda shape: pl.BlockSpec(shape, lambda i: (0,) * len(shape))
    return pl.pallas_call(
        _merge_kernel,
        out_shape=(jax.ShapeDtypeStruct((t, d), jnp.float32), jax.ShapeDtypeStruct((t, d), jnp.bfloat16)),
        grid=(t // tm,),
        in_specs=[pl.BlockSpec((tm, N_BRANCH * d), lambda i: (i, COL_GATES)),
                  row(MIX_W), row(MIX_W), row(MIX_W), row(MIX_W), row(d),
                  const(wb.shape), const(wo.shape), const((1, d)), const((1, d))],
        out_specs=(row(d), row(d)),
        compiler_params=_cparams(("parallel",)),
    )(h, ya, yb, yc, yd, x, wb, wo, g.reshape(1, d), b.reshape(1, d))


def _ffn_kernel(xb_ref, xf_ref, w1_ref, w3_ref, w2_ref, g_ref, b_ref, of_ref, ob_ref, acc_ref):
    j = pl.program_id(1)
    x = xb_ref[...]
    a = _dot(x, w1_ref[...])
    hid = (a * _sigmoid(a) * _dot(x, w3_ref[...])).astype(jnp.bfloat16)
    part = _dot(hid, w2_ref[...])

    @pl.when(j == 0)
    def _():
        acc_ref[...] = part

    @pl.when(j > 0)
    def _():
        acc_ref[...] += part

    @pl.when(j == pl.num_programs(1) - 1)
    def _():
        y = _layer_norm_rows(ALPHA * xf_ref[...] + acc_ref[...], g_ref[...], b_ref[...])
        of_ref[...] = y
        ob_ref[...] = y.astype(jnp.bfloat16)


def _ffn(xb, xf, w1, w3, w2, g, b, tm=512, tf=1408):
    t, d = xf.shape
    f = w1.shape[1]
    row = pl.BlockSpec((tm, d), lambda i, j: (i, 0))
    vec = pl.BlockSpec((1, d), lambda i, j: (0, 0))
    return pl.pallas_call(
        _ffn_kernel,
        out_shape=(jax.ShapeDtypeStruct((t, d), jnp.float32), jax.ShapeDtypeStruct((t, d), jnp.bfloat16)),
        grid=(t // tm, f // tf),
        in_specs=[row, row,
                  pl.BlockSpec((d, tf), lambda i, j: (0, j)),
                  pl.BlockSpec((d, tf), lambda i, j: (0, j)),
                  pl.BlockSpec((tf, d), lambda i, j: (j, 0)),
                  vec, vec],
        out_specs=(row, row),
        scratch_shapes=[pltpu.VMEM((tm, d), jnp.float32)],
        compiler_params=_cparams(("parallel", "arbitrary")),
    )(xb, xf, w1, w3, w2, g.reshape(1, d), b.reshape(1, d))


def _router_kernel(x_ref, r_ref, tri_ref, oi_ref, ow_ref, cnt_ref, carry_ref):
    i = pl.program_id(0)

    @pl.when(i == 0)
    def _():
        carry_ref[...] = jnp.zeros_like(carry_ref)

    x = x_ref[...]
    r = r_ref[...]
    xh = x.astype(jnp.bfloat16)
    xl = (x - xh.astype(jnp.float32)).astype(jnp.bfloat16)
    rh = r.astype(jnp.bfloat16)
    rl = (r - rh.astype(jnp.float32)).astype(jnp.bfloat16)
    logits = (_dot(xh, rh) + _dot(xh, rl)) + (_dot(xl, rh) + _dot(xl, rl))
    tm = x.shape[0]
    lane = lax.broadcasted_iota(jnp.int32, (tm, LANES), 1)
    logits = jnp.where(lane < N_EXPERTS, logits, NEG_BIG)
    m1 = jnp.max(logits, axis=-1, keepdims=True)
    i1 = jnp.min(jnp.where(logits == m1, lane, LANES), axis=-1, keepdims=True)
    rest = jnp.where(lane == i1, NEG_BIG, logits)
    m2 = jnp.max(rest, axis=-1, keepdims=True)
    i2 = jnp.min(jnp.where(rest == m2, lane, LANES), axis=-1, keepdims=True)
    e = jnp.exp(m2 - m1)
    w1 = 1.0 / (1.0 + e)
    w2 = e / (1.0 + e)
    sel1 = lane == i1
    sel2 = lane == i2
    member = jnp.where(sel1 | sel2, 1.0, 0.0)
    before = _dot(tri_ref[...], member.astype(jnp.bfloat16)) + carry_ref[...]
    rank1 = jnp.sum(jnp.where(sel1, before, 0.0), axis=-1, keepdims=True).astype(jnp.int32)
    rank2 = jnp.sum(jnp.where(sel2, before, 0.0), axis=-1, keepdims=True).astype(jnp.int32)
    carry_ref[...] += jnp.sum(member, axis=0, keepdims=True)
    oi_ref[...] = jnp.where(lane == 0, i1, jnp.where(lane == 1, i2, jnp.where(lane == 2, rank1, rank2)))
    ow_ref[...] = jnp.where(lane == 0, w1, w2)
    cnt_ref[...] = carry_ref[...]


def _router(xf, router, tm=512):
    t, d = xf.shape
    r_pad = jnp.zeros((d, LANES), jnp.float32).at[:, :N_EXPERTS].set(router.astype(jnp.float32))
    tri = jnp.asarray(np.tril(np.ones((tm, tm), np.float32), -1), jnp.bfloat16)
    return pl.pallas_call(
        _router_kernel,
        out_shape=(jax.ShapeDtypeStruct((t, LANES), jnp.int32),
                   jax.ShapeDtypeStruct((t, LANES), jnp.float32),
                   jax.ShapeDtypeStruct((1, LANES), jnp.float32)),
        grid=(t // tm,),
        in_specs=[pl.BlockSpec((tm, d), lambda i: (i, 0)),
                  pl.BlockSpec((d, LANES), lambda i: (0, 0)),
                  pl.BlockSpec((tm, tm), lambda i: (0, 0))],
        out_specs=(pl.BlockSpec((tm, LANES), lambda i: (i, 0)),
                   pl.BlockSpec((tm, LANES), lambda i: (i, 0)),
                   pl.BlockSpec((1, LANES), lambda i: (0, 0))),
        scratch_shapes=[pltpu.VMEM((1, LANES), jnp.float32)],
        compiler_params=_cparams(("arbitrary",)),
    )(xf, r_pad, tri)


def _dispatch_kernel(src_ref, x_hbm, o_ref, buf_ref, sem):
    i = pl.program_id(0)
    tm = buf_ref.shape[0]

    def row_copy(k):
        tok = src_ref[i * tm + k]
        return pltpu.make_async_copy(x_hbm.at[pl.ds(tok, 1)], buf_ref.at[pl.ds(k, 1)], sem)

    def start(k, c):
        row_copy(k).start()
        return c

    def wait(k, c):
        row_copy(k).wait()
        return c

    lax.fori_loop(0, tm, start, 0)
    lax.fori_loop(0, tm, wait, 0)
    o_ref[...] = buf_ref[...].astype(o_ref.dtype)


def _dispatch(src, xf, n_rows, tm=256):
    t, d = xf.shape
    return pl.pallas_call(
        _dispatch_kernel,
        out_shape=jax.ShapeDtypeStruct((n_rows, d), jnp.bfloat16),
        grid_spec=pltpu.PrefetchScalarGridSpec(
            num_scalar_prefetch=1,
            grid=(n_rows // tm,),
            in_specs=[pl.BlockSpec(memory_space=pl.ANY)],
            out_specs=pl.BlockSpec((tm, d), lambda i, src: (i, 0)),
            scratch_shapes=[pltpu.VMEM((tm, d), jnp.float32), pltpu.SemaphoreType.DMA(())],
        ),
        compiler_params=_cparams(("arbitrary",)),
    )(src, xf)


def _expert_kernel(te_ref, nu_ref, x_ref, w1_ref, w3_ref, w2_ref, o_ref):
    i = pl.program_id(0)
    j = pl.program_id(1)
    used = i < nu_ref[0]

    @pl.when(used)
    def _():
        x = x_ref[...]
        a = _dot(x, w1_ref[0])
        hid = (a * _sigmoid(a) * _dot(x, w3_ref[0])).astype(jnp.bfloat16)
        part = _dot(hid, w2_ref[0])

        @pl.when(j == 0)
        def _():
            o_ref[...] = part

        @pl.when(j > 0)
        def _():
            o_ref[...] += part

    @pl.when(jnp.logical_not(used) & (j == 0))
    def _():
        o_ref[...] = jnp.zeros_like(o_ref)


def _experts(tile_expert, n_used, xs, w1, w3, w2):
    n_rows, d = xs.shape
    f = w1.shape[2]
    nj = f // MOE_FCHUNK

    def chunk(i, j, nu):
        return jnp.where(i < nu[0], j, nj - 1)

    return pl.pallas_call(
        _expert_kernel,
        out_shape=jax.ShapeDtypeStruct((n_rows, d), jnp.float32),
        grid_spec=pltpu.PrefetchScalarGridSpec(
            num_scalar_prefetch=2,
            grid=(n_rows // MOE_TILE, nj),
            in_specs=[pl.BlockSpec((MOE_TILE, d), lambda i, j, te, nu: (i, 0)),
                      pl.BlockSpec((1, d, MOE_FCHUNK), lambda i, j, te, nu: (te[i], 0, chunk(i, j, nu))),
                      pl.BlockSpec((1, d, MOE_FCHUNK), lambda i, j, te, nu: (te[i], 0, chunk(i, j, nu))),
                      pl.BlockSpec((1, MOE_FCHUNK, d), lambda i, j, te, nu: (te[i], chunk(i, j, nu), 0))],
            out_specs=pl.BlockSpec((MOE_TILE, d), lambda i, j, te, nu: (i, 0)),
        ),
        compiler_params=_cparams(("arbitrary", "arbitrary")),
    )(tile_expert, n_used, xs, w1, w3, w2)


def _combine_kernel(d1_ref, d2_ref, y_hbm, x_ref, w_ref, g_ref, b_ref, o_ref, ya_ref, yb_ref, sem):
    i = pl.program_id(0)
    tm = ya_ref.shape[0]

    def copies(k):
        r1 = d1_ref[i * tm + k]
        r2 = d2_ref[i * tm + k]
        return (pltpu.make_async_copy(y_hbm.at[pl.ds(r1, 1)], ya_ref.at[pl.ds(k, 1)], sem.at[0]),
                pltpu.make_async_copy(y_hbm.at[pl.ds(r2, 1)], yb_ref.at[pl.ds(k, 1)], sem.at[1]))

    def start(k, c):
        ca, cb = copies(k)
        ca.start()
        cb.start()
        return c

    def wait(k, c):
        ca, cb = copies(k)
        ca.wait()
        cb.wait()
        return c

    lax.fori_loop(0, tm, start, 0)
    lax.fori_loop(0, tm, wait, 0)
    w = w_ref[...]
    f = w[:, 0:1] * ya_ref[...] + w[:, 1:2] * yb_ref[...]
    o_ref[...] = _layer_norm_rows(ALPHA * x_ref[...] + f, g_ref[...], b_ref[...])


def _combine(dest1, dest2, y, xf, wts, g, b, tm=256):
    t, d = xf.shape
    return pl.pallas_call(
        _combine_kernel,
        out_shape=jax.ShapeDtypeStruct((t, d), jnp.float32),
        grid_spec=pltpu.PrefetchScalarGridSpec(
            num_scalar_prefetch=2,
            grid=(t // tm,),
            in_specs=[pl.BlockSpec(memory_space=pl.ANY),
                      pl.BlockSpec((tm, d), lambda i, a, c: (i, 0)),
                      pl.BlockSpec((tm, LANES), lambda i, a, c: (i, 0)),
                      pl.BlockSpec((1, d), lambda i, a, c: (0, 0)),
                      pl.BlockSpec((1, d), lambda i, a, c: (0, 0))],
            out_specs=pl.BlockSpec((tm, d), lambda i, a, c: (i, 0)),
            scratch_shapes=[pltpu.VMEM((tm, d), jnp.float32), pltpu.VMEM((tm, d), jnp.float32),
                            pltpu.SemaphoreType.DMA((2,))],
        ),
        compiler_params=_cparams(("arbitrary",)),
    )(dest1, dest2, y, xf, wts, g.reshape(1, d), b.reshape(1, d))


def _moe_ffn(xf, router, w1, w3, w2, g, b):
    t, d = xf.shape
    oi, ow, cnt = _router(xf, router)
    e1, e2, rank1, rank2 = oi[:, 0], oi[:, 1], oi[:, 2], oi[:, 3]
    counts = cnt[0, :N_EXPERTS].astype(jnp.int32)
    tiles = (counts + MOE_TILE - 1) // MOE_TILE
    tile_end = jnp.cumsum(tiles)
    offs = (tile_end - tiles) * MOE_TILE
    dest1 = offs[e1] + rank1
    dest2 = offs[e2] + rank2
    n_rows = TOP_K * t + N_EXPERTS * MOE_TILE
    n_tiles = n_rows // MOE_TILE
    tok = jnp.arange(t, dtype=jnp.int32)
    src = jnp.zeros((n_rows,), jnp.int32).at[dest1].set(tok).at[dest2].set(tok)
    n_used = tile_end[-1:].astype(jnp.int32)
    tile_expert = jnp.searchsorted(tile_end, jnp.arange(n_tiles, dtype=jnp.int32), side="right")
    last_expert = tile_expert[jnp.maximum(n_used[0] - 1, 0)]
    tile_expert = jnp.where(jnp.arange(n_tiles) < n_used[0], tile_expert, last_expert).astype(jnp.int32)
    xs = _dispatch(src, xf, n_rows)
    y = _experts(tile_expert, n_used, xs, w1, w3, w2)
    return _combine(dest1, dest2, y, xf, ow, g, b)


def _prep_layer(l, p, in_cols):
    bf = jnp.bfloat16
    out = {}
    out["w_in"] = _take_cols(p["w_in"][l], in_cols).astype(bf)
    deint = np.concatenate([np.arange(0, HEAD_DIM, 2), np.arange(1, HEAD_DIM, 2)])
    out["gq_a"] = jnp.tile(p["a_q_norm"][l][deint], 2).reshape(1, LANES)
    out["gk_a"] = jnp.tile(p["a_k_norm"][l][deint], 2).reshape(1, LANES)
    per_head = C_NOPE + C_ROPE
    uq_cols = []
    for h in range(C_HEADS):
        base = h * per_head
        uq_cols += [base + np.arange(C_NOPE), base + C_NOPE + np.arange(0, C_ROPE, 2),
                    base + C_NOPE + np.arange(1, C_ROPE, 2), np.full(32, -1)]
    wuq = _take_cols(p["c_w_uq"][l], np.concatenate(uq_cols))
    out["wuq"] = jnp.concatenate([wuq, jnp.zeros((64, wuq.shape[1]), wuq.dtype)], axis=0).astype(bf)
    uk_cols, uv_cols = [], []
    for h in range(C_HEADS):
        base = h * (C_NOPE + C_V)
        uk_cols += [base + np.arange(C_NOPE), np.full(64, -1)]
        uv_cols += [base + C_NOPE + np.arange(C_V)]
    out["wuk"] = _take_cols(p["c_w_ukv"][l], np.concatenate(uk_cols)).astype(bf)
    out["wuv"] = _take_cols(p["c_w_ukv"][l], np.concatenate(uv_cols)).astype(bf)
    out["gq_c"] = jnp.concatenate([p["c_q_norm"][l], jnp.zeros((64,), jnp.float32)]).reshape(1, 256)
    out["gkv_c"] = p["c_kv_norm"][l].reshape(1, C_KV_RANK)
    out["conv_w"] = jnp.concatenate([p["d_conv_w"][l][:, 0, :], jnp.zeros((1, D_CH), jnp.float32)], axis=0)
    out["conv_b"] = p["d_conv_b"][l].reshape(1, D_CH)
    out["d_ln_g"] = p["d_ln_g"][l].reshape(1, D_CH)
    out["d_ln_b"] = p["d_ln_b"][l].reshape(1, D_CH)
    a_rows = np.concatenate([(kvh * 2 + g) * HEAD_DIM + np.arange(HEAD_DIM)
                             for g in range(2) for kvh in range(A_KV_HEADS)])
    wb = p["w_branch"][l]
    out["w_branch"] = jnp.stack([wb[0][a_rows], wb[1], wb[2], wb[3]], axis=0).astype(bf)
    out["w_out"] = p["w_out"][l].astype(bf)
    return out


def kernel(x, ln_in_g, ln_in_b, w_in, a_q_norm, a_k_norm, b_rpb, c_q_norm, c_kv_norm, c_w_uq, c_w_ukv,
           d_conv_w, d_conv_b, d_ln_g, d_ln_b, w_branch, w_out, ln_mix_g, ln_mix_b,
           ffn_w1, ffn_w3, ffn_w2, moe_router, moe_w1, moe_w3, moe_w2, ln_ffn_g, ln_ffn_b):
    batch, seq, d = x.shape
    t = batch * seq
    bf = jnp.bfloat16
    params = dict(w_in=w_in, a_q_norm=a_q_norm, a_k_norm=a_k_norm, c_q_norm=c_q_norm, c_kv_norm=c_kv_norm,
                  c_w_uq=c_w_uq, c_w_ukv=c_w_ukv, d_conv_w=d_conv_w, d_conv_b=d_conv_b, d_ln_g=d_ln_g,
                  d_ln_b=d_ln_b, w_branch=w_branch, w_out=w_out)
    in_cols = _in_proj_columns()
    cos_a, sin_a, cos_c, sin_c = _rope_tables(seq)
    xf, xb = _input_ln(x.reshape(t, d), ln_in_g, ln_in_b)
    for l in range(DEPTH):
        p = _prep_layer(l, params, in_cols)
        h = _matmul(xb, p["w_in"], bf, tm=1024, tn=1280)
        ya = _attn_a(h, cos_a, sin_a, p["gq_a"], p["gk_a"], batch, seq)
        yb = _attn_b(h, _na_bias_tables(b_rpb[l], seq // GRID_W), batch, seq)
        yc = _attn_c(h, p["wuq"], p["wuk"], p["wuv"], p["gq_c"], p["gkv_c"], cos_c, sin_c, batch, seq)
        yd = _conv_branch(h, p["conv_w"], p["conv_b"], p["d_ln_g"], p["d_ln_b"], batch, seq)
        xf, xb = _merge(h, ya, yb, yc, yd, xf, p["w_branch"], p["w_out"], ln_mix_g[l], ln_mix_b[l])
        if l % 2 == 0:
            i = l // 2
            xf, xb = _ffn(xb, xf, ffn_w1[i].astype(bf), ffn_w3[i].astype(bf), ffn_w2[i].astype(bf),
                          ln_ffn_g[l], ln_ffn_b[l])
        else:
            i = l // 2
            xf = _moe_ffn(xf, moe_router[i], moe_w1[i].astype(bf), moe_w3[i].astype(bf), moe_w2[i].astype(bf),
                          ln_ffn_g[l], ln_ffn_b[l])
            xb = xf.astype(bf)
    return xf.reshape(batch, seq, d)
```

```python
import functools

import numpy as np
import jax
import jax.numpy as jnp
from jax import lax
from jax.experimental import pallas as pl
from jax.experimental.pallas import tpu as pltpu

D_MODEL = 1024
DEPTH = 2
GRID_W = 64
HEAD_DIM = 64
ROPE_THETA = 10000.0
RMS_EPS = 1e-6
LN_EPS = 1e-5
A_HEADS = 4
A_KV_HEADS = 2
B_HEADS = 4
NA_WIN_H = 8
NA_WIN_W = 16
C_HEADS = 4
C_NOPE = 64
C_ROPE = 32
C_V = 64
C_Q_RANK = 192
C_KV_RANK = 128
D_CH = 256
CONV_W = 31
N_BRANCH = 4
MIX_W = 256
D_FF = 2816
N_EXPERTS = 8
TOP_K = 2
D_FF_EXPERT = 3584
ALPHA = (2 * DEPTH) ** 0.25

LANES = 128
VMEM_LIMIT = 48 * 1024 * 1024

NEG_BIG = -1e30

COL_GATES = 0
COL_AQ = 4096
COL_AKV = 4352
COL_C = 4608
COL_D = 5120
COL_BQ = 5632
COL_BK = 5888
COL_BV = 6144
D_IN_PAD = 6400

NA_QROWS = 4
NA_KROWS = 12

MOE_TILE = 512
MOE_FCHUNK = 896


def _in_proj_columns():
    o_aq, o_ak, o_av = 0, 256, 384
    o_bq, o_bk, o_bv = 512, 768, 1024
    o_cq, o_ckv, o_kpe = 1280, 1472, 1600
    o_d, o_g = 1632, 2144
    deint = np.concatenate([np.arange(0, HEAD_DIM, 2), np.arange(1, HEAD_DIM, 2)])
    cols = [o_g + np.arange(N_BRANCH * D_MODEL)]
    for g in range(2):
        for kvh in range(A_KV_HEADS):
            cols.append(o_aq + (kvh * 2 + g) * HEAD_DIM + deint)
    for kvh in range(A_KV_HEADS):
        cols.append(o_ak + kvh * HEAD_DIM + deint)
    cols.append(o_av + np.arange(A_KV_HEADS * HEAD_DIM))
    cols.append(o_cq + np.arange(C_Q_RANK))
    cols.append(np.full(64, -1))
    cols.append(o_ckv + np.arange(C_KV_RANK))
    cols.append(np.full(64, -1))
    cols.append(o_kpe + np.arange(0, C_ROPE, 2))
    cols.append(o_kpe + np.arange(1, C_ROPE, 2))
    cols.append(np.full(32, -1))
    cols.append(o_d + np.arange(2 * D_CH))
    cols.append(o_bq + np.arange(3 * B_HEADS * HEAD_DIM))
    cols = np.concatenate(cols)
    assert cols.shape[0] == D_IN_PAD
    return cols


def _take_cols(w, cols):
    cols = np.asarray(cols)
    parts, i, n = [], 0, len(cols)
    while i < n:
        j = i + 1
        if cols[i] < 0:
            while j < n and cols[j] < 0:
                j += 1
            parts.append(jnp.zeros((w.shape[0], j - i), w.dtype))
        else:
            step = int(cols[j] - cols[i]) if j < n and cols[j] > cols[i] else 1
            while j < n and cols[j] == cols[j - 1] + step:
                j += 1
            parts.append(lax.slice(w, (0, int(cols[i])), (w.shape[0], int(cols[j - 1]) + 1), (1, step)))
        i = j
    return jnp.concatenate(parts, axis=1)


def _rope_tables(seq):
    t = np.arange(seq)
    row = (t // GRID_W).astype(np.float32)
    col = (t % GRID_W).astype(np.float32)

    def angles(rot_dim):
        n_freq = rot_dim // 4
        inv = jnp.asarray(ROPE_THETA, jnp.float32) ** (-jnp.arange(n_freq, dtype=jnp.float32) / n_freq)
        return jnp.concatenate([jnp.asarray(row)[:, None] * inv, jnp.asarray(col)[:, None] * inv], axis=-1)

    ang_a = angles(HEAD_DIM)
    ca, sa = jnp.cos(ang_a), jnp.sin(ang_a)
    cos_a = jnp.concatenate([ca, ca, ca, ca], axis=-1)
    sin_a = jnp.concatenate([-sa, sa, -sa, sa], axis=-1)
    ang_c = angles(C_ROPE)
    cc, sc = jnp.cos(ang_c), jnp.sin(ang_c)
    one = jnp.ones((seq, 64), jnp.float32)
    zero = jnp.zeros((seq, 64), jnp.float32)
    cos_c = jnp.concatenate([one, cc, cc, one[:, :32]], axis=-1)
    sin_c = jnp.concatenate([zero, -sc, sc, zero[:, :32]], axis=-1)
    return cos_a, sin_a, cos_c, sin_c


def _na_bias_tables(rpb, rows):
    wh = min(NA_WIN_H, rows)
    n_dr, n_dc = 2 * NA_WIN_H - 1, 2 * NA_WIN_W - 1
    qc = np.arange(GRID_W)[:, None]
    kc = np.arange(GRID_W)[None, :]
    col_start = np.clip(qc - NA_WIN_W // 2, 0, GRID_W - NA_WIN_W)
    ok_c = (kc >= col_start) & (kc < col_start + NA_WIN_W)
    dc = np.clip(kc - qc + (NA_WIN_W - 1), 0, n_dc - 1)
    oh_c = np.eye(n_dc, dtype=np.float32)[dc.reshape(-1)]
    oh_r, ok_r = [], []
    for r0 in (0, NA_QROWS, rows - NA_QROWS):
        ws = int(np.clip(r0 - 4, 0, rows - NA_KROWS))
        qr = r0 + np.arange(NA_QROWS)[:, None]
        kr = ws + np.arange(NA_KROWS)[None, :]
        row_start = np.clip(qr - wh // 2, 0, rows - wh)
        ok_r.append((kr >= row_start) & (kr < row_start + wh))
        dr = np.clip(kr - qr + (NA_WIN_H - 1), 0, n_dr - 1)
        oh_r.append(np.eye(n_dr, dtype=np.float32)[dr.reshape(-1)])
    oh_r = np.concatenate(oh_r, axis=0)
    ok = np.stack(ok_r)[:, :, None, :, None] & ok_c[None, None, :, None, :]
    ok = ok.reshape(3, 1, NA_QROWS * GRID_W, NA_KROWS * GRID_W)
    hi = lax.Precision.HIGHEST
    t1 = jnp.einsum("ma,hab->hmb", jnp.asarray(oh_r), rpb.astype(jnp.float32), precision=hi)
    t2 = jnp.einsum("hmb,nb->hmn", t1, jnp.asarray(oh_c), precision=hi)
    nh = rpb.shape[0]
    t2 = t2.reshape(nh, 3, NA_QROWS, NA_KROWS, GRID_W, GRID_W).transpose(1, 0, 2, 4, 3, 5)
    bias = t2.reshape(3, nh, NA_QROWS * GRID_W, NA_KROWS * GRID_W)
    return jnp.where(jnp.asarray(ok), bias, NEG_BIG)


def _cparams(sem):
    return pltpu.CompilerParams(dimension_semantics=sem, vmem_limit_bytes=VMEM_LIMIT)


def _layer_norm_rows(z, g, b):
    mu = jnp.mean(z, axis=-1, keepdims=True)
    zc = z - mu
    var = jnp.mean(zc * zc, axis=-1, keepdims=True)
    return zc * lax.rsqrt(var + LN_EPS) * g + b


def _sigmoid(x):
    return 0.5 * jnp.tanh(0.5 * x) + 0.5


def _dot(a, b):
    return jnp.dot(a, b, preferred_element_type=jnp.float32)


def _dot_nt(a, b):
    return lax.dot_general(a, b, (((1,), (1,)), ((), ())), preferred_element_type=jnp.float32)


def _ln_kernel(x_ref, g_ref, b_ref, of_ref, ob_ref):
    y = _layer_norm_rows(x_ref[...], g_ref[...], b_ref[...])
    of_ref[...] = y
    ob_ref[...] = y.astype(jnp.bfloat16)


def _input_ln(x, g, b, tm=1024):
    t, d = x.shape
    return pl.pallas_call(
        _ln_kernel,
        out_shape=(jax.ShapeDtypeStruct((t, d), jnp.float32), jax.ShapeDtypeStruct((t, d), jnp.bfloat16)),
        grid=(t // tm,),
        in_specs=[pl.BlockSpec((tm, d), lambda i: (i, 0)),
                  pl.BlockSpec((1, d), lambda i: (0, 0)),
                  pl.BlockSpec((1, d), lambda i: (0, 0))],
        out_specs=(pl.BlockSpec((tm, d), lambda i: (i, 0)), pl.BlockSpec((tm, d), lambda i: (i, 0))),
        compiler_params=_cparams(("parallel",)),
    )(x, g.reshape(1, d), b.reshape(1, d))


def _mm_kernel(a_ref, b_ref, o_ref):
    o_ref[...] = _dot(a_ref[...], b_ref[...]).astype(o_ref.dtype)


def _matmul(a, b, out_dtype, tm, tn):
    m, k = a.shape
    n = b.shape[1]
    return pl.pallas_call(
        _mm_kernel,
        out_shape=jax.ShapeDtypeStruct((m, n), out_dtype),
        grid=(m // tm, n // tn),
        in_specs=[pl.BlockSpec((tm, k), lambda i, j: (i, 0)),
                  pl.BlockSpec((k, tn), lambda i, j: (0, j))],
        out_specs=pl.BlockSpec((tm, tn), lambda i, j: (i, j)),
        compiler_params=_cparams(("parallel", "parallel")),
    )(a, b)


def _softmax_pv(s, v):
    m = jnp.max(s, axis=-1, keepdims=True)
    p = jnp.exp(s - m)
    l = jnp.sum(p, axis=-1, keepdims=True)
    o = _dot(p.astype(jnp.bfloat16), v)
    return o * (1.0 / l)


def _rotate_pairs(x, cos, sin, half, lane):
    w = x.shape[-1]
    first = (lane % (2 * half)) < half
    partner = jnp.where(first, pltpu.roll(x, w - half, 1), pltpu.roll(x, half, 1))
    return x * cos + partner * sin


def _attn_a_kernel(q_ref, kv_ref, cos_ref, sin_ref, gq_ref, gk_ref, o_ref, qs_ref, ks_ref, *, tq):
    s_len = q_ref.shape[0]
    lane = lax.broadcasted_iota(jnp.int32, (s_len, LANES), 1)
    lo = lane < HEAD_DIM
    cos = cos_ref[...]
    sin = sin_ref[...]

    def prep(x, g):
        x2 = x * x
        ss_lo = jnp.sum(jnp.where(lo, x2, 0.0), axis=-1, keepdims=True)
        ss_hi = jnp.sum(jnp.where(lo, 0.0, x2), axis=-1, keepdims=True)
        ms = jnp.where(lo, ss_lo, ss_hi) * (1.0 / HEAD_DIM)
        xn = x * lax.rsqrt(ms + RMS_EPS) * g
        return _rotate_pairs(xn, cos, sin, HEAD_DIM // 2, lane)

    ks_ref[...] = prep(kv_ref[:, 0:LANES].astype(jnp.float32), gk_ref[...]).astype(jnp.bfloat16)
    scale = HEAD_DIM ** -0.5
    for g in range(2):
        qg = prep(q_ref[:, g * LANES:(g + 1) * LANES].astype(jnp.float32), gq_ref[...]) * scale
        qs_ref[:, g * LANES:(g + 1) * LANES] = qg.astype(jnp.bfloat16)

    lane_q = lax.broadcasted_iota(jnp.int32, (tq, LANES), 1)
    lo_q = lane_q < HEAD_DIM

    def body(i, carry):
        r = pl.multiple_of(i * tq, tq)
        k = ks_ref[...]
        v = kv_ref[:, LANES:2 * LANES]
        for g in range(2):
            qt = qs_ref[pl.ds(r, tq), g * LANES:(g + 1) * LANES]
            zero = jnp.zeros_like(qt)
            o_lo = _softmax_pv(_dot_nt(jnp.where(lo_q, qt, zero), k), v)
            o_hi = _softmax_pv(_dot_nt(jnp.where(lo_q, zero, qt), k), v)
            o_ref[pl.ds(r, tq), g * LANES:(g + 1) * LANES] = jnp.where(lo_q, o_lo, o_hi).astype(o_ref.dtype)
        return carry

    lax.fori_loop(0, s_len // tq, body, 0)


def _attn_a(h, cos_a, sin_a, gq, gk, batch, seq, tq=256):
    kern = functools.partial(_attn_a_kernel, tq=tq)
    return pl.pallas_call(
        kern,
        out_shape=jax.ShapeDtypeStruct((batch * seq, MIX_W), jnp.bfloat16),
        grid=(batch,),
        in_specs=[pl.BlockSpec((seq, 256), lambda b: (b, COL_AQ // 256)),
                  pl.BlockSpec((seq, 256), lambda b: (b, COL_AKV // 256)),
                  pl.BlockSpec((seq, LANES), lambda b: (0, 0)),
                  pl.BlockSpec((seq, LANES), lambda b: (0, 0)),
                  pl.BlockSpec((1, LANES), lambda b: (0, 0)),
                  pl.BlockSpec((1, LANES), lambda b: (0, 0))],
        out_specs=pl.BlockSpec((seq, MIX_W), lambda b: (b, 0)),
        scratch_shapes=[pltpu.VMEM((seq, 256), jnp.bfloat16), pltpu.VMEM((seq, LANES), jnp.bfloat16)],
        compiler_params=_cparams(("parallel",)),
    )(h, h, cos_a, sin_a, gq, gk)


def _attn_b_kernel(q_ref, k_ref, v_ref, bias_ref, o_ref, *, rows):
    nq = NA_QROWS * GRID_W
    nk = NA_KROWS * GRID_W
    lane_q = lax.broadcasted_iota(jnp.int32, (nq, LANES), 1)
    lo_q = lane_q < HEAD_DIM
    n_groups = rows // NA_QROWS
    scale = HEAD_DIM ** -0.5

    def body(gi, carry):
        ws = jnp.clip(gi * NA_QROWS - 4, 0, rows - NA_KROWS)
        cls = jnp.where(gi == 0, 0, jnp.where(gi == n_groups - 1, 2, 1))
        qr = pl.multiple_of(gi * nq, nq)
        kr = pl.multiple_of(ws * GRID_W, GRID_W)
        for blk in range(B_HEADS // 2):
            sl = slice(blk * LANES, (blk + 1) * LANES)
            qt = q_ref[pl.ds(qr, nq), sl] * scale
            kw = k_ref[pl.ds(kr, nk), sl]
            vw = v_ref[pl.ds(kr, nk), sl]
            zero = jnp.zeros_like(qt)
            s_lo = _dot_nt(jnp.where(lo_q, qt, zero), kw) + bias_ref[cls, 2 * blk]
            s_hi = _dot_nt(jnp.where(lo_q, zero, qt), kw) + bias_ref[cls, 2 * blk + 1]
            o_lo = _softmax_pv(s_lo, vw)
            o_hi = _softmax_pv(s_hi, vw)
            o_ref[pl.ds(qr, nq), sl] = jnp.where(lo_q, o_lo, o_hi).astype(o_ref.dtype)
        return carry

    lax.fori_loop(0, n_groups, body, 0)


def _attn_b(h, bias, batch, seq):
    rows = seq // GRID_W
    kern = functools.partial(_attn_b_kernel, rows=rows)
    return pl.pallas_call(
        kern,
        out_shape=jax.ShapeDtypeStruct((batch * seq, MIX_W), jnp.bfloat16),
        grid=(batch,),
        in_specs=[pl.BlockSpec((seq, 256), lambda b: (b, COL_BQ // 256)),
                  pl.BlockSpec((seq, 256), lambda b: (b, COL_BK // 256)),
                  pl.BlockSpec((seq, 256), lambda b: (b, COL_BV // 256)),
                  pl.BlockSpec(bias.shape, lambda b: (0, 0, 0, 0))],
        out_specs=pl.BlockSpec((seq, MIX_W), lambda b: (b, 0)),
        compiler_params=_cparams(("parallel",)),
    )(h, h, h, bias)


def _attn_c_kernel(hc_ref, wuq_ref, wuk_ref, wuv_ref, gq_ref, gkv_ref, cos_ref, sin_ref, o_ref,
                   qs_ref, ks_ref, vs_ref, *, tq):
    s_len = hc_ref.shape[0]
    lane = lax.broadcasted_iota(jnp.int32, (s_len, LANES), 1)
    cos = cos_ref[...]
    sin = sin_ref[...]

    cq = hc_ref[:, 0:256].astype(jnp.float32)
    ms = jnp.sum(cq * cq, axis=-1, keepdims=True) * (1.0 / C_Q_RANK)
    cqn = (cq * lax.rsqrt(ms + RMS_EPS) * gq_ref[...]).astype(jnp.bfloat16)
    ckv = hc_ref[:, 256:384].astype(jnp.float32)
    ms = jnp.mean(ckv * ckv, axis=-1, keepdims=True)
    kvn = (ckv * lax.rsqrt(ms + RMS_EPS) * gkv_ref[...]).astype(jnp.bfloat16)
    kpe = _rotate_pairs(hc_ref[:, 384:512].astype(jnp.float32), cos, sin, C_ROPE // 2, lane)
    vs_ref[...] = _dot(kvn, wuv_ref[...]).astype(jnp.bfloat16)
    scale = (C_NOPE + C_ROPE) ** -0.5
    for h in range(C_HEADS):
        sl = slice(h * LANES, (h + 1) * LANES)
        qh = _rotate_pairs(_dot(cqn, wuq_ref[:, sl]), cos, sin, C_ROPE // 2, lane) * scale
        qs_ref[:, sl] = qh.astype(jnp.bfloat16)
        ks_ref[:, sl] = (_dot(kvn, wuk_ref[:, sl]) + kpe).astype(jnp.bfloat16)

    lane_q = lax.broadcasted_iota(jnp.int32, (tq, LANES), 1)
    lo_q = lane_q < C_V

    def body(i, carry):
        r = pl.multiple_of(i * tq, tq)
        for blk in range(C_HEADS // 2):
            vb = vs_ref[:, blk * LANES:(blk + 1) * LANES]
            outs = []
            for par in range(2):
                sl = slice((2 * blk + par) * LANES, (2 * blk + par + 1) * LANES)
                outs.append(_softmax_pv(_dot_nt(qs_ref[pl.ds(r, tq), sl], ks_ref[:, sl]), vb))
            o_ref[pl.ds(r, tq), blk * LANES:(blk + 1) * LANES] = jnp.where(lo_q, outs[0], outs[1]).astype(o_ref.dtype)
        return carry

    lax.fori_loop(0, s_len // tq, body, 0)


def _attn_c(h, wuq, wuk, wuv, gq, gkv, cos_c, sin_c, batch, seq, tq=256):
    kern = functools.partial(_attn_c_kernel, tq=tq)
    const = lambda shape: pl.BlockSpec(shape, lambda b: (0,) * len(shape))
    return pl.pallas_call(
        kern,
        out_shape=jax.ShapeDtypeStruct((batch * seq, MIX_W), jnp.bfloat16),
        grid=(batch,),
        in_specs=[pl.BlockSpec((seq, 512), lambda b: (b, COL_C // 512)),
                  const(wuq.shape), const(wuk.shape), const(wuv.shape),
                  const(gq.shape), const(gkv.shape), const(cos_c.shape), const(sin_c.shape)],
        out_specs=pl.BlockSpec((seq, MIX_W), lambda b: (b, 0)),
        scratch_shapes=[pltpu.VMEM((seq, C_HEADS * LANES), jnp.bfloat16),
                        pltpu.VMEM((seq, C_HEADS * LANES), jnp.bfloat16),
                        pltpu.VMEM((seq, C_HEADS * C_V), jnp.bfloat16)],
        compiler_params=_cparams(("parallel",)),
    )(h, wuq, wuk, wuv, gq, gkv, cos_c, sin_c)


CONV_PAD = 16
CONV_ROWS = 128


def _conv_kernel(hd_ref, w_ref, cb_ref, g_ref, b_ref, o_ref, up_ref):
    s_len = hd_ref.shape[0]
    a = hd_ref[:, 0:D_CH].astype(jnp.float32)
    gate = hd_ref[:, D_CH:2 * D_CH].astype(jnp.float32)
    zeros = jnp.zeros((CONV_PAD, D_CH), jnp.float32)
    up_ref[0:CONV_PAD, :] = zeros
    up_ref[CONV_PAD + s_len:2 * CONV_PAD + s_len, :] = zeros
    up_ref[CONV_PAD:CONV_PAD + s_len, :] = a * _sigmoid(gate)
    shift = CONV_PAD - CONV_W // 2
    for c in range(s_len // CONV_ROWS):
        acc = jnp.zeros((CONV_ROWS, D_CH), jnp.float32)
        for j in range(CONV_W):
            start = c * CONV_ROWS + j + shift
            acc = acc + up_ref[start:start + CONV_ROWS, :] * w_ref[j:j + 1, :]
        u = acc + cb_ref[...]
        y = _layer_norm_rows(u, g_ref[...], b_ref[...])
        o_ref[c * CONV_ROWS:(c + 1) * CONV_ROWS, :] = (y * _sigmoid(y)).astype(o_ref.dtype)


def _conv_branch(h, w, cb, g, b, batch, seq):
    const = lambda shape: pl.BlockSpec(shape, lambda i: (0,) * len(shape))
    return pl.pallas_call(
        _conv_kernel,
        out_shape=jax.ShapeDtypeStruct((batch * seq, MIX_W), jnp.bfloat16),
        grid=(batch,),
        in_specs=[pl.BlockSpec((seq, 512), lambda i: (i, COL_D // 512)),
                  const(w.shape), const(cb.shape), const(g.shape), const(b.shape)],
        out_specs=pl.BlockSpec((seq, MIX_W), lambda i: (i, 0)),
        scratch_shapes=[pltpu.VMEM((seq + 2 * CONV_PAD, D_CH), jnp.float32)],
        compiler_params=_cparams(("parallel",)),
    )(h, w, cb, g, b)


def _merge_kernel(gl_ref, ya_ref, yb_ref, yc_ref, yd_ref, x_ref, wb_ref, wo_ref, g_ref, b_ref,
                  of_ref, ob_ref):
    merged = None
    for n, y_ref in enumerate((ya_ref, yb_ref, yc_ref, yd_ref)):
        gate = _sigmoid(gl_ref[:, n * D_MODEL:(n + 1) * D_MODEL].astype(jnp.float32))
        term = gate * _dot(y_ref[...], wb_ref[n])
        merged = term if merged is None else merged + term
    mix = _dot(merged.astype(jnp.bfloat16), wo_ref[...])
    y = _layer_norm_rows(ALPHA * x_ref[...] + mix, g_ref[...], b_ref[...])
    of_ref[...] = y
    ob_ref[...] = y.astype(jnp.bfloat16)


def _merge(h, ya, yb, yc, yd, x, wb, wo, g, b, tm=512):
    t, d = x.shape
    row = lambda w: pl.BlockSpec((tm, w), lambda i: (i, 0))
    const = lambda shape: pl.BlockSpec(shape, lambda i: (0,) * len(shape))
    return pl.pallas_call(
        _merge_kernel,
        out_shape=(jax.ShapeDtypeStruct((t, d), jnp.float32), jax.ShapeDtypeStruct((t, d), jnp.bfloat16)),
        grid=(t // tm,),
        in_specs=[pl.BlockSpec((tm, N_BRANCH * d), lambda i: (i, COL_GATES)),
                  row(MIX_W), row(MIX_W), row(MIX_W), row(MIX_W), row(d),
                  const(wb.shape), const(wo.shape), const((1, d)), const((1, d))],
        out_specs=(row(d), row(d)),
        compiler_params=_cparams(("parallel",)),
    )(h, ya, yb, yc, yd, x, wb, wo, g.reshape(1, d), b.reshape(1, d))


def _ffn_kernel(xb_ref, xf_ref, w1_ref, w3_ref, w2_ref, g_ref, b_ref, of_ref, ob_ref, acc_ref):
    j = pl.program_id(1)
    x = xb_ref[...]
    a = _dot(x, w1_ref[...])
    hid = (a * _sigmoid(a) * _dot(x, w3_ref[...])).astype(jnp.bfloat16)
    part = _dot(hid, w2_ref[...])

    @pl.when(j == 0)
    def _():
        acc_ref[...] = part

    @pl.when(j > 0)
    def _():
        acc_ref[...] += part

    @pl.when(j == pl.num_programs(1) - 1)
    def _():
        y = _layer_norm_rows(ALPHA * xf_ref[...] + acc_ref[...], g_ref[...], b_ref[...])
        of_ref[...] = y
        ob_ref[...] = y.astype(jnp.bfloat16)


def _ffn(xb, xf, w1, w3, w2, g, b, tm=512, tf=1408):
    t, d = xf.shape
    f = w1.shape[1]
    row = pl.BlockSpec((tm, d), lambda i, j: (i, 0))
    vec = pl.BlockSpec((1, d), lambda i, j: (0, 0))
    return pl.pallas_call(
        _ffn_kernel,
        out_shape=(jax.ShapeDtypeStruct((t, d), jnp.float32), jax.ShapeDtypeStruct((t, d), jnp.bfloat16)),
        grid=(t // tm, f // tf),
        in_specs=[row, row,
                  pl.BlockSpec((d, tf), lambda i, j: (0, j)),
                  pl.BlockSpec((d, tf), lambda i, j: (0, j)),
                  pl.BlockSpec((tf, d), lambda i, j: (j, 0)),
                  vec, vec],
        out_specs=(row, row),
        scratch_shapes=[pltpu.VMEM((tm, d), jnp.float32)],
        compiler_params=_cparams(("parallel", "arbitrary")),
    )(xb, xf, w1, w3, w2, g.reshape(1, d), b.reshape(1, d))


def _router_kernel(x_ref, r_ref, tri_ref, oi_ref, ow_ref, cnt_ref, carry_ref):
    i = pl.program_id(0)

    @pl.when(i == 0)
    def _():
        carry_ref[...] = jnp.zeros_like(carry_ref)

    x = x_ref[...]
    r = r_ref[...]
    xh = x.astype(jnp.bfloat16)
    xl = (x - xh.astype(jnp.float32)).astype(jnp.bfloat16)
    rh = r.astype(jnp.bfloat16)
    rl = (r - rh.astype(jnp.float32)).astype(jnp.bfloat16)
    logits = (_dot(xh, rh) + _dot(xh, rl)) + (_dot(xl, rh) + _dot(xl, rl))
    tm = x.shape[0]
    lane = lax.broadcasted_iota(jnp.int32, (tm, LANES), 1)
    logits = jnp.where(lane < N_EXPERTS, logits, NEG_BIG)
    m1 = jnp.max(logits, axis=-1, keepdims=True)
    i1 = jnp.min(jnp.where(logits == m1, lane, LANES), axis=-1, keepdims=True)
    rest = jnp.where(lane == i1, NEG_BIG, logits)
    m2 = jnp.max(rest, axis=-1, keepdims=True)
    i2 = jnp.min(jnp.where(rest == m2, lane, LANES), axis=-1, keepdims=True)
    e = jnp.exp(m2 - m1)
    w1 = 1.0 / (1.0 + e)
    w2 = e / (1.0 + e)
    sel1 = lane == i1
    sel2 = lane == i2
    member = jnp.where(sel1 | sel2, 1.0, 0.0)
    before = _dot(tri_ref[...], member.astype(jnp.bfloat16)) + carry_ref[...]
    rank1 = jnp.sum(jnp.where(sel1, before, 0.0), axis=-1, keepdims=True).astype(jnp.int32)
    rank2 = jnp.sum(jnp.where(sel2, before, 0.0), axis=-1, keepdims=True).astype(jnp.int32)
    carry_ref[...] += jnp.sum(member, axis=0, keepdims=True)
    oi_ref[...] = jnp.where(lane == 0, i1, jnp.where(lane == 1, i2, jnp.where(lane == 2, rank1, rank2)))
    ow_ref[...] = jnp.where(lane == 0, w1, w2)
    cnt_ref[...] = carry_ref[...]


def _router(xf, router, tm=512):
    t, d = xf.shape
    r_pad = jnp.zeros((d, LANES), jnp.float32).at[:, :N_EXPERTS].set(router.astype(jnp.float32))
    tri = jnp.asarray(np.tril(np.ones((tm, tm), np.float32), -1), jnp.bfloat16)
    return pl.pallas_call(
        _router_kernel,
        out_shape=(jax.ShapeDtypeStruct((t, LANES), jnp.int32),
                   jax.ShapeDtypeStruct((t, LANES), jnp.float32),
                   jax.ShapeDtypeStruct((1, LANES), jnp.float32)),
        grid=(t // tm,),
        in_specs=[pl.BlockSpec((tm, d), lambda i: (i, 0)),
                  pl.BlockSpec((d, LANES), lambda i: (0, 0)),
                  pl.BlockSpec((tm, tm), lambda i: (0, 0))],
        out_specs=(pl.BlockSpec((tm, LANES), lambda i: (i, 0)),
                   pl.BlockSpec((tm, LANES), lambda i: (i, 0)),
                   pl.BlockSpec((1, LANES), lambda i: (0, 0))),
        scratch_shapes=[pltpu.VMEM((1, LANES), jnp.float32)],
        compiler_params=_cparams(("arbitrary",)),
    )(xf, r_pad, tri)


def _dispatch_kernel(src_ref, x_hbm, o_ref, buf_ref, sem):
    i = pl.program_id(0)
    tm = buf_ref.shape[0]

    def row_copy(k):
        tok = src_ref[i * tm + k]
        return pltpu.make_async_copy(x_hbm.at[pl.ds(tok, 1)], buf_ref.at[pl.ds(k, 1)], sem)

    def start(k, c):
        row_copy(k).start()
        return c

    def wait(k, c):
        row_copy(k).wait()
        return c

    lax.fori_loop(0, tm, start, 0)
    lax.fori_loop(0, tm, wait, 0)
    o_ref[...] = buf_ref[...].astype(o_ref.dtype)


def _dispatch(src, xf, n_rows, tm=256):
    t, d = xf.shape
    return pl.pallas_call(
        _dispatch_kernel,
        out_shape=jax.ShapeDtypeStruct((n_rows, d), jnp.bfloat16),
        grid_spec=pltpu.PrefetchScalarGridSpec(
            num_scalar_prefetch=1,
            grid=(n_rows // tm,),
            in_specs=[pl.BlockSpec(memory_space=pl.ANY)],
            out_specs=pl.BlockSpec((tm, d), lambda i, src: (i, 0)),
            scratch_shapes=[pltpu.VMEM((tm, d), jnp.float32), pltpu.SemaphoreType.DMA(())],
        ),
        compiler_params=_cparams(("arbitrary",)),
    )(src, xf)


def _expert_kernel(te_ref, nu_ref, x_ref, w1_ref, w3_ref, w2_ref, o_ref):
    i = pl.program_id(0)
    j = pl.program_id(1)
    used = i < nu_ref[0]

    @pl.when(used)
    def _():
        x = x_ref[...]
        a = _dot(x, w1_ref[0])
        hid = (a * _sigmoid(a) * _dot(x, w3_ref[0])).astype(jnp.bfloat16)
        part = _dot(hid, w2_ref[0])

        @pl.when(j == 0)
        def _():
            o_ref[...] = part

        @pl.when(j > 0)
        def _():
            o_ref[...] += part

    @pl.when(jnp.logical_not(used) & (j == 0))
    def _():
        o_ref[...] = jnp.zeros_like(o_ref)


def _experts(tile_expert, n_used, xs, w1, w3, w2):
    n_rows, d = xs.shape
    f = w1.shape[2]
    nj = f // MOE_FCHUNK

    def chunk(i, j, nu):
        return jnp.where(i < nu[0], j, nj - 1)

    return pl.pallas_call(
        _expert_kernel,
        out_shape=jax.ShapeDtypeStruct((n_rows, d), jnp.float32),
        grid_spec=pltpu.PrefetchScalarGridSpec(
            num_scalar_prefetch=2,
            grid=(n_rows // MOE_TILE, nj),
            in_specs=[pl.BlockSpec((MOE_TILE, d), lambda i, j, te, nu: (i, 0)),
                      pl.BlockSpec((1, d, MOE_FCHUNK), lambda i, j, te, nu: (te[i], 0, chunk(i, j, nu))),
                      pl.BlockSpec((1, d, MOE_FCHUNK), lambda i, j, te, nu: (te[i], 0, chunk(i, j, nu))),
                      pl.BlockSpec((1, MOE_FCHUNK, d), lambda i, j, te, nu: (te[i], chunk(i, j, nu), 0))],
            out_specs=pl.BlockSpec((MOE_TILE, d), lambda i, j, te, nu: (i, 0)),
        ),
        compiler_params=_cparams(("arbitrary", "arbitrary")),
    )(tile_expert, n_used, xs, w1, w3, w2)


def _combine_kernel(d1_ref, d2_ref, y_hbm, x_ref, w_ref, g_ref, b_ref, o_ref, ya_ref, yb_ref, sem):
    i = pl.program_id(0)
    tm = ya_ref.shape[0]

    def copies(k):
        r1 = d1_ref[i * tm + k]
        r2 = d2_ref[i * tm + k]
        return (pltpu.make_async_copy(y_hbm.at[pl.ds(r1, 1)], ya_ref.at[pl.ds(k, 1)], sem.at[0]),
                pltpu.make_async_copy(y_hbm.at[pl.ds(r2, 1)], yb_ref.at[pl.ds(k, 1)], sem.at[1]))

    def start(k, c):
        ca, cb = copies(k)
        ca.start()
        cb.start()
        return c

    def wait(k, c):
        ca, cb = copies(k)
        ca.wait()
        cb.wait()
        return c

    lax.fori_loop(0, tm, start, 0)
    lax.fori_loop(0, tm, wait, 0)
    w = w_ref[...]
    f = w[:, 0:1] * ya_ref[...] + w[:, 1:2] * yb_ref[...]
    o_ref[...] = _layer_norm_rows(ALPHA * x_ref[...] + f, g_ref[...], b_ref[...])


def _combine(dest1, dest2, y, xf, wts, g, b, tm=256):
    t, d = xf.shape
    return pl.pallas_call(
        _combine_kernel,
        out_shape=jax.ShapeDtypeStruct((t, d), jnp.float32),
        grid_spec=pltpu.PrefetchScalarGridSpec(
            num_scalar_prefetch=2,
            grid=(t // tm,),
            in_specs=[pl.BlockSpec(memory_space=pl.ANY),
                      pl.BlockSpec((tm, d), lambda i, a, c: (i, 0)),
                      pl.BlockSpec((tm, LANES), lambda i, a, c: (i, 0)),
                      pl.BlockSpec((1, d), lambda i, a, c: (0, 0)),
                      pl.BlockSpec((1, d), lambda i, a, c: (0, 0))],
            out_specs=pl.BlockSpec((tm, d), lambda i, a, c: (i, 0)),
            scratch_shapes=[pltpu.VMEM((tm, d), jnp.float32), pltpu.VMEM((tm, d), jnp.float32),
                            pltpu.SemaphoreType.DMA((2,))],
        ),
        compiler_params=_cparams(("arbitrary",)),
    )(dest1, dest2, y, xf, wts, g.reshape(1, d), b.reshape(1, d))


def _moe_ffn(xf, router, w1, w3, w2, g, b):
    t, d = xf.shape
    oi, ow, cnt = _router(xf, router)
    e1, e2, rank1, rank2 = oi[:, 0], oi[:, 1], oi[:, 2], oi[:, 3]
    counts = cnt[0, :N_EXPERTS].astype(jnp.int32)
    tiles = (counts + MOE_TILE - 1) // MOE_TILE
    tile_end = jnp.cumsum(tiles)
    offs = (tile_end - tiles) * MOE_TILE
    dest1 = offs[e1] + rank1
    dest2 = offs[e2] + rank2
    n_rows = TOP_K * t + N_EXPERTS * MOE_TILE
    n_tiles = n_rows // MOE_TILE
    tok = jnp.arange(t, dtype=jnp.int32)
    src = jnp.zeros((n_rows,), jnp.int32).at[dest1].set(tok).at[dest2].set(tok)
    n_used = tile_end[-1:].astype(jnp.int32)
    tile_expert = jnp.searchsorted(tile_end, jnp.arange(n_tiles, dtype=jnp.int32), side="right")
    last_expert = tile_expert[jnp.maximum(n_used[0] - 1, 0)]
    tile_expert = jnp.where(jnp.arange(n_tiles) < n_used[0], tile_expert, last_expert).astype(jnp.int32)
    xs = _dispatch(src, xf, n_rows)
    y = _experts(tile_expert, n_used, xs, w1, w3, w2)
    return _combine(dest1, dest2, y, xf, ow, g, b)


def _prep_layer(l, p, in_cols):
    bf = jnp.bfloat16
    out = {}
    out["w_in"] = _take_cols(p["w_in"][l], in_cols).astype(bf)
    deint = np.concatenate([np.arange(0, HEAD_DIM, 2), np.arange(1, HEAD_DIM, 2)])
    out["gq_a"] = jnp.tile(p["a_q_norm"][l][deint], 2).reshape(1, LANES)
    out["gk_a"] = jnp.tile(p["a_k_norm"][l][deint], 2).reshape(1, LANES)
    per_head = C_NOPE + C_ROPE
    uq_cols = []
    for h in range(C_HEADS):
        base = h * per_head
        uq_cols += [base + np.arange(C_NOPE), base + C_NOPE + np.arange(0, C_ROPE, 2),
                    base + C_NOPE + np.arange(1, C_ROPE, 2), np.full(32, -1)]
    wuq = _take_cols(p["c_w_uq"][l], np.concatenate(uq_cols))
    out["wuq"] = jnp.concatenate([wuq, jnp.zeros((64, wuq.shape[1]), wuq.dtype)], axis=0).astype(bf)
    uk_cols, uv_cols = [], []
    for h in range(C_HEADS):
        base = h * (C_NOPE + C_V)
        uk_cols += [base + np.arange(C_NOPE), np.full(64, -1)]
        uv_cols += [base + C_NOPE + np.arange(C_V)]
    out["wuk"] = _take_cols(p["c_w_ukv"][l], np.concatenate(uk_cols)).astype(bf)
    out["wuv"] = _take_cols(p["c_w_ukv"][l], np.concatenate(uv_cols)).astype(bf)
    out["gq_c"] = jnp.concatenate([p["c_q_norm"][l], jnp.zeros((64,), jnp.float32)]).reshape(1, 256)
    out["gkv_c"] = p["c_kv_norm"][l].reshape(1, C_KV_RANK)
    out["conv_w"] = jnp.concatenate([p["d_conv_w"][l][:, 0, :], jnp.zeros((1, D_CH), jnp.float32)], axis=0)
    out["conv_b"] = p["d_conv_b"][l].reshape(1, D_CH)
    out["d_ln_g"] = p["d_ln_g"][l].reshape(1, D_CH)
    out["d_ln_b"] = p["d_ln_b"][l].reshape(1, D_CH)
    a_rows = np.concatenate([(kvh * 2 + g) * HEAD_DIM + np.arange(HEAD_DIM)
                             for g in range(2) for kvh in range(A_KV_HEADS)])
    wb = p["w_branch"][l]
    out["w_branch"] = jnp.stack([wb[0][a_rows], wb[1], wb[2], wb[3]], axis=0).astype(bf)
    out["w_out"] = p["w_out"][l].astype(bf)
    return out


def kernel(x, ln_in_g, ln_in_b, w_in, a_q_norm, a_k_norm, b_rpb, c_q_norm, c_kv_norm, c_w_uq, c_w_ukv,
           d_conv_w, d_conv_b, d_ln_g, d_ln_b, w_branch, w_out, ln_mix_g, ln_mix_b,
           ffn_w1, ffn_w3, ffn_w2, moe_router, moe_w1, moe_w3, moe_w2, ln_ffn_g, ln_ffn_b):
    batch, seq, d = x.shape
    t = batch * seq
    bf = jnp.bfloat16
    params = dict(w_in=w_in, a_q_norm=a_q_norm, a_k_norm=a_k_norm, c_q_norm=c_q_norm, c_kv_norm=c_kv_norm,
                  c_w_uq=c_w_uq, c_w_ukv=c_w_ukv, d_conv_w=d_conv_w, d_conv_b=d_conv_b, d_ln_g=d_ln_g,
                  d_ln_b=d_ln_b, w_branch=w_branch, w_out=w_out)
    in_cols = _in_proj_columns()
    cos_a, sin_a, cos_c, sin_c = _rope_tables(seq)
    xf, xb = _input_ln(x.reshape(t, d), ln_in_g, ln_in_b)
    for l in range(DEPTH):
        p = _prep_layer(l, params, in_cols)
        h = _matmul(xb, p["w_in"], bf, tm=1024, tn=1280)
        ya = _attn_a(h, cos_a, sin_a, p["gq_a"], p["gk_a"], batch, seq)
        yb = _attn_b(h, _na_bias_tables(b_rpb[l], seq // GRID_W), batch, seq)
        yc = _attn_c(h, p["wuq"], p["wuk"], p["wuv"], p["gq_c"], p["gkv_c"], cos_c, sin_c, batch, seq)
        yd = _conv_branch(h, p["conv_w"], p["conv_b"], p["d_ln_g"], p["d_ln_b"], batch, seq)
        xf, xb = _merge(h, ya, yb, yc, yd, xf, p["w_branch"], p["w_out"], ln_mix_g[l], ln_mix_b[l])
        if l % 2 == 0:
            i = l // 2
            xf, xb = _ffn(xb, xf, ffn_w1[i].astype(bf), ffn_w3[i].astype(bf), ffn_w2[i].astype(bf),
                          ln_ffn_g[l], ln_ffn_b[l])
        else:
            i = l // 2
            xf = _moe_ffn(xf, moe_router[i], moe_w1[i].astype(bf), moe_w3[i].astype(bf), moe_w2[i].astype(bf),
                          ln_ffn_g[l], ln_ffn_b[l])
            xb = xf.astype(bf)
    return xf.reshape(batch, seq, d)
```

```python
import functools

import numpy as np
import jax
import jax.numpy as jnp
from jax import lax
from jax.experimental import pallas as pl
from jax.experimental.pallas import tpu as pltpu
from jax.experimental.pallas import tpu_sc as plsc

D_MODEL = 1024
DEPTH = 2
GRID_W = 64
HEAD_DIM = 64
ROPE_THETA = 10000.0
RMS_EPS = 1e-6
LN_EPS = 1e-5
A_HEADS = 4
A_KV_HEADS = 2
B_HEADS = 4
NA_WIN_H = 8
NA_WIN_W = 16
C_HEADS = 4
C_NOPE = 64
C_ROPE = 32
C_V = 64
C_Q_RANK = 192
C_KV_RANK = 128
D_CH = 256
CONV_W = 31
N_BRANCH = 4
MIX_W = 256
D_FF = 2816
N_EXPERTS = 8
TOP_K = 2
D_FF_EXPERT = 3584
ALPHA = (2 * DEPTH) ** 0.25

LANES = 128
VMEM_LIMIT = 48 * 1024 * 1024

NEG_BIG = -1e30

COL_GATES = 0
COL_AQ = 4096
COL_AKV = 4352
COL_C = 4608
COL_D = 5120
COL_BQ = 5632
COL_BK = 5888
COL_BV = 6144
D_IN_PAD = 6400

NA_QROWS = 4
NA_KROWS = 12

MOE_TILE = 512
MOE_FCHUNK = 896


def _in_proj_columns():
    o_aq, o_ak, o_av = 0, 256, 384
    o_bq, o_bk, o_bv = 512, 768, 1024
    o_cq, o_ckv, o_kpe = 1280, 1472, 1600
    o_d, o_g = 1632, 2144
    deint = np.concatenate([np.arange(0, HEAD_DIM, 2), np.arange(1, HEAD_DIM, 2)])
    cols = [o_g + np.arange(N_BRANCH * D_MODEL)]
    for g in range(2):
        for kvh in range(A_KV_HEADS):
            cols.append(o_aq + (kvh * 2 + g) * HEAD_DIM + deint)
    for kvh in range(A_KV_HEADS):
        cols.append(o_ak + kvh * HEAD_DIM + deint)
    cols.append(o_av + np.arange(A_KV_HEADS * HEAD_DIM))
    cols.append(o_cq + np.arange(C_Q_RANK))
    cols.append(np.full(64, -1))
    cols.append(o_ckv + np.arange(C_KV_RANK))
    cols.append(np.full(64, -1))
    cols.append(o_kpe + np.arange(0, C_ROPE, 2))
    cols.append(o_kpe + np.arange(1, C_ROPE, 2))
    cols.append(np.full(32, -1))
    cols.append(o_d + np.arange(2 * D_CH))
    cols.append(o_bq + np.arange(3 * B_HEADS * HEAD_DIM))
    cols = np.concatenate(cols)
    assert cols.shape[0] == D_IN_PAD
    return cols


def _take_cols(w, cols):
    cols = np.asarray(cols)
    parts, i, n = [], 0, len(cols)
    while i < n:
        j = i + 1
        if cols[i] < 0:
            while j < n and cols[j] < 0:
                j += 1
            parts.append(jnp.zeros((w.shape[0], j - i), w.dtype))
        else:
            step = int(cols[j] - cols[i]) if j < n and cols[j] > cols[i] else 1
            while j < n and cols[j] == cols[j - 1] + step:
                j += 1
            parts.append(lax.slice(w, (0, int(cols[i])), (w.shape[0], int(cols[j - 1]) + 1), (1, step)))
        i = j
    return jnp.concatenate(parts, axis=1)


def _rope_tables(seq):
    t = np.arange(seq)
    row = (t // GRID_W).astype(np.float32)
    col = (t % GRID_W).astype(np.float32)

    def angles(rot_dim):
        n_freq = rot_dim // 4
        inv = jnp.asarray(ROPE_THETA, jnp.float32) ** (-jnp.arange(n_freq, dtype=jnp.float32) / n_freq)
        return jnp.concatenate([jnp.asarray(row)[:, None] * inv, jnp.asarray(col)[:, None] * inv], axis=-1)

    ang_a = angles(HEAD_DIM)
    ca, sa = jnp.cos(ang_a), jnp.sin(ang_a)
    cos_a = jnp.concatenate([ca, ca, ca, ca], axis=-1)
    sin_a = jnp.concatenate([-sa, sa, -sa, sa], axis=-1)
    ang_c = angles(C_ROPE)
    cc, sc = jnp.cos(ang_c), jnp.sin(ang_c)
    one = jnp.ones((seq, 64), jnp.float32)
    zero = jnp.zeros((seq, 64), jnp.float32)
    cos_c = jnp.concatenate([one, cc, cc, one[:, :32]], axis=-1)
    sin_c = jnp.concatenate([zero, -sc, sc, zero[:, :32]], axis=-1)
    return cos_a, sin_a, cos_c, sin_c


def _na_bias_tables(rpb, rows):
    wh = min(NA_WIN_H, rows)
    n_dr, n_dc = 2 * NA_WIN_H - 1, 2 * NA_WIN_W - 1
    qc = np.arange(GRID_W)[:, None]
    kc = np.arange(GRID_W)[None, :]
    col_start = np.clip(qc - NA_WIN_W // 2, 0, GRID_W - NA_WIN_W)
    ok_c = (kc >= col_start) & (kc < col_start + NA_WIN_W)
    dc = np.clip(kc - qc + (NA_WIN_W - 1), 0, n_dc - 1)
    oh_c = np.eye(n_dc, dtype=np.float32)[dc.reshape(-1)]
    oh_r, ok_r = [], []
    for r0 in (0, NA_QROWS, rows - NA_QROWS):
        ws = int(np.clip(r0 - 4, 0, rows - NA_KROWS))
        qr = r0 + np.arange(NA_QROWS)[:, None]
        kr = ws + np.arange(NA_KROWS)[None, :]
        row_start = np.clip(qr - wh // 2, 0, rows - wh)
        ok_r.append((kr >= row_start) & (kr < row_start + wh))
        dr = np.clip(kr - qr + (NA_WIN_H - 1), 0, n_dr - 1)
        oh_r.append(np.eye(n_dr, dtype=np.float32)[dr.reshape(-1)])
    oh_r = np.concatenate(oh_r, axis=0)
    ok = np.stack(ok_r)[:, :, None, :, None] & ok_c[None, None, :, None, :]
    ok = ok.reshape(3, 1, NA_QROWS * GRID_W, NA_KROWS * GRID_W)
    hi = lax.Precision.HIGHEST
    t1 = jnp.einsum("ma,hab->hmb", jnp.asarray(oh_r), rpb.astype(jnp.float32), precision=hi)
    t2 = jnp.einsum("hmb,nb->hmn", t1, jnp.asarray(oh_c), precision=hi)
    nh = rpb.shape[0]
    t2 = t2.reshape(nh, 3, NA_QROWS, NA_KROWS, GRID_W, GRID_W).transpose(1, 0, 2, 4, 3, 5)
    bias = t2.reshape(3, nh, NA_QROWS * GRID_W, NA_KROWS * GRID_W)
    return jnp.where(jnp.asarray(ok), bias, NEG_BIG)


def _cparams(sem):
    return pltpu.CompilerParams(dimension_semantics=sem, vmem_limit_bytes=VMEM_LIMIT)


def _layer_norm_rows(z, g, b):
    mu = jnp.mean(z, axis=-1, keepdims=True)
    zc = z - mu
    var = jnp.mean(zc * zc, axis=-1, keepdims=True)
    return zc * lax.rsqrt(var + LN_EPS) * g + b


def _sigmoid(x):
    return 0.5 * jnp.tanh(0.5 * x) + 0.5


def _dot(a, b):
    return jnp.dot(a, b, preferred_element_type=jnp.float32)


def _dot_nt(a, b):
    return lax.dot_general(a, b, (((1,), (1,)), ((), ())), preferred_element_type=jnp.float32)


def _ln_kernel(x_ref, g_ref, b_ref, of_ref, ob_ref):
    y = _layer_norm_rows(x_ref[...], g_ref[...], b_ref[...])
    of_ref[...] = y
    ob_ref[...] = y.astype(jnp.bfloat16)


def _input_ln(x, g, b, tm=1024):
    t, d = x.shape
    return pl.pallas_call(
        _ln_kernel,
        out_shape=(jax.ShapeDtypeStruct((t, d), jnp.float32), jax.ShapeDtypeStruct((t, d), jnp.bfloat16)),
        grid=(t // tm,),
        in_specs=[pl.BlockSpec((tm, d), lambda i: (i, 0)),
                  pl.BlockSpec((1, d), lambda i: (0, 0)),
                  pl.BlockSpec((1, d), lambda i: (0, 0))],
        out_specs=(pl.BlockSpec((tm, d), lambda i: (i, 0)), pl.BlockSpec((tm, d), lambda i: (i, 0))),
        compiler_params=_cparams(("parallel",)),
    )(x, g.reshape(1, d), b.reshape(1, d))


def _mm_kernel(a_ref, b_ref, o_ref):
    o_ref[...] = _dot(a_ref[...], b_ref[...]).astype(o_ref.dtype)


def _matmul(a, b, out_dtype, tm, tn):
    m, k = a.shape
    n = b.shape[1]
    return pl.pallas_call(
        _mm_kernel,
        out_shape=jax.ShapeDtypeStruct((m, n), out_dtype),
        grid=(m // tm, n // tn),
        in_specs=[pl.BlockSpec((tm, k), lambda i, j: (i, 0)),
                  pl.BlockSpec((k, tn), lambda i, j: (0, j))],
        out_specs=pl.BlockSpec((tm, tn), lambda i, j: (i, j)),
        compiler_params=_cparams(("parallel", "parallel")),
    )(a, b)


def _softmax_pv(s, v):
    m = jnp.max(s, axis=-1, keepdims=True)
    p = jnp.exp(s - m)
    l = jnp.sum(p, axis=-1, keepdims=True)
    o = _dot(p.astype(jnp.bfloat16), v)
    return o * (1.0 / l)


def _rotate_pairs(x, cos, sin, half, lane):
    w = x.shape[-1]
    first = (lane % (2 * half)) < half
    partner = jnp.where(first, pltpu.roll(x, w - half, 1), pltpu.roll(x, half, 1))
    return x * cos + partner * sin


def _attn_a_kernel(q_ref, kv_ref, cos_ref, sin_ref, gq_ref, gk_ref, o_ref, qs_ref, ks_ref, *, tq):
    s_len = q_ref.shape[0]
    lane = lax.broadcasted_iota(jnp.int32, (s_len, LANES), 1)
    lo = lane < HEAD_DIM
    cos = cos_ref[...]
    sin = sin_ref[...]

    def prep(x, g):
        x2 = x * x
        ss_lo = jnp.sum(jnp.where(lo, x2, 0.0), axis=-1, keepdims=True)
        ss_hi = jnp.sum(jnp.where(lo, 0.0, x2), axis=-1, keepdims=True)
        ms = jnp.where(lo, ss_lo, ss_hi) * (1.0 / HEAD_DIM)
        xn = x * lax.rsqrt(ms + RMS_EPS) * g
        return _rotate_pairs(xn, cos, sin, HEAD_DIM // 2, lane)

    ks_ref[...] = prep(kv_ref[:, 0:LANES].astype(jnp.float32), gk_ref[...]).astype(jnp.bfloat16)
    scale = HEAD_DIM ** -0.5
    for g in range(2):
        qg = prep(q_ref[:, g * LANES:(g + 1) * LANES].astype(jnp.float32), gq_ref[...]) * scale
        qs_ref[:, g * LANES:(g + 1) * LANES] = qg.astype(jnp.bfloat16)

    lane_q = lax.broadcasted_iota(jnp.int32, (tq, LANES), 1)
    lo_q = lane_q < HEAD_DIM

    def body(i, carry):
        r = pl.multiple_of(i * tq, tq)
        k = ks_ref[...]
        v = kv_ref[:, LANES:2 * LANES]
        for g in range(2):
            qt = qs_ref[pl.ds(r, tq), g * LANES:(g + 1) * LANES]
            zero = jnp.zeros_like(qt)
            o_lo = _softmax_pv(_dot_nt(jnp.where(lo_q, qt, zero), k), v)
            o_hi = _softmax_pv(_dot_nt(jnp.where(lo_q, zero, qt), k), v)
            o_ref[pl.ds(r, tq), g * LANES:(g + 1) * LANES] = jnp.where(lo_q, o_lo, o_hi).astype(o_ref.dtype)
        return carry

    lax.fori_loop(0, s_len // tq, body, 0)


def _attn_a(h, cos_a, sin_a, gq, gk, batch, seq, tq=256):
    kern = functools.partial(_attn_a_kernel, tq=tq)
    return pl.pallas_call(
        kern,
        out_shape=jax.ShapeDtypeStruct((batch * seq, MIX_W), jnp.bfloat16),
        grid=(batch,),
        in_specs=[pl.BlockSpec((seq, 256), lambda b: (b, COL_AQ // 256)),
                  pl.BlockSpec((seq, 256), lambda b: (b, COL_AKV // 256)),
                  pl.BlockSpec((seq, LANES), lambda b: (0, 0)),
                  pl.BlockSpec((seq, LANES), lambda b: (0, 0)),
                  pl.BlockSpec((1, LANES), lambda b: (0, 0)),
                  pl.BlockSpec((1, LANES), lambda b: (0, 0))],
        out_specs=pl.BlockSpec((seq, MIX_W), lambda b: (b, 0)),
        scratch_shapes=[pltpu.VMEM((seq, 256), jnp.bfloat16), pltpu.VMEM((seq, LANES), jnp.bfloat16)],
        compiler_params=_cparams(("parallel",)),
    )(h, h, cos_a, sin_a, gq, gk)


def _attn_b_kernel(q_ref, k_ref, v_ref, bias_ref, o_ref, *, rows):
    nq = NA_QROWS * GRID_W
    nk = NA_KROWS * GRID_W
    lane_q = lax.broadcasted_iota(jnp.int32, (nq, LANES), 1)
    lo_q = lane_q < HEAD_DIM
    n_groups = rows // NA_QROWS
    scale = HEAD_DIM ** -0.5

    def body(gi, carry):
        ws = jnp.clip(gi * NA_QROWS - 4, 0, rows - NA_KROWS)
        cls = jnp.where(gi == 0, 0, jnp.where(gi == n_groups - 1, 2, 1))
        qr = pl.multiple_of(gi * nq, nq)
        kr = pl.multiple_of(ws * GRID_W, GRID_W)
        for blk in range(B_HEADS // 2):
            sl = slice(blk * LANES, (blk + 1) * LANES)
            qt = q_ref[pl.ds(qr, nq), sl] * scale
            kw = k_ref[pl.ds(kr, nk), sl]
            vw = v_ref[pl.ds(kr, nk), sl]
            zero = jnp.zeros_like(qt)
            s_lo = _dot_nt(jnp.where(lo_q, qt, zero), kw) + bias_ref[cls, 2 * blk]
            s_hi = _dot_nt(jnp.where(lo_q, zero, qt), kw) + bias_ref[cls, 2 * blk + 1]
            o_lo = _softmax_pv(s_lo, vw)
            o_hi = _softmax_pv(s_hi, vw)
            o_ref[pl.ds(qr, nq), sl] = jnp.where(lo_q, o_lo, o_hi).astype(o_ref.dtype)
        return carry

    lax.fori_loop(0, n_groups, body, 0)


def _attn_b(h, bias, batch, seq):
    rows = seq // GRID_W
    kern = functools.partial(_attn_b_kernel, rows=rows)
    return pl.pallas_call(
        kern,
        out_shape=jax.ShapeDtypeStruct((batch * seq, MIX_W), jnp.bfloat16),
        grid=(batch,),
        in_specs=[pl.BlockSpec((seq, 256), lambda b: (b, COL_BQ // 256)),
                  pl.BlockSpec((seq, 256), lambda b: (b, COL_BK // 256)),
                  pl.BlockSpec((seq, 256), lambda b: (b, COL_BV // 256)),
                  pl.BlockSpec(bias.shape, lambda b: (0, 0, 0, 0))],
        out_specs=pl.BlockSpec((seq, MIX_W), lambda b: (b, 0)),
        compiler_params=_cparams(("parallel",)),
    )(h, h, h, bias)


def _attn_c_kernel(hc_ref, wuq_ref, wuk_ref, wuv_ref, gq_ref, gkv_ref, cos_ref, sin_ref, o_ref,
                   qs_ref, ks_ref, vs_ref, *, tq):
    s_len = hc_ref.shape[0]
    lane = lax.broadcasted_iota(jnp.int32, (s_len, LANES), 1)
    cos = cos_ref[...]
    sin = sin_ref[...]

    cq = hc_ref[:, 0:256].astype(jnp.float32)
    ms = jnp.sum(cq * cq, axis=-1, keepdims=True) * (1.0 / C_Q_RANK)
    cqn = (cq * lax.rsqrt(ms + RMS_EPS) * gq_ref[...]).astype(jnp.bfloat16)
    ckv = hc_ref[:, 256:384].astype(jnp.float32)
    ms = jnp.mean(ckv * ckv, axis=-1, keepdims=True)
    kvn = (ckv * lax.rsqrt(ms + RMS_EPS) * gkv_ref[...]).astype(jnp.bfloat16)
    kpe = _rotate_pairs(hc_ref[:, 384:512].astype(jnp.float32), cos, sin, C_ROPE // 2, lane)
    vs_ref[...] = _dot(kvn, wuv_ref[...]).astype(jnp.bfloat16)
    scale = (C_NOPE + C_ROPE) ** -0.5
    for h in range(C_HEADS):
        sl = slice(h * LANES, (h + 1) * LANES)
        qh = _rotate_pairs(_dot(cqn, wuq_ref[:, sl]), cos, sin, C_ROPE // 2, lane) * scale
        qs_ref[:, sl] = qh.astype(jnp.bfloat16)
        ks_ref[:, sl] = (_dot(kvn, wuk_ref[:, sl]) + kpe).astype(jnp.bfloat16)

    lane_q = lax.broadcasted_iota(jnp.int32, (tq, LANES), 1)
    lo_q = lane_q < C_V

    def body(i, carry):
        r = pl.multiple_of(i * tq, tq)
        for blk in range(C_HEADS // 2):
            vb = vs_ref[:, blk * LANES:(blk + 1) * LANES]
            outs = []
            for par in range(2):
                sl = slice((2 * blk + par) * LANES, (2 * blk + par + 1) * LANES)
                outs.append(_softmax_pv(_dot_nt(qs_ref[pl.ds(r, tq), sl], ks_ref[:, sl]), vb))
            o_ref[pl.ds(r, tq), blk * LANES:(blk + 1) * LANES] = jnp.where(lo_q, outs[0], outs[1]).astype(o_ref.dtype)
        return carry

    lax.fori_loop(0, s_len // tq, body, 0)


def _attn_c(h, wuq, wuk, wuv, gq, gkv, cos_c, sin_c, batch, seq, tq=256):
    kern = functools.partial(_attn_c_kernel, tq=tq)
    const = lambda shape: pl.BlockSpec(shape, lambda b: (0,) * len(shape))
    return pl.pallas_call(
        kern,
        out_shape=jax.ShapeDtypeStruct((batch * seq, MIX_W), jnp.bfloat16),
        grid=(batch,),
        in_specs=[pl.BlockSpec((seq, 512), lambda b: (b, COL_C // 512)),
                  const(wuq.shape), const(wuk.shape), const(wuv.shape),
                  const(gq.shape), const(gkv.shape), const(cos_c.shape), const(sin_c.shape)],
        out_specs=pl.BlockSpec((seq, MIX_W), lambda b: (b, 0)),
        scratch_shapes=[pltpu.VMEM((seq, C_HEADS * LANES), jnp.bfloat16),
                        pltpu.VMEM((seq, C_HEADS * LANES), jnp.bfloat16),
                        pltpu.VMEM((seq, C_HEADS * C_V), jnp.bfloat16)],
        compiler_params=_cparams(("parallel",)),
    )(h, wuq, wuk, wuv, gq, gkv, cos_c, sin_c)


CONV_PAD = 16
CONV_ROWS = 128


def _conv_kernel(hd_ref, w_ref, cb_ref, g_ref, b_ref, o_ref, up_ref):
    s_len = hd_ref.shape[0]
    a = hd_ref[:, 0:D_CH].astype(jnp.float32)
    gate = hd_ref[:, D_CH:2 * D_CH].astype(jnp.float32)
    zeros = jnp.zeros((CONV_PAD, D_CH), jnp.float32)
    up_ref[0:CONV_PAD, :] = zeros
    up_ref[CONV_PAD + s_len:2 * CONV_PAD + s_len, :] = zeros
    up_ref[CONV_PAD:CONV_PAD + s_len, :] = a * _sigmoid(gate)
    shift = CONV_PAD - CONV_W // 2
    for c in range(s_len // CONV_ROWS):
        acc = jnp.zeros((CONV_ROWS, D_CH), jnp.float32)
        for j in range(CONV_W):
            start = c * CONV_ROWS + j + shift
            acc = acc + up_ref[start:start + CONV_ROWS, :] * w_ref[j:j + 1, :]
        u = acc + cb_ref[...]
        y = _layer_norm_rows(u, g_ref[...], b_ref[...])
        o_ref[c * CONV_ROWS:(c + 1) * CONV_ROWS, :] = (y * _sigmoid(y)).astype(o_ref.dtype)


def _conv_branch(h, w, cb, g, b, batch, seq):
    const = lambda shape: pl.BlockSpec(shape, lambda i: (0,) * len(shape))
    return pl.pallas_call(
        _conv_kernel,
        out_shape=jax.ShapeDtypeStruct((batch * seq, MIX_W), jnp.bfloat16),
        grid=(batch,),
        in_specs=[pl.BlockSpec((seq, 512), lambda i: (i, COL_D // 512)),
                  const(w.shape), const(cb.shape), const(g.shape), const(b.shape)],
        out_specs=pl.BlockSpec((seq, MIX_W), lambda i: (i, 0)),
        scratch_shapes=[pltpu.VMEM((seq + 2 * CONV_PAD, D_CH), jnp.float32)],
        compiler_params=_cparams(("parallel",)),
    )(h, w, cb, g, b)


def _merge_kernel(gl_ref, ya_ref, yb_ref, yc_ref, yd_ref, x_ref, wb_ref, wo_ref, g_ref, b_ref,
                  of_ref, ob_ref):
    merged = None
    for n, y_ref in enumerate((ya_ref, yb_ref, yc_ref, yd_ref)):
        gate = _sigmoid(gl_ref[:, n * D_MODEL:(n + 1) * D_MODEL].astype(jnp.float32))
        term = gate * _dot(y_ref[...], wb_ref[n])
        merged = term if merged is None else merged + term
    mix = _dot(merged.astype(jnp.bfloat16), wo_ref[...])
    y = _layer_norm_rows(ALPHA * x_ref[...] + mix, g_ref[...], b_ref[...])
    of_ref[...] = y
    ob_ref[...] = y.astype(jnp.bfloat16)


def _merge(h, ya, yb, yc, yd, x, wb, wo, g, b, tm=512):
    t, d = x.shape
    row = lambda w: pl.BlockSpec((tm, w), lambda i: (i, 0))
    const = lambda shape: pl.BlockSpec(shape, lambda i: (0,) * len(shape))
    return pl.pallas_call(
        _merge_kernel,
        out_shape=(jax.ShapeDtypeStruct((t, d), jnp.float32), jax.ShapeDtypeStruct((t, d), jnp.bfloat16)),
        grid=(t // tm,),
        in_specs=[pl.BlockSpec((tm, N_BRANCH * d), lambda i: (i, COL_GATES)),
                  row(MIX_W), row(MIX_W), row(MIX_W), row(MIX_W), row(d),
                  const(wb.shape), const(wo.shape), const((1, d)), const((1, d))],
        out_specs=(row(d), row(d)),
        compiler_params=_cparams(("parallel",)),
    )(h, ya, yb, yc, yd, x, wb, wo, g.reshape(1, d), b.reshape(1, d))


def _ffn_kernel(xb_ref, xf_ref, w1_ref, w3_ref, w2_ref, g_ref, b_ref, of_ref, ob_ref, acc_ref):
    j = pl.program_id(1)
    x = xb_ref[...]
    a = _dot(x, w1_ref[...])
    hid = (a * _sigmoid(a) * _dot(x, w3_ref[...])).astype(jnp.bfloat16)
    part = _dot(hid, w2_ref[...])

    @pl.when(j == 0)
    def _():
        acc_ref[...] = part

    @pl.when(j > 0)
    def _():
        acc_ref[...] += part

    @pl.when(j == pl.num_programs(1) - 1)
    def _():
        y = _layer_norm_rows(ALPHA * xf_ref[...] + acc_ref[...], g_ref[...], b_ref[...])
        of_ref[...] = y
        ob_ref[...] = y.astype(jnp.bfloat16)


def _ffn(xb, xf, w1, w3, w2, g, b, tm=512, tf=1408):
    t, d = xf.shape
    f = w1.shape[1]
    row = pl.BlockSpec((tm, d), lambda i, j: (i, 0))
    vec = pl.BlockSpec((1, d), lambda i, j: (0, 0))
    return pl.pallas_call(
        _ffn_kernel,
        out_shape=(jax.ShapeDtypeStruct((t, d), jnp.float32), jax.ShapeDtypeStruct((t, d), jnp.bfloat16)),
        grid=(t // tm, f // tf),
        in_specs=[row, row,
                  pl.BlockSpec((d, tf), lambda i, j: (0, j)),
                  pl.BlockSpec((d, tf), lambda i, j: (0, j)),
                  pl.BlockSpec((tf, d), lambda i, j: (j, 0)),
                  vec, vec],
        out_specs=(row, row),
        scratch_shapes=[pltpu.VMEM((tm, d), jnp.float32)],
        compiler_params=_cparams(("parallel", "arbitrary")),
    )(xb, xf, w1, w3, w2, g.reshape(1, d), b.reshape(1, d))


def _router_kernel(x_ref, r_ref, tri_ref, oi_ref, ow_ref, cnt_ref, carry_ref):
    i = pl.program_id(0)

    @pl.when(i == 0)
    def _():
        carry_ref[...] = jnp.zeros_like(carry_ref)

    x = x_ref[...]
    r = r_ref[...]
    xh = x.astype(jnp.bfloat16)
    xl = (x - xh.astype(jnp.float32)).astype(jnp.bfloat16)
    rh = r.astype(jnp.bfloat16)
    rl = (r - rh.astype(jnp.float32)).astype(jnp.bfloat16)
    logits = (_dot(xh, rh) + _dot(xh, rl)) + (_dot(xl, rh) + _dot(xl, rl))
    tm = x.shape[0]
    lane = lax.broadcasted_iota(jnp.int32, (tm, LANES), 1)
    logits = jnp.where(lane < N_EXPERTS, logits, NEG_BIG)
    m1 = jnp.max(logits, axis=-1, keepdims=True)
    i1 = jnp.min(jnp.where(logits == m1, lane, LANES), axis=-1, keepdims=True)
    rest = jnp.where(lane == i1, NEG_BIG, logits)
    m2 = jnp.max(rest, axis=-1, keepdims=True)
    i2 = jnp.min(jnp.where(rest == m2, lane, LANES), axis=-1, keepdims=True)
    e = jnp.exp(m2 - m1)
    w1 = 1.0 / (1.0 + e)
    w2 = e / (1.0 + e)
    sel1 = lane == i1
    sel2 = lane == i2
    member = jnp.where(sel1 | sel2, 1.0, 0.0)
    before = _dot(tri_ref[...], member.astype(jnp.bfloat16)) + carry_ref[...]
    rank1 = jnp.sum(jnp.where(sel1, before, 0.0), axis=-1, keepdims=True).astype(jnp.int32)
    rank2 = jnp.sum(jnp.where(sel2, before, 0.0), axis=-1, keepdims=True).astype(jnp.int32)
    carry_ref[...] += jnp.sum(member, axis=0, keepdims=True)
    oi_ref[...] = jnp.where(lane == 0, i1, jnp.where(lane == 1, i2, jnp.where(lane == 2, rank1, rank2)))
    ow_ref[...] = jnp.where(lane == 0, w1, w2)
    cnt_ref[...] = carry_ref[...]


def _router(xf, router, tm=512):
    t, d = xf.shape
    r_pad = jnp.zeros((d, LANES), jnp.float32).at[:, :N_EXPERTS].set(router.astype(jnp.float32))
    tri = jnp.asarray(np.tril(np.ones((tm, tm), np.float32), -1), jnp.bfloat16)
    return pl.pallas_call(
        _router_kernel,
        out_shape=(jax.ShapeDtypeStruct((t, LANES), jnp.int32),
                   jax.ShapeDtypeStruct((t, LANES), jnp.float32),
                   jax.ShapeDtypeStruct((1, LANES), jnp.float32)),
        grid=(t // tm,),
        in_specs=[pl.BlockSpec((tm, d), lambda i: (i, 0)),
                  pl.BlockSpec((d, LANES), lambda i: (0, 0)),
                  pl.BlockSpec((tm, tm), lambda i: (0, 0))],
        out_specs=(pl.BlockSpec((tm, LANES), lambda i: (i, 0)),
                   pl.BlockSpec((tm, LANES), lambda i: (i, 0)),
                   pl.BlockSpec((1, LANES), lambda i: (0, 0))),
        scratch_shapes=[pltpu.VMEM((1, LANES), jnp.float32)],
        compiler_params=_cparams(("arbitrary",)),
    )(xf, r_pad, tri)


SC_CORES = 2
SC_SUBCORES = 16
SC_WINDOW = 32


def _sc_gather_rows(table, idx):
    m = idx.shape[0]
    d = table.shape[1]
    n_workers = SC_CORES * SC_SUBCORES
    per_w = m // n_workers
    n_win = per_w // SC_WINDOW
    assert per_w * n_workers == m and n_win * SC_WINDOW == per_w
    mesh = plsc.VectorSubcoreMesh(core_axis_name="c", subcore_axis_name="s")

    @functools.partial(
        pl.kernel, mesh=mesh,
        out_type=jax.ShapeDtypeStruct((m, d), table.dtype),
        scratch_types=[pltpu.VMEM((per_w,), jnp.int32),
                       pltpu.VMEM((SC_WINDOW, d), table.dtype),
                       pltpu.SemaphoreType.DMA],
    )
    def gather(table_hbm, idx_hbm, out_hbm, idx_v, rows_v, sem):
        wid = lax.axis_index("s") * SC_CORES + lax.axis_index("c")
        base = wid * per_w
        pltpu.sync_copy(idx_hbm.at[pl.ds(base, per_w)], idx_v)

        @pl.loop(0, n_win)
        def _(w):
            off = pl.multiple_of(w * SC_WINDOW, SC_WINDOW)
            pltpu.async_copy(table_hbm.at[idx_v.at[pl.ds(off, SC_WINDOW)]], rows_v, sem).wait()
            pltpu.sync_copy(rows_v, out_hbm.at[pl.ds(base + off, SC_WINDOW)])

    return gather(table, idx)


def _expert_kernel(te_ref, nu_ref, x_ref, w1_ref, w3_ref, w2_ref, o_ref):
    i = pl.program_id(0)
    j = pl.program_id(1)
    used = i < nu_ref[0]

    @pl.when(used)
    def _():
        x = x_ref[...].astype(jnp.bfloat16)
        a = _dot(x, w1_ref[0])
        hid = (a * _sigmoid(a) * _dot(x, w3_ref[0])).astype(jnp.bfloat16)
        part = _dot(hid, w2_ref[0])

        @pl.when(j == 0)
        def _():
            o_ref[...] = part

        @pl.when(j > 0)
        def _():
            o_ref[...] += part

    @pl.when(jnp.logical_not(used) & (j == 0))
    def _():
        o_ref[...] = jnp.zeros_like(o_ref)


def _experts(tile_expert, n_used, xs, w1, w3, w2):
    n_rows, d = xs.shape
    f = w1.shape[2]
    nj = f // MOE_FCHUNK

    def chunk(i, j, nu):
        return jnp.where(i < nu[0], j, nj - 1)

    return pl.pallas_call(
        _expert_kernel,
        out_shape=jax.ShapeDtypeStruct((n_rows, d), jnp.float32),
        grid_spec=pltpu.PrefetchScalarGridSpec(
            num_scalar_prefetch=2,
            grid=(n_rows // MOE_TILE, nj),
            in_specs=[pl.BlockSpec((MOE_TILE, d), lambda i, j, te, nu: (i, 0)),
                      pl.BlockSpec((1, d, MOE_FCHUNK), lambda i, j, te, nu: (te[i], 0, chunk(i, j, nu))),
                      pl.BlockSpec((1, d, MOE_FCHUNK), lambda i, j, te, nu: (te[i], 0, chunk(i, j, nu))),
                      pl.BlockSpec((1, MOE_FCHUNK, d), lambda i, j, te, nu: (te[i], chunk(i, j, nu), 0))],
            out_specs=pl.BlockSpec((MOE_TILE, d), lambda i, j, te, nu: (i, 0)),
        ),
        compiler_params=_cparams(("arbitrary", "arbitrary")),
    )(tile_expert, n_used, xs, w1, w3, w2)


def _combine_kernel(ya_ref, yb_ref, x_ref, w_ref, g_ref, b_ref, o_ref):
    w = w_ref[...]
    f = w[:, 0:1] * ya_ref[...] + w[:, 1:2] * yb_ref[...]
    o_ref[...] = _layer_norm_rows(ALPHA * x_ref[...] + f, g_ref[...], b_ref[...])


def _combine(yg, xf, wts, g, b, tm=512):
    t, d = xf.shape
    nt = t // tm
    return pl.pallas_call(
        _combine_kernel,
        out_shape=jax.ShapeDtypeStruct((t, d), jnp.float32),
        grid=(nt,),
        in_specs=[pl.BlockSpec((tm, d), lambda i: (i, 0)),
                  pl.BlockSpec((tm, d), lambda i: (i + nt, 0)),
                  pl.BlockSpec((tm, d), lambda i: (i, 0)),
                  pl.BlockSpec((tm, LANES), lambda i: (i, 0)),
                  pl.BlockSpec((1, d), lambda i: (0, 0)),
                  pl.BlockSpec((1, d), lambda i: (0, 0))],
        out_specs=pl.BlockSpec((tm, d), lambda i: (i, 0)),
        compiler_params=_cparams(("parallel",)),
    )(yg, yg, xf, wts, g.reshape(1, d), b.reshape(1, d))


def _moe_ffn(xf, router, w1, w3, w2, g, b):
    t, d = xf.shape
    oi, ow, cnt = _router(xf, router)
    e1, e2, rank1, rank2 = oi[:, 0], oi[:, 1], oi[:, 2], oi[:, 3]
    counts = cnt[0, :N_EXPERTS].astype(jnp.int32)
    tiles = (counts + MOE_TILE - 1) // MOE_TILE
    tile_end = jnp.cumsum(tiles)
    offs = (tile_end - tiles) * MOE_TILE
    dest1 = offs[e1] + rank1
    dest2 = offs[e2] + rank2
    n_rows = TOP_K * t + N_EXPERTS * MOE_TILE
    n_tiles = n_rows // MOE_TILE
    tok = jnp.arange(t, dtype=jnp.int32)
    src = jnp.zeros((n_rows,), jnp.int32).at[dest1].set(tok).at[dest2].set(tok)
    n_used = tile_end[-1:].astype(jnp.int32)
    tile_expert = jnp.searchsorted(tile_end, jnp.arange(n_tiles, dtype=jnp.int32), side="right")
    last_expert = tile_expert[jnp.maximum(n_used[0] - 1, 0)]
    tile_expert = jnp.where(jnp.arange(n_tiles) < n_used[0], tile_expert, last_expert).astype(jnp.int32)
    xs = _sc_gather_rows(xf, src)
    y = _experts(tile_expert, n_used, xs, w1, w3, w2)
    yg = _sc_gather_rows(y, jnp.concatenate([dest1, dest2]))
    return _combine(yg, xf, ow, g, b)


def _prep_layer(l, p, in_cols):
    bf = jnp.bfloat16
    out = {}
    out["w_in"] = _take_cols(p["w_in"][l], in_cols).astype(bf)
    deint = np.concatenate([np.arange(0, HEAD_DIM, 2), np.arange(1, HEAD_DIM, 2)])
    out["gq_a"] = jnp.tile(p["a_q_norm"][l][deint], 2).reshape(1, LANES)
    out["gk_a"] = jnp.tile(p["a_k_norm"][l][deint], 2).reshape(1, LANES)
    per_head = C_NOPE + C_ROPE
    uq_cols = []
    for h in range(C_HEADS):
        base = h * per_head
        uq_cols += [base + np.arange(C_NOPE), base + C_NOPE + np.arange(0, C_ROPE, 2),
                    base + C_NOPE + np.arange(1, C_ROPE, 2), np.full(32, -1)]
    wuq = _take_cols(p["c_w_uq"][l], np.concatenate(uq_cols))
    out["wuq"] = jnp.concatenate([wuq, jnp.zeros((64, wuq.shape[1]), wuq.dtype)], axis=0).astype(bf)
    uk_cols, uv_cols = [], []
    for h in range(C_HEADS):
        base = h * (C_NOPE + C_V)
        uk_cols += [base + np.arange(C_NOPE), np.full(64, -1)]
        uv_cols += [base + C_NOPE + np.arange(C_V)]
    out["wuk"] = _take_cols(p["c_w_ukv"][l], np.concatenate(uk_cols)).astype(bf)
    out["wuv"] = _take_cols(p["c_w_ukv"][l], np.concatenate(uv_cols)).astype(bf)
    out["gq_c"] = jnp.concatenate([p["c_q_norm"][l], jnp.zeros((64,), jnp.float32)]).reshape(1, 256)
    out["gkv_c"] = p["c_kv_norm"][l].reshape(1, C_KV_RANK)
    out["conv_w"] = jnp.concatenate([p["d_conv_w"][l][:, 0, :], jnp.zeros((1, D_CH), jnp.float32)], axis=0)
    out["conv_b"] = p["d_conv_b"][l].reshape(1, D_CH)
    out["d_ln_g"] = p["d_ln_g"][l].reshape(1, D_CH)
    out["d_ln_b"] = p["d_ln_b"][l].reshape(1, D_CH)
    a_rows = np.concatenate([(kvh * 2 + g) * HEAD_DIM + np.arange(HEAD_DIM)
                             for g in range(2) for kvh in range(A_KV_HEADS)])
    wb = p["w_branch"][l]
    out["w_branch"] = jnp.stack([wb[0][a_rows], wb[1], wb[2], wb[3]], axis=0).astype(bf)
    out["w_out"] = p["w_out"][l].astype(bf)
    return out


def kernel(x, ln_in_g, ln_in_b, w_in, a_q_norm, a_k_norm, b_rpb, c_q_norm, c_kv_norm, c_w_uq, c_w_ukv,
           d_conv_w, d_conv_b, d_ln_g, d_ln_b, w_branch, w_out, ln_mix_g, ln_mix_b,
           ffn_w1, ffn_w3, ffn_w2, moe_router, moe_w1, moe_w3, moe_w2, ln_ffn_g, ln_ffn_b):
    batch, seq, d = x.shape
    t = batch * seq
    bf = jnp.bfloat16
    params = dict(w_in=w_in, a_q_norm=a_q_norm, a_k_norm=a_k_norm, c_q_norm=c_q_norm, c_kv_norm=c_kv_norm,
                  c_w_uq=c_w_uq, c_w_ukv=c_w_ukv, d_conv_w=d_conv_w, d_conv_b=d_conv_b, d_ln_g=d_ln_g,
                  d_ln_b=d_ln_b, w_branch=w_branch, w_out=w_out)
    in_cols = _in_proj_columns()
    cos_a, sin_a, cos_c, sin_c = _rope_tables(seq)
    xf, xb = _input_ln(x.reshape(t, d), ln_in_g, ln_in_b)
    for l in range(DEPTH):
        p = _prep_layer(l, params, in_cols)
        h = _matmul(xb, p["w_in"], bf, tm=1024, tn=1280)
        ya = _attn_a(h, cos_a, sin_a, p["gq_a"], p["gk_a"], batch, seq)
        yb = _attn_b(h, _na_bias_tables(b_rpb[l], seq // GRID_W), batch, seq)
        yc = _attn_c(h, p["wuq"], p["wuk"], p["wuv"], p["gq_c"], p["gkv_c"], cos_c, sin_c, batch, seq)
        yd = _conv_branch(h, p["conv_w"], p["conv_b"], p["d_ln_g"], p["d_ln_b"], batch, seq)
        xf, xb = _merge(h, ya, yb, yc, yd, xf, p["w_branch"], p["w_out"], ln_mix_g[l], ln_mix_b[l])
        if l % 2 == 0:
            i = l // 2
            xf, xb = _ffn(xb, xf, ffn_w1[i].astype(bf), ffn_w3[i].astype(bf), ffn_w2[i].astype(bf),
                          ln_ffn_g[l], ln_ffn_b[l])
        else:
            i = l // 2
            xf = _moe_ffn(xf, moe_router[i], moe_w1[i].astype(bf), moe_w3[i].astype(bf), moe_w2[i].astype(bf),
                          ln_ffn_g[l], ln_ffn_b[l])
            xb = xf.astype(bf)
    return xf.reshape(batch, seq, d)
```

```python
import functools

import numpy as np
import jax
import jax.numpy as jnp
from jax import lax
from jax.experimental import pallas as pl
from jax.experimental.pallas import tpu as pltpu
from jax.experimental.pallas import tpu_sc as plsc

D_MODEL = 1024
DEPTH = 2
GRID_W = 64
HEAD_DIM = 64
ROPE_THETA = 10000.0
RMS_EPS = 1e-6
LN_EPS = 1e-5
A_HEADS = 4
A_KV_HEADS = 2
B_HEADS = 4
NA_WIN_H = 8
NA_WIN_W = 16
C_HEADS = 4
C_NOPE = 64
C_ROPE = 32
C_V = 64
C_Q_RANK = 192
C_KV_RANK = 128
D_CH = 256
CONV_W = 31
N_BRANCH = 4
MIX_W = 256
D_FF = 2816
N_EXPERTS = 8
TOP_K = 2
D_FF_EXPERT = 3584
ALPHA = (2 * DEPTH) ** 0.25

LANES = 128
VMEM_LIMIT = 48 * 1024 * 1024

NEG_BIG = -1e30

COL_GATES = 0
COL_AQ = 4096
COL_AKV = 4352
COL_C = 4608
COL_D = 5120
COL_BQ = 5632
COL_BK = 5888
COL_BV = 6144
D_IN_PAD = 6400

NA_QROWS = 4
NA_KROWS = 12

MOE_TILE = 1024
MOE_SUB = 512
MOE_FCHUNK = 896
MOE_VMEM_LIMIT = 56 * 1024 * 1024


def _in_proj_columns():
    o_aq, o_ak, o_av = 0, 256, 384
    o_bq, o_bk, o_bv = 512, 768, 1024
    o_cq, o_ckv, o_kpe = 1280, 1472, 1600
    o_d, o_g = 1632, 2144
    deint = np.concatenate([np.arange(0, HEAD_DIM, 2), np.arange(1, HEAD_DIM, 2)])
    cols = [o_g + np.arange(N_BRANCH * D_MODEL)]
    for g in range(2):
        for kvh in range(A_KV_HEADS):
            cols.append(o_aq + (kvh * 2 + g) * HEAD_DIM + deint)
    for kvh in range(A_KV_HEADS):
        cols.append(o_ak + kvh * HEAD_DIM + deint)
    cols.append(o_av + np.arange(A_KV_HEADS * HEAD_DIM))
    cols.append(o_cq + np.arange(C_Q_RANK))
    cols.append(np.full(64, -1))
    cols.append(o_ckv + np.arange(C_KV_RANK))
    cols.append(np.full(64, -1))
    cols.append(o_kpe + np.arange(0, C_ROPE, 2))
    cols.append(o_kpe + np.arange(1, C_ROPE, 2))
    cols.append(np.full(32, -1))
    cols.append(o_d + np.arange(2 * D_CH))
    cols.append(o_bq + np.arange(3 * B_HEADS * HEAD_DIM))
    cols = np.concatenate(cols)
    assert cols.shape[0] == D_IN_PAD
    return cols


def _take_cols(w, cols):
    cols = np.asarray(cols)
    parts, i, n = [], 0, len(cols)
    while i < n:
        j = i + 1
        if cols[i] < 0:
            while j < n and cols[j] < 0:
                j += 1
            parts.append(jnp.zeros((w.shape[0], j - i), w.dtype))
        else:
            step = int(cols[j] - cols[i]) if j < n and cols[j] > cols[i] else 1
            while j < n and cols[j] == cols[j - 1] + step:
                j += 1
            c0, n_run = int(cols[i]), j - i
            if step == 1:
                parts.append(w[:, c0:c0 + n_run])
            else:
                assert step == 2
                base = c0 - c0 % 2
                parts.append(w[:, base:base + 2 * n_run].reshape(w.shape[0], n_run, 2)[:, :, c0 % 2])
        i = j
    return jnp.concatenate(parts, axis=1)


def _rope_tables(seq):
    t = np.arange(seq)
    row = (t // GRID_W).astype(np.float32)
    col = (t % GRID_W).astype(np.float32)

    def angles(rot_dim):
        n_freq = rot_dim // 4
        inv = jnp.asarray(ROPE_THETA, jnp.float32) ** (-jnp.arange(n_freq, dtype=jnp.float32) / n_freq)
        return jnp.concatenate([jnp.asarray(row)[:, None] * inv, jnp.asarray(col)[:, None] * inv], axis=-1)

    ang_a = angles(HEAD_DIM)
    ca, sa = jnp.cos(ang_a), jnp.sin(ang_a)
    cos_a = jnp.concatenate([ca, ca, ca, ca], axis=-1)
    sin_a = jnp.concatenate([-sa, sa, -sa, sa], axis=-1)
    ang_c = angles(C_ROPE)
    cc, sc = jnp.cos(ang_c), jnp.sin(ang_c)
    one = jnp.ones((seq, 64), jnp.float32)
    zero = jnp.zeros((seq, 64), jnp.float32)
    cos_c = jnp.concatenate([one, cc, cc, one[:, :32]], axis=-1)
    sin_c = jnp.concatenate([zero, -sc, sc, zero[:, :32]], axis=-1)
    return cos_a, sin_a, cos_c, sin_c


def _na_bias_tables(rpb, rows):
    wh = min(NA_WIN_H, rows)
    n_dr, n_dc = 2 * NA_WIN_H - 1, 2 * NA_WIN_W - 1
    qc = np.arange(GRID_W)[:, None]
    kc = np.arange(GRID_W)[None, :]
    col_start = np.clip(qc - NA_WIN_W // 2, 0, GRID_W - NA_WIN_W)
    ok_c = (kc >= col_start) & (kc < col_start + NA_WIN_W)
    dc = np.clip(kc - qc + (NA_WIN_W - 1), 0, n_dc - 1)
    oh_c = np.eye(n_dc, dtype=np.float32)[dc.reshape(-1)]
    oh_r, ok_r = [], []
    for r0 in (0, NA_QROWS, rows - NA_QROWS):
        ws = int(np.clip(r0 - 4, 0, rows - NA_KROWS))
        qr = r0 + np.arange(NA_QROWS)[:, None]
        kr = ws + np.arange(NA_KROWS)[None, :]
        row_start = np.clip(qr - wh // 2, 0, rows - wh)
        ok_r.append((kr >= row_start) & (kr < row_start + wh))
        dr = np.clip(kr - qr + (NA_WIN_H - 1), 0, n_dr - 1)
        oh_r.append(np.eye(n_dr, dtype=np.float32)[dr.reshape(-1)])
    oh_r = np.concatenate(oh_r, axis=0)
    ok = np.stack(ok_r)[:, :, None, :, None] & ok_c[None, None, :, None, :]
    ok = ok.reshape(3, 1, NA_QROWS * GRID_W, NA_KROWS * GRID_W)
    hi = lax.Precision.HIGHEST
    t1 = jnp.einsum("ma,hab->hmb", jnp.asarray(oh_r), rpb.astype(jnp.float32), precision=hi)
    t2 = jnp.einsum("hmb,nb->hmn", t1, jnp.asarray(oh_c), precision=hi)
    nh = rpb.shape[0]
    t2 = t2.reshape(nh, 3, NA_QROWS, NA_KROWS, GRID_W, GRID_W).transpose(1, 0, 2, 4, 3, 5)
    bias = t2.reshape(3, nh, NA_QROWS * GRID_W, NA_KROWS * GRID_W)
    return jnp.where(jnp.asarray(ok), bias, NEG_BIG)


def _cparams(sem):
    return pltpu.CompilerParams(dimension_semantics=sem, vmem_limit_bytes=VMEM_LIMIT)


def _layer_norm_rows(z, g, b):
    mu = jnp.mean(z, axis=-1, keepdims=True)
    zc = z - mu
    var = jnp.mean(zc * zc, axis=-1, keepdims=True)
    return zc * lax.rsqrt(var + LN_EPS) * g + b


def _sigmoid(x):
    return 0.5 * jnp.tanh(0.5 * x) + 0.5


def _dot(a, b):
    return jnp.dot(a, b, preferred_element_type=jnp.float32)


def _dot_nt(a, b):
    return lax.dot_general(a, b, (((1,), (1,)), ((), ())), preferred_element_type=jnp.float32)


def _ln_kernel(x_ref, g_ref, b_ref, of_ref, ob_ref):
    y = _layer_norm_rows(x_ref[...], g_ref[...], b_ref[...])
    of_ref[...] = y
    ob_ref[...] = y.astype(jnp.bfloat16)


def _input_ln(x, g, b, tm=1024):
    t, d = x.shape
    return pl.pallas_call(
        _ln_kernel,
        out_shape=(jax.ShapeDtypeStruct((t, d), jnp.float32), jax.ShapeDtypeStruct((t, d), jnp.bfloat16)),
        grid=(t // tm,),
        in_specs=[pl.BlockSpec((tm, d), lambda i: (i, 0)),
                  pl.BlockSpec((1, d), lambda i: (0, 0)),
                  pl.BlockSpec((1, d), lambda i: (0, 0))],
        out_specs=(pl.BlockSpec((tm, d), lambda i: (i, 0)), pl.BlockSpec((tm, d), lambda i: (i, 0))),
        compiler_params=_cparams(("parallel",)),
    )(x, g.reshape(1, d), b.reshape(1, d))


def _mm_kernel(a_ref, b_ref, o_ref):
    o_ref[...] = _dot(a_ref[...], b_ref[...]).astype(o_ref.dtype)


def _matmul(a, b, out_dtype, tm, tn):
    m, k = a.shape
    n = b.shape[1]
    return pl.pallas_call(
        _mm_kernel,
        out_shape=jax.ShapeDtypeStruct((m, n), out_dtype),
        grid=(m // tm, n // tn),
        in_specs=[pl.BlockSpec((tm, k), lambda i, j: (i, 0)),
                  pl.BlockSpec((k, tn), lambda i, j: (0, j))],
        out_specs=pl.BlockSpec((tm, tn), lambda i, j: (i, j)),
        compiler_params=_cparams(("parallel", "parallel")),
    )(a, b)


def _softmax_pv(s, v):
    m = jnp.max(s, axis=-1, keepdims=True)
    p = jnp.exp(s - m)
    l = jnp.sum(p, axis=-1, keepdims=True)
    o = _dot(p.astype(jnp.bfloat16), v)
    return o * (1.0 / l)


def _rotate_pairs(x, cos, sin, half, lane):
    w = x.shape[-1]
    first = (lane % (2 * half)) < half
    partner = jnp.where(first, pltpu.roll(x, w - half, 1), pltpu.roll(x, half, 1))
    return x * cos + partner * sin


def _attn_a_kernel(q_ref, kv_ref, cos_ref, sin_ref, gq_ref, gk_ref, o_ref, qs_ref, ks_ref, *, tq):
    s_len = q_ref.shape[0]
    lane = lax.broadcasted_iota(jnp.int32, (s_len, LANES), 1)
    lo = lane < HEAD_DIM
    cos = cos_ref[...]
    sin = sin_ref[...]

    def prep(x, g):
        x2 = x * x
        ss_lo = jnp.sum(jnp.where(lo, x2, 0.0), axis=-1, keepdims=True)
        ss_hi = jnp.sum(jnp.where(lo, 0.0, x2), axis=-1, keepdims=True)
        ms = jnp.where(lo, ss_lo, ss_hi) * (1.0 / HEAD_DIM)
        xn = x * lax.rsqrt(ms + RMS_EPS) * g
        return _rotate_pairs(xn, cos, sin, HEAD_DIM // 2, lane)

    ks_ref[...] = prep(kv_ref[:, 0:LANES].astype(jnp.float32), gk_ref[...]).astype(jnp.bfloat16)
    scale = HEAD_DIM ** -0.5
    for g in range(2):
        qg = prep(q_ref[:, g * LANES:(g + 1) * LANES].astype(jnp.float32), gq_ref[...]) * scale
        qs_ref[:, g * LANES:(g + 1) * LANES] = qg.astype(jnp.bfloat16)

    lane_q = lax.broadcasted_iota(jnp.int32, (tq, LANES), 1)
    lo_q = lane_q < HEAD_DIM

    def body(i, carry):
        r = pl.multiple_of(i * tq, tq)
        k = ks_ref[...]
        v = kv_ref[:, LANES:2 * LANES]
        for g in range(2):
            qt = qs_ref[pl.ds(r, tq), g * LANES:(g + 1) * LANES]
            zero = jnp.zeros_like(qt)
            o_lo = _softmax_pv(_dot_nt(jnp.where(lo_q, qt, zero), k), v)
            o_hi = _softmax_pv(_dot_nt(jnp.where(lo_q, zero, qt), k), v)
            o_ref[pl.ds(r, tq), g * LANES:(g + 1) * LANES] = jnp.where(lo_q, o_lo, o_hi).astype(o_ref.dtype)
        return carry

    lax.fori_loop(0, s_len // tq, body, 0)


def _attn_a(h, cos_a, sin_a, gq, gk, batch, seq, tq=256):
    kern = functools.partial(_attn_a_kernel, tq=tq)
    return pl.pallas_call(
        kern,
        out_shape=jax.ShapeDtypeStruct((batch * seq, MIX_W), jnp.bfloat16),
        grid=(batch,),
        in_specs=[pl.BlockSpec((seq, 256), lambda b: (b, COL_AQ // 256)),
                  pl.BlockSpec((seq, 256), lambda b: (b, COL_AKV // 256)),
                  pl.BlockSpec((seq, LANES), lambda b: (0, 0)),
                  pl.BlockSpec((seq, LANES), lambda b: (0, 0)),
                  pl.BlockSpec((1, LANES), lambda b: (0, 0)),
                  pl.BlockSpec((1, LANES), lambda b: (0, 0))],
        out_specs=pl.BlockSpec((seq, MIX_W), lambda b: (b, 0)),
        scratch_shapes=[pltpu.VMEM((seq, 256), jnp.bfloat16), pltpu.VMEM((seq, LANES), jnp.bfloat16)],
        compiler_params=_cparams(("parallel",)),
    )(h, h, cos_a, sin_a, gq, gk)


def _attn_b_kernel(q_ref, k_ref, v_ref, bias_ref, o_ref, *, rows):
    nq = NA_QROWS * GRID_W
    nk = NA_KROWS * GRID_W
    lane_q = lax.broadcasted_iota(jnp.int32, (nq, LANES), 1)
    lo_q = lane_q < HEAD_DIM
    n_groups = rows // NA_QROWS
    scale = HEAD_DIM ** -0.5

    def body(gi, carry):
        ws = jnp.clip(gi * NA_QROWS - 4, 0, rows - NA_KROWS)
        cls = jnp.where(gi == 0, 0, jnp.where(gi == n_groups - 1, 2, 1))
        qr = pl.multiple_of(gi * nq, nq)
        kr = pl.multiple_of(ws * GRID_W, GRID_W)
        for blk in range(B_HEADS // 2):
            sl = slice(blk * LANES, (blk + 1) * LANES)
            qt = q_ref[pl.ds(qr, nq), sl] * scale
            kw = k_ref[pl.ds(kr, nk), sl]
            vw = v_ref[pl.ds(kr, nk), sl]
            zero = jnp.zeros_like(qt)
            s_lo = _dot_nt(jnp.where(lo_q, qt, zero), kw) + bias_ref[cls, 2 * blk]
            s_hi = _dot_nt(jnp.where(lo_q, zero, qt), kw) + bias_ref[cls, 2 * blk + 1]
            o_lo = _softmax_pv(s_lo, vw)
            o_hi = _softmax_pv(s_hi, vw)
            o_ref[pl.ds(qr, nq), sl] = jnp.where(lo_q, o_lo, o_hi).astype(o_ref.dtype)
        return carry

    lax.fori_loop(0, n_groups, body, 0)


def _attn_b(h, bias, batch, seq):
    rows = seq // GRID_W
    kern = functools.partial(_attn_b_kernel, rows=rows)
    return pl.pallas_call(
        kern,
        out_shape=jax.ShapeDtypeStruct((batch * seq, MIX_W), jnp.bfloat16),
        grid=(batch,),
        in_specs=[pl.BlockSpec((seq, 256), lambda b: (b, COL_BQ // 256)),
                  pl.BlockSpec((seq, 256), lambda b: (b, COL_BK // 256)),
                  pl.BlockSpec((seq, 256), lambda b: (b, COL_BV // 256)),
                  pl.BlockSpec(bias.shape, lambda b: (0, 0, 0, 0))],
        out_specs=pl.BlockSpec((seq, MIX_W), lambda b: (b, 0)),
        compiler_params=_cparams(("parallel",)),
    )(h, h, h, bias)


def _attn_c_kernel(hc_ref, wuq_ref, wuk_ref, wuv_ref, gq_ref, gkv_ref, cos_ref, sin_ref, o_ref,
                   qs_ref, ks_ref, vs_ref, *, tq):
    s_len = hc_ref.shape[0]
    lane = lax.broadcasted_iota(jnp.int32, (s_len, LANES), 1)
    cos = cos_ref[...]
    sin = sin_ref[...]

    cq = hc_ref[:, 0:256].astype(jnp.float32)
    ms = jnp.sum(cq * cq, axis=-1, keepdims=True) * (1.0 / C_Q_RANK)
    cqn = (cq * lax.rsqrt(ms + RMS_EPS) * gq_ref[...]).astype(jnp.bfloat16)
    ckv = hc_ref[:, 256:384].astype(jnp.float32)
    ms = jnp.mean(ckv * ckv, axis=-1, keepdims=True)
    kvn = (ckv * lax.rsqrt(ms + RMS_EPS) * gkv_ref[...]).astype(jnp.bfloat16)
    kpe = _rotate_pairs(hc_ref[:, 384:512].astype(jnp.float32), cos, sin, C_ROPE // 2, lane)
    vs_ref[...] = _dot(kvn, wuv_ref[...]).astype(jnp.bfloat16)
    scale = (C_NOPE + C_ROPE) ** -0.5
    for h in range(C_HEADS):
        sl = slice(h * LANES, (h + 1) * LANES)
        qh = _rotate_pairs(_dot(cqn, wuq_ref[:, sl]), cos, sin, C_ROPE // 2, lane) * scale
        qs_ref[:, sl] = qh.astype(jnp.bfloat16)
        ks_ref[:, sl] = (_dot(kvn, wuk_ref[:, sl]) + kpe).astype(jnp.bfloat16)

    lane_q = lax.broadcasted_iota(jnp.int32, (tq, LANES), 1)
    lo_q = lane_q < C_V

    def body(i, carry):
        r = pl.multiple_of(i * tq, tq)
        for blk in range(C_HEADS // 2):
            vb = vs_ref[:, blk * LANES:(blk + 1) * LANES]
            outs = []
            for par in range(2):
                sl = slice((2 * blk + par) * LANES, (2 * blk + par + 1) * LANES)
                outs.append(_softmax_pv(_dot_nt(qs_ref[pl.ds(r, tq), sl], ks_ref[:, sl]), vb))
            o_ref[pl.ds(r, tq), blk * LANES:(blk + 1) * LANES] = jnp.where(lo_q, outs[0], outs[1]).astype(o_ref.dtype)
        return carry

    lax.fori_loop(0, s_len // tq, body, 0)


def _attn_c(h, wuq, wuk, wuv, gq, gkv, cos_c, sin_c, batch, seq, tq=256):
    kern = functools.partial(_attn_c_kernel, tq=tq)
    const = lambda shape: pl.BlockSpec(shape, lambda b: (0,) * len(shape))
    return pl.pallas_call(
        kern,
        out_shape=jax.ShapeDtypeStruct((batch * seq, MIX_W), jnp.bfloat16),
        grid=(batch,),
        in_specs=[pl.BlockSpec((seq, 512), lambda b: (b, COL_C // 512)),
                  const(wuq.shape), const(wuk.shape), const(wuv.shape),
                  const(gq.shape), const(gkv.shape), const(cos_c.shape), const(sin_c.shape)],
        out_specs=pl.BlockSpec((seq, MIX_W), lambda b: (b, 0)),
        scratch_shapes=[pltpu.VMEM((seq, C_HEADS * LANES), jnp.bfloat16),
                        pltpu.VMEM((seq, C_HEADS * LANES), jnp.bfloat16),
                        pltpu.VMEM((seq, C_HEADS * C_V), jnp.bfloat16)],
        compiler_params=_cparams(("parallel",)),
    )(h, wuq, wuk, wuv, gq, gkv, cos_c, sin_c)


CONV_PAD = 16
CONV_ROWS = 128


def _conv_kernel(hd_ref, w_ref, cb_ref, g_ref, b_ref, o_ref, up_ref):
    s_len = hd_ref.shape[0]
    a = hd_ref[:, 0:D_CH].astype(jnp.float32)
    gate = hd_ref[:, D_CH:2 * D_CH].astype(jnp.float32)
    zeros = jnp.zeros((CONV_PAD, D_CH), jnp.float32)
    up_ref[0:CONV_PAD, :] = zeros
    up_ref[CONV_PAD + s_len:2 * CONV_PAD + s_len, :] = zeros
    up_ref[CONV_PAD:CONV_PAD + s_len, :] = a * _sigmoid(gate)
    shift = CONV_PAD - CONV_W // 2
    for c in range(s_len // CONV_ROWS):
        acc = jnp.zeros((CONV_ROWS, D_CH), jnp.float32)
        for j in range(CONV_W):
            start = c * CONV_ROWS + j + shift
            acc = acc + up_ref[start:start + CONV_ROWS, :] * w_ref[j:j + 1, :]
        u = acc + cb_ref[...]
        y = _layer_norm_rows(u, g_ref[...], b_ref[...])
        o_ref[c * CONV_ROWS:(c + 1) * CONV_ROWS, :] = (y * _sigmoid(y)).astype(o_ref.dtype)


def _conv_branch(h, w, cb, g, b, batch, seq):
    const = lambda shape: pl.BlockSpec(shape, lambda i: (0,) * len(shape))
    return pl.pallas_call(
        _conv_kernel,
        out_shape=jax.ShapeDtypeStruct((batch * seq, MIX_W), jnp.bfloat16),
        grid=(batch,),
        in_specs=[pl.BlockSpec((seq, 512), lambda i: (i, COL_D // 512)),
                  const(w.shape), const(cb.shape), const(g.shape), const(b.shape)],
        out_specs=pl.BlockSpec((seq, MIX_W), lambda i: (i, 0)),
        scratch_shapes=[pltpu.VMEM((seq + 2 * CONV_PAD, D_CH), jnp.float32)],
        compiler_params=_cparams(("parallel",)),
    )(h, w, cb, g, b)


def _merge_kernel(gl_ref, ya_ref, yb_ref, yc_ref, yd_ref, x_ref, wb_ref, wo_ref, g_ref, b_ref,
                  of_ref, ob_ref):
    merged = None
    for n, y_ref in enumerate((ya_ref, yb_ref, yc_ref, yd_ref)):
        gate = _sigmoid(gl_ref[:, n * D_MODEL:(n + 1) * D_MODEL].astype(jnp.float32))
        term = gate * _dot(y_ref[...], wb_ref[n])
        merged = term if merged is None else merged + term
    mix = _dot(merged.astype(jnp.bfloat16), wo_ref[...])
    y = _layer_norm_rows(ALPHA * x_ref[...] + mix, g_ref[...], b_ref[...])
    of_ref[...] = y
    ob_ref[...] = y.astype(jnp.bfloat16)


def _merge(h, ya, yb, yc, yd, x, wb, wo, g, b, tm=512):
    t, d = x.shape
    row = lambda w: pl.BlockSpec((tm, w), lambda i: (i, 0))
    const = lambda shape: pl.BlockSpec(shape, lambda i: (0,) * len(shape))
    return pl.pallas_call(
        _merge_kernel,
        out_shape=(jax.ShapeDtypeStruct((t, d), jnp.float32), jax.ShapeDtypeStruct((t, d), jnp.bfloat16)),
        grid=(t // tm,),
        in_specs=[pl.BlockSpec((tm, N_BRANCH * d), lambda i: (i, COL_GATES)),
                  row(MIX_W), row(MIX_W), row(MIX_W), row(MIX_W), row(d),
                  const(wb.shape), const(wo.shape), const((1, d)), const((1, d))],
        out_specs=(row(d), row(d)),
        compiler_params=_cparams(("parallel",)),
    )(h, ya, yb, yc, yd, x, wb, wo, g.reshape(1, d), b.reshape(1, d))


def _ffn_kernel(xb_ref, xf_ref, w1_ref, w3_ref, w2_ref, g_ref, b_ref, of_ref, ob_ref, acc_ref):
    j = pl.program_id(1)
    x = xb_ref[...]
    a = _dot(x, w1_ref[...])
    hid = (a * _sigmoid(a) * _dot(x, w3_ref[...])).astype(jnp.bfloat16)
    part = _dot(hid, w2_ref[...])

    @pl.when(j == 0)
    def _():
        acc_ref[...] = part

    @pl.when(j > 0)
    def _():
        acc_ref[...] += part

    @pl.when(j == pl.num_programs(1) - 1)
    def _():
        y = _layer_norm_rows(ALPHA * xf_ref[...] + acc_ref[...], g_ref[...], b_ref[...])
        of_ref[...] = y
        ob_ref[...] = y.astype(jnp.bfloat16)


def _ffn(xb, xf, w1, w3, w2, g, b, tm=512, tf=1408):
    t, d = xf.shape
    f = w1.shape[1]
    row = pl.BlockSpec((tm, d), lambda i, j: (i, 0))
    vec = pl.BlockSpec((1, d), lambda i, j: (0, 0))
    return pl.pallas_call(
        _ffn_kernel,
        out_shape=(jax.ShapeDtypeStruct((t, d), jnp.float32), jax.ShapeDtypeStruct((t, d), jnp.bfloat16)),
        grid=(t // tm, f // tf),
        in_specs=[row, row,
                  pl.BlockSpec((d, tf), lambda i, j: (0, j)),
                  pl.BlockSpec((d, tf), lambda i, j: (0, j)),
                  pl.BlockSpec((tf, d), lambda i, j: (j, 0)),
                  vec, vec],
        out_specs=(row, row),
        scratch_shapes=[pltpu.VMEM((tm, d), jnp.float32)],
        compiler_params=_cparams(("parallel", "arbitrary")),
    )(xb, xf, w1, w3, w2, g.reshape(1, d), b.reshape(1, d))


def _router_kernel(x_ref, r_ref, tri_ref, oi_ref, ow_ref, cnt_ref, carry_ref):
    i = pl.program_id(0)

    @pl.when(i == 0)
    def _():
        carry_ref[...] = jnp.zeros_like(carry_ref)

    x = x_ref[...]
    r = r_ref[...]
    xh = x.astype(jnp.bfloat16)
    xl = (x - xh.astype(jnp.float32)).astype(jnp.bfloat16)
    rh = r.astype(jnp.bfloat16)
    rl = (r - rh.astype(jnp.float32)).astype(jnp.bfloat16)
    logits = (_dot(xh, rh) + _dot(xh, rl)) + (_dot(xl, rh) + _dot(xl, rl))
    tm = x.shape[0]
    lane = lax.broadcasted_iota(jnp.int32, (tm, LANES), 1)
    logits = jnp.where(lane < N_EXPERTS, logits, NEG_BIG)
    m1 = jnp.max(logits, axis=-1, keepdims=True)
    i1 = jnp.min(jnp.where(logits == m1, lane, LANES), axis=-1, keepdims=True)
    rest = jnp.where(lane == i1, NEG_BIG, logits)
    m2 = jnp.max(rest, axis=-1, keepdims=True)
    i2 = jnp.min(jnp.where(rest == m2, lane, LANES), axis=-1, keepdims=True)
    e = jnp.exp(m2 - m1)
    w1 = 1.0 / (1.0 + e)
    w2 = e / (1.0 + e)
    sel1 = lane == i1
    sel2 = lane == i2
    member = jnp.where(sel1 | sel2, 1.0, 0.0)
    before = _dot(tri_ref[...], member.astype(jnp.bfloat16)) + carry_ref[...]
    rank1 = jnp.sum(jnp.where(sel1, before, 0.0), axis=-1, keepdims=True).astype(jnp.int32)
    rank2 = jnp.sum(jnp.where(sel2, before, 0.0), axis=-1, keepdims=True).astype(jnp.int32)
    carry_ref[...] += jnp.sum(member, axis=0, keepdims=True)
    oi_ref[...] = jnp.where(lane == 0, i1, jnp.where(lane == 1, i2, jnp.where(lane == 2, rank1, rank2)))
    ow_ref[...] = jnp.where(lane == 0, w1, w2)
    cnt_ref[...] = carry_ref[...]


def _router(xf, router, tm=512):
    t, d = xf.shape
    r_pad = jnp.zeros((d, LANES), jnp.float32).at[:, :N_EXPERTS].set(router.astype(jnp.float32))
    tri = jnp.asarray(np.tril(np.ones((tm, tm), np.float32), -1), jnp.bfloat16)
    return pl.pallas_call(
        _router_kernel,
        out_shape=(jax.ShapeDtypeStruct((t, LANES), jnp.int32),
                   jax.ShapeDtypeStruct((t, LANES), jnp.float32),
                   jax.ShapeDtypeStruct((1, LANES), jnp.float32)),
        grid=(t // tm,),
        in_specs=[pl.BlockSpec((tm, d), lambda i: (i, 0)),
                  pl.BlockSpec((d, LANES), lambda i: (0, 0)),
                  pl.BlockSpec((tm, tm), lambda i: (0, 0))],
        out_specs=(pl.BlockSpec((tm, LANES), lambda i: (i, 0)),
                   pl.BlockSpec((tm, LANES), lambda i: (i, 0)),
                   pl.BlockSpec((1, LANES), lambda i: (0, 0))),
        scratch_shapes=[pltpu.VMEM((1, LANES), jnp.float32)],
        compiler_params=_cparams(("arbitrary",)),
    )(xf, r_pad, tri)


SC_CORES = 2
SC_SUBCORES = 16
SC_WINDOW = 32


def _sc_gather_rows(table, idx):
    m = idx.shape[0]
    d = table.shape[1]
    n_workers = SC_CORES * SC_SUBCORES
    per_w = m // n_workers
    n_win = per_w // SC_WINDOW
    assert per_w * n_workers == m and n_win * SC_WINDOW == per_w
    mesh = plsc.VectorSubcoreMesh(core_axis_name="c", subcore_axis_name="s")

    @functools.partial(
        pl.kernel, mesh=mesh,
        out_type=jax.ShapeDtypeStruct((m, d), table.dtype),
        scratch_types=[pltpu.VMEM((per_w,), jnp.int32),
                       pltpu.VMEM((SC_WINDOW, d), table.dtype),
                       pltpu.SemaphoreType.DMA],
    )
    def gather(table_hbm, idx_hbm, out_hbm, idx_v, rows_v, sem):
        wid = lax.axis_index("s") * SC_CORES + lax.axis_index("c")
        base = wid * per_w
        pltpu.sync_copy(idx_hbm.at[pl.ds(base, per_w)], idx_v)

        @pl.loop(0, n_win)
        def _(w):
            off = pl.multiple_of(w * SC_WINDOW, SC_WINDOW)
            pltpu.async_copy(table_hbm.at[idx_v.at[pl.ds(off, SC_WINDOW)]], rows_v, sem).wait()
            pltpu.sync_copy(rows_v, out_hbm.at[pl.ds(base + off, SC_WINDOW)])

    return gather(table, idx)


def _expert_kernel(te_ref, nv_ref, x_ref, w1_ref, w3_ref, w2_ref, o_ref, w1s_ref, w3s_ref, w2s_ref):
    i = pl.program_id(0)
    j = pl.program_id(1)
    n_valid = nv_ref[i]

    @pl.when(n_valid > 0)
    def _():
        w1s_ref[...] = w1_ref[0].astype(jnp.bfloat16)
        w3s_ref[...] = w3_ref[0].astype(jnp.bfloat16)
        w2s_ref[...] = w2_ref[0].astype(jnp.bfloat16)

    for s in range(MOE_TILE // MOE_SUB):
        rows = pl.ds(s * MOE_SUB, MOE_SUB)

        @pl.when(n_valid > s * MOE_SUB)
        def _():
            x = x_ref[rows, :].astype(jnp.bfloat16)
            a = _dot(x, w1s_ref[...])
            hid = (a * _sigmoid(a) * _dot(x, w3s_ref[...])).astype(jnp.bfloat16)
            part = _dot(hid, w2s_ref[...])

            @pl.when(j == 0)
            def _():
                o_ref[rows, :] = part

            @pl.when(j > 0)
            def _():
                o_ref[rows, :] += part

        @pl.when((n_valid <= s * MOE_SUB) & (j == 0))
        def _():
            o_ref[rows, :] = jnp.zeros((MOE_SUB, o_ref.shape[1]), o_ref.dtype)


def _experts(tile_expert, tile_valid, xs, w1, w3, w2):
    n_rows, d = xs.shape
    f = w1.shape[2]
    nj = f // MOE_FCHUNK

    def chunk(i, j, nv):
        return jnp.where(nv[i] > 0, j, nj - 1)

    return pl.pallas_call(
        _expert_kernel,
        out_shape=jax.ShapeDtypeStruct((n_rows, d), jnp.float32),
        grid_spec=pltpu.PrefetchScalarGridSpec(
            num_scalar_prefetch=2,
            grid=(n_rows // MOE_TILE, nj),
            in_specs=[pl.BlockSpec((MOE_TILE, d), lambda i, j, te, nv: (i, 0)),
                      pl.BlockSpec((1, d, MOE_FCHUNK), lambda i, j, te, nv: (te[i], 0, chunk(i, j, nv))),
                      pl.BlockSpec((1, d, MOE_FCHUNK), lambda i, j, te, nv: (te[i], 0, chunk(i, j, nv))),
                      pl.BlockSpec((1, MOE_FCHUNK, d), lambda i, j, te, nv: (te[i], chunk(i, j, nv), 0))],
            out_specs=pl.BlockSpec((MOE_TILE, d), lambda i, j, te, nv: (i, 0)),
            scratch_shapes=[pltpu.VMEM((d, MOE_FCHUNK), jnp.bfloat16),
                            pltpu.VMEM((d, MOE_FCHUNK), jnp.bfloat16),
                            pltpu.VMEM((MOE_FCHUNK, d), jnp.bfloat16)],
        ),
        compiler_params=pltpu.CompilerParams(dimension_semantics=("arbitrary", "arbitrary"),
                                             vmem_limit_bytes=MOE_VMEM_LIMIT),
    )(tile_expert, tile_valid, xs, w1, w3, w2)


def _combine_kernel(ya_ref, yb_ref, x_ref, w_ref, g_ref, b_ref, o_ref):
    w = w_ref[...]
    f = w[:, 0:1] * ya_ref[...] + w[:, 1:2] * yb_ref[...]
    o_ref[...] = _layer_norm_rows(ALPHA * x_ref[...] + f, g_ref[...], b_ref[...])


def _combine(yg, xf, wts, g, b, tm=512):
    t, d = xf.shape
    nt = t // tm
    return pl.pallas_call(
        _combine_kernel,
        out_shape=jax.ShapeDtypeStruct((t, d), jnp.float32),
        grid=(nt,),
        in_specs=[pl.BlockSpec((tm, d), lambda i: (i, 0)),
                  pl.BlockSpec((tm, d), lambda i: (i + nt, 0)),
                  pl.BlockSpec((tm, d), lambda i: (i, 0)),
                  pl.BlockSpec((tm, LANES), lambda i: (i, 0)),
                  pl.BlockSpec((1, d), lambda i: (0, 0)),
                  pl.BlockSpec((1, d), lambda i: (0, 0))],
        out_specs=pl.BlockSpec((tm, d), lambda i: (i, 0)),
        compiler_params=_cparams(("parallel",)),
    )(yg, yg, xf, wts, g.reshape(1, d), b.reshape(1, d))


def _moe_ffn(xf, router, w1, w3, w2, g, b):
    t, d = xf.shape
    oi, ow, cnt = _router(xf, router)
    e1, e2, rank1, rank2 = oi[:, 0], oi[:, 1], oi[:, 2], oi[:, 3]
    counts = cnt[0, :N_EXPERTS].astype(jnp.int32)
    tiles = (counts + MOE_TILE - 1) // MOE_TILE
    tile_end = jnp.cumsum(tiles)
    tile_start = tile_end - tiles
    offs = tile_start * MOE_TILE
    dest = jnp.concatenate([offs[e1] + rank1, offs[e2] + rank2])
    n_rows = TOP_K * t + N_EXPERTS * MOE_TILE
    n_tiles = n_rows // MOE_TILE
    tok = jnp.arange(t, dtype=jnp.int32)
    src = jnp.zeros((n_rows,), jnp.int32).at[dest].set(jnp.concatenate([tok, tok]))
    tile_id = jnp.arange(n_tiles, dtype=jnp.int32)
    tile_expert = jnp.sum(tile_id[:, None] >= tile_end[None, :], axis=1)
    used = tile_expert < N_EXPERTS
    last_expert = jnp.sum(tile_end[-1] - 1 >= tile_end)
    tile_expert = jnp.where(used, tile_expert, last_expert).astype(jnp.int32)
    tile_valid = jnp.clip(counts[tile_expert] - (tile_id - tile_start[tile_expert]) * MOE_TILE, 0, MOE_TILE)
    tile_valid = jnp.where(used, tile_valid, 0).astype(jnp.int32)
    xs = _sc_gather_rows(xf, src)
    y = _experts(tile_expert, tile_valid, xs, w1, w3, w2)
    yg = _sc_gather_rows(y, dest)
    return _combine(yg, xf, ow, g, b)


def _prep_layer(l, p, in_cols):
    bf = jnp.bfloat16
    out = {}
    out["w_in"] = _take_cols(p["w_in"][l], in_cols).astype(bf)
    deint = np.concatenate([np.arange(0, HEAD_DIM, 2), np.arange(1, HEAD_DIM, 2)])
    out["gq_a"] = jnp.tile(p["a_q_norm"][l][deint], 2).reshape(1, LANES)
    out["gk_a"] = jnp.tile(p["a_k_norm"][l][deint], 2).reshape(1, LANES)
    per_head = C_NOPE + C_ROPE
    uq_cols = []
    for h in range(C_HEADS):
        base = h * per_head
        uq_cols += [base + np.arange(C_NOPE), base + C_NOPE + np.arange(0, C_ROPE, 2),
                    base + C_NOPE + np.arange(1, C_ROPE, 2), np.full(32, -1)]
    wuq = _take_cols(p["c_w_uq"][l], np.concatenate(uq_cols))
    out["wuq"] = jnp.concatenate([wuq, jnp.zeros((64, wuq.shape[1]), wuq.dtype)], axis=0).astype(bf)
    uk_cols, uv_cols = [], []
    for h in range(C_HEADS):
        base = h * (C_NOPE + C_V)
        uk_cols += [base + np.arange(C_NOPE), np.full(64, -1)]
        uv_cols += [base + C_NOPE + np.arange(C_V)]
    out["wuk"] = _take_cols(p["c_w_ukv"][l], np.concatenate(uk_cols)).astype(bf)
    out["wuv"] = _take_cols(p["c_w_ukv"][l], np.concatenate(uv_cols)).astype(bf)
    out["gq_c"] = jnp.concatenate([p["c_q_norm"][l], jnp.zeros((64,), jnp.float32)]).reshape(1, 256)
    out["gkv_c"] = p["c_kv_norm"][l].reshape(1, C_KV_RANK)
    out["conv_w"] = jnp.concatenate([p["d_conv_w"][l][:, 0, :], jnp.zeros((1, D_CH), jnp.float32)], axis=0)
    out["conv_b"] = p["d_conv_b"][l].reshape(1, D_CH)
    out["d_ln_g"] = p["d_ln_g"][l].reshape(1, D_CH)
    out["d_ln_b"] = p["d_ln_b"][l].reshape(1, D_CH)
    a_rows = np.concatenate([(kvh * 2 + g) * HEAD_DIM + np.arange(HEAD_DIM)
                             for g in range(2) for kvh in range(A_KV_HEADS)])
    wb = p["w_branch"][l]
    out["w_branch"] = jnp.stack([wb[0][a_rows], wb[1], wb[2], wb[3]], axis=0).astype(bf)
    out["w_out"] = p["w_out"][l].astype(bf)
    return out


def kernel(x, ln_in_g, ln_in_b, w_in, a_q_norm, a_k_norm, b_rpb, c_q_norm, c_kv_norm, c_w_uq, c_w_ukv,
           d_conv_w, d_conv_b, d_ln_g, d_ln_b, w_branch, w_out, ln_mix_g, ln_mix_b,
           ffn_w1, ffn_w3, ffn_w2, moe_router, moe_w1, moe_w3, moe_w2, ln_ffn_g, ln_ffn_b):
    batch, seq, d = x.shape
    t = batch * seq
    bf = jnp.bfloat16
    params = dict(w_in=w_in, a_q_norm=a_q_norm, a_k_norm=a_k_norm, c_q_norm=c_q_norm, c_kv_norm=c_kv_norm,
                  c_w_uq=c_w_uq, c_w_ukv=c_w_ukv, d_conv_w=d_conv_w, d_conv_b=d_conv_b, d_ln_g=d_ln_g,
                  d_ln_b=d_ln_b, w_branch=w_branch, w_out=w_out)
    in_cols = _in_proj_columns()
    cos_a, sin_a, cos_c, sin_c = _rope_tables(seq)
    xf, xb = _input_ln(x.reshape(t, d), ln_in_g, ln_in_b)
    for l in range(DEPTH):
        p = _prep_layer(l, params, in_cols)
        h = _matmul(xb, p["w_in"], bf, tm=1024, tn=1280)
        ya = _attn_a(h, cos_a, sin_a, p["gq_a"], p["gk_a"], batch, seq)
        yb = _attn_b(h, _na_bias_tables(b_rpb[l], seq // GRID_W), batch, seq)
        yc = _attn_c(h, p["wuq"], p["wuk"], p["wuv"], p["gq_c"], p["gkv_c"], cos_c, sin_c, batch, seq)
        yd = _conv_branch(h, p["conv_w"], p["conv_b"], p["d_ln_g"], p["d_ln_b"], batch, seq)
        xf, xb = _merge(h, ya, yb, yc, yd, xf, p["w_branch"], p["w_out"], ln_mix_g[l], ln_mix_b[l])
        if l % 2 == 0:
            i = l // 2
            xf, xb = _ffn(xb, xf, ffn_w1[i].astype(bf), ffn_w3[i].astype(bf), ffn_w2[i].astype(bf),
                          ln_ffn_g[l], ln_ffn_b[l])
        else:
            i = l // 2
            xf = _moe_ffn(xf, moe_router[i], moe_w1[i], moe_w3[i], moe_w2[i], ln_ffn_g[l], ln_ffn_b[l])
            xb = xf.astype(bf)
    return xf.reshape(batch, seq, d)
```

```python
import functools

import numpy as np
import jax
import jax.numpy as jnp
from jax import lax
from jax.experimental import pallas as pl
from jax.experimental.pallas import tpu as pltpu
from jax.experimental.pallas import tpu_sc as plsc

D_MODEL = 1024
DEPTH = 2
GRID_W = 64
HEAD_DIM = 64
ROPE_THETA = 10000.0
RMS_EPS = 1e-6
LN_EPS = 1e-5
A_HEADS = 4
A_KV_HEADS = 2
B_HEADS = 4
NA_WIN_H = 8
NA_WIN_W = 16
C_HEADS = 4
C_NOPE = 64
C_ROPE = 32
C_V = 64
C_Q_RANK = 192
C_KV_RANK = 128
D_CH = 256
CONV_W = 31
N_BRANCH = 4
MIX_W = 256
D_FF = 2816
N_EXPERTS = 8
TOP_K = 2
D_FF_EXPERT = 3584
ALPHA = (2 * DEPTH) ** 0.25

LANES = 128
VMEM_LIMIT = 48 * 1024 * 1024

LOG2E = 1.4426950408889634
NEG_BIG = -1e30

COL_GATES = 0
COL_AQ = 4096
COL_AKV = 4352
COL_C = 4608
COL_D = 5120
COL_BQ = 5632
COL_BK = 5888
COL_BV = 6144
D_IN_PAD = 6400

NA_QROWS = 4
NA_KROWS = 12

MOE_TILE = 1024
MOE_SUB = 512
MOE_FCHUNK = 896
MOE_VMEM_LIMIT = 56 * 1024 * 1024


def _in_proj_columns():
    o_aq, o_ak, o_av = 0, 256, 384
    o_bq, o_bk, o_bv = 512, 768, 1024
    o_cq, o_ckv, o_kpe = 1280, 1472, 1600
    o_d, o_g = 1632, 2144
    deint = np.concatenate([np.arange(0, HEAD_DIM, 2), np.arange(1, HEAD_DIM, 2)])
    cols = [o_g + np.arange(N_BRANCH * D_MODEL)]
    for g in range(2):
        for kvh in range(A_KV_HEADS):
            cols.append(o_aq + (kvh * 2 + g) * HEAD_DIM + deint)
    for kvh in range(A_KV_HEADS):
        cols.append(o_ak + kvh * HEAD_DIM + deint)
    cols.append(o_av + np.arange(A_KV_HEADS * HEAD_DIM))
    cols.append(o_cq + np.arange(C_Q_RANK))
    cols.append(np.full(64, -1))
    cols.append(o_ckv + np.arange(C_KV_RANK))
    cols.append(np.full(64, -1))
    cols.append(o_kpe + np.arange(0, C_ROPE, 2))
    cols.append(o_kpe + np.arange(1, C_ROPE, 2))
    cols.append(np.full(32, -1))
    cols.append(o_d + np.arange(2 * D_CH))
    cols.append(o_bq + np.arange(3 * B_HEADS * HEAD_DIM))
    cols = np.concatenate(cols)
    assert cols.shape[0] == D_IN_PAD
    return cols


def _take_cols(w, cols):
    cols = np.asarray(cols)
    parts, i, n = [], 0, len(cols)
    while i < n:
        j = i + 1
        if cols[i] < 0:
            while j < n and cols[j] < 0:
                j += 1
            parts.append(jnp.zeros((w.shape[0], j - i), w.dtype))
        else:
            step = int(cols[j] - cols[i]) if j < n and cols[j] > cols[i] else 1
            while j < n and cols[j] == cols[j - 1] + step:
                j += 1
            c0, n_run = int(cols[i]), j - i
            if step == 1:
                parts.append(w[:, c0:c0 + n_run])
            else:
                assert step == 2
                base = c0 - c0 % 2
                parts.append(w[:, base:base + 2 * n_run].reshape(w.shape[0], n_run, 2)[:, :, c0 % 2])
        i = j
    return jnp.concatenate(parts, axis=1)


def _rope_tables(seq):
    t = np.arange(seq)
    row = (t // GRID_W).astype(np.float32)
    col = (t % GRID_W).astype(np.float32)

    def angles(rot_dim):
        n_freq = rot_dim // 4
        inv = jnp.asarray(ROPE_THETA, jnp.float32) ** (-jnp.arange(n_freq, dtype=jnp.float32) / n_freq)
        return jnp.concatenate([jnp.asarray(row)[:, None] * inv, jnp.asarray(col)[:, None] * inv], axis=-1)

    ang_a = angles(HEAD_DIM)
    ca, sa = jnp.cos(ang_a), jnp.sin(ang_a)
    cos_a = jnp.concatenate([ca, ca, ca, ca], axis=-1)
    sin_a = jnp.concatenate([-sa, sa, -sa, sa], axis=-1)
    ang_c = angles(C_ROPE)
    cc, sc = jnp.cos(ang_c), jnp.sin(ang_c)
    one = jnp.ones((seq, 64), jnp.float32)
    zero = jnp.zeros((seq, 64), jnp.float32)
    cos_c = jnp.concatenate([one, cc, cc, one[:, :32]], axis=-1)
    sin_c = jnp.concatenate([zero, -sc, sc, zero[:, :32]], axis=-1)
    return cos_a, sin_a, cos_c, sin_c


def _na_bias_tables(rpb, rows):
    wh = min(NA_WIN_H, rows)
    n_dr, n_dc = 2 * NA_WIN_H - 1, 2 * NA_WIN_W - 1
    qc = np.arange(GRID_W)[:, None]
    kc = np.arange(GRID_W)[None, :]
    col_start = np.clip(qc - NA_WIN_W // 2, 0, GRID_W - NA_WIN_W)
    ok_c = (kc >= col_start) & (kc < col_start + NA_WIN_W)
    dc = np.clip(kc - qc + (NA_WIN_W - 1), 0, n_dc - 1)
    oh_c = np.eye(n_dc, dtype=np.float32)[dc.reshape(-1)]
    oh_r, ok_r = [], []
    for r0 in (0, NA_QROWS, rows - NA_QROWS):
        ws = int(np.clip(r0 - 4, 0, rows - NA_KROWS))
        qr = r0 + np.arange(NA_QROWS)[:, None]
        kr = ws + np.arange(NA_KROWS)[None, :]
        row_start = np.clip(qr - wh // 2, 0, rows - wh)
        ok_r.append((kr >= row_start) & (kr < row_start + wh))
        dr = np.clip(kr - qr + (NA_WIN_H - 1), 0, n_dr - 1)
        oh_r.append(np.eye(n_dr, dtype=np.float32)[dr.reshape(-1)])
    oh_r = np.concatenate(oh_r, axis=0)
    ok = np.stack(ok_r)[:, :, None, :, None] & ok_c[None, None, :, None, :]
    ok = ok.reshape(3, 1, NA_QROWS * GRID_W, NA_KROWS * GRID_W)
    hi = lax.Precision.HIGHEST
    t1 = jnp.einsum("ma,hab->hmb", jnp.asarray(oh_r), rpb.astype(jnp.float32), precision=hi)
    t2 = jnp.einsum("hmb,nb->hmn", t1, jnp.asarray(oh_c), precision=hi)
    nh = rpb.shape[0]
    t2 = t2.reshape(nh, 3, NA_QROWS, NA_KROWS, GRID_W, GRID_W).transpose(1, 0, 2, 4, 3, 5)
    bias = t2.reshape(3, nh, NA_QROWS * GRID_W, NA_KROWS * GRID_W)
    return jnp.where(jnp.asarray(ok), bias * LOG2E, NEG_BIG)


def _cparams(sem):
    return pltpu.CompilerParams(dimension_semantics=sem, vmem_limit_bytes=VMEM_LIMIT)


def _layer_norm_rows(z, g, b):
    mu = jnp.mean(z, axis=-1, keepdims=True)
    zc = z - mu
    var = jnp.mean(zc * zc, axis=-1, keepdims=True)
    return zc * lax.rsqrt(var + LN_EPS) * g + b


def _sigmoid(x):
    return 0.5 * jnp.tanh(0.5 * x) + 0.5


def _dot(a, b):
    return jnp.dot(a, b, preferred_element_type=jnp.float32)


def _dot_nt(a, b):
    return lax.dot_general(a, b, (((1,), (1,)), ((), ())), preferred_element_type=jnp.float32)


def _ln_kernel(x_ref, g_ref, b_ref, of_ref, ob_ref):
    y = _layer_norm_rows(x_ref[...], g_ref[...], b_ref[...])
    of_ref[...] = y
    ob_ref[...] = y.astype(jnp.bfloat16)


def _input_ln(x, g, b, tm=1024):
    t, d = x.shape
    return pl.pallas_call(
        _ln_kernel,
        out_shape=(jax.ShapeDtypeStruct((t, d), jnp.float32), jax.ShapeDtypeStruct((t, d), jnp.bfloat16)),
        grid=(t // tm,),
        in_specs=[pl.BlockSpec((tm, d), lambda i: (i, 0)),
                  pl.BlockSpec((1, d), lambda i: (0, 0)),
                  pl.BlockSpec((1, d), lambda i: (0, 0))],
        out_specs=(pl.BlockSpec((tm, d), lambda i: (i, 0)), pl.BlockSpec((tm, d), lambda i: (i, 0))),
        compiler_params=_cparams(("parallel",)),
    )(x, g.reshape(1, d), b.reshape(1, d))


def _mm_kernel(a_ref, b_ref, o_ref):
    o_ref[...] = _dot(a_ref[...], b_ref[...]).astype(o_ref.dtype)


def _matmul(a, b, out_dtype, tm, tn):
    m, k = a.shape
    n = b.shape[1]
    return pl.pallas_call(
        _mm_kernel,
        out_shape=jax.ShapeDtypeStruct((m, n), out_dtype),
        grid=(m // tm, n // tn),
        in_specs=[pl.BlockSpec((tm, k), lambda i, j: (i, 0)),
                  pl.BlockSpec((k, tn), lambda i, j: (0, j))],
        out_specs=pl.BlockSpec((tm, tn), lambda i, j: (i, j)),
        compiler_params=_cparams(("parallel", "parallel")),
    )(a, b)


def _softmax_pv(s, v):
    m = jnp.max(s, axis=-1, keepdims=True)
    p = jnp.exp2(s - m)
    l = jnp.sum(p, axis=-1, keepdims=True)
    o = _dot(p.astype(jnp.bfloat16), v)
    return o * (1.0 / l)


def _rotate_pairs(x, cos, sin, half, lane):
    w = x.shape[-1]
    first = (lane % (2 * half)) < half
    partner = jnp.where(first, pltpu.roll(x, w - half, 1), pltpu.roll(x, half, 1))
    return x * cos + partner * sin


def _attn_a_kernel(q_ref, kv_ref, cos_ref, sin_ref, gq_ref, gk_ref, o_ref, ks_ref, *, tq):
    s_len = q_ref.shape[0]

    def prep(x, g, cos, sin):
        lane = lax.broadcasted_iota(jnp.int32, x.shape, 1)
        lo = lane < HEAD_DIM
        x2 = x * x
        ss_lo = jnp.sum(jnp.where(lo, x2, 0.0), axis=-1, keepdims=True)
        ss_hi = jnp.sum(jnp.where(lo, 0.0, x2), axis=-1, keepdims=True)
        ms = jnp.where(lo, ss_lo, ss_hi) * (1.0 / HEAD_DIM)
        xn = x * lax.rsqrt(ms + RMS_EPS) * g
        return _rotate_pairs(xn, cos, sin, HEAD_DIM // 2, lane)

    ks_ref[...] = prep(kv_ref[:, 0:LANES].astype(jnp.float32), gk_ref[...],
                       cos_ref[...], sin_ref[...]).astype(jnp.bfloat16)
    scale = HEAD_DIM ** -0.5 * LOG2E
    lane_q = lax.broadcasted_iota(jnp.int32, (tq, LANES), 1)
    lo_q = lane_q < HEAD_DIM

    def body(i, carry):
        r = pl.multiple_of(i * tq, tq)
        k = ks_ref[...]
        v = kv_ref[:, LANES:2 * LANES]
        cos = cos_ref[pl.ds(r, tq), :]
        sin = sin_ref[pl.ds(r, tq), :]
        for g in range(2):
            qt = prep(q_ref[pl.ds(r, tq), g * LANES:(g + 1) * LANES].astype(jnp.float32), gq_ref[...], cos, sin)
            qt = (qt * scale).astype(jnp.bfloat16)
            zero = jnp.zeros_like(qt)
            o_lo = _softmax_pv(_dot_nt(jnp.where(lo_q, qt, zero), k), v)
            o_hi = _softmax_pv(_dot_nt(jnp.where(lo_q, zero, qt), k), v)
            o_ref[pl.ds(r, tq), g * LANES:(g + 1) * LANES] = jnp.where(lo_q, o_lo, o_hi).astype(o_ref.dtype)
        return carry

    lax.fori_loop(0, s_len // tq, body, 0)


def _attn_a(h, cos_a, sin_a, gq, gk, batch, seq, tq=256):
    kern = functools.partial(_attn_a_kernel, tq=tq)
    return pl.pallas_call(
        kern,
        out_shape=jax.ShapeDtypeStruct((batch * seq, MIX_W), jnp.bfloat16),
        grid=(batch,),
        in_specs=[pl.BlockSpec((seq, 256), lambda b: (b, COL_AQ // 256)),
                  pl.BlockSpec((seq, 256), lambda b: (b, COL_AKV // 256)),
                  pl.BlockSpec((seq, LANES), lambda b: (0, 0)),
                  pl.BlockSpec((seq, LANES), lambda b: (0, 0)),
                  pl.BlockSpec((1, LANES), lambda b: (0, 0)),
                  pl.BlockSpec((1, LANES), lambda b: (0, 0))],
        out_specs=pl.BlockSpec((seq, MIX_W), lambda b: (b, 0)),
        scratch_shapes=[pltpu.VMEM((seq, LANES), jnp.bfloat16)],
        compiler_params=_cparams(("parallel",)),
    )(h, h, cos_a, sin_a, gq, gk)


def _attn_b_kernel(q_ref, k_ref, v_ref, bias_ref, o_ref, *, rows):
    nq = NA_QROWS * GRID_W
    nk = NA_KROWS * GRID_W
    lane_q = lax.broadcasted_iota(jnp.int32, (nq, LANES), 1)
    lo_q = lane_q < HEAD_DIM
    n_groups = rows // NA_QROWS
    scale = HEAD_DIM ** -0.5 * LOG2E

    def body(gi, carry):
        ws = jnp.clip(gi * NA_QROWS - 4, 0, rows - NA_KROWS)
        cls = jnp.where(gi == 0, 0, jnp.where(gi == n_groups - 1, 2, 1))
        qr = pl.multiple_of(gi * nq, nq)
        kr = pl.multiple_of(ws * GRID_W, GRID_W)
        for blk in range(B_HEADS // 2):
            sl = slice(blk * LANES, (blk + 1) * LANES)
            qt = q_ref[pl.ds(qr, nq), sl] * scale
            kw = k_ref[pl.ds(kr, nk), sl]
            vw = v_ref[pl.ds(kr, nk), sl]
            zero = jnp.zeros_like(qt)
            s_lo = _dot_nt(jnp.where(lo_q, qt, zero), kw) + bias_ref[cls, 2 * blk]
            s_hi = _dot_nt(jnp.where(lo_q, zero, qt), kw) + bias_ref[cls, 2 * blk + 1]
            o_lo = _softmax_pv(s_lo, vw)
            o_hi = _softmax_pv(s_hi, vw)
            o_ref[pl.ds(qr, nq), sl] = jnp.where(lo_q, o_lo, o_hi).astype(o_ref.dtype)
        return carry

    lax.fori_loop(0, n_groups, body, 0)


def _attn_b(h, bias, batch, seq):
    rows = seq // GRID_W
    kern = functools.partial(_attn_b_kernel, rows=rows)
    return pl.pallas_call(
        kern,
        out_shape=jax.ShapeDtypeStruct((batch * seq, MIX_W), jnp.bfloat16),
        grid=(batch,),
        in_specs=[pl.BlockSpec((seq, 256), lambda b: (b, COL_BQ // 256)),
                  pl.BlockSpec((seq, 256), lambda b: (b, COL_BK // 256)),
                  pl.BlockSpec((seq, 256), lambda b: (b, COL_BV // 256)),
                  pl.BlockSpec(bias.shape, lambda b: (0, 0, 0, 0))],
        out_specs=pl.BlockSpec((seq, MIX_W), lambda b: (b, 0)),
        compiler_params=_cparams(("parallel",)),
    )(h, h, h, bias)


def _attn_c_kernel(hc_ref, wuq_ref, wuk_ref, wuv_ref, gq_ref, gkv_ref, cos_ref, sin_ref, o_ref,
                   qs_ref, ks_ref, vs_ref, *, tq):
    s_len = hc_ref.shape[0]
    lane = lax.broadcasted_iota(jnp.int32, (s_len, LANES), 1)
    cos = cos_ref[...]
    sin = sin_ref[...]

    cq = hc_ref[:, 0:256].astype(jnp.float32)
    ms = jnp.sum(cq * cq, axis=-1, keepdims=True) * (1.0 / C_Q_RANK)
    cqn = (cq * lax.rsqrt(ms + RMS_EPS) * gq_ref[...]).astype(jnp.bfloat16)
    ckv = hc_ref[:, 256:384].astype(jnp.float32)
    ms = jnp.mean(ckv * ckv, axis=-1, keepdims=True)
    kvn = (ckv * lax.rsqrt(ms + RMS_EPS) * gkv_ref[...]).astype(jnp.bfloat16)
    kpe = _rotate_pairs(hc_ref[:, 384:512].astype(jnp.float32), cos, sin, C_ROPE // 2, lane)
    vs_ref[...] = _dot(kvn, wuv_ref[...]).astype(jnp.bfloat16)
    scale = (C_NOPE + C_ROPE) ** -0.5 * LOG2E
    for h in range(C_HEADS):
        sl = slice(h * LANES, (h + 1) * LANES)
        qh = _rotate_pairs(_dot(cqn, wuq_ref[:, sl]), cos, sin, C_ROPE // 2, lane) * scale
        qs_ref[:, sl] = qh.astype(jnp.bfloat16)
        ks_ref[:, sl] = (_dot(kvn, wuk_ref[:, sl]) + kpe).astype(jnp.bfloat16)

    lane_q = lax.broadcasted_iota(jnp.int32, (tq, LANES), 1)
    lo_q = lane_q < C_V

    def body(i, carry):
        r = pl.multiple_of(i * tq, tq)
        for blk in range(C_HEADS // 2):
            vb = vs_ref[:, blk * LANES:(blk + 1) * LANES]
            outs = []
            for par in range(2):
                sl = slice((2 * blk + par) * LANES, (2 * blk + par + 1) * LANES)
                outs.append(_softmax_pv(_dot_nt(qs_ref[pl.ds(r, tq), sl], ks_ref[:, sl]), vb))
            o_ref[pl.ds(r, tq), blk * LANES:(blk + 1) * LANES] = jnp.where(lo_q, outs[0], outs[1]).astype(o_ref.dtype)
        return carry

    lax.fori_loop(0, s_len // tq, body, 0)


def _attn_c(h, wuq, wuk, wuv, gq, gkv, cos_c, sin_c, batch, seq, tq=256):
    kern = functools.partial(_attn_c_kernel, tq=tq)
    const = lambda shape: pl.BlockSpec(shape, lambda b: (0,) * len(shape))
    return pl.pallas_call(
        kern,
        out_shape=jax.ShapeDtypeStruct((batch * seq, MIX_W), jnp.bfloat16),
        grid=(batch,),
        in_specs=[pl.BlockSpec((seq, 512), lambda b: (b, COL_C // 512)),
                  const(wuq.shape), const(wuk.shape), const(wuv.shape),
                  const(gq.shape), const(gkv.shape), const(cos_c.shape), const(sin_c.shape)],
        out_specs=pl.BlockSpec((seq, MIX_W), lambda b: (b, 0)),
        scratch_shapes=[pltpu.VMEM((seq, C_HEADS * LANES), jnp.bfloat16),
                        pltpu.VMEM((seq, C_HEADS * LANES), jnp.bfloat16),
                        pltpu.VMEM((seq, C_HEADS * C_V), jnp.bfloat16)],
        compiler_params=_cparams(("parallel",)),
    )(h, wuq, wuk, wuv, gq, gkv, cos_c, sin_c)


CONV_PAD = 16
CONV_ROWS = 128


def _conv_kernel(hd_ref, w_ref, cb_ref, g_ref, b_ref, o_ref, up_ref):
    s_len = hd_ref.shape[0]
    a = hd_ref[:, 0:D_CH].astype(jnp.float32)
    gate = hd_ref[:, D_CH:2 * D_CH].astype(jnp.float32)
    zeros = jnp.zeros((CONV_PAD, D_CH), jnp.float32)
    up_ref[0:CONV_PAD, :] = zeros
    up_ref[CONV_PAD + s_len:2 * CONV_PAD + s_len, :] = zeros
    up_ref[CONV_PAD:CONV_PAD + s_len, :] = a * _sigmoid(gate)
    shift = CONV_PAD - CONV_W // 2
    for c in range(s_len // CONV_ROWS):
        acc = jnp.zeros((CONV_ROWS, D_CH), jnp.float32)
        for j in range(CONV_W):
            start = c * CONV_ROWS + j + shift
            acc = acc + up_ref[start:start + CONV_ROWS, :] * w_ref[j:j + 1, :]
        u = acc + cb_ref[...]
        y = _layer_norm_rows(u, g_ref[...], b_ref[...])
        o_ref[c * CONV_ROWS:(c + 1) * CONV_ROWS, :] = (y * _sigmoid(y)).astype(o_ref.dtype)


def _conv_branch(h, w, cb, g, b, batch, seq):
    const = lambda shape: pl.BlockSpec(shape, lambda i: (0,) * len(shape))
    return pl.pallas_call(
        _conv_kernel,
        out_shape=jax.ShapeDtypeStruct((batch * seq, MIX_W), jnp.bfloat16),
        grid=(batch,),
        in_specs=[pl.BlockSpec((seq, 512), lambda i: (i, COL_D // 512)),
                  const(w.shape), const(cb.shape), const(g.shape), const(b.shape)],
        out_specs=pl.BlockSpec((seq, MIX_W), lambda i: (i, 0)),
        scratch_shapes=[pltpu.VMEM((seq + 2 * CONV_PAD, D_CH), jnp.float32)],
        compiler_params=_cparams(("parallel",)),
    )(h, w, cb, g, b)


def _merge_kernel(gl_ref, ya_ref, yb_ref, yc_ref, yd_ref, x_ref, wb_ref, wo_ref, g_ref, b_ref,
                  of_ref, ob_ref):
    merged = None
    for n, y_ref in enumerate((ya_ref, yb_ref, yc_ref, yd_ref)):
        gate = _sigmoid(gl_ref[:, n * D_MODEL:(n + 1) * D_MODEL].astype(jnp.float32))
        term = gate * _dot(y_ref[...], wb_ref[n])
        merged = term if merged is None else merged + term
    mix = _dot(merged.astype(jnp.bfloat16), wo_ref[...])
    y = _layer_norm_rows(ALPHA * x_ref[...] + mix, g_ref[...], b_ref[...])
    of_ref[...] = y
    ob_ref[...] = y.astype(jnp.bfloat16)


def _merge(h, ya, yb, yc, yd, x, wb, wo, g, b, tm=512):
    t, d = x.shape
    row = lambda w: pl.BlockSpec((tm, w), lambda i: (i, 0))
    const = lambda shape: pl.BlockSpec(shape, lambda i: (0,) * len(shape))
    return pl.pallas_call(
        _merge_kernel,
        out_shape=(jax.ShapeDtypeStruct((t, d), jnp.float32), jax.ShapeDtypeStruct((t, d), jnp.bfloat16)),
        grid=(t // tm,),
        in_specs=[pl.BlockSpec((tm, N_BRANCH * d), lambda i: (i, COL_GATES)),
                  row(MIX_W), row(MIX_W), row(MIX_W), row(MIX_W), row(d),
                  const(wb.shape), const(wo.shape), const((1, d)), const((1, d))],
        out_specs=(row(d), row(d)),
        compiler_params=_cparams(("parallel",)),
    )(h, ya, yb, yc, yd, x, wb, wo, g.reshape(1, d), b.reshape(1, d))


def _ffn_kernel(xb_ref, xf_ref, w1_ref, w3_ref, w2_ref, g_ref, b_ref, of_ref, ob_ref, acc_ref):
    j = pl.program_id(1)
    x = xb_ref[...]
    a = _dot(x, w1_ref[...])
    hid = (a * _sigmoid(a) * _dot(x, w3_ref[...])).astype(jnp.bfloat16)
    part = _dot(hid, w2_ref[...])

    @pl.when(j == 0)
    def _():
        acc_ref[...] = part

    @pl.when(j > 0)
    def _():
        acc_ref[...] += part

    @pl.when(j == pl.num_programs(1) - 1)
    def _():
        y = _layer_norm_rows(ALPHA * xf_ref[...] + acc_ref[...], g_ref[...], b_ref[...])
        of_ref[...] = y
        ob_ref[...] = y.astype(jnp.bfloat16)


def _ffn(xb, xf, w1, w3, w2, g, b, tm=512, tf=1408):
    t, d = xf.shape
    f = w1.shape[1]
    row = pl.BlockSpec((tm, d), lambda i, j: (i, 0))
    vec = pl.BlockSpec((1, d), lambda i, j: (0, 0))
    return pl.pallas_call(
        _ffn_kernel,
        out_shape=(jax.ShapeDtypeStruct((t, d), jnp.float32), jax.ShapeDtypeStruct((t, d), jnp.bfloat16)),
        grid=(t // tm, f // tf),
        in_specs=[row, row,
                  pl.BlockSpec((d, tf), lambda i, j: (0, j)),
                  pl.BlockSpec((d, tf), lambda i, j: (0, j)),
                  pl.BlockSpec((tf, d), lambda i, j: (j, 0)),
                  vec, vec],
        out_specs=(row, row),
        scratch_shapes=[pltpu.VMEM((tm, d), jnp.float32)],
        compiler_params=_cparams(("parallel", "arbitrary")),
    )(xb, xf, w1, w3, w2, g.reshape(1, d), b.reshape(1, d))


def _router_kernel(x_ref, r_ref, tri_ref, oi_ref, ow_ref, cnt_ref, carry_ref):
    i = pl.program_id(0)

    @pl.when(i == 0)
    def _():
        carry_ref[...] = jnp.zeros_like(carry_ref)

    x = x_ref[...]
    r = r_ref[...]
    xh = x.astype(jnp.bfloat16)
    xl = (x - xh.astype(jnp.float32)).astype(jnp.bfloat16)
    rh = r.astype(jnp.bfloat16)
    rl = (r - rh.astype(jnp.float32)).astype(jnp.bfloat16)
    logits = (_dot(xh, rh) + _dot(xh, rl)) + (_dot(xl, rh) + _dot(xl, rl))
    tm = x.shape[0]
    lane = lax.broadcasted_iota(jnp.int32, (tm, LANES), 1)
    logits = jnp.where(lane < N_EXPERTS, logits, NEG_BIG)
    m1 = jnp.max(logits, axis=-1, keepdims=True)
    i1 = jnp.min(jnp.where(logits == m1, lane, LANES), axis=-1, keepdims=True)
    rest = jnp.where(lane == i1, NEG_BIG, logits)
    m2 = jnp.max(rest, axis=-1, keepdims=True)
    i2 = jnp.min(jnp.where(rest == m2, lane, LANES), axis=-1, keepdims=True)
    e = jnp.exp(m2 - m1)
    w1 = 1.0 / (1.0 + e)
    w2 = e / (1.0 + e)
    sel1 = lane == i1
    sel2 = lane == i2
    member = jnp.where(sel1 | sel2, 1.0, 0.0)
    before = _dot(tri_ref[...], member.astype(jnp.bfloat16)) + carry_ref[...]
    rank1 = jnp.sum(jnp.where(sel1, before, 0.0), axis=-1, keepdims=True).astype(jnp.int32)
    rank2 = jnp.sum(jnp.where(sel2, before, 0.0), axis=-1, keepdims=True).astype(jnp.int32)
    carry_ref[...] += jnp.sum(member, axis=0, keepdims=True)
    oi_ref[...] = jnp.where(lane == 0, i1, jnp.where(lane == 1, i2, jnp.where(lane == 2, rank1, rank2)))
    ow_ref[...] = jnp.where(lane == 0, w1, w2)
    cnt_ref[...] = carry_ref[...]


def _router(xf, router, tm=512):
    t, d = xf.shape
    r_pad = jnp.zeros((d, LANES), jnp.float32).at[:, :N_EXPERTS].set(router.astype(jnp.float32))
    tri = jnp.asarray(np.tril(np.ones((tm, tm), np.float32), -1), jnp.bfloat16)
    return pl.pallas_call(
        _router_kernel,
        out_shape=(jax.ShapeDtypeStruct((t, LANES), jnp.int32),
                   jax.ShapeDtypeStruct((t, LANES), jnp.float32),
                   jax.ShapeDtypeStruct((1, LANES), jnp.float32)),
        grid=(t // tm,),
        in_specs=[pl.BlockSpec((tm, d), lambda i: (i, 0)),
                  pl.BlockSpec((d, LANES), lambda i: (0, 0)),
                  pl.BlockSpec((tm, tm), lambda i: (0, 0))],
        out_specs=(pl.BlockSpec((tm, LANES), lambda i: (i, 0)),
                   pl.BlockSpec((tm, LANES), lambda i: (i, 0)),
                   pl.BlockSpec((1, LANES), lambda i: (0, 0))),
        scratch_shapes=[pltpu.VMEM((1, LANES), jnp.float32)],
        compiler_params=_cparams(("arbitrary",)),
    )(xf, r_pad, tri)


SC_CORES = 2
SC_SUBCORES = 16
SC_WINDOW = 32


def _sc_gather_rows(table, idx):
    m = idx.shape[0]
    d = table.shape[1]
    n_workers = SC_CORES * SC_SUBCORES
    per_w = m // n_workers
    n_win = per_w // SC_WINDOW
    assert per_w * n_workers == m and n_win * SC_WINDOW == per_w
    mesh = plsc.VectorSubcoreMesh(core_axis_name="c", subcore_axis_name="s")

    @functools.partial(
        pl.kernel, mesh=mesh,
        out_type=jax.ShapeDtypeStruct((m, d), table.dtype),
        scratch_types=[pltpu.VMEM((per_w,), jnp.int32),
                       pltpu.VMEM((SC_WINDOW, d), table.dtype),
                       pltpu.SemaphoreType.DMA],
    )
    def gather(table_hbm, idx_hbm, out_hbm, idx_v, rows_v, sem):
        wid = lax.axis_index("s") * SC_CORES + lax.axis_index("c")
        base = wid * per_w
        pltpu.sync_copy(idx_hbm.at[pl.ds(base, per_w)], idx_v)

        @pl.loop(0, n_win)
        def _(w):
            off = pl.multiple_of(w * SC_WINDOW, SC_WINDOW)
            pltpu.async_copy(table_hbm.at[idx_v.at[pl.ds(off, SC_WINDOW)]], rows_v, sem).wait()
            pltpu.sync_copy(rows_v, out_hbm.at[pl.ds(base + off, SC_WINDOW)])

    return gather(table, idx)


def _expert_kernel(te_ref, nv_ref, x_ref, w1_ref, w3_ref, w2_ref, o_ref, w1s_ref, w3s_ref, w2s_ref):
    i = pl.program_id(0)
    j = pl.program_id(1)
    n_valid = nv_ref[i]

    @pl.when(n_valid > 0)
    def _():
        w1s_ref[...] = w1_ref[0].astype(jnp.bfloat16)
        w3s_ref[...] = w3_ref[0].astype(jnp.bfloat16)
        w2s_ref[...] = w2_ref[0].astype(jnp.bfloat16)

    for s in range(MOE_TILE // MOE_SUB):
        rows = pl.ds(s * MOE_SUB, MOE_SUB)

        @pl.when(n_valid > s * MOE_SUB)
        def _():
            x = x_ref[rows, :].astype(jnp.bfloat16)
            a = _dot(x, w1s_ref[...])
            hid = (a * _sigmoid(a) * _dot(x, w3s_ref[...])).astype(jnp.bfloat16)
            part = _dot(hid, w2s_ref[...])

            @pl.when(j == 0)
            def _():
                o_ref[rows, :] = part

            @pl.when(j > 0)
            def _():
                o_ref[rows, :] += part

        @pl.when((n_valid <= s * MOE_SUB) & (j == 0))
        def _():
            o_ref[rows, :] = jnp.zeros((MOE_SUB, o_ref.shape[1]), o_ref.dtype)


def _experts(tile_expert, tile_valid, xs, w1, w3, w2):
    n_rows, d = xs.shape
    f = w1.shape[2]
    nj = f // MOE_FCHUNK

    def chunk(i, j, nv):
        return jnp.where(nv[i] > 0, j, nj - 1)

    return pl.pallas_call(
        _expert_kernel,
        out_shape=jax.ShapeDtypeStruct((n_rows, d), jnp.float32),
        grid_spec=pltpu.PrefetchScalarGridSpec(
            num_scalar_prefetch=2,
            grid=(n_rows // MOE_TILE, nj),
            in_specs=[pl.BlockSpec((MOE_TILE, d), lambda i, j, te, nv: (i, 0)),
                      pl.BlockSpec((1, d, MOE_FCHUNK), lambda i, j, te, nv: (te[i], 0, chunk(i, j, nv))),
                      pl.BlockSpec((1, d, MOE_FCHUNK), lambda i, j, te, nv: (te[i], 0, chunk(i, j, nv))),
                      pl.BlockSpec((1, MOE_FCHUNK, d), lambda i, j, te, nv: (te[i], chunk(i, j, nv), 0))],
            out_specs=pl.BlockSpec((MOE_TILE, d), lambda i, j, te, nv: (i, 0)),
            scratch_shapes=[pltpu.VMEM((d, MOE_FCHUNK), jnp.bfloat16),
                            pltpu.VMEM((d, MOE_FCHUNK), jnp.bfloat16),
                            pltpu.VMEM((MOE_FCHUNK, d), jnp.bfloat16)],
        ),
        compiler_params=pltpu.CompilerParams(dimension_semantics=("arbitrary", "arbitrary"),
                                             vmem_limit_bytes=MOE_VMEM_LIMIT),
    )(tile_expert, tile_valid, xs, w1, w3, w2)


def _combine_kernel(ya_ref, yb_ref, x_ref, w_ref, g_ref, b_ref, o_ref):
    w = w_ref[...]
    f = w[:, 0:1] * ya_ref[...] + w[:, 1:2] * yb_ref[...]
    o_ref[...] = _layer_norm_rows(ALPHA * x_ref[...] + f, g_ref[...], b_ref[...])


def _combine(yg, xf, wts, g, b, tm=512):
    t, d = xf.shape
    nt = t // tm
    return pl.pallas_call(
        _combine_kernel,
        out_shape=jax.ShapeDtypeStruct((t, d), jnp.float32),
        grid=(nt,),
        in_specs=[pl.BlockSpec((tm, d), lambda i: (i, 0)),
                  pl.BlockSpec((tm, d), lambda i: (i + nt, 0)),
                  pl.BlockSpec((tm, d), lambda i: (i, 0)),
                  pl.BlockSpec((tm, LANES), lambda i: (i, 0)),
                  pl.BlockSpec((1, d), lambda i: (0, 0)),
                  pl.BlockSpec((1, d), lambda i: (0, 0))],
        out_specs=pl.BlockSpec((tm, d), lambda i: (i, 0)),
        compiler_params=_cparams(("parallel",)),
    )(yg, yg, xf, wts, g.reshape(1, d), b.reshape(1, d))


def _moe_ffn(xf, router, w1, w3, w2, g, b):
    t, d = xf.shape
    oi, ow, cnt = _router(xf, router)
    e1, e2, rank1, rank2 = oi[:, 0], oi[:, 1], oi[:, 2], oi[:, 3]
    counts = cnt[0, :N_EXPERTS].astype(jnp.int32)
    tiles = (counts + MOE_TILE - 1) // MOE_TILE
    tile_end = jnp.cumsum(tiles)
    tile_start = tile_end - tiles
    offs = tile_start * MOE_TILE
    dest = jnp.concatenate([offs[e1] + rank1, offs[e2] + rank2])
    n_rows = TOP_K * t + N_EXPERTS * MOE_TILE
    n_tiles = n_rows // MOE_TILE
    tok = jnp.arange(t, dtype=jnp.int32)
    src = (jnp.arange(n_rows, dtype=jnp.int32) % t).at[dest].set(jnp.concatenate([tok, tok]))
    tile_id = jnp.arange(n_tiles, dtype=jnp.int32)
    tile_expert = jnp.sum(tile_id[:, None] >= tile_end[None, :], axis=1)
    used = tile_expert < N_EXPERTS
    last_expert = jnp.sum(tile_end[-1] - 1 >= tile_end)
    tile_expert = jnp.where(used, tile_expert, last_expert).astype(jnp.int32)
    tile_valid = jnp.clip(counts[tile_expert] - (tile_id - tile_start[tile_expert]) * MOE_TILE, 0, MOE_TILE)
    tile_valid = jnp.where(used, tile_valid, 0).astype(jnp.int32)
    xs = _sc_gather_rows(xf, src)
    y = _experts(tile_expert, tile_valid, xs, w1, w3, w2)
    yg = _sc_gather_rows(y, dest)
    return _combine(yg, xf, ow, g, b)


def _prep_layer(l, p, in_cols):
    bf = jnp.bfloat16
    out = {}
    out["w_in"] = _take_cols(p["w_in"][l], in_cols).astype(bf)
    deint = np.concatenate([np.arange(0, HEAD_DIM, 2), np.arange(1, HEAD_DIM, 2)])
    out["gq_a"] = jnp.tile(p["a_q_norm"][l][deint], 2).reshape(1, LANES)
    out["gk_a"] = jnp.tile(p["a_k_norm"][l][deint], 2).reshape(1, LANES)
    per_head = C_NOPE + C_ROPE
    uq_cols = []
    for h in range(C_HEADS):
        base = h * per_head
        uq_cols += [base + np.arange(C_NOPE), base + C_NOPE + np.arange(0, C_ROPE, 2),
                    base + C_NOPE + np.arange(1, C_ROPE, 2), np.full(32, -1)]
    wuq = _take_cols(p["c_w_uq"][l], np.concatenate(uq_cols))
    out["wuq"] = jnp.concatenate([wuq, jnp.zeros((64, wuq.shape[1]), wuq.dtype)], axis=0).astype(bf)
    uk_cols, uv_cols = [], []
    for h in range(C_HEADS):
        base = h * (C_NOPE + C_V)
        uk_cols += [base + np.arange(C_NOPE), np.full(64, -1)]
        uv_cols += [base + C_NOPE + np.arange(C_V)]
    out["wuk"] = _take_cols(p["c_w_ukv"][l], np.concatenate(uk_cols)).astype(bf)
    out["wuv"] = _take_cols(p["c_w_ukv"][l], np.concatenate(uv_cols)).astype(bf)
    out["gq_c"] = jnp.concatenate([p["c_q_norm"][l], jnp.zeros((64,), jnp.float32)]).reshape(1, 256)
    out["gkv_c"] = p["c_kv_norm"][l].reshape(1, C_KV_RANK)
    out["conv_w"] = jnp.concatenate([p["d_conv_w"][l][:, 0, :], jnp.zeros((1, D_CH), jnp.float32)], axis=0)
    out["conv_b"] = p["d_conv_b"][l].reshape(1, D_CH)
    out["d_ln_g"] = p["d_ln_g"][l].reshape(1, D_CH)
    out["d_ln_b"] = p["d_ln_b"][l].reshape(1, D_CH)
    a_rows = np.concatenate([(kvh * 2 + g) * HEAD_DIM + np.arange(HEAD_DIM)
                             for g in range(2) for kvh in range(A_KV_HEADS)])
    wb = p["w_branch"][l]
    out["w_branch"] = jnp.stack([wb[0][a_rows], wb[1], wb[2], wb[3]], axis=0).astype(bf)
    out["w_out"] = p["w_out"][l].astype(bf)
    return out


def kernel(x, ln_in_g, ln_in_b, w_in, a_q_norm, a_k_norm, b_rpb, c_q_norm, c_kv_norm, c_w_uq, c_w_ukv,
           d_conv_w, d_conv_b, d_ln_g, d_ln_b, w_branch, w_out, ln_mix_g, ln_mix_b,
           ffn_w1, ffn_w3, ffn_w2, moe_router, moe_w1, moe_w3, moe_w2, ln_ffn_g, ln_ffn_b):
    batch, seq, d = x.shape
    t = batch * seq
    bf = jnp.bfloat16
    params = dict(w_in=w_in, a_q_norm=a_q_norm, a_k_norm=a_k_norm, c_q_norm=c_q_norm, c_kv_norm=c_kv_norm,
                  c_w_uq=c_w_uq, c_w_ukv=c_w_ukv, d_conv_w=d_conv_w, d_conv_b=d_conv_b, d_ln_g=d_ln_g,
                  d_ln_b=d_ln_b, w_branch=w_branch, w_out=w_out)
    in_cols = _in_proj_columns()
    cos_a, sin_a, cos_c, sin_c = _rope_tables(seq)
    xf, xb = _input_ln(x.reshape(t, d), ln_in_g, ln_in_b)
    for l in range(DEPTH):
        p = _prep_layer(l, params, in_cols)
        h = _matmul(xb, p["w_in"], bf, tm=1024, tn=1280)
        ya = _attn_a(h, cos_a, sin_a, p["gq_a"], p["gk_a"], batch, seq)
        yb = _attn_b(h, _na_bias_tables(b_rpb[l], seq // GRID_W), batch, seq)
        yc = _attn_c(h, p["wuq"], p["wuk"], p["wuv"], p["gq_c"], p["gkv_c"], cos_c, sin_c, batch, seq)
        yd = _conv_branch(h, p["conv_w"], p["conv_b"], p["d_ln_g"], p["d_ln_b"], batch, seq)
        xf, xb = _merge(h, ya, yb, yc, yd, xf, p["w_branch"], p["w_out"], ln_mix_g[l], ln_mix_b[l])
        if l % 2 == 0:
            i = l // 2
            xf, xb = _ffn(xb, xf, ffn_w1[i].astype(bf), ffn_w3[i].astype(bf), ffn_w2[i].astype(bf),
                          ln_ffn_g[l], ln_ffn_b[l])
        else:
            i = l // 2
            xf = _moe_ffn(xf, moe_router[i], moe_w1[i], moe_w3[i], moe_w2[i], ln_ffn_g[l], ln_ffn_b[l])
            xb = xf.astype(bf)
    return xf.reshape(batch, seq, d)
```

```python
import functools

import numpy as np
import jax
import jax.numpy as jnp
from jax import lax
from jax.experimental import pallas as pl
from jax.experimental.pallas import tpu as pltpu
from jax.experimental.pallas import tpu_sc as plsc

D_MODEL = 1024
DEPTH = 2
GRID_W = 64
HEAD_DIM = 64
ROPE_THETA = 10000.0
RMS_EPS = 1e-6
LN_EPS = 1e-5
A_HEADS = 4
A_KV_HEADS = 2
B_HEADS = 4
NA_WIN_H = 8
NA_WIN_W = 16
C_HEADS = 4
C_NOPE = 64
C_ROPE = 32
C_V = 64
C_Q_RANK = 192
C_KV_RANK = 128
D_CH = 256
CONV_W = 31
N_BRANCH = 4
MIX_W = 256
D_FF = 2816
N_EXPERTS = 8
TOP_K = 2
D_FF_EXPERT = 3584
ALPHA = (2 * DEPTH) ** 0.25

LANES = 128
VMEM_LIMIT = 48 * 1024 * 1024

LOG2E = 1.4426950408889634
NEG_BIG = -1e30

COL_GATES = 0
COL_AQ = 4096
COL_AKV = 4352
COL_C = 4608
COL_D = 5120
COL_BQ = 5632
COL_BK = 5888
COL_BV = 6144
D_IN_PAD = 6400

NA_QROWS = 4
NA_KROWS = 12

MOE_TILE = 1024
MOE_SUB = 512
MOE_FCHUNK = 896
MOE_VMEM_LIMIT = 56 * 1024 * 1024


def _in_proj_columns():
    o_aq, o_ak, o_av = 0, 256, 384
    o_bq, o_bk, o_bv = 512, 768, 1024
    o_cq, o_ckv, o_kpe = 1280, 1472, 1600
    o_d, o_g = 1632, 2144
    deint = np.concatenate([np.arange(0, HEAD_DIM, 2), np.arange(1, HEAD_DIM, 2)])
    cols = [o_g + np.arange(N_BRANCH * D_MODEL)]
    for g in range(2):
        for kvh in range(A_KV_HEADS):
            cols.append(o_aq + (kvh * 2 + g) * HEAD_DIM + deint)
    for kvh in range(A_KV_HEADS):
        cols.append(o_ak + kvh * HEAD_DIM + deint)
    cols.append(o_av + np.arange(A_KV_HEADS * HEAD_DIM))
    cols.append(o_cq + np.arange(C_Q_RANK))
    cols.append(np.full(64, -1))
    cols.append(o_ckv + np.arange(C_KV_RANK))
    cols.append(np.full(64, -1))
    cols.append(o_kpe + np.arange(0, C_ROPE, 2))
    cols.append(o_kpe + np.arange(1, C_ROPE, 2))
    cols.append(np.full(32, -1))
    cols.append(o_d + np.arange(2 * D_CH))
    cols.append(o_bq + np.arange(3 * B_HEADS * HEAD_DIM))
    cols = np.concatenate(cols)
    assert cols.shape[0] == D_IN_PAD
    return cols


def _take_cols(w, cols):
    cols = np.asarray(cols)
    parts, i, n = [], 0, len(cols)
    while i < n:
        j = i + 1
        if cols[i] < 0:
            while j < n and cols[j] < 0:
                j += 1
            parts.append(jnp.zeros((w.shape[0], j - i), w.dtype))
        else:
            step = int(cols[j] - cols[i]) if j < n and cols[j] > cols[i] else 1
            while j < n and cols[j] == cols[j - 1] + step:
                j += 1
            c0, n_run = int(cols[i]), j - i
            if step == 1:
                parts.append(w[:, c0:c0 + n_run])
            else:
                assert step == 2
                base = c0 - c0 % 2
                parts.append(w[:, base:base + 2 * n_run].reshape(w.shape[0], n_run, 2)[:, :, c0 % 2])
        i = j
    return jnp.concatenate(parts, axis=1)


def _rope_tables(seq):
    t = np.arange(seq)
    row = (t // GRID_W).astype(np.float32)
    col = (t % GRID_W).astype(np.float32)

    def angles(rot_dim):
        n_freq = rot_dim // 4
        inv = jnp.asarray(ROPE_THETA, jnp.float32) ** (-jnp.arange(n_freq, dtype=jnp.float32) / n_freq)
        return jnp.concatenate([jnp.asarray(row)[:, None] * inv, jnp.asarray(col)[:, None] * inv], axis=-1)

    ang_a = angles(HEAD_DIM)
    ca, sa = jnp.cos(ang_a), jnp.sin(ang_a)
    cos_a = jnp.concatenate([ca, ca, ca, ca], axis=-1)
    sin_a = jnp.concatenate([-sa, sa, -sa, sa], axis=-1)
    ang_c = angles(C_ROPE)
    cc, sc = jnp.cos(ang_c), jnp.sin(ang_c)
    one = jnp.ones((seq, 64), jnp.float32)
    zero = jnp.zeros((seq, 64), jnp.float32)
    cos_c = jnp.concatenate([one, cc, cc, one[:, :32]], axis=-1)
    sin_c = jnp.concatenate([zero, -sc, sc, zero[:, :32]], axis=-1)
    return cos_a, sin_a, cos_c, sin_c


def _na_bias_tables(rpb, rows):
    wh = min(NA_WIN_H, rows)
    n_dr, n_dc = 2 * NA_WIN_H - 1, 2 * NA_WIN_W - 1
    qc = np.arange(GRID_W)[:, None]
    kc = np.arange(GRID_W)[None, :]
    col_start = np.clip(qc - NA_WIN_W // 2, 0, GRID_W - NA_WIN_W)
    ok_c = (kc >= col_start) & (kc < col_start + NA_WIN_W)
    dc = np.clip(kc - qc + (NA_WIN_W - 1), 0, n_dc - 1)
    oh_c = np.eye(n_dc, dtype=np.float32)[dc.reshape(-1)]
    oh_r, ok_r = [], []
    for r0 in (0, NA_QROWS, rows - NA_QROWS):
        ws = int(np.clip(r0 - 4, 0, rows - NA_KROWS))
        qr = r0 + np.arange(NA_QROWS)[:, None]
        kr = ws + np.arange(NA_KROWS)[None, :]
        row_start = np.clip(qr - wh // 2, 0, rows - wh)
        ok_r.append((kr >= row_start) & (kr < row_start + wh))
        dr = np.clip(kr - qr + (NA_WIN_H - 1), 0, n_dr - 1)
        oh_r.append(np.eye(n_dr, dtype=np.float32)[dr.reshape(-1)])
    oh_r = np.concatenate(oh_r, axis=0)
    ok = np.stack(ok_r)[:, :, None, :, None] & ok_c[None, None, :, None, :]
    ok = ok.reshape(3, 1, NA_QROWS * GRID_W, NA_KROWS * GRID_W)
    hi = lax.Precision.HIGHEST
    t1 = jnp.einsum("ma,hab->hmb", jnp.asarray(oh_r), rpb.astype(jnp.float32), precision=hi)
    t2 = jnp.einsum("hmb,nb->hmn", t1, jnp.asarray(oh_c), precision=hi)
    nh = rpb.shape[0]
    t2 = t2.reshape(nh, 3, NA_QROWS, NA_KROWS, GRID_W, GRID_W).transpose(1, 0, 2, 4, 3, 5)
    bias = t2.reshape(3, nh, NA_QROWS * GRID_W, NA_KROWS * GRID_W)
    return jnp.where(jnp.asarray(ok), bias * LOG2E, NEG_BIG)


def _cparams(sem):
    return pltpu.CompilerParams(dimension_semantics=sem, vmem_limit_bytes=VMEM_LIMIT)


def _layer_norm_rows(z, g, b):
    mu = jnp.mean(z, axis=-1, keepdims=True)
    zc = z - mu
    var = jnp.mean(zc * zc, axis=-1, keepdims=True)
    return zc * lax.rsqrt(var + LN_EPS) * g + b


def _sigmoid(x):
    return 0.5 * jnp.tanh(0.5 * x) + 0.5


def _dot(a, b):
    return jnp.dot(a, b, preferred_element_type=jnp.float32)


def _dot_nt(a, b):
    return lax.dot_general(a, b, (((1,), (1,)), ((), ())), preferred_element_type=jnp.float32)


def _ln_kernel(x_ref, g_ref, b_ref, of_ref, ob_ref):
    y = _layer_norm_rows(x_ref[...], g_ref[...], b_ref[...])
    of_ref[...] = y
    ob_ref[...] = y.astype(jnp.bfloat16)


def _input_ln(x, g, b, tm=1024):
    t, d = x.shape
    return pl.pallas_call(
        _ln_kernel,
        out_shape=(jax.ShapeDtypeStruct((t, d), jnp.float32), jax.ShapeDtypeStruct((t, d), jnp.bfloat16)),
        grid=(t // tm,),
        in_specs=[pl.BlockSpec((tm, d), lambda i: (i, 0)),
                  pl.BlockSpec((1, d), lambda i: (0, 0)),
                  pl.BlockSpec((1, d), lambda i: (0, 0))],
        out_specs=(pl.BlockSpec((tm, d), lambda i: (i, 0)), pl.BlockSpec((tm, d), lambda i: (i, 0))),
        compiler_params=_cparams(("parallel",)),
    )(x, g.reshape(1, d), b.reshape(1, d))


def _mm_kernel(a_ref, b_ref, o_ref):
    o_ref[...] = _dot(a_ref[...], b_ref[...]).astype(o_ref.dtype)


def _matmul(a, b, out_dtype, tm, tn):
    m, k = a.shape
    n = b.shape[1]
    return pl.pallas_call(
        _mm_kernel,
        out_shape=jax.ShapeDtypeStruct((m, n), out_dtype),
        grid=(m // tm, n // tn),
        in_specs=[pl.BlockSpec((tm, k), lambda i, j: (i, 0)),
                  pl.BlockSpec((k, tn), lambda i, j: (0, j))],
        out_specs=pl.BlockSpec((tm, tn), lambda i, j: (i, j)),
        compiler_params=_cparams(("parallel", "parallel")),
    )(a, b)


def _ones_halves(vblk):
    lane = lax.broadcasted_iota(jnp.int32, vblk.shape, 1)
    ones = jnp.ones_like(vblk)
    return jnp.where(lane < HEAD_DIM, vblk, ones), jnp.where(lane < HEAD_DIM, ones, vblk)


def _attend_pair(s_lo, s_hi, v_lo, v_hi, lo_q):
    outs = []
    for s, v in ((s_lo, v_lo), (s_hi, v_hi)):
        p = jnp.exp2(s - jnp.max(s, axis=-1, keepdims=True)).astype(jnp.bfloat16)
        outs.append(_dot(p, v))
    num = jnp.where(lo_q, outs[0], outs[1])
    den = jnp.where(lo_q, pltpu.roll(outs[0], HEAD_DIM, 1), pltpu.roll(outs[1], HEAD_DIM, 1))
    return num / den


def _rotate_pairs(x, cos, sin, half, lane):
    w = x.shape[-1]
    first = (lane % (2 * half)) < half
    partner = jnp.where(first, pltpu.roll(x, w - half, 1), pltpu.roll(x, half, 1))
    return x * cos + partner * sin


def _attn_a_kernel(q_ref, kv_ref, cos_ref, sin_ref, gq_ref, gk_ref, o_ref, ks_ref, va_ref, *, tq):
    s_len = q_ref.shape[0]

    def prep(x, g, cos, sin):
        lane = lax.broadcasted_iota(jnp.int32, x.shape, 1)
        lo = lane < HEAD_DIM
        x2 = x * x
        ss_lo = jnp.sum(jnp.where(lo, x2, 0.0), axis=-1, keepdims=True)
        ss_hi = jnp.sum(jnp.where(lo, 0.0, x2), axis=-1, keepdims=True)
        ms = jnp.where(lo, ss_lo, ss_hi) * (1.0 / HEAD_DIM)
        xn = x * lax.rsqrt(ms + RMS_EPS) * g
        return _rotate_pairs(xn, cos, sin, HEAD_DIM // 2, lane)

    ks_ref[...] = prep(kv_ref[:, 0:LANES].astype(jnp.float32), gk_ref[...],
                       cos_ref[...], sin_ref[...]).astype(jnp.bfloat16)
    scale = HEAD_DIM ** -0.5 * LOG2E
    lane_q = lax.broadcasted_iota(jnp.int32, (tq, LANES), 1)
    lo_q = lane_q < HEAD_DIM

    va_ref[0], va_ref[1] = _ones_halves(kv_ref[:, LANES:2 * LANES])

    def body(i, carry):
        r = pl.multiple_of(i * tq, tq)
        k = ks_ref[...]
        cos = cos_ref[pl.ds(r, tq), :]
        sin = sin_ref[pl.ds(r, tq), :]
        for g in range(2):
            qt = prep(q_ref[pl.ds(r, tq), g * LANES:(g + 1) * LANES].astype(jnp.float32), gq_ref[...], cos, sin)
            qt = (qt * scale).astype(jnp.bfloat16)
            zero = jnp.zeros_like(qt)
            out = _attend_pair(_dot_nt(jnp.where(lo_q, qt, zero), k), _dot_nt(jnp.where(lo_q, zero, qt), k),
                               va_ref[0], va_ref[1], lo_q)
            o_ref[pl.ds(r, tq), g * LANES:(g + 1) * LANES] = out.astype(o_ref.dtype)
        return carry

    lax.fori_loop(0, s_len // tq, body, 0)


def _attn_a(h, cos_a, sin_a, gq, gk, batch, seq, tq=256):
    kern = functools.partial(_attn_a_kernel, tq=tq)
    return pl.pallas_call(
        kern,
        out_shape=jax.ShapeDtypeStruct((batch * seq, MIX_W), jnp.bfloat16),
        grid=(batch,),
        in_specs=[pl.BlockSpec((seq, 256), lambda b: (b, COL_AQ // 256)),
                  pl.BlockSpec((seq, 256), lambda b: (b, COL_AKV // 256)),
                  pl.BlockSpec((seq, LANES), lambda b: (0, 0)),
                  pl.BlockSpec((seq, LANES), lambda b: (0, 0)),
                  pl.BlockSpec((1, LANES), lambda b: (0, 0)),
                  pl.BlockSpec((1, LANES), lambda b: (0, 0))],
        out_specs=pl.BlockSpec((seq, MIX_W), lambda b: (b, 0)),
        scratch_shapes=[pltpu.VMEM((seq, LANES), jnp.bfloat16), pltpu.VMEM((2, seq, LANES), jnp.bfloat16)],
        compiler_params=_cparams(("parallel",)),
    )(h, h, cos_a, sin_a, gq, gk)


def _attn_b_kernel(q_ref, k_ref, v_ref, bias_ref, o_ref, va_ref, *, rows):
    nq = NA_QROWS * GRID_W
    nk = NA_KROWS * GRID_W
    lane_q = lax.broadcasted_iota(jnp.int32, (nq, LANES), 1)
    lo_q = lane_q < HEAD_DIM
    n_groups = rows // NA_QROWS
    scale = HEAD_DIM ** -0.5 * LOG2E
    for blk in range(B_HEADS // 2):
        va_ref[2 * blk], va_ref[2 * blk + 1] = _ones_halves(v_ref[:, blk * LANES:(blk + 1) * LANES])

    def body(gi, carry):
        ws = jnp.clip(gi * NA_QROWS - 4, 0, rows - NA_KROWS)
        cls = jnp.where(gi == 0, 0, jnp.where(gi == n_groups - 1, 2, 1))
        qr = pl.multiple_of(gi * nq, nq)
        kr = pl.multiple_of(ws * GRID_W, GRID_W)
        for blk in range(B_HEADS // 2):
            sl = slice(blk * LANES, (blk + 1) * LANES)
            qt = q_ref[pl.ds(qr, nq), sl] * scale
            kw = k_ref[pl.ds(kr, nk), sl]
            zero = jnp.zeros_like(qt)
            s_lo = _dot_nt(jnp.where(lo_q, qt, zero), kw) + bias_ref[cls, 2 * blk]
            s_hi = _dot_nt(jnp.where(lo_q, zero, qt), kw) + bias_ref[cls, 2 * blk + 1]
            out = _attend_pair(s_lo, s_hi, va_ref[2 * blk, pl.ds(kr, nk), :], va_ref[2 * blk + 1, pl.ds(kr, nk), :], lo_q)
            o_ref[pl.ds(qr, nq), sl] = out.astype(o_ref.dtype)
        return carry

    lax.fori_loop(0, n_groups, body, 0)


def _attn_b(h, bias, batch, seq):
    rows = seq // GRID_W
    kern = functools.partial(_attn_b_kernel, rows=rows)
    return pl.pallas_call(
        kern,
        out_shape=jax.ShapeDtypeStruct((batch * seq, MIX_W), jnp.bfloat16),
        grid=(batch,),
        in_specs=[pl.BlockSpec((seq, 256), lambda b: (b, COL_BQ // 256)),
                  pl.BlockSpec((seq, 256), lambda b: (b, COL_BK // 256)),
                  pl.BlockSpec((seq, 256), lambda b: (b, COL_BV // 256)),
                  pl.BlockSpec(bias.shape, lambda b: (0, 0, 0, 0))],
        out_specs=pl.BlockSpec((seq, MIX_W), lambda b: (b, 0)),
        scratch_shapes=[pltpu.VMEM((B_HEADS, seq, LANES), jnp.bfloat16)],
        compiler_params=_cparams(("parallel",)),
    )(h, h, h, bias)


def _attn_c_kernel(hc_ref, wuq_ref, wuk_ref, wuv_ref, gq_ref, gkv_ref, cos_ref, sin_ref, o_ref,
                   qs_ref, ks_ref, vs_ref, *, tq):
    s_len = hc_ref.shape[0]
    lane = lax.broadcasted_iota(jnp.int32, (s_len, LANES), 1)
    cos = cos_ref[...]
    sin = sin_ref[...]

    cq = hc_ref[:, 0:256].astype(jnp.float32)
    ms = jnp.sum(cq * cq, axis=-1, keepdims=True) * (1.0 / C_Q_RANK)
    cqn = (cq * lax.rsqrt(ms + RMS_EPS) * gq_ref[...]).astype(jnp.bfloat16)
    ckv = hc_ref[:, 256:384].astype(jnp.float32)
    ms = jnp.mean(ckv * ckv, axis=-1, keepdims=True)
    kvn = (ckv * lax.rsqrt(ms + RMS_EPS) * gkv_ref[...]).astype(jnp.bfloat16)
    kpe = _rotate_pairs(hc_ref[:, 384:512].astype(jnp.float32), cos, sin, C_ROPE // 2, lane)
    v_all = _dot(kvn, wuv_ref[...]).astype(jnp.bfloat16)
    for blk in range(C_HEADS // 2):
        vs_ref[2 * blk], vs_ref[2 * blk + 1] = _ones_halves(v_all[:, blk * LANES:(blk + 1) * LANES])
    scale = (C_NOPE + C_ROPE) ** -0.5 * LOG2E
    for h in range(C_HEADS):
        sl = slice(h * LANES, (h + 1) * LANES)
        qh = _rotate_pairs(_dot(cqn, wuq_ref[:, sl]), cos, sin, C_ROPE // 2, lane) * scale
        qs_ref[:, sl] = qh.astype(jnp.bfloat16)
        ks_ref[:, sl] = (_dot(kvn, wuk_ref[:, sl]) + kpe).astype(jnp.bfloat16)

    lane_q = lax.broadcasted_iota(jnp.int32, (tq, LANES), 1)
    lo_q = lane_q < C_V

    def body(i, carry):
        r = pl.multiple_of(i * tq, tq)
        for blk in range(C_HEADS // 2):
            s = []
            for par in range(2):
                sl = slice((2 * blk + par) * LANES, (2 * blk + par + 1) * LANES)
                s.append(_dot_nt(qs_ref[pl.ds(r, tq), sl], ks_ref[:, sl]))
            out = _attend_pair(s[0], s[1], vs_ref[2 * blk], vs_ref[2 * blk + 1], lo_q)
            o_ref[pl.ds(r, tq), blk * LANES:(blk + 1) * LANES] = out.astype(o_ref.dtype)
        return carry

    lax.fori_loop(0, s_len // tq, body, 0)


def _attn_c(h, wuq, wuk, wuv, gq, gkv, cos_c, sin_c, batch, seq, tq=256):
    kern = functools.partial(_attn_c_kernel, tq=tq)
    const = lambda shape: pl.BlockSpec(shape, lambda b: (0,) * len(shape))
    return pl.pallas_call(
        kern,
        out_shape=jax.ShapeDtypeStruct((batch * seq, MIX_W), jnp.bfloat16),
        grid=(batch,),
        in_specs=[pl.BlockSpec((seq, 512), lambda b: (b, COL_C // 512)),
                  const(wuq.shape), const(wuk.shape), const(wuv.shape),
                  const(gq.shape), const(gkv.shape), const(cos_c.shape), const(sin_c.shape)],
        out_specs=pl.BlockSpec((seq, MIX_W), lambda b: (b, 0)),
        scratch_shapes=[pltpu.VMEM((seq, C_HEADS * LANES), jnp.bfloat16),
                        pltpu.VMEM((seq, C_HEADS * LANES), jnp.bfloat16),
                        pltpu.VMEM((C_HEADS, seq, LANES), jnp.bfloat16)],
        compiler_params=_cparams(("parallel",)),
    )(h, wuq, wuk, wuv, gq, gkv, cos_c, sin_c)


CONV_PAD = 16
CONV_ROWS = 128


def _conv_kernel(hd_ref, w_ref, cb_ref, g_ref, b_ref, o_ref, up_ref):
    s_len = hd_ref.shape[0]
    a = hd_ref[:, 0:D_CH].astype(jnp.float32)
    gate = hd_ref[:, D_CH:2 * D_CH].astype(jnp.float32)
    zeros = jnp.zeros((CONV_PAD, D_CH), jnp.float32)
    up_ref[0:CONV_PAD, :] = zeros
    up_ref[CONV_PAD + s_len:2 * CONV_PAD + s_len, :] = zeros
    up_ref[CONV_PAD:CONV_PAD + s_len, :] = a * _sigmoid(gate)
    shift = CONV_PAD - CONV_W // 2
    for c in range(s_len // CONV_ROWS):
        acc = jnp.zeros((CONV_ROWS, D_CH), jnp.float32)
        for j in range(CONV_W):
            start = c * CONV_ROWS + j + shift
            acc = acc + up_ref[start:start + CONV_ROWS, :] * w_ref[j:j + 1, :]
        u = acc + cb_ref[...]
        y = _layer_norm_rows(u, g_ref[...], b_ref[...])
        o_ref[c * CONV_ROWS:(c + 1) * CONV_ROWS, :] = (y * _sigmoid(y)).astype(o_ref.dtype)


def _conv_branch(h, w, cb, g, b, batch, seq):
    const = lambda shape: pl.BlockSpec(shape, lambda i: (0,) * len(shape))
    return pl.pallas_call(
        _conv_kernel,
        out_shape=jax.ShapeDtypeStruct((batch * seq, MIX_W), jnp.bfloat16),
        grid=(batch,),
        in_specs=[pl.BlockSpec((seq, 512), lambda i: (i, COL_D // 512)),
                  const(w.shape), const(cb.shape), const(g.shape), const(b.shape)],
        out_specs=pl.BlockSpec((seq, MIX_W), lambda i: (i, 0)),
        scratch_shapes=[pltpu.VMEM((seq + 2 * CONV_PAD, D_CH), jnp.float32)],
        compiler_params=_cparams(("parallel",)),
    )(h, w, cb, g, b)


def _merge_kernel(gl_ref, ya_ref, yb_ref, yc_ref, yd_ref, x_ref, wb_ref, wo_ref, g_ref, b_ref,
                  of_ref, ob_ref):
    merged = None
    for n, y_ref in enumerate((ya_ref, yb_ref, yc_ref, yd_ref)):
        gate = _sigmoid(gl_ref[:, n * D_MODEL:(n + 1) * D_MODEL].astype(jnp.float32))
        term = gate * _dot(y_ref[...], wb_ref[n])
        merged = term if merged is None else merged + term
    mix = _dot(merged.astype(jnp.bfloat16), wo_ref[...])
    y = _layer_norm_rows(ALPHA * x_ref[...] + mix, g_ref[...], b_ref[...])
    of_ref[...] = y
    ob_ref[...] = y.astype(jnp.bfloat16)


def _merge(h, ya, yb, yc, yd, x, wb, wo, g, b, tm=512):
    t, d = x.shape
    row = lambda w: pl.BlockSpec((tm, w), lambda i: (i, 0))
    const = lambda shape: pl.BlockSpec(shape, lambda i: (0,) * len(shape))
    return pl.pallas_call(
        _merge_kernel,
        out_shape=(jax.ShapeDtypeStruct((t, d), jnp.float32), jax.ShapeDtypeStruct((t, d), jnp.bfloat16)),
        grid=(t // tm,),
        in_specs=[pl.BlockSpec((tm, N_BRANCH * d), lambda i: (i, COL_GATES)),
                  row(MIX_W), row(MIX_W), row(MIX_W), row(MIX_W), row(d),
                  const(wb.shape), const(wo.shape), const((1, d)), const((1, d))],
        out_specs=(row(d), row(d)),
        compiler_params=_cparams(("parallel",)),
    )(h, ya, yb, yc, yd, x, wb, wo, g.reshape(1, d), b.reshape(1, d))


def _ffn_kernel(xb_ref, xf_ref, w1_ref, w3_ref, w2_ref, g_ref, b_ref, of_ref, ob_ref, acc_ref):
    j = pl.program_id(1)
    x = xb_ref[...]
    a = _dot(x, w1_ref[...])
    hid = (a * _sigmoid(a) * _dot(x, w3_ref[...])).astype(jnp.bfloat16)
    part = _dot(hid, w2_ref[...])

    @pl.when(j == 0)
    def _():
        acc_ref[...] = part

    @pl.when(j > 0)
    def _():
        acc_ref[...] += part

    @pl.when(j == pl.num_programs(1) - 1)
    def _():
        y = _layer_norm_rows(ALPHA * xf_ref[...] + acc_ref[...], g_ref[...], b_ref[...])
        of_ref[...] = y
        ob_ref[...] = y.astype(jnp.bfloat16)


def _ffn(xb, xf, w1, w3, w2, g, b, tm=512, tf=1408):
    t, d = xf.shape
    f = w1.shape[1]
    row = pl.BlockSpec((tm, d), lambda i, j: (i, 0))
    vec = pl.BlockSpec((1, d), lambda i, j: (0, 0))
    return pl.pallas_call(
        _ffn_kernel,
        out_shape=(jax.ShapeDtypeStruct((t, d), jnp.float32), jax.ShapeDtypeStruct((t, d), jnp.bfloat16)),
        grid=(t // tm, f // tf),
        in_specs=[row, row,
                  pl.BlockSpec((d, tf), lambda i, j: (0, j)),
                  pl.BlockSpec((d, tf), lambda i, j: (0, j)),
                  pl.BlockSpec((tf, d), lambda i, j: (j, 0)),
                  vec, vec],
        out_specs=(row, row),
        scratch_shapes=[pltpu.VMEM((tm, d), jnp.float32)],
        compiler_params=_cparams(("parallel", "arbitrary")),
    )(xb, xf, w1, w3, w2, g.reshape(1, d), b.reshape(1, d))


def _pack_rows(x):
    half = x.shape[1] // 2
    bits = lax.bitcast_convert_type(x.astype(jnp.bfloat16).astype(jnp.float32), jnp.uint32)
    return (bits[:, :half] >> 16) | (bits[:, half:] & jnp.uint32(0xFFFF0000))


def _unpack_rows(p):
    lo = lax.bitcast_convert_type(p << 16, jnp.float32)
    hi = lax.bitcast_convert_type(p & jnp.uint32(0xFFFF0000), jnp.float32)
    return jnp.concatenate([lo, hi], axis=1)


def _router_kernel(x_ref, r_ref, tri_ref, oi_ref, ow_ref, cnt_ref, xp_ref, carry_ref):
    i = pl.program_id(0)

    @pl.when(i == 0)
    def _():
        carry_ref[...] = jnp.zeros_like(carry_ref)

    x = x_ref[...]
    r = r_ref[...]
    xh = x.astype(jnp.bfloat16)
    xl = (x - xh.astype(jnp.float32)).astype(jnp.bfloat16)
    rh = r.astype(jnp.bfloat16)
    rl = (r - rh.astype(jnp.float32)).astype(jnp.bfloat16)
    logits = (_dot(xh, rh) + _dot(xh, rl)) + (_dot(xl, rh) + _dot(xl, rl))
    tm = x.shape[0]
    lane = lax.broadcasted_iota(jnp.int32, (tm, LANES), 1)
    logits = jnp.where(lane < N_EXPERTS, logits, NEG_BIG)
    m1 = jnp.max(logits, axis=-1, keepdims=True)
    i1 = jnp.min(jnp.where(logits == m1, lane, LANES), axis=-1, keepdims=True)
    rest = jnp.where(lane == i1, NEG_BIG, logits)
    m2 = jnp.max(rest, axis=-1, keepdims=True)
    i2 = jnp.min(jnp.where(rest == m2, lane, LANES), axis=-1, keepdims=True)
    e = jnp.exp(m2 - m1)
    w1 = 1.0 / (1.0 + e)
    w2 = e / (1.0 + e)
    sel1 = lane == i1
    sel2 = lane == i2
    member = jnp.where(sel1 | sel2, 1.0, 0.0)
    before = _dot(tri_ref[...], member.astype(jnp.bfloat16)) + carry_ref[...]
    rank1 = jnp.sum(jnp.where(sel1, before, 0.0), axis=-1, keepdims=True).astype(jnp.int32)
    rank2 = jnp.sum(jnp.where(sel2, before, 0.0), axis=-1, keepdims=True).astype(jnp.int32)
    carry_ref[...] += jnp.sum(member, axis=0, keepdims=True)
    oi_ref[...] = jnp.where(lane == 0, i1, jnp.where(lane == 1, i2, jnp.where(lane == 2, rank1, rank2)))
    ow_ref[...] = jnp.where(lane == 0, w1, w2)
    cnt_ref[...] = carry_ref[...]
    xp_ref[...] = _pack_rows(x)


def _router(xf, router, tm=512):
    t, d = xf.shape
    r_pad = jnp.zeros((d, LANES), jnp.float32).at[:, :N_EXPERTS].set(router.astype(jnp.float32))
    tri = jnp.asarray(np.tril(np.ones((tm, tm), np.float32), -1), jnp.bfloat16)
    return pl.pallas_call(
        _router_kernel,
        out_shape=(jax.ShapeDtypeStruct((t, LANES), jnp.int32),
                   jax.ShapeDtypeStruct((t, LANES), jnp.float32),
                   jax.ShapeDtypeStruct((1, LANES), jnp.float32),
                   jax.ShapeDtypeStruct((t, d // 2), jnp.uint32)),
        grid=(t // tm,),
        in_specs=[pl.BlockSpec((tm, d), lambda i: (i, 0)),
                  pl.BlockSpec((d, LANES), lambda i: (0, 0)),
                  pl.BlockSpec((tm, tm), lambda i: (0, 0))],
        out_specs=(pl.BlockSpec((tm, LANES), lambda i: (i, 0)),
                   pl.BlockSpec((tm, LANES), lambda i: (i, 0)),
                   pl.BlockSpec((1, LANES), lambda i: (0, 0)),
                   pl.BlockSpec((tm, d // 2), lambda i: (i, 0))),
        scratch_shapes=[pltpu.VMEM((1, LANES), jnp.float32)],
        compiler_params=_cparams(("arbitrary",)),
    )(xf, r_pad, tri)


SC_CORES = 2
SC_SUBCORES = 16
SC_WINDOW = 64


def _sc_gather_rows(table, idx):
    m = idx.shape[0]
    d = table.shape[1]
    n_workers = SC_CORES * SC_SUBCORES
    per_w = m // n_workers
    n_win = per_w // SC_WINDOW
    assert per_w * n_workers == m and n_win * SC_WINDOW == per_w
    mesh = plsc.VectorSubcoreMesh(core_axis_name="c", subcore_axis_name="s")

    @functools.partial(
        pl.kernel, mesh=mesh,
        out_type=jax.ShapeDtypeStruct((m, d), table.dtype),
        scratch_types=[pltpu.VMEM((per_w,), jnp.int32),
                       pltpu.VMEM((SC_WINDOW, d), table.dtype),
                       pltpu.SemaphoreType.DMA],
    )
    def gather(table_hbm, idx_hbm, out_hbm, idx_v, rows_v, sem):
        wid = lax.axis_index("s") * SC_CORES + lax.axis_index("c")
        base = wid * per_w
        pltpu.sync_copy(idx_hbm.at[pl.ds(base, per_w)], idx_v)

        @pl.loop(0, n_win)
        def _(w):
            off = pl.multiple_of(w * SC_WINDOW, SC_WINDOW)
            pltpu.async_copy(table_hbm.at[idx_v.at[pl.ds(off, SC_WINDOW)]], rows_v, sem).wait()
            pltpu.sync_copy(rows_v, out_hbm.at[pl.ds(base + off, SC_WINDOW)])

    return gather(table, idx)


def _expert_kernel(te_ref, nv_ref, x_ref, w1_ref, w3_ref, w2_ref, o_ref,
                   w1s_ref, w3s_ref, w2s_ref, xb_ref, acc_ref):
    i = pl.program_id(0)
    j = pl.program_id(1)
    last = pl.num_programs(1) - 1
    n_valid = nv_ref[i]

    @pl.when(n_valid > 0)
    def _():
        w1s_ref[...] = w1_ref[0].astype(jnp.bfloat16)
        w3s_ref[...] = w3_ref[0].astype(jnp.bfloat16)
        w2s_ref[...] = w2_ref[0].astype(jnp.bfloat16)

    for s in range(MOE_TILE // MOE_SUB):
        rows = pl.ds(s * MOE_SUB, MOE_SUB)
        live = n_valid > s * MOE_SUB

        @pl.when(live & (j == 0))
        def _():
            xb_ref[rows, :] = _unpack_rows(x_ref[rows, :]).astype(jnp.bfloat16)

        @pl.when(live)
        def _():
            x = xb_ref[rows, :]
            a = _dot(x, w1s_ref[...])
            hid = (a * _sigmoid(a) * _dot(x, w3s_ref[...])).astype(jnp.bfloat16)
            part = _dot(hid, w2s_ref[...])

            @pl.when(j == 0)
            def _():
                acc_ref[rows, :] = part

            @pl.when(j > 0)
            def _():
                acc_ref[rows, :] += part

        @pl.when(live & (j == last))
        def _():
            o_ref[rows, :] = _pack_rows(acc_ref[rows, :])

        @pl.when(jnp.logical_not(live) & (j == last))
        def _():
            o_ref[rows, :] = jnp.zeros((MOE_SUB, o_ref.shape[1]), o_ref.dtype)


def _experts(tile_expert, tile_valid, xs, w1, w3, w2):
    n_rows = xs.shape[0]
    d, f = w1.shape[1], w1.shape[2]
    nj = f // MOE_FCHUNK

    def chunk(i, j, nv):
        return jnp.where(nv[i] > 0, j, nj - 1)

    return pl.pallas_call(
        _expert_kernel,
        out_shape=jax.ShapeDtypeStruct((n_rows, d // 2), jnp.uint32),
        grid_spec=pltpu.PrefetchScalarGridSpec(
            num_scalar_prefetch=2,
            grid=(n_rows // MOE_TILE, nj),
            in_specs=[pl.BlockSpec((MOE_TILE, d // 2), lambda i, j, te, nv: (i, 0)),
                      pl.BlockSpec((1, d, MOE_FCHUNK), lambda i, j, te, nv: (te[i], 0, chunk(i, j, nv))),
                      pl.BlockSpec((1, d, MOE_FCHUNK), lambda i, j, te, nv: (te[i], 0, chunk(i, j, nv))),
                      pl.BlockSpec((1, MOE_FCHUNK, d), lambda i, j, te, nv: (te[i], chunk(i, j, nv), 0))],
            out_specs=pl.BlockSpec((MOE_TILE, d // 2), lambda i, j, te, nv: (i, 0)),
            scratch_shapes=[pltpu.VMEM((d, MOE_FCHUNK), jnp.bfloat16),
                            pltpu.VMEM((d, MOE_FCHUNK), jnp.bfloat16),
                            pltpu.VMEM((MOE_FCHUNK, d), jnp.bfloat16),
                            pltpu.VMEM((MOE_TILE, d), jnp.bfloat16),
                            pltpu.VMEM((MOE_TILE, d), jnp.float32)],
        ),
        compiler_params=pltpu.CompilerParams(dimension_semantics=("arbitrary", "arbitrary"),
                                             vmem_limit_bytes=MOE_VMEM_LIMIT),
    )(tile_expert, tile_valid, xs, w1, w3, w2)


def _combine_kernel(ya_ref, yb_ref, x_ref, w_ref, g_ref, b_ref, o_ref):
    w = w_ref[...]
    f = w[:, 0:1] * _unpack_rows(ya_ref[...]) + w[:, 1:2] * _unpack_rows(yb_ref[...])
    o_ref[...] = _layer_norm_rows(ALPHA * x_ref[...] + f, g_ref[...], b_ref[...])


def _combine(yg, xf, wts, g, b, tm=512):
    t, d = xf.shape
    nt = t // tm
    return pl.pallas_call(
        _combine_kernel,
        out_shape=jax.ShapeDtypeStruct((t, d), jnp.float32),
        grid=(nt,),
        in_specs=[pl.BlockSpec((tm, d // 2), lambda i: (i, 0)),
                  pl.BlockSpec((tm, d // 2), lambda i: (i + nt, 0)),
                  pl.BlockSpec((tm, d), lambda i: (i, 0)),
                  pl.BlockSpec((tm, LANES), lambda i: (i, 0)),
                  pl.BlockSpec((1, d), lambda i: (0, 0)),
                  pl.BlockSpec((1, d), lambda i: (0, 0))],
        out_specs=pl.BlockSpec((tm, d), lambda i: (i, 0)),
        compiler_params=_cparams(("parallel",)),
    )(yg, yg, xf, wts, g.reshape(1, d), b.reshape(1, d))


def _moe_ffn(xf, router, w1, w3, w2, g, b):
    t, d = xf.shape
    oi, ow, cnt, xp = _router(xf, router)
    e1, e2, rank1, rank2 = oi[:, 0], oi[:, 1], oi[:, 2], oi[:, 3]
    counts = cnt[0, :N_EXPERTS].astype(jnp.int32)
    tiles = (counts + MOE_TILE - 1) // MOE_TILE
    tile_end = jnp.cumsum(tiles)
    tile_start = tile_end - tiles
    offs = tile_start * MOE_TILE
    dest = jnp.concatenate([offs[e1] + rank1, offs[e2] + rank2])
    n_rows = TOP_K * t + N_EXPERTS * MOE_TILE
    n_tiles = n_rows // MOE_TILE
    tok = jnp.arange(t, dtype=jnp.int32)
    src = (jnp.arange(n_rows, dtype=jnp.int32) % t).at[dest].set(jnp.concatenate([tok, tok]))
    tile_id = jnp.arange(n_tiles, dtype=jnp.int32)
    tile_expert = jnp.sum(tile_id[:, None] >= tile_end[None, :], axis=1)
    used = tile_expert < N_EXPERTS
    last_expert = jnp.sum(tile_end[-1] - 1 >= tile_end)
    tile_expert = jnp.where(used, tile_expert, last_expert).astype(jnp.int32)
    tile_valid = jnp.clip(counts[tile_expert] - (tile_id - tile_start[tile_expert]) * MOE_TILE, 0, MOE_TILE)
    tile_valid = jnp.where(used, tile_valid, 0).astype(jnp.int32)
    xs = _sc_gather_rows(xp, src)
    y = _experts(tile_expert, tile_valid, xs, w1, w3, w2)
    yg = _sc_gather_rows(y, dest)
    return _combine(yg, xf, ow, g, b)


def _prep_layer(l, p, in_cols):
    bf = jnp.bfloat16
    out = {}
    out["w_in"] = _take_cols(p["w_in"][l], in_cols).astype(bf)
    deint = np.concatenate([np.arange(0, HEAD_DIM, 2), np.arange(1, HEAD_DIM, 2)])
    out["gq_a"] = jnp.tile(p["a_q_norm"][l][deint], 2).reshape(1, LANES)
    out["gk_a"] = jnp.tile(p["a_k_norm"][l][deint], 2).reshape(1, LANES)
    per_head = C_NOPE + C_ROPE
    uq_cols = []
    for h in range(C_HEADS):
        base = h * per_head
        uq_cols += [base + np.arange(C_NOPE), base + C_NOPE + np.arange(0, C_ROPE, 2),
                    base + C_NOPE + np.arange(1, C_ROPE, 2), np.full(32, -1)]
    wuq = _take_cols(p["c_w_uq"][l], np.concatenate(uq_cols))
    out["wuq"] = jnp.concatenate([wuq, jnp.zeros((64, wuq.shape[1]), wuq.dtype)], axis=0).astype(bf)
    uk_cols, uv_cols = [], []
    for h in range(C_HEADS):
        base = h * (C_NOPE + C_V)
        uk_cols += [base + np.arange(C_NOPE), np.full(64, -1)]
        uv_cols += [base + C_NOPE + np.arange(C_V)]
    out["wuk"] = _take_cols(p["c_w_ukv"][l], np.concatenate(uk_cols)).astype(bf)
    out["wuv"] = _take_cols(p["c_w_ukv"][l], np.concatenate(uv_cols)).astype(bf)
    out["gq_c"] = jnp.concatenate([p["c_q_norm"][l], jnp.zeros((64,), jnp.float32)]).reshape(1, 256)
    out["gkv_c"] = p["c_kv_norm"][l].reshape(1, C_KV_RANK)
    out["conv_w"] = jnp.concatenate([p["d_conv_w"][l][:, 0, :], jnp.zeros((1, D_CH), jnp.float32)], axis=0)
    out["conv_b"] = p["d_conv_b"][l].reshape(1, D_CH)
    out["d_ln_g"] = p["d_ln_g"][l].reshape(1, D_CH)
    out["d_ln_b"] = p["d_ln_b"][l].reshape(1, D_CH)
    a_rows = np.concatenate([(kvh * 2 + g) * HEAD_DIM + np.arange(HEAD_DIM)
                             for g in range(2) for kvh in range(A_KV_HEADS)])
    wb = p["w_branch"][l]
    out["w_branch"] = jnp.stack([wb[0][a_rows], wb[1], wb[2], wb[3]], axis=0).astype(bf)
    out["w_out"] = p["w_out"][l].astype(bf)
    return out


def kernel(x, ln_in_g, ln_in_b, w_in, a_q_norm, a_k_norm, b_rpb, c_q_norm, c_kv_norm, c_w_uq, c_w_ukv,
           d_conv_w, d_conv_b, d_ln_g, d_ln_b, w_branch, w_out, ln_mix_g, ln_mix_b,
           ffn_w1, ffn_w3, ffn_w2, moe_router, moe_w1, moe_w3, moe_w2, ln_ffn_g, ln_ffn_b):
    batch, seq, d = x.shape
    t = batch * seq
    bf = jnp.bfloat16
    params = dict(w_in=w_in, a_q_norm=a_q_norm, a_k_norm=a_k_norm, c_q_norm=c_q_norm, c_kv_norm=c_kv_norm,
                  c_w_uq=c_w_uq, c_w_ukv=c_w_ukv, d_conv_w=d_conv_w, d_conv_b=d_conv_b, d_ln_g=d_ln_g,
                  d_ln_b=d_ln_b, w_branch=w_branch, w_out=w_out)
    in_cols = _in_proj_columns()
    cos_a, sin_a, cos_c, sin_c = _rope_tables(seq)
    xf, xb = _input_ln(x.reshape(t, d), ln_in_g, ln_in_b)
    for l in range(DEPTH):
        p = _prep_layer(l, params, in_cols)
        h = _matmul(xb, p["w_in"], bf, tm=1024, tn=1280)
        ya = _attn_a(h, cos_a, sin_a, p["gq_a"], p["gk_a"], batch, seq)
        yb = _attn_b(h, _na_bias_tables(b_rpb[l], seq // GRID_W), batch, seq)
        yc = _attn_c(h, p["wuq"], p["wuk"], p["wuv"], p["gq_c"], p["gkv_c"], cos_c, sin_c, batch, seq)
        yd = _conv_branch(h, p["conv_w"], p["conv_b"], p["d_ln_g"], p["d_ln_b"], batch, seq)
        xf, xb = _merge(h, ya, yb, yc, yd, xf, p["w_branch"], p["w_out"], ln_mix_g[l], ln_mix_b[l])
        if l % 2 == 0:
            i = l // 2
            xf, xb = _ffn(xb, xf, ffn_w1[i].astype(bf), ffn_w3[i].astype(bf), ffn_w2[i].astype(bf),
                          ln_ffn_g[l], ln_ffn_b[l])
        else:
            i = l // 2
            xf = _moe_ffn(xf, moe_router[i], moe_w1[i], moe_w3[i], moe_w2[i], ln_ffn_g[l], ln_ffn_b[l])
            xb = xf.astype(bf)
    return xf.reshape(batch, seq, d)
```

```python
import functools

import numpy as np
import jax
import jax.numpy as jnp
from jax import lax
from jax.experimental import pallas as pl
from jax.experimental.pallas import tpu as pltpu
from jax.experimental.pallas import tpu_sc as plsc

D_MODEL = 1024
DEPTH = 2
GRID_W = 64
HEAD_DIM = 64
ROPE_THETA = 10000.0
RMS_EPS = 1e-6
LN_EPS = 1e-5
A_HEADS = 4
A_KV_HEADS = 2
B_HEADS = 4
NA_WIN_H = 8
NA_WIN_W = 16
C_HEADS = 4
C_NOPE = 64
C_ROPE = 32
C_V = 64
C_Q_RANK = 192
C_KV_RANK = 128
D_CH = 256
CONV_W = 31
N_BRANCH = 4
MIX_W = 256
D_FF = 2816
N_EXPERTS = 8
TOP_K = 2
D_FF_EXPERT = 3584
ALPHA = (2 * DEPTH) ** 0.25

LANES = 128
SUBLANES = 8
VMEM_LIMIT = 48 * 1024 * 1024

LOG2E = 1.4426950408889634
NEG_BIG = -1e30

COL_GATES = 0
COL_AQ = 4096
COL_AKV = 4352
COL_C = 4608
COL_D = 5120
COL_BQ = 5632
COL_BK = 5888
COL_BV = 6144
D_IN_PAD = 6400

NA_QROWS = 4
NA_KROWS = 12

MOE_TILE = 1024
MOE_SUB = 512
MOE_FCHUNK = 896
MOE_VMEM_LIMIT = 56 * 1024 * 1024


def _in_proj_columns():
    o_aq, o_ak, o_av = 0, 256, 384
    o_bq, o_bk, o_bv = 512, 768, 1024
    o_cq, o_ckv, o_kpe = 1280, 1472, 1600
    o_d, o_g = 1632, 2144
    deint = np.concatenate([np.arange(0, HEAD_DIM, 2), np.arange(1, HEAD_DIM, 2)])
    cols = [o_g + np.arange(N_BRANCH * D_MODEL)]
    for g in range(2):
        for kvh in range(A_KV_HEADS):
            cols.append(o_aq + (kvh * 2 + g) * HEAD_DIM + deint)
    for kvh in range(A_KV_HEADS):
        cols.append(o_ak + kvh * HEAD_DIM + deint)
    cols.append(o_av + np.arange(A_KV_HEADS * HEAD_DIM))
    cols.append(o_cq + np.arange(C_Q_RANK))
    cols.append(np.full(64, -1))
    cols.append(o_ckv + np.arange(C_KV_RANK))
    cols.append(np.full(64, -1))
    cols.append(o_kpe + np.arange(0, C_ROPE, 2))
    cols.append(o_kpe + np.arange(1, C_ROPE, 2))
    cols.append(np.full(32, -1))
    cols.append(o_d + np.arange(2 * D_CH))
    cols.append(o_bq + np.arange(3 * B_HEADS * HEAD_DIM))
    cols = np.concatenate(cols)
    assert cols.shape[0] == D_IN_PAD
    return cols


def _take_cols(w, cols):
    cols = np.asarray(cols)
    parts, i, n = [], 0, len(cols)
    while i < n:
        j = i + 1
        if cols[i] < 0:
            while j < n and cols[j] < 0:
                j += 1
            parts.append(jnp.zeros((w.shape[0], j - i), w.dtype))
        else:
            step = int(cols[j] - cols[i]) if j < n and cols[j] > cols[i] else 1
            while j < n and cols[j] == cols[j - 1] + step:
                j += 1
            c0, n_run = int(cols[i]), j - i
            if step == 1:
                parts.append(w[:, c0:c0 + n_run])
            else:
                assert step == 2
                base = c0 - c0 % 2
                parts.append(w[:, base:base + 2 * n_run].reshape(w.shape[0], n_run, 2)[:, :, c0 % 2])
        i = j
    return jnp.concatenate(parts, axis=1)


def _rope_tables(seq):
    t = np.arange(seq)
    row = (t // GRID_W).astype(np.float32)
    col = (t % GRID_W).astype(np.float32)

    def angles(rot_dim):
        n_freq = rot_dim // 4
        inv = jnp.asarray(ROPE_THETA, jnp.float32) ** (-jnp.arange(n_freq, dtype=jnp.float32) / n_freq)
        return jnp.concatenate([jnp.asarray(row)[:, None] * inv, jnp.asarray(col)[:, None] * inv], axis=-1)

    ang_a = angles(HEAD_DIM)
    ca, sa = jnp.cos(ang_a), jnp.sin(ang_a)
    cos_a = jnp.concatenate([ca, ca, ca, ca], axis=-1)
    sin_a = jnp.concatenate([-sa, sa, -sa, sa], axis=-1)
    ang_c = angles(C_ROPE)
    cc, sc = jnp.cos(ang_c), jnp.sin(ang_c)
    one = jnp.ones((seq, 64), jnp.float32)
    zero = jnp.zeros((seq, 64), jnp.float32)
    cos_c = jnp.concatenate([one, cc, cc, one[:, :32]], axis=-1)
    sin_c = jnp.concatenate([zero, -sc, sc, zero[:, :32]], axis=-1)
    return cos_a, sin_a, cos_c, sin_c


def _na_bias_tables(rpb, rows):
    wh = min(NA_WIN_H, rows)
    n_dr, n_dc = 2 * NA_WIN_H - 1, 2 * NA_WIN_W - 1
    qc = np.arange(GRID_W)[:, None]
    kc = np.arange(GRID_W)[None, :]
    col_start = np.clip(qc - NA_WIN_W // 2, 0, GRID_W - NA_WIN_W)
    ok_c = (kc >= col_start) & (kc < col_start + NA_WIN_W)
    dc = np.clip(kc - qc + (NA_WIN_W - 1), 0, n_dc - 1)
    oh_c = np.eye(n_dc, dtype=np.float32)[dc.reshape(-1)]
    oh_r, ok_r = [], []
    for r0 in (0, NA_QROWS, rows - NA_QROWS):
        ws = int(np.clip(r0 - 4, 0, rows - NA_KROWS))
        qr = r0 + np.arange(NA_QROWS)[:, None]
        kr = ws + np.arange(NA_KROWS)[None, :]
        row_start = np.clip(qr - wh // 2, 0, rows - wh)
        ok_r.append((kr >= row_start) & (kr < row_start + wh))
        dr = np.clip(kr - qr + (NA_WIN_H - 1), 0, n_dr - 1)
        oh_r.append(np.eye(n_dr, dtype=np.float32)[dr.reshape(-1)])
    oh_r = np.concatenate(oh_r, axis=0)
    ok = np.stack(ok_r)[:, :, None, :, None] & ok_c[None, None, :, None, :]
    ok = ok.reshape(3, 1, NA_QROWS * GRID_W, NA_KROWS * GRID_W)
    hi = lax.Precision.HIGHEST
    t1 = jnp.einsum("ma,hab->hmb", jnp.asarray(oh_r), rpb.astype(jnp.float32), precision=hi)
    t2 = jnp.einsum("hmb,nb->hmn", t1, jnp.asarray(oh_c), precision=hi)
    nh = rpb.shape[0]
    t2 = t2.reshape(nh, 3, NA_QROWS, NA_KROWS, GRID_W, GRID_W).transpose(1, 0, 2, 4, 3, 5)
    bias = t2.reshape(3, nh, NA_QROWS * GRID_W, NA_KROWS * GRID_W)
    return jnp.where(jnp.asarray(ok), bias * LOG2E, NEG_BIG)


def _cparams(sem):
    return pltpu.CompilerParams(dimension_semantics=sem, vmem_limit_bytes=VMEM_LIMIT)


def _layer_norm_rows(z, g, b):
    mu = jnp.mean(z, axis=-1, keepdims=True)
    zc = z - mu
    var = jnp.mean(zc * zc, axis=-1, keepdims=True)
    return zc * lax.rsqrt(var + LN_EPS) * g + b


def _sigmoid(x):
    return 0.5 * jnp.tanh(0.5 * x) + 0.5


def _dot(a, b):
    return jnp.dot(a, b, preferred_element_type=jnp.float32)


def _dot_nt(a, b):
    return lax.dot_general(a, b, (((1,), (1,)), ((), ())), preferred_element_type=jnp.float32)


def _ln_kernel(x_ref, g_ref, b_ref, of_ref, ob_ref):
    y = _layer_norm_rows(x_ref[...], g_ref[...], b_ref[...])
    of_ref[...] = y
    ob_ref[...] = y.astype(jnp.bfloat16)


def _input_ln(x, g, b, tm=1024):
    t, d = x.shape
    return pl.pallas_call(
        _ln_kernel,
        out_shape=(jax.ShapeDtypeStruct((t, d), jnp.float32), jax.ShapeDtypeStruct((t, d), jnp.bfloat16)),
        grid=(t // tm,),
        in_specs=[pl.BlockSpec((tm, d), lambda i: (i, 0)),
                  pl.BlockSpec((1, d), lambda i: (0, 0)),
                  pl.BlockSpec((1, d), lambda i: (0, 0))],
        out_specs=(pl.BlockSpec((tm, d), lambda i: (i, 0)), pl.BlockSpec((tm, d), lambda i: (i, 0))),
        compiler_params=_cparams(("parallel",)),
    )(x, g.reshape(1, d), b.reshape(1, d))


def _mm_kernel(a_ref, b_ref, o_ref):
    o_ref[...] = _dot(a_ref[...], b_ref[...]).astype(o_ref.dtype)


def _matmul(a, b, out_dtype, tm, tn):
    m, k = a.shape
    n = b.shape[1]
    return pl.pallas_call(
        _mm_kernel,
        out_shape=jax.ShapeDtypeStruct((m, n), out_dtype),
        grid=(m // tm, n // tn),
        in_specs=[pl.BlockSpec((tm, k), lambda i, j: (i, 0)),
                  pl.BlockSpec((k, tn), lambda i, j: (0, j))],
        out_specs=pl.BlockSpec((tm, tn), lambda i, j: (i, j)),
        compiler_params=_cparams(("parallel", "parallel")),
    )(a, b)


def _ones_halves(vblk):
    lane = lax.broadcasted_iota(jnp.int32, vblk.shape, 1)
    ones = jnp.ones_like(vblk)
    return jnp.where(lane < HEAD_DIM, vblk, ones), jnp.where(lane < HEAD_DIM, ones, vblk)


def _attend_pair(s_lo, s_hi, v_lo, v_hi, lo_q):
    outs = []
    for s, v in ((s_lo, v_lo), (s_hi, v_hi)):
        p = jnp.exp2(s - jnp.max(s, axis=-1, keepdims=True)).astype(jnp.bfloat16)
        outs.append(_dot(p, v))
    num = jnp.where(lo_q, outs[0], outs[1])
    den = jnp.where(lo_q, pltpu.roll(outs[0], HEAD_DIM, 1), pltpu.roll(outs[1], HEAD_DIM, 1))
    return num / den


def _rotate_pairs(x, cos, sin, half, lane):
    w = x.shape[-1]
    first = (lane % (2 * half)) < half
    partner = jnp.where(first, pltpu.roll(x, w - half, 1), pltpu.roll(x, half, 1))
    return x * cos + partner * sin


def _attn_a_kernel(q_ref, kv_ref, cos_ref, sin_ref, gq_ref, gk_ref, o_ref, ks_ref, va_ref, *, tq):
    s_len = q_ref.shape[0]

    def prep(x, g, cos, sin):
        lane = lax.broadcasted_iota(jnp.int32, x.shape, 1)
        lo = lane < HEAD_DIM
        x2 = x * x
        ss_lo = jnp.sum(jnp.where(lo, x2, 0.0), axis=-1, keepdims=True)
        ss_hi = jnp.sum(jnp.where(lo, 0.0, x2), axis=-1, keepdims=True)
        ms = jnp.where(lo, ss_lo, ss_hi) * (1.0 / HEAD_DIM)
        xn = x * lax.rsqrt(ms + RMS_EPS) * g
        return _rotate_pairs(xn, cos, sin, HEAD_DIM // 2, lane)

    ks_ref[...] = prep(kv_ref[:, 0:LANES].astype(jnp.float32), gk_ref[...],
                       cos_ref[...], sin_ref[...]).astype(jnp.bfloat16)
    scale = HEAD_DIM ** -0.5 * LOG2E
    lane_q = lax.broadcasted_iota(jnp.int32, (tq, LANES), 1)
    lo_q = lane_q < HEAD_DIM

    va_ref[0], va_ref[1] = _ones_halves(kv_ref[:, LANES:2 * LANES])

    def body(i, carry):
        r = pl.multiple_of(i * tq, tq)
        k = ks_ref[...]
        cos = cos_ref[pl.ds(r, tq), :]
        sin = sin_ref[pl.ds(r, tq), :]
        for g in range(2):
            qt = prep(q_ref[pl.ds(r, tq), g * LANES:(g + 1) * LANES].astype(jnp.float32), gq_ref[...], cos, sin)
            qt = (qt * scale).astype(jnp.bfloat16)
            zero = jnp.zeros_like(qt)
            out = _attend_pair(_dot_nt(jnp.where(lo_q, qt, zero), k), _dot_nt(jnp.where(lo_q, zero, qt), k),
                               va_ref[0], va_ref[1], lo_q)
            o_ref[pl.ds(r, tq), g * LANES:(g + 1) * LANES] = out.astype(o_ref.dtype)
        return carry

    lax.fori_loop(0, s_len // tq, body, 0)


def _attn_a(h, cos_a, sin_a, gq, gk, batch, seq, tq=256):
    kern = functools.partial(_attn_a_kernel, tq=tq)
    return pl.pallas_call(
        kern,
        out_shape=jax.ShapeDtypeStruct((batch * seq, MIX_W), jnp.bfloat16),
        grid=(batch,),
        in_specs=[pl.BlockSpec((seq, 256), lambda b: (b, COL_AQ // 256)),
                  pl.BlockSpec((seq, 256), lambda b: (b, COL_AKV // 256)),
                  pl.BlockSpec((seq, LANES), lambda b: (0, 0)),
                  pl.BlockSpec((seq, LANES), lambda b: (0, 0)),
                  pl.BlockSpec((1, LANES), lambda b: (0, 0)),
                  pl.BlockSpec((1, LANES), lambda b: (0, 0))],
        out_specs=pl.BlockSpec((seq, MIX_W), lambda b: (b, 0)),
        scratch_shapes=[pltpu.VMEM((seq, LANES), jnp.bfloat16), pltpu.VMEM((2, seq, LANES), jnp.bfloat16)],
        compiler_params=_cparams(("parallel",)),
    )(h, h, cos_a, sin_a, gq, gk)


def _attn_b_kernel(q_ref, k_ref, v_ref, bias_ref, o_ref, va_ref, *, rows):
    nq = NA_QROWS * GRID_W
    nk = NA_KROWS * GRID_W
    lane_q = lax.broadcasted_iota(jnp.int32, (nq, LANES), 1)
    lo_q = lane_q < HEAD_DIM
    n_groups = rows // NA_QROWS
    scale = HEAD_DIM ** -0.5 * LOG2E
    for blk in range(B_HEADS // 2):
        va_ref[2 * blk], va_ref[2 * blk + 1] = _ones_halves(v_ref[:, blk * LANES:(blk + 1) * LANES])

    def body(gi, carry):
        ws = jnp.clip(gi * NA_QROWS - 4, 0, rows - NA_KROWS)
        cls = jnp.where(gi == 0, 0, jnp.where(gi == n_groups - 1, 2, 1))
        qr = pl.multiple_of(gi * nq, nq)
        kr = pl.multiple_of(ws * GRID_W, GRID_W)
        for blk in range(B_HEADS // 2):
            sl = slice(blk * LANES, (blk + 1) * LANES)
            qt = q_ref[pl.ds(qr, nq), sl] * scale
            kw = k_ref[pl.ds(kr, nk), sl]
            zero = jnp.zeros_like(qt)
            s_lo = _dot_nt(jnp.where(lo_q, qt, zero), kw) + bias_ref[cls, 2 * blk]
            s_hi = _dot_nt(jnp.where(lo_q, zero, qt), kw) + bias_ref[cls, 2 * blk + 1]
            out = _attend_pair(s_lo, s_hi, va_ref[2 * blk, pl.ds(kr, nk), :], va_ref[2 * blk + 1, pl.ds(kr, nk), :], lo_q)
            o_ref[pl.ds(qr, nq), sl] = out.astype(o_ref.dtype)
        return carry

    lax.fori_loop(0, n_groups, body, 0)


def _attn_b(h, bias, batch, seq):
    rows = seq // GRID_W
    kern = functools.partial(_attn_b_kernel, rows=rows)
    return pl.pallas_call(
        kern,
        out_shape=jax.ShapeDtypeStruct((batch * seq, MIX_W), jnp.bfloat16),
        grid=(batch,),
        in_specs=[pl.BlockSpec((seq, 256), lambda b: (b, COL_BQ // 256)),
                  pl.BlockSpec((seq, 256), lambda b: (b, COL_BK // 256)),
                  pl.BlockSpec((seq, 256), lambda b: (b, COL_BV // 256)),
                  pl.BlockSpec(bias.shape, lambda b: (0, 0, 0, 0))],
        out_specs=pl.BlockSpec((seq, MIX_W), lambda b: (b, 0)),
        scratch_shapes=[pltpu.VMEM((B_HEADS, seq, LANES), jnp.bfloat16)],
        compiler_params=_cparams(("parallel",)),
    )(h, h, h, bias)


def _attn_c_kernel(hc_ref, wuq_ref, wuk_ref, wuv_ref, gq_ref, gkv_ref, cos_ref, sin_ref, o_ref,
                   qs_ref, ks_ref, vs_ref, *, tq):
    s_len = hc_ref.shape[0]
    lane = lax.broadcasted_iota(jnp.int32, (s_len, LANES), 1)
    cos = cos_ref[...]
    sin = sin_ref[...]

    cq = hc_ref[:, 0:256].astype(jnp.float32)
    ms = jnp.sum(cq * cq, axis=-1, keepdims=True) * (1.0 / C_Q_RANK)
    cqn = (cq * lax.rsqrt(ms + RMS_EPS) * gq_ref[...]).astype(jnp.bfloat16)
    ckv = hc_ref[:, 256:384].astype(jnp.float32)
    ms = jnp.mean(ckv * ckv, axis=-1, keepdims=True)
    kvn = (ckv * lax.rsqrt(ms + RMS_EPS) * gkv_ref[...]).astype(jnp.bfloat16)
    kpe = _rotate_pairs(hc_ref[:, 384:512].astype(jnp.float32), cos, sin, C_ROPE // 2, lane)
    v_all = _dot(kvn, wuv_ref[...]).astype(jnp.bfloat16)
    for blk in range(C_HEADS // 2):
        vs_ref[2 * blk], vs_ref[2 * blk + 1] = _ones_halves(v_all[:, blk * LANES:(blk + 1) * LANES])
    scale = (C_NOPE + C_ROPE) ** -0.5 * LOG2E
    for h in range(C_HEADS):
        sl = slice(h * LANES, (h + 1) * LANES)
        qh = _rotate_pairs(_dot(cqn, wuq_ref[:, sl]), cos, sin, C_ROPE // 2, lane) * scale
        qs_ref[:, sl] = qh.astype(jnp.bfloat16)
        ks_ref[:, sl] = (_dot(kvn, wuk_ref[:, sl]) + kpe).astype(jnp.bfloat16)

    lane_q = lax.broadcasted_iota(jnp.int32, (tq, LANES), 1)
    lo_q = lane_q < C_V

    def body(i, carry):
        r = pl.multiple_of(i * tq, tq)
        for blk in range(C_HEADS // 2):
            s = []
            for par in range(2):
                sl = slice((2 * blk + par) * LANES, (2 * blk + par + 1) * LANES)
                s.append(_dot_nt(qs_ref[pl.ds(r, tq), sl], ks_ref[:, sl]))
            out = _attend_pair(s[0], s[1], vs_ref[2 * blk], vs_ref[2 * blk + 1], lo_q)
            o_ref[pl.ds(r, tq), blk * LANES:(blk + 1) * LANES] = out.astype(o_ref.dtype)
        return carry

    lax.fori_loop(0, s_len // tq, body, 0)


def _attn_c(h, wuq, wuk, wuv, gq, gkv, cos_c, sin_c, batch, seq, tq=256):
    kern = functools.partial(_attn_c_kernel, tq=tq)
    const = lambda shape: pl.BlockSpec(shape, lambda b: (0,) * len(shape))
    return pl.pallas_call(
        kern,
        out_shape=jax.ShapeDtypeStruct((batch * seq, MIX_W), jnp.bfloat16),
        grid=(batch,),
        in_specs=[pl.BlockSpec((seq, 512), lambda b: (b, COL_C // 512)),
                  const(wuq.shape), const(wuk.shape), const(wuv.shape),
                  const(gq.shape), const(gkv.shape), const(cos_c.shape), const(sin_c.shape)],
        out_specs=pl.BlockSpec((seq, MIX_W), lambda b: (b, 0)),
        scratch_shapes=[pltpu.VMEM((seq, C_HEADS * LANES), jnp.bfloat16),
                        pltpu.VMEM((seq, C_HEADS * LANES), jnp.bfloat16),
                        pltpu.VMEM((C_HEADS, seq, LANES), jnp.bfloat16)],
        compiler_params=_cparams(("parallel",)),
    )(h, wuq, wuk, wuv, gq, gkv, cos_c, sin_c)


CONV_PAD = 16
CONV_ROWS = 128


def _conv_kernel(hd_ref, w_ref, cb_ref, g_ref, b_ref, o_ref, up_ref):
    s_len = hd_ref.shape[0]
    a = hd_ref[:, 0:D_CH].astype(jnp.float32)
    gate = hd_ref[:, D_CH:2 * D_CH].astype(jnp.float32)
    zeros = jnp.zeros((CONV_PAD, D_CH), jnp.float32)
    up_ref[0, 0:CONV_PAD, :] = zeros
    up_ref[0, CONV_PAD + s_len:2 * CONV_PAD + s_len, :] = zeros
    up_ref[0, CONV_PAD:CONV_PAD + s_len, :] = a * _sigmoid(gate)
    n_shifted = s_len + 2 * CONV_PAD - SUBLANES
    for s in range(1, SUBLANES):
        up_ref[s, 0:n_shifted, :] = up_ref[0, s:s + n_shifted, :]
    shift = CONV_PAD - CONV_W // 2
    for c in range(s_len // CONV_ROWS):
        acc = jnp.zeros((CONV_ROWS, D_CH), jnp.float32)
        for j in range(CONV_W):
            s = (j + shift) % SUBLANES
            start = c * CONV_ROWS + j + shift - s
            acc = acc + up_ref[s, start:start + CONV_ROWS, :] * w_ref[j:j + 1, :]
        u = acc + cb_ref[...]
        y = _layer_norm_rows(u, g_ref[...], b_ref[...])
        o_ref[c * CONV_ROWS:(c + 1) * CONV_ROWS, :] = (y * _sigmoid(y)).astype(o_ref.dtype)


def _conv_branch(h, w, cb, g, b, batch, seq):
    const = lambda shape: pl.BlockSpec(shape, lambda i: (0,) * len(shape))
    return pl.pallas_call(
        _conv_kernel,
        out_shape=jax.ShapeDtypeStruct((batch * seq, MIX_W), jnp.bfloat16),
        grid=(batch,),
        in_specs=[pl.BlockSpec((seq, 512), lambda i: (i, COL_D // 512)),
                  const(w.shape), const(cb.shape), const(g.shape), const(b.shape)],
        out_specs=pl.BlockSpec((seq, MIX_W), lambda i: (i, 0)),
        scratch_shapes=[pltpu.VMEM((SUBLANES, seq + 2 * CONV_PAD, D_CH), jnp.float32)],
        compiler_params=_cparams(("parallel",)),
    )(h, w, cb, g, b)


def _merge_kernel(gl_ref, ya_ref, yb_ref, yc_ref, yd_ref, x_ref, wb_ref, wo_ref, g_ref, b_ref,
                  of_ref, ob_ref):
    merged = None
    for n, y_ref in enumerate((ya_ref, yb_ref, yc_ref, yd_ref)):
        gate = _sigmoid(gl_ref[:, n * D_MODEL:(n + 1) * D_MODEL].astype(jnp.float32))
        term = gate * _dot(y_ref[...], wb_ref[n])
        merged = term if merged is None else merged + term
    mix = _dot(merged.astype(jnp.bfloat16), wo_ref[...])
    y = _layer_norm_rows(ALPHA * x_ref[...] + mix, g_ref[...], b_ref[...])
    of_ref[...] = y
    ob_ref[...] = y.astype(jnp.bfloat16)


def _merge(h, ya, yb, yc, yd, x, wb, wo, g, b, tm=512):
    t, d = x.shape
    row = lambda w: pl.BlockSpec((tm, w), lambda i: (i, 0))
    const = lambda shape: pl.BlockSpec(shape, lambda i: (0,) * len(shape))
    return pl.pallas_call(
        _merge_kernel,
        out_shape=(jax.ShapeDtypeStruct((t, d), jnp.float32), jax.ShapeDtypeStruct((t, d), jnp.bfloat16)),
        grid=(t // tm,),
        in_specs=[pl.BlockSpec((tm, N_BRANCH * d), lambda i: (i, COL_GATES)),
                  row(MIX_W), row(MIX_W), row(MIX_W), row(MIX_W), row(d),
                  const(wb.shape), const(wo.shape), const((1, d)), const((1, d))],
        out_specs=(row(d), row(d)),
        compiler_params=_cparams(("parallel",)),
    )(h, ya, yb, yc, yd, x, wb, wo, g.reshape(1, d), b.reshape(1, d))


def _ffn_kernel(xb_ref, xf_ref, w1_ref, w3_ref, w2_ref, g_ref, b_ref, of_ref, ob_ref, acc_ref):
    j = pl.program_id(1)
    x = xb_ref[...]
    a = _dot(x, w1_ref[...])
    hid = (a * _sigmoid(a) * _dot(x, w3_ref[...])).astype(jnp.bfloat16)
    part = _dot(hid, w2_ref[...])

    @pl.when(j == 0)
    def _():
        acc_ref[...] = part

    @pl.when(j > 0)
    def _():
        acc_ref[...] += part

    @pl.when(j == pl.num_programs(1) - 1)
    def _():
        y = _layer_norm_rows(ALPHA * xf_ref[...] + acc_ref[...], g_ref[...], b_ref[...])
        of_ref[...] = y
        ob_ref[...] = y.astype(jnp.bfloat16)


def _ffn(xb, xf, w1, w3, w2, g, b, tm=512, tf=1408):
    t, d = xf.shape
    f = w1.shape[1]
    row = pl.BlockSpec((tm, d), lambda i, j: (i, 0))
    vec = pl.BlockSpec((1, d), lambda i, j: (0, 0))
    return pl.pallas_call(
        _ffn_kernel,
        out_shape=(jax.ShapeDtypeStruct((t, d), jnp.float32), jax.ShapeDtypeStruct((t, d), jnp.bfloat16)),
        grid=(t // tm, f // tf),
        in_specs=[row, row,
                  pl.BlockSpec((d, tf), lambda i, j: (0, j)),
                  pl.BlockSpec((d, tf), lambda i, j: (0, j)),
                  pl.BlockSpec((tf, d), lambda i, j: (j, 0)),
                  vec, vec],
        out_specs=(row, row),
        scratch_shapes=[pltpu.VMEM((tm, d), jnp.float32)],
        compiler_params=_cparams(("parallel", "arbitrary")),
    )(xb, xf, w1, w3, w2, g.reshape(1, d), b.reshape(1, d))


def _pack_rows(x):
    half = x.shape[1] // 2
    bits = lax.bitcast_convert_type(x.astype(jnp.bfloat16).astype(jnp.float32), jnp.uint32)
    return (bits[:, :half] >> 16) | (bits[:, half:] & jnp.uint32(0xFFFF0000))


def _unpack_rows(p):
    lo = lax.bitcast_convert_type(p << 16, jnp.float32)
    hi = lax.bitcast_convert_type(p & jnp.uint32(0xFFFF0000), jnp.float32)
    return jnp.concatenate([lo, hi], axis=1)


def _router_kernel(x_ref, r_ref, tri_ref, oi_ref, ow_ref, cnt_ref, xp_ref, carry_ref):
    i = pl.program_id(0)

    @pl.when(i == 0)
    def _():
        carry_ref[...] = jnp.zeros_like(carry_ref)

    x = x_ref[...]
    r = r_ref[...]
    xh = x.astype(jnp.bfloat16)
    xl = (x - xh.astype(jnp.float32)).astype(jnp.bfloat16)
    rh = r.astype(jnp.bfloat16)
    rl = (r - rh.astype(jnp.float32)).astype(jnp.bfloat16)
    logits = (_dot(xh, rh) + _dot(xh, rl)) + (_dot(xl, rh) + _dot(xl, rl))
    tm = x.shape[0]
    lane = lax.broadcasted_iota(jnp.int32, (tm, LANES), 1)
    logits = jnp.where(lane < N_EXPERTS, logits, NEG_BIG)
    m1 = jnp.max(logits, axis=-1, keepdims=True)
    i1 = jnp.min(jnp.where(logits == m1, lane, LANES), axis=-1, keepdims=True)
    rest = jnp.where(lane == i1, NEG_BIG, logits)
    m2 = jnp.max(rest, axis=-1, keepdims=True)
    i2 = jnp.min(jnp.where(rest == m2, lane, LANES), axis=-1, keepdims=True)
    e = jnp.exp(m2 - m1)
    w1 = 1.0 / (1.0 + e)
    w2 = e / (1.0 + e)
    sel1 = lane == i1
    sel2 = lane == i2
    member = jnp.where(sel1 | sel2, 1.0, 0.0)
    before = _dot(tri_ref[...], member.astype(jnp.bfloat16)) + carry_ref[...]
    rank1 = jnp.sum(jnp.where(sel1, before, 0.0), axis=-1, keepdims=True).astype(jnp.int32)
    rank2 = jnp.sum(jnp.where(sel2, before, 0.0), axis=-1, keepdims=True).astype(jnp.int32)
    carry_ref[...] += jnp.sum(member, axis=0, keepdims=True)
    oi_ref[...] = jnp.where(lane == 0, i1, jnp.where(lane == 1, i2, jnp.where(lane == 2, rank1, rank2)))
    ow_ref[...] = jnp.where(lane == 0, w1, w2)
    cnt_ref[...] = carry_ref[...]
    xp_ref[...] = _pack_rows(x)


def _router(xf, router, tm=512):
    t, d = xf.shape
    r_pad = jnp.zeros((d, LANES), jnp.float32).at[:, :N_EXPERTS].set(router.astype(jnp.float32))
    tri = jnp.asarray(np.tril(np.ones((tm, tm), np.float32), -1), jnp.bfloat16)
    return pl.pallas_call(
        _router_kernel,
        out_shape=(jax.ShapeDtypeStruct((t, LANES), jnp.int32),
                   jax.ShapeDtypeStruct((t, LANES), jnp.float32),
                   jax.ShapeDtypeStruct((1, LANES), jnp.float32),
                   jax.ShapeDtypeStruct((t, d // 2), jnp.uint32)),
        grid=(t // tm,),
        in_specs=[pl.BlockSpec((tm, d), lambda i: (i, 0)),
                  pl.BlockSpec((d, LANES), lambda i: (0, 0)),
                  pl.BlockSpec((tm, tm), lambda i: (0, 0))],
        out_specs=(pl.BlockSpec((tm, LANES), lambda i: (i, 0)),
                   pl.BlockSpec((tm, LANES), lambda i: (i, 0)),
                   pl.BlockSpec((1, LANES), lambda i: (0, 0)),
                   pl.BlockSpec((tm, d // 2), lambda i: (i, 0))),
        scratch_shapes=[pltpu.VMEM((1, LANES), jnp.float32)],
        compiler_params=_cparams(("arbitrary",)),
    )(xf, r_pad, tri)


SC_CORES = 2
SC_SUBCORES = 16
SC_WINDOW = 64


def _sc_gather_rows(table, idx):
    m = idx.shape[0]
    d = table.shape[1]
    n_workers = SC_CORES * SC_SUBCORES
    per_w = m // n_workers
    n_win = per_w // SC_WINDOW
    assert per_w * n_workers == m and n_win * SC_WINDOW == per_w
    mesh = plsc.VectorSubcoreMesh(core_axis_name="c", subcore_axis_name="s")

    @functools.partial(
        pl.kernel, mesh=mesh,
        out_type=jax.ShapeDtypeStruct((m, d), table.dtype),
        scratch_types=[pltpu.VMEM((per_w,), jnp.int32),
                       pltpu.VMEM((SC_WINDOW, d), table.dtype),
                       pltpu.SemaphoreType.DMA],
    )
    def gather(table_hbm, idx_hbm, out_hbm, idx_v, rows_v, sem):
        wid = lax.axis_index("s") * SC_CORES + lax.axis_index("c")
        base = wid * per_w
        pltpu.sync_copy(idx_hbm.at[pl.ds(base, per_w)], idx_v)

        @pl.loop(0, n_win)
        def _(w):
            off = pl.multiple_of(w * SC_WINDOW, SC_WINDOW)
            pltpu.async_copy(table_hbm.at[idx_v.at[pl.ds(off, SC_WINDOW)]], rows_v, sem).wait()
            pltpu.sync_copy(rows_v, out_hbm.at[pl.ds(base + off, SC_WINDOW)])

    return gather(table, idx)


def _sc_scatter_rows(rows, idx, n_out):
    n_copies, m = idx.shape
    d = rows.shape[1]
    n_workers = SC_CORES * SC_SUBCORES
    per_w = m // n_workers
    n_win = per_w // SC_WINDOW
    assert per_w * n_workers == m and n_win * SC_WINDOW == per_w
    idx_w = idx.reshape(n_copies, n_workers, n_win, SC_WINDOW).transpose(1, 0, 2, 3)
    idx_w = idx_w.reshape(n_workers, n_copies * n_win, SC_WINDOW)
    mesh = plsc.VectorSubcoreMesh(core_axis_name="c", subcore_axis_name="s")

    @functools.partial(
        pl.kernel, mesh=mesh,
        out_type=jax.ShapeDtypeStruct((n_out, d), rows.dtype),
        scratch_types=[pltpu.VMEM((n_copies * n_win, SC_WINDOW), jnp.int32),
                       pltpu.VMEM((SC_WINDOW, d), rows.dtype)],
    )
    def scatter(rows_hbm, idx_hbm, out_hbm, idx_v, rows_v):
        wid = lax.axis_index("s") * SC_CORES + lax.axis_index("c")
        base = wid * per_w
        pltpu.sync_copy(idx_hbm.at[wid], idx_v)

        @pl.loop(0, n_win)
        def _(w):
            off = pl.multiple_of(w * SC_WINDOW, SC_WINDOW)
            pltpu.sync_copy(rows_hbm.at[pl.ds(base + off, SC_WINDOW)], rows_v)
            for k in range(n_copies):
                pltpu.sync_copy(rows_v, out_hbm.at[idx_v.at[k * n_win + w]])

    return scatter(rows, idx_w)


def _expert_kernel(te_ref, nv_ref, x_ref, w1_ref, w3_ref, w2_ref, o_ref,
                   w1s_ref, w3s_ref, w2s_ref, xb_ref, acc_ref):
    i = pl.program_id(0)
    j = pl.program_id(1)
    last = pl.num_programs(1) - 1
    n_valid = nv_ref[i]

    @pl.when(n_valid > 0)
    def _():
        w1s_ref[...] = w1_ref[0].astype(jnp.bfloat16)
        w3s_ref[...] = w3_ref[0].astype(jnp.bfloat16)
        w2s_ref[...] = w2_ref[0].astype(jnp.bfloat16)

    for s in range(MOE_TILE // MOE_SUB):
        rows = pl.ds(s * MOE_SUB, MOE_SUB)
        live = n_valid > s * MOE_SUB

        @pl.when(live & (j == 0))
        def _():
            row = lax.broadcasted_iota(jnp.int32, (MOE_SUB, 1), 0) + s * MOE_SUB
            x = jnp.where(row < n_valid, _unpack_rows(x_ref[rows, :]), 0.0)
            xb_ref[rows, :] = x.astype(jnp.bfloat16)

        @pl.when(live)
        def _():
            x = xb_ref[rows, :]
            a = _dot(x, w1s_ref[...])
            hid = (a * _sigmoid(a) * _dot(x, w3s_ref[...])).astype(jnp.bfloat16)
            part = _dot(hid, w2s_ref[...])

            @pl.when(j == 0)
            def _():
                acc_ref[rows, :] = part

            @pl.when(j > 0)
            def _():
                acc_ref[rows, :] += part

        @pl.when(live & (j == last))
        def _():
            o_ref[rows, :] = _pack_rows(acc_ref[rows, :])

        @pl.when(jnp.logical_not(live) & (j == last))
        def _():
            o_ref[rows, :] = jnp.zeros((MOE_SUB, o_ref.shape[1]), o_ref.dtype)


def _experts(tile_expert, tile_valid, xs, w1, w3, w2):
    n_rows = xs.shape[0]
    d, f = w1.shape[1], w1.shape[2]
    nj = f // MOE_FCHUNK

    def chunk(i, j, nv):
        return jnp.where(nv[i] > 0, j, nj - 1)

    return pl.pallas_call(
        _expert_kernel,
        out_shape=jax.ShapeDtypeStruct((n_rows, d // 2), jnp.uint32),
        grid_spec=pltpu.PrefetchScalarGridSpec(
            num_scalar_prefetch=2,
            grid=(n_rows // MOE_TILE, nj),
            in_specs=[pl.BlockSpec((MOE_TILE, d // 2), lambda i, j, te, nv: (i, 0)),
                      pl.BlockSpec((1, d, MOE_FCHUNK), lambda i, j, te, nv: (te[i], 0, chunk(i, j, nv))),
                      pl.BlockSpec((1, d, MOE_FCHUNK), lambda i, j, te, nv: (te[i], 0, chunk(i, j, nv))),
                      pl.BlockSpec((1, MOE_FCHUNK, d), lambda i, j, te, nv: (te[i], chunk(i, j, nv), 0))],
            out_specs=pl.BlockSpec((MOE_TILE, d // 2), lambda i, j, te, nv: (i, 0)),
            scratch_shapes=[pltpu.VMEM((d, MOE_FCHUNK), jnp.bfloat16),
                            pltpu.VMEM((d, MOE_FCHUNK), jnp.bfloat16),
                            pltpu.VMEM((MOE_FCHUNK, d), jnp.bfloat16),
                            pltpu.VMEM((MOE_TILE, d), jnp.bfloat16),
                            pltpu.VMEM((MOE_TILE, d), jnp.float32)],
        ),
        compiler_params=pltpu.CompilerParams(dimension_semantics=("arbitrary", "arbitrary"),
                                             vmem_limit_bytes=MOE_VMEM_LIMIT),
    )(tile_expert, tile_valid, xs, w1, w3, w2)


def _combine_kernel(ya_ref, yb_ref, x_ref, w_ref, g_ref, b_ref, o_ref):
    w = w_ref[...]
    f = w[:, 0:1] * _unpack_rows(ya_ref[...]) + w[:, 1:2] * _unpack_rows(yb_ref[...])
    o_ref[...] = _layer_norm_rows(ALPHA * x_ref[...] + f, g_ref[...], b_ref[...])


def _combine(yg, xf, wts, g, b, tm=512):
    t, d = xf.shape
    nt = t // tm
    return pl.pallas_call(
        _combine_kernel,
        out_shape=jax.ShapeDtypeStruct((t, d), jnp.float32),
        grid=(nt,),
        in_specs=[pl.BlockSpec((tm, d // 2), lambda i: (i, 0)),
                  pl.BlockSpec((tm, d // 2), lambda i: (i + nt, 0)),
                  pl.BlockSpec((tm, d), lambda i: (i, 0)),
                  pl.BlockSpec((tm, LANES), lambda i: (i, 0)),
                  pl.BlockSpec((1, d), lambda i: (0, 0)),
                  pl.BlockSpec((1, d), lambda i: (0, 0))],
        out_specs=pl.BlockSpec((tm, d), lambda i: (i, 0)),
        compiler_params=_cparams(("parallel",)),
    )(yg, yg, xf, wts, g.reshape(1, d), b.reshape(1, d))


def _moe_ffn(xf, router, w1, w3, w2, g, b):
    t, d = xf.shape
    oi, ow, cnt, xp = _router(xf, router)
    e1, e2, rank1, rank2 = oi[:, 0], oi[:, 1], oi[:, 2], oi[:, 3]
    counts = cnt[0, :N_EXPERTS].astype(jnp.int32)
    tiles = (counts + MOE_TILE - 1) // MOE_TILE
    tile_end = jnp.cumsum(tiles)
    tile_start = tile_end - tiles
    offs = tile_start * MOE_TILE
    dest = jnp.concatenate([offs[e1] + rank1, offs[e2] + rank2])
    n_rows = TOP_K * t + N_EXPERTS * MOE_TILE
    n_tiles = n_rows // MOE_TILE
    tile_id = jnp.arange(n_tiles, dtype=jnp.int32)
    tile_expert = jnp.sum(tile_id[:, None] >= tile_end[None, :], axis=1)
    used = tile_expert < N_EXPERTS
    last_expert = jnp.sum(tile_end[-1] - 1 >= tile_end)
    tile_expert = jnp.where(used, tile_expert, last_expert).astype(jnp.int32)
    tile_valid = jnp.clip(counts[tile_expert] - (tile_id - tile_start[tile_expert]) * MOE_TILE, 0, MOE_TILE)
    tile_valid = jnp.where(used, tile_valid, 0).astype(jnp.int32)
    xs = _sc_scatter_rows(xp, dest.reshape(TOP_K, t), n_rows)
    y = _experts(tile_expert, tile_valid, xs, w1, w3, w2)
    yg = _sc_gather_rows(y, dest)
    return _combine(yg, xf, ow, g, b)


def _prep_layer(l, p, in_cols):
    bf = jnp.bfloat16
    out = {}
    out["w_in"] = _take_cols(p["w_in"][l], in_cols).astype(bf)
    deint = np.concatenate([np.arange(0, HEAD_DIM, 2), np.arange(1, HEAD_DIM, 2)])
    out["gq_a"] = jnp.tile(p["a_q_norm"][l][deint], 2).reshape(1, LANES)
    out["gk_a"] = jnp.tile(p["a_k_norm"][l][deint], 2).reshape(1, LANES)
    per_head = C_NOPE + C_ROPE
    uq_cols = []
    for h in range(C_HEADS):
        base = h * per_head
        uq_cols += [base + np.arange(C_NOPE), base + C_NOPE + np.arange(0, C_ROPE, 2),
                    base + C_NOPE + np.arange(1, C_ROPE, 2), np.full(32, -1)]
    wuq = _take_cols(p["c_w_uq"][l], np.concatenate(uq_cols))
    out["wuq"] = jnp.concatenate([wuq, jnp.zeros((64, wuq.shape[1]), wuq.dtype)], axis=0).astype(bf)
    uk_cols, uv_cols = [], []
    for h in range(C_HEADS):
        base = h * (C_NOPE + C_V)
        uk_cols += [base + np.arange(C_NOPE), np.full(64, -1)]
        uv_cols += [base + C_NOPE + np.arange(C_V)]
    out["wuk"] = _take_cols(p["c_w_ukv"][l], np.concatenate(uk_cols)).astype(bf)
    out["wuv"] = _take_cols(p["c_w_ukv"][l], np.concatenate(uv_cols)).astype(bf)
    out["gq_c"] = jnp.concatenate([p["c_q_norm"][l], jnp.zeros((64,), jnp.float32)]).reshape(1, 256)
    out["gkv_c"] = p["c_kv_norm"][l].reshape(1, C_KV_RANK)
    out["conv_w"] = jnp.concatenate([p["d_conv_w"][l][:, 0, :], jnp.zeros((1, D_CH), jnp.float32)], axis=0)
    out["conv_b"] = p["d_conv_b"][l].reshape(1, D_CH)
    out["d_ln_g"] = p["d_ln_g"][l].reshape(1, D_CH)
    out["d_ln_b"] = p["d_ln_b"][l].reshape(1, D_CH)
    a_rows = np.concatenate([(kvh * 2 + g) * HEAD_DIM + np.arange(HEAD_DIM)
                             for g in range(2) for kvh in range(A_KV_HEADS)])
    wb = p["w_branch"][l]
    out["w_branch"] = jnp.stack([wb[0][a_rows], wb[1], wb[2], wb[3]], axis=0).astype(bf)
    out["w_out"] = p["w_out"][l].astype(bf)
    return out


def kernel(x, ln_in_g, ln_in_b, w_in, a_q_norm, a_k_norm, b_rpb, c_q_norm, c_kv_norm, c_w_uq, c_w_ukv,
           d_conv_w, d_conv_b, d_ln_g, d_ln_b, w_branch, w_out, ln_mix_g, ln_mix_b,
           ffn_w1, ffn_w3, ffn_w2, moe_router, moe_w1, moe_w3, moe_w2, ln_ffn_g, ln_ffn_b):
    batch, seq, d = x.shape
    t = batch * seq
    bf = jnp.bfloat16
    params = dict(w_in=w_in, a_q_norm=a_q_norm, a_k_norm=a_k_norm, c_q_norm=c_q_norm, c_kv_norm=c_kv_norm,
                  c_w_uq=c_w_uq, c_w_ukv=c_w_ukv, d_conv_w=d_conv_w, d_conv_b=d_conv_b, d_ln_g=d_ln_g,
                  d_ln_b=d_ln_b, w_branch=w_branch, w_out=w_out)
    in_cols = _in_proj_columns()
    cos_a, sin_a, cos_c, sin_c = _rope_tables(seq)
    xf, xb = _input_ln(x.reshape(t, d), ln_in_g, ln_in_b)
    for l in range(DEPTH):
        p = _prep_layer(l, params, in_cols)
        h = _matmul(xb, p["w_in"], bf, tm=1024, tn=1280)
        ya = _attn_a(h, cos_a, sin_a, p["gq_a"], p["gk_a"], batch, seq)
        yb = _attn_b(h, _na_bias_tables(b_rpb[l], seq // GRID_W), batch, seq)
        yc = _attn_c(h, p["wuq"], p["wuk"], p["wuv"], p["gq_c"], p["gkv_c"], cos_c, sin_c, batch, seq)
        yd = _conv_branch(h, p["conv_w"], p["conv_b"], p["d_ln_g"], p["d_ln_b"], batch, seq)
        xf, xb = _merge(h, ya, yb, yc, yd, xf, p["w_branch"], p["w_out"], ln_mix_g[l], ln_mix_b[l])
        if l % 2 == 0:
            i = l // 2
            xf, xb = _ffn(xb, xf, ffn_w1[i].astype(bf), ffn_w3[i].astype(bf), ffn_w2[i].astype(bf),
                          ln_ffn_g[l], ln_ffn_b[l])
        else:
            i = l // 2
            xf = _moe_ffn(xf, moe_router[i], moe_w1[i], moe_w3[i], moe_w2[i], ln_ffn_g[l], ln_ffn_b[l])
            xb = xf.astype(bf)
    return xf.reshape(batch, seq, d)
```

```python
import functools

import numpy as np
import jax
import jax.numpy as jnp
from jax import lax
from jax.experimental import pallas as pl
from jax.experimental.pallas import tpu as pltpu
from jax.experimental.pallas import tpu_sc as plsc

D_MODEL = 1024
DEPTH = 2
GRID_W = 64
HEAD_DIM = 64
ROPE_THETA = 10000.0
RMS_EPS = 1e-6
LN_EPS = 1e-5
A_HEADS = 4
A_KV_HEADS = 2
B_HEADS = 4
NA_WIN_H = 8
NA_WIN_W = 16
C_HEADS = 4
C_NOPE = 64
C_ROPE = 32
C_V = 64
C_Q_RANK = 192
C_KV_RANK = 128
D_CH = 256
CONV_W = 31
N_BRANCH = 4
MIX_W = 256
D_FF = 2816
N_EXPERTS = 8
TOP_K = 2
D_FF_EXPERT = 3584
ALPHA = (2 * DEPTH) ** 0.25

LANES = 128
SUBLANES = 8
VMEM_LIMIT = 48 * 1024 * 1024

LOG2E = 1.4426950408889634
NEG_BIG = -1e30

COL_GATES = 0
COL_AQ = 4096
COL_AKV = 4352
COL_C = 4608
COL_D = 5120
COL_BQ = 5632
COL_BK = 5888
COL_BV = 6144
D_IN_PAD = 6400

NA_QROWS = 4
NA_KROWS = 12

MOE_TILE = 1024
MOE_SUB = 512
MOE_FCHUNK = 896
MOE_VMEM_LIMIT = 56 * 1024 * 1024


def _in_proj_columns():
    o_aq, o_ak, o_av = 0, 256, 384
    o_bq, o_bk, o_bv = 512, 768, 1024
    o_cq, o_ckv, o_kpe = 1280, 1472, 1600
    o_d, o_g = 1632, 2144
    deint = np.concatenate([np.arange(0, HEAD_DIM, 2), np.arange(1, HEAD_DIM, 2)])
    cols = [o_g + np.arange(N_BRANCH * D_MODEL)]
    for g in range(2):
        for kvh in range(A_KV_HEADS):
            cols.append(o_aq + (kvh * 2 + g) * HEAD_DIM + deint)
    for kvh in range(A_KV_HEADS):
        cols.append(o_ak + kvh * HEAD_DIM + deint)
    cols.append(o_av + np.arange(A_KV_HEADS * HEAD_DIM))
    cols.append(o_cq + np.arange(C_Q_RANK))
    cols.append(np.full(64, -1))
    cols.append(o_ckv + np.arange(C_KV_RANK))
    cols.append(np.full(64, -1))
    cols.append(o_kpe + np.arange(0, C_ROPE, 2))
    cols.append(o_kpe + np.arange(1, C_ROPE, 2))
    cols.append(np.full(32, -1))
    cols.append(o_d + np.arange(2 * D_CH))
    cols.append(o_bq + np.arange(3 * B_HEADS * HEAD_DIM))
    cols = np.concatenate(cols)
    assert cols.shape[0] == D_IN_PAD
    return cols


def _take_cols(w, cols):
    cols = np.asarray(cols)
    parts, i, n = [], 0, len(cols)
    while i < n:
        j = i + 1
        if cols[i] < 0:
            while j < n and cols[j] < 0:
                j += 1
            parts.append(jnp.zeros((w.shape[0], j - i), w.dtype))
        else:
            step = int(cols[j] - cols[i]) if j < n and cols[j] > cols[i] else 1
            while j < n and cols[j] == cols[j - 1] + step:
                j += 1
            c0, n_run = int(cols[i]), j - i
            if step == 1:
                parts.append(w[:, c0:c0 + n_run])
            else:
                assert step == 2
                base = c0 - c0 % 2
                parts.append(w[:, base:base + 2 * n_run].reshape(w.shape[0], n_run, 2)[:, :, c0 % 2])
        i = j
    return jnp.concatenate(parts, axis=1)


def _rope_tables(seq):
    t = np.arange(seq)
    row = (t // GRID_W).astype(np.float32)
    col = (t % GRID_W).astype(np.float32)

    def angles(rot_dim):
        n_freq = rot_dim // 4
        inv = jnp.asarray(ROPE_THETA, jnp.float32) ** (-jnp.arange(n_freq, dtype=jnp.float32) / n_freq)
        return jnp.concatenate([jnp.asarray(row)[:, None] * inv, jnp.asarray(col)[:, None] * inv], axis=-1)

    ang_a = angles(HEAD_DIM)
    ca, sa = jnp.cos(ang_a), jnp.sin(ang_a)
    cos_a = jnp.concatenate([ca, ca, ca, ca], axis=-1)
    sin_a = jnp.concatenate([-sa, sa, -sa, sa], axis=-1)
    ang_c = angles(C_ROPE)
    cc, sc = jnp.cos(ang_c), jnp.sin(ang_c)
    one = jnp.ones((seq, 64), jnp.float32)
    zero = jnp.zeros((seq, 64), jnp.float32)
    cos_c = jnp.concatenate([one, cc, cc, one[:, :32]], axis=-1)
    sin_c = jnp.concatenate([zero, -sc, sc, zero[:, :32]], axis=-1)
    return cos_a, sin_a, cos_c, sin_c


def _na_bias_tables(rpb, rows):
    wh = min(NA_WIN_H, rows)
    n_dr, n_dc = 2 * NA_WIN_H - 1, 2 * NA_WIN_W - 1
    qc = np.arange(GRID_W)[:, None]
    kc = np.arange(GRID_W)[None, :]
    col_start = np.clip(qc - NA_WIN_W // 2, 0, GRID_W - NA_WIN_W)
    ok_c = (kc >= col_start) & (kc < col_start + NA_WIN_W)
    dc = np.clip(kc - qc + (NA_WIN_W - 1), 0, n_dc - 1)
    oh_c = np.eye(n_dc, dtype=np.float32)[dc.reshape(-1)]
    oh_r, ok_r = [], []
    for r0 in (0, NA_QROWS, rows - NA_QROWS):
        ws = int(np.clip(r0 - 4, 0, rows - NA_KROWS))
        qr = r0 + np.arange(NA_QROWS)[:, None]
        kr = ws + np.arange(NA_KROWS)[None, :]
        row_start = np.clip(qr - wh // 2, 0, rows - wh)
        ok_r.append((kr >= row_start) & (kr < row_start + wh))
        dr = np.clip(kr - qr + (NA_WIN_H - 1), 0, n_dr - 1)
        oh_r.append(np.eye(n_dr, dtype=np.float32)[dr.reshape(-1)])
    oh_r = np.concatenate(oh_r, axis=0)
    ok = np.stack(ok_r)[:, :, None, :, None] & ok_c[None, None, :, None, :]
    ok = ok.reshape(3, 1, NA_QROWS * GRID_W, NA_KROWS * GRID_W)
    hi = lax.Precision.HIGHEST
    t1 = jnp.einsum("ma,hab->hmb", jnp.asarray(oh_r), rpb.astype(jnp.float32) * LOG2E, precision=hi)
    t2 = jnp.einsum("hmb,nb->hmn", t1, jnp.asarray(oh_c), precision=hi).astype(jnp.bfloat16)
    nh = rpb.shape[0]
    t2 = t2.reshape(nh, 3, NA_QROWS, NA_KROWS, GRID_W, GRID_W).transpose(1, 0, 2, 4, 3, 5)
    bias = t2.reshape(3, nh, NA_QROWS * GRID_W, NA_KROWS * GRID_W)
    return jnp.where(jnp.asarray(ok), bias, jnp.asarray(NEG_BIG, jnp.bfloat16))


def _cparams(sem):
    return pltpu.CompilerParams(dimension_semantics=sem, vmem_limit_bytes=VMEM_LIMIT)


def _layer_norm_rows(z, g, b):
    mu = jnp.mean(z, axis=-1, keepdims=True)
    zc = z - mu
    var = jnp.mean(zc * zc, axis=-1, keepdims=True)
    return zc * lax.rsqrt(var + LN_EPS) * g + b


def _sigmoid(x):
    return 0.5 * jnp.tanh(0.5 * x) + 0.5


def _dot(a, b):
    return jnp.dot(a, b, preferred_element_type=jnp.float32)


def _dot_nt(a, b):
    return lax.dot_general(a, b, (((1,), (1,)), ((), ())), preferred_element_type=jnp.float32)


def _ln_kernel(x_ref, g_ref, b_ref, of_ref, ob_ref):
    y = _layer_norm_rows(x_ref[...], g_ref[...], b_ref[...])
    of_ref[...] = y
    ob_ref[...] = y.astype(jnp.bfloat16)


def _input_ln(x, g, b, tm=1024):
    t, d = x.shape
    return pl.pallas_call(
        _ln_kernel,
        out_shape=(jax.ShapeDtypeStruct((t, d), jnp.float32), jax.ShapeDtypeStruct((t, d), jnp.bfloat16)),
        grid=(t // tm,),
        in_specs=[pl.BlockSpec((tm, d), lambda i: (i, 0)),
                  pl.BlockSpec((1, d), lambda i: (0, 0)),
                  pl.BlockSpec((1, d), lambda i: (0, 0))],
        out_specs=(pl.BlockSpec((tm, d), lambda i: (i, 0)), pl.BlockSpec((tm, d), lambda i: (i, 0))),
        compiler_params=_cparams(("parallel",)),
    )(x, g.reshape(1, d), b.reshape(1, d))


def _mm_kernel(a_ref, b_ref, o_ref):
    o_ref[...] = _dot(a_ref[...], b_ref[...]).astype(o_ref.dtype)


def _matmul(a, b, out_dtype, tm, tn):
    m, k = a.shape
    n = b.shape[1]
    return pl.pallas_call(
        _mm_kernel,
        out_shape=jax.ShapeDtypeStruct((m, n), out_dtype),
        grid=(m // tm, n // tn),
        in_specs=[pl.BlockSpec((tm, k), lambda i, j: (i, 0)),
                  pl.BlockSpec((k, tn), lambda i, j: (0, j))],
        out_specs=pl.BlockSpec((tm, tn), lambda i, j: (i, j)),
        compiler_params=_cparams(("parallel", "parallel")),
    )(a, b)


def _ones_halves(vblk):
    lane = lax.broadcasted_iota(jnp.int32, vblk.shape, 1)
    ones = jnp.ones_like(vblk)
    return jnp.where(lane < HEAD_DIM, vblk, ones), jnp.where(lane < HEAD_DIM, ones, vblk)


def _attend_pair(s_lo, s_hi, v_lo, v_hi, lo_q):
    outs = []
    for s, v in ((s_lo, v_lo), (s_hi, v_hi)):
        p = jnp.exp2(s - jnp.max(s, axis=-1, keepdims=True)).astype(jnp.bfloat16)
        outs.append(_dot(p, v))
    num = jnp.where(lo_q, outs[0], outs[1])
    den = jnp.where(lo_q, pltpu.roll(outs[0], HEAD_DIM, 1), pltpu.roll(outs[1], HEAD_DIM, 1))
    return num / den


def _rotate_pairs(x, cos, sin, half, lane):
    w = x.shape[-1]
    first = (lane % (2 * half)) < half
    partner = jnp.where(first, pltpu.roll(x, w - half, 1), pltpu.roll(x, half, 1))
    return x * cos + partner * sin


def _attn_a_kernel(q_ref, kv_ref, cos_ref, sin_ref, gq_ref, gk_ref, o_ref, ks_ref, va_ref, *, tq):
    s_len = q_ref.shape[0]

    def prep(x, g, cos, sin):
        lane = lax.broadcasted_iota(jnp.int32, x.shape, 1)
        lo = lane < HEAD_DIM
        x2 = x * x
        ss_lo = jnp.sum(jnp.where(lo, x2, 0.0), axis=-1, keepdims=True)
        ss_hi = jnp.sum(jnp.where(lo, 0.0, x2), axis=-1, keepdims=True)
        ms = jnp.where(lo, ss_lo, ss_hi) * (1.0 / HEAD_DIM)
        xn = x * lax.rsqrt(ms + RMS_EPS) * g
        return _rotate_pairs(xn, cos, sin, HEAD_DIM // 2, lane)

    ks_ref[...] = prep(kv_ref[:, 0:LANES].astype(jnp.float32), gk_ref[...],
                       cos_ref[...], sin_ref[...]).astype(jnp.bfloat16)
    scale = HEAD_DIM ** -0.5 * LOG2E
    lane_q = lax.broadcasted_iota(jnp.int32, (tq, LANES), 1)
    lo_q = lane_q < HEAD_DIM

    va_ref[0], va_ref[1] = _ones_halves(kv_ref[:, LANES:2 * LANES])

    def body(i, carry):
        r = pl.multiple_of(i * tq, tq)
        k = ks_ref[...]
        cos = cos_ref[pl.ds(r, tq), :]
        sin = sin_ref[pl.ds(r, tq), :]
        for g in range(2):
            qt = prep(q_ref[pl.ds(r, tq), g * LANES:(g + 1) * LANES].astype(jnp.float32), gq_ref[...], cos, sin)
            qt = (qt * scale).astype(jnp.bfloat16)
            zero = jnp.zeros_like(qt)
            out = _attend_pair(_dot_nt(jnp.where(lo_q, qt, zero), k), _dot_nt(jnp.where(lo_q, zero, qt), k),
                               va_ref[0], va_ref[1], lo_q)
            o_ref[pl.ds(r, tq), g * LANES:(g + 1) * LANES] = out.astype(o_ref.dtype)
        return carry

    lax.fori_loop(0, s_len // tq, body, 0)


def _attn_a(h, cos_a, sin_a, gq, gk, batch, seq, tq=256):
    kern = functools.partial(_attn_a_kernel, tq=tq)
    return pl.pallas_call(
        kern,
        out_shape=jax.ShapeDtypeStruct((batch * seq, MIX_W), jnp.bfloat16),
        grid=(batch,),
        in_specs=[pl.BlockSpec((seq, 256), lambda b: (b, COL_AQ // 256)),
                  pl.BlockSpec((seq, 256), lambda b: (b, COL_AKV // 256)),
                  pl.BlockSpec((seq, LANES), lambda b: (0, 0)),
                  pl.BlockSpec((seq, LANES), lambda b: (0, 0)),
                  pl.BlockSpec((1, LANES), lambda b: (0, 0)),
                  pl.BlockSpec((1, LANES), lambda b: (0, 0))],
        out_specs=pl.BlockSpec((seq, MIX_W), lambda b: (b, 0)),
        scratch_shapes=[pltpu.VMEM((seq, LANES), jnp.bfloat16), pltpu.VMEM((2, seq, LANES), jnp.bfloat16)],
        compiler_params=_cparams(("parallel",)),
    )(h, h, cos_a, sin_a, gq, gk)


def _attn_b_kernel(q_ref, k_ref, v_ref, bias_ref, o_ref, va_ref, *, rows):
    nq = NA_QROWS * GRID_W
    nk = NA_KROWS * GRID_W
    lane_q = lax.broadcasted_iota(jnp.int32, (nq, LANES), 1)
    lo_q = lane_q < HEAD_DIM
    n_groups = rows // NA_QROWS
    scale = HEAD_DIM ** -0.5 * LOG2E
    for blk in range(B_HEADS // 2):
        va_ref[2 * blk], va_ref[2 * blk + 1] = _ones_halves(v_ref[:, blk * LANES:(blk + 1) * LANES])

    def body(gi, carry):
        ws = jnp.clip(gi * NA_QROWS - 4, 0, rows - NA_KROWS)
        cls = jnp.where(gi == 0, 0, jnp.where(gi == n_groups - 1, 2, 1))
        qr = pl.multiple_of(gi * nq, nq)
        kr = pl.multiple_of(ws * GRID_W, GRID_W)
        for blk in range(B_HEADS // 2):
            sl = slice(blk * LANES, (blk + 1) * LANES)
            qt = q_ref[pl.ds(qr, nq), sl] * scale
            kw = k_ref[pl.ds(kr, nk), sl]
            zero = jnp.zeros_like(qt)
            s_lo = _dot_nt(jnp.where(lo_q, qt, zero), kw) + bias_ref[cls, 2 * blk].astype(jnp.float32)
            s_hi = _dot_nt(jnp.where(lo_q, zero, qt), kw) + bias_ref[cls, 2 * blk + 1].astype(jnp.float32)
            out = _attend_pair(s_lo, s_hi, va_ref[2 * blk, pl.ds(kr, nk), :], va_ref[2 * blk + 1, pl.ds(kr, nk), :], lo_q)
            o_ref[pl.ds(qr, nq), sl] = out.astype(o_ref.dtype)
        return carry

    lax.fori_loop(0, n_groups, body, 0)


def _attn_b(h, bias, batch, seq):
    rows = seq // GRID_W
    kern = functools.partial(_attn_b_kernel, rows=rows)
    return pl.pallas_call(
        kern,
        out_shape=jax.ShapeDtypeStruct((batch * seq, MIX_W), jnp.bfloat16),
        grid=(batch,),
        in_specs=[pl.BlockSpec((seq, 256), lambda b: (b, COL_BQ // 256)),
                  pl.BlockSpec((seq, 256), lambda b: (b, COL_BK // 256)),
                  pl.BlockSpec((seq, 256), lambda b: (b, COL_BV // 256)),
                  pl.BlockSpec(bias.shape, lambda b: (0, 0, 0, 0))],
        out_specs=pl.BlockSpec((seq, MIX_W), lambda b: (b, 0)),
        scratch_shapes=[pltpu.VMEM((B_HEADS, seq, LANES), jnp.bfloat16)],
        compiler_params=_cparams(("parallel",)),
    )(h, h, h, bias)


def _attn_c_kernel(hc_ref, wuq_ref, wuk_ref, wuv_ref, gq_ref, gkv_ref, cos_ref, sin_ref, o_ref,
                   qs_ref, ks_ref, vs_ref, *, tq):
    s_len = hc_ref.shape[0]
    lane = lax.broadcasted_iota(jnp.int32, (s_len, LANES), 1)
    cos = cos_ref[...]
    sin = sin_ref[...]

    cq = hc_ref[:, 0:256].astype(jnp.float32)
    ms = jnp.sum(cq * cq, axis=-1, keepdims=True) * (1.0 / C_Q_RANK)
    cqn = (cq * lax.rsqrt(ms + RMS_EPS) * gq_ref[...]).astype(jnp.bfloat16)
    ckv = hc_ref[:, 256:384].astype(jnp.float32)
    ms = jnp.mean(ckv * ckv, axis=-1, keepdims=True)
    kvn = (ckv * lax.rsqrt(ms + RMS_EPS) * gkv_ref[...]).astype(jnp.bfloat16)
    kpe = _rotate_pairs(hc_ref[:, 384:512].astype(jnp.float32), cos, sin, C_ROPE // 2, lane)
    v_all = _dot(kvn, wuv_ref[...]).astype(jnp.bfloat16)
    for blk in range(C_HEADS // 2):
        vs_ref[2 * blk], vs_ref[2 * blk + 1] = _ones_halves(v_all[:, blk * LANES:(blk + 1) * LANES])
    scale = (C_NOPE + C_ROPE) ** -0.5 * LOG2E
    for h in range(C_HEADS):
        sl = slice(h * LANES, (h + 1) * LANES)
        qh = _rotate_pairs(_dot(cqn, wuq_ref[:, sl]), cos, sin, C_ROPE // 2, lane) * scale
        qs_ref[:, sl] = qh.astype(jnp.bfloat16)
        ks_ref[:, sl] = (_dot(kvn, wuk_ref[:, sl]) + kpe).astype(jnp.bfloat16)

    lane_q = lax.broadcasted_iota(jnp.int32, (tq, LANES), 1)
    lo_q = lane_q < C_V

    def body(i, carry):
        r = pl.multiple_of(i * tq, tq)
        for blk in range(C_HEADS // 2):
            s = []
            for par in range(2):
                sl = slice((2 * blk + par) * LANES, (2 * blk + par + 1) * LANES)
                s.append(_dot_nt(qs_ref[pl.ds(r, tq), sl], ks_ref[:, sl]))
            out = _attend_pair(s[0], s[1], vs_ref[2 * blk], vs_ref[2 * blk + 1], lo_q)
            o_ref[pl.ds(r, tq), blk * LANES:(blk + 1) * LANES] = out.astype(o_ref.dtype)
        return carry

    lax.fori_loop(0, s_len // tq, body, 0)


def _attn_c(h, wuq, wuk, wuv, gq, gkv, cos_c, sin_c, batch, seq, tq=256):
    kern = functools.partial(_attn_c_kernel, tq=tq)
    const = lambda shape: pl.BlockSpec(shape, lambda b: (0,) * len(shape))
    return pl.pallas_call(
        kern,
        out_shape=jax.ShapeDtypeStruct((batch * seq, MIX_W), jnp.bfloat16),
        grid=(batch,),
        in_specs=[pl.BlockSpec((seq, 512), lambda b: (b, COL_C // 512)),
                  const(wuq.shape), const(wuk.shape), const(wuv.shape),
                  const(gq.shape), const(gkv.shape), const(cos_c.shape), const(sin_c.shape)],
        out_specs=pl.BlockSpec((seq, MIX_W), lambda b: (b, 0)),
        scratch_shapes=[pltpu.VMEM((seq, C_HEADS * LANES), jnp.bfloat16),
                        pltpu.VMEM((seq, C_HEADS * LANES), jnp.bfloat16),
                        pltpu.VMEM((C_HEADS, seq, LANES), jnp.bfloat16)],
        compiler_params=_cparams(("parallel",)),
    )(h, wuq, wuk, wuv, gq, gkv, cos_c, sin_c)


CONV_PAD = 16
CONV_ROWS = 128


def _conv_kernel(hd_ref, w_ref, cb_ref, g_ref, b_ref, o_ref, up_ref):
    s_len = hd_ref.shape[0]
    a = hd_ref[:, 0:D_CH].astype(jnp.float32)
    gate = hd_ref[:, D_CH:2 * D_CH].astype(jnp.float32)
    zeros = jnp.zeros((CONV_PAD, D_CH), jnp.float32)
    up_ref[0, 0:CONV_PAD, :] = zeros
    up_ref[0, CONV_PAD + s_len:2 * CONV_PAD + s_len, :] = zeros
    up_ref[0, CONV_PAD:CONV_PAD + s_len, :] = a * _sigmoid(gate)
    n_shifted = s_len + 2 * CONV_PAD - SUBLANES
    for s in range(1, SUBLANES):
        up_ref[s, 0:n_shifted, :] = up_ref[0, s:s + n_shifted, :]
    shift = CONV_PAD - CONV_W // 2
    for c in range(s_len // CONV_ROWS):
        acc = jnp.zeros((CONV_ROWS, D_CH), jnp.float32)
        for j in range(CONV_W):
            s = (j + shift) % SUBLANES
            start = c * CONV_ROWS + j + shift - s
            acc = acc + up_ref[s, start:start + CONV_ROWS, :] * w_ref[j:j + 1, :]
        u = acc + cb_ref[...]
        y = _layer_norm_rows(u, g_ref[...], b_ref[...])
        o_ref[c * CONV_ROWS:(c + 1) * CONV_ROWS, :] = (y * _sigmoid(y)).astype(o_ref.dtype)


def _conv_branch(h, w, cb, g, b, batch, seq):
    const = lambda shape: pl.BlockSpec(shape, lambda i: (0,) * len(shape))
    return pl.pallas_call(
        _conv_kernel,
        out_shape=jax.ShapeDtypeStruct((batch * seq, MIX_W), jnp.bfloat16),
        grid=(batch,),
        in_specs=[pl.BlockSpec((seq, 512), lambda i: (i, COL_D // 512)),
                  const(w.shape), const(cb.shape), const(g.shape), const(b.shape)],
        out_specs=pl.BlockSpec((seq, MIX_W), lambda i: (i, 0)),
        scratch_shapes=[pltpu.VMEM((SUBLANES, seq + 2 * CONV_PAD, D_CH), jnp.float32)],
        compiler_params=_cparams(("parallel",)),
    )(h, w, cb, g, b)


def _merge_kernel(gl_ref, ya_ref, yb_ref, yc_ref, yd_ref, x_ref, wb_ref, wo_ref, g_ref, b_ref,
                  of_ref, ob_ref):
    merged = None
    for n, y_ref in enumerate((ya_ref, yb_ref, yc_ref, yd_ref)):
        gate = _sigmoid(gl_ref[:, n * D_MODEL:(n + 1) * D_MODEL].astype(jnp.float32))
        term = gate * _dot(y_ref[...], wb_ref[n])
        merged = term if merged is None else merged + term
    mix = _dot(merged.astype(jnp.bfloat16), wo_ref[...])
    y = _layer_norm_rows(ALPHA * x_ref[...] + mix, g_ref[...], b_ref[...])
    of_ref[...] = y
    ob_ref[...] = y.astype(jnp.bfloat16)


def _merge(h, ya, yb, yc, yd, x, wb, wo, g, b, tm=512):
    t, d = x.shape
    row = lambda w: pl.BlockSpec((tm, w), lambda i: (i, 0))
    const = lambda shape: pl.BlockSpec(shape, lambda i: (0,) * len(shape))
    return pl.pallas_call(
        _merge_kernel,
        out_shape=(jax.ShapeDtypeStruct((t, d), jnp.float32), jax.ShapeDtypeStruct((t, d), jnp.bfloat16)),
        grid=(t // tm,),
        in_specs=[pl.BlockSpec((tm, N_BRANCH * d), lambda i: (i, COL_GATES)),
                  row(MIX_W), row(MIX_W), row(MIX_W), row(MIX_W), row(d),
                  const(wb.shape), const(wo.shape), const((1, d)), const((1, d))],
        out_specs=(row(d), row(d)),
        compiler_params=_cparams(("parallel",)),
    )(h, ya, yb, yc, yd, x, wb, wo, g.reshape(1, d), b.reshape(1, d))


def _ffn_kernel(xb_ref, xf_ref, w1_ref, w3_ref, w2_ref, g_ref, b_ref, of_ref, ob_ref, acc_ref):
    j = pl.program_id(1)
    x = xb_ref[...]
    a = _dot(x, w1_ref[...])
    hid = (a * _sigmoid(a) * _dot(x, w3_ref[...])).astype(jnp.bfloat16)
    part = _dot(hid, w2_ref[...])

    @pl.when(j == 0)
    def _():
        acc_ref[...] = part

    @pl.when(j > 0)
    def _():
        acc_ref[...] += part

    @pl.when(j == pl.num_programs(1) - 1)
    def _():
        y = _layer_norm_rows(ALPHA * xf_ref[...] + acc_ref[...], g_ref[...], b_ref[...])
        of_ref[...] = y
        ob_ref[...] = y.astype(jnp.bfloat16)


def _ffn(xb, xf, w1, w3, w2, g, b, tm=512, tf=1408):
    t, d = xf.shape
    f = w1.shape[1]
    row = pl.BlockSpec((tm, d), lambda i, j: (i, 0))
    vec = pl.BlockSpec((1, d), lambda i, j: (0, 0))
    return pl.pallas_call(
        _ffn_kernel,
        out_shape=(jax.ShapeDtypeStruct((t, d), jnp.float32), jax.ShapeDtypeStruct((t, d), jnp.bfloat16)),
        grid=(t // tm, f // tf),
        in_specs=[row, row,
                  pl.BlockSpec((d, tf), lambda i, j: (0, j)),
                  pl.BlockSpec((d, tf), lambda i, j: (0, j)),
                  pl.BlockSpec((tf, d), lambda i, j: (j, 0)),
                  vec, vec],
        out_specs=(row, row),
        scratch_shapes=[pltpu.VMEM((tm, d), jnp.float32)],
        compiler_params=_cparams(("parallel", "arbitrary")),
    )(xb, xf, w1, w3, w2, g.reshape(1, d), b.reshape(1, d))


def _pack_rows(x):
    half = x.shape[1] // 2
    bits = lax.bitcast_convert_type(x.astype(jnp.bfloat16).astype(jnp.float32), jnp.uint32)
    return (bits[:, :half] >> 16) | (bits[:, half:] & jnp.uint32(0xFFFF0000))


def _unpack_rows(p):
    lo = lax.bitcast_convert_type(p << 16, jnp.float32)
    hi = lax.bitcast_convert_type(p & jnp.uint32(0xFFFF0000), jnp.float32)
    return jnp.concatenate([lo, hi], axis=1)


def _router_kernel(x_ref, r_ref, tri_ref, oi_ref, ow_ref, cnt_ref, xp_ref, carry_ref):
    i = pl.program_id(0)

    @pl.when(i == 0)
    def _():
        carry_ref[...] = jnp.zeros_like(carry_ref)

    x = x_ref[...]
    r = r_ref[...]
    xh = x.astype(jnp.bfloat16)
    xl = (x - xh.astype(jnp.float32)).astype(jnp.bfloat16)
    rh = r.astype(jnp.bfloat16)
    rl = (r - rh.astype(jnp.float32)).astype(jnp.bfloat16)
    r2 = jnp.concatenate([rh, rl], axis=1)
    p = _dot(xh, r2) + _dot(xl, r2)
    logits = p[:, :LANES] + p[:, LANES:]
    tm = x.shape[0]
    lane = lax.broadcasted_iota(jnp.int32, (tm, LANES), 1)
    logits = jnp.where(lane < N_EXPERTS, logits, NEG_BIG)
    m1 = jnp.max(logits, axis=-1, keepdims=True)
    i1 = jnp.min(jnp.where(logits == m1, lane, LANES), axis=-1, keepdims=True)
    rest = jnp.where(lane == i1, NEG_BIG, logits)
    m2 = jnp.max(rest, axis=-1, keepdims=True)
    i2 = jnp.min(jnp.where(rest == m2, lane, LANES), axis=-1, keepdims=True)
    e = jnp.exp(m2 - m1)
    w1 = 1.0 / (1.0 + e)
    w2 = e / (1.0 + e)
    sel1 = lane == i1
    sel2 = lane == i2
    member = jnp.where(sel1 | sel2, 1.0, 0.0)
    before = _dot(tri_ref[...], member.astype(jnp.bfloat16)) + carry_ref[...]
    rank1 = jnp.sum(jnp.where(sel1, before, 0.0), axis=-1, keepdims=True).astype(jnp.int32)
    rank2 = jnp.sum(jnp.where(sel2, before, 0.0), axis=-1, keepdims=True).astype(jnp.int32)
    carry_ref[...] += jnp.sum(member, axis=0, keepdims=True)
    oi = jnp.where(lane == 0, i1, jnp.where(lane == 1, i2, jnp.where(lane == 2, rank1, rank2)))
    oi_ref[...] = oi.T[0:SUBLANES, :]
    ow_ref[...] = jnp.where(lane == 0, w1, w2)
    cnt_ref[...] = carry_ref[...]
    xp_ref[...] = _pack_rows(x)


def _router(xf, router, tm=512):
    t, d = xf.shape
    r_pad = jnp.zeros((d, LANES), jnp.float32).at[:, :N_EXPERTS].set(router.astype(jnp.float32))
    tri = jnp.asarray(np.tril(np.ones((tm, tm), np.float32), -1), jnp.bfloat16)
    return pl.pallas_call(
        _router_kernel,
        out_shape=(jax.ShapeDtypeStruct((SUBLANES, t), jnp.int32),
                   jax.ShapeDtypeStruct((t, LANES), jnp.float32),
                   jax.ShapeDtypeStruct((1, LANES), jnp.float32),
                   jax.ShapeDtypeStruct((t, d // 2), jnp.uint32)),
        grid=(t // tm,),
        in_specs=[pl.BlockSpec((tm, d), lambda i: (i, 0)),
                  pl.BlockSpec((d, LANES), lambda i: (0, 0)),
                  pl.BlockSpec((tm, tm), lambda i: (0, 0))],
        out_specs=(pl.BlockSpec((SUBLANES, tm), lambda i: (0, i)),
                   pl.BlockSpec((tm, LANES), lambda i: (i, 0)),
                   pl.BlockSpec((1, LANES), lambda i: (0, 0)),
                   pl.BlockSpec((tm, d // 2), lambda i: (i, 0))),
        scratch_shapes=[pltpu.VMEM((1, LANES), jnp.float32)],
        compiler_params=_cparams(("arbitrary",)),
    )(xf, r_pad, tri)


SC_CORES = 2
SC_SUBCORES = 16
SC_WINDOW = 64


def _sc_gather_rows(table, idx):
    m = idx.shape[0]
    d = table.shape[1]
    n_workers = SC_CORES * SC_SUBCORES
    per_w = m // n_workers
    n_win = per_w // SC_WINDOW
    assert per_w * n_workers == m and n_win * SC_WINDOW == per_w
    mesh = plsc.VectorSubcoreMesh(core_axis_name="c", subcore_axis_name="s")

    @functools.partial(
        pl.kernel, mesh=mesh,
        out_type=jax.ShapeDtypeStruct((m, d), table.dtype),
        scratch_types=[pltpu.VMEM((per_w,), jnp.int32),
                       pltpu.VMEM((SC_WINDOW, d), table.dtype),
                       pltpu.SemaphoreType.DMA],
    )
    def gather(table_hbm, idx_hbm, out_hbm, idx_v, rows_v, sem):
        wid = lax.axis_index("s") * SC_CORES + lax.axis_index("c")
        base = wid * per_w
        pltpu.sync_copy(idx_hbm.at[pl.ds(base, per_w)], idx_v)

        @pl.loop(0, n_win)
        def _(w):
            off = pl.multiple_of(w * SC_WINDOW, SC_WINDOW)
            pltpu.async_copy(table_hbm.at[idx_v.at[pl.ds(off, SC_WINDOW)]], rows_v, sem).wait()
            pltpu.sync_copy(rows_v, out_hbm.at[pl.ds(base + off, SC_WINDOW)])

    return gather(table, idx)


def _sc_scatter_rows(rows, idx, n_out):
    n_copies, m = idx.shape
    d = rows.shape[1]
    n_workers = SC_CORES * SC_SUBCORES
    per_w = m // n_workers
    n_win = per_w // SC_WINDOW
    assert per_w * n_workers == m and n_win * SC_WINDOW == per_w
    idx_w = idx.reshape(n_copies, n_workers, n_win, SC_WINDOW).transpose(1, 0, 2, 3)
    idx_w = idx_w.reshape(n_workers, n_copies * n_win, SC_WINDOW)
    mesh = plsc.VectorSubcoreMesh(core_axis_name="c", subcore_axis_name="s")

    @functools.partial(
        pl.kernel, mesh=mesh,
        out_type=jax.ShapeDtypeStruct((n_out, d), rows.dtype),
        scratch_types=[pltpu.VMEM((n_copies * n_win, SC_WINDOW), jnp.int32),
                       pltpu.VMEM((SC_WINDOW, d), rows.dtype)],
    )
    def scatter(rows_hbm, idx_hbm, out_hbm, idx_v, rows_v):
        wid = lax.axis_index("s") * SC_CORES + lax.axis_index("c")
        base = wid * per_w
        pltpu.sync_copy(idx_hbm.at[wid], idx_v)

        @pl.loop(0, n_win)
        def _(w):
            off = pl.multiple_of(w * SC_WINDOW, SC_WINDOW)
            pltpu.sync_copy(rows_hbm.at[pl.ds(base + off, SC_WINDOW)], rows_v)
            for k in range(n_copies):
                pltpu.sync_copy(rows_v, out_hbm.at[idx_v.at[k * n_win + w]])

    return scatter(rows, idx_w)


def _expert_kernel(te_ref, nv_ref, x_ref, w1_ref, w3_ref, w2_ref, o_ref,
                   w1s_ref, w3s_ref, w2s_ref, xb_ref, acc_ref):
    i = pl.program_id(0)
    j = pl.program_id(1)
    last = pl.num_programs(1) - 1
    n_valid = nv_ref[i]

    for s in range(MOE_TILE // MOE_SUB):
        rows = pl.ds(s * MOE_SUB, MOE_SUB)
        live = n_valid > s * MOE_SUB

        @pl.when(live & (j == 0))
        def _():
            row = lax.broadcasted_iota(jnp.int32, (MOE_SUB, 1), 0) + s * MOE_SUB
            x = jnp.where(row < n_valid, _unpack_rows(x_ref[rows, :]), 0.0)
            xb_ref[rows, :] = x.astype(jnp.bfloat16)

        @pl.when(live)
        def _():
            if s == 0:
                w1s_ref[...] = w1_ref[0].astype(jnp.bfloat16)
                w3s_ref[...] = w3_ref[0].astype(jnp.bfloat16)
                w2s_ref[...] = w2_ref[0].astype(jnp.bfloat16)
            x = xb_ref[rows, :]
            a = _dot(x, w1s_ref[...])
            hid = (a * _sigmoid(a) * _dot(x, w3s_ref[...])).astype(jnp.bfloat16)
            part = _dot(hid, w2s_ref[...])

            @pl.when(j == 0)
            def _():
                acc_ref[rows, :] = part

            @pl.when(j > 0)
            def _():
                acc_ref[rows, :] += part

        @pl.when(live & (j == last))
        def _():
            o_ref[rows, :] = _pack_rows(acc_ref[rows, :])

        @pl.when(jnp.logical_not(live) & (j == last))
        def _():
            o_ref[rows, :] = jnp.zeros((MOE_SUB, o_ref.shape[1]), o_ref.dtype)


def _experts(tile_expert, tile_valid, xs, w1, w3, w2):
    n_rows = xs.shape[0]
    d, f = w1.shape[1], w1.shape[2]
    nj = f // MOE_FCHUNK

    def chunk(i, j, nv):
        return jnp.where(nv[i] > 0, j, nj - 1)

    return pl.pallas_call(
        _expert_kernel,
        out_shape=jax.ShapeDtypeStruct((n_rows, d // 2), jnp.uint32),
        grid_spec=pltpu.PrefetchScalarGridSpec(
            num_scalar_prefetch=2,
            grid=(n_rows // MOE_TILE, nj),
            in_specs=[pl.BlockSpec((MOE_TILE, d // 2), lambda i, j, te, nv: (i, 0)),
                      pl.BlockSpec((1, d, MOE_FCHUNK), lambda i, j, te, nv: (te[i], 0, chunk(i, j, nv))),
                      pl.BlockSpec((1, d, MOE_FCHUNK), lambda i, j, te, nv: (te[i], 0, chunk(i, j, nv))),
                      pl.BlockSpec((1, MOE_FCHUNK, d), lambda i, j, te, nv: (te[i], chunk(i, j, nv), 0))],
            out_specs=pl.BlockSpec((MOE_TILE, d // 2), lambda i, j, te, nv: (i, 0)),
            scratch_shapes=[pltpu.VMEM((d, MOE_FCHUNK), jnp.bfloat16),
                            pltpu.VMEM((d, MOE_FCHUNK), jnp.bfloat16),
                            pltpu.VMEM((MOE_FCHUNK, d), jnp.bfloat16),
                            pltpu.VMEM((MOE_TILE, d), jnp.bfloat16),
                            pltpu.VMEM((MOE_TILE, d), jnp.float32)],
        ),
        compiler_params=pltpu.CompilerParams(dimension_semantics=("arbitrary", "arbitrary"),
                                             vmem_limit_bytes=MOE_VMEM_LIMIT),
    )(tile_expert, tile_valid, xs, w1, w3, w2)


def _combine_kernel(ya_ref, yb_ref, x_ref, w_ref, g_ref, b_ref, o_ref):
    w = w_ref[...]
    f = w[:, 0:1] * _unpack_rows(ya_ref[...]) + w[:, 1:2] * _unpack_rows(yb_ref[...])
    o_ref[...] = _layer_norm_rows(ALPHA * x_ref[...] + f, g_ref[...], b_ref[...])


def _combine(yg, xf, wts, g, b, tm=512):
    t, d = xf.shape
    nt = t // tm
    return pl.pallas_call(
        _combine_kernel,
        out_shape=jax.ShapeDtypeStruct((t, d), jnp.float32),
        grid=(nt,),
        in_specs=[pl.BlockSpec((tm, d // 2), lambda i: (i, 0)),
                  pl.BlockSpec((tm, d // 2), lambda i: (i + nt, 0)),
                  pl.BlockSpec((tm, d), lambda i: (i, 0)),
                  pl.BlockSpec((tm, LANES), lambda i: (i, 0)),
                  pl.BlockSpec((1, d), lambda i: (0, 0)),
                  pl.BlockSpec((1, d), lambda i: (0, 0))],
        out_specs=pl.BlockSpec((tm, d), lambda i: (i, 0)),
        compiler_params=_cparams(("parallel",)),
    )(yg, yg, xf, wts, g.reshape(1, d), b.reshape(1, d))


def _moe_ffn(xf, router, w1, w3, w2, g, b):
    t, d = xf.shape
    oi, ow, cnt, xp = _router(xf, router)
    e1, e2, rank1, rank2 = oi[0], oi[1], oi[2], oi[3]
    counts = cnt[0, :N_EXPERTS].astype(jnp.int32)
    tiles = (counts + MOE_TILE - 1) // MOE_TILE
    tile_end = jnp.cumsum(tiles)
    tile_start = tile_end - tiles
    offs = tile_start * MOE_TILE
    dest = jnp.concatenate([offs[e1] + rank1, offs[e2] + rank2])
    n_rows = TOP_K * t + N_EXPERTS * MOE_TILE
    n_tiles = n_rows // MOE_TILE
    tile_id = jnp.arange(n_tiles, dtype=jnp.int32)
    tile_expert = jnp.sum(tile_id[:, None] >= tile_end[None, :], axis=1)
    used = tile_expert < N_EXPERTS
    last_expert = jnp.sum(tile_end[-1] - 1 >= tile_end)
    tile_expert = jnp.where(used, tile_expert, last_expert).astype(jnp.int32)
    tile_valid = jnp.clip(counts[tile_expert] - (tile_id - tile_start[tile_expert]) * MOE_TILE, 0, MOE_TILE)
    tile_valid = jnp.where(used, tile_valid, 0).astype(jnp.int32)
    xs = _sc_scatter_rows(xp, dest.reshape(TOP_K, t), n_rows)
    y = _experts(tile_expert, tile_valid, xs, w1, w3, w2)
    yg = _sc_gather_rows(y, dest)
    return _combine(yg, xf, ow, g, b)


def _prep_layer(l, p, in_cols):
    bf = jnp.bfloat16
    out = {}
    out["w_in"] = _take_cols(p["w_in"][l], in_cols).astype(bf)
    deint = np.concatenate([np.arange(0, HEAD_DIM, 2), np.arange(1, HEAD_DIM, 2)])
    out["gq_a"] = jnp.tile(p["a_q_norm"][l][deint], 2).reshape(1, LANES)
    out["gk_a"] = jnp.tile(p["a_k_norm"][l][deint], 2).reshape(1, LANES)
    per_head = C_NOPE + C_ROPE
    uq_cols = []
    for h in range(C_HEADS):
        base = h * per_head
        uq_cols += [base + np.arange(C_NOPE), base + C_NOPE + np.arange(0, C_ROPE, 2),
                    base + C_NOPE + np.arange(1, C_ROPE, 2), np.full(32, -1)]
    wuq = _take_cols(p["c_w_uq"][l], np.concatenate(uq_cols))
    out["wuq"] = jnp.concatenate([wuq, jnp.zeros((64, wuq.shape[1]), wuq.dtype)], axis=0).astype(bf)
    uk_cols, uv_cols = [], []
    for h in range(C_HEADS):
        base = h * (C_NOPE + C_V)
        uk_cols += [base + np.arange(C_NOPE), np.full(64, -1)]
        uv_cols += [base + C_NOPE + np.arange(C_V)]
    out["wuk"] = _take_cols(p["c_w_ukv"][l], np.concatenate(uk_cols)).astype(bf)
    out["wuv"] = _take_cols(p["c_w_ukv"][l], np.concatenate(uv_cols)).astype(bf)
    out["gq_c"] = jnp.concatenate([p["c_q_norm"][l], jnp.zeros((64,), jnp.float32)]).reshape(1, 256)
    out["gkv_c"] = p["c_kv_norm"][l].reshape(1, C_KV_RANK)
    out["conv_w"] = jnp.concatenate([p["d_conv_w"][l][:, 0, :], jnp.zeros((1, D_CH), jnp.float32)], axis=0)
    out["conv_b"] = p["d_conv_b"][l].reshape(1, D_CH)
    out["d_ln_g"] = p["d_ln_g"][l].reshape(1, D_CH)
    out["d_ln_b"] = p["d_ln_b"][l].reshape(1, D_CH)
    a_rows = np.concatenate([(kvh * 2 + g) * HEAD_DIM + np.arange(HEAD_DIM)
                             for g in range(2) for kvh in range(A_KV_HEADS)])
    wb = p["w_branch"][l]
    out["w_branch"] = jnp.stack([wb[0][a_rows], wb[1], wb[2], wb[3]], axis=0).astype(bf)
    out["w_out"] = p["w_out"][l].astype(bf)
    return out


def kernel(x, ln_in_g, ln_in_b, w_in, a_q_norm, a_k_norm, b_rpb, c_q_norm, c_kv_norm, c_w_uq, c_w_ukv,
           d_conv_w, d_conv_b, d_ln_g, d_ln_b, w_branch, w_out, ln_mix_g, ln_mix_b,
           ffn_w1, ffn_w3, ffn_w2, moe_router, moe_w1, moe_w3, moe_w2, ln_ffn_g, ln_ffn_b):
    batch, seq, d = x.shape
    t = batch * seq
    bf = jnp.bfloat16
    params = dict(w_in=w_in, a_q_norm=a_q_norm, a_k_norm=a_k_norm, c_q_norm=c_q_norm, c_kv_norm=c_kv_norm,
                  c_w_uq=c_w_uq, c_w_ukv=c_w_ukv, d_conv_w=d_conv_w, d_conv_b=d_conv_b, d_ln_g=d_ln_g,
                  d_ln_b=d_ln_b, w_branch=w_branch, w_out=w_out)
    in_cols = _in_proj_columns()
    cos_a, sin_a, cos_c, sin_c = _rope_tables(seq)
    xf, xb = _input_ln(x.reshape(t, d), ln_in_g, ln_in_b)
    for l in range(DEPTH):
        p = _prep_layer(l, params, in_cols)
        h = _matmul(xb, p["w_in"], bf, tm=1024, tn=1280)
        ya = _attn_a(h, cos_a, sin_a, p["gq_a"], p["gk_a"], batch, seq)
        yb = _attn_b(h, _na_bias_tables(b_rpb[l], seq // GRID_W), batch, seq)
        yc = _attn_c(h, p["wuq"], p["wuk"], p["wuv"], p["gq_c"], p["gkv_c"], cos_c, sin_c, batch, seq)
        yd = _conv_branch(h, p["conv_w"], p["conv_b"], p["d_ln_g"], p["d_ln_b"], batch, seq)
        xf, xb = _merge(h, ya, yb, yc, yd, xf, p["w_branch"], p["w_out"], ln_mix_g[l], ln_mix_b[l])
        if l % 2 == 0:
            i = l // 2
            xf, xb = _ffn(xb, xf, ffn_w1[i].astype(bf), ffn_w3[i].astype(bf), ffn_w2[i].astype(bf),
                          ln_ffn_g[l], ln_ffn_b[l])
        else:
            i = l // 2
            xf = _moe_ffn(xf, moe_router[i], moe_w1[i], moe_w3[i], moe_w2[i], ln_ffn_g[l], ln_ffn_b[l])
            xb = xf.astype(bf)
    return xf.reshape(batch, seq, d)
```

```python
import functools

import numpy as np
import jax
import jax.numpy as jnp
from jax import lax
from jax.experimental import pallas as pl
from jax.experimental.pallas import tpu as pltpu
from jax.experimental.pallas import tpu_sc as plsc

D_MODEL = 1024
DEPTH = 2
GRID_W = 64
HEAD_DIM = 64
ROPE_THETA = 10000.0
RMS_EPS = 1e-6
LN_EPS = 1e-5
A_HEADS = 4
A_KV_HEADS = 2
B_HEADS = 4
NA_WIN_H = 8
NA_WIN_W = 16
C_HEADS = 4
C_NOPE = 64
C_ROPE = 32
C_V = 64
C_Q_RANK = 192
C_KV_RANK = 128
D_CH = 256
CONV_W = 31
N_BRANCH = 4
MIX_W = 256
D_FF = 2816
N_EXPERTS = 8
TOP_K = 2
D_FF_EXPERT = 3584
ALPHA = (2 * DEPTH) ** 0.25

LANES = 128
SUBLANES = 8
VMEM_LIMIT = 48 * 1024 * 1024

LOG2E = 1.4426950408889634
NEG_BIG = -1e30

COL_GATES = 0
COL_AQ = 4096
COL_AKV = 4352
COL_C = 4608
COL_D = 5120
COL_BQ = 5632
COL_BK = 5888
COL_BV = 6144
D_IN_PAD = 6400

NA_QROWS = 4
NA_KROWS = 12

MOE_TILE = 1024
MOE_SUB = 512
MOE_FCHUNK = 896
MOE_VMEM_LIMIT = 56 * 1024 * 1024


def _in_proj_columns():
    o_aq, o_ak, o_av = 0, 256, 384
    o_bq, o_bk, o_bv = 512, 768, 1024
    o_cq, o_ckv, o_kpe = 1280, 1472, 1600
    o_d, o_g = 1632, 2144
    deint = np.concatenate([np.arange(0, HEAD_DIM, 2), np.arange(1, HEAD_DIM, 2)])
    cols = [o_g + np.arange(N_BRANCH * D_MODEL)]
    for g in range(2):
        for kvh in range(A_KV_HEADS):
            cols.append(o_aq + (kvh * 2 + g) * HEAD_DIM + deint)
    for kvh in range(A_KV_HEADS):
        cols.append(o_ak + kvh * HEAD_DIM + deint)
    cols.append(o_av + np.arange(A_KV_HEADS * HEAD_DIM))
    cols.append(o_cq + np.arange(C_Q_RANK))
    cols.append(np.full(64, -1))
    cols.append(o_ckv + np.arange(C_KV_RANK))
    cols.append(np.full(64, -1))
    cols.append(o_kpe + np.arange(0, C_ROPE, 2))
    cols.append(o_kpe + np.arange(1, C_ROPE, 2))
    cols.append(np.full(32, -1))
    cols.append(o_d + np.arange(2 * D_CH))
    cols.append(o_bq + np.arange(3 * B_HEADS * HEAD_DIM))
    cols = np.concatenate(cols)
    assert cols.shape[0] == D_IN_PAD
    return cols


def _take_cols(w, cols):
    cols = np.asarray(cols)
    parts, i, n = [], 0, len(cols)
    while i < n:
        j = i + 1
        if cols[i] < 0:
            while j < n and cols[j] < 0:
                j += 1
            parts.append(jnp.zeros((w.shape[0], j - i), w.dtype))
        else:
            step = int(cols[j] - cols[i]) if j < n and cols[j] > cols[i] else 1
            while j < n and cols[j] == cols[j - 1] + step:
                j += 1
            c0, n_run = int(cols[i]), j - i
            if step == 1:
                parts.append(w[:, c0:c0 + n_run])
            else:
                assert step == 2
                base = c0 - c0 % 2
                parts.append(w[:, base:base + 2 * n_run].reshape(w.shape[0], n_run, 2)[:, :, c0 % 2])
        i = j
    return jnp.concatenate(parts, axis=1)


def _rope_tables(seq):
    t = np.arange(seq)
    row = (t // GRID_W).astype(np.float32)
    col = (t % GRID_W).astype(np.float32)

    def angles(rot_dim):
        n_freq = rot_dim // 4
        inv = jnp.asarray(ROPE_THETA, jnp.float32) ** (-jnp.arange(n_freq, dtype=jnp.float32) / n_freq)
        return jnp.concatenate([jnp.asarray(row)[:, None] * inv, jnp.asarray(col)[:, None] * inv], axis=-1)

    ang_a = angles(HEAD_DIM)
    ca, sa = jnp.cos(ang_a), jnp.sin(ang_a)
    cos_a = jnp.concatenate([ca, ca, ca, ca], axis=-1)
    sin_a = jnp.concatenate([-sa, sa, -sa, sa], axis=-1)
    ang_c = angles(C_ROPE)
    cc, sc = jnp.cos(ang_c), jnp.sin(ang_c)
    one = jnp.ones((seq, 64), jnp.float32)
    zero = jnp.zeros((seq, 64), jnp.float32)
    cos_c = jnp.concatenate([one, cc, cc, one[:, :32]], axis=-1)
    sin_c = jnp.concatenate([zero, -sc, sc, zero[:, :32]], axis=-1)
    return cos_a, sin_a, cos_c, sin_c


def _na_bias_tables(rpb, rows):
    wh = min(NA_WIN_H, rows)
    n_dr, n_dc = 2 * NA_WIN_H - 1, 2 * NA_WIN_W - 1
    qc = np.arange(GRID_W)[:, None]
    kc = np.arange(GRID_W)[None, :]
    col_start = np.clip(qc - NA_WIN_W // 2, 0, GRID_W - NA_WIN_W)
    ok_c = (kc >= col_start) & (kc < col_start + NA_WIN_W)
    dc = np.clip(kc - qc + (NA_WIN_W - 1), 0, n_dc - 1)
    oh_c = np.eye(n_dc, dtype=np.float32)[dc.reshape(-1)]
    oh_r, ok_r = [], []
    for r0 in (0, NA_QROWS, rows - NA_QROWS):
        ws = int(np.clip(r0 - 4, 0, rows - NA_KROWS))
        qr = r0 + np.arange(NA_QROWS)[:, None]
        kr = ws + np.arange(NA_KROWS)[None, :]
        row_start = np.clip(qr - wh // 2, 0, rows - wh)
        ok_r.append((kr >= row_start) & (kr < row_start + wh))
        dr = np.clip(kr - qr + (NA_WIN_H - 1), 0, n_dr - 1)
        oh_r.append(np.eye(n_dr, dtype=np.float32)[dr.reshape(-1)])
    oh_r = np.concatenate(oh_r, axis=0)
    ok = np.stack(ok_r)[:, :, None, :, None] & ok_c[None, None, :, None, :]
    ok = ok.reshape(3, 1, NA_QROWS * GRID_W, NA_KROWS * GRID_W)
    hi = lax.Precision.HIGHEST
    t1 = jnp.einsum("ma,hab->hmb", jnp.asarray(oh_r), rpb.astype(jnp.float32) * LOG2E, precision=hi)
    t2 = jnp.einsum("hmb,nb->hmn", t1, jnp.asarray(oh_c), precision=hi).astype(jnp.bfloat16)
    nh = rpb.shape[0]
    t2 = t2.reshape(nh, 3, NA_QROWS, NA_KROWS, GRID_W, GRID_W).transpose(1, 0, 2, 4, 3, 5)
    bias = t2.reshape(3, nh, NA_QROWS * GRID_W, NA_KROWS * GRID_W)
    return jnp.where(jnp.asarray(ok), bias, jnp.asarray(NEG_BIG, jnp.bfloat16))


def _cparams(sem):
    return pltpu.CompilerParams(dimension_semantics=sem, vmem_limit_bytes=VMEM_LIMIT)


def _layer_norm_rows(z, g, b):
    mu = jnp.mean(z, axis=-1, keepdims=True)
    zc = z - mu
    var = jnp.mean(zc * zc, axis=-1, keepdims=True)
    return zc * lax.rsqrt(var + LN_EPS) * g + b


def _sigmoid(x):
    return 0.5 * jnp.tanh(0.5 * x) + 0.5


def _dot(a, b):
    return jnp.dot(a, b, preferred_element_type=jnp.float32)


def _dot_nt(a, b):
    return lax.dot_general(a, b, (((1,), (1,)), ((), ())), preferred_element_type=jnp.float32)


def _ln_kernel(x_ref, g_ref, b_ref, of_ref, ob_ref):
    y = _layer_norm_rows(x_ref[...], g_ref[...], b_ref[...])
    of_ref[...] = y
    ob_ref[...] = y.astype(jnp.bfloat16)


def _input_ln(x, g, b, tm=1024):
    t, d = x.shape
    return pl.pallas_call(
        _ln_kernel,
        out_shape=(jax.ShapeDtypeStruct((t, d), jnp.float32), jax.ShapeDtypeStruct((t, d), jnp.bfloat16)),
        grid=(t // tm,),
        in_specs=[pl.BlockSpec((tm, d), lambda i: (i, 0)),
                  pl.BlockSpec((1, d), lambda i: (0, 0)),
                  pl.BlockSpec((1, d), lambda i: (0, 0))],
        out_specs=(pl.BlockSpec((tm, d), lambda i: (i, 0)), pl.BlockSpec((tm, d), lambda i: (i, 0))),
        compiler_params=_cparams(("parallel",)),
    )(x, g.reshape(1, d), b.reshape(1, d))


def _mm_kernel(a_ref, b_ref, o_ref):
    o_ref[...] = _dot(a_ref[...], b_ref[...]).astype(o_ref.dtype)


def _matmul(a, b, out_dtype, tm, tn):
    m, k = a.shape
    n = b.shape[1]
    return pl.pallas_call(
        _mm_kernel,
        out_shape=jax.ShapeDtypeStruct((m, n), out_dtype),
        grid=(m // tm, n // tn),
        in_specs=[pl.BlockSpec((tm, k), lambda i, j: (i, 0)),
                  pl.BlockSpec((k, tn), lambda i, j: (0, j))],
        out_specs=pl.BlockSpec((tm, tn), lambda i, j: (i, j)),
        compiler_params=_cparams(("parallel", "parallel")),
    )(a, b)


def _ones_halves(vblk):
    lane = lax.broadcasted_iota(jnp.int32, vblk.shape, 1)
    ones = jnp.ones_like(vblk)
    return jnp.where(lane < HEAD_DIM, vblk, ones), jnp.where(lane < HEAD_DIM, ones, vblk)


def _attend_pair(s_lo, s_hi, v_lo, v_hi, lo_q):
    outs = []
    for s, v in ((s_lo, v_lo), (s_hi, v_hi)):
        p = jnp.exp2(s - jnp.max(s, axis=-1, keepdims=True)).astype(jnp.bfloat16)
        outs.append(_dot(p, v))
    num = jnp.where(lo_q, outs[0], outs[1])
    den = jnp.where(lo_q, pltpu.roll(outs[0], HEAD_DIM, 1), pltpu.roll(outs[1], HEAD_DIM, 1))
    return num / den


def _rotate_pairs(x, cos, sin, half, lane):
    w = x.shape[-1]
    first = (lane % (2 * half)) < half
    partner = jnp.where(first, pltpu.roll(x, w - half, 1), pltpu.roll(x, half, 1))
    return x * cos + partner * sin


def _attn_a_kernel(q_ref, kv_ref, cos_ref, sin_ref, gq_ref, gk_ref, o_ref, ks_ref, va_ref, *, tq):
    s_len = q_ref.shape[0]

    def prep(x, g, cos, sin):
        lane = lax.broadcasted_iota(jnp.int32, x.shape, 1)
        lo = lane < HEAD_DIM
        x2 = x * x
        ss_lo = jnp.sum(jnp.where(lo, x2, 0.0), axis=-1, keepdims=True)
        ss_hi = jnp.sum(jnp.where(lo, 0.0, x2), axis=-1, keepdims=True)
        ms = jnp.where(lo, ss_lo, ss_hi) * (1.0 / HEAD_DIM)
        xn = x * lax.rsqrt(ms + RMS_EPS) * g
        return _rotate_pairs(xn, cos, sin, HEAD_DIM // 2, lane)

    ks_ref[...] = prep(kv_ref[:, 0:LANES].astype(jnp.float32), gk_ref[...],
                       cos_ref[...], sin_ref[...]).astype(jnp.bfloat16)
    scale = HEAD_DIM ** -0.5 * LOG2E
    lane_q = lax.broadcasted_iota(jnp.int32, (tq, LANES), 1)
    lo_q = lane_q < HEAD_DIM

    va_ref[0], va_ref[1] = _ones_halves(kv_ref[:, LANES:2 * LANES])

    def body(i, carry):
        r = pl.multiple_of(i * tq, tq)
        k = ks_ref[...]
        cos = cos_ref[pl.ds(r, tq), :]
        sin = sin_ref[pl.ds(r, tq), :]
        for g in range(2):
            qt = prep(q_ref[pl.ds(r, tq), g * LANES:(g + 1) * LANES].astype(jnp.float32), gq_ref[...], cos, sin)
            qt = (qt * scale).astype(jnp.bfloat16)
            zero = jnp.zeros_like(qt)
            out = _attend_pair(_dot_nt(jnp.where(lo_q, qt, zero), k), _dot_nt(jnp.where(lo_q, zero, qt), k),
                               va_ref[0], va_ref[1], lo_q)
            o_ref[pl.ds(r, tq), g * LANES:(g + 1) * LANES] = out.astype(o_ref.dtype)
        return carry

    lax.fori_loop(0, s_len // tq, body, 0)


def _attn_a(h, cos_a, sin_a, gq, gk, batch, seq, tq=256):
    kern = functools.partial(_attn_a_kernel, tq=tq)
    return pl.pallas_call(
        kern,
        out_shape=jax.ShapeDtypeStruct((batch * seq, MIX_W), jnp.bfloat16),
        grid=(batch,),
        in_specs=[pl.BlockSpec((seq, 256), lambda b: (b, COL_AQ // 256)),
                  pl.BlockSpec((seq, 256), lambda b: (b, COL_AKV // 256)),
                  pl.BlockSpec((seq, LANES), lambda b: (0, 0)),
                  pl.BlockSpec((seq, LANES), lambda b: (0, 0)),
                  pl.BlockSpec((1, LANES), lambda b: (0, 0)),
                  pl.BlockSpec((1, LANES), lambda b: (0, 0))],
        out_specs=pl.BlockSpec((seq, MIX_W), lambda b: (b, 0)),
        scratch_shapes=[pltpu.VMEM((seq, LANES), jnp.bfloat16), pltpu.VMEM((2, seq, LANES), jnp.bfloat16)],
        compiler_params=_cparams(("parallel",)),
    )(h, h, cos_a, sin_a, gq, gk)


def _attn_b_kernel(q_ref, k_ref, v_ref, bias_ref, o_ref, va_ref, *, rows):
    nq = NA_QROWS * GRID_W
    nk = NA_KROWS * GRID_W
    lane_q = lax.broadcasted_iota(jnp.int32, (nq, LANES), 1)
    lo_q = lane_q < HEAD_DIM
    n_groups = rows // NA_QROWS
    scale = HEAD_DIM ** -0.5 * LOG2E
    for blk in range(B_HEADS // 2):
        va_ref[2 * blk], va_ref[2 * blk + 1] = _ones_halves(v_ref[:, blk * LANES:(blk + 1) * LANES])

    def body(gi, carry):
        ws = jnp.clip(gi * NA_QROWS - 4, 0, rows - NA_KROWS)
        cls = jnp.where(gi == 0, 0, jnp.where(gi == n_groups - 1, 2, 1))
        qr = pl.multiple_of(gi * nq, nq)
        kr = pl.multiple_of(ws * GRID_W, GRID_W)
        for blk in range(B_HEADS // 2):
            sl = slice(blk * LANES, (blk + 1) * LANES)
            qt = q_ref[pl.ds(qr, nq), sl] * scale
            kw = k_ref[pl.ds(kr, nk), sl]
            zero = jnp.zeros_like(qt)
            s_lo = _dot_nt(jnp.where(lo_q, qt, zero), kw) + bias_ref[cls, 2 * blk].astype(jnp.float32)
            s_hi = _dot_nt(jnp.where(lo_q, zero, qt), kw) + bias_ref[cls, 2 * blk + 1].astype(jnp.float32)
            out = _attend_pair(s_lo, s_hi, va_ref[2 * blk, pl.ds(kr, nk), :], va_ref[2 * blk + 1, pl.ds(kr, nk), :], lo_q)
            o_ref[pl.ds(qr, nq), sl] = out.astype(o_ref.dtype)
        return carry

    lax.fori_loop(0, n_groups, body, 0)


def _attn_b(h, bias, batch, seq):
    rows = seq // GRID_W
    kern = functools.partial(_attn_b_kernel, rows=rows)
    return pl.pallas_call(
        kern,
        out_shape=jax.ShapeDtypeStruct((batch * seq, MIX_W), jnp.bfloat16),
        grid=(batch,),
        in_specs=[pl.BlockSpec((seq, 256), lambda b: (b, COL_BQ // 256)),
                  pl.BlockSpec((seq, 256), lambda b: (b, COL_BK // 256)),
                  pl.BlockSpec((seq, 256), lambda b: (b, COL_BV // 256)),
                  pl.BlockSpec(bias.shape, lambda b: (0, 0, 0, 0))],
        out_specs=pl.BlockSpec((seq, MIX_W), lambda b: (b, 0)),
        scratch_shapes=[pltpu.VMEM((B_HEADS, seq, LANES), jnp.bfloat16)],
        compiler_params=_cparams(("parallel",)),
    )(h, h, h, bias)


def _attn_c_kernel(hc_ref, wuq_ref, wuk_ref, wuv_ref, gq_ref, gkv_ref, cos_ref, sin_ref, o_ref,
                   qs_ref, ks_ref, vs_ref, *, tq):
    s_len = hc_ref.shape[0]
    lane = lax.broadcasted_iota(jnp.int32, (s_len, LANES), 1)
    cos = cos_ref[...]
    sin = sin_ref[...]

    cq = hc_ref[:, 0:256].astype(jnp.float32)
    ms = jnp.sum(cq * cq, axis=-1, keepdims=True) * (1.0 / C_Q_RANK)
    cqn = (cq * lax.rsqrt(ms + RMS_EPS) * gq_ref[...]).astype(jnp.bfloat16)
    ckv = hc_ref[:, 256:384].astype(jnp.float32)
    ms = jnp.mean(ckv * ckv, axis=-1, keepdims=True)
    kvn = (ckv * lax.rsqrt(ms + RMS_EPS) * gkv_ref[...]).astype(jnp.bfloat16)
    kpe = _rotate_pairs(hc_ref[:, 384:512].astype(jnp.float32), cos, sin, C_ROPE // 2, lane)
    v_all = _dot(kvn, wuv_ref[...]).astype(jnp.bfloat16)
    for blk in range(C_HEADS // 2):
        vs_ref[2 * blk], vs_ref[2 * blk + 1] = _ones_halves(v_all[:, blk * LANES:(blk + 1) * LANES])
    scale = (C_NOPE + C_ROPE) ** -0.5 * LOG2E
    for h in range(C_HEADS):
        sl = slice(h * LANES, (h + 1) * LANES)
        qh = _rotate_pairs(_dot(cqn, wuq_ref[:, sl]), cos, sin, C_ROPE // 2, lane) * scale
        qs_ref[:, sl] = qh.astype(jnp.bfloat16)
        ks_ref[:, sl] = (_dot(kvn, wuk_ref[:, sl]) + kpe).astype(jnp.bfloat16)

    lane_q = lax.broadcasted_iota(jnp.int32, (tq, LANES), 1)
    lo_q = lane_q < C_V

    def body(i, carry):
        r = pl.multiple_of(i * tq, tq)
        for blk in range(C_HEADS // 2):
            s = []
            for par in range(2):
                sl = slice((2 * blk + par) * LANES, (2 * blk + par + 1) * LANES)
                s.append(_dot_nt(qs_ref[pl.ds(r, tq), sl], ks_ref[:, sl]))
            out = _attend_pair(s[0], s[1], vs_ref[2 * blk], vs_ref[2 * blk + 1], lo_q)
            o_ref[pl.ds(r, tq), blk * LANES:(blk + 1) * LANES] = out.astype(o_ref.dtype)
        return carry

    lax.fori_loop(0, s_len // tq, body, 0)


def _attn_c(h, wuq, wuk, wuv, gq, gkv, cos_c, sin_c, batch, seq, tq=256):
    kern = functools.partial(_attn_c_kernel, tq=tq)
    const = lambda shape: pl.BlockSpec(shape, lambda b: (0,) * len(shape))
    return pl.pallas_call(
        kern,
        out_shape=jax.ShapeDtypeStruct((batch * seq, MIX_W), jnp.bfloat16),
        grid=(batch,),
        in_specs=[pl.BlockSpec((seq, 512), lambda b: (b, COL_C // 512)),
                  const(wuq.shape), const(wuk.shape), const(wuv.shape),
                  const(gq.shape), const(gkv.shape), const(cos_c.shape), const(sin_c.shape)],
        out_specs=pl.BlockSpec((seq, MIX_W), lambda b: (b, 0)),
        scratch_shapes=[pltpu.VMEM((seq, C_HEADS * LANES), jnp.bfloat16),
                        pltpu.VMEM((seq, C_HEADS * LANES), jnp.bfloat16),
                        pltpu.VMEM((C_HEADS, seq, LANES), jnp.bfloat16)],
        compiler_params=_cparams(("parallel",)),
    )(h, wuq, wuk, wuv, gq, gkv, cos_c, sin_c)


CONV_PAD = 16
CONV_ROWS = 128


def _conv_kernel(hd_ref, w_ref, cb_ref, g_ref, b_ref, o_ref, up_ref):
    s_len = hd_ref.shape[0]
    a = hd_ref[:, 0:D_CH].astype(jnp.float32)
    gate = hd_ref[:, D_CH:2 * D_CH].astype(jnp.float32)
    zeros = jnp.zeros((CONV_PAD, D_CH), jnp.float32)
    up_ref[0, 0:CONV_PAD, :] = zeros
    up_ref[0, CONV_PAD + s_len:2 * CONV_PAD + s_len, :] = zeros
    up_ref[0, CONV_PAD:CONV_PAD + s_len, :] = a * _sigmoid(gate)
    n_shifted = s_len + 2 * CONV_PAD - SUBLANES
    for s in range(1, SUBLANES):
        up_ref[s, 0:n_shifted, :] = up_ref[0, s:s + n_shifted, :]
    shift = CONV_PAD - CONV_W // 2
    for c in range(s_len // CONV_ROWS):
        acc = jnp.zeros((CONV_ROWS, D_CH), jnp.float32)
        for j in range(CONV_W):
            s = (j + shift) % SUBLANES
            start = c * CONV_ROWS + j + shift - s
            acc = acc + up_ref[s, start:start + CONV_ROWS, :] * w_ref[j:j + 1, :]
        u = acc + cb_ref[...]
        y = _layer_norm_rows(u, g_ref[...], b_ref[...])
        o_ref[c * CONV_ROWS:(c + 1) * CONV_ROWS, :] = (y * _sigmoid(y)).astype(o_ref.dtype)


def _conv_branch(h, w, cb, g, b, batch, seq):
    const = lambda shape: pl.BlockSpec(shape, lambda i: (0,) * len(shape))
    return pl.pallas_call(
        _conv_kernel,
        out_shape=jax.ShapeDtypeStruct((batch * seq, MIX_W), jnp.bfloat16),
        grid=(batch,),
        in_specs=[pl.BlockSpec((seq, 512), lambda i: (i, COL_D // 512)),
                  const(w.shape), const(cb.shape), const(g.shape), const(b.shape)],
        out_specs=pl.BlockSpec((seq, MIX_W), lambda i: (i, 0)),
        scratch_shapes=[pltpu.VMEM((SUBLANES, seq + 2 * CONV_PAD, D_CH), jnp.float32)],
        compiler_params=_cparams(("parallel",)),
    )(h, w, cb, g, b)


def _merge_kernel(gl_ref, ya_ref, yb_ref, yc_ref, yd_ref, x_ref, wb_ref, wo_ref, g_ref, b_ref,
                  of_ref, ob_ref):
    merged = None
    for n, y_ref in enumerate((ya_ref, yb_ref, yc_ref, yd_ref)):
        t = jnp.tanh(gl_ref[:, n * D_MODEL:(n + 1) * D_MODEL].astype(jnp.float32))
        term = (t + 1.0) * _dot(y_ref[...], wb_ref[n])
        merged = term if merged is None else merged + term
    mix = _dot(merged.astype(jnp.bfloat16), wo_ref[...])
    y = _layer_norm_rows(ALPHA * x_ref[...] + mix, g_ref[...], b_ref[...])
    of_ref[...] = y
    ob_ref[...] = y.astype(jnp.bfloat16)


def _merge(h, ya, yb, yc, yd, x, wb, wo, g, b, tm=512):
    t, d = x.shape
    row = lambda w: pl.BlockSpec((tm, w), lambda i: (i, 0))
    const = lambda shape: pl.BlockSpec(shape, lambda i: (0,) * len(shape))
    return pl.pallas_call(
        _merge_kernel,
        out_shape=(jax.ShapeDtypeStruct((t, d), jnp.float32), jax.ShapeDtypeStruct((t, d), jnp.bfloat16)),
        grid=(t // tm,),
        in_specs=[pl.BlockSpec((tm, N_BRANCH * d), lambda i: (i, COL_GATES)),
                  row(MIX_W), row(MIX_W), row(MIX_W), row(MIX_W), row(d),
                  const(wb.shape), const(wo.shape), const((1, d)), const((1, d))],
        out_specs=(row(d), row(d)),
        compiler_params=_cparams(("parallel",)),
    )(h, ya, yb, yc, yd, x, wb, wo, g.reshape(1, d), b.reshape(1, d))


def _ffn_kernel(xb_ref, xf_ref, w13_ref, w2_ref, g_ref, b_ref, of_ref, ob_ref, acc_ref):
    j = pl.program_id(1)
    tf = w2_ref.shape[0]
    ab = _dot(xb_ref[...], w13_ref[...])
    a = ab[:, :tf]
    hid = (a * _sigmoid(a) * ab[:, tf:]).astype(jnp.bfloat16)
    part = _dot(hid, w2_ref[...])

    @pl.when(j == 0)
    def _():
        acc_ref[...] = part

    @pl.when(j > 0)
    def _():
        acc_ref[...] += part

    @pl.when(j == pl.num_programs(1) - 1)
    def _():
        y = _layer_norm_rows(ALPHA * xf_ref[...] + acc_ref[...], g_ref[...], b_ref[...])
        of_ref[...] = y
        ob_ref[...] = y.astype(jnp.bfloat16)


def _ffn(xb, xf, w1, w3, w2, g, b, tm=512, tf=1408):
    t, d = xf.shape
    f = w1.shape[1]
    nj = f // tf
    w13 = jnp.concatenate([w1.reshape(d, nj, tf), w3.reshape(d, nj, tf)], axis=2).reshape(d, 2 * f)
    row = pl.BlockSpec((tm, d), lambda i, j: (i, 0))
    vec = pl.BlockSpec((1, d), lambda i, j: (0, 0))
    return pl.pallas_call(
        _ffn_kernel,
        out_shape=(jax.ShapeDtypeStruct((t, d), jnp.float32), jax.ShapeDtypeStruct((t, d), jnp.bfloat16)),
        grid=(t // tm, nj),
        in_specs=[row, row,
                  pl.BlockSpec((d, 2 * tf), lambda i, j: (0, j)),
                  pl.BlockSpec((tf, d), lambda i, j: (j, 0)),
                  vec, vec],
        out_specs=(row, row),
        scratch_shapes=[pltpu.VMEM((tm, d), jnp.float32)],
        compiler_params=_cparams(("parallel", "arbitrary")),
    )(xb, xf, w13, w2, g.reshape(1, d), b.reshape(1, d))


def _pack_rows(x):
    half = x.shape[1] // 2
    bits = lax.bitcast_convert_type(x.astype(jnp.bfloat16).astype(jnp.float32), jnp.uint32)
    return (bits[:, :half] >> 16) | (bits[:, half:] & jnp.uint32(0xFFFF0000))


def _unpack_rows(p):
    lo = lax.bitcast_convert_type(p << 16, jnp.float32)
    hi = lax.bitcast_convert_type(p & jnp.uint32(0xFFFF0000), jnp.float32)
    return jnp.concatenate([lo, hi], axis=1)


def _router_kernel(x_ref, r_ref, tri_ref, oi_ref, ow_ref, cnt_ref, xp_ref, carry_ref):
    i = pl.program_id(0)

    @pl.when(i == 0)
    def _():
        carry_ref[...] = jnp.zeros_like(carry_ref)

    x = x_ref[...]
    r = r_ref[...]
    xh = x.astype(jnp.bfloat16)
    xl = (x - xh.astype(jnp.float32)).astype(jnp.bfloat16)
    rh = r.astype(jnp.bfloat16)
    rl = (r - rh.astype(jnp.float32)).astype(jnp.bfloat16)
    r2 = jnp.concatenate([rh, rl], axis=1)
    p = _dot(xh, r2) + _dot(xl, r2)
    logits = p[:, :LANES] + p[:, LANES:]
    tm = x.shape[0]
    lane = lax.broadcasted_iota(jnp.int32, (tm, LANES), 1)
    logits = jnp.where(lane < N_EXPERTS, logits, NEG_BIG)
    m1 = jnp.max(logits, axis=-1, keepdims=True)
    i1 = jnp.min(jnp.where(logits == m1, lane, LANES), axis=-1, keepdims=True)
    rest = jnp.where(lane == i1, NEG_BIG, logits)
    m2 = jnp.max(rest, axis=-1, keepdims=True)
    i2 = jnp.min(jnp.where(rest == m2, lane, LANES), axis=-1, keepdims=True)
    e = jnp.exp(m2 - m1)
    w1 = 1.0 / (1.0 + e)
    w2 = e / (1.0 + e)
    sel1 = lane == i1
    sel2 = lane == i2
    member = jnp.where(sel1 | sel2, 1.0, 0.0)
    before = _dot(tri_ref[...], member.astype(jnp.bfloat16)) + carry_ref[...]
    rank1 = jnp.sum(jnp.where(sel1, before, 0.0), axis=-1, keepdims=True).astype(jnp.int32)
    rank2 = jnp.sum(jnp.where(sel2, before, 0.0), axis=-1, keepdims=True).astype(jnp.int32)
    carry_ref[...] += jnp.sum(member, axis=0, keepdims=True)
    oi = jnp.where(lane == 0, i1, jnp.where(lane == 1, i2, jnp.where(lane == 2, rank1, rank2)))
    oi_ref[...] = oi.T[0:SUBLANES, :]
    ow_ref[...] = jnp.where(lane == 0, w1, w2)
    cnt_ref[...] = carry_ref[...]
    xp_ref[...] = _pack_rows(x)


def _router(xf, router, tm=512):
    t, d = xf.shape
    r_pad = jnp.zeros((d, LANES), jnp.float32).at[:, :N_EXPERTS].set(router.astype(jnp.float32))
    tri = jnp.asarray(np.tril(np.ones((tm, tm), np.float32), -1), jnp.bfloat16)
    return pl.pallas_call(
        _router_kernel,
        out_shape=(jax.ShapeDtypeStruct((SUBLANES, t), jnp.int32),
                   jax.ShapeDtypeStruct((t, LANES), jnp.float32),
                   jax.ShapeDtypeStruct((1, LANES), jnp.float32),
                   jax.ShapeDtypeStruct((t, d // 2), jnp.uint32)),
        grid=(t // tm,),
        in_specs=[pl.BlockSpec((tm, d), lambda i: (i, 0)),
                  pl.BlockSpec((d, LANES), lambda i: (0, 0)),
                  pl.BlockSpec((tm, tm), lambda i: (0, 0))],
        out_specs=(pl.BlockSpec((SUBLANES, tm), lambda i: (0, i)),
                   pl.BlockSpec((tm, LANES), lambda i: (i, 0)),
                   pl.BlockSpec((1, LANES), lambda i: (0, 0)),
                   pl.BlockSpec((tm, d // 2), lambda i: (i, 0))),
        scratch_shapes=[pltpu.VMEM((1, LANES), jnp.float32)],
        compiler_params=_cparams(("arbitrary",)),
    )(xf, r_pad, tri)


SC_CORES = 2
SC_SUBCORES = 16
SC_WINDOW = 64


def _sc_gather_rows(table, idx):
    m = idx.shape[0]
    d = table.shape[1]
    n_workers = SC_CORES * SC_SUBCORES
    per_w = m // n_workers
    n_win = per_w // SC_WINDOW
    assert per_w * n_workers == m and n_win * SC_WINDOW == per_w
    mesh = plsc.VectorSubcoreMesh(core_axis_name="c", subcore_axis_name="s")

    @functools.partial(
        pl.kernel, mesh=mesh,
        out_type=jax.ShapeDtypeStruct((m, d), table.dtype),
        scratch_types=[pltpu.VMEM((per_w,), jnp.int32),
                       pltpu.VMEM((SC_WINDOW, d), table.dtype),
                       pltpu.SemaphoreType.DMA],
    )
    def gather(table_hbm, idx_hbm, out_hbm, idx_v, rows_v, sem):
        wid = lax.axis_index("s") * SC_CORES + lax.axis_index("c")
        base = wid * per_w
        pltpu.sync_copy(idx_hbm.at[pl.ds(base, per_w)], idx_v)

        @pl.loop(0, n_win)
        def _(w):
            off = pl.multiple_of(w * SC_WINDOW, SC_WINDOW)
            pltpu.async_copy(table_hbm.at[idx_v.at[pl.ds(off, SC_WINDOW)]], rows_v, sem).wait()
            pltpu.sync_copy(rows_v, out_hbm.at[pl.ds(base + off, SC_WINDOW)])

    return gather(table, idx)


def _sc_scatter_rows(rows, idx, n_out):
    n_copies, m = idx.shape
    d = rows.shape[1]
    n_workers = SC_CORES * SC_SUBCORES
    per_w = m // n_workers
    n_win = per_w // SC_WINDOW
    assert per_w * n_workers == m and n_win * SC_WINDOW == per_w
    idx_w = idx.reshape(n_copies, n_workers, n_win, SC_WINDOW).transpose(1, 0, 2, 3)
    idx_w = idx_w.reshape(n_workers, n_copies * n_win, SC_WINDOW)
    mesh = plsc.VectorSubcoreMesh(core_axis_name="c", subcore_axis_name="s")

    @functools.partial(
        pl.kernel, mesh=mesh,
        out_type=jax.ShapeDtypeStruct((n_out, d), rows.dtype),
        scratch_types=[pltpu.VMEM((n_copies * n_win, SC_WINDOW), jnp.int32),
                       pltpu.VMEM((SC_WINDOW, d), rows.dtype)],
    )
    def scatter(rows_hbm, idx_hbm, out_hbm, idx_v, rows_v):
        wid = lax.axis_index("s") * SC_CORES + lax.axis_index("c")
        base = wid * per_w
        pltpu.sync_copy(idx_hbm.at[wid], idx_v)

        @pl.loop(0, n_win)
        def _(w):
            off = pl.multiple_of(w * SC_WINDOW, SC_WINDOW)
            pltpu.sync_copy(rows_hbm.at[pl.ds(base + off, SC_WINDOW)], rows_v)
            for k in range(n_copies):
                pltpu.sync_copy(rows_v, out_hbm.at[idx_v.at[k * n_win + w]])

    return scatter(rows, idx_w)


def _expert_kernel(te_ref, nv_ref, x_ref, w1_ref, w3_ref, w2_ref, o_ref,
                   w13s_ref, w2s_ref, xb_ref, acc_ref):
    i = pl.program_id(0)
    j = pl.program_id(1)
    last = pl.num_programs(1) - 1
    n_valid = nv_ref[i]

    for s in range(MOE_TILE // MOE_SUB):
        rows = pl.ds(s * MOE_SUB, MOE_SUB)
        live = n_valid > s * MOE_SUB

        @pl.when(live & (j == 0))
        def _():
            row = lax.broadcasted_iota(jnp.int32, (MOE_SUB, 1), 0) + s * MOE_SUB
            x = jnp.where(row < n_valid, _unpack_rows(x_ref[rows, :]), 0.0)
            xb_ref[rows, :] = x.astype(jnp.bfloat16)

        @pl.when(live)
        def _():
            if s == 0:
                w13s_ref[:, 0:MOE_FCHUNK] = w1_ref[0].astype(jnp.bfloat16)
                w13s_ref[:, MOE_FCHUNK:2 * MOE_FCHUNK] = w3_ref[0].astype(jnp.bfloat16)
                w2s_ref[...] = w2_ref[0].astype(jnp.bfloat16)
            x = xb_ref[rows, :]
            ab = _dot(x, w13s_ref[...])
            a = ab[:, :MOE_FCHUNK]
            hid = (a * _sigmoid(a) * ab[:, MOE_FCHUNK:]).astype(jnp.bfloat16)
            part = _dot(hid, w2s_ref[...])

            @pl.when(j == 0)
            def _():
                acc_ref[rows, :] = part

            @pl.when(j > 0)
            def _():
                acc_ref[rows, :] += part

        @pl.when(live & (j == last))
        def _():
            o_ref[rows, :] = _pack_rows(acc_ref[rows, :])

        @pl.when(jnp.logical_not(live) & (j == last))
        def _():
            o_ref[rows, :] = jnp.zeros((MOE_SUB, o_ref.shape[1]), o_ref.dtype)


def _experts(tile_expert, tile_valid, xs, w1, w3, w2):
    n_rows = xs.shape[0]
    d, f = w1.shape[1], w1.shape[2]
    nj = f // MOE_FCHUNK

    def chunk(i, j, nv):
        return jnp.where(nv[i] > 0, j, nj - 1)

    return pl.pallas_call(
        _expert_kernel,
        out_shape=jax.ShapeDtypeStruct((n_rows, d // 2), jnp.uint32),
        grid_spec=pltpu.PrefetchScalarGridSpec(
            num_scalar_prefetch=2,
            grid=(n_rows // MOE_TILE, nj),
            in_specs=[pl.BlockSpec((MOE_TILE, d // 2), lambda i, j, te, nv: (i, 0)),
                      pl.BlockSpec((1, d, MOE_FCHUNK), lambda i, j, te, nv: (te[i], 0, chunk(i, j, nv))),
                      pl.BlockSpec((1, d, MOE_FCHUNK), lambda i, j, te, nv: (te[i], 0, chunk(i, j, nv))),
                      pl.BlockSpec((1, MOE_FCHUNK, d), lambda i, j, te, nv: (te[i], chunk(i, j, nv), 0))],
            out_specs=pl.BlockSpec((MOE_TILE, d // 2), lambda i, j, te, nv: (i, 0)),
            scratch_shapes=[pltpu.VMEM((d, 2 * MOE_FCHUNK), jnp.bfloat16),
                            pltpu.VMEM((MOE_FCHUNK, d), jnp.bfloat16),
                            pltpu.VMEM((MOE_TILE, d), jnp.bfloat16),
                            pltpu.VMEM((MOE_TILE, d), jnp.float32)],
        ),
        compiler_params=pltpu.CompilerParams(dimension_semantics=("arbitrary", "arbitrary"),
                                             vmem_limit_bytes=MOE_VMEM_LIMIT),
    )(tile_expert, tile_valid, xs, w1, w3, w2)


def _combine_kernel(ya_ref, yb_ref, x_ref, w_ref, g_ref, b_ref, o_ref):
    w = w_ref[...]
    f = w[:, 0:1] * _unpack_rows(ya_ref[...]) + w[:, 1:2] * _unpack_rows(yb_ref[...])
    o_ref[...] = _layer_norm_rows(ALPHA * x_ref[...] + f, g_ref[...], b_ref[...])


def _combine(yg, xf, wts, g, b, tm=512):
    t, d = xf.shape
    nt = t // tm
    return pl.pallas_call(
        _combine_kernel,
        out_shape=jax.ShapeDtypeStruct((t, d), jnp.float32),
        grid=(nt,),
        in_specs=[pl.BlockSpec((tm, d // 2), lambda i: (i, 0)),
                  pl.BlockSpec((tm, d // 2), lambda i: (i + nt, 0)),
                  pl.BlockSpec((tm, d), lambda i: (i, 0)),
                  pl.BlockSpec((tm, LANES), lambda i: (i, 0)),
                  pl.BlockSpec((1, d), lambda i: (0, 0)),
                  pl.BlockSpec((1, d), lambda i: (0, 0))],
        out_specs=pl.BlockSpec((tm, d), lambda i: (i, 0)),
        compiler_params=_cparams(("parallel",)),
    )(yg, yg, xf, wts, g.reshape(1, d), b.reshape(1, d))


def _moe_ffn(xf, router, w1, w3, w2, g, b):
    t, d = xf.shape
    oi, ow, cnt, xp = _router(xf, router)
    e1, e2, rank1, rank2 = oi[0], oi[1], oi[2], oi[3]
    counts = cnt[0, :N_EXPERTS].astype(jnp.int32)
    tiles = (counts + MOE_TILE - 1) // MOE_TILE
    tile_end = jnp.cumsum(tiles)
    tile_start = tile_end - tiles
    offs = tile_start * MOE_TILE
    dest = jnp.concatenate([offs[e1] + rank1, offs[e2] + rank2])
    n_rows = TOP_K * t + N_EXPERTS * MOE_TILE
    n_tiles = n_rows // MOE_TILE
    tile_id = jnp.arange(n_tiles, dtype=jnp.int32)
    tile_expert = jnp.sum(tile_id[:, None] >= tile_end[None, :], axis=1)
    used = tile_expert < N_EXPERTS
    last_expert = jnp.sum(tile_end[-1] - 1 >= tile_end)
    tile_expert = jnp.where(used, tile_expert, last_expert).astype(jnp.int32)
    tile_valid = jnp.clip(counts[tile_expert] - (tile_id - tile_start[tile_expert]) * MOE_TILE, 0, MOE_TILE)
    tile_valid = jnp.where(used, tile_valid, 0).astype(jnp.int32)
    xs = _sc_scatter_rows(xp, dest.reshape(TOP_K, t), n_rows)
    y = _experts(tile_expert, tile_valid, xs, w1, w3, w2)
    yg = _sc_gather_rows(y, dest)
    return _combine(yg, xf, ow, g, b)


def _prep_layer(l, p, in_cols):
    bf = jnp.bfloat16
    out = {}
    col_scale = np.where(np.arange(D_IN_PAD) < N_BRANCH * D_MODEL, 0.5, 1.0).astype(np.float32)
    out["w_in"] = (_take_cols(p["w_in"][l], in_cols) * col_scale).astype(bf)
    deint = np.concatenate([np.arange(0, HEAD_DIM, 2), np.arange(1, HEAD_DIM, 2)])
    out["gq_a"] = jnp.tile(p["a_q_norm"][l][deint], 2).reshape(1, LANES)
    out["gk_a"] = jnp.tile(p["a_k_norm"][l][deint], 2).reshape(1, LANES)
    per_head = C_NOPE + C_ROPE
    uq_cols = []
    for h in range(C_HEADS):
        base = h * per_head
        uq_cols += [base + np.arange(C_NOPE), base + C_NOPE + np.arange(0, C_ROPE, 2),
                    base + C_NOPE + np.arange(1, C_ROPE, 2), np.full(32, -1)]
    wuq = _take_cols(p["c_w_uq"][l], np.concatenate(uq_cols))
    out["wuq"] = jnp.concatenate([wuq, jnp.zeros((64, wuq.shape[1]), wuq.dtype)], axis=0).astype(bf)
    uk_cols, uv_cols = [], []
    for h in range(C_HEADS):
        base = h * (C_NOPE + C_V)
        uk_cols += [base + np.arange(C_NOPE), np.full(64, -1)]
        uv_cols += [base + C_NOPE + np.arange(C_V)]
    out["wuk"] = _take_cols(p["c_w_ukv"][l], np.concatenate(uk_cols)).astype(bf)
    out["wuv"] = _take_cols(p["c_w_ukv"][l], np.concatenate(uv_cols)).astype(bf)
    out["gq_c"] = jnp.concatenate([p["c_q_norm"][l], jnp.zeros((64,), jnp.float32)]).reshape(1, 256)
    out["gkv_c"] = p["c_kv_norm"][l].reshape(1, C_KV_RANK)
    out["conv_w"] = jnp.concatenate([p["d_conv_w"][l][:, 0, :], jnp.zeros((1, D_CH), jnp.float32)], axis=0)
    out["conv_b"] = p["d_conv_b"][l].reshape(1, D_CH)
    out["d_ln_g"] = p["d_ln_g"][l].reshape(1, D_CH)
    out["d_ln_b"] = p["d_ln_b"][l].reshape(1, D_CH)
    a_rows = np.concatenate([(kvh * 2 + g) * HEAD_DIM + np.arange(HEAD_DIM)
                             for g in range(2) for kvh in range(A_KV_HEADS)])
    wb = p["w_branch"][l]
    out["w_branch"] = (0.5 * jnp.stack([wb[0][a_rows], wb[1], wb[2], wb[3]], axis=0)).astype(bf)
    out["w_out"] = p["w_out"][l].astype(bf)
    return out


def kernel(x, ln_in_g, ln_in_b, w_in, a_q_norm, a_k_norm, b_rpb, c_q_norm, c_kv_norm, c_w_uq, c_w_ukv,
           d_conv_w, d_conv_b, d_ln_g, d_ln_b, w_branch, w_out, ln_mix_g, ln_mix_b,
           ffn_w1, ffn_w3, ffn_w2, moe_router, moe_w1, moe_w3, moe_w2, ln_ffn_g, ln_ffn_b):
    batch, seq, d = x.shape
    t = batch * seq
    bf = jnp.bfloat16
    params = dict(w_in=w_in, a_q_norm=a_q_norm, a_k_norm=a_k_norm, c_q_norm=c_q_norm, c_kv_norm=c_kv_norm,
                  c_w_uq=c_w_uq, c_w_ukv=c_w_ukv, d_conv_w=d_conv_w, d_conv_b=d_conv_b, d_ln_g=d_ln_g,
                  d_ln_b=d_ln_b, w_branch=w_branch, w_out=w_out)
    in_cols = _in_proj_columns()
    cos_a, sin_a, cos_c, sin_c = _rope_tables(seq)
    xf, xb = _input_ln(x.reshape(t, d), ln_in_g, ln_in_b)
    for l in range(DEPTH):
        p = _prep_layer(l, params, in_cols)
        h = _matmul(xb, p["w_in"], bf, tm=1024, tn=1280)
        ya = _attn_a(h, cos_a, sin_a, p["gq_a"], p["gk_a"], batch, seq)
        yb = _attn_b(h, _na_bias_tables(b_rpb[l], seq // GRID_W), batch, seq)
        yc = _attn_c(h, p["wuq"], p["wuk"], p["wuv"], p["gq_c"], p["gkv_c"], cos_c, sin_c, batch, seq)
        yd = _conv_branch(h, p["conv_w"], p["conv_b"], p["d_ln_g"], p["d_ln_b"], batch, seq)
        xf, xb = _merge(h, ya, yb, yc, yd, xf, p["w_branch"], p["w_out"], ln_mix_g[l], ln_mix_b[l])
        if l % 2 == 0:
            i = l // 2
            xf, xb = _ffn(xb, xf, ffn_w1[i].astype(bf), ffn_w3[i].astype(bf), ffn_w2[i].astype(bf),
                          ln_ffn_g[l], ln_ffn_b[l])
        else:
            i = l // 2
            xf = _moe_ffn(xf, moe_router[i], moe_w1[i], moe_w3[i], moe_w2[i], ln_ffn_g[l], ln_ffn_b[l])
            xb = xf.astype(bf)
    return xf.reshape(batch, seq, d)
```

```python
import functools

import numpy as np
import jax
import jax.numpy as jnp
from jax import lax
from jax.experimental import pallas as pl
from jax.experimental.pallas import tpu as pltpu
from jax.experimental.pallas import tpu_sc as plsc

D_MODEL = 1024
DEPTH = 2
GRID_W = 64
HEAD_DIM = 64
ROPE_THETA = 10000.0
RMS_EPS = 1e-6
LN_EPS = 1e-5
A_HEADS = 4
A_KV_HEADS = 2
B_HEADS = 4
NA_WIN_H = 8
NA_WIN_W = 16
C_HEADS = 4
C_NOPE = 64
C_ROPE = 32
C_V = 64
C_Q_RANK = 192
C_KV_RANK = 128
D_CH = 256
CONV_W = 31
N_BRANCH = 4
MIX_W = 256
D_FF = 2816
N_EXPERTS = 8
TOP_K = 2
D_FF_EXPERT = 3584
ALPHA = (2 * DEPTH) ** 0.25

LANES = 128
SUBLANES = 8
VMEM_LIMIT = 48 * 1024 * 1024

LOG2E = 1.4426950408889634
NEG_BIG = -1e30

COL_GATES = 0
COL_AQ = 4096
COL_AKV = 4352
COL_C = 4608
COL_D = 5120
COL_BQ = 5632
COL_BK = 5888
COL_BV = 6144
D_IN_PAD = 6400

NA_QROWS = 4
NA_KROWS = 12

MOE_TILE = 1024
MOE_SUB = 512
MOE_FCHUNK = 896
MOE_VMEM_LIMIT = 56 * 1024 * 1024


def _in_proj_columns():
    o_aq, o_ak, o_av = 0, 256, 384
    o_bq, o_bk, o_bv = 512, 768, 1024
    o_cq, o_ckv, o_kpe = 1280, 1472, 1600
    o_d, o_g = 1632, 2144
    deint = np.concatenate([np.arange(0, HEAD_DIM, 2), np.arange(1, HEAD_DIM, 2)])
    cols = [o_g + np.arange(N_BRANCH * D_MODEL)]
    for g in range(2):
        for kvh in range(A_KV_HEADS):
            cols.append(o_aq + (kvh * 2 + g) * HEAD_DIM + deint)
    for kvh in range(A_KV_HEADS):
        cols.append(o_ak + kvh * HEAD_DIM + deint)
    cols.append(o_av + np.arange(A_KV_HEADS * HEAD_DIM))
    cols.append(o_cq + np.arange(C_Q_RANK))
    cols.append(np.full(64, -1))
    cols.append(o_ckv + np.arange(C_KV_RANK))
    cols.append(np.full(64, -1))
    cols.append(o_kpe + np.arange(0, C_ROPE, 2))
    cols.append(o_kpe + np.arange(1, C_ROPE, 2))
    cols.append(np.full(32, -1))
    cols.append(o_d + np.arange(2 * D_CH))
    cols.append(o_bq + np.arange(3 * B_HEADS * HEAD_DIM))
    cols = np.concatenate(cols)
    assert cols.shape[0] == D_IN_PAD
    return cols


def _take_cols(w, cols):
    cols = np.asarray(cols)
    parts, i, n = [], 0, len(cols)
    while i < n:
        j = i + 1
        if cols[i] < 0:
            while j < n and cols[j] < 0:
                j += 1
            parts.append(jnp.zeros((w.shape[0], j - i), w.dtype))
        else:
            step = int(cols[j] - cols[i]) if j < n and cols[j] > cols[i] else 1
            while j < n and cols[j] == cols[j - 1] + step:
                j += 1
            c0, n_run = int(cols[i]), j - i
            if step == 1:
                parts.append(w[:, c0:c0 + n_run])
            else:
                assert step == 2
                base = c0 - c0 % 2
                parts.append(w[:, base:base + 2 * n_run].reshape(w.shape[0], n_run, 2)[:, :, c0 % 2])
        i = j
    return jnp.concatenate(parts, axis=1)


def _rope_tables(seq):
    t = np.arange(seq)
    row = (t // GRID_W).astype(np.float32)
    col = (t % GRID_W).astype(np.float32)

    def angles(rot_dim):
        n_freq = rot_dim // 4
        inv = jnp.asarray(ROPE_THETA, jnp.float32) ** (-jnp.arange(n_freq, dtype=jnp.float32) / n_freq)
        return jnp.concatenate([jnp.asarray(row)[:, None] * inv, jnp.asarray(col)[:, None] * inv], axis=-1)

    ang_a = angles(HEAD_DIM)
    ca, sa = jnp.cos(ang_a), jnp.sin(ang_a)
    cos_a = jnp.concatenate([ca, ca, ca, ca], axis=-1)
    sin_a = jnp.concatenate([-sa, sa, -sa, sa], axis=-1)
    ang_c = angles(C_ROPE)
    cc, sc = jnp.cos(ang_c), jnp.sin(ang_c)
    one = jnp.ones((seq, 64), jnp.float32)
    zero = jnp.zeros((seq, 64), jnp.float32)
    cos_c = jnp.concatenate([one, cc, cc, one[:, :32]], axis=-1)
    sin_c = jnp.concatenate([zero, -sc, sc, zero[:, :32]], axis=-1)
    return cos_a, sin_a, cos_c, sin_c


def _na_bias_tables(rpb, rows):
    wh = min(NA_WIN_H, rows)
    n_dr, n_dc = 2 * NA_WIN_H - 1, 2 * NA_WIN_W - 1
    qc = np.arange(GRID_W)[:, None]
    kc = np.arange(GRID_W)[None, :]
    col_start = np.clip(qc - NA_WIN_W // 2, 0, GRID_W - NA_WIN_W)
    ok_c = (kc >= col_start) & (kc < col_start + NA_WIN_W)
    dc = np.clip(kc - qc + (NA_WIN_W - 1), 0, n_dc - 1)
    oh_c = np.eye(n_dc, dtype=np.float32)[dc.reshape(-1)]
    oh_r, ok_r = [], []
    for r0 in (0, NA_QROWS, rows - NA_QROWS):
        ws = int(np.clip(r0 - 4, 0, rows - NA_KROWS))
        qr = r0 + np.arange(NA_QROWS)[:, None]
        kr = ws + np.arange(NA_KROWS)[None, :]
        row_start = np.clip(qr - wh // 2, 0, rows - wh)
        ok_r.append((kr >= row_start) & (kr < row_start + wh))
        dr = np.clip(kr - qr + (NA_WIN_H - 1), 0, n_dr - 1)
        oh_r.append(np.eye(n_dr, dtype=np.float32)[dr.reshape(-1)])
    oh_r = np.concatenate(oh_r, axis=0)
    ok = np.stack(ok_r)[:, :, None, :, None] & ok_c[None, None, :, None, :]
    ok = ok.reshape(3, 1, NA_QROWS * GRID_W, NA_KROWS * GRID_W)
    hi = lax.Precision.HIGHEST
    t1 = jnp.einsum("ma,hab->hmb", jnp.asarray(oh_r), rpb.astype(jnp.float32) * LOG2E, precision=hi)
    t2 = jnp.einsum("hmb,nb->hmn", t1, jnp.asarray(oh_c), precision=hi).astype(jnp.bfloat16)
    nh = rpb.shape[0]
    t2 = t2.reshape(nh, 3, NA_QROWS, NA_KROWS, GRID_W, GRID_W).transpose(1, 0, 2, 4, 3, 5)
    bias = t2.reshape(3, nh, NA_QROWS * GRID_W, NA_KROWS * GRID_W)
    return jnp.where(jnp.asarray(ok), bias, jnp.asarray(NEG_BIG, jnp.bfloat16))


def _cparams(sem):
    return pltpu.CompilerParams(dimension_semantics=sem, vmem_limit_bytes=VMEM_LIMIT)


def _layer_norm_rows(z, g, b):
    mu = jnp.mean(z, axis=-1, keepdims=True)
    zc = z - mu
    var = jnp.mean(zc * zc, axis=-1, keepdims=True)
    return zc * lax.rsqrt(var + LN_EPS) * g + b


def _sigmoid(x):
    return 0.5 * jnp.tanh(0.5 * x) + 0.5


def _dot(a, b):
    return jnp.dot(a, b, preferred_element_type=jnp.float32)


def _dot_nt(a, b):
    return lax.dot_general(a, b, (((1,), (1,)), ((), ())), preferred_element_type=jnp.float32)


def _ln_kernel(x_ref, g_ref, b_ref, of_ref, ob_ref):
    y = _layer_norm_rows(x_ref[...], g_ref[...], b_ref[...])
    of_ref[...] = y
    ob_ref[...] = y.astype(jnp.bfloat16)


def _input_ln(x, g, b, tm=1024):
    t, d = x.shape
    return pl.pallas_call(
        _ln_kernel,
        out_shape=(jax.ShapeDtypeStruct((t, d), jnp.float32), jax.ShapeDtypeStruct((t, d), jnp.bfloat16)),
        grid=(t // tm,),
        in_specs=[pl.BlockSpec((tm, d), lambda i: (i, 0)),
                  pl.BlockSpec((1, d), lambda i: (0, 0)),
                  pl.BlockSpec((1, d), lambda i: (0, 0))],
        out_specs=(pl.BlockSpec((tm, d), lambda i: (i, 0)), pl.BlockSpec((tm, d), lambda i: (i, 0))),
        compiler_params=_cparams(("parallel",)),
    )(x, g.reshape(1, d), b.reshape(1, d))


def _mm_kernel(a_ref, b_ref, o_ref):
    o_ref[...] = _dot(a_ref[...], b_ref[...]).astype(o_ref.dtype)


def _matmul(a, b, out_dtype, tm, tn):
    m, k = a.shape
    n = b.shape[1]
    return pl.pallas_call(
        _mm_kernel,
        out_shape=jax.ShapeDtypeStruct((m, n), out_dtype),
        grid=(m // tm, n // tn),
        in_specs=[pl.BlockSpec((tm, k), lambda i, j: (i, 0)),
                  pl.BlockSpec((k, tn), lambda i, j: (0, j))],
        out_specs=pl.BlockSpec((tm, tn), lambda i, j: (i, j)),
        compiler_params=_cparams(("parallel", "parallel")),
    )(a, b)


def _ones_halves(vblk):
    lane = lax.broadcasted_iota(jnp.int32, vblk.shape, 1)
    ones = jnp.ones_like(vblk)
    return jnp.where(lane < HEAD_DIM, vblk, ones), jnp.where(lane < HEAD_DIM, ones, vblk)


def _attend_pair(s_lo, s_hi, v_lo, v_hi, lo_q):
    outs = []
    for s, v in ((s_lo, v_lo), (s_hi, v_hi)):
        p = jnp.exp2(s - jnp.max(s, axis=-1, keepdims=True)).astype(jnp.bfloat16)
        outs.append(_dot(p, v))
    num = jnp.where(lo_q, outs[0], outs[1])
    den = jnp.where(lo_q, pltpu.roll(outs[0], HEAD_DIM, 1), pltpu.roll(outs[1], HEAD_DIM, 1))
    return num / den


def _rotate_pairs(x, cos, sin, half, lane):
    w = x.shape[-1]
    first = (lane % (2 * half)) < half
    partner = jnp.where(first, pltpu.roll(x, w - half, 1), pltpu.roll(x, half, 1))
    return x * cos + partner * sin


def _attn_a_kernel(q_ref, kv_ref, cos_ref, sin_ref, gq_ref, gk_ref, o_ref, ks_ref, va_ref, *, tq):
    s_len = q_ref.shape[0]

    def prep(x, g, cos, sin):
        lane = lax.broadcasted_iota(jnp.int32, x.shape, 1)
        lo = lane < HEAD_DIM
        x2 = x * x
        ss_lo = jnp.sum(jnp.where(lo, x2, 0.0), axis=-1, keepdims=True)
        ss_hi = jnp.sum(jnp.where(lo, 0.0, x2), axis=-1, keepdims=True)
        ms = jnp.where(lo, ss_lo, ss_hi) * (1.0 / HEAD_DIM)
        xn = x * lax.rsqrt(ms + RMS_EPS) * g
        return _rotate_pairs(xn, cos, sin, HEAD_DIM // 2, lane)

    ks_ref[...] = prep(kv_ref[:, 0:LANES].astype(jnp.float32), gk_ref[...],
                       cos_ref[...], sin_ref[...]).astype(jnp.bfloat16)
    scale = HEAD_DIM ** -0.5 * LOG2E
    lane_q = lax.broadcasted_iota(jnp.int32, (tq, LANES), 1)
    lo_q = lane_q < HEAD_DIM

    va_ref[0], va_ref[1] = _ones_halves(kv_ref[:, LANES:2 * LANES])

    def body(i, carry):
        r = pl.multiple_of(i * tq, tq)
        k = ks_ref[...]
        cos = cos_ref[pl.ds(r, tq), :]
        sin = sin_ref[pl.ds(r, tq), :]
        for g in range(2):
            qt = prep(q_ref[pl.ds(r, tq), g * LANES:(g + 1) * LANES].astype(jnp.float32), gq_ref[...], cos, sin)
            qt = (qt * scale).astype(jnp.bfloat16)
            zero = jnp.zeros_like(qt)
            out = _attend_pair(_dot_nt(jnp.where(lo_q, qt, zero), k), _dot_nt(jnp.where(lo_q, zero, qt), k),
                               va_ref[0], va_ref[1], lo_q)
            o_ref[pl.ds(r, tq), g * LANES:(g + 1) * LANES] = out.astype(o_ref.dtype)
        return carry

    lax.fori_loop(0, s_len // tq, body, 0)


def _attn_a(h, cos_a, sin_a, gq, gk, batch, seq, tq=256):
    kern = functools.partial(_attn_a_kernel, tq=tq)
    return pl.pallas_call(
        kern,
        out_shape=jax.ShapeDtypeStruct((batch * seq, MIX_W), jnp.bfloat16),
        grid=(batch,),
        in_specs=[pl.BlockSpec((seq, 256), lambda b: (b, COL_AQ // 256)),
                  pl.BlockSpec((seq, 256), lambda b: (b, COL_AKV // 256)),
                  pl.BlockSpec((seq, LANES), lambda b: (0, 0)),
                  pl.BlockSpec((seq, LANES), lambda b: (0, 0)),
                  pl.BlockSpec((1, LANES), lambda b: (0, 0)),
                  pl.BlockSpec((1, LANES), lambda b: (0, 0))],
        out_specs=pl.BlockSpec((seq, MIX_W), lambda b: (b, 0)),
        scratch_shapes=[pltpu.VMEM((seq, LANES), jnp.bfloat16), pltpu.VMEM((2, seq, LANES), jnp.bfloat16)],
        compiler_params=_cparams(("parallel",)),
    )(h, h, cos_a, sin_a, gq, gk)


def _attn_b_kernel(q_ref, k_ref, v_ref, bias_ref, o_ref, va_ref, *, rows):
    nq = NA_QROWS * GRID_W
    nk = NA_KROWS * GRID_W
    lane_q = lax.broadcasted_iota(jnp.int32, (nq, LANES), 1)
    lo_q = lane_q < HEAD_DIM
    n_groups = rows // NA_QROWS
    scale = HEAD_DIM ** -0.5 * LOG2E
    for blk in range(B_HEADS // 2):
        va_ref[2 * blk], va_ref[2 * blk + 1] = _ones_halves(v_ref[:, blk * LANES:(blk + 1) * LANES])

    def body(gi, carry):
        ws = jnp.clip(gi * NA_QROWS - 4, 0, rows - NA_KROWS)
        cls = jnp.where(gi == 0, 0, jnp.where(gi == n_groups - 1, 2, 1))
        qr = pl.multiple_of(gi * nq, nq)
        kr = pl.multiple_of(ws * GRID_W, GRID_W)
        for blk in range(B_HEADS // 2):
            sl = slice(blk * LANES, (blk + 1) * LANES)
            qt = q_ref[pl.ds(qr, nq), sl] * scale
            kw = k_ref[pl.ds(kr, nk), sl]
            zero = jnp.zeros_like(qt)
            s_lo = _dot_nt(jnp.where(lo_q, qt, zero), kw) + bias_ref[cls, 2 * blk].astype(jnp.float32)
            s_hi = _dot_nt(jnp.where(lo_q, zero, qt), kw) + bias_ref[cls, 2 * blk + 1].astype(jnp.float32)
            out = _attend_pair(s_lo, s_hi, va_ref[2 * blk, pl.ds(kr, nk), :], va_ref[2 * blk + 1, pl.ds(kr, nk), :], lo_q)
            o_ref[pl.ds(qr, nq), sl] = out.astype(o_ref.dtype)
        return carry

    lax.fori_loop(0, n_groups, body, 0)


def _attn_b(h, bias, batch, seq):
    rows = seq // GRID_W
    kern = functools.partial(_attn_b_kernel, rows=rows)
    return pl.pallas_call(
        kern,
        out_shape=jax.ShapeDtypeStruct((batch * seq, MIX_W), jnp.bfloat16),
        grid=(batch,),
        in_specs=[pl.BlockSpec((seq, 256), lambda b: (b, COL_BQ // 256)),
                  pl.BlockSpec((seq, 256), lambda b: (b, COL_BK // 256)),
                  pl.BlockSpec((seq, 256), lambda b: (b, COL_BV // 256)),
                  pl.BlockSpec(bias.shape, lambda b: (0, 0, 0, 0))],
        out_specs=pl.BlockSpec((seq, MIX_W), lambda b: (b, 0)),
        scratch_shapes=[pltpu.VMEM((B_HEADS, seq, LANES), jnp.bfloat16)],
        compiler_params=_cparams(("parallel",)),
    )(h, h, h, bias)


def _attn_c_kernel(hc_ref, wuq_ref, wuk_ref, wuv_ref, gq_ref, gkv_ref, cos_ref, sin_ref, o_ref,
                   qs_ref, ks_ref, vs_ref, *, tq):
    s_len = hc_ref.shape[0]
    lane = lax.broadcasted_iota(jnp.int32, (s_len, LANES), 1)
    cos = cos_ref[...]
    sin = sin_ref[...]

    cq = hc_ref[:, 0:256].astype(jnp.float32)
    ms = jnp.sum(cq * cq, axis=-1, keepdims=True) * (1.0 / C_Q_RANK)
    cqn = (cq * lax.rsqrt(ms + RMS_EPS) * gq_ref[...]).astype(jnp.bfloat16)
    ckv = hc_ref[:, 256:384].astype(jnp.float32)
    ms = jnp.mean(ckv * ckv, axis=-1, keepdims=True)
    kvn = (ckv * lax.rsqrt(ms + RMS_EPS) * gkv_ref[...]).astype(jnp.bfloat16)
    kpe = _rotate_pairs(hc_ref[:, 384:512].astype(jnp.float32), cos, sin, C_ROPE // 2, lane)
    v_all = _dot(kvn, wuv_ref[...]).astype(jnp.bfloat16)
    for blk in range(C_HEADS // 2):
        vs_ref[2 * blk], vs_ref[2 * blk + 1] = _ones_halves(v_all[:, blk * LANES:(blk + 1) * LANES])
    scale = (C_NOPE + C_ROPE) ** -0.5 * LOG2E
    for h in range(C_HEADS):
        sl = slice(h * LANES, (h + 1) * LANES)
        qh = _rotate_pairs(_dot(cqn, wuq_ref[:, sl]), cos, sin, C_ROPE // 2, lane) * scale
        qs_ref[:, sl] = qh.astype(jnp.bfloat16)
        ks_ref[:, sl] = (_dot(kvn, wuk_ref[:, sl]) + kpe).astype(jnp.bfloat16)

    lane_q = lax.broadcasted_iota(jnp.int32, (tq, LANES), 1)
    lo_q = lane_q < C_V

    def body(i, carry):
        r = pl.multiple_of(i * tq, tq)
        for blk in range(C_HEADS // 2):
            s = []
            for par in range(2):
                sl = slice((2 * blk + par) * LANES, (2 * blk + par + 1) * LANES)
                s.append(_dot_nt(qs_ref[pl.ds(r, tq), sl], ks_ref[:, sl]))
            out = _attend_pair(s[0], s[1], vs_ref[2 * blk], vs_ref[2 * blk + 1], lo_q)
            o_ref[pl.ds(r, tq), blk * LANES:(blk + 1) * LANES] = out.astype(o_ref.dtype)
        return carry

    lax.fori_loop(0, s_len // tq, body, 0)


def _attn_c(h, wuq, wuk, wuv, gq, gkv, cos_c, sin_c, batch, seq, tq=256):
    kern = functools.partial(_attn_c_kernel, tq=tq)
    const = lambda shape: pl.BlockSpec(shape, lambda b: (0,) * len(shape))
    return pl.pallas_call(
        kern,
        out_shape=jax.ShapeDtypeStruct((batch * seq, MIX_W), jnp.bfloat16),
        grid=(batch,),
        in_specs=[pl.BlockSpec((seq, 512), lambda b: (b, COL_C // 512)),
                  const(wuq.shape), const(wuk.shape), const(wuv.shape),
                  const(gq.shape), const(gkv.shape), const(cos_c.shape), const(sin_c.shape)],
        out_specs=pl.BlockSpec((seq, MIX_W), lambda b: (b, 0)),
        scratch_shapes=[pltpu.VMEM((seq, C_HEADS * LANES), jnp.bfloat16),
                        pltpu.VMEM((seq, C_HEADS * LANES), jnp.bfloat16),
                        pltpu.VMEM((C_HEADS, seq, LANES), jnp.bfloat16)],
        compiler_params=_cparams(("parallel",)),
    )(h, wuq, wuk, wuv, gq, gkv, cos_c, sin_c)


CONV_PAD = 16
CONV_ROWS = 128


def _conv_kernel(hd_ref, w_ref, cb_ref, g_ref, b_ref, o_ref, up_ref):
    s_len = hd_ref.shape[0]
    a = hd_ref[:, 0:D_CH].astype(jnp.float32)
    gate = hd_ref[:, D_CH:2 * D_CH].astype(jnp.float32)
    zeros = jnp.zeros((CONV_PAD, D_CH), jnp.float32)
    up_ref[0, 0:CONV_PAD, :] = zeros
    up_ref[0, CONV_PAD + s_len:2 * CONV_PAD + s_len, :] = zeros
    up_ref[0, CONV_PAD:CONV_PAD + s_len, :] = a * _sigmoid(gate)
    n_shifted = s_len + 2 * CONV_PAD - SUBLANES
    for s in range(1, SUBLANES):
        up_ref[s, 0:n_shifted, :] = up_ref[0, s:s + n_shifted, :]
    shift = CONV_PAD - CONV_W // 2
    for c in range(s_len // CONV_ROWS):
        acc = jnp.zeros((CONV_ROWS, D_CH), jnp.float32)
        for j in range(CONV_W):
            s = (j + shift) % SUBLANES
            start = c * CONV_ROWS + j + shift - s
            acc = acc + up_ref[s, start:start + CONV_ROWS, :] * w_ref[j:j + 1, :]
        u = acc + cb_ref[...]
        y = _layer_norm_rows(u, g_ref[...], b_ref[...])
        o_ref[c * CONV_ROWS:(c + 1) * CONV_ROWS, :] = (y * _sigmoid(y)).astype(o_ref.dtype)


def _conv_branch(h, w, cb, g, b, batch, seq):
    const = lambda shape: pl.BlockSpec(shape, lambda i: (0,) * len(shape))
    return pl.pallas_call(
        _conv_kernel,
        out_shape=jax.ShapeDtypeStruct((batch * seq, MIX_W), jnp.bfloat16),
        grid=(batch,),
        in_specs=[pl.BlockSpec((seq, 512), lambda i: (i, COL_D // 512)),
                  const(w.shape), const(cb.shape), const(g.shape), const(b.shape)],
        out_specs=pl.BlockSpec((seq, MIX_W), lambda i: (i, 0)),
        scratch_shapes=[pltpu.VMEM((SUBLANES, seq + 2 * CONV_PAD, D_CH), jnp.float32)],
        compiler_params=_cparams(("parallel",)),
    )(h, w, cb, g, b)


def _merge_kernel(gl_ref, ya_ref, yb_ref, yc_ref, yd_ref, x_ref, wb_ref, wo_ref, g_ref, b_ref,
                  of_ref, ob_ref):
    merged = None
    for n, y_ref in enumerate((ya_ref, yb_ref, yc_ref, yd_ref)):
        t = jnp.tanh(gl_ref[:, n * D_MODEL:(n + 1) * D_MODEL].astype(jnp.float32))
        term = (t + 1.0) * _dot(y_ref[...], wb_ref[n])
        merged = term if merged is None else merged + term
    mix = _dot(merged.astype(jnp.bfloat16), wo_ref[...])
    y = _layer_norm_rows(ALPHA * x_ref[...] + mix, g_ref[...], b_ref[...])
    of_ref[...] = y
    ob_ref[...] = y.astype(jnp.bfloat16)


def _merge(h, ya, yb, yc, yd, x, wb, wo, g, b, tm=512):
    t, d = x.shape
    row = lambda w: pl.BlockSpec((tm, w), lambda i: (i, 0))
    const = lambda shape: pl.BlockSpec(shape, lambda i: (0,) * len(shape))
    return pl.pallas_call(
        _merge_kernel,
        out_shape=(jax.ShapeDtypeStruct((t, d), jnp.float32), jax.ShapeDtypeStruct((t, d), jnp.bfloat16)),
        grid=(t // tm,),
        in_specs=[pl.BlockSpec((tm, N_BRANCH * d), lambda i: (i, COL_GATES)),
                  row(MIX_W), row(MIX_W), row(MIX_W), row(MIX_W), row(d),
                  const(wb.shape), const(wo.shape), const((1, d)), const((1, d))],
        out_specs=(row(d), row(d)),
        compiler_params=_cparams(("parallel",)),
    )(h, ya, yb, yc, yd, x, wb, wo, g.reshape(1, d), b.reshape(1, d))


def _ffn_kernel(xb_ref, xf_ref, w13_ref, w2_ref, g_ref, b_ref, of_ref, ob_ref, *, tf):
    x = xb_ref[...]
    f = w2_ref.shape[0]
    acc = None
    for c in range(f // tf):
        ab = _dot(x, w13_ref[:, 2 * c * tf:2 * (c + 1) * tf])
        a = ab[:, :tf]
        hid = (a * _sigmoid(a) * ab[:, tf:]).astype(jnp.bfloat16)
        part = _dot(hid, w2_ref[c * tf:(c + 1) * tf, :])
        acc = part if acc is None else acc + part
    y = _layer_norm_rows(ALPHA * xf_ref[...] + acc, g_ref[...], b_ref[...])
    of_ref[...] = y
    ob_ref[...] = y.astype(jnp.bfloat16)


def _ffn(xb, xf, w1, w3, w2, g, b, tm=512, tf=1408):
    t, d = xf.shape
    f = w1.shape[1]
    w13 = jnp.concatenate([w[:, j * tf:(j + 1) * tf] for j in range(f // tf) for w in (w1, w3)], axis=1)
    row = pl.BlockSpec((tm, d), lambda i: (i, 0))
    vec = pl.BlockSpec((1, d), lambda i: (0, 0))
    resident = lambda shape: pl.BlockSpec(shape, lambda i: (0, 0), pipeline_mode=pl.Buffered(1))
    return pl.pallas_call(
        functools.partial(_ffn_kernel, tf=tf),
        out_shape=(jax.ShapeDtypeStruct((t, d), jnp.float32), jax.ShapeDtypeStruct((t, d), jnp.bfloat16)),
        grid=(t // tm,),
        in_specs=[row, row, resident(w13.shape), resident(w2.shape), vec, vec],
        out_specs=(row, row),
        compiler_params=_cparams(("parallel",)),
    )(xb, xf, w13, w2, g.reshape(1, d), b.reshape(1, d))


def _pack_rows(x):
    half = x.shape[1] // 2
    bits = lax.bitcast_convert_type(x.astype(jnp.bfloat16).astype(jnp.float32), jnp.uint32)
    return (bits[:, :half] >> 16) | (bits[:, half:] & jnp.uint32(0xFFFF0000))


def _unpack_rows(p):
    lo = lax.bitcast_convert_type(p << 16, jnp.float32)
    hi = lax.bitcast_convert_type(p & jnp.uint32(0xFFFF0000), jnp.float32)
    return jnp.concatenate([lo, hi], axis=1)


def _router_kernel(x_ref, r_ref, tri_ref, oi_ref, ow_ref, cnt_ref, xp_ref, carry_ref):
    i = pl.program_id(0)

    @pl.when(i == 0)
    def _():
        carry_ref[...] = jnp.zeros_like(carry_ref)

    x = x_ref[...]
    r = r_ref[...]
    xh = x.astype(jnp.bfloat16)
    xl = (x - xh.astype(jnp.float32)).astype(jnp.bfloat16)
    rh = r.astype(jnp.bfloat16)
    rl = (r - rh.astype(jnp.float32)).astype(jnp.bfloat16)
    r2 = jnp.concatenate([rh, rl], axis=1)
    p = _dot(xh, r2) + _dot(xl, r2)
    logits = p[:, :LANES] + p[:, LANES:]
    tm = x.shape[0]
    lane = lax.broadcasted_iota(jnp.int32, (tm, LANES), 1)
    logits = jnp.where(lane < N_EXPERTS, logits, NEG_BIG)
    m1 = jnp.max(logits, axis=-1, keepdims=True)
    i1 = jnp.min(jnp.where(logits == m1, lane, LANES), axis=-1, keepdims=True)
    rest = jnp.where(lane == i1, NEG_BIG, logits)
    m2 = jnp.max(rest, axis=-1, keepdims=True)
    i2 = jnp.min(jnp.where(rest == m2, lane, LANES), axis=-1, keepdims=True)
    e = jnp.exp(m2 - m1)
    w1 = 1.0 / (1.0 + e)
    w2 = e / (1.0 + e)
    sel1 = lane == i1
    sel2 = lane == i2
    member = jnp.where(sel1 | sel2, 1.0, 0.0)
    before = _dot(tri_ref[...], member.astype(jnp.bfloat16)) + carry_ref[...]
    rank1 = jnp.sum(jnp.where(sel1, before, 0.0), axis=-1, keepdims=True).astype(jnp.int32)
    rank2 = jnp.sum(jnp.where(sel2, before, 0.0), axis=-1, keepdims=True).astype(jnp.int32)
    carry_ref[...] += jnp.sum(member, axis=0, keepdims=True)
    oi = jnp.where(lane == 0, i1, jnp.where(lane == 1, i2, jnp.where(lane == 2, rank1, rank2)))
    oi_ref[...] = oi.T[0:SUBLANES, :]
    ow_ref[...] = jnp.where(lane == 0, w1, w2)
    cnt_ref[...] = carry_ref[...]
    xp_ref[...] = _pack_rows(x)


def _router(xf, router, tm=512):
    t, d = xf.shape
    r_pad = jnp.zeros((d, LANES), jnp.float32).at[:, :N_EXPERTS].set(router.astype(jnp.float32))
    tri = jnp.asarray(np.tril(np.ones((tm, tm), np.float32), -1), jnp.bfloat16)
    return pl.pallas_call(
        _router_kernel,
        out_shape=(jax.ShapeDtypeStruct((SUBLANES, t), jnp.int32),
                   jax.ShapeDtypeStruct((t, LANES), jnp.float32),
                   jax.ShapeDtypeStruct((1, LANES), jnp.float32),
                   jax.ShapeDtypeStruct((t, d // 2), jnp.uint32)),
        grid=(t // tm,),
        in_specs=[pl.BlockSpec((tm, d), lambda i: (i, 0)),
                  pl.BlockSpec((d, LANES), lambda i: (0, 0)),
                  pl.BlockSpec((tm, tm), lambda i: (0, 0))],
        out_specs=(pl.BlockSpec((SUBLANES, tm), lambda i: (0, i)),
                   pl.BlockSpec((tm, LANES), lambda i: (i, 0)),
                   pl.BlockSpec((1, LANES), lambda i: (0, 0)),
                   pl.BlockSpec((tm, d // 2), lambda i: (i, 0))),
        scratch_shapes=[pltpu.VMEM((1, LANES), jnp.float32)],
        compiler_params=_cparams(("arbitrary",)),
    )(xf, r_pad, tri)


SC_CORES = 2
SC_SUBCORES = 16
SC_WINDOW = 64


def _sc_gather_rows(table, idx):
    m = idx.shape[0]
    d = table.shape[1]
    n_workers = SC_CORES * SC_SUBCORES
    per_w = m // n_workers
    n_win = per_w // SC_WINDOW
    assert per_w * n_workers == m and n_win * SC_WINDOW == per_w
    mesh = plsc.VectorSubcoreMesh(core_axis_name="c", subcore_axis_name="s")

    @functools.partial(
        pl.kernel, mesh=mesh,
        out_type=jax.ShapeDtypeStruct((m, d), table.dtype),
        scratch_types=[pltpu.VMEM((per_w,), jnp.int32),
                       pltpu.VMEM((SC_WINDOW, d), table.dtype),
                       pltpu.SemaphoreType.DMA],
    )
    def gather(table_hbm, idx_hbm, out_hbm, idx_v, rows_v, sem):
        wid = lax.axis_index("s") * SC_CORES + lax.axis_index("c")
        base = wid * per_w
        pltpu.sync_copy(idx_hbm.at[pl.ds(base, per_w)], idx_v)

        @pl.loop(0, n_win)
        def _(w):
            off = pl.multiple_of(w * SC_WINDOW, SC_WINDOW)
            pltpu.async_copy(table_hbm.at[idx_v.at[pl.ds(off, SC_WINDOW)]], rows_v, sem).wait()
            pltpu.sync_copy(rows_v, out_hbm.at[pl.ds(base + off, SC_WINDOW)])

    return gather(table, idx)


def _sc_scatter_rows(rows, idx, n_out):
    n_copies, m = idx.shape
    d = rows.shape[1]
    n_workers = SC_CORES * SC_SUBCORES
    per_w = m // n_workers
    n_win = per_w // SC_WINDOW
    assert per_w * n_workers == m and n_win * SC_WINDOW == per_w
    idx_w = idx.reshape(n_copies, n_workers, n_win, SC_WINDOW).transpose(1, 0, 2, 3)
    idx_w = idx_w.reshape(n_workers, n_copies * n_win, SC_WINDOW)
    mesh = plsc.VectorSubcoreMesh(core_axis_name="c", subcore_axis_name="s")

    @functools.partial(
        pl.kernel, mesh=mesh,
        out_type=jax.ShapeDtypeStruct((n_out, d), rows.dtype),
        scratch_types=[pltpu.VMEM((n_copies * n_win, SC_WINDOW), jnp.int32),
                       pltpu.VMEM((SC_WINDOW, d), rows.dtype)],
    )
    def scatter(rows_hbm, idx_hbm, out_hbm, idx_v, rows_v):
        wid = lax.axis_index("s") * SC_CORES + lax.axis_index("c")
        base = wid * per_w
        pltpu.sync_copy(idx_hbm.at[wid], idx_v)

        @pl.loop(0, n_win)
        def _(w):
            off = pl.multiple_of(w * SC_WINDOW, SC_WINDOW)
            pltpu.sync_copy(rows_hbm.at[pl.ds(base + off, SC_WINDOW)], rows_v)
            for k in range(n_copies):
                pltpu.sync_copy(rows_v, out_hbm.at[idx_v.at[k * n_win + w]])

    return scatter(rows, idx_w)


def _expert_kernel(te_ref, nv_ref, x_ref, w1_ref, w3_ref, w2_ref, o_ref,
                   w13s_ref, w2s_ref, xb_ref, acc_ref):
    i = pl.program_id(0)
    j = pl.program_id(1)
    last = pl.num_programs(1) - 1
    n_valid = nv_ref[i]
    n_piece = MOE_TILE // MOE_SUB
    piece_rows = [pl.ds(s * MOE_SUB, MOE_SUB) for s in range(n_piece)]

    for s, rows in enumerate(piece_rows):
        @pl.when((n_valid > s * MOE_SUB) & (j == 0))
        def _():
            row = lax.broadcasted_iota(jnp.int32, (MOE_SUB, 1), 0) + s * MOE_SUB
            x = jnp.where(row < n_valid, _unpack_rows(x_ref[rows, :]), 0.0)
            xb_ref[rows, :] = x.astype(jnp.bfloat16)
            acc_ref[rows, :] = jnp.zeros((MOE_SUB, acc_ref.shape[1]), jnp.float32)

    def compute(n_live):
        w13s_ref[:, 0:MOE_FCHUNK] = w1_ref[0].astype(jnp.bfloat16)
        w13s_ref[:, MOE_FCHUNK:2 * MOE_FCHUNK] = w3_ref[0].astype(jnp.bfloat16)
        w2s_ref[...] = w2_ref[0].astype(jnp.bfloat16)
        for rows in piece_rows[:n_live]:
            ab = _dot(xb_ref[rows, :], w13s_ref[...])
            a = ab[:, :MOE_FCHUNK]
            hid = (a * _sigmoid(a) * ab[:, MOE_FCHUNK:]).astype(jnp.bfloat16)
            acc_ref[rows, :] += _dot(hid, w2s_ref[...])

    for n_live in range(1, n_piece + 1):
        upper = n_live * MOE_SUB if n_live < n_piece else MOE_TILE
        pl.when((n_valid > (n_live - 1) * MOE_SUB) & (n_valid <= upper))(functools.partial(compute, n_live))

    for s, rows in enumerate(piece_rows):
        @pl.when((n_valid > s * MOE_SUB) & (j == last))
        def _():
            o_ref[rows, :] = _pack_rows(acc_ref[rows, :])

        @pl.when((n_valid <= s * MOE_SUB) & (j == last))
        def _():
            o_ref[rows, :] = jnp.zeros((MOE_SUB, o_ref.shape[1]), o_ref.dtype)


def _experts(tile_expert, tile_valid, xs, w1, w3, w2):
    n_rows = xs.shape[0]
    d, f = w1.shape[1], w1.shape[2]
    nj = f // MOE_FCHUNK

    def chunk(i, j, nv):
        return jnp.where(nv[i] > 0, j, nj - 1)

    return pl.pallas_call(
        _expert_kernel,
        out_shape=jax.ShapeDtypeStruct((n_rows, d // 2), jnp.uint32),
        grid_spec=pltpu.PrefetchScalarGridSpec(
            num_scalar_prefetch=2,
            grid=(n_rows // MOE_TILE, nj),
            in_specs=[pl.BlockSpec((MOE_TILE, d // 2), lambda i, j, te, nv: (i, 0)),
                      pl.BlockSpec((1, d, MOE_FCHUNK), lambda i, j, te, nv: (te[i], 0, chunk(i, j, nv))),
                      pl.BlockSpec((1, d, MOE_FCHUNK), lambda i, j, te, nv: (te[i], 0, chunk(i, j, nv))),
                      pl.BlockSpec((1, MOE_FCHUNK, d), lambda i, j, te, nv: (te[i], chunk(i, j, nv), 0))],
            out_specs=pl.BlockSpec((MOE_TILE, d // 2), lambda i, j, te, nv: (i, 0)),
            scratch_shapes=[pltpu.VMEM((d, 2 * MOE_FCHUNK), jnp.bfloat16),
                            pltpu.VMEM((MOE_FCHUNK, d), jnp.bfloat16),
                            pltpu.VMEM((MOE_TILE, d), jnp.bfloat16),
                            pltpu.VMEM((MOE_TILE, d), jnp.float32)],
        ),
        compiler_params=pltpu.CompilerParams(dimension_semantics=("arbitrary", "arbitrary"),
                                             vmem_limit_bytes=MOE_VMEM_LIMIT),
    )(tile_expert, tile_valid, xs, w1, w3, w2)


def _combine_kernel(ya_ref, yb_ref, x_ref, w_ref, g_ref, b_ref, o_ref):
    w = w_ref[...]
    f = w[:, 0:1] * _unpack_rows(ya_ref[...]) + w[:, 1:2] * _unpack_rows(yb_ref[...])
    o_ref[...] = _layer_norm_rows(ALPHA * x_ref[...] + f, g_ref[...], b_ref[...])


def _combine(yg, xf, wts, g, b, tm=512):
    t, d = xf.shape
    nt = t // tm
    return pl.pallas_call(
        _combine_kernel,
        out_shape=jax.ShapeDtypeStruct((t, d), jnp.float32),
        grid=(nt,),
        in_specs=[pl.BlockSpec((tm, d // 2), lambda i: (i, 0)),
                  pl.BlockSpec((tm, d // 2), lambda i: (i + nt, 0)),
                  pl.BlockSpec((tm, d), lambda i: (i, 0)),
                  pl.BlockSpec((tm, LANES), lambda i: (i, 0)),
                  pl.BlockSpec((1, d), lambda i: (0, 0)),
                  pl.BlockSpec((1, d), lambda i: (0, 0))],
        out_specs=pl.BlockSpec((tm, d), lambda i: (i, 0)),
        compiler_params=_cparams(("parallel",)),
    )(yg, yg, xf, wts, g.reshape(1, d), b.reshape(1, d))


def _moe_ffn(xf, router, w1, w3, w2, g, b):
    t, d = xf.shape
    oi, ow, cnt, xp = _router(xf, router)
    e1, e2, rank1, rank2 = oi[0], oi[1], oi[2], oi[3]
    counts = cnt[0, :N_EXPERTS].astype(jnp.int32)
    tiles = (counts + MOE_TILE - 1) // MOE_TILE
    tile_end = jnp.cumsum(tiles)
    tile_start = tile_end - tiles
    offs = tile_start * MOE_TILE
    dest = jnp.concatenate([offs[e1] + rank1, offs[e2] + rank2])
    n_rows = TOP_K * t + N_EXPERTS * MOE_TILE
    n_tiles = n_rows // MOE_TILE
    tile_id = jnp.arange(n_tiles, dtype=jnp.int32)
    tile_expert = jnp.sum(tile_id[:, None] >= tile_end[None, :], axis=1)
    used = tile_expert < N_EXPERTS
    last_expert = jnp.sum(tile_end[-1] - 1 >= tile_end)
    tile_expert = jnp.where(used, tile_expert, last_expert).astype(jnp.int32)
    tile_valid = jnp.clip(counts[tile_expert] - (tile_id - tile_start[tile_expert]) * MOE_TILE, 0, MOE_TILE)
    tile_valid = jnp.where(used, tile_valid, 0).astype(jnp.int32)
    xs = _sc_scatter_rows(xp, dest.reshape(TOP_K, t), n_rows)
    y = _experts(tile_expert, tile_valid, xs, w1, w3, w2)
    yg = _sc_gather_rows(y, dest)
    return _combine(yg, xf, ow, g, b)


def _prep_layer(l, p, in_cols):
    bf = jnp.bfloat16
    out = {}
    n_gate = N_BRANCH * D_MODEL
    w_l = p["w_in"][l]
    gate_cols = (0.5 * w_l[:, int(in_cols[0]):int(in_cols[0]) + n_gate]).astype(bf)
    out["w_in"] = jnp.concatenate([gate_cols, _take_cols(w_l, in_cols[n_gate:]).astype(bf)], axis=1)
    deint = np.concatenate([np.arange(0, HEAD_DIM, 2), np.arange(1, HEAD_DIM, 2)])
    out["gq_a"] = jnp.tile(p["a_q_norm"][l][deint], 2).reshape(1, LANES)
    out["gk_a"] = jnp.tile(p["a_k_norm"][l][deint], 2).reshape(1, LANES)
    per_head = C_NOPE + C_ROPE
    uq_cols = []
    for h in range(C_HEADS):
        base = h * per_head
        uq_cols += [base + np.arange(C_NOPE), base + C_NOPE + np.arange(0, C_ROPE, 2),
                    base + C_NOPE + np.arange(1, C_ROPE, 2), np.full(32, -1)]
    wuq = _take_cols(p["c_w_uq"][l], np.concatenate(uq_cols))
    out["wuq"] = jnp.concatenate([wuq, jnp.zeros((64, wuq.shape[1]), wuq.dtype)], axis=0).astype(bf)
    uk_cols, uv_cols = [], []
    for h in range(C_HEADS):
        base = h * (C_NOPE + C_V)
        uk_cols += [base + np.arange(C_NOPE), np.full(64, -1)]
        uv_cols += [base + C_NOPE + np.arange(C_V)]
    out["wuk"] = _take_cols(p["c_w_ukv"][l], np.concatenate(uk_cols)).astype(bf)
    out["wuv"] = _take_cols(p["c_w_ukv"][l], np.concatenate(uv_cols)).astype(bf)
    out["gq_c"] = jnp.concatenate([p["c_q_norm"][l], jnp.zeros((64,), jnp.float32)]).reshape(1, 256)
    out["gkv_c"] = p["c_kv_norm"][l].reshape(1, C_KV_RANK)
    out["conv_w"] = jnp.concatenate([p["d_conv_w"][l][:, 0, :], jnp.zeros((1, D_CH), jnp.float32)], axis=0)
    out["conv_b"] = p["d_conv_b"][l].reshape(1, D_CH)
    out["d_ln_g"] = p["d_ln_g"][l].reshape(1, D_CH)
    out["d_ln_b"] = p["d_ln_b"][l].reshape(1, D_CH)
    a_rows = np.concatenate([(kvh * 2 + g) * HEAD_DIM + np.arange(HEAD_DIM)
                             for g in range(2) for kvh in range(A_KV_HEADS)])
    wb = p["w_branch"][l]
    out["w_branch"] = (0.5 * jnp.stack([wb[0][a_rows], wb[1], wb[2], wb[3]], axis=0)).astype(bf)
    out["w_out"] = p["w_out"][l].astype(bf)
    return out


def kernel(x, ln_in_g, ln_in_b, w_in, a_q_norm, a_k_norm, b_rpb, c_q_norm, c_kv_norm, c_w_uq, c_w_ukv,
           d_conv_w, d_conv_b, d_ln_g, d_ln_b, w_branch, w_out, ln_mix_g, ln_mix_b,
           ffn_w1, ffn_w3, ffn_w2, moe_router, moe_w1, moe_w3, moe_w2, ln_ffn_g, ln_ffn_b):
    batch, seq, d = x.shape
    t = batch * seq
    bf = jnp.bfloat16
    params = dict(w_in=w_in, a_q_norm=a_q_norm, a_k_norm=a_k_norm, c_q_norm=c_q_norm, c_kv_norm=c_kv_norm,
                  c_w_uq=c_w_uq, c_w_ukv=c_w_ukv, d_conv_w=d_conv_w, d_conv_b=d_conv_b, d_ln_g=d_ln_g,
                  d_ln_b=d_ln_b, w_branch=w_branch, w_out=w_out)
    in_cols = _in_proj_columns()
    cos_a, sin_a, cos_c, sin_c = _rope_tables(seq)
    xf, xb = _input_ln(x.reshape(t, d), ln_in_g, ln_in_b)
    for l in range(DEPTH):
        p = _prep_layer(l, params, in_cols)
        h = _matmul(xb, p["w_in"], bf, tm=1024, tn=3200)
        ya = _attn_a(h, cos_a, sin_a, p["gq_a"], p["gk_a"], batch, seq)
        yb = _attn_b(h, _na_bias_tables(b_rpb[l], seq // GRID_W), batch, seq)
        yc = _attn_c(h, p["wuq"], p["wuk"], p["wuv"], p["gq_c"], p["gkv_c"], cos_c, sin_c, batch, seq)
        yd = _conv_branch(h, p["conv_w"], p["conv_b"], p["d_ln_g"], p["d_ln_b"], batch, seq)
        xf, xb = _merge(h, ya, yb, yc, yd, xf, p["w_branch"], p["w_out"], ln_mix_g[l], ln_mix_b[l])
        if l % 2 == 0:
            i = l // 2
            xf, xb = _ffn(xb, xf, ffn_w1[i].astype(bf), ffn_w3[i].astype(bf), ffn_w2[i].astype(bf),
                          ln_ffn_g[l], ln_ffn_b[l])
        else:
            i = l // 2
            xf = _moe_ffn(xf, moe_router[i], moe_w1[i], moe_w3[i], moe_w2[i], ln_ffn_g[l], ln_ffn_b[l])
            xb = xf.astype(bf)
    return xf.reshape(batch, seq, d)
```

```python
import functools

import numpy as np
import jax
import jax.numpy as jnp
from jax import lax
from jax.experimental import pallas as pl
from jax.experimental.pallas import tpu as pltpu
from jax.experimental.pallas import tpu_sc as plsc

D_MODEL = 1024
DEPTH = 2
GRID_W = 64
HEAD_DIM = 64
ROPE_THETA = 10000.0
RMS_EPS = 1e-6
LN_EPS = 1e-5
A_HEADS = 4
A_KV_HEADS = 2
B_HEADS = 4
NA_WIN_H = 8
NA_WIN_W = 16
C_HEADS = 4
C_NOPE = 64
C_ROPE = 32
C_V = 64
C_Q_RANK = 192
C_KV_RANK = 128
D_CH = 256
CONV_W = 31
N_BRANCH = 4
MIX_W = 256
D_FF = 2816
N_EXPERTS = 8
TOP_K = 2
D_FF_EXPERT = 3584
ALPHA = (2 * DEPTH) ** 0.25

LANES = 128
SUBLANES = 8
VMEM_LIMIT = 48 * 1024 * 1024

LOG2E = 1.4426950408889634
NEG_BIG = -1e30

COL_GATES = 0
COL_AQ = 4096
COL_AKV = 4352
COL_C = 4608
COL_D = 5120
COL_BQ = 5632
COL_BK = 5888
COL_BV = 6144
D_IN_PAD = 6400

NA_QROWS = 4
NA_KROWS = 12

MOE_TILE = 2048
MOE_SUB = 512
MOE_FCHUNK = 512
MOE_VMEM_LIMIT = 56 * 1024 * 1024


def _in_proj_columns():
    o_aq, o_ak, o_av = 0, 256, 384
    o_bq, o_bk, o_bv = 512, 768, 1024
    o_cq, o_ckv, o_kpe = 1280, 1472, 1600
    o_d, o_g = 1632, 2144
    deint = np.concatenate([np.arange(0, HEAD_DIM, 2), np.arange(1, HEAD_DIM, 2)])
    cols = [o_g + np.arange(N_BRANCH * D_MODEL)]
    for g in range(2):
        for kvh in range(A_KV_HEADS):
            cols.append(o_aq + (kvh * 2 + g) * HEAD_DIM + deint)
    for kvh in range(A_KV_HEADS):
        cols.append(o_ak + kvh * HEAD_DIM + deint)
    cols.append(o_av + np.arange(A_KV_HEADS * HEAD_DIM))
    cols.append(o_cq + np.arange(C_Q_RANK))
    cols.append(np.full(64, -1))
    cols.append(o_ckv + np.arange(C_KV_RANK))
    cols.append(np.full(64, -1))
    cols.append(o_kpe + np.arange(0, C_ROPE, 2))
    cols.append(o_kpe + np.arange(1, C_ROPE, 2))
    cols.append(np.full(32, -1))
    cols.append(o_d + np.arange(2 * D_CH))
    cols.append(o_bq + np.arange(3 * B_HEADS * HEAD_DIM))
    cols = np.concatenate(cols)
    assert cols.shape[0] == D_IN_PAD
    return cols


def _take_cols(w, cols):
    cols = np.asarray(cols)
    parts, i, n = [], 0, len(cols)
    while i < n:
        j = i + 1
        if cols[i] < 0:
            while j < n and cols[j] < 0:
                j += 1
            parts.append(jnp.zeros((w.shape[0], j - i), w.dtype))
        else:
            step = int(cols[j] - cols[i]) if j < n and cols[j] > cols[i] else 1
            while j < n and cols[j] == cols[j - 1] + step:
                j += 1
            c0, n_run = int(cols[i]), j - i
            if step == 1:
                parts.append(w[:, c0:c0 + n_run])
            else:
                assert step == 2
                base = c0 - c0 % 2
                parts.append(w[:, base:base + 2 * n_run].reshape(w.shape[0], n_run, 2)[:, :, c0 % 2])
        i = j
    return jnp.concatenate(parts, axis=1)


def _rope_tables(seq):
    t = np.arange(seq)
    row = (t // GRID_W).astype(np.float32)
    col = (t % GRID_W).astype(np.float32)

    def angles(rot_dim):
        n_freq = rot_dim // 4
        inv = jnp.asarray(ROPE_THETA, jnp.float32) ** (-jnp.arange(n_freq, dtype=jnp.float32) / n_freq)
        return jnp.concatenate([jnp.asarray(row)[:, None] * inv, jnp.asarray(col)[:, None] * inv], axis=-1)

    ang_a = angles(HEAD_DIM)
    ca, sa = jnp.cos(ang_a), jnp.sin(ang_a)
    cos_a = jnp.concatenate([ca, ca, ca, ca], axis=-1)
    sin_a = jnp.concatenate([-sa, sa, -sa, sa], axis=-1)
    ang_c = angles(C_ROPE)
    cc, sc = jnp.cos(ang_c), jnp.sin(ang_c)
    one = jnp.ones((seq, 64), jnp.float32)
    zero = jnp.zeros((seq, 64), jnp.float32)
    cos_c = jnp.concatenate([one, cc, cc, one[:, :32]], axis=-1)
    sin_c = jnp.concatenate([zero, -sc, sc, zero[:, :32]], axis=-1)
    return cos_a, sin_a, cos_c, sin_c


def _na_bias_tables(rpb, rows):
    wh = min(NA_WIN_H, rows)
    n_dr, n_dc = 2 * NA_WIN_H - 1, 2 * NA_WIN_W - 1
    qc = np.arange(GRID_W)[:, None]
    kc = np.arange(GRID_W)[None, :]
    col_start = np.clip(qc - NA_WIN_W // 2, 0, GRID_W - NA_WIN_W)
    ok_c = (kc >= col_start) & (kc < col_start + NA_WIN_W)
    dc = np.clip(kc - qc + (NA_WIN_W - 1), 0, n_dc - 1)
    oh_c = np.eye(n_dc, dtype=np.float32)[dc.reshape(-1)]
    oh_r, ok_r = [], []
    for r0 in (0, NA_QROWS, rows - NA_QROWS):
        ws = int(np.clip(r0 - 4, 0, rows - NA_KROWS))
        qr = r0 + np.arange(NA_QROWS)[:, None]
        kr = ws + np.arange(NA_KROWS)[None, :]
        row_start = np.clip(qr - wh // 2, 0, rows - wh)
        ok_r.append((kr >= row_start) & (kr < row_start + wh))
        dr = np.clip(kr - qr + (NA_WIN_H - 1), 0, n_dr - 1)
        oh_r.append(np.eye(n_dr, dtype=np.float32)[dr.reshape(-1)])
    oh_r = np.concatenate(oh_r, axis=0)
    ok = np.stack(ok_r)[:, :, None, :, None] & ok_c[None, None, :, None, :]
    ok = ok.reshape(3, 1, NA_QROWS * GRID_W, NA_KROWS * GRID_W)
    hi = lax.Precision.HIGHEST
    t1 = jnp.einsum("ma,hab->hmb", jnp.asarray(oh_r), rpb.astype(jnp.float32) * LOG2E, precision=hi)
    t2 = jnp.einsum("hmb,nb->hmn", t1, jnp.asarray(oh_c), precision=hi).astype(jnp.bfloat16)
    nh = rpb.shape[0]
    t2 = t2.reshape(nh, 3, NA_QROWS, NA_KROWS, GRID_W, GRID_W).transpose(1, 0, 2, 4, 3, 5)
    bias = t2.reshape(3, nh, NA_QROWS * GRID_W, NA_KROWS * GRID_W)
    return jnp.where(jnp.asarray(ok), bias, jnp.asarray(NEG_BIG, jnp.bfloat16))


def _cparams(sem):
    return pltpu.CompilerParams(dimension_semantics=sem, vmem_limit_bytes=VMEM_LIMIT)


def _layer_norm_rows(z, g, b):
    mu = jnp.mean(z, axis=-1, keepdims=True)
    zc = z - mu
    var = jnp.mean(zc * zc, axis=-1, keepdims=True)
    return zc * lax.rsqrt(var + LN_EPS) * g + b


def _sigmoid(x):
    return 0.5 * jnp.tanh(0.5 * x) + 0.5


def _dot(a, b):
    return jnp.dot(a, b, preferred_element_type=jnp.float32)


def _dot_nt(a, b):
    return lax.dot_general(a, b, (((1,), (1,)), ((), ())), preferred_element_type=jnp.float32)


def _ln_kernel(x_ref, g_ref, b_ref, of_ref, ob_ref):
    y = _layer_norm_rows(x_ref[...], g_ref[...], b_ref[...])
    of_ref[...] = y
    ob_ref[...] = y.astype(jnp.bfloat16)


def _input_ln(x, g, b, tm=1024):
    t, d = x.shape
    return pl.pallas_call(
        _ln_kernel,
        out_shape=(jax.ShapeDtypeStruct((t, d), jnp.float32), jax.ShapeDtypeStruct((t, d), jnp.bfloat16)),
        grid=(t // tm,),
        in_specs=[pl.BlockSpec((tm, d), lambda i: (i, 0)),
                  pl.BlockSpec((1, d), lambda i: (0, 0)),
                  pl.BlockSpec((1, d), lambda i: (0, 0))],
        out_specs=(pl.BlockSpec((tm, d), lambda i: (i, 0)), pl.BlockSpec((tm, d), lambda i: (i, 0))),
        compiler_params=_cparams(("parallel",)),
    )(x, g.reshape(1, d), b.reshape(1, d))


def _mm_kernel(a_ref, b_ref, o_ref):
    o_ref[...] = _dot(a_ref[...], b_ref[...]).astype(o_ref.dtype)


def _matmul(a, b, out_dtype, tm, tn):
    m, k = a.shape
    n = b.shape[1]
    return pl.pallas_call(
        _mm_kernel,
        out_shape=jax.ShapeDtypeStruct((m, n), out_dtype),
        grid=(m // tm, n // tn),
        in_specs=[pl.BlockSpec((tm, k), lambda i, j: (i, 0)),
                  pl.BlockSpec((k, tn), lambda i, j: (0, j))],
        out_specs=pl.BlockSpec((tm, tn), lambda i, j: (i, j)),
        compiler_params=_cparams(("parallel", "parallel")),
    )(a, b)


def _ones_halves(vblk):
    lane = lax.broadcasted_iota(jnp.int32, vblk.shape, 1)
    ones = jnp.ones_like(vblk)
    return jnp.where(lane < HEAD_DIM, vblk, ones), jnp.where(lane < HEAD_DIM, ones, vblk)


def _attend_pair(s_lo, s_hi, v_lo, v_hi, lo_q):
    outs = []
    for s, v in ((s_lo, v_lo), (s_hi, v_hi)):
        p = jnp.exp2(s - jnp.max(s, axis=-1, keepdims=True)).astype(jnp.bfloat16)
        outs.append(_dot(p, v))
    num = jnp.where(lo_q, outs[0], outs[1])
    den = jnp.where(lo_q, pltpu.roll(outs[0], HEAD_DIM, 1), pltpu.roll(outs[1], HEAD_DIM, 1))
    return num / den


def _rotate_pairs(x, cos, sin, half, lane):
    w = x.shape[-1]
    first = (lane % (2 * half)) < half
    partner = jnp.where(first, pltpu.roll(x, w - half, 1), pltpu.roll(x, half, 1))
    return x * cos + partner * sin


def _attn_a_kernel(q_ref, kv_ref, cos_ref, sin_ref, gq_ref, gk_ref, o_ref, ks_ref, va_ref, *, tq):
    s_len = q_ref.shape[0]

    def prep(x, g, cos, sin):
        lane = lax.broadcasted_iota(jnp.int32, x.shape, 1)
        lo = lane < HEAD_DIM
        x2 = x * x
        ss_lo = jnp.sum(jnp.where(lo, x2, 0.0), axis=-1, keepdims=True)
        ss_hi = jnp.sum(jnp.where(lo, 0.0, x2), axis=-1, keepdims=True)
        ms = jnp.where(lo, ss_lo, ss_hi) * (1.0 / HEAD_DIM)
        xn = x * lax.rsqrt(ms + RMS_EPS) * g
        return _rotate_pairs(xn, cos, sin, HEAD_DIM // 2, lane)

    ks_ref[...] = prep(kv_ref[:, 0:LANES].astype(jnp.float32), gk_ref[...],
                       cos_ref[...], sin_ref[...]).astype(jnp.bfloat16)
    scale = HEAD_DIM ** -0.5 * LOG2E
    lane_q = lax.broadcasted_iota(jnp.int32, (tq, LANES), 1)
    lo_q = lane_q < HEAD_DIM

    va_ref[0], va_ref[1] = _ones_halves(kv_ref[:, LANES:2 * LANES])

    def body(i, carry):
        r = pl.multiple_of(i * tq, tq)
        k = ks_ref[...]
        cos = cos_ref[pl.ds(r, tq), :]
        sin = sin_ref[pl.ds(r, tq), :]
        for g in range(2):
            qt = prep(q_ref[pl.ds(r, tq), g * LANES:(g + 1) * LANES].astype(jnp.float32), gq_ref[...], cos, sin)
            qt = (qt * scale).astype(jnp.bfloat16)
            zero = jnp.zeros_like(qt)
            out = _attend_pair(_dot_nt(jnp.where(lo_q, qt, zero), k), _dot_nt(jnp.where(lo_q, zero, qt), k),
                               va_ref[0], va_ref[1], lo_q)
            o_ref[pl.ds(r, tq), g * LANES:(g + 1) * LANES] = out.astype(o_ref.dtype)
        return carry

    lax.fori_loop(0, s_len // tq, body, 0)


def _attn_a(h, cos_a, sin_a, gq, gk, batch, seq, tq=256):
    kern = functools.partial(_attn_a_kernel, tq=tq)
    return pl.pallas_call(
        kern,
        out_shape=jax.ShapeDtypeStruct((batch * seq, MIX_W), jnp.bfloat16),
        grid=(batch,),
        in_specs=[pl.BlockSpec((seq, 256), lambda b: (b, COL_AQ // 256)),
                  pl.BlockSpec((seq, 256), lambda b: (b, COL_AKV // 256)),
                  pl.BlockSpec((seq, LANES), lambda b: (0, 0)),
                  pl.BlockSpec((seq, LANES), lambda b: (0, 0)),
                  pl.BlockSpec((1, LANES), lambda b: (0, 0)),
                  pl.BlockSpec((1, LANES), lambda b: (0, 0))],
        out_specs=pl.BlockSpec((seq, MIX_W), lambda b: (b, 0)),
        scratch_shapes=[pltpu.VMEM((seq, LANES), jnp.bfloat16), pltpu.VMEM((2, seq, LANES), jnp.bfloat16)],
        compiler_params=_cparams(("parallel",)),
    )(h, h, cos_a, sin_a, gq, gk)


def _attn_b_kernel(q_ref, k_ref, v_ref, bias_ref, o_ref, va_ref, *, rows):
    nq = NA_QROWS * GRID_W
    nk = NA_KROWS * GRID_W
    lane_q = lax.broadcasted_iota(jnp.int32, (nq, LANES), 1)
    lo_q = lane_q < HEAD_DIM
    n_groups = rows // NA_QROWS
    scale = HEAD_DIM ** -0.5 * LOG2E
    for blk in range(B_HEADS // 2):
        va_ref[2 * blk], va_ref[2 * blk + 1] = _ones_halves(v_ref[:, blk * LANES:(blk + 1) * LANES])

    def body(gi, carry):
        ws = jnp.clip(gi * NA_QROWS - 4, 0, rows - NA_KROWS)
        cls = jnp.where(gi == 0, 0, jnp.where(gi == n_groups - 1, 2, 1))
        qr = pl.multiple_of(gi * nq, nq)
        kr = pl.multiple_of(ws * GRID_W, GRID_W)
        for blk in range(B_HEADS // 2):
            sl = slice(blk * LANES, (blk + 1) * LANES)
            qt = q_ref[pl.ds(qr, nq), sl] * scale
            kw = k_ref[pl.ds(kr, nk), sl]
            zero = jnp.zeros_like(qt)
            s_lo = _dot_nt(jnp.where(lo_q, qt, zero), kw) + bias_ref[cls, 2 * blk].astype(jnp.float32)
            s_hi = _dot_nt(jnp.where(lo_q, zero, qt), kw) + bias_ref[cls, 2 * blk + 1].astype(jnp.float32)
            out = _attend_pair(s_lo, s_hi, va_ref[2 * blk, pl.ds(kr, nk), :], va_ref[2 * blk + 1, pl.ds(kr, nk), :], lo_q)
            o_ref[pl.ds(qr, nq), sl] = out.astype(o_ref.dtype)
        return carry

    lax.fori_loop(0, n_groups, body, 0)


def _attn_b(h, bias, batch, seq):
    rows = seq // GRID_W
    kern = functools.partial(_attn_b_kernel, rows=rows)
    return pl.pallas_call(
        kern,
        out_shape=jax.ShapeDtypeStruct((batch * seq, MIX_W), jnp.bfloat16),
        grid=(batch,),
        in_specs=[pl.BlockSpec((seq, 256), lambda b: (b, COL_BQ // 256)),
                  pl.BlockSpec((seq, 256), lambda b: (b, COL_BK // 256)),
                  pl.BlockSpec((seq, 256), lambda b: (b, COL_BV // 256)),
                  pl.BlockSpec(bias.shape, lambda b: (0, 0, 0, 0))],
        out_specs=pl.BlockSpec((seq, MIX_W), lambda b: (b, 0)),
        scratch_shapes=[pltpu.VMEM((B_HEADS, seq, LANES), jnp.bfloat16)],
        compiler_params=_cparams(("parallel",)),
    )(h, h, h, bias)


def _attn_c_kernel(hc_ref, wuq_ref, wuk_ref, wuv_ref, gq_ref, gkv_ref, cos_ref, sin_ref, o_ref,
                   qs_ref, ks_ref, vs_ref, *, tq):
    s_len = hc_ref.shape[0]
    lane = lax.broadcasted_iota(jnp.int32, (s_len, LANES), 1)
    cos = cos_ref[...]
    sin = sin_ref[...]

    cq = hc_ref[:, 0:256].astype(jnp.float32)
    ms = jnp.sum(cq * cq, axis=-1, keepdims=True) * (1.0 / C_Q_RANK)
    cqn = (cq * lax.rsqrt(ms + RMS_EPS) * gq_ref[...]).astype(jnp.bfloat16)
    ckv = hc_ref[:, 256:384].astype(jnp.float32)
    ms = jnp.mean(ckv * ckv, axis=-1, keepdims=True)
    kvn = (ckv * lax.rsqrt(ms + RMS_EPS) * gkv_ref[...]).astype(jnp.bfloat16)
    kpe = _rotate_pairs(hc_ref[:, 384:512].astype(jnp.float32), cos, sin, C_ROPE // 2, lane)
    v_all = _dot(kvn, wuv_ref[...]).astype(jnp.bfloat16)
    for blk in range(C_HEADS // 2):
        vs_ref[2 * blk], vs_ref[2 * blk + 1] = _ones_halves(v_all[:, blk * LANES:(blk + 1) * LANES])
    scale = (C_NOPE + C_ROPE) ** -0.5 * LOG2E
    for h in range(C_HEADS):
        sl = slice(h * LANES, (h + 1) * LANES)
        qh = _rotate_pairs(_dot(cqn, wuq_ref[:, sl]), cos, sin, C_ROPE // 2, lane) * scale
        qs_ref[:, sl] = qh.astype(jnp.bfloat16)
        ks_ref[:, sl] = (_dot(kvn, wuk_ref[:, sl]) + kpe).astype(jnp.bfloat16)

    lane_q = lax.broadcasted_iota(jnp.int32, (tq, LANES), 1)
    lo_q = lane_q < C_V

    def body(i, carry):
        r = pl.multiple_of(i * tq, tq)
        for blk in range(C_HEADS // 2):
            s = []
            for par in range(2):
                sl = slice((2 * blk + par) * LANES, (2 * blk + par + 1) * LANES)
                s.append(_dot_nt(qs_ref[pl.ds(r, tq), sl], ks_ref[:, sl]))
            out = _attend_pair(s[0], s[1], vs_ref[2 * blk], vs_ref[2 * blk + 1], lo_q)
            o_ref[pl.ds(r, tq), blk * LANES:(blk + 1) * LANES] = out.astype(o_ref.dtype)
        return carry

    lax.fori_loop(0, s_len // tq, body, 0)


def _attn_c(h, wuq, wuk, wuv, gq, gkv, cos_c, sin_c, batch, seq, tq=256):
    kern = functools.partial(_attn_c_kernel, tq=tq)
    const = lambda shape: pl.BlockSpec(shape, lambda b: (0,) * len(shape))
    return pl.pallas_call(
        kern,
        out_shape=jax.ShapeDtypeStruct((batch * seq, MIX_W), jnp.bfloat16),
        grid=(batch,),
        in_specs=[pl.BlockSpec((seq, 512), lambda b: (b, COL_C // 512)),
                  const(wuq.shape), const(wuk.shape), const(wuv.shape),
                  const(gq.shape), const(gkv.shape), const(cos_c.shape), const(sin_c.shape)],
        out_specs=pl.BlockSpec((seq, MIX_W), lambda b: (b, 0)),
        scratch_shapes=[pltpu.VMEM((seq, C_HEADS * LANES), jnp.bfloat16),
                        pltpu.VMEM((seq, C_HEADS * LANES), jnp.bfloat16),
                        pltpu.VMEM((C_HEADS, seq, LANES), jnp.bfloat16)],
        compiler_params=_cparams(("parallel",)),
    )(h, wuq, wuk, wuv, gq, gkv, cos_c, sin_c)


CONV_PAD = 16
CONV_ROWS = 128


def _conv_kernel(hd_ref, w_ref, cb_ref, g_ref, b_ref, o_ref, up_ref):
    s_len = hd_ref.shape[0]
    a = hd_ref[:, 0:D_CH].astype(jnp.float32)
    gate = hd_ref[:, D_CH:2 * D_CH].astype(jnp.float32)
    zeros = jnp.zeros((CONV_PAD, D_CH), jnp.float32)
    up_ref[0, 0:CONV_PAD, :] = zeros
    up_ref[0, CONV_PAD + s_len:2 * CONV_PAD + s_len, :] = zeros
    up_ref[0, CONV_PAD:CONV_PAD + s_len, :] = a * _sigmoid(gate)
    n_shifted = s_len + 2 * CONV_PAD - SUBLANES
    for s in range(1, SUBLANES):
        up_ref[s, 0:n_shifted, :] = up_ref[0, s:s + n_shifted, :]
    shift = CONV_PAD - CONV_W // 2
    for c in range(s_len // CONV_ROWS):
        acc = jnp.zeros((CONV_ROWS, D_CH), jnp.float32)
        for j in range(CONV_W):
            s = (j + shift) % SUBLANES
            start = c * CONV_ROWS + j + shift - s
            acc = acc + up_ref[s, start:start + CONV_ROWS, :] * w_ref[j:j + 1, :]
        u = acc + cb_ref[...]
        y = _layer_norm_rows(u, g_ref[...], b_ref[...])
        o_ref[c * CONV_ROWS:(c + 1) * CONV_ROWS, :] = (y * _sigmoid(y)).astype(o_ref.dtype)


def _conv_branch(h, w, cb, g, b, batch, seq):
    const = lambda shape: pl.BlockSpec(shape, lambda i: (0,) * len(shape))
    return pl.pallas_call(
        _conv_kernel,
        out_shape=jax.ShapeDtypeStruct((batch * seq, MIX_W), jnp.bfloat16),
        grid=(batch,),
        in_specs=[pl.BlockSpec((seq, 512), lambda i: (i, COL_D // 512)),
                  const(w.shape), const(cb.shape), const(g.shape), const(b.shape)],
        out_specs=pl.BlockSpec((seq, MIX_W), lambda i: (i, 0)),
        scratch_shapes=[pltpu.VMEM((SUBLANES, seq + 2 * CONV_PAD, D_CH), jnp.float32)],
        compiler_params=_cparams(("parallel",)),
    )(h, w, cb, g, b)


def _merge_kernel(gl_ref, ya_ref, yb_ref, yc_ref, yd_ref, x_ref, wb_ref, wo_ref, g_ref, b_ref,
                  of_ref, ob_ref):
    merged = None
    for n, y_ref in enumerate((ya_ref, yb_ref, yc_ref, yd_ref)):
        t = jnp.tanh(gl_ref[:, n * D_MODEL:(n + 1) * D_MODEL].astype(jnp.float32))
        term = (t + 1.0) * _dot(y_ref[...], wb_ref[n])
        merged = term if merged is None else merged + term
    mix = _dot(merged.astype(jnp.bfloat16), wo_ref[...])
    y = _layer_norm_rows(ALPHA * x_ref[...] + mix, g_ref[...], b_ref[...])
    of_ref[...] = y
    ob_ref[...] = y.astype(jnp.bfloat16)


def _merge(h, ya, yb, yc, yd, x, wb, wo, g, b, tm=512):
    t, d = x.shape
    row = lambda w: pl.BlockSpec((tm, w), lambda i: (i, 0))
    const = lambda shape: pl.BlockSpec(shape, lambda i: (0,) * len(shape))
    return pl.pallas_call(
        _merge_kernel,
        out_shape=(jax.ShapeDtypeStruct((t, d), jnp.float32), jax.ShapeDtypeStruct((t, d), jnp.bfloat16)),
        grid=(t // tm,),
        in_specs=[pl.BlockSpec((tm, N_BRANCH * d), lambda i: (i, COL_GATES)),
                  row(MIX_W), row(MIX_W), row(MIX_W), row(MIX_W), row(d),
                  const(wb.shape), const(wo.shape), const((1, d)), const((1, d))],
        out_specs=(row(d), row(d)),
        compiler_params=_cparams(("parallel",)),
    )(h, ya, yb, yc, yd, x, wb, wo, g.reshape(1, d), b.reshape(1, d))


def _ffn_kernel(xb_ref, xf_ref, w13_ref, w2_ref, g_ref, b_ref, of_ref, ob_ref, *, tf):
    x = xb_ref[...]
    f = w2_ref.shape[0]
    acc = None
    for c in range(f // tf):
        ab = _dot(x, w13_ref[:, 2 * c * tf:2 * (c + 1) * tf])
        a = ab[:, :tf]
        hid = (a * _sigmoid(a) * ab[:, tf:]).astype(jnp.bfloat16)
        part = _dot(hid, w2_ref[c * tf:(c + 1) * tf, :])
        acc = part if acc is None else acc + part
    y = _layer_norm_rows(ALPHA * xf_ref[...] + acc, g_ref[...], b_ref[...])
    of_ref[...] = y
    ob_ref[...] = y.astype(jnp.bfloat16)


def _ffn(xb, xf, w1, w3, w2, g, b, tm=512, tf=1408):
    t, d = xf.shape
    f = w1.shape[1]
    w13 = jnp.concatenate([w[:, j * tf:(j + 1) * tf] for j in range(f // tf) for w in (w1, w3)], axis=1)
    row = pl.BlockSpec((tm, d), lambda i: (i, 0))
    vec = pl.BlockSpec((1, d), lambda i: (0, 0))
    resident = lambda shape: pl.BlockSpec(shape, lambda i: (0, 0), pipeline_mode=pl.Buffered(1))
    return pl.pallas_call(
        functools.partial(_ffn_kernel, tf=tf),
        out_shape=(jax.ShapeDtypeStruct((t, d), jnp.float32), jax.ShapeDtypeStruct((t, d), jnp.bfloat16)),
        grid=(t // tm,),
        in_specs=[row, row, resident(w13.shape), resident(w2.shape), vec, vec],
        out_specs=(row, row),
        compiler_params=_cparams(("parallel",)),
    )(xb, xf, w13, w2, g.reshape(1, d), b.reshape(1, d))


def _pack_rows(x):
    half = x.shape[1] // 2
    bits = lax.bitcast_convert_type(x.astype(jnp.bfloat16).astype(jnp.float32), jnp.uint32)
    return (bits[:, :half] >> 16) | (bits[:, half:] & jnp.uint32(0xFFFF0000))


def _unpack_rows(p):
    lo = lax.bitcast_convert_type(p << 16, jnp.float32)
    hi = lax.bitcast_convert_type(p & jnp.uint32(0xFFFF0000), jnp.float32)
    return jnp.concatenate([lo, hi], axis=1)


def _router_kernel(x_ref, r_ref, tri_ref, oi_ref, ow_ref, cnt_ref, xp_ref, carry_ref):
    i = pl.program_id(0)

    @pl.when(i == 0)
    def _():
        carry_ref[...] = jnp.zeros_like(carry_ref)

    x = x_ref[...]
    r = r_ref[...]
    xh = x.astype(jnp.bfloat16)
    xl = (x - xh.astype(jnp.float32)).astype(jnp.bfloat16)
    rh = r.astype(jnp.bfloat16)
    rl = (r - rh.astype(jnp.float32)).astype(jnp.bfloat16)
    r2 = jnp.concatenate([rh, rl], axis=1)
    p = _dot(xh, r2) + _dot(xl, r2)
    logits = p[:, :LANES] + p[:, LANES:]
    tm = x.shape[0]
    lane = lax.broadcasted_iota(jnp.int32, (tm, LANES), 1)
    logits = jnp.where(lane < N_EXPERTS, logits, NEG_BIG)
    m1 = jnp.max(logits, axis=-1, keepdims=True)
    i1 = jnp.min(jnp.where(logits == m1, lane, LANES), axis=-1, keepdims=True)
    rest = jnp.where(lane == i1, NEG_BIG, logits)
    m2 = jnp.max(rest, axis=-1, keepdims=True)
    i2 = jnp.min(jnp.where(rest == m2, lane, LANES), axis=-1, keepdims=True)
    e = jnp.exp(m2 - m1)
    w1 = 1.0 / (1.0 + e)
    w2 = e / (1.0 + e)
    sel1 = lane == i1
    sel2 = lane == i2
    member = jnp.where(sel1 | sel2, 1.0, 0.0)
    before = _dot(tri_ref[...], member.astype(jnp.bfloat16)) + carry_ref[...]
    rank1 = jnp.sum(jnp.where(sel1, before, 0.0), axis=-1, keepdims=True).astype(jnp.int32)
    rank2 = jnp.sum(jnp.where(sel2, before, 0.0), axis=-1, keepdims=True).astype(jnp.int32)
    carry_ref[...] += jnp.sum(member, axis=0, keepdims=True)
    oi = jnp.where(lane == 0, i1, jnp.where(lane == 1, i2, jnp.where(lane == 2, rank1, rank2)))
    oi_ref[...] = oi.T[0:SUBLANES, :]
    ow_ref[...] = jnp.where(lane == 0, w1, w2)
    cnt_ref[...] = carry_ref[...]
    xp_ref[...] = _pack_rows(x)


def _router(xf, router, tm=512):
    t, d = xf.shape
    r_pad = jnp.zeros((d, LANES), jnp.float32).at[:, :N_EXPERTS].set(router.astype(jnp.float32))
    tri = jnp.asarray(np.tril(np.ones((tm, tm), np.float32), -1), jnp.bfloat16)
    return pl.pallas_call(
        _router_kernel,
        out_shape=(jax.ShapeDtypeStruct((SUBLANES, t), jnp.int32),
                   jax.ShapeDtypeStruct((t, LANES), jnp.float32),
                   jax.ShapeDtypeStruct((1, LANES), jnp.float32),
                   jax.ShapeDtypeStruct((t, d // 2), jnp.uint32)),
        grid=(t // tm,),
        in_specs=[pl.BlockSpec((tm, d), lambda i: (i, 0)),
                  pl.BlockSpec((d, LANES), lambda i: (0, 0)),
                  pl.BlockSpec((tm, tm), lambda i: (0, 0))],
        out_specs=(pl.BlockSpec((SUBLANES, tm), lambda i: (0, i)),
                   pl.BlockSpec((tm, LANES), lambda i: (i, 0)),
                   pl.BlockSpec((1, LANES), lambda i: (0, 0)),
                   pl.BlockSpec((tm, d // 2), lambda i: (i, 0))),
        scratch_shapes=[pltpu.VMEM((1, LANES), jnp.float32)],
        compiler_params=_cparams(("arbitrary",)),
    )(xf, r_pad, tri)


SC_CORES = 2
SC_SUBCORES = 16
SC_WINDOW = 64


def _sc_gather_rows(table, idx):
    m = idx.shape[0]
    d = table.shape[1]
    n_workers = SC_CORES * SC_SUBCORES
    per_w = m // n_workers
    n_win = per_w // SC_WINDOW
    assert per_w * n_workers == m and n_win * SC_WINDOW == per_w
    mesh = plsc.VectorSubcoreMesh(core_axis_name="c", subcore_axis_name="s")

    @functools.partial(
        pl.kernel, mesh=mesh,
        out_type=jax.ShapeDtypeStruct((m, d), table.dtype),
        scratch_types=[pltpu.VMEM((per_w,), jnp.int32),
                       pltpu.VMEM((SC_WINDOW, d), table.dtype),
                       pltpu.SemaphoreType.DMA],
    )
    def gather(table_hbm, idx_hbm, out_hbm, idx_v, rows_v, sem):
        wid = lax.axis_index("s") * SC_CORES + lax.axis_index("c")
        base = wid * per_w
        pltpu.sync_copy(idx_hbm.at[pl.ds(base, per_w)], idx_v)

        @pl.loop(0, n_win)
        def _(w):
            off = pl.multiple_of(w * SC_WINDOW, SC_WINDOW)
            pltpu.async_copy(table_hbm.at[idx_v.at[pl.ds(off, SC_WINDOW)]], rows_v, sem).wait()
            pltpu.sync_copy(rows_v, out_hbm.at[pl.ds(base + off, SC_WINDOW)])

    return gather(table, idx)


def _sc_scatter_rows(rows, idx, n_out):
    n_copies, m = idx.shape
    d = rows.shape[1]
    n_workers = SC_CORES * SC_SUBCORES
    per_w = m // n_workers
    n_win = per_w // SC_WINDOW
    assert per_w * n_workers == m and n_win * SC_WINDOW == per_w
    idx_w = idx.reshape(n_copies, n_workers, n_win, SC_WINDOW).transpose(1, 0, 2, 3)
    idx_w = idx_w.reshape(n_workers, n_copies * n_win, SC_WINDOW)
    mesh = plsc.VectorSubcoreMesh(core_axis_name="c", subcore_axis_name="s")

    @functools.partial(
        pl.kernel, mesh=mesh,
        out_type=jax.ShapeDtypeStruct((n_out, d), rows.dtype),
        scratch_types=[pltpu.VMEM((n_copies * n_win, SC_WINDOW), jnp.int32),
                       pltpu.VMEM((SC_WINDOW, d), rows.dtype)],
    )
    def scatter(rows_hbm, idx_hbm, out_hbm, idx_v, rows_v):
        wid = lax.axis_index("s") * SC_CORES + lax.axis_index("c")
        base = wid * per_w
        pltpu.sync_copy(idx_hbm.at[wid], idx_v)

        @pl.loop(0, n_win)
        def _(w):
            off = pl.multiple_of(w * SC_WINDOW, SC_WINDOW)
            pltpu.sync_copy(rows_hbm.at[pl.ds(base + off, SC_WINDOW)], rows_v)
            for k in range(n_copies):
                pltpu.sync_copy(rows_v, out_hbm.at[idx_v.at[k * n_win + w]])

    return scatter(rows, idx_w)


def _expert_kernel(te_ref, nv_ref, x_ref, w1_ref, w3_ref, w2_ref, o_ref,
                   w13s_ref, w2s_ref, xb_ref, acc_ref):
    i = pl.program_id(0)
    j = pl.program_id(1)
    last = pl.num_programs(1) - 1
    n_valid = nv_ref[i]
    n_piece = MOE_TILE // MOE_SUB
    piece_rows = [pl.ds(s * MOE_SUB, MOE_SUB) for s in range(n_piece)]

    for s, rows in enumerate(piece_rows):
        @pl.when((n_valid > s * MOE_SUB) & (j == 0))
        def _():
            row = lax.broadcasted_iota(jnp.int32, (MOE_SUB, 1), 0) + s * MOE_SUB
            x = jnp.where(row < n_valid, _unpack_rows(x_ref[rows, :]), 0.0)
            xb_ref[rows, :] = x.astype(jnp.bfloat16)
            acc_ref[rows, :] = jnp.zeros((MOE_SUB, acc_ref.shape[1]), jnp.float32)

    def compute(n_live):
        w13s_ref[:, 0:MOE_FCHUNK] = w1_ref[0].astype(jnp.bfloat16)
        w13s_ref[:, MOE_FCHUNK:2 * MOE_FCHUNK] = w3_ref[0].astype(jnp.bfloat16)
        w2s_ref[...] = w2_ref[0].astype(jnp.bfloat16)
        for rows in piece_rows[:n_live]:
            ab = _dot(xb_ref[rows, :], w13s_ref[...])
            a = ab[:, :MOE_FCHUNK]
            hid = (a * _sigmoid(a) * ab[:, MOE_FCHUNK:]).astype(jnp.bfloat16)
            acc_ref[rows, :] += _dot(hid, w2s_ref[...])

    for n_live in range(1, n_piece + 1):
        upper = n_live * MOE_SUB if n_live < n_piece else MOE_TILE
        pl.when((n_valid > (n_live - 1) * MOE_SUB) & (n_valid <= upper))(functools.partial(compute, n_live))

    for s, rows in enumerate(piece_rows):
        @pl.when((n_valid > s * MOE_SUB) & (j == last))
        def _():
            o_ref[rows, :] = _pack_rows(acc_ref[rows, :])

        @pl.when((n_valid <= s * MOE_SUB) & (j == last))
        def _():
            o_ref[rows, :] = jnp.zeros((MOE_SUB, o_ref.shape[1]), o_ref.dtype)


def _experts(tile_expert, tile_valid, xs, w1, w3, w2):
    n_rows = xs.shape[0]
    d, f = w1.shape[1], w1.shape[2]
    nj = f // MOE_FCHUNK

    def chunk(i, j, nv):
        return jnp.where(nv[i] > 0, j, nj - 1)

    return pl.pallas_call(
        _expert_kernel,
        out_shape=jax.ShapeDtypeStruct((n_rows, d // 2), jnp.uint32),
        grid_spec=pltpu.PrefetchScalarGridSpec(
            num_scalar_prefetch=2,
            grid=(n_rows // MOE_TILE, nj),
            in_specs=[pl.BlockSpec((MOE_TILE, d // 2), lambda i, j, te, nv: (i, 0)),
                      pl.BlockSpec((1, d, MOE_FCHUNK), lambda i, j, te, nv: (te[i], 0, chunk(i, j, nv))),
                      pl.BlockSpec((1, d, MOE_FCHUNK), lambda i, j, te, nv: (te[i], 0, chunk(i, j, nv))),
                      pl.BlockSpec((1, MOE_FCHUNK, d), lambda i, j, te, nv: (te[i], chunk(i, j, nv), 0))],
            out_specs=pl.BlockSpec((MOE_TILE, d // 2), lambda i, j, te, nv: (i, 0)),
            scratch_shapes=[pltpu.VMEM((d, 2 * MOE_FCHUNK), jnp.bfloat16),
                            pltpu.VMEM((MOE_FCHUNK, d), jnp.bfloat16),
                            pltpu.VMEM((MOE_TILE, d), jnp.bfloat16),
                            pltpu.VMEM((MOE_TILE, d), jnp.float32)],
        ),
        compiler_params=pltpu.CompilerParams(dimension_semantics=("arbitrary", "arbitrary"),
                                             vmem_limit_bytes=MOE_VMEM_LIMIT),
    )(tile_expert, tile_valid, xs, w1, w3, w2)


def _combine_kernel(ya_ref, yb_ref, x_ref, w_ref, g_ref, b_ref, o_ref):
    w = w_ref[...]
    f = w[:, 0:1] * _unpack_rows(ya_ref[...]) + w[:, 1:2] * _unpack_rows(yb_ref[...])
    o_ref[...] = _layer_norm_rows(ALPHA * x_ref[...] + f, g_ref[...], b_ref[...])


def _combine(yg, xf, wts, g, b, tm=512):
    t, d = xf.shape
    nt = t // tm
    return pl.pallas_call(
        _combine_kernel,
        out_shape=jax.ShapeDtypeStruct((t, d), jnp.float32),
        grid=(nt,),
        in_specs=[pl.BlockSpec((tm, d // 2), lambda i: (i, 0)),
                  pl.BlockSpec((tm, d // 2), lambda i: (i + nt, 0)),
                  pl.BlockSpec((tm, d), lambda i: (i, 0)),
                  pl.BlockSpec((tm, LANES), lambda i: (i, 0)),
                  pl.BlockSpec((1, d), lambda i: (0, 0)),
                  pl.BlockSpec((1, d), lambda i: (0, 0))],
        out_specs=pl.BlockSpec((tm, d), lambda i: (i, 0)),
        compiler_params=_cparams(("parallel",)),
    )(yg, yg, xf, wts, g.reshape(1, d), b.reshape(1, d))


def _moe_ffn(xf, router, w1, w3, w2, g, b):
    t, d = xf.shape
    oi, ow, cnt, xp = _router(xf, router)
    e1, e2, rank1, rank2 = oi[0], oi[1], oi[2], oi[3]
    counts = cnt[0, :N_EXPERTS].astype(jnp.int32)
    tiles = (counts + MOE_TILE - 1) // MOE_TILE
    tile_end = jnp.cumsum(tiles)
    tile_start = tile_end - tiles
    offs = tile_start * MOE_TILE
    dest = jnp.concatenate([offs[e1] + rank1, offs[e2] + rank2])
    n_rows = TOP_K * t + N_EXPERTS * MOE_TILE
    n_tiles = n_rows // MOE_TILE
    tile_id = jnp.arange(n_tiles, dtype=jnp.int32)
    tile_expert = jnp.sum(tile_id[:, None] >= tile_end[None, :], axis=1)
    used = tile_expert < N_EXPERTS
    last_expert = jnp.sum(tile_end[-1] - 1 >= tile_end)
    tile_expert = jnp.where(used, tile_expert, last_expert).astype(jnp.int32)
    tile_valid = jnp.clip(counts[tile_expert] - (tile_id - tile_start[tile_expert]) * MOE_TILE, 0, MOE_TILE)
    tile_valid = jnp.where(used, tile_valid, 0).astype(jnp.int32)
    xs = _sc_scatter_rows(xp, dest.reshape(TOP_K, t), n_rows)
    y = _experts(tile_expert, tile_valid, xs, w1, w3, w2)
    yg = _sc_gather_rows(y, dest)
    return _combine(yg, xf, ow, g, b)


def _prep_layer(l, p, in_cols):
    bf = jnp.bfloat16
    out = {}
    n_gate = N_BRANCH * D_MODEL
    w_l = p["w_in"][l]
    gate_cols = (0.5 * w_l[:, int(in_cols[0]):int(in_cols[0]) + n_gate]).astype(bf)
    out["w_in"] = jnp.concatenate([gate_cols, _take_cols(w_l, in_cols[n_gate:]).astype(bf)], axis=1)
    deint = np.concatenate([np.arange(0, HEAD_DIM, 2), np.arange(1, HEAD_DIM, 2)])
    out["gq_a"] = jnp.tile(p["a_q_norm"][l][deint], 2).reshape(1, LANES)
    out["gk_a"] = jnp.tile(p["a_k_norm"][l][deint], 2).reshape(1, LANES)
    per_head = C_NOPE + C_ROPE
    uq_cols = []
    for h in range(C_HEADS):
        base = h * per_head
        uq_cols += [base + np.arange(C_NOPE), base + C_NOPE + np.arange(0, C_ROPE, 2),
                    base + C_NOPE + np.arange(1, C_ROPE, 2), np.full(32, -1)]
    wuq = _take_cols(p["c_w_uq"][l], np.concatenate(uq_cols))
    out["wuq"] = jnp.concatenate([wuq, jnp.zeros((64, wuq.shape[1]), wuq.dtype)], axis=0).astype(bf)
    uk_cols, uv_cols = [], []
    for h in range(C_HEADS):
        base = h * (C_NOPE + C_V)
        uk_cols += [base + np.arange(C_NOPE), np.full(64, -1)]
        uv_cols += [base + C_NOPE + np.arange(C_V)]
    out["wuk"] = _take_cols(p["c_w_ukv"][l], np.concatenate(uk_cols)).astype(bf)
    out["wuv"] = _take_cols(p["c_w_ukv"][l], np.concatenate(uv_cols)).astype(bf)
    out["gq_c"] = jnp.concatenate([p["c_q_norm"][l], jnp.zeros((64,), jnp.float32)]).reshape(1, 256)
    out["gkv_c"] = p["c_kv_norm"][l].reshape(1, C_KV_RANK)
    out["conv_w"] = jnp.concatenate([p["d_conv_w"][l][:, 0, :], jnp.zeros((1, D_CH), jnp.float32)], axis=0)
    out["conv_b"] = p["d_conv_b"][l].reshape(1, D_CH)
    out["d_ln_g"] = p["d_ln_g"][l].reshape(1, D_CH)
    out["d_ln_b"] = p["d_ln_b"][l].reshape(1, D_CH)
    a_rows = np.concatenate([(kvh * 2 + g) * HEAD_DIM + np.arange(HEAD_DIM)
                             for g in range(2) for kvh in range(A_KV_HEADS)])
    wb = p["w_branch"][l]
    out["w_branch"] = (0.5 * jnp.stack([wb[0][a_rows], wb[1], wb[2], wb[3]], axis=0)).astype(bf)
    out["w_out"] = p["w_out"][l].astype(bf)
    return out


def kernel(x, ln_in_g, ln_in_b, w_in, a_q_norm, a_k_norm, b_rpb, c_q_norm, c_kv_norm, c_w_uq, c_w_ukv,
           d_conv_w, d_conv_b, d_ln_g, d_ln_b, w_branch, w_out, ln_mix_g, ln_mix_b,
           ffn_w1, ffn_w3, ffn_w2, moe_router, moe_w1, moe_w3, moe_w2, ln_ffn_g, ln_ffn_b):
    batch, seq, d = x.shape
    t = batch * seq
    bf = jnp.bfloat16
    params = dict(w_in=w_in, a_q_norm=a_q_norm, a_k_norm=a_k_norm, c_q_norm=c_q_norm, c_kv_norm=c_kv_norm,
                  c_w_uq=c_w_uq, c_w_ukv=c_w_ukv, d_conv_w=d_conv_w, d_conv_b=d_conv_b, d_ln_g=d_ln_g,
                  d_ln_b=d_ln_b, w_branch=w_branch, w_out=w_out)
    in_cols = _in_proj_columns()
    cos_a, sin_a, cos_c, sin_c = _rope_tables(seq)
    xf, xb = _input_ln(x.reshape(t, d), ln_in_g, ln_in_b)
    for l in range(DEPTH):
        p = _prep_layer(l, params, in_cols)
        h = _matmul(xb, p["w_in"], bf, tm=1024, tn=3200)
        ya = _attn_a(h, cos_a, sin_a, p["gq_a"], p["gk_a"], batch, seq)
        yb = _attn_b(h, _na_bias_tables(b_rpb[l], seq // GRID_W), batch, seq)
        yc = _attn_c(h, p["wuq"], p["wuk"], p["wuv"], p["gq_c"], p["gkv_c"], cos_c, sin_c, batch, seq)
        yd = _conv_branch(h, p["conv_w"], p["conv_b"], p["d_ln_g"], p["d_ln_b"], batch, seq)
        xf, xb = _merge(h, ya, yb, yc, yd, xf, p["w_branch"], p["w_out"], ln_mix_g[l], ln_mix_b[l])
        if l % 2 == 0:
            i = l // 2
            xf, xb = _ffn(xb, xf, ffn_w1[i].astype(bf), ffn_w3[i].astype(bf), ffn_w2[i].astype(bf),
                          ln_ffn_g[l], ln_ffn_b[l])
        else:
            i = l // 2
            xf = _moe_ffn(xf, moe_router[i], moe_w1[i], moe_w3[i], moe_w2[i], ln_ffn_g[l], ln_ffn_b[l])
            xb = xf.astype(bf)
    return xf.reshape(batch, seq, d)
```

```python
import functools

import numpy as np
import jax
import jax.numpy as jnp
from jax import lax
from jax.experimental import pallas as pl
from jax.experimental.pallas import tpu as pltpu
from jax.experimental.pallas import tpu_sc as plsc

D_MODEL = 1024
DEPTH = 2
GRID_W = 64
HEAD_DIM = 64
ROPE_THETA = 10000.0
RMS_EPS = 1e-6
LN_EPS = 1e-5
A_HEADS = 4
A_KV_HEADS = 2
B_HEADS = 4
NA_WIN_H = 8
NA_WIN_W = 16
C_HEADS = 4
C_NOPE = 64
C_ROPE = 32
C_V = 64
C_Q_RANK = 192
C_KV_RANK = 128
D_CH = 256
CONV_W = 31
N_BRANCH = 4
MIX_W = 256
D_FF = 2816
N_EXPERTS = 8
TOP_K = 2
D_FF_EXPERT = 3584
ALPHA = (2 * DEPTH) ** 0.25

LANES = 128
SUBLANES = 8
VMEM_LIMIT = 48 * 1024 * 1024

LOG2E = 1.4426950408889634
NEG_BIG = -1e30

COL_GATES = 0
COL_AQ = 4096
COL_AKV = 4352
COL_C = 4608
COL_D = 5120
COL_BQ = 5632
COL_BK = 5888
COL_BV = 6144
D_IN_PAD = 6400

NA_QROWS = 4
NA_KROWS = 12

MOE_TILE = 2048
MOE_SUB = 512
MOE_FCHUNK = 512
MOE_VMEM_LIMIT = 56 * 1024 * 1024


def _in_proj_columns():
    o_aq, o_ak, o_av = 0, 256, 384
    o_bq, o_bk, o_bv = 512, 768, 1024
    o_cq, o_ckv, o_kpe = 1280, 1472, 1600
    o_d, o_g = 1632, 2144
    deint = np.concatenate([np.arange(0, HEAD_DIM, 2), np.arange(1, HEAD_DIM, 2)])
    cols = [o_g + np.arange(N_BRANCH * D_MODEL)]
    for g in range(2):
        for kvh in range(A_KV_HEADS):
            cols.append(o_aq + (kvh * 2 + g) * HEAD_DIM + deint)
    for kvh in range(A_KV_HEADS):
        cols.append(o_ak + kvh * HEAD_DIM + deint)
    cols.append(o_av + np.arange(A_KV_HEADS * HEAD_DIM))
    cols.append(o_cq + np.arange(C_Q_RANK))
    cols.append(np.full(64, -1))
    cols.append(o_ckv + np.arange(C_KV_RANK))
    cols.append(np.full(64, -1))
    cols.append(o_kpe + np.arange(0, C_ROPE, 2))
    cols.append(o_kpe + np.arange(1, C_ROPE, 2))
    cols.append(np.full(32, -1))
    cols.append(o_d + np.arange(2 * D_CH))
    cols.append(o_bq + np.arange(3 * B_HEADS * HEAD_DIM))
    cols = np.concatenate(cols)
    assert cols.shape[0] == D_IN_PAD
    return cols


def _take_cols(w, cols):
    cols = np.asarray(cols)
    parts, i, n = [], 0, len(cols)
    while i < n:
        j = i + 1
        if cols[i] < 0:
            while j < n and cols[j] < 0:
                j += 1
            parts.append(jnp.zeros((w.shape[0], j - i), w.dtype))
        else:
            step = int(cols[j] - cols[i]) if j < n and cols[j] > cols[i] else 1
            while j < n and cols[j] == cols[j - 1] + step:
                j += 1
            c0, n_run = int(cols[i]), j - i
            if step == 1:
                parts.append(w[:, c0:c0 + n_run])
            else:
                assert step == 2
                base = c0 - c0 % 2
                parts.append(w[:, base:base + 2 * n_run].reshape(w.shape[0], n_run, 2)[:, :, c0 % 2])
        i = j
    return jnp.concatenate(parts, axis=1)


def _rope_tables(seq):
    t = np.arange(seq)
    row = (t // GRID_W).astype(np.float32)
    col = (t % GRID_W).astype(np.float32)

    def angles(rot_dim):
        n_freq = rot_dim // 4
        inv = jnp.asarray(ROPE_THETA, jnp.float32) ** (-jnp.arange(n_freq, dtype=jnp.float32) / n_freq)
        return jnp.concatenate([jnp.asarray(row)[:, None] * inv, jnp.asarray(col)[:, None] * inv], axis=-1)

    ang_a = angles(HEAD_DIM)
    ca, sa = jnp.cos(ang_a), jnp.sin(ang_a)
    cos_a = jnp.concatenate([ca, ca, ca, ca], axis=-1)
    sin_a = jnp.concatenate([-sa, sa, -sa, sa], axis=-1)
    ang_c = angles(C_ROPE)
    cc, sc = jnp.cos(ang_c), jnp.sin(ang_c)
    one = jnp.ones((seq, 64), jnp.float32)
    zero = jnp.zeros((seq, 64), jnp.float32)
    cos_c = jnp.concatenate([one, cc, cc, one[:, :32]], axis=-1)
    sin_c = jnp.concatenate([zero, -sc, sc, zero[:, :32]], axis=-1)
    return cos_a, sin_a, cos_c, sin_c


def _na_bias_tables(rpb, rows):
    wh = min(NA_WIN_H, rows)
    n_dr, n_dc = 2 * NA_WIN_H - 1, 2 * NA_WIN_W - 1
    qc = np.arange(GRID_W)[:, None]
    kc = np.arange(GRID_W)[None, :]
    col_start = np.clip(qc - NA_WIN_W // 2, 0, GRID_W - NA_WIN_W)
    ok_c = (kc >= col_start) & (kc < col_start + NA_WIN_W)
    dc = np.clip(kc - qc + (NA_WIN_W - 1), 0, n_dc - 1)
    oh_c = np.eye(n_dc, dtype=np.float32)[dc.reshape(-1)]
    oh_r, ok_r = [], []
    for r0 in (0, NA_QROWS, rows - NA_QROWS):
        ws = int(np.clip(r0 - 4, 0, rows - NA_KROWS))
        qr = r0 + np.arange(NA_QROWS)[:, None]
        kr = ws + np.arange(NA_KROWS)[None, :]
        row_start = np.clip(qr - wh // 2, 0, rows - wh)
        ok_r.append((kr >= row_start) & (kr < row_start + wh))
        dr = np.clip(kr - qr + (NA_WIN_H - 1), 0, n_dr - 1)
        oh_r.append(np.eye(n_dr, dtype=np.float32)[dr.reshape(-1)])
    oh_r = np.concatenate(oh_r, axis=0)
    ok = np.stack(ok_r)[:, :, None, :, None] & ok_c[None, None, :, None, :]
    ok = ok.reshape(3, 1, NA_QROWS * GRID_W, NA_KROWS * GRID_W)
    hi = lax.Precision.HIGHEST
    t1 = jnp.einsum("ma,hab->hmb", jnp.asarray(oh_r), rpb.astype(jnp.float32) * LOG2E, precision=hi)
    t2 = jnp.einsum("hmb,nb->hmn", t1, jnp.asarray(oh_c), precision=hi).astype(jnp.bfloat16)
    nh = rpb.shape[0]
    t2 = t2.reshape(nh, 3, NA_QROWS, NA_KROWS, GRID_W, GRID_W).transpose(1, 0, 2, 4, 3, 5)
    bias = t2.reshape(3, nh, NA_QROWS * GRID_W, NA_KROWS * GRID_W)
    return jnp.where(jnp.asarray(ok), bias, jnp.asarray(NEG_BIG, jnp.bfloat16))


def _cparams(sem):
    return pltpu.CompilerParams(dimension_semantics=sem, vmem_limit_bytes=VMEM_LIMIT)


def _layer_norm_rows(z, g, b):
    mu = jnp.mean(z, axis=-1, keepdims=True)
    zc = z - mu
    var = jnp.mean(zc * zc, axis=-1, keepdims=True)
    return zc * lax.rsqrt(var + LN_EPS) * g + b


def _sigmoid(x):
    return 0.5 * jnp.tanh(0.5 * x) + 0.5


def _dot(a, b):
    return jnp.dot(a, b, preferred_element_type=jnp.float32)


def _dot_nt(a, b):
    return lax.dot_general(a, b, (((1,), (1,)), ((), ())), preferred_element_type=jnp.float32)


IN_PROJ_CHUNK = 1280


def _project_chunks(xb, w_ref, h_ref):
    for c in range(w_ref.shape[1] // IN_PROJ_CHUNK):
        cols = slice(c * IN_PROJ_CHUNK, (c + 1) * IN_PROJ_CHUNK)
        h_ref[:, cols] = _dot(xb, w_ref[:, cols]).astype(h_ref.dtype)


def _in_proj_kernel(x_ref, w_ref, h_ref):
    _project_chunks(x_ref[...], w_ref, h_ref)


def _in_proj_ln_kernel(x_ref, g_ref, b_ref, w_ref, h_ref, xf_ref):
    y = _layer_norm_rows(x_ref[...], g_ref[...], b_ref[...])
    xf_ref[...] = y
    _project_chunks(y.astype(jnp.bfloat16), w_ref, h_ref)


def _in_proj(x, w, ln=None, tm=512):
    t, d = x.shape
    n = w.shape[1]
    row = lambda width: pl.BlockSpec((tm, width), lambda i: (i, 0))
    vec = pl.BlockSpec((1, d), lambda i: (0, 0))
    w_spec = pl.BlockSpec((d, n), lambda i: (0, 0), pipeline_mode=pl.Buffered(1))
    h_shape = jax.ShapeDtypeStruct((t, n), jnp.bfloat16)
    if ln is None:
        return pl.pallas_call(
            _in_proj_kernel, out_shape=h_shape, grid=(t // tm,),
            in_specs=[row(d), w_spec], out_specs=row(n),
            compiler_params=_cparams(("parallel",)),
        )(x, w)
    g, b = ln
    return pl.pallas_call(
        _in_proj_ln_kernel,
        out_shape=(h_shape, jax.ShapeDtypeStruct((t, d), jnp.float32)),
        grid=(t // tm,),
        in_specs=[row(d), vec, vec, w_spec], out_specs=(row(n), row(d)),
        compiler_params=_cparams(("parallel",)),
    )(x, g.reshape(1, d), b.reshape(1, d), w)


def _ones_halves(vblk):
    lane = lax.broadcasted_iota(jnp.int32, vblk.shape, 1)
    ones = jnp.ones_like(vblk)
    return jnp.where(lane < HEAD_DIM, vblk, ones), jnp.where(lane < HEAD_DIM, ones, vblk)


def _attend_pair(s_lo, s_hi, v_lo, v_hi, lo_q):
    outs = []
    for s, v in ((s_lo, v_lo), (s_hi, v_hi)):
        p = jnp.exp2(s - jnp.max(s, axis=-1, keepdims=True)).astype(jnp.bfloat16)
        outs.append(_dot(p, v))
    num = jnp.where(lo_q, outs[0], outs[1])
    den = jnp.where(lo_q, pltpu.roll(outs[0], HEAD_DIM, 1), pltpu.roll(outs[1], HEAD_DIM, 1))
    return num / den


def _rotate_pairs(x, cos, sin, half, lane):
    w = x.shape[-1]
    first = (lane % (2 * half)) < half
    partner = jnp.where(first, pltpu.roll(x, w - half, 1), pltpu.roll(x, half, 1))
    return x * cos + partner * sin


def _attn_a_kernel(q_ref, kv_ref, cos_ref, sin_ref, gq_ref, gk_ref, o_ref, ks_ref, va_ref, *, tq):
    s_len = q_ref.shape[0]

    def prep(x, g, cos, sin):
        lane = lax.broadcasted_iota(jnp.int32, x.shape, 1)
        lo = lane < HEAD_DIM
        x2 = x * x
        ss_lo = jnp.sum(jnp.where(lo, x2, 0.0), axis=-1, keepdims=True)
        ss_hi = jnp.sum(jnp.where(lo, 0.0, x2), axis=-1, keepdims=True)
        ms = jnp.where(lo, ss_lo, ss_hi) * (1.0 / HEAD_DIM)
        xn = x * lax.rsqrt(ms + RMS_EPS) * g
        return _rotate_pairs(xn, cos, sin, HEAD_DIM // 2, lane)

    ks_ref[...] = prep(kv_ref[:, 0:LANES].astype(jnp.float32), gk_ref[...],
                       cos_ref[...], sin_ref[...]).astype(jnp.bfloat16)
    scale = HEAD_DIM ** -0.5 * LOG2E
    lane_q = lax.broadcasted_iota(jnp.int32, (tq, LANES), 1)
    lo_q = lane_q < HEAD_DIM

    va_ref[0], va_ref[1] = _ones_halves(kv_ref[:, LANES:2 * LANES])

    def body(i, carry):
        r = pl.multiple_of(i * tq, tq)
        k = ks_ref[...]
        cos = cos_ref[pl.ds(r, tq), :]
        sin = sin_ref[pl.ds(r, tq), :]
        for g in range(2):
            qt = prep(q_ref[pl.ds(r, tq), g * LANES:(g + 1) * LANES].astype(jnp.float32), gq_ref[...], cos, sin)
            qt = (qt * scale).astype(jnp.bfloat16)
            zero = jnp.zeros_like(qt)
            out = _attend_pair(_dot_nt(jnp.where(lo_q, qt, zero), k), _dot_nt(jnp.where(lo_q, zero, qt), k),
                               va_ref[0], va_ref[1], lo_q)
            o_ref[pl.ds(r, tq), g * LANES:(g + 1) * LANES] = out.astype(o_ref.dtype)
        return carry

    lax.fori_loop(0, s_len // tq, body, 0, unroll=4)


def _attn_a(h, cos_a, sin_a, gq, gk, batch, seq, tq=256):
    kern = functools.partial(_attn_a_kernel, tq=tq)
    return pl.pallas_call(
        kern,
        out_shape=jax.ShapeDtypeStruct((batch * seq, MIX_W), jnp.bfloat16),
        grid=(batch,),
        in_specs=[pl.BlockSpec((seq, 256), lambda b: (b, COL_AQ // 256)),
                  pl.BlockSpec((seq, 256), lambda b: (b, COL_AKV // 256)),
                  pl.BlockSpec((seq, LANES), lambda b: (0, 0)),
                  pl.BlockSpec((seq, LANES), lambda b: (0, 0)),
                  pl.BlockSpec((1, LANES), lambda b: (0, 0)),
                  pl.BlockSpec((1, LANES), lambda b: (0, 0))],
        out_specs=pl.BlockSpec((seq, MIX_W), lambda b: (b, 0)),
        scratch_shapes=[pltpu.VMEM((seq, LANES), jnp.bfloat16), pltpu.VMEM((2, seq, LANES), jnp.bfloat16)],
        compiler_params=_cparams(("parallel",)),
    )(h, h, cos_a, sin_a, gq, gk)


def _attn_b_kernel(q_ref, k_ref, v_ref, bias_ref, o_ref, va_ref, *, rows):
    nq = NA_QROWS * GRID_W
    nk = NA_KROWS * GRID_W
    lane_q = lax.broadcasted_iota(jnp.int32, (nq, LANES), 1)
    lo_q = lane_q < HEAD_DIM
    n_groups = rows // NA_QROWS
    scale = HEAD_DIM ** -0.5 * LOG2E
    for blk in range(B_HEADS // 2):
        va_ref[2 * blk], va_ref[2 * blk + 1] = _ones_halves(v_ref[:, blk * LANES:(blk + 1) * LANES])

    def body(gi, carry):
        ws = jnp.clip(gi * NA_QROWS - 4, 0, rows - NA_KROWS)
        cls = jnp.where(gi == 0, 0, jnp.where(gi == n_groups - 1, 2, 1))
        qr = pl.multiple_of(gi * nq, nq)
        kr = pl.multiple_of(ws * GRID_W, GRID_W)
        for blk in range(B_HEADS // 2):
            sl = slice(blk * LANES, (blk + 1) * LANES)
            qt = q_ref[pl.ds(qr, nq), sl] * scale
            kw = k_ref[pl.ds(kr, nk), sl]
            zero = jnp.zeros_like(qt)
            s_lo = _dot_nt(jnp.where(lo_q, qt, zero), kw) + bias_ref[cls, 2 * blk].astype(jnp.float32)
            s_hi = _dot_nt(jnp.where(lo_q, zero, qt), kw) + bias_ref[cls, 2 * blk + 1].astype(jnp.float32)
            out = _attend_pair(s_lo, s_hi, va_ref[2 * blk, pl.ds(kr, nk), :], va_ref[2 * blk + 1, pl.ds(kr, nk), :], lo_q)
            o_ref[pl.ds(qr, nq), sl] = out.astype(o_ref.dtype)
        return carry

    lax.fori_loop(0, n_groups, body, 0, unroll=4)


def _attn_b(h, bias, batch, seq):
    rows = seq // GRID_W
    kern = functools.partial(_attn_b_kernel, rows=rows)
    return pl.pallas_call(
        kern,
        out_shape=jax.ShapeDtypeStruct((batch * seq, MIX_W), jnp.bfloat16),
        grid=(batch,),
        in_specs=[pl.BlockSpec((seq, 256), lambda b: (b, COL_BQ // 256)),
                  pl.BlockSpec((seq, 256), lambda b: (b, COL_BK // 256)),
                  pl.BlockSpec((seq, 256), lambda b: (b, COL_BV // 256)),
                  pl.BlockSpec(bias.shape, lambda b: (0, 0, 0, 0))],
        out_specs=pl.BlockSpec((seq, MIX_W), lambda b: (b, 0)),
        scratch_shapes=[pltpu.VMEM((B_HEADS, seq, LANES), jnp.bfloat16)],
        compiler_params=_cparams(("parallel",)),
    )(h, h, h, bias)


def _attn_c_kernel(hc_ref, wuq_ref, wuk_ref, wuv_ref, gq_ref, gkv_ref, cos_ref, sin_ref, o_ref,
                   qs_ref, ks_ref, vs_ref, *, tq):
    s_len = hc_ref.shape[0]
    lane = lax.broadcasted_iota(jnp.int32, (s_len, LANES), 1)
    cos = cos_ref[...]
    sin = sin_ref[...]

    cq = hc_ref[:, 0:256].astype(jnp.float32)
    ms = jnp.sum(cq * cq, axis=-1, keepdims=True) * (1.0 / C_Q_RANK)
    cqn = (cq * lax.rsqrt(ms + RMS_EPS) * gq_ref[...]).astype(jnp.bfloat16)
    ckv = hc_ref[:, 256:384].astype(jnp.float32)
    ms = jnp.mean(ckv * ckv, axis=-1, keepdims=True)
    kvn = (ckv * lax.rsqrt(ms + RMS_EPS) * gkv_ref[...]).astype(jnp.bfloat16)
    kpe = _rotate_pairs(hc_ref[:, 384:512].astype(jnp.float32), cos, sin, C_ROPE // 2, lane)
    v_all = _dot(kvn, wuv_ref[...]).astype(jnp.bfloat16)
    for blk in range(C_HEADS // 2):
        vs_ref[2 * blk], vs_ref[2 * blk + 1] = _ones_halves(v_all[:, blk * LANES:(blk + 1) * LANES])
    scale = (C_NOPE + C_ROPE) ** -0.5 * LOG2E
    for h in range(C_HEADS):
        sl = slice(h * LANES, (h + 1) * LANES)
        qh = _rotate_pairs(_dot(cqn, wuq_ref[:, sl]), cos, sin, C_ROPE // 2, lane) * scale
        qs_ref[:, sl] = qh.astype(jnp.bfloat16)
        ks_ref[:, sl] = (_dot(kvn, wuk_ref[:, sl]) + kpe).astype(jnp.bfloat16)

    lane_q = lax.broadcasted_iota(jnp.int32, (tq, LANES), 1)
    lo_q = lane_q < C_V

    def body(i, carry):
        r = pl.multiple_of(i * tq, tq)
        for blk in range(C_HEADS // 2):
            s = []
            for par in range(2):
                sl = slice((2 * blk + par) * LANES, (2 * blk + par + 1) * LANES)
                s.append(_dot_nt(qs_ref[pl.ds(r, tq), sl], ks_ref[:, sl]))
            out = _attend_pair(s[0], s[1], vs_ref[2 * blk], vs_ref[2 * blk + 1], lo_q)
            o_ref[pl.ds(r, tq), blk * LANES:(blk + 1) * LANES] = out.astype(o_ref.dtype)
        return carry

    lax.fori_loop(0, s_len // tq, body, 0, unroll=4)


def _attn_c(h, wuq, wuk, wuv, gq, gkv, cos_c, sin_c, batch, seq, tq=256):
    kern = functools.partial(_attn_c_kernel, tq=tq)
    const = lambda shape: pl.BlockSpec(shape, lambda b: (0,) * len(shape))
    return pl.pallas_call(
        kern,
        out_shape=jax.ShapeDtypeStruct((batch * seq, MIX_W), jnp.bfloat16),
        grid=(batch,),
        in_specs=[pl.BlockSpec((seq, 512), lambda b: (b, COL_C // 512)),
                  const(wuq.shape), const(wuk.shape), const(wuv.shape),
                  const(gq.shape), const(gkv.shape), const(cos_c.shape), const(sin_c.shape)],
        out_specs=pl.BlockSpec((seq, MIX_W), lambda b: (b, 0)),
        scratch_shapes=[pltpu.VMEM((seq, C_HEADS * LANES), jnp.bfloat16),
                        pltpu.VMEM((seq, C_HEADS * LANES), jnp.bfloat16),
                        pltpu.VMEM((C_HEADS, seq, LANES), jnp.bfloat16)],
        compiler_params=_cparams(("parallel",)),
    )(h, wuq, wuk, wuv, gq, gkv, cos_c, sin_c)


CONV_PAD = 16
CONV_ROWS = 128


def _conv_kernel(hd_ref, w_ref, cb_ref, g_ref, b_ref, o_ref, up_ref):
    s_len = hd_ref.shape[0]
    a = hd_ref[:, 0:D_CH].astype(jnp.float32)
    gate = hd_ref[:, D_CH:2 * D_CH].astype(jnp.float32)
    zeros = jnp.zeros((CONV_PAD, D_CH), jnp.float32)
    up_ref[0, 0:CONV_PAD, :] = zeros
    up_ref[0, CONV_PAD + s_len:2 * CONV_PAD + s_len, :] = zeros
    up_ref[0, CONV_PAD:CONV_PAD + s_len, :] = a * _sigmoid(gate)
    n_shifted = s_len + 2 * CONV_PAD - SUBLANES
    for s in range(1, SUBLANES):
        up_ref[s, 0:n_shifted, :] = up_ref[0, s:s + n_shifted, :]
    shift = CONV_PAD - CONV_W // 2
    for c in range(s_len // CONV_ROWS):
        acc = jnp.zeros((CONV_ROWS, D_CH), jnp.float32)
        for j in range(CONV_W):
            s = (j + shift) % SUBLANES
            start = c * CONV_ROWS + j + shift - s
            acc = acc + up_ref[s, start:start + CONV_ROWS, :] * w_ref[j:j + 1, :]
        u = acc + cb_ref[...]
        y = _layer_norm_rows(u, g_ref[...], b_ref[...])
        o_ref[c * CONV_ROWS:(c + 1) * CONV_ROWS, :] = (y * _sigmoid(y)).astype(o_ref.dtype)


def _conv_branch(h, w, cb, g, b, batch, seq):
    const = lambda shape: pl.BlockSpec(shape, lambda i: (0,) * len(shape))
    return pl.pallas_call(
        _conv_kernel,
        out_shape=jax.ShapeDtypeStruct((batch * seq, MIX_W), jnp.bfloat16),
        grid=(batch,),
        in_specs=[pl.BlockSpec((seq, 512), lambda i: (i, COL_D // 512)),
                  const(w.shape), const(cb.shape), const(g.shape), const(b.shape)],
        out_specs=pl.BlockSpec((seq, MIX_W), lambda i: (i, 0)),
        scratch_shapes=[pltpu.VMEM((SUBLANES, seq + 2 * CONV_PAD, D_CH), jnp.float32)],
        compiler_params=_cparams(("parallel",)),
    )(h, w, cb, g, b)


def _merge_kernel(gl_ref, ya_ref, yb_ref, yc_ref, yd_ref, x_ref, wb_ref, wo_ref, g_ref, b_ref,
                  of_ref, ob_ref):
    merged = None
    for n, y_ref in enumerate((ya_ref, yb_ref, yc_ref, yd_ref)):
        t = jnp.tanh(gl_ref[:, n * D_MODEL:(n + 1) * D_MODEL].astype(jnp.float32))
        term = (t + 1.0) * _dot(y_ref[...], wb_ref[n])
        merged = term if merged is None else merged + term
    mix = _dot(merged.astype(jnp.bfloat16), wo_ref[...])
    y = _layer_norm_rows(ALPHA * x_ref[...] + mix, g_ref[...], b_ref[...])
    of_ref[...] = y
    ob_ref[...] = y.astype(jnp.bfloat16)


def _merge(h, ya, yb, yc, yd, x, wb, wo, g, b, tm=512):
    t, d = x.shape
    row = lambda w: pl.BlockSpec((tm, w), lambda i: (i, 0))
    const = lambda shape: pl.BlockSpec(shape, lambda i: (0,) * len(shape))
    return pl.pallas_call(
        _merge_kernel,
        out_shape=(jax.ShapeDtypeStruct((t, d), jnp.float32), jax.ShapeDtypeStruct((t, d), jnp.bfloat16)),
        grid=(t // tm,),
        in_specs=[pl.BlockSpec((tm, N_BRANCH * d), lambda i: (i, COL_GATES)),
                  row(MIX_W), row(MIX_W), row(MIX_W), row(MIX_W), row(d),
                  const(wb.shape), const(wo.shape), const((1, d)), const((1, d))],
        out_specs=(row(d), row(d)),
        compiler_params=_cparams(("parallel",)),
    )(h, ya, yb, yc, yd, x, wb, wo, g.reshape(1, d), b.reshape(1, d))


def _ffn_kernel(xb_ref, xf_ref, w13_ref, w2_ref, g_ref, b_ref, of_ref, ob_ref, *, tf):
    x = xb_ref[...]
    f = w2_ref.shape[0]
    acc = None
    for c in range(f // tf):
        ab = _dot(x, w13_ref[:, 2 * c * tf:2 * (c + 1) * tf])
        a = ab[:, :tf]
        hid = (a * _sigmoid(a) * ab[:, tf:]).astype(jnp.bfloat16)
        part = _dot(hid, w2_ref[c * tf:(c + 1) * tf, :])
        acc = part if acc is None else acc + part
    y = _layer_norm_rows(ALPHA * xf_ref[...] + acc, g_ref[...], b_ref[...])
    of_ref[...] = y
    ob_ref[...] = y.astype(jnp.bfloat16)


def _ffn(xb, xf, w1, w3, w2, g, b, tm=512, tf=1408):
    t, d = xf.shape
    f = w1.shape[1]
    w13 = jnp.concatenate([w[:, j * tf:(j + 1) * tf] for j in range(f // tf) for w in (w1, w3)], axis=1)
    row = pl.BlockSpec((tm, d), lambda i: (i, 0))
    vec = pl.BlockSpec((1, d), lambda i: (0, 0))
    resident = lambda shape: pl.BlockSpec(shape, lambda i: (0, 0), pipeline_mode=pl.Buffered(1))
    return pl.pallas_call(
        functools.partial(_ffn_kernel, tf=tf),
        out_shape=(jax.ShapeDtypeStruct((t, d), jnp.float32), jax.ShapeDtypeStruct((t, d), jnp.bfloat16)),
        grid=(t // tm,),
        in_specs=[row, row, resident(w13.shape), resident(w2.shape), vec, vec],
        out_specs=(row, row),
        compiler_params=_cparams(("parallel",)),
    )(xb, xf, w13, w2, g.reshape(1, d), b.reshape(1, d))


def _pack_rows(x):
    half = x.shape[1] // 2
    bits = lax.bitcast_convert_type(x.astype(jnp.bfloat16).astype(jnp.float32), jnp.uint32)
    return (bits[:, :half] >> 16) | (bits[:, half:] & jnp.uint32(0xFFFF0000))


def _unpack_rows(p):
    lo = lax.bitcast_convert_type(p << 16, jnp.float32)
    hi = lax.bitcast_convert_type(p & jnp.uint32(0xFFFF0000), jnp.float32)
    return jnp.concatenate([lo, hi], axis=1)


def _router_kernel(x_ref, r_ref, tri_ref, oi_ref, ow_ref, cnt_ref, xp_ref, carry_ref):
    i = pl.program_id(0)

    @pl.when(i == 0)
    def _():
        carry_ref[...] = jnp.zeros_like(carry_ref)

    x = x_ref[...]
    r = r_ref[...]
    xh = x.astype(jnp.bfloat16)
    xl = (x - xh.astype(jnp.float32)).astype(jnp.bfloat16)
    rh = r.astype(jnp.bfloat16)
    rl = (r - rh.astype(jnp.float32)).astype(jnp.bfloat16)
    r2 = jnp.concatenate([rh, rl], axis=1)
    p = _dot(xh, r2) + _dot(xl, r2)
    logits = p[:, :LANES] + p[:, LANES:]
    tm = x.shape[0]
    lane = lax.broadcasted_iota(jnp.int32, (tm, LANES), 1)
    logits = jnp.where(lane < N_EXPERTS, logits, NEG_BIG)
    m1 = jnp.max(logits, axis=-1, keepdims=True)
    i1 = jnp.min(jnp.where(logits == m1, lane, LANES), axis=-1, keepdims=True)
    rest = jnp.where(lane == i1, NEG_BIG, logits)
    m2 = jnp.max(rest, axis=-1, keepdims=True)
    i2 = jnp.min(jnp.where(rest == m2, lane, LANES), axis=-1, keepdims=True)
    e = jnp.exp(m2 - m1)
    w1 = 1.0 / (1.0 + e)
    w2 = e / (1.0 + e)
    sel1 = lane == i1
    sel2 = lane == i2
    member = jnp.where(sel1 | sel2, 1.0, 0.0)
    before = _dot(tri_ref[...], member.astype(jnp.bfloat16)) + carry_ref[...]
    rank1 = jnp.sum(jnp.where(sel1, before, 0.0), axis=-1, keepdims=True).astype(jnp.int32)
    rank2 = jnp.sum(jnp.where(sel2, before, 0.0), axis=-1, keepdims=True).astype(jnp.int32)
    carry_ref[...] += jnp.sum(member, axis=0, keepdims=True)
    oi = jnp.where(lane == 0, i1, jnp.where(lane == 1, i2, jnp.where(lane == 2, rank1, rank2)))
    oi_ref[...] = oi.T[0:SUBLANES, :]
    ow_ref[...] = jnp.where(lane == 0, w1, w2)
    cnt_ref[...] = carry_ref[...]
    xp_ref[...] = _pack_rows(x)


def _router(xf, router, tm=512):
    t, d = xf.shape
    r_pad = jnp.zeros((d, LANES), jnp.float32).at[:, :N_EXPERTS].set(router.astype(jnp.float32))
    tri = jnp.asarray(np.tril(np.ones((tm, tm), np.float32), -1), jnp.bfloat16)
    return pl.pallas_call(
        _router_kernel,
        out_shape=(jax.ShapeDtypeStruct((SUBLANES, t), jnp.int32),
                   jax.ShapeDtypeStruct((t, LANES), jnp.float32),
                   jax.ShapeDtypeStruct((1, LANES), jnp.float32),
                   jax.ShapeDtypeStruct((t, d // 2), jnp.uint32)),
        grid=(t // tm,),
        in_specs=[pl.BlockSpec((tm, d), lambda i: (i, 0)),
                  pl.BlockSpec((d, LANES), lambda i: (0, 0)),
                  pl.BlockSpec((tm, tm), lambda i: (0, 0))],
        out_specs=(pl.BlockSpec((SUBLANES, tm), lambda i: (0, i)),
                   pl.BlockSpec((tm, LANES), lambda i: (i, 0)),
                   pl.BlockSpec((1, LANES), lambda i: (0, 0)),
                   pl.BlockSpec((tm, d // 2), lambda i: (i, 0))),
        scratch_shapes=[pltpu.VMEM((1, LANES), jnp.float32)],
        compiler_params=_cparams(("arbitrary",)),
    )(xf, r_pad, tri)


SC_CORES = 2
SC_SUBCORES = 16
SC_WINDOW = 64


def _sc_gather_rows(table, idx):
    m = idx.shape[0]
    d = table.shape[1]
    n_workers = SC_CORES * SC_SUBCORES
    per_w = m // n_workers
    n_win = per_w // SC_WINDOW
    assert per_w * n_workers == m and n_win * SC_WINDOW == per_w
    mesh = plsc.VectorSubcoreMesh(core_axis_name="c", subcore_axis_name="s")

    @functools.partial(
        pl.kernel, mesh=mesh,
        out_type=jax.ShapeDtypeStruct((m, d), table.dtype),
        scratch_types=[pltpu.VMEM((per_w,), jnp.int32),
                       pltpu.VMEM((SC_WINDOW, d), table.dtype),
                       pltpu.SemaphoreType.DMA],
    )
    def gather(table_hbm, idx_hbm, out_hbm, idx_v, rows_v, sem):
        wid = lax.axis_index("s") * SC_CORES + lax.axis_index("c")
        base = wid * per_w
        pltpu.sync_copy(idx_hbm.at[pl.ds(base, per_w)], idx_v)

        @pl.loop(0, n_win)
        def _(w):
            off = pl.multiple_of(w * SC_WINDOW, SC_WINDOW)
            pltpu.async_copy(table_hbm.at[idx_v.at[pl.ds(off, SC_WINDOW)]], rows_v, sem).wait()
            pltpu.sync_copy(rows_v, out_hbm.at[pl.ds(base + off, SC_WINDOW)])

    return gather(table, idx)


def _sc_scatter_rows(rows, idx, n_out):
    n_copies, m = idx.shape
    d = rows.shape[1]
    n_workers = SC_CORES * SC_SUBCORES
    per_w = m // n_workers
    n_win = per_w // SC_WINDOW
    assert per_w * n_workers == m and n_win * SC_WINDOW == per_w
    idx_w = idx.reshape(n_copies, n_workers, n_win, SC_WINDOW).transpose(1, 0, 2, 3)
    idx_w = idx_w.reshape(n_workers, n_copies * n_win, SC_WINDOW)
    mesh = plsc.VectorSubcoreMesh(core_axis_name="c", subcore_axis_name="s")

    @functools.partial(
        pl.kernel, mesh=mesh,
        out_type=jax.ShapeDtypeStruct((n_out, d), rows.dtype),
        scratch_types=[pltpu.VMEM((n_copies * n_win, SC_WINDOW), jnp.int32),
                       pltpu.VMEM((SC_WINDOW, d), rows.dtype)],
    )
    def scatter(rows_hbm, idx_hbm, out_hbm, idx_v, rows_v):
        wid = lax.axis_index("s") * SC_CORES + lax.axis_index("c")
        base = wid * per_w
        pltpu.sync_copy(idx_hbm.at[wid], idx_v)

        @pl.loop(0, n_win)
        def _(w):
            off = pl.multiple_of(w * SC_WINDOW, SC_WINDOW)
            pltpu.sync_copy(rows_hbm.at[pl.ds(base + off, SC_WINDOW)], rows_v)
            for k in range(n_copies):
                pltpu.sync_copy(rows_v, out_hbm.at[idx_v.at[k * n_win + w]])

    return scatter(rows, idx_w)


def _expert_kernel(te_ref, nv_ref, x_ref, w1_ref, w3_ref, w2_ref, o_ref,
                   w13s_ref, w2s_ref, xb_ref, acc_ref):
    i = pl.program_id(0)
    j = pl.program_id(1)
    last = pl.num_programs(1) - 1
    n_valid = nv_ref[i]
    n_piece = MOE_TILE // MOE_SUB
    piece_rows = [pl.ds(s * MOE_SUB, MOE_SUB) for s in range(n_piece)]

    for s, rows in enumerate(piece_rows):
        @pl.when((n_valid > s * MOE_SUB) & (j == 0))
        def _():
            row = lax.broadcasted_iota(jnp.int32, (MOE_SUB, 1), 0) + s * MOE_SUB
            x = jnp.where(row < n_valid, _unpack_rows(x_ref[rows, :]), 0.0)
            xb_ref[rows, :] = x.astype(jnp.bfloat16)
            acc_ref[rows, :] = jnp.zeros((MOE_SUB, acc_ref.shape[1]), jnp.float32)

    def compute(n_live):
        w13s_ref[:, 0:MOE_FCHUNK] = w1_ref[0].astype(jnp.bfloat16)
        w13s_ref[:, MOE_FCHUNK:2 * MOE_FCHUNK] = w3_ref[0].astype(jnp.bfloat16)
        w2s_ref[...] = w2_ref[0].astype(jnp.bfloat16)
        for rows in piece_rows[:n_live]:
            ab = _dot(xb_ref[rows, :], w13s_ref[...])
            a = ab[:, :MOE_FCHUNK]
            hid = (a * _sigmoid(a) * ab[:, MOE_FCHUNK:]).astype(jnp.bfloat16)
            acc_ref[rows, :] += _dot(hid, w2s_ref[...])

    for n_live in range(1, n_piece + 1):
        upper = n_live * MOE_SUB if n_live < n_piece else MOE_TILE
        pl.when((n_valid > (n_live - 1) * MOE_SUB) & (n_valid <= upper))(functools.partial(compute, n_live))

    for s, rows in enumerate(piece_rows):
        @pl.when((n_valid > s * MOE_SUB) & (j == last))
        def _():
            o_ref[rows, :] = _pack_rows(acc_ref[rows, :])

        @pl.when((n_valid <= s * MOE_SUB) & (j == last))
        def _():
            o_ref[rows, :] = jnp.zeros((MOE_SUB, o_ref.shape[1]), o_ref.dtype)


def _experts(tile_expert, tile_valid, xs, w1, w3, w2):
    n_rows = xs.shape[0]
    d, f = w1.shape[1], w1.shape[2]
    nj = f // MOE_FCHUNK

    def chunk(i, j, nv):
        return jnp.where(nv[i] > 0, j, nj - 1)

    return pl.pallas_call(
        _expert_kernel,
        out_shape=jax.ShapeDtypeStruct((n_rows, d // 2), jnp.uint32),
        grid_spec=pltpu.PrefetchScalarGridSpec(
            num_scalar_prefetch=2,
            grid=(n_rows // MOE_TILE, nj),
            in_specs=[pl.BlockSpec((MOE_TILE, d // 2), lambda i, j, te, nv: (i, 0)),
                      pl.BlockSpec((1, d, MOE_FCHUNK), lambda i, j, te, nv: (te[i], 0, chunk(i, j, nv))),
                      pl.BlockSpec((1, d, MOE_FCHUNK), lambda i, j, te, nv: (te[i], 0, chunk(i, j, nv))),
                      pl.BlockSpec((1, MOE_FCHUNK, d), lambda i, j, te, nv: (te[i], chunk(i, j, nv), 0))],
            out_specs=pl.BlockSpec((MOE_TILE, d // 2), lambda i, j, te, nv: (i, 0)),
            scratch_shapes=[pltpu.VMEM((d, 2 * MOE_FCHUNK), jnp.bfloat16),
                            pltpu.VMEM((MOE_FCHUNK, d), jnp.bfloat16),
                            pltpu.VMEM((MOE_TILE, d), jnp.bfloat16),
                            pltpu.VMEM((MOE_TILE, d), jnp.float32)],
        ),
        compiler_params=pltpu.CompilerParams(dimension_semantics=("arbitrary", "arbitrary"),
                                             vmem_limit_bytes=MOE_VMEM_LIMIT),
    )(tile_expert, tile_valid, xs, w1, w3, w2)


def _combine_kernel(ya_ref, yb_ref, x_ref, w_ref, g_ref, b_ref, o_ref):
    w = w_ref[...]
    f = w[:, 0:1] * _unpack_rows(ya_ref[...]) + w[:, 1:2] * _unpack_rows(yb_ref[...])
    o_ref[...] = _layer_norm_rows(ALPHA * x_ref[...] + f, g_ref[...], b_ref[...])


def _combine(yg, xf, wts, g, b, tm=512):
    t, d = xf.shape
    nt = t // tm
    return pl.pallas_call(
        _combine_kernel,
        out_shape=jax.ShapeDtypeStruct((t, d), jnp.float32),
        grid=(nt,),
        in_specs=[pl.BlockSpec((tm, d // 2), lambda i: (i, 0)),
                  pl.BlockSpec((tm, d // 2), lambda i: (i + nt, 0)),
                  pl.BlockSpec((tm, d), lambda i: (i, 0)),
                  pl.BlockSpec((tm, LANES), lambda i: (i, 0)),
                  pl.BlockSpec((1, d), lambda i: (0, 0)),
                  pl.BlockSpec((1, d), lambda i: (0, 0))],
        out_specs=pl.BlockSpec((tm, d), lambda i: (i, 0)),
        compiler_params=_cparams(("parallel",)),
    )(yg, yg, xf, wts, g.reshape(1, d), b.reshape(1, d))


def _moe_ffn(xf, router, w1, w3, w2, g, b):
    t, d = xf.shape
    oi, ow, cnt, xp = _router(xf, router)
    e1, e2, rank1, rank2 = oi[0], oi[1], oi[2], oi[3]
    counts = cnt[0, :N_EXPERTS].astype(jnp.int32)
    tiles = (counts + MOE_TILE - 1) // MOE_TILE
    tile_end = jnp.cumsum(tiles)
    tile_start = tile_end - tiles
    offs = tile_start * MOE_TILE
    dest = jnp.concatenate([offs[e1] + rank1, offs[e2] + rank2])
    n_rows = TOP_K * t + N_EXPERTS * MOE_TILE
    n_tiles = n_rows // MOE_TILE
    tile_id = jnp.arange(n_tiles, dtype=jnp.int32)
    tile_expert = jnp.sum(tile_id[:, None] >= tile_end[None, :], axis=1)
    used = tile_expert < N_EXPERTS
    last_expert = jnp.sum(tile_end[-1] - 1 >= tile_end)
    tile_expert = jnp.where(used, tile_expert, last_expert).astype(jnp.int32)
    tile_valid = jnp.clip(counts[tile_expert] - (tile_id - tile_start[tile_expert]) * MOE_TILE, 0, MOE_TILE)
    tile_valid = jnp.where(used, tile_valid, 0).astype(jnp.int32)
    xs = _sc_scatter_rows(xp, dest.reshape(TOP_K, t), n_rows)
    y = _experts(tile_expert, tile_valid, xs, w1, w3, w2)
    yg = _sc_gather_rows(y, dest)
    return _combine(yg, xf, ow, g, b)


def _prep_layer(l, p, in_cols):
    bf = jnp.bfloat16
    out = {}
    n_gate = N_BRANCH * D_MODEL
    w_l = p["w_in"][l]
    gate_cols = (0.5 * w_l[:, int(in_cols[0]):int(in_cols[0]) + n_gate]).astype(bf)
    out["w_in"] = jnp.concatenate([gate_cols, _take_cols(w_l, in_cols[n_gate:]).astype(bf)], axis=1)
    deint = np.concatenate([np.arange(0, HEAD_DIM, 2), np.arange(1, HEAD_DIM, 2)])
    out["gq_a"] = jnp.tile(p["a_q_norm"][l][deint], 2).reshape(1, LANES)
    out["gk_a"] = jnp.tile(p["a_k_norm"][l][deint], 2).reshape(1, LANES)
    per_head = C_NOPE + C_ROPE
    uq_cols = []
    for h in range(C_HEADS):
        base = h * per_head
        uq_cols += [base + np.arange(C_NOPE), base + C_NOPE + np.arange(0, C_ROPE, 2),
                    base + C_NOPE + np.arange(1, C_ROPE, 2), np.full(32, -1)]
    wuq = _take_cols(p["c_w_uq"][l], np.concatenate(uq_cols))
    out["wuq"] = jnp.concatenate([wuq, jnp.zeros((64, wuq.shape[1]), wuq.dtype)], axis=0).astype(bf)
    uk_cols, uv_cols = [], []
    for h in range(C_HEADS):
        base = h * (C_NOPE + C_V)
        uk_cols += [base + np.arange(C_NOPE), np.full(64, -1)]
        uv_cols += [base + C_NOPE + np.arange(C_V)]
    out["wuk"] = _take_cols(p["c_w_ukv"][l], np.concatenate(uk_cols)).astype(bf)
    out["wuv"] = _take_cols(p["c_w_ukv"][l], np.concatenate(uv_cols)).astype(bf)
    out["gq_c"] = jnp.concatenate([p["c_q_norm"][l], jnp.zeros((64,), jnp.float32)]).reshape(1, 256)
    out["gkv_c"] = p["c_kv_norm"][l].reshape(1, C_KV_RANK)
    out["conv_w"] = jnp.concatenate([p["d_conv_w"][l][:, 0, :], jnp.zeros((1, D_CH), jnp.float32)], axis=0)
    out["conv_b"] = p["d_conv_b"][l].reshape(1, D_CH)
    out["d_ln_g"] = p["d_ln_g"][l].reshape(1, D_CH)
    out["d_ln_b"] = p["d_ln_b"][l].reshape(1, D_CH)
    a_rows = np.concatenate([(kvh * 2 + g) * HEAD_DIM + np.arange(HEAD_DIM)
                             for g in range(2) for kvh in range(A_KV_HEADS)])
    wb = p["w_branch"][l]
    out["w_branch"] = (0.5 * jnp.stack([wb[0][a_rows], wb[1], wb[2], wb[3]], axis=0)).astype(bf)
    out["w_out"] = p["w_out"][l].astype(bf)
    return out


def kernel(x, ln_in_g, ln_in_b, w_in, a_q_norm, a_k_norm, b_rpb, c_q_norm, c_kv_norm, c_w_uq, c_w_ukv,
           d_conv_w, d_conv_b, d_ln_g, d_ln_b, w_branch, w_out, ln_mix_g, ln_mix_b,
           ffn_w1, ffn_w3, ffn_w2, moe_router, moe_w1, moe_w3, moe_w2, ln_ffn_g, ln_ffn_b):
    batch, seq, d = x.shape
    t = batch * seq
    bf = jnp.bfloat16
    params = dict(w_in=w_in, a_q_norm=a_q_norm, a_k_norm=a_k_norm, c_q_norm=c_q_norm, c_kv_norm=c_kv_norm,
                  c_w_uq=c_w_uq, c_w_ukv=c_w_ukv, d_conv_w=d_conv_w, d_conv_b=d_conv_b, d_ln_g=d_ln_g,
                  d_ln_b=d_ln_b, w_branch=w_branch, w_out=w_out)
    in_cols = _in_proj_columns()
    cos_a, sin_a, cos_c, sin_c = _rope_tables(seq)
    xf = xb = None
    for l in range(DEPTH):
        p = _prep_layer(l, params, in_cols)
        if l == 0:
            h, xf = _in_proj(x.reshape(t, d), p["w_in"], ln=(ln_in_g, ln_in_b))
        else:
            h = _in_proj(xb, p["w_in"])
        ya = _attn_a(h, cos_a, sin_a, p["gq_a"], p["gk_a"], batch, seq)
        yb = _attn_b(h, _na_bias_tables(b_rpb[l], seq // GRID_W), batch, seq)
        yc = _attn_c(h, p["wuq"], p["wuk"], p["wuv"], p["gq_c"], p["gkv_c"], cos_c, sin_c, batch, seq)
        yd = _conv_branch(h, p["conv_w"], p["conv_b"], p["d_ln_g"], p["d_ln_b"], batch, seq)
        xf, xb = _merge(h, ya, yb, yc, yd, xf, p["w_branch"], p["w_out"], ln_mix_g[l], ln_mix_b[l])
        if l % 2 == 0:
            i = l // 2
            xf, xb = _ffn(xb, xf, ffn_w1[i].astype(bf), ffn_w3[i].astype(bf), ffn_w2[i].astype(bf),
                          ln_ffn_g[l], ln_ffn_b[l])
        else:
            i = l // 2
            xf = _moe_ffn(xf, moe_router[i], moe_w1[i], moe_w3[i], moe_w2[i], ln_ffn_g[l], ln_ffn_b[l])
            xb = xf.astype(bf)
    return xf.reshape(batch, seq, d)
```

```python
import functools

import numpy as np
import jax
import jax.numpy as jnp
from jax import lax
from jax.experimental import pallas as pl
from jax.experimental.pallas import tpu as pltpu
from jax.experimental.pallas import tpu_sc as plsc

D_MODEL = 1024
DEPTH = 2
GRID_W = 64
HEAD_DIM = 64
ROPE_THETA = 10000.0
RMS_EPS = 1e-6
LN_EPS = 1e-5
A_HEADS = 4
A_KV_HEADS = 2
B_HEADS = 4
NA_WIN_H = 8
NA_WIN_W = 16
C_HEADS = 4
C_NOPE = 64
C_ROPE = 32
C_V = 64
C_Q_RANK = 192
C_KV_RANK = 128
D_CH = 256
CONV_W = 31
N_BRANCH = 4
MIX_W = 256
D_FF = 2816
N_EXPERTS = 8
TOP_K = 2
D_FF_EXPERT = 3584
ALPHA = (2 * DEPTH) ** 0.25

LANES = 128
SUBLANES = 8
MXU_TILE = 256
VMEM_LIMIT = 48 * 1024 * 1024

LOG2E = 1.4426950408889634
NEG_BIG = -1e30

COL_GATES = 0
COL_AQ = 4096
COL_AKV = 4352
COL_C = 4608
COL_D = 5120
COL_BQ = 5632
COL_BK = 5888
COL_BV = 6144
D_IN_PAD = 6400

NA_QROWS = 4
NA_KROWS = 12

MOE_TILE = 2048
MOE_SUB = 512
MOE_FCHUNK = 512
MOE_VMEM_LIMIT = 56 * 1024 * 1024


def _in_proj_columns():
    o_aq, o_ak, o_av = 0, 256, 384
    o_bq, o_bk, o_bv = 512, 768, 1024
    o_cq, o_ckv, o_kpe = 1280, 1472, 1600
    o_d, o_g = 1632, 2144
    deint = np.concatenate([np.arange(0, HEAD_DIM, 2), np.arange(1, HEAD_DIM, 2)])
    cols = [o_g + np.arange(N_BRANCH * D_MODEL)]
    for g in range(2):
        for kvh in range(A_KV_HEADS):
            cols.append(o_aq + (kvh * 2 + g) * HEAD_DIM + deint)
    for kvh in range(A_KV_HEADS):
        cols.append(o_ak + kvh * HEAD_DIM + deint)
    cols.append(o_av + np.arange(A_KV_HEADS * HEAD_DIM))
    cols.append(o_cq + np.arange(C_Q_RANK))
    cols.append(np.full(64, -1))
    cols.append(o_ckv + np.arange(C_KV_RANK))
    cols.append(np.full(64, -1))
    cols.append(o_kpe + np.arange(0, C_ROPE, 2))
    cols.append(o_kpe + np.arange(1, C_ROPE, 2))
    cols.append(np.full(32, -1))
    cols.append(o_d + np.arange(2 * D_CH))
    cols.append(o_bq + np.arange(3 * B_HEADS * HEAD_DIM))
    cols = np.concatenate(cols)
    assert cols.shape[0] == D_IN_PAD
    return cols


def _take_cols(w, cols):
    cols = np.asarray(cols)
    runs, i, n = [], 0, len(cols)
    while i < n:
        j = i + 1
        if cols[i] < 0:
            while j < n and cols[j] < 0:
                j += 1
            kind = "zero"
        else:
            while j < n and cols[j] == cols[j - 1] + 1:
                j += 1
            kind = "slice" if j - i >= 64 else "shuffle"
        if kind == "shuffle" and runs and runs[-1][0] == "shuffle":
            runs[-1] = ("shuffle", runs[-1][1], j)
        else:
            runs.append((kind, i, j))
        i = j
    parts = []
    for kind, i, j in runs:
        if kind == "zero":
            parts.append(jnp.zeros((w.shape[0], j - i), w.dtype))
        elif kind == "slice":
            parts.append(w[:, int(cols[i]):int(cols[i]) + j - i])
        else:
            lo, hi = int(cols[i:j].min()), int(cols[i:j].max()) + 1
            onehot = np.zeros((hi - lo, j - i), np.float32)
            onehot[cols[i:j] - lo, np.arange(j - i)] = 1.0
            parts.append(jnp.dot(w[:, lo:hi], jnp.asarray(onehot, w.dtype), precision=lax.Precision.HIGHEST))
    return jnp.concatenate(parts, axis=1)


def _rope_tables(seq):
    t = np.arange(seq)
    row = (t // GRID_W).astype(np.float32)
    col = (t % GRID_W).astype(np.float32)

    def angles(rot_dim):
        n_freq = rot_dim // 4
        inv = jnp.asarray(ROPE_THETA, jnp.float32) ** (-jnp.arange(n_freq, dtype=jnp.float32) / n_freq)
        return jnp.concatenate([jnp.asarray(row)[:, None] * inv, jnp.asarray(col)[:, None] * inv], axis=-1)

    ang_a = angles(HEAD_DIM)
    ca, sa = jnp.cos(ang_a), jnp.sin(ang_a)
    cos_a = jnp.concatenate([ca, ca, ca, ca], axis=-1)
    sin_a = jnp.concatenate([-sa, sa, -sa, sa], axis=-1)
    ang_c = angles(C_ROPE)
    cc, sc = jnp.cos(ang_c), jnp.sin(ang_c)
    one = jnp.ones((seq, 64), jnp.float32)
    zero = jnp.zeros((seq, 64), jnp.float32)
    cos_c = jnp.concatenate([one, cc, cc, one[:, :32]], axis=-1)
    sin_c = jnp.concatenate([zero, -sc, sc, zero[:, :32]], axis=-1)
    return cos_a, sin_a, cos_c, sin_c


def _na_bias_tables(rpb, rows):
    wh = min(NA_WIN_H, rows)
    n_dr, n_dc = 2 * NA_WIN_H - 1, 2 * NA_WIN_W - 1
    qc = np.arange(GRID_W)[:, None]
    kc = np.arange(GRID_W)[None, :]
    col_start = np.clip(qc - NA_WIN_W // 2, 0, GRID_W - NA_WIN_W)
    ok_c = (kc >= col_start) & (kc < col_start + NA_WIN_W)
    dc = np.clip(kc - qc + (NA_WIN_W - 1), 0, n_dc - 1)
    oh_c = np.eye(n_dc, dtype=np.float32)[dc.reshape(-1)]
    oh_r, ok_r = [], []
    for r0 in (0, NA_QROWS, rows - NA_QROWS):
        ws = int(np.clip(r0 - 4, 0, rows - NA_KROWS))
        qr = r0 + np.arange(NA_QROWS)[:, None]
        kr = ws + np.arange(NA_KROWS)[None, :]
        row_start = np.clip(qr - wh // 2, 0, rows - wh)
        ok_r.append((kr >= row_start) & (kr < row_start + wh))
        dr = np.clip(kr - qr + (NA_WIN_H - 1), 0, n_dr - 1)
        oh_r.append(np.eye(n_dr, dtype=np.float32)[dr.reshape(-1)])
    oh_r = np.concatenate(oh_r, axis=0)
    ok = np.stack(ok_r)[:, :, None, :, None] & ok_c[None, None, :, None, :]
    ok = ok.reshape(3, 1, NA_QROWS * GRID_W, NA_KROWS * GRID_W)
    hi = lax.Precision.HIGHEST
    t1 = jnp.einsum("ma,hab->hmb", jnp.asarray(oh_r), rpb.astype(jnp.float32) * LOG2E, precision=hi)
    t2 = jnp.einsum("hmb,nb->hmn", t1, jnp.asarray(oh_c), precision=hi).astype(jnp.bfloat16)
    nh = rpb.shape[0]
    t2 = t2.reshape(nh, 3, NA_QROWS, NA_KROWS, GRID_W, GRID_W).transpose(1, 0, 2, 4, 3, 5)
    bias = t2.reshape(3, nh, NA_QROWS * GRID_W, NA_KROWS * GRID_W)
    return jnp.where(jnp.asarray(ok), bias, jnp.asarray(NEG_BIG, jnp.bfloat16))


def _cparams(sem):
    return pltpu.CompilerParams(dimension_semantics=sem, vmem_limit_bytes=VMEM_LIMIT)


def _layer_norm_rows(z, g, b):
    mu = jnp.mean(z, axis=-1, keepdims=True)
    zc = z - mu
    var = jnp.mean(zc * zc, axis=-1, keepdims=True)
    return zc * lax.rsqrt(var + LN_EPS) * g + b


def _sigmoid(x):
    return 0.5 * jnp.tanh(0.5 * x) + 0.5


def _dot(a, b):
    return jnp.dot(a, b, preferred_element_type=jnp.float32)


def _dot_nt(a, b):
    return lax.dot_general(a, b, (((1,), (1,)), ((), ())), preferred_element_type=jnp.float32)


IN_PROJ_CHUNK = 1280


def _project_chunks(xb, w_ref, h_ref):
    for c in range(w_ref.shape[1] // IN_PROJ_CHUNK):
        cols = slice(c * IN_PROJ_CHUNK, (c + 1) * IN_PROJ_CHUNK)
        h_ref[:, cols] = _dot(xb, w_ref[:, cols]).astype(h_ref.dtype)


def _in_proj_kernel(x_ref, w_ref, h_ref):
    _project_chunks(x_ref[...], w_ref, h_ref)


def _in_proj_ln_kernel(x_ref, g_ref, b_ref, w_ref, h_ref, xf_ref):
    y = _layer_norm_rows(x_ref[...], g_ref[...], b_ref[...])
    xf_ref[...] = y
    _project_chunks(y.astype(jnp.bfloat16), w_ref, h_ref)


def _in_proj(x, w, ln=None, tm=512):
    t, d = x.shape
    n = w.shape[1]
    row = lambda width: pl.BlockSpec((tm, width), lambda i: (i, 0))
    vec = pl.BlockSpec((1, d), lambda i: (0, 0))
    w_spec = pl.BlockSpec((d, n), lambda i: (0, 0), pipeline_mode=pl.Buffered(1))
    h_shape = jax.ShapeDtypeStruct((t, n), jnp.bfloat16)
    if ln is None:
        return pl.pallas_call(
            _in_proj_kernel, out_shape=h_shape, grid=(t // tm,),
            in_specs=[row(d), w_spec], out_specs=row(n),
            compiler_params=_cparams(("parallel",)),
        )(x, w)
    g, b = ln
    return pl.pallas_call(
        _in_proj_ln_kernel,
        out_shape=(h_shape, jax.ShapeDtypeStruct((t, d), jnp.float32)),
        grid=(t // tm,),
        in_specs=[row(d), vec, vec, w_spec], out_specs=(row(n), row(d)),
        compiler_params=_cparams(("parallel",)),
    )(x, g.reshape(1, d), b.reshape(1, d), w)


def _ones_halves(vblk):
    lane = lax.broadcasted_iota(jnp.int32, vblk.shape, 1)
    ones = jnp.ones_like(vblk)
    return jnp.where(lane < HEAD_DIM, vblk, ones), jnp.where(lane < HEAD_DIM, ones, vblk)


def _attend_pair(s_lo, s_hi, v_lo, v_hi, lo_q):
    outs = []
    for s, v in ((s_lo, v_lo), (s_hi, v_hi)):
        p = jnp.exp2(s - jnp.max(s, axis=-1, keepdims=True)).astype(jnp.bfloat16)
        outs.append(_dot(p, v))
    num = jnp.where(lo_q, outs[0], outs[1])
    den = jnp.where(lo_q, pltpu.roll(outs[0], HEAD_DIM, 1), pltpu.roll(outs[1], HEAD_DIM, 1))
    return num / den


def _rotate_pairs(x, cos, sin, half, lane):
    w = x.shape[-1]
    first = (lane % (2 * half)) < half
    partner = jnp.where(first, pltpu.roll(x, w - half, 1), pltpu.roll(x, half, 1))
    return x * cos + partner * sin


def _attn_a_kernel(q_ref, kv_ref, cos_ref, sin_ref, gq_ref, gk_ref, o_ref, ks_ref, va_ref, *, tq):
    s_len = q_ref.shape[0]

    def prep(x, g, cos, sin):
        lane = lax.broadcasted_iota(jnp.int32, x.shape, 1)
        lo = lane < HEAD_DIM
        x2 = x * x
        ss_lo = jnp.sum(jnp.where(lo, x2, 0.0), axis=-1, keepdims=True)
        ss_hi = jnp.sum(jnp.where(lo, 0.0, x2), axis=-1, keepdims=True)
        ms = jnp.where(lo, ss_lo, ss_hi) * (1.0 / HEAD_DIM)
        xn = x * lax.rsqrt(ms + RMS_EPS) * g
        return _rotate_pairs(xn, cos, sin, HEAD_DIM // 2, lane)

    ks_ref[...] = prep(kv_ref[:, 0:LANES].astype(jnp.float32), gk_ref[...],
                       cos_ref[...], sin_ref[...]).astype(jnp.bfloat16)
    scale = HEAD_DIM ** -0.5 * LOG2E
    lane_q = lax.broadcasted_iota(jnp.int32, (tq, LANES), 1)
    lo_q = lane_q < HEAD_DIM

    va_ref[0], va_ref[1] = _ones_halves(kv_ref[:, LANES:2 * LANES])

    def body(i, carry):
        r = pl.multiple_of(i * tq, tq)
        k = ks_ref[...]
        cos = cos_ref[pl.ds(r, tq), :]
        sin = sin_ref[pl.ds(r, tq), :]
        for g in range(2):
            qt = prep(q_ref[pl.ds(r, tq), g * LANES:(g + 1) * LANES].astype(jnp.float32), gq_ref[...], cos, sin)
            qt = (qt * scale).astype(jnp.bfloat16)
            zero = jnp.zeros_like(qt)
            out = _attend_pair(_dot_nt(jnp.where(lo_q, qt, zero), k), _dot_nt(jnp.where(lo_q, zero, qt), k),
                               va_ref[0], va_ref[1], lo_q)
            o_ref[pl.ds(r, tq), g * LANES:(g + 1) * LANES] = out.astype(o_ref.dtype)
        return carry

    lax.fori_loop(0, s_len // tq, body, 0, unroll=4)


def _attn_a(h, cos_a, sin_a, gq, gk, batch, seq, tq=256):
    kern = functools.partial(_attn_a_kernel, tq=tq)
    return pl.pallas_call(
        kern,
        out_shape=jax.ShapeDtypeStruct((batch * seq, MIX_W), jnp.bfloat16),
        grid=(batch,),
        in_specs=[pl.BlockSpec((seq, 256), lambda b: (b, COL_AQ // 256)),
                  pl.BlockSpec((seq, 256), lambda b: (b, COL_AKV // 256)),
                  pl.BlockSpec((seq, LANES), lambda b: (0, 0)),
                  pl.BlockSpec((seq, LANES), lambda b: (0, 0)),
                  pl.BlockSpec((1, LANES), lambda b: (0, 0)),
                  pl.BlockSpec((1, LANES), lambda b: (0, 0))],
        out_specs=pl.BlockSpec((seq, MIX_W), lambda b: (b, 0)),
        scratch_shapes=[pltpu.VMEM((seq, LANES), jnp.bfloat16), pltpu.VMEM((2, seq, LANES), jnp.bfloat16)],
        compiler_params=_cparams(("parallel",)),
    )(h, h, cos_a, sin_a, gq, gk)


def _attn_b_kernel(q_ref, k_ref, v_ref, bias_ref, o_ref, va_ref, *, rows):
    nq = NA_QROWS * GRID_W
    nk = NA_KROWS * GRID_W
    lane_q = lax.broadcasted_iota(jnp.int32, (nq, LANES), 1)
    lo_q = lane_q < HEAD_DIM
    n_groups = rows // NA_QROWS
    scale = HEAD_DIM ** -0.5 * LOG2E
    for blk in range(B_HEADS // 2):
        va_ref[2 * blk], va_ref[2 * blk + 1] = _ones_halves(v_ref[:, blk * LANES:(blk + 1) * LANES])

    def body(gi, carry):
        ws = jnp.clip(gi * NA_QROWS - 4, 0, rows - NA_KROWS)
        cls = jnp.where(gi == 0, 0, jnp.where(gi == n_groups - 1, 2, 1))
        qr = pl.multiple_of(gi * nq, nq)
        kr = pl.multiple_of(ws * GRID_W, GRID_W)
        for blk in range(B_HEADS // 2):
            sl = slice(blk * LANES, (blk + 1) * LANES)
            qt = q_ref[pl.ds(qr, nq), sl] * scale
            kw = k_ref[pl.ds(kr, nk), sl]
            zero = jnp.zeros_like(qt)
            s_lo = _dot_nt(jnp.where(lo_q, qt, zero), kw) + bias_ref[cls, 2 * blk].astype(jnp.float32)
            s_hi = _dot_nt(jnp.where(lo_q, zero, qt), kw) + bias_ref[cls, 2 * blk + 1].astype(jnp.float32)
            out = _attend_pair(s_lo, s_hi, va_ref[2 * blk, pl.ds(kr, nk), :], va_ref[2 * blk + 1, pl.ds(kr, nk), :], lo_q)
            o_ref[pl.ds(qr, nq), sl] = out.astype(o_ref.dtype)
        return carry

    lax.fori_loop(0, n_groups, body, 0, unroll=4)


def _attn_b(h, bias, batch, seq):
    rows = seq // GRID_W
    kern = functools.partial(_attn_b_kernel, rows=rows)
    return pl.pallas_call(
        kern,
        out_shape=jax.ShapeDtypeStruct((batch * seq, MIX_W), jnp.bfloat16),
        grid=(batch,),
        in_specs=[pl.BlockSpec((seq, 256), lambda b: (b, COL_BQ // 256)),
                  pl.BlockSpec((seq, 256), lambda b: (b, COL_BK // 256)),
                  pl.BlockSpec((seq, 256), lambda b: (b, COL_BV // 256)),
                  pl.BlockSpec(bias.shape, lambda b: (0, 0, 0, 0))],
        out_specs=pl.BlockSpec((seq, MIX_W), lambda b: (b, 0)),
        scratch_shapes=[pltpu.VMEM((B_HEADS, seq, LANES), jnp.bfloat16)],
        compiler_params=_cparams(("parallel",)),
    )(h, h, h, bias)


def _attn_c_kernel(hc_ref, wuq_ref, wuk_ref, wuv_ref, gq_ref, gkv_ref, cos_ref, sin_ref, o_ref,
                   qs_ref, ks_ref, vs_ref, *, tq):
    s_len = hc_ref.shape[0]
    lane = lax.broadcasted_iota(jnp.int32, (s_len, LANES), 1)
    cos = cos_ref[...]
    sin = sin_ref[...]

    cq = hc_ref[:, 0:256].astype(jnp.float32)
    ms = jnp.sum(cq * cq, axis=-1, keepdims=True) * (1.0 / C_Q_RANK)
    cqn = (cq * lax.rsqrt(ms + RMS_EPS) * gq_ref[...]).astype(jnp.bfloat16)
    ckv = hc_ref[:, 256:384].astype(jnp.float32)
    ms = jnp.mean(ckv * ckv, axis=-1, keepdims=True)
    kvn = (ckv * lax.rsqrt(ms + RMS_EPS) * gkv_ref[...]).astype(jnp.bfloat16)
    kpe = _rotate_pairs(hc_ref[:, 384:512].astype(jnp.float32), cos, sin, C_ROPE // 2, lane)
    v_all = _dot(kvn, wuv_ref[...]).astype(jnp.bfloat16)
    for blk in range(C_HEADS // 2):
        vs_ref[2 * blk], vs_ref[2 * blk + 1] = _ones_halves(v_all[:, blk * LANES:(blk + 1) * LANES])
    scale = (C_NOPE + C_ROPE) ** -0.5 * LOG2E
    for h in range(C_HEADS):
        sl = slice(h * LANES, (h + 1) * LANES)
        qh = _rotate_pairs(_dot(cqn, wuq_ref[:, sl]), cos, sin, C_ROPE // 2, lane) * scale
        qs_ref[:, sl] = qh.astype(jnp.bfloat16)
        ks_ref[:, sl] = (_dot(kvn, wuk_ref[:, sl]) + kpe).astype(jnp.bfloat16)

    lane_q = lax.broadcasted_iota(jnp.int32, (tq, LANES), 1)
    lo_q = lane_q < C_V

    def body(i, carry):
        r = pl.multiple_of(i * tq, tq)
        for blk in range(C_HEADS // 2):
            s = []
            for par in range(2):
                sl = slice((2 * blk + par) * LANES, (2 * blk + par + 1) * LANES)
                s.append(_dot_nt(qs_ref[pl.ds(r, tq), sl], ks_ref[:, sl]))
            out = _attend_pair(s[0], s[1], vs_ref[2 * blk], vs_ref[2 * blk + 1], lo_q)
            o_ref[pl.ds(r, tq), blk * LANES:(blk + 1) * LANES] = out.astype(o_ref.dtype)
        return carry

    lax.fori_loop(0, s_len // tq, body, 0, unroll=4)


def _attn_c(h, wuq, wuk, wuv, gq, gkv, cos_c, sin_c, batch, seq, tq=256):
    kern = functools.partial(_attn_c_kernel, tq=tq)
    const = lambda shape: pl.BlockSpec(shape, lambda b: (0,) * len(shape))
    return pl.pallas_call(
        kern,
        out_shape=jax.ShapeDtypeStruct((batch * seq, MIX_W), jnp.bfloat16),
        grid=(batch,),
        in_specs=[pl.BlockSpec((seq, 512), lambda b: (b, COL_C // 512)),
                  const(wuq.shape), const(wuk.shape), const(wuv.shape),
                  const(gq.shape), const(gkv.shape), const(cos_c.shape), const(sin_c.shape)],
        out_specs=pl.BlockSpec((seq, MIX_W), lambda b: (b, 0)),
        scratch_shapes=[pltpu.VMEM((seq, C_HEADS * LANES), jnp.bfloat16),
                        pltpu.VMEM((seq, C_HEADS * LANES), jnp.bfloat16),
                        pltpu.VMEM((C_HEADS, seq, LANES), jnp.bfloat16)],
        compiler_params=_cparams(("parallel",)),
    )(h, wuq, wuk, wuv, gq, gkv, cos_c, sin_c)


CONV_PAD = 16
CONV_ROWS = 128


def _conv_kernel(hd_ref, w_ref, cb_ref, g_ref, b_ref, o_ref, up_ref):
    s_len = hd_ref.shape[0]
    a = hd_ref[:, 0:D_CH].astype(jnp.float32)
    gate = hd_ref[:, D_CH:2 * D_CH].astype(jnp.float32)
    zeros = jnp.zeros((CONV_PAD, D_CH), jnp.float32)
    up_ref[0, 0:CONV_PAD, :] = zeros
    up_ref[0, CONV_PAD + s_len:2 * CONV_PAD + s_len, :] = zeros
    up_ref[0, CONV_PAD:CONV_PAD + s_len, :] = a * _sigmoid(gate)
    n_shifted = s_len + 2 * CONV_PAD - SUBLANES
    for s in range(1, SUBLANES):
        up_ref[s, 0:n_shifted, :] = up_ref[0, s:s + n_shifted, :]
    shift = CONV_PAD - CONV_W // 2
    for c in range(s_len // CONV_ROWS):
        acc = jnp.zeros((CONV_ROWS, D_CH), jnp.float32)
        for j in range(CONV_W):
            s = (j + shift) % SUBLANES
            start = c * CONV_ROWS + j + shift - s
            acc = acc + up_ref[s, start:start + CONV_ROWS, :] * w_ref[j:j + 1, :]
        u = acc + cb_ref[...]
        y = _layer_norm_rows(u, g_ref[...], b_ref[...])
        o_ref[c * CONV_ROWS:(c + 1) * CONV_ROWS, :] = (y * _sigmoid(y)).astype(o_ref.dtype)


def _conv_branch(h, w, cb, g, b, batch, seq):
    const = lambda shape: pl.BlockSpec(shape, lambda i: (0,) * len(shape))
    return pl.pallas_call(
        _conv_kernel,
        out_shape=jax.ShapeDtypeStruct((batch * seq, MIX_W), jnp.bfloat16),
        grid=(batch,),
        in_specs=[pl.BlockSpec((seq, 512), lambda i: (i, COL_D // 512)),
                  const(w.shape), const(cb.shape), const(g.shape), const(b.shape)],
        out_specs=pl.BlockSpec((seq, MIX_W), lambda i: (i, 0)),
        scratch_shapes=[pltpu.VMEM((SUBLANES, seq + 2 * CONV_PAD, D_CH), jnp.float32)],
        compiler_params=_cparams(("parallel",)),
    )(h, w, cb, g, b)


def _merge_kernel(gl_ref, ya_ref, yb_ref, yc_ref, yd_ref, x_ref, wb_ref, wo_ref, g_ref, b_ref,
                  of_ref, ob_ref):
    merged = None
    for n, y_ref in enumerate((ya_ref, yb_ref, yc_ref, yd_ref)):
        t = jnp.tanh(gl_ref[:, n * D_MODEL:(n + 1) * D_MODEL].astype(jnp.float32))
        term = (t + 1.0) * _dot(y_ref[...], wb_ref[n])
        merged = term if merged is None else merged + term
    mix = _dot(merged.astype(jnp.bfloat16), wo_ref[...])
    y = _layer_norm_rows(ALPHA * x_ref[...] + mix, g_ref[...], b_ref[...])
    of_ref[...] = y
    ob_ref[...] = y.astype(jnp.bfloat16)


def _merge(h, ya, yb, yc, yd, x, wb, wo, g, b, tm=512):
    t, d = x.shape
    row = lambda w: pl.BlockSpec((tm, w), lambda i: (i, 0))
    const = lambda shape: pl.BlockSpec(shape, lambda i: (0,) * len(shape))
    return pl.pallas_call(
        _merge_kernel,
        out_shape=(jax.ShapeDtypeStruct((t, d), jnp.float32), jax.ShapeDtypeStruct((t, d), jnp.bfloat16)),
        grid=(t // tm,),
        in_specs=[pl.BlockSpec((tm, N_BRANCH * d), lambda i: (i, COL_GATES)),
                  row(MIX_W), row(MIX_W), row(MIX_W), row(MIX_W), row(d),
                  const(wb.shape), const(wo.shape), const((1, d)), const((1, d))],
        out_specs=(row(d), row(d)),
        compiler_params=_cparams(("parallel",)),
    )(h, ya, yb, yc, yd, x, wb, wo, g.reshape(1, d), b.reshape(1, d))


def _ffn_kernel(xb_ref, xf_ref, w1_ref, w3_ref, w2_ref, g_ref, b_ref, of_ref, ob_ref):
    x = xb_ref[...]
    f = w2_ref.shape[0]
    split = (f // MXU_TILE + 1) // 2 * MXU_TILE
    acc = None
    for lo, hi in ((0, split), (split, f)):
        a = _dot(x, w1_ref[:, lo:hi])
        hid = (a * _sigmoid(a) * _dot(x, w3_ref[:, lo:hi])).astype(jnp.bfloat16)
        part = _dot(hid, w2_ref[lo:hi, :])
        acc = part if acc is None else acc + part
    y = _layer_norm_rows(ALPHA * xf_ref[...] + acc, g_ref[...], b_ref[...])
    of_ref[...] = y
    ob_ref[...] = y.astype(jnp.bfloat16)


def _ffn(xb, xf, w1, w3, w2, g, b, tm=512):
    t, d = xf.shape
    row = pl.BlockSpec((tm, d), lambda i: (i, 0))
    vec = pl.BlockSpec((1, d), lambda i: (0, 0))
    resident = lambda shape: pl.BlockSpec(shape, lambda i: (0, 0), pipeline_mode=pl.Buffered(1))
    return pl.pallas_call(
        _ffn_kernel,
        out_shape=(jax.ShapeDtypeStruct((t, d), jnp.float32), jax.ShapeDtypeStruct((t, d), jnp.bfloat16)),
        grid=(t // tm,),
        in_specs=[row, row, resident(w1.shape), resident(w3.shape), resident(w2.shape), vec, vec],
        out_specs=(row, row),
        compiler_params=_cparams(("parallel",)),
    )(xb, xf, w1, w3, w2, g.reshape(1, d), b.reshape(1, d))


def _pack_rows(x):
    half = x.shape[1] // 2
    bits = lax.bitcast_convert_type(x.astype(jnp.bfloat16).astype(jnp.float32), jnp.uint32)
    return (bits[:, :half] >> 16) | (bits[:, half:] & jnp.uint32(0xFFFF0000))


def _unpack_rows(p):
    lo = lax.bitcast_convert_type(p << 16, jnp.float32)
    hi = lax.bitcast_convert_type(p & jnp.uint32(0xFFFF0000), jnp.float32)
    return jnp.concatenate([lo, hi], axis=1)


def _router_kernel(x_ref, r_ref, tri_ref, oi_ref, ow_ref, cnt_ref, xp_ref, carry_ref):
    i = pl.program_id(0)

    @pl.when(i == 0)
    def _():
        carry_ref[...] = jnp.zeros_like(carry_ref)

    x = x_ref[...]
    r = r_ref[...]
    xh = x.astype(jnp.bfloat16)
    xl = (x - xh.astype(jnp.float32)).astype(jnp.bfloat16)
    rh = r.astype(jnp.bfloat16)
    rl = (r - rh.astype(jnp.float32)).astype(jnp.bfloat16)
    r2 = jnp.concatenate([rh, rl], axis=1)
    p = _dot(xh, r2) + _dot(xl, r2)
    logits = p[:, :LANES] + p[:, LANES:]
    tm = x.shape[0]
    lane = lax.broadcasted_iota(jnp.int32, (tm, LANES), 1)
    logits = jnp.where(lane < N_EXPERTS, logits, NEG_BIG)
    m1 = jnp.max(logits, axis=-1, keepdims=True)
    i1 = jnp.min(jnp.where(logits == m1, lane, LANES), axis=-1, keepdims=True)
    rest = jnp.where(lane == i1, NEG_BIG, logits)
    m2 = jnp.max(rest, axis=-1, keepdims=True)
    i2 = jnp.min(jnp.where(rest == m2, lane, LANES), axis=-1, keepdims=True)
    e = jnp.exp(m2 - m1)
    w1 = 1.0 / (1.0 + e)
    w2 = e / (1.0 + e)
    sel1 = lane == i1
    sel2 = lane == i2
    member = jnp.where(sel1 | sel2, 1.0, 0.0)
    before = _dot(tri_ref[...], member.astype(jnp.bfloat16)) + carry_ref[...]
    rank1 = jnp.sum(jnp.where(sel1, before, 0.0), axis=-1, keepdims=True).astype(jnp.int32)
    rank2 = jnp.sum(jnp.where(sel2, before, 0.0), axis=-1, keepdims=True).astype(jnp.int32)
    carry_ref[...] += jnp.sum(member, axis=0, keepdims=True)
    oi = jnp.where(lane == 0, i1, jnp.where(lane == 1, i2, jnp.where(lane == 2, rank1, rank2)))
    oi_ref[...] = oi.T[0:SUBLANES, :]
    ow_ref[...] = jnp.where(lane == 0, w1, w2)
    cnt_ref[...] = carry_ref[...]
    xp_ref[...] = _pack_rows(x)


def _router(xf, router, tm=512):
    t, d = xf.shape
    r_pad = jnp.zeros((d, LANES), jnp.float32).at[:, :N_EXPERTS].set(router.astype(jnp.float32))
    tri = jnp.asarray(np.tril(np.ones((tm, tm), np.float32), -1), jnp.bfloat16)
    return pl.pallas_call(
        _router_kernel,
        out_shape=(jax.ShapeDtypeStruct((SUBLANES, t), jnp.int32),
                   jax.ShapeDtypeStruct((t, LANES), jnp.float32),
                   jax.ShapeDtypeStruct((1, LANES), jnp.float32),
                   jax.ShapeDtypeStruct((t, d // 2), jnp.uint32)),
        grid=(t // tm,),
        in_specs=[pl.BlockSpec((tm, d), lambda i: (i, 0)),
                  pl.BlockSpec((d, LANES), lambda i: (0, 0)),
                  pl.BlockSpec((tm, tm), lambda i: (0, 0))],
        out_specs=(pl.BlockSpec((SUBLANES, tm), lambda i: (0, i)),
                   pl.BlockSpec((tm, LANES), lambda i: (i, 0)),
                   pl.BlockSpec((1, LANES), lambda i: (0, 0)),
                   pl.BlockSpec((tm, d // 2), lambda i: (i, 0))),
        scratch_shapes=[pltpu.VMEM((1, LANES), jnp.float32)],
        compiler_params=_cparams(("arbitrary",)),
    )(xf, r_pad, tri)


SC_CORES = 2
SC_SUBCORES = 16
SC_WINDOW = 64


def _sc_gather_rows(table, idx):
    m = idx.shape[0]
    d = table.shape[1]
    n_workers = SC_CORES * SC_SUBCORES
    per_w = m // n_workers
    n_win = per_w // SC_WINDOW
    assert per_w * n_workers == m and n_win * SC_WINDOW == per_w
    mesh = plsc.VectorSubcoreMesh(core_axis_name="c", subcore_axis_name="s")

    @functools.partial(
        pl.kernel, mesh=mesh,
        out_type=jax.ShapeDtypeStruct((m, d), table.dtype),
        scratch_types=[pltpu.VMEM((per_w,), jnp.int32),
                       pltpu.VMEM((SC_WINDOW, d), table.dtype),
                       pltpu.SemaphoreType.DMA],
    )
    def gather(table_hbm, idx_hbm, out_hbm, idx_v, rows_v, sem):
        wid = lax.axis_index("s") * SC_CORES + lax.axis_index("c")
        base = wid * per_w
        pltpu.sync_copy(idx_hbm.at[pl.ds(base, per_w)], idx_v)

        @pl.loop(0, n_win)
        def _(w):
            off = pl.multiple_of(w * SC_WINDOW, SC_WINDOW)
            pltpu.async_copy(table_hbm.at[idx_v.at[pl.ds(off, SC_WINDOW)]], rows_v, sem).wait()
            pltpu.sync_copy(rows_v, out_hbm.at[pl.ds(base + off, SC_WINDOW)])

    return gather(table, idx)


def _sc_scatter_rows(rows, idx, n_out):
    n_copies, m = idx.shape
    d = rows.shape[1]
    n_workers = SC_CORES * SC_SUBCORES
    per_w = m // n_workers
    n_win = per_w // SC_WINDOW
    assert per_w * n_workers == m and n_win * SC_WINDOW == per_w
    idx_w = idx.reshape(n_copies, n_workers, n_win, SC_WINDOW).transpose(1, 0, 2, 3)
    idx_w = idx_w.reshape(n_workers, n_copies * n_win, SC_WINDOW)
    mesh = plsc.VectorSubcoreMesh(core_axis_name="c", subcore_axis_name="s")

    @functools.partial(
        pl.kernel, mesh=mesh,
        out_type=jax.ShapeDtypeStruct((n_out, d), rows.dtype),
        scratch_types=[pltpu.VMEM((n_copies * n_win, SC_WINDOW), jnp.int32),
                       pltpu.VMEM((SC_WINDOW, d), rows.dtype)],
    )
    def scatter(rows_hbm, idx_hbm, out_hbm, idx_v, rows_v):
        wid = lax.axis_index("s") * SC_CORES + lax.axis_index("c")
        base = wid * per_w
        pltpu.sync_copy(idx_hbm.at[wid], idx_v)

        @pl.loop(0, n_win)
        def _(w):
            off = pl.multiple_of(w * SC_WINDOW, SC_WINDOW)
            pltpu.sync_copy(rows_hbm.at[pl.ds(base + off, SC_WINDOW)], rows_v)
            for k in range(n_copies):
                pltpu.sync_copy(rows_v, out_hbm.at[idx_v.at[k * n_win + w]])

    return scatter(rows, idx_w)


def _expert_kernel(te_ref, nv_ref, x_ref, w1_ref, w3_ref, w2_ref, o_ref,
                   w13s_ref, w2s_ref, xb_ref, acc_ref):
    i = pl.program_id(0)
    j = pl.program_id(1)
    last = pl.num_programs(1) - 1
    n_valid = nv_ref[i]
    n_piece = MOE_TILE // MOE_SUB
    piece_rows = [pl.ds(s * MOE_SUB, MOE_SUB) for s in range(n_piece)]

    for s, rows in enumerate(piece_rows):
        @pl.when((n_valid > s * MOE_SUB) & (j == 0))
        def _():
            row = lax.broadcasted_iota(jnp.int32, (MOE_SUB, 1), 0) + s * MOE_SUB
            x = jnp.where(row < n_valid, _unpack_rows(x_ref[rows, :]), 0.0)
            xb_ref[rows, :] = x.astype(jnp.bfloat16)
            acc_ref[rows, :] = jnp.zeros((MOE_SUB, acc_ref.shape[1]), jnp.float32)

    def compute(n_live):
        w13s_ref[:, 0:MOE_FCHUNK] = w1_ref[0].astype(jnp.bfloat16)
        w13s_ref[:, MOE_FCHUNK:2 * MOE_FCHUNK] = w3_ref[0].astype(jnp.bfloat16)
        w2s_ref[...] = w2_ref[0].astype(jnp.bfloat16)
        for rows in piece_rows[:n_live]:
            ab = _dot(xb_ref[rows, :], w13s_ref[...])
            a = ab[:, :MOE_FCHUNK]
            hid = (a * _sigmoid(a) * ab[:, MOE_FCHUNK:]).astype(jnp.bfloat16)
            acc_ref[rows, :] += _dot(hid, w2s_ref[...])

    for n_live in range(1, n_piece + 1):
        upper = n_live * MOE_SUB if n_live < n_piece else MOE_TILE
        pl.when((n_valid > (n_live - 1) * MOE_SUB) & (n_valid <= upper))(functools.partial(compute, n_live))

    for s, rows in enumerate(piece_rows):
        @pl.when((n_valid > s * MOE_SUB) & (j == last))
        def _():
            o_ref[rows, :] = _pack_rows(acc_ref[rows, :])

        @pl.when((n_valid <= s * MOE_SUB) & (j == last))
        def _():
            o_ref[rows, :] = jnp.zeros((MOE_SUB, o_ref.shape[1]), o_ref.dtype)


def _experts(tile_expert, tile_valid, xs, w1, w3, w2):
    n_rows = xs.shape[0]
    d, f = w1.shape[1], w1.shape[2]
    nj = f // MOE_FCHUNK

    def chunk(i, j, nv):
        return jnp.where(nv[i] > 0, j, nj - 1)

    return pl.pallas_call(
        _expert_kernel,
        out_shape=jax.ShapeDtypeStruct((n_rows, d // 2), jnp.uint32),
        grid_spec=pltpu.PrefetchScalarGridSpec(
            num_scalar_prefetch=2,
            grid=(n_rows // MOE_TILE, nj),
            in_specs=[pl.BlockSpec((MOE_TILE, d // 2), lambda i, j, te, nv: (i, 0)),
                      pl.BlockSpec((1, d, MOE_FCHUNK), lambda i, j, te, nv: (te[i], 0, chunk(i, j, nv))),
                      pl.BlockSpec((1, d, MOE_FCHUNK), lambda i, j, te, nv: (te[i], 0, chunk(i, j, nv))),
                      pl.BlockSpec((1, MOE_FCHUNK, d), lambda i, j, te, nv: (te[i], chunk(i, j, nv), 0))],
            out_specs=pl.BlockSpec((MOE_TILE, d // 2), lambda i, j, te, nv: (i, 0)),
            scratch_shapes=[pltpu.VMEM((d, 2 * MOE_FCHUNK), jnp.bfloat16),
                            pltpu.VMEM((MOE_FCHUNK, d), jnp.bfloat16),
                            pltpu.VMEM((MOE_TILE, d), jnp.bfloat16),
                            pltpu.VMEM((MOE_TILE, d), jnp.float32)],
        ),
        compiler_params=pltpu.CompilerParams(dimension_semantics=("arbitrary", "arbitrary"),
                                             vmem_limit_bytes=MOE_VMEM_LIMIT),
    )(tile_expert, tile_valid, xs, w1, w3, w2)


def _combine_kernel(ya_ref, yb_ref, x_ref, w_ref, g_ref, b_ref, o_ref):
    w = w_ref[...]
    f = w[:, 0:1] * _unpack_rows(ya_ref[...]) + w[:, 1:2] * _unpack_rows(yb_ref[...])
    o_ref[...] = _layer_norm_rows(ALPHA * x_ref[...] + f, g_ref[...], b_ref[...])


def _combine(yg, xf, wts, g, b, tm=512):
    t, d = xf.shape
    nt = t // tm
    return pl.pallas_call(
        _combine_kernel,
        out_shape=jax.ShapeDtypeStruct((t, d), jnp.float32),
        grid=(nt,),
        in_specs=[pl.BlockSpec((tm, d // 2), lambda i: (i, 0)),
                  pl.BlockSpec((tm, d // 2), lambda i: (i + nt, 0)),
                  pl.BlockSpec((tm, d), lambda i: (i, 0)),
                  pl.BlockSpec((tm, LANES), lambda i: (i, 0)),
                  pl.BlockSpec((1, d), lambda i: (0, 0)),
                  pl.BlockSpec((1, d), lambda i: (0, 0))],
        out_specs=pl.BlockSpec((tm, d), lambda i: (i, 0)),
        compiler_params=_cparams(("parallel",)),
    )(yg, yg, xf, wts, g.reshape(1, d), b.reshape(1, d))


def _moe_ffn(xf, router, w1, w3, w2, g, b):
    t, d = xf.shape
    oi, ow, cnt, xp = _router(xf, router)
    e1, e2, rank1, rank2 = oi[0], oi[1], oi[2], oi[3]
    counts = cnt[0, :N_EXPERTS].astype(jnp.int32)
    tiles = (counts + MOE_TILE - 1) // MOE_TILE
    tile_end = jnp.cumsum(tiles)
    tile_start = tile_end - tiles
    offs = tile_start * MOE_TILE
    dest = jnp.concatenate([offs[e1] + rank1, offs[e2] + rank2])
    n_rows = TOP_K * t + N_EXPERTS * MOE_TILE
    n_tiles = n_rows // MOE_TILE
    tile_id = jnp.arange(n_tiles, dtype=jnp.int32)
    tile_expert = jnp.sum(tile_id[:, None] >= tile_end[None, :], axis=1)
    used = tile_expert < N_EXPERTS
    last_expert = jnp.sum(tile_end[-1] - 1 >= tile_end)
    tile_expert = jnp.where(used, tile_expert, last_expert).astype(jnp.int32)
    tile_valid = jnp.clip(counts[tile_expert] - (tile_id - tile_start[tile_expert]) * MOE_TILE, 0, MOE_TILE)
    tile_valid = jnp.where(used, tile_valid, 0).astype(jnp.int32)
    xs = _sc_scatter_rows(xp, dest.reshape(TOP_K, t), n_rows)
    y = _experts(tile_expert, tile_valid, xs, w1, w3, w2)
    yg = _sc_gather_rows(y, dest)
    return _combine(yg, xf, ow, g, b)


def _prep_layer(l, p, in_cols):
    bf = jnp.bfloat16
    out = {}
    n_gate = N_BRANCH * D_MODEL
    w_l = p["w_in"][l]
    gate_cols = (0.5 * w_l[:, int(in_cols[0]):int(in_cols[0]) + n_gate]).astype(bf)
    out["w_in"] = jnp.concatenate([gate_cols, _take_cols(w_l, in_cols[n_gate:]).astype(bf)], axis=1)
    deint = np.concatenate([np.arange(0, HEAD_DIM, 2), np.arange(1, HEAD_DIM, 2)])
    out["gq_a"] = jnp.tile(p["a_q_norm"][l][deint], 2).reshape(1, LANES)
    out["gk_a"] = jnp.tile(p["a_k_norm"][l][deint], 2).reshape(1, LANES)
    per_head = C_NOPE + C_ROPE
    uq_cols = []
    for h in range(C_HEADS):
        base = h * per_head
        uq_cols += [base + np.arange(C_NOPE), base + C_NOPE + np.arange(0, C_ROPE, 2),
                    base + C_NOPE + np.arange(1, C_ROPE, 2), np.full(32, -1)]
    wuq = _take_cols(p["c_w_uq"][l], np.concatenate(uq_cols))
    out["wuq"] = jnp.concatenate([wuq, jnp.zeros((64, wuq.shape[1]), wuq.dtype)], axis=0).astype(bf)
    uk_cols, uv_cols = [], []
    for h in range(C_HEADS):
        base = h * (C_NOPE + C_V)
        uk_cols += [base + np.arange(C_NOPE), np.full(64, -1)]
        uv_cols += [base + C_NOPE + np.arange(C_V)]
    out["wuk"] = _take_cols(p["c_w_ukv"][l], np.concatenate(uk_cols)).astype(bf)
    out["wuv"] = _take_cols(p["c_w_ukv"][l], np.concatenate(uv_cols)).astype(bf)
    out["gq_c"] = jnp.concatenate([p["c_q_norm"][l], jnp.zeros((64,), jnp.float32)]).reshape(1, 256)
    out["gkv_c"] = p["c_kv_norm"][l].reshape(1, C_KV_RANK)
    out["conv_w"] = jnp.concatenate([p["d_conv_w"][l][:, 0, :], jnp.zeros((1, D_CH), jnp.float32)], axis=0)
    out["conv_b"] = p["d_conv_b"][l].reshape(1, D_CH)
    out["d_ln_g"] = p["d_ln_g"][l].reshape(1, D_CH)
    out["d_ln_b"] = p["d_ln_b"][l].reshape(1, D_CH)
    a_rows = np.concatenate([(kvh * 2 + g) * HEAD_DIM + np.arange(HEAD_DIM)
                             for g in range(2) for kvh in range(A_KV_HEADS)])
    wb = p["w_branch"][l]
    out["w_branch"] = (0.5 * jnp.stack([wb[0][a_rows], wb[1], wb[2], wb[3]], axis=0)).astype(bf)
    out["w_out"] = p["w_out"][l].astype(bf)
    return out


def kernel(x, ln_in_g, ln_in_b, w_in, a_q_norm, a_k_norm, b_rpb, c_q_norm, c_kv_norm, c_w_uq, c_w_ukv,
           d_conv_w, d_conv_b, d_ln_g, d_ln_b, w_branch, w_out, ln_mix_g, ln_mix_b,
           ffn_w1, ffn_w3, ffn_w2, moe_router, moe_w1, moe_w3, moe_w2, ln_ffn_g, ln_ffn_b):
    batch, seq, d = x.shape
    t = batch * seq
    bf = jnp.bfloat16
    params = dict(w_in=w_in, a_q_norm=a_q_norm, a_k_norm=a_k_norm, c_q_norm=c_q_norm, c_kv_norm=c_kv_norm,
                  c_w_uq=c_w_uq, c_w_ukv=c_w_ukv, d_conv_w=d_conv_w, d_conv_b=d_conv_b, d_ln_g=d_ln_g,
                  d_ln_b=d_ln_b, w_branch=w_branch, w_out=w_out)
    in_cols = _in_proj_columns()
    cos_a, sin_a, cos_c, sin_c = _rope_tables(seq)
    xf = xb = None
    for l in range(DEPTH):
        p = _prep_layer(l, params, in_cols)
        if l == 0:
            h, xf = _in_proj(x.reshape(t, d), p["w_in"], ln=(ln_in_g, ln_in_b))
        else:
            h = _in_proj(xb, p["w_in"])
        ya = _attn_a(h, cos_a, sin_a, p["gq_a"], p["gk_a"], batch, seq)
        yb = _attn_b(h, _na_bias_tables(b_rpb[l], seq // GRID_W), batch, seq)
        yc = _attn_c(h, p["wuq"], p["wuk"], p["wuv"], p["gq_c"], p["gkv_c"], cos_c, sin_c, batch, seq)
        yd = _conv_branch(h, p["conv_w"], p["conv_b"], p["d_ln_g"], p["d_ln_b"], batch, seq)
        xf, xb = _merge(h, ya, yb, yc, yd, xf, p["w_branch"], p["w_out"], ln_mix_g[l], ln_mix_b[l])
        if l % 2 == 0:
            i = l // 2
            xf, xb = _ffn(xb, xf, ffn_w1[i].astype(bf), ffn_w3[i].astype(bf), ffn_w2[i].astype(bf),
                          ln_ffn_g[l], ln_ffn_b[l])
        else:
            i = l // 2
            xf = _moe_ffn(xf, moe_router[i], moe_w1[i], moe_w3[i], moe_w2[i], ln_ffn_g[l], ln_ffn_b[l])
            xb = xf.astype(bf)
    return xf.reshape(batch, seq, d)
```

```python
import functools

import numpy as np
import jax
import jax.numpy as jnp
from jax import lax
from jax.experimental import pallas as pl
from jax.experimental.pallas import tpu as pltpu
from jax.experimental.pallas import tpu_sc as plsc

D_MODEL = 1024
DEPTH = 2
GRID_W = 64
HEAD_DIM = 64
ROPE_THETA = 10000.0
RMS_EPS = 1e-6
LN_EPS = 1e-5
A_HEADS = 4
A_KV_HEADS = 2
B_HEADS = 4
NA_WIN_H = 8
NA_WIN_W = 16
C_HEADS = 4
C_NOPE = 64
C_ROPE = 32
C_V = 64
C_Q_RANK = 192
C_KV_RANK = 128
D_CH = 256
CONV_W = 31
N_BRANCH = 4
MIX_W = 256
D_FF = 2816
N_EXPERTS = 8
TOP_K = 2
D_FF_EXPERT = 3584
ALPHA = (2 * DEPTH) ** 0.25

LANES = 128
SUBLANES = 8
MXU_TILE = 256
VMEM_LIMIT = 48 * 1024 * 1024

LOG2E = 1.4426950408889634
NEG_BIG = -1e30

COL_GATES = 0
COL_AQ = 4096
COL_AKV = 4352
COL_C = 4608
COL_D = 5120
COL_BQ = 5632
COL_BK = 5888
COL_BV = 6144
D_IN_PAD = 6400

NA_QROWS = 4
NA_KROWS = 12

MOE_TILE = 2048
MOE_SUB = 512
MOE_FCHUNK = 512
MOE_VMEM_LIMIT = 56 * 1024 * 1024


def _in_proj_columns():
    o_aq, o_ak, o_av = 0, 256, 384
    o_bq, o_bk, o_bv = 512, 768, 1024
    o_cq, o_ckv, o_kpe = 1280, 1472, 1600
    o_d, o_g = 1632, 2144
    deint = np.concatenate([np.arange(0, HEAD_DIM, 2), np.arange(1, HEAD_DIM, 2)])
    cols = [o_g + np.arange(N_BRANCH * D_MODEL)]
    for g in range(2):
        for kvh in range(A_KV_HEADS):
            cols.append(o_aq + (kvh * 2 + g) * HEAD_DIM + deint)
    for kvh in range(A_KV_HEADS):
        cols.append(o_ak + kvh * HEAD_DIM + deint)
    cols.append(o_av + np.arange(A_KV_HEADS * HEAD_DIM))
    cols.append(o_cq + np.arange(C_Q_RANK))
    cols.append(np.full(64, -1))
    cols.append(o_ckv + np.arange(C_KV_RANK))
    cols.append(np.full(64, -1))
    cols.append(o_kpe + np.arange(0, C_ROPE, 2))
    cols.append(o_kpe + np.arange(1, C_ROPE, 2))
    cols.append(np.full(32, -1))
    cols.append(o_d + np.arange(2 * D_CH))
    cols.append(o_bq + np.arange(3 * B_HEADS * HEAD_DIM))
    cols = np.concatenate(cols)
    assert cols.shape[0] == D_IN_PAD
    return cols


def _take_cols(w, cols):
    cols = np.asarray(cols)
    runs, i, n = [], 0, len(cols)
    while i < n:
        j = i + 1
        if cols[i] < 0:
            while j < n and cols[j] < 0:
                j += 1
            kind = "zero"
        else:
            while j < n and cols[j] == cols[j - 1] + 1:
                j += 1
            kind = "slice" if j - i >= 64 else "shuffle"
        if kind == "shuffle" and runs and runs[-1][0] == "shuffle":
            runs[-1] = ("shuffle", runs[-1][1], j)
        else:
            runs.append((kind, i, j))
        i = j
    parts = []
    for kind, i, j in runs:
        if kind == "zero":
            parts.append(jnp.zeros((w.shape[0], j - i), w.dtype))
        elif kind == "slice":
            parts.append(w[:, int(cols[i]):int(cols[i]) + j - i])
        else:
            lo, hi = int(cols[i:j].min()), int(cols[i:j].max()) + 1
            onehot = np.zeros((hi - lo, j - i), np.float32)
            onehot[cols[i:j] - lo, np.arange(j - i)] = 1.0
            parts.append(jnp.dot(w[:, lo:hi], jnp.asarray(onehot, w.dtype), precision=lax.Precision.HIGHEST))
    return jnp.concatenate(parts, axis=1)


def _rope_tables(seq):
    t = np.arange(seq)
    row = (t // GRID_W).astype(np.float32)
    col = (t % GRID_W).astype(np.float32)

    def angles(rot_dim):
        n_freq = rot_dim // 4
        inv = jnp.asarray(ROPE_THETA, jnp.float32) ** (-jnp.arange(n_freq, dtype=jnp.float32) / n_freq)
        return jnp.concatenate([jnp.asarray(row)[:, None] * inv, jnp.asarray(col)[:, None] * inv], axis=-1)

    ang_a = angles(HEAD_DIM)
    ca, sa = jnp.cos(ang_a), jnp.sin(ang_a)
    cos_a = jnp.concatenate([ca, ca, ca, ca], axis=-1)
    sin_a = jnp.concatenate([-sa, sa, -sa, sa], axis=-1)
    ang_c = angles(C_ROPE)
    cc, sc = jnp.cos(ang_c), jnp.sin(ang_c)
    one = jnp.ones((seq, 64), jnp.float32)
    zero = jnp.zeros((seq, 64), jnp.float32)
    cos_c = jnp.concatenate([one, cc, cc, one[:, :32]], axis=-1)
    sin_c = jnp.concatenate([zero, -sc, sc, zero[:, :32]], axis=-1)
    return cos_a, sin_a, cos_c, sin_c


def _na_bias_tables(rpb, rows):
    wh = min(NA_WIN_H, rows)
    n_dr, n_dc = 2 * NA_WIN_H - 1, 2 * NA_WIN_W - 1
    qc = np.arange(GRID_W)[:, None]
    kc = np.arange(GRID_W)[None, :]
    col_start = np.clip(qc - NA_WIN_W // 2, 0, GRID_W - NA_WIN_W)
    ok_c = (kc >= col_start) & (kc < col_start + NA_WIN_W)
    dc = np.clip(kc - qc + (NA_WIN_W - 1), 0, n_dc - 1)
    oh_c = np.eye(n_dc, dtype=np.float32)[dc.reshape(-1)]
    oh_r, ok_r = [], []
    for r0 in (0, NA_QROWS, rows - NA_QROWS):
        ws = int(np.clip(r0 - 4, 0, rows - NA_KROWS))
        qr = r0 + np.arange(NA_QROWS)[:, None]
        kr = ws + np.arange(NA_KROWS)[None, :]
        row_start = np.clip(qr - wh // 2, 0, rows - wh)
        ok_r.append((kr >= row_start) & (kr < row_start + wh))
        dr = np.clip(kr - qr + (NA_WIN_H - 1), 0, n_dr - 1)
        oh_r.append(np.eye(n_dr, dtype=np.float32)[dr.reshape(-1)])
    oh_r = np.concatenate(oh_r, axis=0)
    ok = np.stack(ok_r)[:, :, None, :, None] & ok_c[None, None, :, None, :]
    ok = ok.reshape(3, 1, NA_QROWS * GRID_W, NA_KROWS * GRID_W)
    hi = lax.Precision.HIGHEST
    t1 = jnp.einsum("ma,lhab->lhmb", jnp.asarray(oh_r), rpb.astype(jnp.float32) * LOG2E, precision=hi)
    t2 = jnp.einsum("lhmb,nb->lhmn", t1, jnp.asarray(oh_c), precision=hi).astype(jnp.bfloat16)
    nl, nh = rpb.shape[0], rpb.shape[1]
    t2 = t2.reshape(nl, nh, 3, NA_QROWS, NA_KROWS, GRID_W, GRID_W).transpose(0, 2, 1, 3, 5, 4, 6)
    bias = t2.reshape(nl, 3, nh, NA_QROWS * GRID_W, NA_KROWS * GRID_W)
    return jnp.where(jnp.asarray(ok), bias, jnp.asarray(NEG_BIG, jnp.bfloat16))


def _cparams(sem):
    return pltpu.CompilerParams(dimension_semantics=sem, vmem_limit_bytes=VMEM_LIMIT)


def _layer_norm_rows(z, g, b):
    mu = jnp.mean(z, axis=-1, keepdims=True)
    zc = z - mu
    var = jnp.mean(zc * zc, axis=-1, keepdims=True)
    return zc * lax.rsqrt(var + LN_EPS) * g + b


def _sigmoid(x):
    return 0.5 * jnp.tanh(0.5 * x) + 0.5


def _dot(a, b):
    return jnp.dot(a, b, preferred_element_type=jnp.float32)


def _dot_nt(a, b):
    return lax.dot_general(a, b, (((1,), (1,)), ((), ())), preferred_element_type=jnp.float32)


IN_PROJ_CHUNK = 1280


def _project_chunks(xb, w_ref, h_ref):
    for c in range(w_ref.shape[1] // IN_PROJ_CHUNK):
        cols = slice(c * IN_PROJ_CHUNK, (c + 1) * IN_PROJ_CHUNK)
        h_ref[:, cols] = _dot(xb, w_ref[:, cols]).astype(h_ref.dtype)


def _in_proj_kernel(x_ref, w_ref, h_ref):
    _project_chunks(x_ref[...], w_ref, h_ref)


def _in_proj_ln_kernel(x_ref, g_ref, b_ref, w_ref, h_ref, xf_ref):
    y = _layer_norm_rows(x_ref[...], g_ref[...], b_ref[...])
    xf_ref[...] = y
    _project_chunks(y.astype(jnp.bfloat16), w_ref, h_ref)


def _in_proj(x, w, ln=None, tm=512):
    t, d = x.shape
    n = w.shape[1]
    row = lambda width: pl.BlockSpec((tm, width), lambda i: (i, 0))
    vec = pl.BlockSpec((1, d), lambda i: (0, 0))
    w_spec = pl.BlockSpec((d, n), lambda i: (0, 0), pipeline_mode=pl.Buffered(1))
    h_shape = jax.ShapeDtypeStruct((t, n), jnp.bfloat16)
    if ln is None:
        return pl.pallas_call(
            _in_proj_kernel, out_shape=h_shape, grid=(t // tm,),
            in_specs=[row(d), w_spec], out_specs=row(n),
            compiler_params=_cparams(("parallel",)),
        )(x, w)
    g, b = ln
    return pl.pallas_call(
        _in_proj_ln_kernel,
        out_shape=(h_shape, jax.ShapeDtypeStruct((t, d), jnp.float32)),
        grid=(t // tm,),
        in_specs=[row(d), vec, vec, w_spec], out_specs=(row(n), row(d)),
        compiler_params=_cparams(("parallel",)),
    )(x, g.reshape(1, d), b.reshape(1, d), w)


def _ones_halves(vblk):
    lane = lax.broadcasted_iota(jnp.int32, vblk.shape, 1)
    ones = jnp.ones_like(vblk)
    return jnp.where(lane < HEAD_DIM, vblk, ones), jnp.where(lane < HEAD_DIM, ones, vblk)


def _attend_pair(s_lo, s_hi, v_lo, v_hi, lo_q):
    outs = []
    for s, v in ((s_lo, v_lo), (s_hi, v_hi)):
        p = jnp.exp2(s - jnp.max(s, axis=-1, keepdims=True)).astype(jnp.bfloat16)
        outs.append(_dot(p, v))
    num = jnp.where(lo_q, outs[0], outs[1])
    den = jnp.where(lo_q, pltpu.roll(outs[0], HEAD_DIM, 1), pltpu.roll(outs[1], HEAD_DIM, 1))
    return num / den


def _rotate_pairs(x, cos, sin, half, lane):
    w = x.shape[-1]
    first = (lane % (2 * half)) < half
    partner = jnp.where(first, pltpu.roll(x, w - half, 1), pltpu.roll(x, half, 1))
    return x * cos + partner * sin


def _attn_a_kernel(q_ref, kv_ref, cos_ref, sin_ref, gq_ref, gk_ref, o_ref, ks_ref, va_ref, *, tq):
    s_len = q_ref.shape[0]

    def prep(x, g, cos, sin):
        lane = lax.broadcasted_iota(jnp.int32, x.shape, 1)
        lo = lane < HEAD_DIM
        x2 = x * x
        ss_lo = jnp.sum(jnp.where(lo, x2, 0.0), axis=-1, keepdims=True)
        ss_hi = jnp.sum(jnp.where(lo, 0.0, x2), axis=-1, keepdims=True)
        ms = jnp.where(lo, ss_lo, ss_hi) * (1.0 / HEAD_DIM)
        xn = x * lax.rsqrt(ms + RMS_EPS) * g
        return _rotate_pairs(xn, cos, sin, HEAD_DIM // 2, lane)

    ks_ref[...] = prep(kv_ref[:, 0:LANES].astype(jnp.float32), gk_ref[...],
                       cos_ref[...], sin_ref[...]).astype(jnp.bfloat16)
    scale = HEAD_DIM ** -0.5 * LOG2E
    lane_q = lax.broadcasted_iota(jnp.int32, (tq, LANES), 1)
    lo_q = lane_q < HEAD_DIM

    va_ref[0], va_ref[1] = _ones_halves(kv_ref[:, LANES:2 * LANES])

    def body(i, carry):
        r = pl.multiple_of(i * tq, tq)
        k = ks_ref[...]
        cos = cos_ref[pl.ds(r, tq), :]
        sin = sin_ref[pl.ds(r, tq), :]
        for g in range(2):
            qt = prep(q_ref[pl.ds(r, tq), g * LANES:(g + 1) * LANES].astype(jnp.float32), gq_ref[...], cos, sin)
            qt = (qt * scale).astype(jnp.bfloat16)
            zero = jnp.zeros_like(qt)
            out = _attend_pair(_dot_nt(jnp.where(lo_q, qt, zero), k), _dot_nt(jnp.where(lo_q, zero, qt), k),
                               va_ref[0], va_ref[1], lo_q)
            o_ref[pl.ds(r, tq), g * LANES:(g + 1) * LANES] = out.astype(o_ref.dtype)
        return carry

    lax.fori_loop(0, s_len // tq, body, 0, unroll=4)


def _attn_a(h, cos_a, sin_a, gq, gk, batch, seq, tq=256):
    kern = functools.partial(_attn_a_kernel, tq=tq)
    return pl.pallas_call(
        kern,
        out_shape=jax.ShapeDtypeStruct((batch * seq, MIX_W), jnp.bfloat16),
        grid=(batch,),
        in_specs=[pl.BlockSpec((seq, 256), lambda b: (b, COL_AQ // 256)),
                  pl.BlockSpec((seq, 256), lambda b: (b, COL_AKV // 256)),
                  pl.BlockSpec((seq, LANES), lambda b: (0, 0)),
                  pl.BlockSpec((seq, LANES), lambda b: (0, 0)),
                  pl.BlockSpec((1, LANES), lambda b: (0, 0)),
                  pl.BlockSpec((1, LANES), lambda b: (0, 0))],
        out_specs=pl.BlockSpec((seq, MIX_W), lambda b: (b, 0)),
        scratch_shapes=[pltpu.VMEM((seq, LANES), jnp.bfloat16), pltpu.VMEM((2, seq, LANES), jnp.bfloat16)],
        compiler_params=_cparams(("parallel",)),
    )(h, h, cos_a, sin_a, gq, gk)


def _attn_b_kernel(q_ref, k_ref, v_ref, bias_ref, o_ref, va_ref, *, rows):
    nq = NA_QROWS * GRID_W
    nk = NA_KROWS * GRID_W
    lane_q = lax.broadcasted_iota(jnp.int32, (nq, LANES), 1)
    lo_q = lane_q < HEAD_DIM
    n_groups = rows // NA_QROWS
    scale = HEAD_DIM ** -0.5 * LOG2E
    for blk in range(B_HEADS // 2):
        va_ref[2 * blk], va_ref[2 * blk + 1] = _ones_halves(v_ref[:, blk * LANES:(blk + 1) * LANES])

    def body(gi, carry):
        ws = jnp.clip(gi * NA_QROWS - 4, 0, rows - NA_KROWS)
        cls = jnp.where(gi == 0, 0, jnp.where(gi == n_groups - 1, 2, 1))
        qr = pl.multiple_of(gi * nq, nq)
        kr = pl.multiple_of(ws * GRID_W, GRID_W)
        for blk in range(B_HEADS // 2):
            sl = slice(blk * LANES, (blk + 1) * LANES)
            qt = q_ref[pl.ds(qr, nq), sl] * scale
            kw = k_ref[pl.ds(kr, nk), sl]
            zero = jnp.zeros_like(qt)
            s_lo = _dot_nt(jnp.where(lo_q, qt, zero), kw) + bias_ref[cls, 2 * blk].astype(jnp.float32)
            s_hi = _dot_nt(jnp.where(lo_q, zero, qt), kw) + bias_ref[cls, 2 * blk + 1].astype(jnp.float32)
            out = _attend_pair(s_lo, s_hi, va_ref[2 * blk, pl.ds(kr, nk), :], va_ref[2 * blk + 1, pl.ds(kr, nk), :], lo_q)
            o_ref[pl.ds(qr, nq), sl] = out.astype(o_ref.dtype)
        return carry

    lax.fori_loop(0, n_groups, body, 0, unroll=4)


def _attn_b(h, bias, layer, batch, seq):
    rows = seq // GRID_W
    kern = functools.partial(_attn_b_kernel, rows=rows)
    return pl.pallas_call(
        kern,
        out_shape=jax.ShapeDtypeStruct((batch * seq, MIX_W), jnp.bfloat16),
        grid=(batch,),
        in_specs=[pl.BlockSpec((seq, 256), lambda b: (b, COL_BQ // 256)),
                  pl.BlockSpec((seq, 256), lambda b: (b, COL_BK // 256)),
                  pl.BlockSpec((seq, 256), lambda b: (b, COL_BV // 256)),
                  pl.BlockSpec((None,) + bias.shape[1:], lambda b: (layer, 0, 0, 0, 0))],
        out_specs=pl.BlockSpec((seq, MIX_W), lambda b: (b, 0)),
        scratch_shapes=[pltpu.VMEM((B_HEADS, seq, LANES), jnp.bfloat16)],
        compiler_params=_cparams(("parallel",)),
    )(h, h, h, bias)


def _attn_c_kernel(hc_ref, wuq_ref, wuk_ref, wuv_ref, gq_ref, gkv_ref, cos_ref, sin_ref, o_ref,
                   qs_ref, ks_ref, vs_ref, *, tq):
    s_len = hc_ref.shape[0]
    lane = lax.broadcasted_iota(jnp.int32, (s_len, LANES), 1)
    cos = cos_ref[...]
    sin = sin_ref[...]

    cq = hc_ref[:, 0:256].astype(jnp.float32)
    ms = jnp.sum(cq * cq, axis=-1, keepdims=True) * (1.0 / C_Q_RANK)
    cqn = (cq * lax.rsqrt(ms + RMS_EPS) * gq_ref[...]).astype(jnp.bfloat16)
    ckv = hc_ref[:, 256:384].astype(jnp.float32)
    ms = jnp.mean(ckv * ckv, axis=-1, keepdims=True)
    kvn = (ckv * lax.rsqrt(ms + RMS_EPS) * gkv_ref[...]).astype(jnp.bfloat16)
    kpe = _rotate_pairs(hc_ref[:, 384:512].astype(jnp.float32), cos, sin, C_ROPE // 2, lane)
    v_all = _dot(kvn, wuv_ref[...]).astype(jnp.bfloat16)
    for blk in range(C_HEADS // 2):
        vs_ref[2 * blk], vs_ref[2 * blk + 1] = _ones_halves(v_all[:, blk * LANES:(blk + 1) * LANES])
    scale = (C_NOPE + C_ROPE) ** -0.5 * LOG2E
    for h in range(C_HEADS):
        sl = slice(h * LANES, (h + 1) * LANES)
        qh = _rotate_pairs(_dot(cqn, wuq_ref[:, sl]), cos, sin, C_ROPE // 2, lane) * scale
        qs_ref[:, sl] = qh.astype(jnp.bfloat16)
        ks_ref[:, sl] = (_dot(kvn, wuk_ref[:, sl]) + kpe).astype(jnp.bfloat16)

    lane_q = lax.broadcasted_iota(jnp.int32, (tq, LANES), 1)
    lo_q = lane_q < C_V

    def body(i, carry):
        r = pl.multiple_of(i * tq, tq)
        for blk in range(C_HEADS // 2):
            s = []
            for par in range(2):
                sl = slice((2 * blk + par) * LANES, (2 * blk + par + 1) * LANES)
                s.append(_dot_nt(qs_ref[pl.ds(r, tq), sl], ks_ref[:, sl]))
            out = _attend_pair(s[0], s[1], vs_ref[2 * blk], vs_ref[2 * blk + 1], lo_q)
            o_ref[pl.ds(r, tq), blk * LANES:(blk + 1) * LANES] = out.astype(o_ref.dtype)
        return carry

    lax.fori_loop(0, s_len // tq, body, 0, unroll=4)


def _attn_c(h, wuq, wuk, wuv, gq, gkv, cos_c, sin_c, batch, seq, tq=256):
    kern = functools.partial(_attn_c_kernel, tq=tq)
    const = lambda shape: pl.BlockSpec(shape, lambda b: (0,) * len(shape))
    return pl.pallas_call(
        kern,
        out_shape=jax.ShapeDtypeStruct((batch * seq, MIX_W), jnp.bfloat16),
        grid=(batch,),
        in_specs=[pl.BlockSpec((seq, 512), lambda b: (b, COL_C // 512)),
                  const(wuq.shape), const(wuk.shape), const(wuv.shape),
                  const(gq.shape), const(gkv.shape), const(cos_c.shape), const(sin_c.shape)],
        out_specs=pl.BlockSpec((seq, MIX_W), lambda b: (b, 0)),
        scratch_shapes=[pltpu.VMEM((seq, C_HEADS * LANES), jnp.bfloat16),
                        pltpu.VMEM((seq, C_HEADS * LANES), jnp.bfloat16),
                        pltpu.VMEM((C_HEADS, seq, LANES), jnp.bfloat16)],
        compiler_params=_cparams(("parallel",)),
    )(h, wuq, wuk, wuv, gq, gkv, cos_c, sin_c)


CONV_PAD = 16
CONV_ROWS = 128


def _conv_kernel(hd_ref, w_ref, cb_ref, g_ref, b_ref, o_ref, up_ref):
    s_len = hd_ref.shape[0]
    a = hd_ref[:, 0:D_CH].astype(jnp.float32)
    gate = hd_ref[:, D_CH:2 * D_CH].astype(jnp.float32)
    zeros = jnp.zeros((CONV_PAD, D_CH), jnp.float32)
    up_ref[0, 0:CONV_PAD, :] = zeros
    up_ref[0, CONV_PAD + s_len:2 * CONV_PAD + s_len, :] = zeros
    up_ref[0, CONV_PAD:CONV_PAD + s_len, :] = a * _sigmoid(gate)
    n_shifted = s_len + 2 * CONV_PAD - SUBLANES
    for s in range(1, SUBLANES):
        up_ref[s, 0:n_shifted, :] = up_ref[0, s:s + n_shifted, :]
    shift = CONV_PAD - CONV_W // 2
    for c in range(s_len // CONV_ROWS):
        acc = jnp.zeros((CONV_ROWS, D_CH), jnp.float32)
        for j in range(CONV_W):
            s = (j + shift) % SUBLANES
            start = c * CONV_ROWS + j + shift - s
            acc = acc + up_ref[s, start:start + CONV_ROWS, :] * w_ref[j:j + 1, :]
        u = acc + cb_ref[...]
        y = _layer_norm_rows(u, g_ref[...], b_ref[...])
        o_ref[c * CONV_ROWS:(c + 1) * CONV_ROWS, :] = (y * _sigmoid(y)).astype(o_ref.dtype)


def _conv_branch(h, w, cb, g, b, batch, seq):
    const = lambda shape: pl.BlockSpec(shape, lambda i: (0,) * len(shape))
    return pl.pallas_call(
        _conv_kernel,
        out_shape=jax.ShapeDtypeStruct((batch * seq, MIX_W), jnp.bfloat16),
        grid=(batch,),
        in_specs=[pl.BlockSpec((seq, 512), lambda i: (i, COL_D // 512)),
                  const(w.shape), const(cb.shape), const(g.shape), const(b.shape)],
        out_specs=pl.BlockSpec((seq, MIX_W), lambda i: (i, 0)),
        scratch_shapes=[pltpu.VMEM((SUBLANES, seq + 2 * CONV_PAD, D_CH), jnp.float32)],
        compiler_params=_cparams(("parallel",)),
    )(h, w, cb, g, b)


def _merge_kernel(gl_ref, ya_ref, yb_ref, yc_ref, yd_ref, x_ref, wb_ref, wo_ref, g_ref, b_ref,
                  of_ref, ob_ref):
    merged = None
    for n, y_ref in enumerate((ya_ref, yb_ref, yc_ref, yd_ref)):
        t = jnp.tanh(gl_ref[:, n * D_MODEL:(n + 1) * D_MODEL].astype(jnp.float32))
        term = (t + 1.0) * _dot(y_ref[...], wb_ref[n])
        merged = term if merged is None else merged + term
    mix = _dot(merged.astype(jnp.bfloat16), wo_ref[...])
    y = _layer_norm_rows(ALPHA * x_ref[...] + mix, g_ref[...], b_ref[...])
    of_ref[...] = y
    ob_ref[...] = y.astype(jnp.bfloat16)


def _merge(h, ya, yb, yc, yd, x, wb, wo, g, b, tm=512):
    t, d = x.shape
    row = lambda w: pl.BlockSpec((tm, w), lambda i: (i, 0))
    const = lambda shape: pl.BlockSpec(shape, lambda i: (0,) * len(shape))
    return pl.pallas_call(
        _merge_kernel,
        out_shape=(jax.ShapeDtypeStruct((t, d), jnp.float32), jax.ShapeDtypeStruct((t, d), jnp.bfloat16)),
        grid=(t // tm,),
        in_specs=[pl.BlockSpec((tm, N_BRANCH * d), lambda i: (i, COL_GATES)),
                  row(MIX_W), row(MIX_W), row(MIX_W), row(MIX_W), row(d),
                  const(wb.shape), const(wo.shape), const((1, d)), const((1, d))],
        out_specs=(row(d), row(d)),
        compiler_params=_cparams(("parallel",)),
    )(h, ya, yb, yc, yd, x, wb, wo, g.reshape(1, d), b.reshape(1, d))


def _ffn_kernel(xb_ref, xf_ref, w1_ref, w3_ref, w2_ref, g_ref, b_ref, of_ref, ob_ref):
    x = xb_ref[...]
    f = w2_ref.shape[0]
    split = (f // MXU_TILE + 1) // 2 * MXU_TILE
    acc = None
    for lo, hi in ((0, split), (split, f)):
        a = _dot(x, w1_ref[:, lo:hi])
        hid = (a * _sigmoid(a) * _dot(x, w3_ref[:, lo:hi])).astype(jnp.bfloat16)
        part = _dot(hid, w2_ref[lo:hi, :])
        acc = part if acc is None else acc + part
    y = _layer_norm_rows(ALPHA * xf_ref[...] + acc, g_ref[...], b_ref[...])
    of_ref[...] = y
    ob_ref[...] = y.astype(jnp.bfloat16)


def _ffn(xb, xf, w1, w3, w2, g, b, tm=512):
    t, d = xf.shape
    row = pl.BlockSpec((tm, d), lambda i: (i, 0))
    vec = pl.BlockSpec((1, d), lambda i: (0, 0))
    resident = lambda shape: pl.BlockSpec(shape, lambda i: (0, 0), pipeline_mode=pl.Buffered(1))
    return pl.pallas_call(
        _ffn_kernel,
        out_shape=(jax.ShapeDtypeStruct((t, d), jnp.float32), jax.ShapeDtypeStruct((t, d), jnp.bfloat16)),
        grid=(t // tm,),
        in_specs=[row, row, resident(w1.shape), resident(w3.shape), resident(w2.shape), vec, vec],
        out_specs=(row, row),
        compiler_params=_cparams(("parallel",)),
    )(xb, xf, w1, w3, w2, g.reshape(1, d), b.reshape(1, d))


def _pack_rows(x):
    half = x.shape[1] // 2
    bits = lax.bitcast_convert_type(x.astype(jnp.bfloat16).astype(jnp.float32), jnp.uint32)
    return (bits[:, :half] >> 16) | (bits[:, half:] & jnp.uint32(0xFFFF0000))


def _unpack_rows(p):
    lo = lax.bitcast_convert_type(p << 16, jnp.float32)
    hi = lax.bitcast_convert_type(p & jnp.uint32(0xFFFF0000), jnp.float32)
    return jnp.concatenate([lo, hi], axis=1)


def _router_kernel(x_ref, r_ref, tri_ref, oi_ref, ow_ref, cnt_ref, xp_ref, carry_ref):
    i = pl.program_id(0)

    @pl.when(i == 0)
    def _():
        carry_ref[...] = jnp.zeros_like(carry_ref)

    x = x_ref[...]
    r = r_ref[...]
    xh = x.astype(jnp.bfloat16)
    xl = (x - xh.astype(jnp.float32)).astype(jnp.bfloat16)
    rh = r.astype(jnp.bfloat16)
    rl = (r - rh.astype(jnp.float32)).astype(jnp.bfloat16)
    r2 = jnp.concatenate([rh, rl], axis=1)
    p = _dot(xh, r2) + _dot(xl, r2)
    logits = p[:, :LANES] + p[:, LANES:]
    tm = x.shape[0]
    lane = lax.broadcasted_iota(jnp.int32, (tm, LANES), 1)
    logits = jnp.where(lane < N_EXPERTS, logits, NEG_BIG)
    m1 = jnp.max(logits, axis=-1, keepdims=True)
    i1 = jnp.min(jnp.where(logits == m1, lane, LANES), axis=-1, keepdims=True)
    rest = jnp.where(lane == i1, NEG_BIG, logits)
    m2 = jnp.max(rest, axis=-1, keepdims=True)
    i2 = jnp.min(jnp.where(rest == m2, lane, LANES), axis=-1, keepdims=True)
    e = jnp.exp(m2 - m1)
    w1 = 1.0 / (1.0 + e)
    w2 = e / (1.0 + e)
    sel1 = lane == i1
    sel2 = lane == i2
    member = jnp.where(sel1 | sel2, 1.0, 0.0)
    before = _dot(tri_ref[...], member.astype(jnp.bfloat16)) + carry_ref[...]
    rank1 = jnp.sum(jnp.where(sel1, before, 0.0), axis=-1, keepdims=True).astype(jnp.int32)
    rank2 = jnp.sum(jnp.where(sel2, before, 0.0), axis=-1, keepdims=True).astype(jnp.int32)
    carry_ref[...] += jnp.sum(member, axis=0, keepdims=True)
    oi = jnp.where(lane == 0, i1, jnp.where(lane == 1, i2, jnp.where(lane == 2, rank1, rank2)))
    oi_ref[...] = oi.T[0:SUBLANES, :]
    ow_ref[...] = jnp.where(lane == 0, w1, w2)
    cnt_ref[...] = carry_ref[...]
    xp_ref[...] = _pack_rows(x)


def _router(xf, router, tm=512):
    t, d = xf.shape
    r_pad = jnp.zeros((d, LANES), jnp.float32).at[:, :N_EXPERTS].set(router.astype(jnp.float32))
    tri = jnp.asarray(np.tril(np.ones((tm, tm), np.float32), -1), jnp.bfloat16)
    return pl.pallas_call(
        _router_kernel,
        out_shape=(jax.ShapeDtypeStruct((SUBLANES, t), jnp.int32),
                   jax.ShapeDtypeStruct((t, LANES), jnp.float32),
                   jax.ShapeDtypeStruct((1, LANES), jnp.float32),
                   jax.ShapeDtypeStruct((t, d // 2), jnp.uint32)),
        grid=(t // tm,),
        in_specs=[pl.BlockSpec((tm, d), lambda i: (i, 0)),
                  pl.BlockSpec((d, LANES), lambda i: (0, 0)),
                  pl.BlockSpec((tm, tm), lambda i: (0, 0))],
        out_specs=(pl.BlockSpec((SUBLANES, tm), lambda i: (0, i)),
                   pl.BlockSpec((tm, LANES), lambda i: (i, 0)),
                   pl.BlockSpec((1, LANES), lambda i: (0, 0)),
                   pl.BlockSpec((tm, d // 2), lambda i: (i, 0))),
        scratch_shapes=[pltpu.VMEM((1, LANES), jnp.float32)],
        compiler_params=_cparams(("arbitrary",)),
    )(xf, r_pad, tri)


SC_CORES = 2
SC_SUBCORES = 16
SC_WINDOW = 64


def _sc_gather_rows(table, idx):
    m = idx.shape[0]
    d = table.shape[1]
    n_workers = SC_CORES * SC_SUBCORES
    per_w = m // n_workers
    n_win = per_w // SC_WINDOW
    assert per_w * n_workers == m and n_win * SC_WINDOW == per_w
    mesh = plsc.VectorSubcoreMesh(core_axis_name="c", subcore_axis_name="s")

    @functools.partial(
        pl.kernel, mesh=mesh,
        out_type=jax.ShapeDtypeStruct((m, d), table.dtype),
        scratch_types=[pltpu.VMEM((per_w,), jnp.int32),
                       pltpu.VMEM((SC_WINDOW, d), table.dtype),
                       pltpu.SemaphoreType.DMA],
    )
    def gather(table_hbm, idx_hbm, out_hbm, idx_v, rows_v, sem):
        wid = lax.axis_index("s") * SC_CORES + lax.axis_index("c")
        base = wid * per_w
        pltpu.sync_copy(idx_hbm.at[pl.ds(base, per_w)], idx_v)

        @pl.loop(0, n_win)
        def _(w):
            off = pl.multiple_of(w * SC_WINDOW, SC_WINDOW)
            pltpu.async_copy(table_hbm.at[idx_v.at[pl.ds(off, SC_WINDOW)]], rows_v, sem).wait()
            pltpu.sync_copy(rows_v, out_hbm.at[pl.ds(base + off, SC_WINDOW)])

    return gather(table, idx)


def _sc_scatter_rows(rows, idx, n_out):
    n_copies, m = idx.shape
    d = rows.shape[1]
    n_workers = SC_CORES * SC_SUBCORES
    per_w = m // n_workers
    n_win = per_w // SC_WINDOW
    assert per_w * n_workers == m and n_win * SC_WINDOW == per_w
    idx_w = idx.reshape(n_copies, n_workers, n_win, SC_WINDOW).transpose(1, 0, 2, 3)
    idx_w = idx_w.reshape(n_workers, n_copies * n_win, SC_WINDOW)
    mesh = plsc.VectorSubcoreMesh(core_axis_name="c", subcore_axis_name="s")

    @functools.partial(
        pl.kernel, mesh=mesh,
        out_type=jax.ShapeDtypeStruct((n_out, d), rows.dtype),
        scratch_types=[pltpu.VMEM((n_copies * n_win, SC_WINDOW), jnp.int32),
                       pltpu.VMEM((SC_WINDOW, d), rows.dtype)],
    )
    def scatter(rows_hbm, idx_hbm, out_hbm, idx_v, rows_v):
        wid = lax.axis_index("s") * SC_CORES + lax.axis_index("c")
        base = wid * per_w
        pltpu.sync_copy(idx_hbm.at[wid], idx_v)

        @pl.loop(0, n_win)
        def _(w):
            off = pl.multiple_of(w * SC_WINDOW, SC_WINDOW)
            pltpu.sync_copy(rows_hbm.at[pl.ds(base + off, SC_WINDOW)], rows_v)
            for k in range(n_copies):
                pltpu.sync_copy(rows_v, out_hbm.at[idx_v.at[k * n_win + w]])

    return scatter(rows, idx_w)


def _expert_kernel(te_ref, nv_ref, x_ref, w1_ref, w3_ref, w2_ref, o_ref,
                   w13s_ref, w2s_ref, xb_ref, acc_ref):
    i = pl.program_id(0)
    j = pl.program_id(1)
    last = pl.num_programs(1) - 1
    n_valid = nv_ref[i]
    n_piece = MOE_TILE // MOE_SUB
    piece_rows = [pl.ds(s * MOE_SUB, MOE_SUB) for s in range(n_piece)]

    for s, rows in enumerate(piece_rows):
        @pl.when((n_valid > s * MOE_SUB) & (j == 0))
        def _():
            row = lax.broadcasted_iota(jnp.int32, (MOE_SUB, 1), 0) + s * MOE_SUB
            x = jnp.where(row < n_valid, _unpack_rows(x_ref[rows, :]), 0.0)
            xb_ref[rows, :] = x.astype(jnp.bfloat16)
            acc_ref[rows, :] = jnp.zeros((MOE_SUB, acc_ref.shape[1]), jnp.float32)

    def compute(n_live):
        w13s_ref[:, 0:MOE_FCHUNK] = w1_ref[0].astype(jnp.bfloat16)
        w13s_ref[:, MOE_FCHUNK:2 * MOE_FCHUNK] = w3_ref[0].astype(jnp.bfloat16)
        w2s_ref[...] = w2_ref[0].astype(jnp.bfloat16)
        for rows in piece_rows[:n_live]:
            ab = _dot(xb_ref[rows, :], w13s_ref[...])
            a = ab[:, :MOE_FCHUNK]
            hid = (a * _sigmoid(a) * ab[:, MOE_FCHUNK:]).astype(jnp.bfloat16)
            acc_ref[rows, :] += _dot(hid, w2s_ref[...])

    for n_live in range(1, n_piece + 1):
        upper = n_live * MOE_SUB if n_live < n_piece else MOE_TILE
        pl.when((n_valid > (n_live - 1) * MOE_SUB) & (n_valid <= upper))(functools.partial(compute, n_live))

    for s, rows in enumerate(piece_rows):
        @pl.when((n_valid > s * MOE_SUB) & (j == last))
        def _():
            o_ref[rows, :] = _pack_rows(acc_ref[rows, :])

        @pl.when((n_valid <= s * MOE_SUB) & (j == last))
        def _():
            o_ref[rows, :] = jnp.zeros((MOE_SUB, o_ref.shape[1]), o_ref.dtype)


def _experts(tile_expert, tile_valid, xs, w1, w3, w2):
    n_rows = xs.shape[0]
    d, f = w1.shape[1], w1.shape[2]
    nj = f // MOE_FCHUNK

    def chunk(i, j, nv):
        return jnp.where(nv[i] > 0, j, nj - 1)

    return pl.pallas_call(
        _expert_kernel,
        out_shape=jax.ShapeDtypeStruct((n_rows, d // 2), jnp.uint32),
        grid_spec=pltpu.PrefetchScalarGridSpec(
            num_scalar_prefetch=2,
            grid=(n_rows // MOE_TILE, nj),
            in_specs=[pl.BlockSpec((MOE_TILE, d // 2), lambda i, j, te, nv: (i, 0)),
                      pl.BlockSpec((1, d, MOE_FCHUNK), lambda i, j, te, nv: (te[i], 0, chunk(i, j, nv))),
                      pl.BlockSpec((1, d, MOE_FCHUNK), lambda i, j, te, nv: (te[i], 0, chunk(i, j, nv))),
                      pl.BlockSpec((1, MOE_FCHUNK, d), lambda i, j, te, nv: (te[i], chunk(i, j, nv), 0))],
            out_specs=pl.BlockSpec((MOE_TILE, d // 2), lambda i, j, te, nv: (i, 0)),
            scratch_shapes=[pltpu.VMEM((d, 2 * MOE_FCHUNK), jnp.bfloat16),
                            pltpu.VMEM((MOE_FCHUNK, d), jnp.bfloat16),
                            pltpu.VMEM((MOE_TILE, d), jnp.bfloat16),
                            pltpu.VMEM((MOE_TILE, d), jnp.float32)],
        ),
        compiler_params=pltpu.CompilerParams(dimension_semantics=("arbitrary", "arbitrary"),
                                             vmem_limit_bytes=MOE_VMEM_LIMIT),
    )(tile_expert, tile_valid, xs, w1, w3, w2)


def _combine_kernel(ya_ref, yb_ref, x_ref, w_ref, g_ref, b_ref, o_ref):
    w = w_ref[...]
    f = w[:, 0:1] * _unpack_rows(ya_ref[...]) + w[:, 1:2] * _unpack_rows(yb_ref[...])
    o_ref[...] = _layer_norm_rows(ALPHA * x_ref[...] + f, g_ref[...], b_ref[...])


def _combine(yg, xf, wts, g, b, tm=512):
    t, d = xf.shape
    nt = t // tm
    return pl.pallas_call(
        _combine_kernel,
        out_shape=jax.ShapeDtypeStruct((t, d), jnp.float32),
        grid=(nt,),
        in_specs=[pl.BlockSpec((tm, d // 2), lambda i: (i, 0)),
                  pl.BlockSpec((tm, d // 2), lambda i: (i + nt, 0)),
                  pl.BlockSpec((tm, d), lambda i: (i, 0)),
                  pl.BlockSpec((tm, LANES), lambda i: (i, 0)),
                  pl.BlockSpec((1, d), lambda i: (0, 0)),
                  pl.BlockSpec((1, d), lambda i: (0, 0))],
        out_specs=pl.BlockSpec((tm, d), lambda i: (i, 0)),
        compiler_params=_cparams(("parallel",)),
    )(yg, yg, xf, wts, g.reshape(1, d), b.reshape(1, d))


def _moe_ffn(xf, router, w1, w3, w2, g, b):
    t, d = xf.shape
    oi, ow, cnt, xp = _router(xf, router)
    e1, e2, rank1, rank2 = oi[0], oi[1], oi[2], oi[3]
    counts = cnt[0, :N_EXPERTS].astype(jnp.int32)
    tiles = (counts + MOE_TILE - 1) // MOE_TILE
    tile_end = jnp.cumsum(tiles)
    tile_start = tile_end - tiles
    offs = tile_start * MOE_TILE
    dest = jnp.concatenate([offs[e1] + rank1, offs[e2] + rank2])
    n_rows = TOP_K * t + N_EXPERTS * MOE_TILE
    n_tiles = n_rows // MOE_TILE
    tile_id = jnp.arange(n_tiles, dtype=jnp.int32)
    tile_expert = jnp.sum(tile_id[:, None] >= tile_end[None, :], axis=1)
    used = tile_expert < N_EXPERTS
    last_expert = jnp.sum(tile_end[-1] - 1 >= tile_end)
    tile_expert = jnp.where(used, tile_expert, last_expert).astype(jnp.int32)
    tile_valid = jnp.clip(counts[tile_expert] - (tile_id - tile_start[tile_expert]) * MOE_TILE, 0, MOE_TILE)
    tile_valid = jnp.where(used, tile_valid, 0).astype(jnp.int32)
    xs = _sc_scatter_rows(xp, dest.reshape(TOP_K, t), n_rows)
    y = _experts(tile_expert, tile_valid, xs, w1, w3, w2)
    yg = _sc_gather_rows(y, dest)
    return _combine(yg, xf, ow, g, b)


def _prep_layer(l, p, in_cols):
    bf = jnp.bfloat16
    out = {}
    n_gate = N_BRANCH * D_MODEL
    w_l = p["w_in_bf16"][l]
    gate_cols = 0.5 * w_l[:, int(in_cols[0]):int(in_cols[0]) + n_gate]
    out["w_in"] = jnp.concatenate([gate_cols, _take_cols(w_l, in_cols[n_gate:])], axis=1)
    deint = np.concatenate([np.arange(0, HEAD_DIM, 2), np.arange(1, HEAD_DIM, 2)])
    out["gq_a"] = jnp.tile(p["a_q_norm"][l][deint], 2).reshape(1, LANES)
    out["gk_a"] = jnp.tile(p["a_k_norm"][l][deint], 2).reshape(1, LANES)
    per_head = C_NOPE + C_ROPE
    uq_cols = []
    for h in range(C_HEADS):
        base = h * per_head
        uq_cols += [base + np.arange(C_NOPE), base + C_NOPE + np.arange(0, C_ROPE, 2),
                    base + C_NOPE + np.arange(1, C_ROPE, 2), np.full(32, -1)]
    wuq = _take_cols(p["c_w_uq"][l], np.concatenate(uq_cols))
    out["wuq"] = jnp.concatenate([wuq, jnp.zeros((64, wuq.shape[1]), wuq.dtype)], axis=0).astype(bf)
    uk_cols, uv_cols = [], []
    for h in range(C_HEADS):
        base = h * (C_NOPE + C_V)
        uk_cols += [base + np.arange(C_NOPE), np.full(64, -1)]
        uv_cols += [base + C_NOPE + np.arange(C_V)]
    out["wuk"] = _take_cols(p["c_w_ukv"][l], np.concatenate(uk_cols)).astype(bf)
    out["wuv"] = _take_cols(p["c_w_ukv"][l], np.concatenate(uv_cols)).astype(bf)
    out["gq_c"] = jnp.concatenate([p["c_q_norm"][l], jnp.zeros((64,), jnp.float32)]).reshape(1, 256)
    out["gkv_c"] = p["c_kv_norm"][l].reshape(1, C_KV_RANK)
    out["conv_w"] = jnp.concatenate([p["d_conv_w"][l][:, 0, :], jnp.zeros((1, D_CH), jnp.float32)], axis=0)
    out["conv_b"] = p["d_conv_b"][l].reshape(1, D_CH)
    out["d_ln_g"] = p["d_ln_g"][l].reshape(1, D_CH)
    out["d_ln_b"] = p["d_ln_b"][l].reshape(1, D_CH)
    a_rows = np.concatenate([(kvh * 2 + g) * HEAD_DIM + np.arange(HEAD_DIM)
                             for g in range(2) for kvh in range(A_KV_HEADS)])
    wb = p["w_branch"][l]
    out["w_branch"] = (0.5 * jnp.stack([wb[0][a_rows], wb[1], wb[2], wb[3]], axis=0)).astype(bf)
    out["w_out"] = p["w_out"][l].astype(bf)
    return out


def kernel(x, ln_in_g, ln_in_b, w_in, a_q_norm, a_k_norm, b_rpb, c_q_norm, c_kv_norm, c_w_uq, c_w_ukv,
           d_conv_w, d_conv_b, d_ln_g, d_ln_b, w_branch, w_out, ln_mix_g, ln_mix_b,
           ffn_w1, ffn_w3, ffn_w2, moe_router, moe_w1, moe_w3, moe_w2, ln_ffn_g, ln_ffn_b):
    batch, seq, d = x.shape
    t = batch * seq
    bf = jnp.bfloat16
    params = dict(w_in_bf16=w_in.astype(bf), a_q_norm=a_q_norm, a_k_norm=a_k_norm, c_q_norm=c_q_norm, c_kv_norm=c_kv_norm,
                  c_w_uq=c_w_uq, c_w_ukv=c_w_ukv, d_conv_w=d_conv_w, d_conv_b=d_conv_b, d_ln_g=d_ln_g,
                  d_ln_b=d_ln_b, w_branch=w_branch, w_out=w_out)
    in_cols = _in_proj_columns()
    cos_a, sin_a, cos_c, sin_c = _rope_tables(seq)
    na_bias = _na_bias_tables(b_rpb, seq // GRID_W)
    xf = xb = None
    for l in range(DEPTH):
        p = _prep_layer(l, params, in_cols)
        if l == 0:
            h, xf = _in_proj(x.reshape(t, d), p["w_in"], ln=(ln_in_g, ln_in_b))
        else:
            h = _in_proj(xb, p["w_in"])
        ya = _attn_a(h, cos_a, sin_a, p["gq_a"], p["gk_a"], batch, seq)
        yb = _attn_b(h, na_bias, l, batch, seq)
        yc = _attn_c(h, p["wuq"], p["wuk"], p["wuv"], p["gq_c"], p["gkv_c"], cos_c, sin_c, batch, seq)
        yd = _conv_branch(h, p["conv_w"], p["conv_b"], p["d_ln_g"], p["d_ln_b"], batch, seq)
        xf, xb = _merge(h, ya, yb, yc, yd, xf, p["w_branch"], p["w_out"], ln_mix_g[l], ln_mix_b[l])
        if l % 2 == 0:
            i = l // 2
            xf, xb = _ffn(xb, xf, ffn_w1[i].astype(bf), ffn_w3[i].astype(bf), ffn_w2[i].astype(bf),
                          ln_ffn_g[l], ln_ffn_b[l])
        else:
            i = l // 2
            xf = _moe_ffn(xf, moe_router[i], moe_w1[i], moe_w3[i], moe_w2[i], ln_ffn_g[l], ln_ffn_b[l])
            xb = xf.astype(bf)
    return xf.reshape(batch, seq, d)
```

```python
import functools

import numpy as np
import jax
import jax.numpy as jnp
from jax import lax
from jax.experimental import pallas as pl
from jax.experimental.pallas import tpu as pltpu
from jax.experimental.pallas import tpu_sc as plsc

D_MODEL = 1024
DEPTH = 2
GRID_W = 64
HEAD_DIM = 64
ROPE_THETA = 10000.0
RMS_EPS = 1e-6
LN_EPS = 1e-5
A_HEADS = 4
A_KV_HEADS = 2
B_HEADS = 4
NA_WIN_H = 8
NA_WIN_W = 16
C_HEADS = 4
C_NOPE = 64
C_ROPE = 32
C_V = 64
C_Q_RANK = 192
C_KV_RANK = 128
D_CH = 256
CONV_W = 31
N_BRANCH = 4
MIX_W = 256
D_FF = 2816
N_EXPERTS = 8
TOP_K = 2
D_FF_EXPERT = 3584
ALPHA = (2 * DEPTH) ** 0.25

LANES = 128
SUBLANES = 8
MXU_TILE = 256
VMEM_LIMIT = 48 * 1024 * 1024

LOG2E = 1.4426950408889634
NEG_BIG = -1e30

COL_GATES = 0
COL_AQ = 4096
COL_AKV = 4352
COL_C = 4608
COL_D = 5120
COL_BQ = 5632
COL_BK = 5888
COL_BV = 6144
D_IN_PAD = 6400

NA_QROWS = 4
NA_KROWS = 12

MOE_TILE = 2048
MOE_SUB = 512
MOE_FCHUNK = 512
MOE_VMEM_LIMIT = 56 * 1024 * 1024


def _in_proj_columns():
    o_aq, o_ak, o_av = 0, 256, 384
    o_bq, o_bk, o_bv = 512, 768, 1024
    o_cq, o_ckv, o_kpe = 1280, 1472, 1600
    o_d, o_g = 1632, 2144
    deint = np.concatenate([np.arange(0, HEAD_DIM, 2), np.arange(1, HEAD_DIM, 2)])
    cols = [o_g + np.arange(N_BRANCH * D_MODEL)]
    for g in range(2):
        for kvh in range(A_KV_HEADS):
            cols.append(o_aq + (kvh * 2 + g) * HEAD_DIM + deint)
    for kvh in range(A_KV_HEADS):
        cols.append(o_ak + kvh * HEAD_DIM + deint)
    cols.append(o_av + np.arange(A_KV_HEADS * HEAD_DIM))
    cols.append(o_cq + np.arange(C_Q_RANK))
    cols.append(np.full(64, -1))
    cols.append(o_ckv + np.arange(C_KV_RANK))
    cols.append(np.full(32, -1))
    cols.append(o_kpe + np.arange(0, C_ROPE, 2))
    cols.append(np.full(48, -1))
    cols.append(o_kpe + np.arange(1, C_ROPE, 2))
    cols.append(np.full(16, -1))
    cols.append(o_d + np.arange(2 * D_CH))
    cols.append(o_bq + np.arange(3 * B_HEADS * HEAD_DIM))
    cols = np.concatenate(cols)
    assert cols.shape[0] == D_IN_PAD
    return cols


def _take_cols(w, cols):
    cols = np.asarray(cols)
    runs, i, n = [], 0, len(cols)
    while i < n:
        j = i + 1
        if cols[i] < 0:
            while j < n and cols[j] < 0:
                j += 1
            kind = "zero"
        else:
            while j < n and cols[j] == cols[j - 1] + 1:
                j += 1
            kind = "slice" if j - i >= 64 else "shuffle"
        if kind == "shuffle" and runs and runs[-1][0] == "shuffle":
            runs[-1] = ("shuffle", runs[-1][1], j)
        else:
            runs.append((kind, i, j))
        i = j
    parts = []
    for kind, i, j in runs:
        if kind == "zero":
            parts.append(jnp.zeros((w.shape[0], j - i), w.dtype))
        elif kind == "slice":
            parts.append(w[:, int(cols[i]):int(cols[i]) + j - i])
        else:
            lo, hi = int(cols[i:j].min()), int(cols[i:j].max()) + 1
            onehot = np.zeros((hi - lo, j - i), np.float32)
            onehot[cols[i:j] - lo, np.arange(j - i)] = 1.0
            parts.append(jnp.dot(w[:, lo:hi], jnp.asarray(onehot, w.dtype), precision=lax.Precision.HIGHEST))
    return jnp.concatenate(parts, axis=1)


def _rope_tables(seq):
    t = np.arange(seq)
    row = (t // GRID_W).astype(np.float32)
    col = (t % GRID_W).astype(np.float32)

    def angles(rot_dim):
        n_freq = rot_dim // 4
        inv = jnp.asarray(ROPE_THETA, jnp.float32) ** (-jnp.arange(n_freq, dtype=jnp.float32) / n_freq)
        return jnp.concatenate([jnp.asarray(row)[:, None] * inv, jnp.asarray(col)[:, None] * inv], axis=-1)

    ang_a = angles(HEAD_DIM)
    ca, sa = jnp.cos(ang_a), jnp.sin(ang_a)
    cos_a = jnp.concatenate([ca, ca, ca, ca], axis=-1)
    sin_a = jnp.concatenate([-sa, sa, -sa, sa], axis=-1)
    ang_c = angles(C_ROPE)
    cc, sc = jnp.cos(ang_c), jnp.sin(ang_c)
    one = jnp.ones((seq, 64), jnp.float32)
    zero = jnp.zeros((seq, 64), jnp.float32)
    cos_c = jnp.concatenate([one[:, :32], cc, one[:, :48], cc, one[:, :16]], axis=-1)
    sin_c = jnp.concatenate([zero[:, :32], -sc, zero[:, :48], sc, zero[:, :16]], axis=-1)
    return cos_a, sin_a, cos_c, sin_c


def _na_bias_tables(rpb, rows):
    wh = min(NA_WIN_H, rows)
    n_dr, n_dc = 2 * NA_WIN_H - 1, 2 * NA_WIN_W - 1
    qc = np.arange(GRID_W)[:, None]
    kc = np.arange(GRID_W)[None, :]
    col_start = np.clip(qc - NA_WIN_W // 2, 0, GRID_W - NA_WIN_W)
    ok_c = (kc >= col_start) & (kc < col_start + NA_WIN_W)
    dc = np.clip(kc - qc + (NA_WIN_W - 1), 0, n_dc - 1)
    oh_c = np.eye(n_dc, dtype=np.float32)[dc.reshape(-1)]
    oh_r, ok_r = [], []
    for r0 in (0, NA_QROWS, rows - NA_QROWS):
        ws = int(np.clip(r0 - 4, 0, rows - NA_KROWS))
        qr = r0 + np.arange(NA_QROWS)[:, None]
        kr = ws + np.arange(NA_KROWS)[None, :]
        row_start = np.clip(qr - wh // 2, 0, rows - wh)
        ok_r.append((kr >= row_start) & (kr < row_start + wh))
        dr = np.clip(kr - qr + (NA_WIN_H - 1), 0, n_dr - 1)
        oh_r.append(np.eye(n_dr, dtype=np.float32)[dr.reshape(-1)])
    oh_r = np.concatenate(oh_r, axis=0)
    ok = np.stack(ok_r)[:, :, None, :, None] & ok_c[None, None, :, None, :]
    ok = ok.reshape(3, 1, NA_QROWS * GRID_W, NA_KROWS * GRID_W)
    hi = lax.Precision.HIGHEST
    t1 = jnp.einsum("ma,lhab->lhmb", jnp.asarray(oh_r), rpb.astype(jnp.float32) * LOG2E, precision=hi)
    t2 = jnp.einsum("lhmb,nb->lhmn", t1, jnp.asarray(oh_c), precision=hi).astype(jnp.bfloat16)
    nl, nh = rpb.shape[0], rpb.shape[1]
    t2 = t2.reshape(nl, nh, 3, NA_QROWS, NA_KROWS, GRID_W, GRID_W).transpose(0, 2, 1, 3, 5, 4, 6)
    bias = t2.reshape(nl, 3, nh, NA_QROWS * GRID_W, NA_KROWS * GRID_W)
    return jnp.where(jnp.asarray(ok), bias, jnp.asarray(NEG_BIG, jnp.bfloat16))


def _cparams(sem):
    return pltpu.CompilerParams(dimension_semantics=sem, vmem_limit_bytes=VMEM_LIMIT)


def _layer_norm_rows(z, g, b):
    mu = jnp.mean(z, axis=-1, keepdims=True)
    zc = z - mu
    var = jnp.mean(zc * zc, axis=-1, keepdims=True)
    return zc * lax.rsqrt(var + LN_EPS) * g + b


def _sigmoid(x):
    return 0.5 * jnp.tanh(0.5 * x) + 0.5


def _dot(a, b):
    return jnp.dot(a, b, preferred_element_type=jnp.float32)


def _dot_nt(a, b):
    return lax.dot_general(a, b, (((1,), (1,)), ((), ())), preferred_element_type=jnp.float32)


IN_PROJ_CHUNK = 1280


def _project_chunks(xb, w_refs, h_ref):
    off = 0
    for w_ref in w_refs:
        n_tiles = w_ref.shape[1] // MXU_TILE
        n_chunks = -(-w_ref.shape[1] // IN_PROJ_CHUNK)
        lo = 0
        for c in range(n_chunks):
            hi = (n_tiles * (c + 1) // n_chunks) * MXU_TILE
            h_ref[:, off + lo:off + hi] = _dot(xb, w_ref[:, lo:hi]).astype(h_ref.dtype)
            lo = hi
        off += w_ref.shape[1]


def _in_proj_kernel(x_ref, wg_ref, wr_ref, h_ref):
    _project_chunks(x_ref[...], (wg_ref, wr_ref), h_ref)


def _in_proj_ln_kernel(x_ref, g_ref, b_ref, wg_ref, wr_ref, h_ref, xf_ref):
    y = _layer_norm_rows(x_ref[...], g_ref[...], b_ref[...])
    xf_ref[...] = y
    _project_chunks(y.astype(jnp.bfloat16), (wg_ref, wr_ref), h_ref)


def _in_proj(x, w_gate, w_rest, ln=None, tm=512):
    t, d = x.shape
    n = w_gate.shape[1] + w_rest.shape[1]
    row = lambda width: pl.BlockSpec((tm, width), lambda i: (i, 0))
    vec = pl.BlockSpec((1, d), lambda i: (0, 0))
    resident = lambda w: pl.BlockSpec(w.shape, lambda i: (0, 0), pipeline_mode=pl.Buffered(1))
    h_shape = jax.ShapeDtypeStruct((t, n), jnp.bfloat16)
    if ln is None:
        return pl.pallas_call(
            _in_proj_kernel, out_shape=h_shape, grid=(t // tm,),
            in_specs=[row(d), resident(w_gate), resident(w_rest)], out_specs=row(n),
            compiler_params=_cparams(("parallel",)),
        )(x, w_gate, w_rest)
    g, b = ln
    return pl.pallas_call(
        _in_proj_ln_kernel,
        out_shape=(h_shape, jax.ShapeDtypeStruct((t, d), jnp.float32)),
        grid=(t // tm,),
        in_specs=[row(d), vec, vec, resident(w_gate), resident(w_rest)], out_specs=(row(n), row(d)),
        compiler_params=_cparams(("parallel",)),
    )(x, g.reshape(1, d), b.reshape(1, d), w_gate, w_rest)


def _ones_halves(vblk):
    lane = lax.broadcasted_iota(jnp.int32, vblk.shape, 1)
    ones = jnp.ones_like(vblk)
    return jnp.where(lane < HEAD_DIM, vblk, ones), jnp.where(lane < HEAD_DIM, ones, vblk)


def _attend_pair(s_lo, s_hi, v_lo, v_hi, lo_q):
    outs = []
    for s, v in ((s_lo, v_lo), (s_hi, v_hi)):
        p = jnp.exp2(s - jnp.max(s, axis=-1, keepdims=True)).astype(jnp.bfloat16)
        outs.append(_dot(p, v))
    num = jnp.where(lo_q, outs[0], outs[1])
    den = jnp.where(lo_q, pltpu.roll(outs[0], HEAD_DIM, 1), pltpu.roll(outs[1], HEAD_DIM, 1))
    return num / den


def _rotate_pairs(x, cos, sin, half, lane):
    w = x.shape[-1]
    first = (lane % (2 * half)) < half
    partner = jnp.where(first, pltpu.roll(x, w - half, 1), pltpu.roll(x, half, 1))
    return x * cos + partner * sin


def _attn_a_kernel(q_ref, kv_ref, cos_ref, sin_ref, gq_ref, gk_ref, o_ref, ks_ref, va_ref, *, tq):
    s_len = q_ref.shape[0]

    def prep(x, g, cos, sin):
        lane = lax.broadcasted_iota(jnp.int32, x.shape, 1)
        lo = lane < HEAD_DIM
        x2 = x * x
        ss_lo = jnp.sum(jnp.where(lo, x2, 0.0), axis=-1, keepdims=True)
        ss_hi = jnp.sum(jnp.where(lo, 0.0, x2), axis=-1, keepdims=True)
        ms = jnp.where(lo, ss_lo, ss_hi) * (1.0 / HEAD_DIM)
        xn = x * lax.rsqrt(ms + RMS_EPS) * g
        return _rotate_pairs(xn, cos, sin, HEAD_DIM // 2, lane)

    ks_ref[...] = prep(kv_ref[:, 0:LANES].astype(jnp.float32), gk_ref[...],
                       cos_ref[...], sin_ref[...]).astype(jnp.bfloat16)
    scale = HEAD_DIM ** -0.5 * LOG2E
    lane_q = lax.broadcasted_iota(jnp.int32, (tq, LANES), 1)
    lo_q = lane_q < HEAD_DIM

    va_ref[0], va_ref[1] = _ones_halves(kv_ref[:, LANES:2 * LANES])

    def body(i, carry):
        r = pl.multiple_of(i * tq, tq)
        k = ks_ref[...]
        cos = cos_ref[pl.ds(r, tq), :]
        sin = sin_ref[pl.ds(r, tq), :]
        for g in range(2):
            qt = prep(q_ref[pl.ds(r, tq), g * LANES:(g + 1) * LANES].astype(jnp.float32), gq_ref[...], cos, sin)
            qt = (qt * scale).astype(jnp.bfloat16)
            zero = jnp.zeros_like(qt)
            out = _attend_pair(_dot_nt(jnp.where(lo_q, qt, zero), k), _dot_nt(jnp.where(lo_q, zero, qt), k),
                               va_ref[0], va_ref[1], lo_q)
            o_ref[pl.ds(r, tq), g * LANES:(g + 1) * LANES] = out.astype(o_ref.dtype)
        return carry

    lax.fori_loop(0, s_len // tq, body, 0, unroll=4)


def _attn_a(h, cos_a, sin_a, gq, gk, batch, seq, tq=256):
    kern = functools.partial(_attn_a_kernel, tq=tq)
    return pl.pallas_call(
        kern,
        out_shape=jax.ShapeDtypeStruct((batch * seq, MIX_W), jnp.bfloat16),
        grid=(batch,),
        in_specs=[pl.BlockSpec((seq, 256), lambda b: (b, COL_AQ // 256)),
                  pl.BlockSpec((seq, 256), lambda b: (b, COL_AKV // 256)),
                  pl.BlockSpec((seq, LANES), lambda b: (0, 0)),
                  pl.BlockSpec((seq, LANES), lambda b: (0, 0)),
                  pl.BlockSpec((1, LANES), lambda b: (0, 0)),
                  pl.BlockSpec((1, LANES), lambda b: (0, 0))],
        out_specs=pl.BlockSpec((seq, MIX_W), lambda b: (b, 0)),
        scratch_shapes=[pltpu.VMEM((seq, LANES), jnp.bfloat16), pltpu.VMEM((2, seq, LANES), jnp.bfloat16)],
        compiler_params=_cparams(("parallel",)),
    )(h, h, cos_a, sin_a, gq, gk)


def _attn_b_kernel(q_ref, k_ref, v_ref, bias_ref, o_ref, va_ref, *, rows):
    nq = NA_QROWS * GRID_W
    nk = NA_KROWS * GRID_W
    lane_q = lax.broadcasted_iota(jnp.int32, (nq, LANES), 1)
    lo_q = lane_q < HEAD_DIM
    n_groups = rows // NA_QROWS
    scale = HEAD_DIM ** -0.5 * LOG2E
    for blk in range(B_HEADS // 2):
        va_ref[2 * blk], va_ref[2 * blk + 1] = _ones_halves(v_ref[:, blk * LANES:(blk + 1) * LANES])

    def body(gi, carry):
        ws = jnp.clip(gi * NA_QROWS - 4, 0, rows - NA_KROWS)
        cls = jnp.where(gi == 0, 0, jnp.where(gi == n_groups - 1, 2, 1))
        qr = pl.multiple_of(gi * nq, nq)
        kr = pl.multiple_of(ws * GRID_W, GRID_W)
        for blk in range(B_HEADS // 2):
            sl = slice(blk * LANES, (blk + 1) * LANES)
            qt = q_ref[pl.ds(qr, nq), sl] * scale
            kw = k_ref[pl.ds(kr, nk), sl]
            zero = jnp.zeros_like(qt)
            s_lo = _dot_nt(jnp.where(lo_q, qt, zero), kw) + bias_ref[cls, 2 * blk].astype(jnp.float32)
            s_hi = _dot_nt(jnp.where(lo_q, zero, qt), kw) + bias_ref[cls, 2 * blk + 1].astype(jnp.float32)
            out = _attend_pair(s_lo, s_hi, va_ref[2 * blk, pl.ds(kr, nk), :], va_ref[2 * blk + 1, pl.ds(kr, nk), :], lo_q)
            o_ref[pl.ds(qr, nq), sl] = out.astype(o_ref.dtype)
        return carry

    lax.fori_loop(0, n_groups, body, 0, unroll=True)


def _attn_b(h, bias, layer, batch, seq):
    rows = seq // GRID_W
    kern = functools.partial(_attn_b_kernel, rows=rows)
    return pl.pallas_call(
        kern,
        out_shape=jax.ShapeDtypeStruct((batch * seq, MIX_W), jnp.bfloat16),
        grid=(batch,),
        in_specs=[pl.BlockSpec((seq, 256), lambda b: (b, COL_BQ // 256)),
                  pl.BlockSpec((seq, 256), lambda b: (b, COL_BK // 256)),
                  pl.BlockSpec((seq, 256), lambda b: (b, COL_BV // 256)),
                  pl.BlockSpec((None,) + bias.shape[1:], lambda b: (layer, 0, 0, 0, 0))],
        out_specs=pl.BlockSpec((seq, MIX_W), lambda b: (b, 0)),
        scratch_shapes=[pltpu.VMEM((B_HEADS, seq, LANES), jnp.bfloat16)],
        compiler_params=_cparams(("parallel",)),
    )(h, h, h, bias)


def _attn_c_kernel(hc_ref, wuq_ref, wuk_ref, wuv_ref, gq_ref, gkv_ref, cos_ref, sin_ref, o_ref,
                   qs_ref, ks_ref, vs_ref, *, tq):
    s_len = hc_ref.shape[0]
    cos = cos_ref[...]
    sin = sin_ref[...]

    def rotate(x):
        return x * cos + pltpu.roll(x, LANES // 2, 1) * sin

    cq = hc_ref[:, 0:256].astype(jnp.float32)
    ms = jnp.sum(cq * cq, axis=-1, keepdims=True) * (1.0 / C_Q_RANK)
    cqn = (cq * lax.rsqrt(ms + RMS_EPS) * gq_ref[...]).astype(jnp.bfloat16)
    ckv = hc_ref[:, 256:384].astype(jnp.float32)
    ms = jnp.mean(ckv * ckv, axis=-1, keepdims=True)
    kvn = (ckv * lax.rsqrt(ms + RMS_EPS) * gkv_ref[...]).astype(jnp.bfloat16)
    kpe = rotate(hc_ref[:, 384:512].astype(jnp.float32))
    v_all = _dot(kvn, wuv_ref[...]).astype(jnp.bfloat16)
    for blk in range(C_HEADS // 2):
        vs_ref[2 * blk], vs_ref[2 * blk + 1] = _ones_halves(v_all[:, blk * LANES:(blk + 1) * LANES])
    scale = (C_NOPE + C_ROPE) ** -0.5 * LOG2E
    for h in range(C_HEADS):
        sl = slice(h * LANES, (h + 1) * LANES)
        qh = rotate(_dot(cqn, wuq_ref[:, sl])) * scale
        qs_ref[:, sl] = qh.astype(jnp.bfloat16)
        ks_ref[:, sl] = (_dot(kvn, wuk_ref[:, sl]) + kpe).astype(jnp.bfloat16)

    lane_q = lax.broadcasted_iota(jnp.int32, (tq, LANES), 1)
    lo_q = lane_q < C_V

    def body(i, carry):
        r = pl.multiple_of(i * tq, tq)
        for blk in range(C_HEADS // 2):
            s = []
            for par in range(2):
                sl = slice((2 * blk + par) * LANES, (2 * blk + par + 1) * LANES)
                s.append(_dot_nt(qs_ref[pl.ds(r, tq), sl], ks_ref[:, sl]))
            out = _attend_pair(s[0], s[1], vs_ref[2 * blk], vs_ref[2 * blk + 1], lo_q)
            o_ref[pl.ds(r, tq), blk * LANES:(blk + 1) * LANES] = out.astype(o_ref.dtype)
        return carry

    lax.fori_loop(0, s_len // tq, body, 0, unroll=4)


def _attn_c(h, wuq, wuk, wuv, gq, gkv, cos_c, sin_c, batch, seq, tq=256):
    kern = functools.partial(_attn_c_kernel, tq=tq)
    const = lambda shape: pl.BlockSpec(shape, lambda b: (0,) * len(shape))
    return pl.pallas_call(
        kern,
        out_shape=jax.ShapeDtypeStruct((batch * seq, MIX_W), jnp.bfloat16),
        grid=(batch,),
        in_specs=[pl.BlockSpec((seq, 512), lambda b: (b, COL_C // 512)),
                  const(wuq.shape), const(wuk.shape), const(wuv.shape),
                  const(gq.shape), const(gkv.shape), const(cos_c.shape), const(sin_c.shape)],
        out_specs=pl.BlockSpec((seq, MIX_W), lambda b: (b, 0)),
        scratch_shapes=[pltpu.VMEM((seq, C_HEADS * LANES), jnp.bfloat16),
                        pltpu.VMEM((seq, C_HEADS * LANES), jnp.bfloat16),
                        pltpu.VMEM((C_HEADS, seq, LANES), jnp.bfloat16)],
        compiler_params=_cparams(("parallel",)),
    )(h, wuq, wuk, wuv, gq, gkv, cos_c, sin_c)


CONV_PAD = 16
CONV_ROWS = 128


def _conv_kernel(hd_ref, w_ref, cb_ref, g_ref, b_ref, o_ref, up_ref):
    s_len = hd_ref.shape[0]
    a = hd_ref[:, 0:D_CH].astype(jnp.float32)
    gate = hd_ref[:, D_CH:2 * D_CH].astype(jnp.float32)
    zeros = jnp.zeros((CONV_PAD, D_CH), jnp.float32)
    up_ref[0, 0:CONV_PAD, :] = zeros
    up_ref[0, CONV_PAD + s_len:2 * CONV_PAD + s_len, :] = zeros
    up_ref[0, CONV_PAD:CONV_PAD + s_len, :] = a * _sigmoid(gate)
    n_shifted = s_len + 2 * CONV_PAD - SUBLANES
    for s in range(1, SUBLANES):
        up_ref[s, 0:n_shifted, :] = up_ref[0, s:s + n_shifted, :]
    shift = CONV_PAD - CONV_W // 2
    for c in range(s_len // CONV_ROWS):
        acc = jnp.zeros((CONV_ROWS, D_CH), jnp.float32)
        for j in range(CONV_W):
            s = (j + shift) % SUBLANES
            start = c * CONV_ROWS + j + shift - s
            acc = acc + up_ref[s, start:start + CONV_ROWS, :] * w_ref[j:j + 1, :]
        u = acc + cb_ref[...]
        y = _layer_norm_rows(u, g_ref[...], b_ref[...])
        o_ref[c * CONV_ROWS:(c + 1) * CONV_ROWS, :] = (y * _sigmoid(y)).astype(o_ref.dtype)


def _conv_branch(h, w, cb, g, b, batch, seq):
    const = lambda shape: pl.BlockSpec(shape, lambda i: (0,) * len(shape))
    return pl.pallas_call(
        _conv_kernel,
        out_shape=jax.ShapeDtypeStruct((batch * seq, MIX_W), jnp.bfloat16),
        grid=(batch,),
        in_specs=[pl.BlockSpec((seq, 512), lambda i: (i, COL_D // 512)),
                  const(w.shape), const(cb.shape), const(g.shape), const(b.shape)],
        out_specs=pl.BlockSpec((seq, MIX_W), lambda i: (i, 0)),
        scratch_shapes=[pltpu.VMEM((SUBLANES, seq + 2 * CONV_PAD, D_CH), jnp.float32)],
        compiler_params=_cparams(("parallel",)),
    )(h, w, cb, g, b)


def _merge_kernel(gl_ref, ya_ref, yb_ref, yc_ref, yd_ref, x_ref, wb_ref, wo_ref, g_ref, b_ref,
                  of_ref, ob_ref):
    merged = None
    for n, y_ref in enumerate((ya_ref, yb_ref, yc_ref, yd_ref)):
        t = jnp.tanh(gl_ref[:, n * D_MODEL:(n + 1) * D_MODEL].astype(jnp.float32))
        term = (t + 1.0) * _dot(y_ref[...], wb_ref[n])
        merged = term if merged is None else merged + term
    mix = _dot(merged.astype(jnp.bfloat16), wo_ref[...])
    y = _layer_norm_rows(ALPHA * x_ref[...] + mix, g_ref[...], b_ref[...])
    of_ref[...] = y
    ob_ref[...] = y.astype(jnp.bfloat16)


def _merge(h, ya, yb, yc, yd, x, wb, wo, g, b, tm=512):
    t, d = x.shape
    row = lambda w: pl.BlockSpec((tm, w), lambda i: (i, 0))
    const = lambda shape: pl.BlockSpec(shape, lambda i: (0,) * len(shape))
    return pl.pallas_call(
        _merge_kernel,
        out_shape=(jax.ShapeDtypeStruct((t, d), jnp.float32), jax.ShapeDtypeStruct((t, d), jnp.bfloat16)),
        grid=(t // tm,),
        in_specs=[pl.BlockSpec((tm, N_BRANCH * d), lambda i: (i, COL_GATES)),
                  row(MIX_W), row(MIX_W), row(MIX_W), row(MIX_W), row(d),
                  const(wb.shape), const(wo.shape), const((1, d)), const((1, d))],
        out_specs=(row(d), row(d)),
        compiler_params=_cparams(("parallel",)),
    )(h, ya, yb, yc, yd, x, wb, wo, g.reshape(1, d), b.reshape(1, d))


def _ffn_kernel(xb_ref, xf_ref, w1_ref, w3_ref, w2_ref, g_ref, b_ref, of_ref, ob_ref):
    x = xb_ref[...]
    f = w2_ref.shape[0]
    split = (f // MXU_TILE + 1) // 2 * MXU_TILE
    acc = None
    for lo, hi in ((0, split), (split, f)):
        a = _dot(x, w1_ref[:, lo:hi])
        hid = (a * _sigmoid(a) * _dot(x, w3_ref[:, lo:hi])).astype(jnp.bfloat16)
        part = _dot(hid, w2_ref[lo:hi, :])
        acc = part if acc is None else acc + part
    y = _layer_norm_rows(ALPHA * xf_ref[...] + acc, g_ref[...], b_ref[...])
    of_ref[...] = y
    ob_ref[...] = y.astype(jnp.bfloat16)


def _ffn(xb, xf, w1, w3, w2, g, b, tm=512):
    t, d = xf.shape
    row = pl.BlockSpec((tm, d), lambda i: (i, 0))
    vec = pl.BlockSpec((1, d), lambda i: (0, 0))
    resident = lambda shape: pl.BlockSpec(shape, lambda i: (0, 0), pipeline_mode=pl.Buffered(1))
    return pl.pallas_call(
        _ffn_kernel,
        out_shape=(jax.ShapeDtypeStruct((t, d), jnp.float32), jax.ShapeDtypeStruct((t, d), jnp.bfloat16)),
        grid=(t // tm,),
        in_specs=[row, row, resident(w1.shape), resident(w3.shape), resident(w2.shape), vec, vec],
        out_specs=(row, row),
        compiler_params=_cparams(("parallel",)),
    )(xb, xf, w1, w3, w2, g.reshape(1, d), b.reshape(1, d))


def _pack_rows(x):
    half = x.shape[1] // 2
    bits = lax.bitcast_convert_type(x.astype(jnp.bfloat16).astype(jnp.float32), jnp.uint32)
    return (bits[:, :half] >> 16) | (bits[:, half:] & jnp.uint32(0xFFFF0000))


def _unpack_rows(p):
    lo = lax.bitcast_convert_type(p << 16, jnp.float32)
    hi = lax.bitcast_convert_type(p & jnp.uint32(0xFFFF0000), jnp.float32)
    return jnp.concatenate([lo, hi], axis=1)


def _router_kernel(x_ref, r_ref, tri_ref, oi_ref, ow_ref, cnt_ref, xp_ref, carry_ref):
    i = pl.program_id(0)

    @pl.when(i == 0)
    def _():
        carry_ref[...] = jnp.zeros_like(carry_ref)

    x = x_ref[...]
    r = r_ref[...]
    xh = x.astype(jnp.bfloat16)
    xl = (x - xh.astype(jnp.float32)).astype(jnp.bfloat16)
    rh = r.astype(jnp.bfloat16)
    rl = (r - rh.astype(jnp.float32)).astype(jnp.bfloat16)
    r2 = jnp.concatenate([rh, rl], axis=1)
    p = _dot(xh, r2) + _dot(xl, r2)
    logits = p[:, :LANES] + p[:, LANES:]
    tm = x.shape[0]
    lane = lax.broadcasted_iota(jnp.int32, (tm, LANES), 1)
    logits = jnp.where(lane < N_EXPERTS, logits, NEG_BIG)
    m1 = jnp.max(logits, axis=-1, keepdims=True)
    i1 = jnp.min(jnp.where(logits == m1, lane, LANES), axis=-1, keepdims=True)
    rest = jnp.where(lane == i1, NEG_BIG, logits)
    m2 = jnp.max(rest, axis=-1, keepdims=True)
    i2 = jnp.min(jnp.where(rest == m2, lane, LANES), axis=-1, keepdims=True)
    e = jnp.exp(m2 - m1)
    w1 = 1.0 / (1.0 + e)
    w2 = e / (1.0 + e)
    sel1 = lane == i1
    sel2 = lane == i2
    member = jnp.where(sel1 | sel2, 1.0, 0.0)
    before = _dot(tri_ref[...], member.astype(jnp.bfloat16)) + carry_ref[...]
    rank1 = jnp.sum(jnp.where(sel1, before, 0.0), axis=-1, keepdims=True).astype(jnp.int32)
    rank2 = jnp.sum(jnp.where(sel2, before, 0.0), axis=-1, keepdims=True).astype(jnp.int32)
    carry_ref[...] += jnp.sum(member, axis=0, keepdims=True)
    oi = jnp.where(lane == 0, i1, jnp.where(lane == 1, i2, jnp.where(lane == 2, rank1, rank2)))
    oi_ref[...] = oi.T[0:SUBLANES, :]
    ow_ref[...] = jnp.where(lane == 0, w1, w2)
    cnt_ref[...] = carry_ref[...]
    xp_ref[...] = _pack_rows(x)


def _router(xf, router, tm=512):
    t, d = xf.shape
    r_pad = jnp.zeros((d, LANES), jnp.float32).at[:, :N_EXPERTS].set(router.astype(jnp.float32))
    tri = jnp.asarray(np.tril(np.ones((tm, tm), np.float32), -1), jnp.bfloat16)
    return pl.pallas_call(
        _router_kernel,
        out_shape=(jax.ShapeDtypeStruct((SUBLANES, t), jnp.int32),
                   jax.ShapeDtypeStruct((t, LANES), jnp.float32),
                   jax.ShapeDtypeStruct((1, LANES), jnp.float32),
                   jax.ShapeDtypeStruct((t, d // 2), jnp.uint32)),
        grid=(t // tm,),
        in_specs=[pl.BlockSpec((tm, d), lambda i: (i, 0)),
                  pl.BlockSpec((d, LANES), lambda i: (0, 0)),
                  pl.BlockSpec((tm, tm), lambda i: (0, 0))],
        out_specs=(pl.BlockSpec((SUBLANES, tm), lambda i: (0, i)),
                   pl.BlockSpec((tm, LANES), lambda i: (i, 0)),
                   pl.BlockSpec((1, LANES), lambda i: (0, 0)),
                   pl.BlockSpec((tm, d // 2), lambda i: (i, 0))),
        scratch_shapes=[pltpu.VMEM((1, LANES), jnp.float32)],
        compiler_params=_cparams(("arbitrary",)),
    )(xf, r_pad, tri)


SC_CORES = 2
SC_SUBCORES = 16
SC_WINDOW = 64


def _sc_gather_rows(table, idx):
    m = idx.shape[0]
    d = table.shape[1]
    n_workers = SC_CORES * SC_SUBCORES
    per_w = m // n_workers
    n_win = per_w // SC_WINDOW
    assert per_w * n_workers == m and n_win * SC_WINDOW == per_w
    mesh = plsc.VectorSubcoreMesh(core_axis_name="c", subcore_axis_name="s")

    @functools.partial(
        pl.kernel, mesh=mesh,
        out_type=jax.ShapeDtypeStruct((m, d), table.dtype),
        scratch_types=[pltpu.VMEM((per_w,), jnp.int32),
                       pltpu.VMEM((SC_WINDOW, d), table.dtype),
                       pltpu.SemaphoreType.DMA],
    )
    def gather(table_hbm, idx_hbm, out_hbm, idx_v, rows_v, sem):
        wid = lax.axis_index("s") * SC_CORES + lax.axis_index("c")
        base = wid * per_w
        pltpu.sync_copy(idx_hbm.at[pl.ds(base, per_w)], idx_v)

        @pl.loop(0, n_win)
        def _(w):
            off = pl.multiple_of(w * SC_WINDOW, SC_WINDOW)
            pltpu.async_copy(table_hbm.at[idx_v.at[pl.ds(off, SC_WINDOW)]], rows_v, sem).wait()
            pltpu.sync_copy(rows_v, out_hbm.at[pl.ds(base + off, SC_WINDOW)])

    return gather(table, idx)


def _sc_scatter_rows(rows, idx, n_out):
    n_copies, m = idx.shape
    d = rows.shape[1]
    n_workers = SC_CORES * SC_SUBCORES
    per_w = m // n_workers
    n_win = per_w // SC_WINDOW
    assert per_w * n_workers == m and n_win * SC_WINDOW == per_w
    idx_w = idx.reshape(n_copies, n_workers, n_win, SC_WINDOW).transpose(1, 0, 2, 3)
    idx_w = idx_w.reshape(n_workers, n_copies * n_win, SC_WINDOW)
    mesh = plsc.VectorSubcoreMesh(core_axis_name="c", subcore_axis_name="s")

    @functools.partial(
        pl.kernel, mesh=mesh,
        out_type=jax.ShapeDtypeStruct((n_out, d), rows.dtype),
        scratch_types=[pltpu.VMEM((n_copies * n_win, SC_WINDOW), jnp.int32),
                       pltpu.VMEM((SC_WINDOW, d), rows.dtype)],
    )
    def scatter(rows_hbm, idx_hbm, out_hbm, idx_v, rows_v):
        wid = lax.axis_index("s") * SC_CORES + lax.axis_index("c")
        base = wid * per_w
        pltpu.sync_copy(idx_hbm.at[wid], idx_v)

        @pl.loop(0, n_win)
        def _(w):
            off = pl.multiple_of(w * SC_WINDOW, SC_WINDOW)
            pltpu.sync_copy(rows_hbm.at[pl.ds(base + off, SC_WINDOW)], rows_v)
            for k in range(n_copies):
                pltpu.sync_copy(rows_v, out_hbm.at[idx_v.at[k * n_win + w]])

    return scatter(rows, idx_w)


def _expert_kernel(te_ref, nv_ref, x_ref, w1_ref, w3_ref, w2_ref, o_ref,
                   w13s_ref, w2s_ref, xb_ref, acc_ref):
    i = pl.program_id(0)
    j = pl.program_id(1)
    last = pl.num_programs(1) - 1
    n_valid = nv_ref[i]
    n_piece = MOE_TILE // MOE_SUB
    piece_rows = [pl.ds(s * MOE_SUB, MOE_SUB) for s in range(n_piece)]

    for s, rows in enumerate(piece_rows):
        @pl.when((n_valid > s * MOE_SUB) & (j == 0))
        def _():
            row = lax.broadcasted_iota(jnp.int32, (MOE_SUB, 1), 0) + s * MOE_SUB
            x = jnp.where(row < n_valid, _unpack_rows(x_ref[rows, :]), 0.0)
            xb_ref[rows, :] = x.astype(jnp.bfloat16)
            acc_ref[rows, :] = jnp.zeros((MOE_SUB, acc_ref.shape[1]), jnp.float32)

    def compute(n_live):
        w13s_ref[:, 0:MOE_FCHUNK] = w1_ref[0].astype(jnp.bfloat16)
        w13s_ref[:, MOE_FCHUNK:2 * MOE_FCHUNK] = w3_ref[0].astype(jnp.bfloat16)
        w2s_ref[...] = w2_ref[0].astype(jnp.bfloat16)
        for rows in piece_rows[:n_live]:
            ab = _dot(xb_ref[rows, :], w13s_ref[...])
            a = ab[:, :MOE_FCHUNK]
            hid = (a * _sigmoid(a) * ab[:, MOE_FCHUNK:]).astype(jnp.bfloat16)
            acc_ref[rows, :] += _dot(hid, w2s_ref[...])

    for n_live in range(1, n_piece + 1):
        upper = n_live * MOE_SUB if n_live < n_piece else MOE_TILE
        pl.when((n_valid > (n_live - 1) * MOE_SUB) & (n_valid <= upper))(functools.partial(compute, n_live))

    for s, rows in enumerate(piece_rows):
        @pl.when((n_valid > s * MOE_SUB) & (j == last))
        def _():
            o_ref[rows, :] = _pack_rows(acc_ref[rows, :])

        @pl.when((n_valid <= s * MOE_SUB) & (j == last))
        def _():
            o_ref[rows, :] = jnp.zeros((MOE_SUB, o_ref.shape[1]), o_ref.dtype)


def _experts(tile_expert, tile_valid, xs, w1, w3, w2):
    n_rows = xs.shape[0]
    d, f = w1.shape[1], w1.shape[2]
    nj = f // MOE_FCHUNK

    def chunk(i, j, nv):
        return jnp.where(nv[i] > 0, j, nj - 1)

    return pl.pallas_call(
        _expert_kernel,
        out_shape=jax.ShapeDtypeStruct((n_rows, d // 2), jnp.uint32),
        grid_spec=pltpu.PrefetchScalarGridSpec(
            num_scalar_prefetch=2,
            grid=(n_rows // MOE_TILE, nj),
            in_specs=[pl.BlockSpec((MOE_TILE, d // 2), lambda i, j, te, nv: (i, 0)),
                      pl.BlockSpec((1, d, MOE_FCHUNK), lambda i, j, te, nv: (te[i], 0, chunk(i, j, nv))),
                      pl.BlockSpec((1, d, MOE_FCHUNK), lambda i, j, te, nv: (te[i], 0, chunk(i, j, nv))),
                      pl.BlockSpec((1, MOE_FCHUNK, d), lambda i, j, te, nv: (te[i], chunk(i, j, nv), 0))],
            out_specs=pl.BlockSpec((MOE_TILE, d // 2), lambda i, j, te, nv: (i, 0)),
            scratch_shapes=[pltpu.VMEM((d, 2 * MOE_FCHUNK), jnp.bfloat16),
                            pltpu.VMEM((MOE_FCHUNK, d), jnp.bfloat16),
                            pltpu.VMEM((MOE_TILE, d), jnp.bfloat16),
                            pltpu.VMEM((MOE_TILE, d), jnp.float32)],
        ),
        compiler_params=pltpu.CompilerParams(dimension_semantics=("arbitrary", "arbitrary"),
                                             vmem_limit_bytes=MOE_VMEM_LIMIT),
    )(tile_expert, tile_valid, xs, w1, w3, w2)


def _combine_kernel(ya_ref, yb_ref, x_ref, w_ref, g_ref, b_ref, o_ref):
    w = w_ref[...]
    f = w[:, 0:1] * _unpack_rows(ya_ref[...]) + w[:, 1:2] * _unpack_rows(yb_ref[...])
    o_ref[...] = _layer_norm_rows(ALPHA * x_ref[...] + f, g_ref[...], b_ref[...])


def _combine(yg, xf, wts, g, b, tm=512):
    t, d = xf.shape
    nt = t // tm
    return pl.pallas_call(
        _combine_kernel,
        out_shape=jax.ShapeDtypeStruct((t, d), jnp.float32),
        grid=(nt,),
        in_specs=[pl.BlockSpec((tm, d // 2), lambda i: (i, 0)),
                  pl.BlockSpec((tm, d // 2), lambda i: (i + nt, 0)),
                  pl.BlockSpec((tm, d), lambda i: (i, 0)),
                  pl.BlockSpec((tm, LANES), lambda i: (i, 0)),
                  pl.BlockSpec((1, d), lambda i: (0, 0)),
                  pl.BlockSpec((1, d), lambda i: (0, 0))],
        out_specs=pl.BlockSpec((tm, d), lambda i: (i, 0)),
        compiler_params=_cparams(("parallel",)),
    )(yg, yg, xf, wts, g.reshape(1, d), b.reshape(1, d))


def _moe_ffn(xf, router, w1, w3, w2, g, b):
    t, d = xf.shape
    oi, ow, cnt, xp = _router(xf, router)
    e1, e2, rank1, rank2 = oi[0], oi[1], oi[2], oi[3]
    counts = cnt[0, :N_EXPERTS].astype(jnp.int32)
    tiles = (counts + MOE_TILE - 1) // MOE_TILE
    tile_end = jnp.cumsum(tiles)
    tile_start = tile_end - tiles
    offs = tile_start * MOE_TILE
    dest = jnp.concatenate([offs[e1] + rank1, offs[e2] + rank2])
    n_rows = TOP_K * t + N_EXPERTS * MOE_TILE
    n_tiles = n_rows // MOE_TILE
    tile_id = jnp.arange(n_tiles, dtype=jnp.int32)
    tile_expert = jnp.sum(tile_id[:, None] >= tile_end[None, :], axis=1)
    used = tile_expert < N_EXPERTS
    last_expert = jnp.sum(tile_end[-1] - 1 >= tile_end)
    tile_expert = jnp.where(used, tile_expert, last_expert).astype(jnp.int32)
    tile_valid = jnp.clip(counts[tile_expert] - (tile_id - tile_start[tile_expert]) * MOE_TILE, 0, MOE_TILE)
    tile_valid = jnp.where(used, tile_valid, 0).astype(jnp.int32)
    xs = _sc_scatter_rows(xp, dest.reshape(TOP_K, t), n_rows)
    y = _experts(tile_expert, tile_valid, xs, w1, w3, w2)
    yg = _sc_gather_rows(y, dest)
    return _combine(yg, xf, ow, g, b)


def _prep_layer(l, p, in_cols):
    bf = jnp.bfloat16
    out = {}
    n_gate = N_BRANCH * D_MODEL
    w_l = p["w_in_bf16"][l]
    out["w_gate"] = 0.5 * w_l[:, int(in_cols[0]):int(in_cols[0]) + n_gate]
    out["w_rest"] = _take_cols(w_l, in_cols[n_gate:])
    deint = np.concatenate([np.arange(0, HEAD_DIM, 2), np.arange(1, HEAD_DIM, 2)])
    out["gq_a"] = jnp.tile(p["a_q_norm"][l][deint], 2).reshape(1, LANES)
    out["gk_a"] = jnp.tile(p["a_k_norm"][l][deint], 2).reshape(1, LANES)
    per_head = C_NOPE + C_ROPE
    uq_cols = []
    for h in range(C_HEADS):
        base = h * per_head
        uq_cols += [base + np.arange(32), base + C_NOPE + np.arange(0, C_ROPE, 2), np.full(16, -1),
                    base + 32 + np.arange(32), base + C_NOPE + np.arange(1, C_ROPE, 2), np.full(16, -1)]
    wuq = _take_cols(p["c_w_uq"][l], np.concatenate(uq_cols))
    out["wuq"] = jnp.concatenate([wuq, jnp.zeros((64, wuq.shape[1]), wuq.dtype)], axis=0).astype(bf)
    uk_cols, uv_cols = [], []
    for h in range(C_HEADS):
        base = h * (C_NOPE + C_V)
        uk_cols += [base + np.arange(32), np.full(32, -1), base + 32 + np.arange(32), np.full(32, -1)]
        uv_cols += [base + C_NOPE + np.arange(C_V)]
    out["wuk"] = _take_cols(p["c_w_ukv"][l], np.concatenate(uk_cols)).astype(bf)
    out["wuv"] = _take_cols(p["c_w_ukv"][l], np.concatenate(uv_cols)).astype(bf)
    out["gq_c"] = jnp.concatenate([p["c_q_norm"][l], jnp.zeros((64,), jnp.float32)]).reshape(1, 256)
    out["gkv_c"] = p["c_kv_norm"][l].reshape(1, C_KV_RANK)
    out["conv_w"] = jnp.concatenate([p["d_conv_w"][l][:, 0, :], jnp.zeros((1, D_CH), jnp.float32)], axis=0)
    out["conv_b"] = p["d_conv_b"][l].reshape(1, D_CH)
    out["d_ln_g"] = p["d_ln_g"][l].reshape(1, D_CH)
    out["d_ln_b"] = p["d_ln_b"][l].reshape(1, D_CH)
    a_rows = np.concatenate([(kvh * 2 + g) * HEAD_DIM + np.arange(HEAD_DIM)
                             for g in range(2) for kvh in range(A_KV_HEADS)])
    wb = p["w_branch"][l]
    out["w_branch"] = (0.5 * jnp.stack([wb[0][a_rows], wb[1], wb[2], wb[3]], axis=0)).astype(bf)
    out["w_out"] = p["w_out"][l].astype(bf)
    return out


def kernel(x, ln_in_g, ln_in_b, w_in, a_q_norm, a_k_norm, b_rpb, c_q_norm, c_kv_norm, c_w_uq, c_w_ukv,
           d_conv_w, d_conv_b, d_ln_g, d_ln_b, w_branch, w_out, ln_mix_g, ln_mix_b,
           ffn_w1, ffn_w3, ffn_w2, moe_router, moe_w1, moe_w3, moe_w2, ln_ffn_g, ln_ffn_b):
    batch, seq, d = x.shape
    t = batch * seq
    bf = jnp.bfloat16
    params = dict(w_in_bf16=w_in.astype(bf), a_q_norm=a_q_norm, a_k_norm=a_k_norm, c_q_norm=c_q_norm, c_kv_norm=c_kv_norm,
                  c_w_uq=c_w_uq, c_w_ukv=c_w_ukv, d_conv_w=d_conv_w, d_conv_b=d_conv_b, d_ln_g=d_ln_g,
                  d_ln_b=d_ln_b, w_branch=w_branch, w_out=w_out)
    in_cols = _in_proj_columns()
    cos_a, sin_a, cos_c, sin_c = _rope_tables(seq)
    na_bias = _na_bias_tables(b_rpb, seq // GRID_W)
    xf = xb = None
    for l in range(DEPTH):
        p = _prep_layer(l, params, in_cols)
        if l == 0:
            h, xf = _in_proj(x.reshape(t, d), p["w_gate"], p["w_rest"], ln=(ln_in_g, ln_in_b))
        else:
            h = _in_proj(xb, p["w_gate"], p["w_rest"])
        ya = _attn_a(h, cos_a, sin_a, p["gq_a"], p["gk_a"], batch, seq)
        yb = _attn_b(h, na_bias, l, batch, seq)
        yc = _attn_c(h, p["wuq"], p["wuk"], p["wuv"], p["gq_c"], p["gkv_c"], cos_c, sin_c, batch, seq)
        yd = _conv_branch(h, p["conv_w"], p["conv_b"], p["d_ln_g"], p["d_ln_b"], batch, seq)
        xf, xb = _merge(h, ya, yb, yc, yd, xf, p["w_branch"], p["w_out"], ln_mix_g[l], ln_mix_b[l])
        if l % 2 == 0:
            i = l // 2
            xf, xb = _ffn(xb, xf, ffn_w1[i].astype(bf), ffn_w3[i].astype(bf), ffn_w2[i].astype(bf),
                          ln_ffn_g[l], ln_ffn_b[l])
        else:
            i = l // 2
            xf = _moe_ffn(xf, moe_router[i], moe_w1[i], moe_w3[i], moe_w2[i], ln_ffn_g[l], ln_ffn_b[l])
            xb = xf.astype(bf)
    return xf.reshape(batch, seq, d)
```

```python
import functools

import numpy as np
import jax
import jax.numpy as jnp
from jax import lax
from jax.experimental import pallas as pl
from jax.experimental.pallas import tpu as pltpu
from jax.experimental.pallas import tpu_sc as plsc

D_MODEL = 1024
DEPTH = 2
GRID_W = 64
HEAD_DIM = 64
ROPE_THETA = 10000.0
RMS_EPS = 1e-6
LN_EPS = 1e-5
A_HEADS = 4
A_KV_HEADS = 2
B_HEADS = 4
NA_WIN_H = 8
NA_WIN_W = 16
C_HEADS = 4
C_NOPE = 64
C_ROPE = 32
C_V = 64
C_Q_RANK = 192
C_KV_RANK = 128
D_CH = 256
CONV_W = 31
N_BRANCH = 4
MIX_W = 256
D_FF = 2816
N_EXPERTS = 8
TOP_K = 2
D_FF_EXPERT = 3584
ALPHA = (2 * DEPTH) ** 0.25

LANES = 128
SUBLANES = 8
MXU_TILE = 256
VMEM_LIMIT = 48 * 1024 * 1024

LOG2E = 1.4426950408889634
NEG_BIG = -1e30

COL_GATES = 0
COL_AQ = 4096
COL_AKV = 4352
COL_C = 4608
COL_D = 5120
COL_BQ = 5632
COL_BK = 5888
COL_BV = 6144
D_IN_PAD = 6400

NA_QROWS = 4
NA_KROWS = 12

MOE_TILE = 2048
MOE_SUB = 512
MOE_FCHUNK = 512
MOE_VMEM_LIMIT = 56 * 1024 * 1024


def _in_proj_columns():
    o_aq, o_ak, o_av = 0, 256, 384
    o_bq, o_bk, o_bv = 512, 768, 1024
    o_cq, o_ckv, o_kpe = 1280, 1472, 1600
    o_d, o_g = 1632, 2144
    deint = np.concatenate([np.arange(0, HEAD_DIM, 2), np.arange(1, HEAD_DIM, 2)])
    cols = [o_g + np.arange(N_BRANCH * D_MODEL)]
    half = HEAD_DIM // 2
    for g in range(2):
        for part in range(2):
            for kvh in range(A_KV_HEADS):
                cols.append(o_aq + (kvh * 2 + g) * HEAD_DIM + deint[part * half:(part + 1) * half])
    for part in range(2):
        for kvh in range(A_KV_HEADS):
            cols.append(o_ak + kvh * HEAD_DIM + deint[part * half:(part + 1) * half])
    cols.append(o_av + np.arange(A_KV_HEADS * HEAD_DIM))
    cols.append(o_cq + np.arange(C_Q_RANK))
    cols.append(np.full(64, -1))
    cols.append(o_ckv + np.arange(C_KV_RANK))
    cols.append(np.full(32, -1))
    cols.append(o_kpe + np.arange(0, C_ROPE, 2))
    cols.append(np.full(48, -1))
    cols.append(o_kpe + np.arange(1, C_ROPE, 2))
    cols.append(np.full(16, -1))
    cols.append(o_d + np.arange(2 * D_CH))
    cols.append(o_bq + np.arange(3 * B_HEADS * HEAD_DIM))
    cols = np.concatenate(cols)
    assert cols.shape[0] == D_IN_PAD
    return cols


def _take_cols(w, cols):
    cols = np.asarray(cols)
    runs, i, n = [], 0, len(cols)
    while i < n:
        j = i + 1
        if cols[i] < 0:
            while j < n and cols[j] < 0:
                j += 1
            kind = "zero"
        else:
            while j < n and cols[j] == cols[j - 1] + 1:
                j += 1
            kind = "slice" if j - i >= 64 else "shuffle"
        if kind == "shuffle" and runs and runs[-1][0] == "shuffle":
            runs[-1] = ("shuffle", runs[-1][1], j)
        else:
            runs.append((kind, i, j))
        i = j
    parts = []
    for kind, i, j in runs:
        if kind == "zero":
            parts.append(jnp.zeros((w.shape[0], j - i), w.dtype))
        elif kind == "slice":
            parts.append(w[:, int(cols[i]):int(cols[i]) + j - i])
        else:
            lo, hi = int(cols[i:j].min()), int(cols[i:j].max()) + 1
            onehot = np.zeros((hi - lo, j - i), np.float32)
            onehot[cols[i:j] - lo, np.arange(j - i)] = 1.0
            parts.append(jnp.dot(w[:, lo:hi], jnp.asarray(onehot, w.dtype), precision=lax.Precision.HIGHEST))
    return jnp.concatenate(parts, axis=1)


def _rope_tables(seq):
    t = np.arange(seq)
    row = (t // GRID_W).astype(np.float32)
    col = (t % GRID_W).astype(np.float32)

    def angles(rot_dim):
        n_freq = rot_dim // 4
        inv = jnp.asarray(ROPE_THETA, jnp.float32) ** (-jnp.arange(n_freq, dtype=jnp.float32) / n_freq)
        return jnp.concatenate([jnp.asarray(row)[:, None] * inv, jnp.asarray(col)[:, None] * inv], axis=-1)

    ang_a = angles(HEAD_DIM)
    ca, sa = jnp.cos(ang_a), jnp.sin(ang_a)
    cos_a = jnp.concatenate([ca, ca, ca, ca], axis=-1)
    sin_a = jnp.concatenate([-sa, -sa, sa, sa], axis=-1)
    ang_c = angles(C_ROPE)
    cc, sc = jnp.cos(ang_c), jnp.sin(ang_c)
    one = jnp.ones((seq, 64), jnp.float32)
    zero = jnp.zeros((seq, 64), jnp.float32)
    cos_c = jnp.concatenate([one[:, :32], cc, one[:, :48], cc, one[:, :16]], axis=-1)
    sin_c = jnp.concatenate([zero[:, :32], -sc, zero[:, :48], sc, zero[:, :16]], axis=-1)
    return cos_a, sin_a, cos_c, sin_c


def _na_bias_tables(rpb, rows):
    wh = min(NA_WIN_H, rows)
    n_dr, n_dc = 2 * NA_WIN_H - 1, 2 * NA_WIN_W - 1
    qc = np.arange(GRID_W)[:, None]
    kc = np.arange(GRID_W)[None, :]
    col_start = np.clip(qc - NA_WIN_W // 2, 0, GRID_W - NA_WIN_W)
    ok_c = (kc >= col_start) & (kc < col_start + NA_WIN_W)
    dc = np.clip(kc - qc + (NA_WIN_W - 1), 0, n_dc - 1)
    oh_c = np.eye(n_dc, dtype=np.float32)[dc.reshape(-1)]
    oh_r, ok_r = [], []
    for r0 in (0, NA_QROWS, rows - NA_QROWS):
        ws = int(np.clip(r0 - 4, 0, rows - NA_KROWS))
        qr = r0 + np.arange(NA_QROWS)[:, None]
        kr = ws + np.arange(NA_KROWS)[None, :]
        row_start = np.clip(qr - wh // 2, 0, rows - wh)
        ok_r.append((kr >= row_start) & (kr < row_start + wh))
        dr = np.clip(kr - qr + (NA_WIN_H - 1), 0, n_dr - 1)
        oh_r.append(np.eye(n_dr, dtype=np.float32)[dr.reshape(-1)])
    oh_r = np.concatenate(oh_r, axis=0)
    ok = np.stack(ok_r)[:, :, None, :, None] & ok_c[None, None, :, None, :]
    ok = ok.reshape(3, 1, NA_QROWS * GRID_W, NA_KROWS * GRID_W)
    hi = lax.Precision.HIGHEST
    t1 = jnp.einsum("ma,lhab->lhmb", jnp.asarray(oh_r), rpb.astype(jnp.float32) * LOG2E, precision=hi)
    t2 = jnp.einsum("lhmb,nb->lhmn", t1, jnp.asarray(oh_c), precision=hi).astype(jnp.bfloat16)
    nl, nh = rpb.shape[0], rpb.shape[1]
    t2 = t2.reshape(nl, nh, 3, NA_QROWS, NA_KROWS, GRID_W, GRID_W).transpose(0, 2, 1, 3, 5, 4, 6)
    bias = t2.reshape(nl, 3, nh, NA_QROWS * GRID_W, NA_KROWS * GRID_W)
    return jnp.where(jnp.asarray(ok), bias, jnp.asarray(NEG_BIG, jnp.bfloat16))


def _cparams(sem):
    return pltpu.CompilerParams(dimension_semantics=sem, vmem_limit_bytes=VMEM_LIMIT)


def _layer_norm_rows(z, g, b):
    mu = jnp.mean(z, axis=-1, keepdims=True)
    zc = z - mu
    var = jnp.mean(zc * zc, axis=-1, keepdims=True)
    return zc * lax.rsqrt(var + LN_EPS) * g + b


def _sigmoid(x):
    return 0.5 * jnp.tanh(0.5 * x) + 0.5


def _dot(a, b):
    return jnp.dot(a, b, preferred_element_type=jnp.float32)


def _dot_nt(a, b):
    return lax.dot_general(a, b, (((1,), (1,)), ((), ())), preferred_element_type=jnp.float32)


IN_PROJ_CHUNK = 1280


def _project_chunks(xb, w_refs, h_ref):
    off = 0
    for w_ref in w_refs:
        n_tiles = w_ref.shape[1] // MXU_TILE
        n_chunks = -(-w_ref.shape[1] // IN_PROJ_CHUNK)
        lo = 0
        for c in range(n_chunks):
            hi = (n_tiles * (c + 1) // n_chunks) * MXU_TILE
            h_ref[:, off + lo:off + hi] = _dot(xb, w_ref[:, lo:hi]).astype(h_ref.dtype)
            lo = hi
        off += w_ref.shape[1]


def _in_proj_kernel(x_ref, wg_ref, wr_ref, h_ref):
    _project_chunks(x_ref[...], (wg_ref, wr_ref), h_ref)


def _in_proj_ln_kernel(x_ref, g_ref, b_ref, wg_ref, wr_ref, h_ref, xf_ref):
    y = _layer_norm_rows(x_ref[...], g_ref[...], b_ref[...])
    xf_ref[...] = y
    _project_chunks(y.astype(jnp.bfloat16), (wg_ref, wr_ref), h_ref)


def _in_proj(x, w_gate, w_rest, ln=None, tm=512):
    t, d = x.shape
    n = w_gate.shape[1] + w_rest.shape[1]
    row = lambda width: pl.BlockSpec((tm, width), lambda i: (i, 0))
    vec = pl.BlockSpec((1, d), lambda i: (0, 0))
    resident = lambda w: pl.BlockSpec(w.shape, lambda i: (0, 0), pipeline_mode=pl.Buffered(1))
    h_shape = jax.ShapeDtypeStruct((t, n), jnp.bfloat16)
    if ln is None:
        return pl.pallas_call(
            _in_proj_kernel, out_shape=h_shape, grid=(t // tm,),
            in_specs=[row(d), resident(w_gate), resident(w_rest)], out_specs=row(n),
            compiler_params=_cparams(("parallel",)),
        )(x, w_gate, w_rest)
    g, b = ln
    return pl.pallas_call(
        _in_proj_ln_kernel,
        out_shape=(h_shape, jax.ShapeDtypeStruct((t, d), jnp.float32)),
        grid=(t // tm,),
        in_specs=[row(d), vec, vec, resident(w_gate), resident(w_rest)], out_specs=(row(n), row(d)),
        compiler_params=_cparams(("parallel",)),
    )(x, g.reshape(1, d), b.reshape(1, d), w_gate, w_rest)


def _ones_halves(vblk):
    lane = lax.broadcasted_iota(jnp.int32, vblk.shape, 1)
    ones = jnp.ones_like(vblk)
    return jnp.where(lane < HEAD_DIM, vblk, ones), jnp.where(lane < HEAD_DIM, ones, vblk)


def _attend_pair(s_lo, s_hi, v_lo, v_hi, lo_q):
    outs = []
    for s, v in ((s_lo, v_lo), (s_hi, v_hi)):
        p = jnp.exp2(s - jnp.max(s, axis=-1, keepdims=True)).astype(jnp.bfloat16)
        outs.append(_dot(p, v))
    num = jnp.where(lo_q, outs[0], outs[1])
    den = jnp.where(lo_q, pltpu.roll(outs[0], HEAD_DIM, 1), pltpu.roll(outs[1], HEAD_DIM, 1))
    return num / den


def _attn_a_kernel(q_ref, kv_ref, cos_ref, sin_ref, gq_ref, gk_ref, o_ref, ks_ref, va_ref, *, tq):
    s_len = q_ref.shape[0]

    def first_head(shape):
        lane = lax.broadcasted_iota(jnp.int32, shape, 1)
        return (lane % HEAD_DIM) < HEAD_DIM // 2

    def prep(x, g, cos, sin):
        head_a = first_head(x.shape)
        x2 = x * x
        ss_a = jnp.sum(jnp.where(head_a, x2, 0.0), axis=-1, keepdims=True)
        ss_b = jnp.sum(jnp.where(head_a, 0.0, x2), axis=-1, keepdims=True)
        ms = jnp.where(head_a, ss_a, ss_b) * (1.0 / HEAD_DIM)
        xn = x * lax.rsqrt(ms + RMS_EPS) * g
        return xn * cos + pltpu.roll(xn, LANES // 2, 1) * sin

    ks_ref[...] = prep(kv_ref[:, 0:LANES].astype(jnp.float32), gk_ref[...],
                       cos_ref[...], sin_ref[...]).astype(jnp.bfloat16)
    scale = HEAD_DIM ** -0.5 * LOG2E
    lane_q = lax.broadcasted_iota(jnp.int32, (tq, LANES), 1)
    lo_q = lane_q < HEAD_DIM
    head_a_q = first_head((tq, LANES))

    va_ref[0], va_ref[1] = _ones_halves(kv_ref[:, LANES:2 * LANES])

    def body(i, carry):
        r = pl.multiple_of(i * tq, tq)
        k = ks_ref[...]
        cos = cos_ref[pl.ds(r, tq), :]
        sin = sin_ref[pl.ds(r, tq), :]
        for g in range(2):
            qt = prep(q_ref[pl.ds(r, tq), g * LANES:(g + 1) * LANES].astype(jnp.float32), gq_ref[...], cos, sin)
            qt = (qt * scale).astype(jnp.bfloat16)
            zero = jnp.zeros_like(qt)
            out = _attend_pair(_dot_nt(jnp.where(head_a_q, qt, zero), k), _dot_nt(jnp.where(head_a_q, zero, qt), k),
                               va_ref[0], va_ref[1], lo_q)
            o_ref[pl.ds(r, tq), g * LANES:(g + 1) * LANES] = out.astype(o_ref.dtype)
        return carry

    lax.fori_loop(0, s_len // tq, body, 0, unroll=4)


def _attn_a(h, cos_a, sin_a, gq, gk, batch, seq, tq=256):
    kern = functools.partial(_attn_a_kernel, tq=tq)
    return pl.pallas_call(
        kern,
        out_shape=jax.ShapeDtypeStruct((batch * seq, MIX_W), jnp.bfloat16),
        grid=(batch,),
        in_specs=[pl.BlockSpec((seq, 256), lambda b: (b, COL_AQ // 256)),
                  pl.BlockSpec((seq, 256), lambda b: (b, COL_AKV // 256)),
                  pl.BlockSpec((seq, LANES), lambda b: (0, 0)),
                  pl.BlockSpec((seq, LANES), lambda b: (0, 0)),
                  pl.BlockSpec((1, LANES), lambda b: (0, 0)),
                  pl.BlockSpec((1, LANES), lambda b: (0, 0))],
        out_specs=pl.BlockSpec((seq, MIX_W), lambda b: (b, 0)),
        scratch_shapes=[pltpu.VMEM((seq, LANES), jnp.bfloat16), pltpu.VMEM((2, seq, LANES), jnp.bfloat16)],
        compiler_params=_cparams(("parallel",)),
    )(h, h, cos_a, sin_a, gq, gk)


def _attn_b_kernel(q_ref, k_ref, v_ref, bias_ref, o_ref, va_ref, *, rows):
    nq = NA_QROWS * GRID_W
    nk = NA_KROWS * GRID_W
    lane_q = lax.broadcasted_iota(jnp.int32, (nq, LANES), 1)
    lo_q = lane_q < HEAD_DIM
    n_groups = rows // NA_QROWS
    scale = HEAD_DIM ** -0.5 * LOG2E
    for blk in range(B_HEADS // 2):
        va_ref[2 * blk], va_ref[2 * blk + 1] = _ones_halves(v_ref[:, blk * LANES:(blk + 1) * LANES])

    def body(gi, carry):
        ws = jnp.clip(gi * NA_QROWS - 4, 0, rows - NA_KROWS)
        cls = jnp.where(gi == 0, 0, jnp.where(gi == n_groups - 1, 2, 1))
        qr = pl.multiple_of(gi * nq, nq)
        kr = pl.multiple_of(ws * GRID_W, GRID_W)
        for blk in range(B_HEADS // 2):
            sl = slice(blk * LANES, (blk + 1) * LANES)
            qt = q_ref[pl.ds(qr, nq), sl] * scale
            kw = k_ref[pl.ds(kr, nk), sl]
            zero = jnp.zeros_like(qt)
            s_lo = _dot_nt(jnp.where(lo_q, qt, zero), kw) + bias_ref[cls, 2 * blk].astype(jnp.float32)
            s_hi = _dot_nt(jnp.where(lo_q, zero, qt), kw) + bias_ref[cls, 2 * blk + 1].astype(jnp.float32)
            out = _attend_pair(s_lo, s_hi, va_ref[2 * blk, pl.ds(kr, nk), :], va_ref[2 * blk + 1, pl.ds(kr, nk), :], lo_q)
            o_ref[pl.ds(qr, nq), sl] = out.astype(o_ref.dtype)
        return carry

    lax.fori_loop(0, n_groups, body, 0, unroll=True)


def _attn_b(h, bias, layer, batch, seq):
    rows = seq // GRID_W
    kern = functools.partial(_attn_b_kernel, rows=rows)
    return pl.pallas_call(
        kern,
        out_shape=jax.ShapeDtypeStruct((batch * seq, MIX_W), jnp.bfloat16),
        grid=(batch,),
        in_specs=[pl.BlockSpec((seq, 256), lambda b: (b, COL_BQ // 256)),
                  pl.BlockSpec((seq, 256), lambda b: (b, COL_BK // 256)),
                  pl.BlockSpec((seq, 256), lambda b: (b, COL_BV // 256)),
                  pl.BlockSpec((None,) + bias.shape[1:], lambda b: (layer, 0, 0, 0, 0))],
        out_specs=pl.BlockSpec((seq, MIX_W), lambda b: (b, 0)),
        scratch_shapes=[pltpu.VMEM((B_HEADS, seq, LANES), jnp.bfloat16)],
        compiler_params=_cparams(("parallel",)),
    )(h, h, h, bias)


def _attn_c_kernel(hc_ref, wuq_ref, wuk_ref, wuv_ref, gq_ref, gkv_ref, cos_ref, sin_ref, o_ref,
                   qs_ref, ks_ref, vs_ref, *, tq):
    s_len = hc_ref.shape[0]
    cos = cos_ref[...]
    sin = sin_ref[...]

    def rotate(x):
        return x * cos + pltpu.roll(x, LANES // 2, 1) * sin

    cq = hc_ref[:, 0:256].astype(jnp.float32)
    ms = jnp.sum(cq * cq, axis=-1, keepdims=True) * (1.0 / C_Q_RANK)
    cqn = (cq * lax.rsqrt(ms + RMS_EPS) * gq_ref[...]).astype(jnp.bfloat16)
    ckv = hc_ref[:, 256:384].astype(jnp.float32)
    ms = jnp.mean(ckv * ckv, axis=-1, keepdims=True)
    kvn = (ckv * lax.rsqrt(ms + RMS_EPS) * gkv_ref[...]).astype(jnp.bfloat16)
    kpe = rotate(hc_ref[:, 384:512].astype(jnp.float32))
    v_all = _dot(kvn, wuv_ref[...]).astype(jnp.bfloat16)
    for blk in range(C_HEADS // 2):
        vs_ref[2 * blk], vs_ref[2 * blk + 1] = _ones_halves(v_all[:, blk * LANES:(blk + 1) * LANES])
    scale = (C_NOPE + C_ROPE) ** -0.5 * LOG2E
    for h in range(C_HEADS):
        sl = slice(h * LANES, (h + 1) * LANES)
        qh = rotate(_dot(cqn, wuq_ref[:, sl])) * scale
        qs_ref[:, sl] = qh.astype(jnp.bfloat16)
        ks_ref[:, sl] = (_dot(kvn, wuk_ref[:, sl]) + kpe).astype(jnp.bfloat16)

    lane_q = lax.broadcasted_iota(jnp.int32, (tq, LANES), 1)
    lo_q = lane_q < C_V

    def body(i, carry):
        r = pl.multiple_of(i * tq, tq)
        for blk in range(C_HEADS // 2):
            s = []
            for par in range(2):
                sl = slice((2 * blk + par) * LANES, (2 * blk + par + 1) * LANES)
                s.append(_dot_nt(qs_ref[pl.ds(r, tq), sl], ks_ref[:, sl]))
            out = _attend_pair(s[0], s[1], vs_ref[2 * blk], vs_ref[2 * blk + 1], lo_q)
            o_ref[pl.ds(r, tq), blk * LANES:(blk + 1) * LANES] = out.astype(o_ref.dtype)
        return carry

    lax.fori_loop(0, s_len // tq, body, 0, unroll=4)


def _attn_c(h, wuq, wuk, wuv, gq, gkv, cos_c, sin_c, batch, seq, tq=256):
    kern = functools.partial(_attn_c_kernel, tq=tq)
    const = lambda shape: pl.BlockSpec(shape, lambda b: (0,) * len(shape))
    return pl.pallas_call(
        kern,
        out_shape=jax.ShapeDtypeStruct((batch * seq, MIX_W), jnp.bfloat16),
        grid=(batch,),
        in_specs=[pl.BlockSpec((seq, 512), lambda b: (b, COL_C // 512)),
                  const(wuq.shape), const(wuk.shape), const(wuv.shape),
                  const(gq.shape), const(gkv.shape), const(cos_c.shape), const(sin_c.shape)],
        out_specs=pl.BlockSpec((seq, MIX_W), lambda b: (b, 0)),
        scratch_shapes=[pltpu.VMEM((seq, C_HEADS * LANES), jnp.bfloat16),
                        pltpu.VMEM((seq, C_HEADS * LANES), jnp.bfloat16),
                        pltpu.VMEM((C_HEADS, seq, LANES), jnp.bfloat16)],
        compiler_params=_cparams(("parallel",)),
    )(h, wuq, wuk, wuv, gq, gkv, cos_c, sin_c)


CONV_PAD = 16
CONV_ROWS = 128


def _conv_kernel(hd_ref, w_ref, cb_ref, g_ref, b_ref, o_ref, up_ref):
    s_len = hd_ref.shape[0]
    a = hd_ref[:, 0:D_CH].astype(jnp.float32)
    gate = hd_ref[:, D_CH:2 * D_CH].astype(jnp.float32)
    zeros = jnp.zeros((CONV_PAD, D_CH), jnp.float32)
    up_ref[0, 0:CONV_PAD, :] = zeros
    up_ref[0, CONV_PAD + s_len:2 * CONV_PAD + s_len, :] = zeros
    up_ref[0, CONV_PAD:CONV_PAD + s_len, :] = a * _sigmoid(gate)
    n_shifted = s_len + 2 * CONV_PAD - SUBLANES
    for s in range(1, SUBLANES):
        up_ref[s, 0:n_shifted, :] = up_ref[0, s:s + n_shifted, :]
    shift = CONV_PAD - CONV_W // 2
    for c in range(s_len // CONV_ROWS):
        acc = jnp.zeros((CONV_ROWS, D_CH), jnp.float32)
        for j in range(CONV_W):
            s = (j + shift) % SUBLANES
            start = c * CONV_ROWS + j + shift - s
            acc = acc + up_ref[s, start:start + CONV_ROWS, :] * w_ref[j:j + 1, :]
        u = acc + cb_ref[...]
        y = _layer_norm_rows(u, g_ref[...], b_ref[...])
        o_ref[c * CONV_ROWS:(c + 1) * CONV_ROWS, :] = (y * _sigmoid(y)).astype(o_ref.dtype)


def _conv_branch(h, w, cb, g, b, batch, seq):
    const = lambda shape: pl.BlockSpec(shape, lambda i: (0,) * len(shape))
    return pl.pallas_call(
        _conv_kernel,
        out_shape=jax.ShapeDtypeStruct((batch * seq, MIX_W), jnp.bfloat16),
        grid=(batch,),
        in_specs=[pl.BlockSpec((seq, 512), lambda i: (i, COL_D // 512)),
                  const(w.shape), const(cb.shape), const(g.shape), const(b.shape)],
        out_specs=pl.BlockSpec((seq, MIX_W), lambda i: (i, 0)),
        scratch_shapes=[pltpu.VMEM((SUBLANES, seq + 2 * CONV_PAD, D_CH), jnp.float32)],
        compiler_params=_cparams(("parallel",)),
    )(h, w, cb, g, b)


def _merge_kernel(gl_ref, ya_ref, yb_ref, yc_ref, yd_ref, x_ref, wb_ref, wo_ref, g_ref, b_ref,
                  of_ref, ob_ref):
    merged = None
    for n, y_ref in enumerate((ya_ref, yb_ref, yc_ref, yd_ref)):
        t = jnp.tanh(gl_ref[:, n * D_MODEL:(n + 1) * D_MODEL].astype(jnp.float32))
        term = (t + 1.0) * _dot(y_ref[...], wb_ref[n])
        merged = term if merged is None else merged + term
    mix = _dot(merged.astype(jnp.bfloat16), wo_ref[...])
    y = _layer_norm_rows(ALPHA * x_ref[...] + mix, g_ref[...], b_ref[...])
    of_ref[...] = y
    ob_ref[...] = y.astype(jnp.bfloat16)


def _merge(h, ya, yb, yc, yd, x, wb, wo, g, b, tm=512):
    t, d = x.shape
    row = lambda w: pl.BlockSpec((tm, w), lambda i: (i, 0))
    const = lambda shape: pl.BlockSpec(shape, lambda i: (0,) * len(shape))
    return pl.pallas_call(
        _merge_kernel,
        out_shape=(jax.ShapeDtypeStruct((t, d), jnp.float32), jax.ShapeDtypeStruct((t, d), jnp.bfloat16)),
        grid=(t // tm,),
        in_specs=[pl.BlockSpec((tm, N_BRANCH * d), lambda i: (i, COL_GATES)),
                  row(MIX_W), row(MIX_W), row(MIX_W), row(MIX_W), row(d),
                  const(wb.shape), const(wo.shape), const((1, d)), const((1, d))],
        out_specs=(row(d), row(d)),
        compiler_params=_cparams(("parallel",)),
    )(h, ya, yb, yc, yd, x, wb, wo, g.reshape(1, d), b.reshape(1, d))


def _ffn_kernel(xb_ref, xf_ref, w1_ref, w3_ref, w2_ref, g_ref, b_ref, of_ref, ob_ref):
    x = xb_ref[...]
    f = w2_ref.shape[0]
    split = (f // MXU_TILE + 1) // 2 * MXU_TILE
    acc = None
    for lo, hi in ((0, split), (split, f)):
        a = _dot(x, w1_ref[:, lo:hi])
        hid = (a * _sigmoid(a) * _dot(x, w3_ref[:, lo:hi])).astype(jnp.bfloat16)
        part = _dot(hid, w2_ref[lo:hi, :])
        acc = part if acc is None else acc + part
    y = _layer_norm_rows(ALPHA * xf_ref[...] + acc, g_ref[...], b_ref[...])
    of_ref[...] = y
    ob_ref[...] = y.astype(jnp.bfloat16)


def _ffn(xb, xf, w1, w3, w2, g, b, tm=512):
    t, d = xf.shape
    row = pl.BlockSpec((tm, d), lambda i: (i, 0))
    vec = pl.BlockSpec((1, d), lambda i: (0, 0))
    resident = lambda shape: pl.BlockSpec(shape, lambda i: (0, 0), pipeline_mode=pl.Buffered(1))
    return pl.pallas_call(
        _ffn_kernel,
        out_shape=(jax.ShapeDtypeStruct((t, d), jnp.float32), jax.ShapeDtypeStruct((t, d), jnp.bfloat16)),
        grid=(t // tm,),
        in_specs=[row, row, resident(w1.shape), resident(w3.shape), resident(w2.shape), vec, vec],
        out_specs=(row, row),
        compiler_params=_cparams(("parallel",)),
    )(xb, xf, w1, w3, w2, g.reshape(1, d), b.reshape(1, d))


def _pack_rows(x):
    half = x.shape[1] // 2
    bits = lax.bitcast_convert_type(x.astype(jnp.bfloat16).astype(jnp.float32), jnp.uint32)
    return (bits[:, :half] >> 16) | (bits[:, half:] & jnp.uint32(0xFFFF0000))


def _unpack_rows(p):
    lo = lax.bitcast_convert_type(p << 16, jnp.float32)
    hi = lax.bitcast_convert_type(p & jnp.uint32(0xFFFF0000), jnp.float32)
    return jnp.concatenate([lo, hi], axis=1)


def _router_kernel(x_ref, r_ref, tri_ref, oi_ref, ow_ref, cnt_ref, xp_ref, carry_ref):
    i = pl.program_id(0)

    @pl.when(i == 0)
    def _():
        carry_ref[...] = jnp.zeros_like(carry_ref)

    x = x_ref[...]
    r = r_ref[...]
    xh = x.astype(jnp.bfloat16)
    xl = (x - xh.astype(jnp.float32)).astype(jnp.bfloat16)
    rh = r.astype(jnp.bfloat16)
    rl = (r - rh.astype(jnp.float32)).astype(jnp.bfloat16)
    r2 = jnp.concatenate([rh, rl], axis=1)
    p = _dot(xh, r2) + _dot(xl, r2)
    logits = p[:, :LANES] + p[:, LANES:]
    tm = x.shape[0]
    lane = lax.broadcasted_iota(jnp.int32, (tm, LANES), 1)
    logits = jnp.where(lane < N_EXPERTS, logits, NEG_BIG)
    m1 = jnp.max(logits, axis=-1, keepdims=True)
    i1 = jnp.min(jnp.where(logits == m1, lane, LANES), axis=-1, keepdims=True)
    rest = jnp.where(lane == i1, NEG_BIG, logits)
    m2 = jnp.max(rest, axis=-1, keepdims=True)
    i2 = jnp.min(jnp.where(rest == m2, lane, LANES), axis=-1, keepdims=True)
    e = jnp.exp(m2 - m1)
    w1 = 1.0 / (1.0 + e)
    w2 = e / (1.0 + e)
    sel1 = lane == i1
    sel2 = lane == i2
    member = jnp.where(sel1 | sel2, 1.0, 0.0)
    before = _dot(tri_ref[...], member.astype(jnp.bfloat16)) + carry_ref[...]
    rank1 = jnp.sum(jnp.where(sel1, before, 0.0), axis=-1, keepdims=True).astype(jnp.int32)
    rank2 = jnp.sum(jnp.where(sel2, before, 0.0), axis=-1, keepdims=True).astype(jnp.int32)
    carry_ref[...] += jnp.sum(member, axis=0, keepdims=True)
    oi = jnp.where(lane == 0, i1, jnp.where(lane == 1, i2, jnp.where(lane == 2, rank1, rank2)))
    oi_ref[...] = oi.T[0:SUBLANES, :]
    ow_ref[...] = jnp.where(lane == 0, w1, w2)
    cnt_ref[...] = carry_ref[...]
    xp_ref[...] = _pack_rows(x)


def _router(xf, router, tm=512):
    t, d = xf.shape
    r_pad = jnp.zeros((d, LANES), jnp.float32).at[:, :N_EXPERTS].set(router.astype(jnp.float32))
    tri = jnp.asarray(np.tril(np.ones((tm, tm), np.float32), -1), jnp.bfloat16)
    return pl.pallas_call(
        _router_kernel,
        out_shape=(jax.ShapeDtypeStruct((SUBLANES, t), jnp.int32),
                   jax.ShapeDtypeStruct((t, LANES), jnp.float32),
                   jax.ShapeDtypeStruct((1, LANES), jnp.float32),
                   jax.ShapeDtypeStruct((t, d // 2), jnp.uint32)),
        grid=(t // tm,),
        in_specs=[pl.BlockSpec((tm, d), lambda i: (i, 0)),
                  pl.BlockSpec((d, LANES), lambda i: (0, 0)),
                  pl.BlockSpec((tm, tm), lambda i: (0, 0))],
        out_specs=(pl.BlockSpec((SUBLANES, tm), lambda i: (0, i)),
                   pl.BlockSpec((tm, LANES), lambda i: (i, 0)),
                   pl.BlockSpec((1, LANES), lambda i: (0, 0)),
                   pl.BlockSpec((tm, d // 2), lambda i: (i, 0))),
        scratch_shapes=[pltpu.VMEM((1, LANES), jnp.float32)],
        compiler_params=_cparams(("arbitrary",)),
    )(xf, r_pad, tri)


SC_CORES = 2
SC_SUBCORES = 16
SC_WINDOW = 64


def _sc_gather_rows(table, idx):
    m = idx.shape[0]
    d = table.shape[1]
    n_workers = SC_CORES * SC_SUBCORES
    per_w = m // n_workers
    n_win = per_w // SC_WINDOW
    assert per_w * n_workers == m and n_win * SC_WINDOW == per_w
    mesh = plsc.VectorSubcoreMesh(core_axis_name="c", subcore_axis_name="s")

    @functools.partial(
        pl.kernel, mesh=mesh,
        out_type=jax.ShapeDtypeStruct((m, d), table.dtype),
        scratch_types=[pltpu.VMEM((per_w,), jnp.int32),
                       pltpu.VMEM((SC_WINDOW, d), table.dtype),
                       pltpu.SemaphoreType.DMA],
    )
    def gather(table_hbm, idx_hbm, out_hbm, idx_v, rows_v, sem):
        wid = lax.axis_index("s") * SC_CORES + lax.axis_index("c")
        base = wid * per_w
        pltpu.sync_copy(idx_hbm.at[pl.ds(base, per_w)], idx_v)

        @pl.loop(0, n_win)
        def _(w):
            off = pl.multiple_of(w * SC_WINDOW, SC_WINDOW)
            pltpu.async_copy(table_hbm.at[idx_v.at[pl.ds(off, SC_WINDOW)]], rows_v, sem).wait()
            pltpu.sync_copy(rows_v, out_hbm.at[pl.ds(base + off, SC_WINDOW)])

    return gather(table, idx)


def _sc_scatter_rows(rows, idx, n_out):
    n_copies, m = idx.shape
    d = rows.shape[1]
    n_workers = SC_CORES * SC_SUBCORES
    per_w = m // n_workers
    n_win = per_w // SC_WINDOW
    assert per_w * n_workers == m and n_win * SC_WINDOW == per_w
    idx_w = idx.reshape(n_copies, n_workers, n_win, SC_WINDOW).transpose(1, 0, 2, 3)
    idx_w = idx_w.reshape(n_workers, n_copies * n_win, SC_WINDOW)
    mesh = plsc.VectorSubcoreMesh(core_axis_name="c", subcore_axis_name="s")

    @functools.partial(
        pl.kernel, mesh=mesh,
        out_type=jax.ShapeDtypeStruct((n_out, d), rows.dtype),
        scratch_types=[pltpu.VMEM((n_copies * n_win, SC_WINDOW), jnp.int32),
                       pltpu.VMEM((SC_WINDOW, d), rows.dtype)],
    )
    def scatter(rows_hbm, idx_hbm, out_hbm, idx_v, rows_v):
        wid = lax.axis_index("s") * SC_CORES + lax.axis_index("c")
        base = wid * per_w
        pltpu.sync_copy(idx_hbm.at[wid], idx_v)

        @pl.loop(0, n_win)
        def _(w):
            off = pl.multiple_of(w * SC_WINDOW, SC_WINDOW)
            pltpu.sync_copy(rows_hbm.at[pl.ds(base + off, SC_WINDOW)], rows_v)
            for k in range(n_copies):
                pltpu.sync_copy(rows_v, out_hbm.at[idx_v.at[k * n_win + w]])

    return scatter(rows, idx_w)


def _expert_kernel(te_ref, nv_ref, x_ref, w1_ref, w3_ref, w2_ref, o_ref,
                   w13s_ref, w2s_ref, xb_ref, acc_ref):
    i = pl.program_id(0)
    j = pl.program_id(1)
    last = pl.num_programs(1) - 1
    n_valid = nv_ref[i]
    n_piece = MOE_TILE // MOE_SUB
    piece_rows = [pl.ds(s * MOE_SUB, MOE_SUB) for s in range(n_piece)]

    for s, rows in enumerate(piece_rows):
        @pl.when((n_valid > s * MOE_SUB) & (j == 0))
        def _():
            row = lax.broadcasted_iota(jnp.int32, (MOE_SUB, 1), 0) + s * MOE_SUB
            x = jnp.where(row < n_valid, _unpack_rows(x_ref[rows, :]), 0.0)
            xb_ref[rows, :] = x.astype(jnp.bfloat16)
            acc_ref[rows, :] = jnp.zeros((MOE_SUB, acc_ref.shape[1]), jnp.float32)

    def compute(n_live):
        w13s_ref[:, 0:MOE_FCHUNK] = w1_ref[0].astype(jnp.bfloat16)
        w13s_ref[:, MOE_FCHUNK:2 * MOE_FCHUNK] = w3_ref[0].astype(jnp.bfloat16)
        w2s_ref[...] = w2_ref[0].astype(jnp.bfloat16)
        for rows in piece_rows[:n_live]:
            ab = _dot(xb_ref[rows, :], w13s_ref[...])
            a = ab[:, :MOE_FCHUNK]
            hid = (a * _sigmoid(a) * ab[:, MOE_FCHUNK:]).astype(jnp.bfloat16)
            acc_ref[rows, :] += _dot(hid, w2s_ref[...])

    for n_live in range(1, n_piece + 1):
        upper = n_live * MOE_SUB if n_live < n_piece else MOE_TILE
        pl.when((n_valid > (n_live - 1) * MOE_SUB) & (n_valid <= upper))(functools.partial(compute, n_live))

    for s, rows in enumerate(piece_rows):
        @pl.when((n_valid > s * MOE_SUB) & (j == last))
        def _():
            o_ref[rows, :] = _pack_rows(acc_ref[rows, :])

        @pl.when((n_valid <= s * MOE_SUB) & (j == last))
        def _():
            o_ref[rows, :] = jnp.zeros((MOE_SUB, o_ref.shape[1]), o_ref.dtype)


def _experts(tile_expert, tile_valid, xs, w1, w3, w2):
    n_rows = xs.shape[0]
    d, f = w1.shape[1], w1.shape[2]
    nj = f // MOE_FCHUNK

    def chunk(i, j, nv):
        return jnp.where(nv[i] > 0, j, nj - 1)

    return pl.pallas_call(
        _expert_kernel,
        out_shape=jax.ShapeDtypeStruct((n_rows, d // 2), jnp.uint32),
        grid_spec=pltpu.PrefetchScalarGridSpec(
            num_scalar_prefetch=2,
            grid=(n_rows // MOE_TILE, nj),
            in_specs=[pl.BlockSpec((MOE_TILE, d // 2), lambda i, j, te, nv: (i, 0)),
                      pl.BlockSpec((1, d, MOE_FCHUNK), lambda i, j, te, nv: (te[i], 0, chunk(i, j, nv))),
                      pl.BlockSpec((1, d, MOE_FCHUNK), lambda i, j, te, nv: (te[i], 0, chunk(i, j, nv))),
                      pl.BlockSpec((1, MOE_FCHUNK, d), lambda i, j, te, nv: (te[i], chunk(i, j, nv), 0))],
            out_specs=pl.BlockSpec((MOE_TILE, d // 2), lambda i, j, te, nv: (i, 0)),
            scratch_shapes=[pltpu.VMEM((d, 2 * MOE_FCHUNK), jnp.bfloat16),
                            pltpu.VMEM((MOE_FCHUNK, d), jnp.bfloat16),
                            pltpu.VMEM((MOE_TILE, d), jnp.bfloat16),
                            pltpu.VMEM((MOE_TILE, d), jnp.float32)],
        ),
        compiler_params=pltpu.CompilerParams(dimension_semantics=("arbitrary", "arbitrary"),
                                             vmem_limit_bytes=MOE_VMEM_LIMIT),
    )(tile_expert, tile_valid, xs, w1, w3, w2)


def _combine_kernel(ya_ref, yb_ref, x_ref, w_ref, g_ref, b_ref, o_ref):
    w = w_ref[...]
    f = w[:, 0:1] * _unpack_rows(ya_ref[...]) + w[:, 1:2] * _unpack_rows(yb_ref[...])
    o_ref[...] = _layer_norm_rows(ALPHA * x_ref[...] + f, g_ref[...], b_ref[...])


def _combine(yg, xf, wts, g, b, tm=512):
    t, d = xf.shape
    nt = t // tm
    return pl.pallas_call(
        _combine_kernel,
        out_shape=jax.ShapeDtypeStruct((t, d), jnp.float32),
        grid=(nt,),
        in_specs=[pl.BlockSpec((tm, d // 2), lambda i: (i, 0)),
                  pl.BlockSpec((tm, d // 2), lambda i: (i + nt, 0)),
                  pl.BlockSpec((tm, d), lambda i: (i, 0)),
                  pl.BlockSpec((tm, LANES), lambda i: (i, 0)),
                  pl.BlockSpec((1, d), lambda i: (0, 0)),
                  pl.BlockSpec((1, d), lambda i: (0, 0))],
        out_specs=pl.BlockSpec((tm, d), lambda i: (i, 0)),
        compiler_params=_cparams(("parallel",)),
    )(yg, yg, xf, wts, g.reshape(1, d), b.reshape(1, d))


def _moe_ffn(xf, router, w1, w3, w2, g, b):
    t, d = xf.shape
    oi, ow, cnt, xp = _router(xf, router)
    e1, e2, rank1, rank2 = oi[0], oi[1], oi[2], oi[3]
    counts = cnt[0, :N_EXPERTS].astype(jnp.int32)
    tiles = (counts + MOE_TILE - 1) // MOE_TILE
    tile_end = jnp.cumsum(tiles)
    tile_start = tile_end - tiles
    offs = tile_start * MOE_TILE
    dest = jnp.concatenate([offs[e1] + rank1, offs[e2] + rank2])
    n_rows = TOP_K * t + N_EXPERTS * MOE_TILE
    n_tiles = n_rows // MOE_TILE
    tile_id = jnp.arange(n_tiles, dtype=jnp.int32)
    tile_expert = jnp.sum(tile_id[:, None] >= tile_end[None, :], axis=1)
    used = tile_expert < N_EXPERTS
    last_expert = jnp.sum(tile_end[-1] - 1 >= tile_end)
    tile_expert = jnp.where(used, tile_expert, last_expert).astype(jnp.int32)
    tile_valid = jnp.clip(counts[tile_expert] - (tile_id - tile_start[tile_expert]) * MOE_TILE, 0, MOE_TILE)
    tile_valid = jnp.where(used, tile_valid, 0).astype(jnp.int32)
    xs = _sc_scatter_rows(xp, dest.reshape(TOP_K, t), n_rows)
    y = _experts(tile_expert, tile_valid, xs, w1, w3, w2)
    yg = _sc_gather_rows(y, dest)
    return _combine(yg, xf, ow, g, b)


def _prep_layer(l, p, in_cols):
    bf = jnp.bfloat16
    out = {}
    n_gate = N_BRANCH * D_MODEL
    w_l = p["w_in_bf16"][l]
    out["w_gate"] = 0.5 * w_l[:, int(in_cols[0]):int(in_cols[0]) + n_gate]
    out["w_rest"] = _take_cols(w_l, in_cols[n_gate:])
    deint = np.concatenate([np.arange(0, HEAD_DIM, 2), np.arange(1, HEAD_DIM, 2)])
    lane_dim = np.concatenate([deint[:32], deint[:32], deint[32:], deint[32:]])
    out["gq_a"] = p["a_q_norm"][l][lane_dim].reshape(1, LANES)
    out["gk_a"] = p["a_k_norm"][l][lane_dim].reshape(1, LANES)
    per_head = C_NOPE + C_ROPE
    uq_cols = []
    for h in range(C_HEADS):
        base = h * per_head
        uq_cols += [base + np.arange(32), base + C_NOPE + np.arange(0, C_ROPE, 2), np.full(16, -1),
                    base + 32 + np.arange(32), base + C_NOPE + np.arange(1, C_ROPE, 2), np.full(16, -1)]
    wuq = _take_cols(p["c_w_uq"][l], np.concatenate(uq_cols))
    out["wuq"] = jnp.concatenate([wuq, jnp.zeros((64, wuq.shape[1]), wuq.dtype)], axis=0).astype(bf)
    uk_cols, uv_cols = [], []
    for h in range(C_HEADS):
        base = h * (C_NOPE + C_V)
        uk_cols += [base + np.arange(32), np.full(32, -1), base + 32 + np.arange(32), np.full(32, -1)]
        uv_cols += [base + C_NOPE + np.arange(C_V)]
    out["wuk"] = _take_cols(p["c_w_ukv"][l], np.concatenate(uk_cols)).astype(bf)
    out["wuv"] = _take_cols(p["c_w_ukv"][l], np.concatenate(uv_cols)).astype(bf)
    out["gq_c"] = jnp.concatenate([p["c_q_norm"][l], jnp.zeros((64,), jnp.float32)]).reshape(1, 256)
    out["gkv_c"] = p["c_kv_norm"][l].reshape(1, C_KV_RANK)
    out["conv_w"] = jnp.concatenate([p["d_conv_w"][l][:, 0, :], jnp.zeros((1, D_CH), jnp.float32)], axis=0)
    out["conv_b"] = p["d_conv_b"][l].reshape(1, D_CH)
    out["d_ln_g"] = p["d_ln_g"][l].reshape(1, D_CH)
    out["d_ln_b"] = p["d_ln_b"][l].reshape(1, D_CH)
    a_rows = np.concatenate([(kvh * 2 + g) * HEAD_DIM + np.arange(HEAD_DIM)
                             for g in range(2) for kvh in range(A_KV_HEADS)])
    wb = p["w_branch"][l]
    out["w_branch"] = (0.5 * jnp.stack([wb[0][a_rows], wb[1], wb[2], wb[3]], axis=0)).astype(bf)
    out["w_out"] = p["w_out"][l].astype(bf)
    return out


def kernel(x, ln_in_g, ln_in_b, w_in, a_q_norm, a_k_norm, b_rpb, c_q_norm, c_kv_norm, c_w_uq, c_w_ukv,
           d_conv_w, d_conv_b, d_ln_g, d_ln_b, w_branch, w_out, ln_mix_g, ln_mix_b,
           ffn_w1, ffn_w3, ffn_w2, moe_router, moe_w1, moe_w3, moe_w2, ln_ffn_g, ln_ffn_b):
    batch, seq, d = x.shape
    t = batch * seq
    bf = jnp.bfloat16
    params = dict(w_in_bf16=w_in.astype(bf), a_q_norm=a_q_norm, a_k_norm=a_k_norm, c_q_norm=c_q_norm, c_kv_norm=c_kv_norm,
                  c_w_uq=c_w_uq, c_w_ukv=c_w_ukv, d_conv_w=d_conv_w, d_conv_b=d_conv_b, d_ln_g=d_ln_g,
                  d_ln_b=d_ln_b, w_branch=w_branch, w_out=w_out)
    in_cols = _in_proj_columns()
    cos_a, sin_a, cos_c, sin_c = _rope_tables(seq)
    na_bias = _na_bias_tables(b_rpb, seq // GRID_W)
    xf = xb = None
    for l in range(DEPTH):
        p = _prep_layer(l, params, in_cols)
        if l == 0:
            h, xf = _in_proj(x.reshape(t, d), p["w_gate"], p["w_rest"], ln=(ln_in_g, ln_in_b))
        else:
            h = _in_proj(xb, p["w_gate"], p["w_rest"])
        ya = _attn_a(h, cos_a, sin_a, p["gq_a"], p["gk_a"], batch, seq)
        yb = _attn_b(h, na_bias, l, batch, seq)
        yc = _attn_c(h, p["wuq"], p["wuk"], p["wuv"], p["gq_c"], p["gkv_c"], cos_c, sin_c, batch, seq)
        yd = _conv_branch(h, p["conv_w"], p["conv_b"], p["d_ln_g"], p["d_ln_b"], batch, seq)
        xf, xb = _merge(h, ya, yb, yc, yd, xf, p["w_branch"], p["w_out"], ln_mix_g[l], ln_mix_b[l])
        if l % 2 == 0:
            i = l // 2
            xf, xb = _ffn(xb, xf, ffn_w1[i].astype(bf), ffn_w3[i].astype(bf), ffn_w2[i].astype(bf),
                          ln_ffn_g[l], ln_ffn_b[l])
        else:
            i = l // 2
            xf = _moe_ffn(xf, moe_router[i], moe_w1[i], moe_w3[i], moe_w2[i], ln_ffn_g[l], ln_ffn_b[l])
            xb = xf.astype(bf)
    return xf.reshape(batch, seq, d)
```

```python
import functools

import numpy as np
import jax
import jax.numpy as jnp
from jax import lax
from jax.experimental import pallas as pl
from jax.experimental.pallas import tpu as pltpu
from jax.experimental.pallas import tpu_sc as plsc

D_MODEL = 1024
DEPTH = 2
GRID_W = 64
HEAD_DIM = 64
ROPE_THETA = 10000.0
RMS_EPS = 1e-6
LN_EPS = 1e-5
A_HEADS = 4
A_KV_HEADS = 2
B_HEADS = 4
NA_WIN_H = 8
NA_WIN_W = 16
C_HEADS = 4
C_NOPE = 64
C_ROPE = 32
C_V = 64
C_Q_RANK = 192
C_KV_RANK = 128
D_CH = 256
CONV_W = 31
N_BRANCH = 4
MIX_W = 256
D_FF = 2816
N_EXPERTS = 8
TOP_K = 2
D_FF_EXPERT = 3584
ALPHA = (2 * DEPTH) ** 0.25

LANES = 128
SUBLANES = 8
MXU_TILE = 256
VMEM_LIMIT = 48 * 1024 * 1024

LOG2E = 1.4426950408889634
NEG_BIG = -1e30

COL_GATES = 0
COL_AQ = 4096
COL_AKV = 4352
COL_C = 4608
COL_D = 5120
COL_BQ = 5632
COL_BK = 5888
COL_BV = 6144
D_IN_PAD = 6400

NA_QROWS = 4
NA_KROWS = 12

MOE_TILE = 2048
MOE_SUB = 512
MOE_FCHUNK = 512
MOE_VMEM_LIMIT = 56 * 1024 * 1024


def _in_proj_columns():
    o_aq, o_ak, o_av = 0, 256, 384
    o_bq, o_bk, o_bv = 512, 768, 1024
    o_cq, o_ckv, o_kpe = 1280, 1472, 1600
    o_d, o_g = 1632, 2144
    deint = np.concatenate([np.arange(0, HEAD_DIM, 2), np.arange(1, HEAD_DIM, 2)])
    cols = [o_g + np.arange(N_BRANCH * D_MODEL)]
    half = HEAD_DIM // 2
    for g in range(2):
        for part in range(2):
            for kvh in range(A_KV_HEADS):
                cols.append(o_aq + (kvh * 2 + g) * HEAD_DIM + deint[part * half:(part + 1) * half])
    for part in range(2):
        for kvh in range(A_KV_HEADS):
            cols.append(o_ak + kvh * HEAD_DIM + deint[part * half:(part + 1) * half])
    cols.append(o_av + np.arange(A_KV_HEADS * HEAD_DIM))
    cols.append(o_cq + np.arange(C_Q_RANK))
    cols.append(np.full(64, -1))
    cols.append(o_ckv + np.arange(C_KV_RANK))
    cols.append(np.full(32, -1))
    cols.append(o_kpe + np.arange(0, C_ROPE, 2))
    cols.append(np.full(48, -1))
    cols.append(o_kpe + np.arange(1, C_ROPE, 2))
    cols.append(np.full(16, -1))
    cols.append(o_d + np.arange(2 * D_CH))
    cols.append(o_bq + np.arange(3 * B_HEADS * HEAD_DIM))
    cols = np.concatenate(cols)
    assert cols.shape[0] == D_IN_PAD
    return cols


def _take_cols(w, cols):
    cols = np.asarray(cols)
    runs, i, n = [], 0, len(cols)
    while i < n:
        j = i + 1
        if cols[i] < 0:
            while j < n and cols[j] < 0:
                j += 1
            kind = "zero"
        else:
            while j < n and cols[j] == cols[j - 1] + 1:
                j += 1
            kind = "slice" if j - i >= 64 else "shuffle"
        if kind == "shuffle" and runs and runs[-1][0] == "shuffle":
            runs[-1] = ("shuffle", runs[-1][1], j)
        else:
            runs.append((kind, i, j))
        i = j
    parts = []
    for kind, i, j in runs:
        if kind == "zero":
            parts.append(jnp.zeros((w.shape[0], j - i), w.dtype))
        elif kind == "slice":
            parts.append(w[:, int(cols[i]):int(cols[i]) + j - i])
        else:
            lo, hi = int(cols[i:j].min()), int(cols[i:j].max()) + 1
            onehot = np.zeros((hi - lo, j - i), np.float32)
            onehot[cols[i:j] - lo, np.arange(j - i)] = 1.0
            parts.append(jnp.dot(w[:, lo:hi], jnp.asarray(onehot, w.dtype), precision=lax.Precision.HIGHEST))
    return jnp.concatenate(parts, axis=1)


def _rope_tables(seq):
    t = np.arange(seq)
    row = (t // GRID_W).astype(np.float32)
    col = (t % GRID_W).astype(np.float32)

    def angles(rot_dim):
        n_freq = rot_dim // 4
        inv = jnp.asarray(ROPE_THETA, jnp.float32) ** (-jnp.arange(n_freq, dtype=jnp.float32) / n_freq)
        return jnp.concatenate([jnp.asarray(row)[:, None] * inv, jnp.asarray(col)[:, None] * inv], axis=-1)

    ang_a = angles(HEAD_DIM)
    ca, sa = jnp.cos(ang_a), jnp.sin(ang_a)
    cos_a = jnp.concatenate([ca, ca, ca, ca], axis=-1)
    sin_a = jnp.concatenate([-sa, -sa, sa, sa], axis=-1)
    ang_c = angles(C_ROPE)
    cc, sc = jnp.cos(ang_c), jnp.sin(ang_c)
    one = jnp.ones((seq, 64), jnp.float32)
    zero = jnp.zeros((seq, 64), jnp.float32)
    cos_c = jnp.concatenate([one[:, :32], cc, one[:, :48], cc, one[:, :16]], axis=-1)
    sin_c = jnp.concatenate([zero[:, :32], -sc, zero[:, :48], sc, zero[:, :16]], axis=-1)
    return cos_a, sin_a, cos_c, sin_c


def _na_bias_tables(rpb, rows):
    wh = min(NA_WIN_H, rows)
    n_dr, n_dc = 2 * NA_WIN_H - 1, 2 * NA_WIN_W - 1
    qc = np.arange(GRID_W)[:, None]
    kc = np.arange(GRID_W)[None, :]
    col_start = np.clip(qc - NA_WIN_W // 2, 0, GRID_W - NA_WIN_W)
    ok_c = (kc >= col_start) & (kc < col_start + NA_WIN_W)
    dc = np.clip(kc - qc + (NA_WIN_W - 1), 0, n_dc - 1)
    oh_c = np.eye(n_dc, dtype=np.float32)[dc.reshape(-1)]
    oh_r, ok_r = [], []
    for r0 in (0, NA_QROWS, rows - NA_QROWS):
        ws = int(np.clip(r0 - 4, 0, rows - NA_KROWS))
        qr = r0 + np.arange(NA_QROWS)[:, None]
        kr = ws + np.arange(NA_KROWS)[None, :]
        row_start = np.clip(qr - wh // 2, 0, rows - wh)
        ok_r.append((kr >= row_start) & (kr < row_start + wh))
        dr = np.clip(kr - qr + (NA_WIN_H - 1), 0, n_dr - 1)
        oh_r.append(np.eye(n_dr, dtype=np.float32)[dr.reshape(-1)])
    oh_r = np.concatenate(oh_r, axis=0)
    ok = np.stack(ok_r)[:, :, None, :, None] & ok_c[None, None, :, None, :]
    ok = ok.reshape(3, 1, NA_QROWS * GRID_W, NA_KROWS * GRID_W)
    hi = lax.Precision.HIGHEST
    t1 = jnp.einsum("ma,lhab->lhmb", jnp.asarray(oh_r), rpb.astype(jnp.float32) * LOG2E, precision=hi)
    t2 = jnp.einsum("lhmb,nb->lhmn", t1, jnp.asarray(oh_c), precision=hi).astype(jnp.bfloat16)
    nl, nh = rpb.shape[0], rpb.shape[1]
    t2 = t2.reshape(nl, nh, 3, NA_QROWS, NA_KROWS, GRID_W, GRID_W).transpose(0, 2, 1, 3, 5, 4, 6)
    bias = t2.reshape(nl, 3, nh, NA_QROWS * GRID_W, NA_KROWS * GRID_W)
    return jnp.where(jnp.asarray(ok), bias, jnp.asarray(NEG_BIG, jnp.bfloat16))


def _cparams(sem):
    return pltpu.CompilerParams(dimension_semantics=sem, vmem_limit_bytes=VMEM_LIMIT)


def _layer_norm_rows(z, g, b):
    mu = jnp.mean(z, axis=-1, keepdims=True)
    zc = z - mu
    var = jnp.mean(zc * zc, axis=-1, keepdims=True)
    return zc * lax.rsqrt(var + LN_EPS) * g + b


def _sigmoid(x):
    return 0.5 * jnp.tanh(0.5 * x) + 0.5


def _dot(a, b):
    return jnp.dot(a, b, preferred_element_type=jnp.float32)


def _dot_nt(a, b):
    return lax.dot_general(a, b, (((1,), (1,)), ((), ())), preferred_element_type=jnp.float32)


IN_PROJ_CHUNK = 1280


def _project_chunks(xb, w_refs, h_ref):
    off = 0
    for w_ref in w_refs:
        n_tiles = w_ref.shape[1] // MXU_TILE
        n_chunks = -(-w_ref.shape[1] // IN_PROJ_CHUNK)
        lo = 0
        for c in range(n_chunks):
            hi = (n_tiles * (c + 1) // n_chunks) * MXU_TILE
            h_ref[:, off + lo:off + hi] = _dot(xb, w_ref[:, lo:hi]).astype(h_ref.dtype)
            lo = hi
        off += w_ref.shape[1]


def _in_proj_kernel(x_ref, wg_ref, wr_ref, h_ref):
    _project_chunks(x_ref[...], (wg_ref, wr_ref), h_ref)


def _in_proj_ln_kernel(x_ref, g_ref, b_ref, wg_ref, wr_ref, h_ref, xf_ref):
    y = _layer_norm_rows(x_ref[...], g_ref[...], b_ref[...])
    xf_ref[...] = y
    _project_chunks(y.astype(jnp.bfloat16), (wg_ref, wr_ref), h_ref)


def _in_proj(x, w_gate, w_rest, ln=None, tm=512):
    t, d = x.shape
    n = w_gate.shape[1] + w_rest.shape[1]
    row = lambda width: pl.BlockSpec((tm, width), lambda i: (i, 0))
    vec = pl.BlockSpec((1, d), lambda i: (0, 0))
    resident = lambda w: pl.BlockSpec(w.shape, lambda i: (0, 0), pipeline_mode=pl.Buffered(1))
    h_shape = jax.ShapeDtypeStruct((t, n), jnp.bfloat16)
    if ln is None:
        return pl.pallas_call(
            _in_proj_kernel, out_shape=h_shape, grid=(t // tm,),
            in_specs=[row(d), resident(w_gate), resident(w_rest)], out_specs=row(n),
            compiler_params=_cparams(("parallel",)),
        )(x, w_gate, w_rest)
    g, b = ln
    return pl.pallas_call(
        _in_proj_ln_kernel,
        out_shape=(h_shape, jax.ShapeDtypeStruct((t, d), jnp.float32)),
        grid=(t // tm,),
        in_specs=[row(d), vec, vec, resident(w_gate), resident(w_rest)], out_specs=(row(n), row(d)),
        compiler_params=_cparams(("parallel",)),
    )(x, g.reshape(1, d), b.reshape(1, d), w_gate, w_rest)


def _ones_halves(vblk):
    lane = lax.broadcasted_iota(jnp.int32, vblk.shape, 1)
    ones = jnp.ones_like(vblk)
    return jnp.where(lane < HEAD_DIM, vblk, ones), jnp.where(lane < HEAD_DIM, ones, vblk)


def _attend_pair(s_lo, s_hi, v_lo, v_hi, lo_q):
    outs = []
    for s, v in ((s_lo, v_lo), (s_hi, v_hi)):
        p = jnp.exp2(s - jnp.max(s, axis=-1, keepdims=True)).astype(jnp.bfloat16)
        outs.append(_dot(p, v))
    num = jnp.where(lo_q, outs[0], outs[1])
    den = jnp.where(lo_q, pltpu.roll(outs[0], HEAD_DIM, 1), pltpu.roll(outs[1], HEAD_DIM, 1))
    return num / den


def _attn_a_kernel(q_ref, kv_ref, cos_ref, sin_ref, gq_ref, gk_ref, o_ref, ks_ref, va_ref, *, tq):
    s_len = q_ref.shape[0]

    def first_head(shape):
        lane = lax.broadcasted_iota(jnp.int32, shape, 1)
        return (lane % HEAD_DIM) < HEAD_DIM // 2

    def prep(x, g, cos, sin):
        head_a = first_head(x.shape)
        x2 = x * x
        ss_a = jnp.sum(jnp.where(head_a, x2, 0.0), axis=-1, keepdims=True)
        ss_b = jnp.sum(jnp.where(head_a, 0.0, x2), axis=-1, keepdims=True)
        ms = jnp.where(head_a, ss_a, ss_b) * (1.0 / HEAD_DIM)
        xn = x * lax.rsqrt(ms + RMS_EPS) * g
        return xn * cos + pltpu.roll(xn, LANES // 2, 1) * sin

    ks_ref[...] = prep(kv_ref[:, 0:LANES].astype(jnp.float32), gk_ref[...],
                       cos_ref[...], sin_ref[...]).astype(jnp.bfloat16)
    scale = HEAD_DIM ** -0.5 * LOG2E
    lane_q = lax.broadcasted_iota(jnp.int32, (tq, LANES), 1)
    lo_q = lane_q < HEAD_DIM
    head_a_q = first_head((tq, LANES))

    va_ref[0], va_ref[1] = _ones_halves(kv_ref[:, LANES:2 * LANES])

    def body(i, carry):
        r = pl.multiple_of(i * tq, tq)
        k = ks_ref[...]
        cos = cos_ref[pl.ds(r, tq), :]
        sin = sin_ref[pl.ds(r, tq), :]
        for g in range(2):
            qt = prep(q_ref[pl.ds(r, tq), g * LANES:(g + 1) * LANES].astype(jnp.float32), gq_ref[...], cos, sin)
            qt = (qt * scale).astype(jnp.bfloat16)
            zero = jnp.zeros_like(qt)
            out = _attend_pair(_dot_nt(jnp.where(head_a_q, qt, zero), k), _dot_nt(jnp.where(head_a_q, zero, qt), k),
                               va_ref[0], va_ref[1], lo_q)
            o_ref[pl.ds(r, tq), g * LANES:(g + 1) * LANES] = out.astype(o_ref.dtype)
        return carry

    lax.fori_loop(0, s_len // tq, body, 0, unroll=4)


def _attn_a(h, cos_a, sin_a, gq, gk, batch, seq, tq=256):
    kern = functools.partial(_attn_a_kernel, tq=tq)
    return pl.pallas_call(
        kern,
        out_shape=jax.ShapeDtypeStruct((batch * seq, MIX_W), jnp.bfloat16),
        grid=(batch,),
        in_specs=[pl.BlockSpec((seq, 256), lambda b: (b, COL_AQ // 256)),
                  pl.BlockSpec((seq, 256), lambda b: (b, COL_AKV // 256)),
                  pl.BlockSpec((seq, LANES), lambda b: (0, 0)),
                  pl.BlockSpec((seq, LANES), lambda b: (0, 0)),
                  pl.BlockSpec((1, LANES), lambda b: (0, 0)),
                  pl.BlockSpec((1, LANES), lambda b: (0, 0))],
        out_specs=pl.BlockSpec((seq, MIX_W), lambda b: (b, 0)),
        scratch_shapes=[pltpu.VMEM((seq, LANES), jnp.bfloat16), pltpu.VMEM((2, seq, LANES), jnp.bfloat16)],
        compiler_params=_cparams(("parallel",)),
    )(h, h, cos_a, sin_a, gq, gk)


def _attn_b_kernel(q_ref, k_ref, v_ref, bias_ref, o_ref, va_ref, *, rows):
    nq = NA_QROWS * GRID_W
    nk = NA_KROWS * GRID_W
    lane_q = lax.broadcasted_iota(jnp.int32, (nq, LANES), 1)
    lo_q = lane_q < HEAD_DIM
    n_groups = rows // NA_QROWS
    scale = HEAD_DIM ** -0.5 * LOG2E
    for blk in range(B_HEADS // 2):
        va_ref[2 * blk], va_ref[2 * blk + 1] = _ones_halves(v_ref[:, blk * LANES:(blk + 1) * LANES])

    def body(gi, carry):
        ws = jnp.clip(gi * NA_QROWS - 4, 0, rows - NA_KROWS)
        cls = jnp.where(gi == 0, 0, jnp.where(gi == n_groups - 1, 2, 1))
        qr = pl.multiple_of(gi * nq, nq)
        kr = pl.multiple_of(ws * GRID_W, GRID_W)
        for blk in range(B_HEADS // 2):
            sl = slice(blk * LANES, (blk + 1) * LANES)
            qt = q_ref[pl.ds(qr, nq), sl] * scale
            kw = k_ref[pl.ds(kr, nk), sl]
            zero = jnp.zeros_like(qt)
            s_lo = _dot_nt(jnp.where(lo_q, qt, zero), kw) + bias_ref[cls, 2 * blk].astype(jnp.float32)
            s_hi = _dot_nt(jnp.where(lo_q, zero, qt), kw) + bias_ref[cls, 2 * blk + 1].astype(jnp.float32)
            out = _attend_pair(s_lo, s_hi, va_ref[2 * blk, pl.ds(kr, nk), :], va_ref[2 * blk + 1, pl.ds(kr, nk), :], lo_q)
            o_ref[pl.ds(qr, nq), sl] = out.astype(o_ref.dtype)
        return carry

    lax.fori_loop(0, n_groups, body, 0, unroll=True)


def _attn_b(h, bias, layer, batch, seq):
    rows = seq // GRID_W
    kern = functools.partial(_attn_b_kernel, rows=rows)
    return pl.pallas_call(
        kern,
        out_shape=jax.ShapeDtypeStruct((batch * seq, MIX_W), jnp.bfloat16),
        grid=(batch,),
        in_specs=[pl.BlockSpec((seq, 256), lambda b: (b, COL_BQ // 256)),
                  pl.BlockSpec((seq, 256), lambda b: (b, COL_BK // 256)),
                  pl.BlockSpec((seq, 256), lambda b: (b, COL_BV // 256)),
                  pl.BlockSpec((None,) + bias.shape[1:], lambda b: (layer, 0, 0, 0, 0))],
        out_specs=pl.BlockSpec((seq, MIX_W), lambda b: (b, 0)),
        scratch_shapes=[pltpu.VMEM((B_HEADS, seq, LANES), jnp.bfloat16)],
        compiler_params=_cparams(("parallel",)),
    )(h, h, h, bias)


def _attn_c_kernel(hc_ref, wuq_ref, wuk_ref, wuv_ref, gq_ref, gkv_ref, cos_ref, sin_ref, o_ref,
                   qs_ref, ks_ref, vs_ref, *, tq):
    s_len = hc_ref.shape[0]
    cos = cos_ref[...]
    sin = sin_ref[...]

    def rotate(x):
        return x * cos + pltpu.roll(x, LANES // 2, 1) * sin

    cq = hc_ref[:, 0:256].astype(jnp.float32)
    ms = jnp.sum(cq * cq, axis=-1, keepdims=True) * (1.0 / C_Q_RANK)
    cqn = (cq * lax.rsqrt(ms + RMS_EPS) * gq_ref[...]).astype(jnp.bfloat16)
    ckv = hc_ref[:, 256:384].astype(jnp.float32)
    ms = jnp.mean(ckv * ckv, axis=-1, keepdims=True)
    kvn = (ckv * lax.rsqrt(ms + RMS_EPS) * gkv_ref[...]).astype(jnp.bfloat16)
    kpe = rotate(hc_ref[:, 384:512].astype(jnp.float32))
    v_all = _dot(kvn, wuv_ref[...]).astype(jnp.bfloat16)
    for blk in range(C_HEADS // 2):
        vs_ref[2 * blk], vs_ref[2 * blk + 1] = _ones_halves(v_all[:, blk * LANES:(blk + 1) * LANES])
    scale = (C_NOPE + C_ROPE) ** -0.5 * LOG2E
    for h in range(C_HEADS):
        sl = slice(h * LANES, (h + 1) * LANES)
        qh = rotate(_dot(cqn, wuq_ref[:, sl])) * scale
        qs_ref[:, sl] = qh.astype(jnp.bfloat16)
        ks_ref[:, sl] = (_dot(kvn, wuk_ref[:, sl]) + kpe).astype(jnp.bfloat16)

    lane_q = lax.broadcasted_iota(jnp.int32, (tq, LANES), 1)
    lo_q = lane_q < C_V

    def body(i, carry):
        r = pl.multiple_of(i * tq, tq)
        for blk in range(C_HEADS // 2):
            s = []
            for par in range(2):
                sl = slice((2 * blk + par) * LANES, (2 * blk + par + 1) * LANES)
                s.append(_dot_nt(qs_ref[pl.ds(r, tq), sl], ks_ref[:, sl]))
            out = _attend_pair(s[0], s[1], vs_ref[2 * blk], vs_ref[2 * blk + 1], lo_q)
            o_ref[pl.ds(r, tq), blk * LANES:(blk + 1) * LANES] = out.astype(o_ref.dtype)
        return carry

    lax.fori_loop(0, s_len // tq, body, 0, unroll=4)


def _attn_c(h, wuq, wuk, wuv, gq, gkv, cos_c, sin_c, batch, seq, tq=256):
    kern = functools.partial(_attn_c_kernel, tq=tq)
    const = lambda shape: pl.BlockSpec(shape, lambda b: (0,) * len(shape))
    return pl.pallas_call(
        kern,
        out_shape=jax.ShapeDtypeStruct((batch * seq, MIX_W), jnp.bfloat16),
        grid=(batch,),
        in_specs=[pl.BlockSpec((seq, 512), lambda b: (b, COL_C // 512)),
                  const(wuq.shape), const(wuk.shape), const(wuv.shape),
                  const(gq.shape), const(gkv.shape), const(cos_c.shape), const(sin_c.shape)],
        out_specs=pl.BlockSpec((seq, MIX_W), lambda b: (b, 0)),
        scratch_shapes=[pltpu.VMEM((seq, C_HEADS * LANES), jnp.bfloat16),
                        pltpu.VMEM((seq, C_HEADS * LANES), jnp.bfloat16),
                        pltpu.VMEM((C_HEADS, seq, LANES), jnp.bfloat16)],
        compiler_params=_cparams(("parallel",)),
    )(h, wuq, wuk, wuv, gq, gkv, cos_c, sin_c)


CONV_PAD = 16
CONV_ROWS = 128


def _conv_kernel(hd_ref, w_ref, cb_ref, g_ref, b_ref, o_ref, up_ref):
    s_len = hd_ref.shape[0]
    a = hd_ref[:, 0:D_CH].astype(jnp.float32)
    gate = hd_ref[:, D_CH:2 * D_CH].astype(jnp.float32)
    zeros = jnp.zeros((CONV_PAD, D_CH), jnp.float32)
    up_ref[0, 0:CONV_PAD, :] = zeros
    up_ref[0, CONV_PAD + s_len:2 * CONV_PAD + s_len, :] = zeros
    up_ref[0, CONV_PAD:CONV_PAD + s_len, :] = a * _sigmoid(gate)
    n_shifted = s_len + 2 * CONV_PAD - SUBLANES
    for s in range(1, SUBLANES):
        up_ref[s, 0:n_shifted, :] = up_ref[0, s:s + n_shifted, :]
    shift = CONV_PAD - CONV_W // 2
    for c in range(s_len // CONV_ROWS):
        acc = jnp.zeros((CONV_ROWS, D_CH), jnp.float32)
        for j in range(CONV_W):
            s = (j + shift) % SUBLANES
            start = c * CONV_ROWS + j + shift - s
            acc = acc + up_ref[s, start:start + CONV_ROWS, :] * w_ref[j:j + 1, :]
        u = acc + cb_ref[...]
        y = _layer_norm_rows(u, g_ref[...], b_ref[...])
        o_ref[c * CONV_ROWS:(c + 1) * CONV_ROWS, :] = (y * _sigmoid(y)).astype(o_ref.dtype)


def _conv_branch(h, w, cb, g, b, batch, seq):
    const = lambda shape: pl.BlockSpec(shape, lambda i: (0,) * len(shape))
    return pl.pallas_call(
        _conv_kernel,
        out_shape=jax.ShapeDtypeStruct((batch * seq, MIX_W), jnp.bfloat16),
        grid=(batch,),
        in_specs=[pl.BlockSpec((seq, 512), lambda i: (i, COL_D // 512)),
                  const(w.shape), const(cb.shape), const(g.shape), const(b.shape)],
        out_specs=pl.BlockSpec((seq, MIX_W), lambda i: (i, 0)),
        scratch_shapes=[pltpu.VMEM((SUBLANES, seq + 2 * CONV_PAD, D_CH), jnp.float32)],
        compiler_params=_cparams(("parallel",)),
    )(h, w, cb, g, b)


def _merge_kernel(gl_ref, ya_ref, yb_ref, yc_ref, yd_ref, x_ref, wb_ref, wo_ref, g_ref, b_ref,
                  of_ref, ob_ref):
    merged = None
    for n, y_ref in enumerate((ya_ref, yb_ref, yc_ref, yd_ref)):
        t = jnp.tanh(gl_ref[:, n * D_MODEL:(n + 1) * D_MODEL].astype(jnp.float32))
        term = (t + 1.0) * _dot(y_ref[...], wb_ref[n])
        merged = term if merged is None else merged + term
    mix = _dot(merged.astype(jnp.bfloat16), wo_ref[...])
    y = _layer_norm_rows(ALPHA * x_ref[...] + mix, g_ref[...], b_ref[...])
    of_ref[...] = y
    ob_ref[...] = y.astype(jnp.bfloat16)


def _merge(h, ya, yb, yc, yd, x, wb, wo, g, b, tm=512):
    t, d = x.shape
    row = lambda w: pl.BlockSpec((tm, w), lambda i: (i, 0))
    const = lambda shape: pl.BlockSpec(shape, lambda i: (0,) * len(shape))
    return pl.pallas_call(
        _merge_kernel,
        out_shape=(jax.ShapeDtypeStruct((t, d), jnp.float32), jax.ShapeDtypeStruct((t, d), jnp.bfloat16)),
        grid=(t // tm,),
        in_specs=[pl.BlockSpec((tm, N_BRANCH * d), lambda i: (i, COL_GATES)),
                  row(MIX_W), row(MIX_W), row(MIX_W), row(MIX_W), row(d),
                  const(wb.shape), const(wo.shape), const((1, d)), const((1, d))],
        out_specs=(row(d), row(d)),
        compiler_params=_cparams(("parallel",)),
    )(h, ya, yb, yc, yd, x, wb, wo, g.reshape(1, d), b.reshape(1, d))


def _ffn_kernel(xb_ref, xf_ref, w1_ref, w3_ref, w2_ref, g_ref, b_ref, of_ref, ob_ref):
    x = xb_ref[...]
    f = w2_ref.shape[0]
    split = (f // MXU_TILE + 1) // 2 * MXU_TILE
    acc = None
    for lo, hi in ((0, split), (split, f)):
        a = _dot(x, w1_ref[:, lo:hi])
        hid = (a * _sigmoid(a) * _dot(x, w3_ref[:, lo:hi])).astype(jnp.bfloat16)
        part = _dot(hid, w2_ref[lo:hi, :])
        acc = part if acc is None else acc + part
    y = _layer_norm_rows(ALPHA * xf_ref[...] + acc, g_ref[...], b_ref[...])
    of_ref[...] = y
    ob_ref[...] = y.astype(jnp.bfloat16)


def _ffn(xb, xf, w1, w3, w2, g, b, tm=512):
    t, d = xf.shape
    row = pl.BlockSpec((tm, d), lambda i: (i, 0))
    vec = pl.BlockSpec((1, d), lambda i: (0, 0))
    resident = lambda shape: pl.BlockSpec(shape, lambda i: (0, 0), pipeline_mode=pl.Buffered(1))
    return pl.pallas_call(
        _ffn_kernel,
        out_shape=(jax.ShapeDtypeStruct((t, d), jnp.float32), jax.ShapeDtypeStruct((t, d), jnp.bfloat16)),
        grid=(t // tm,),
        in_specs=[row, row, resident(w1.shape), resident(w3.shape), resident(w2.shape), vec, vec],
        out_specs=(row, row),
        compiler_params=_cparams(("parallel",)),
    )(xb, xf, w1, w3, w2, g.reshape(1, d), b.reshape(1, d))


def _pack_rows(x):
    half = x.shape[1] // 2
    bits = lax.bitcast_convert_type(x.astype(jnp.bfloat16).astype(jnp.float32), jnp.uint32)
    return (bits[:, :half] >> 16) | (bits[:, half:] & jnp.uint32(0xFFFF0000))


def _unpack_rows(p):
    lo = lax.bitcast_convert_type(p << 16, jnp.float32)
    hi = lax.bitcast_convert_type(p & jnp.uint32(0xFFFF0000), jnp.float32)
    return jnp.concatenate([lo, hi], axis=1)


def _router_kernel(x_ref, r_ref, tri_ref, oi_ref, ow_ref, cnt_ref, xp_ref, carry_ref):
    i = pl.program_id(0)

    @pl.when(i == 0)
    def _():
        carry_ref[...] = jnp.zeros_like(carry_ref)

    x = x_ref[...]
    r = r_ref[...]
    xh = x.astype(jnp.bfloat16)
    xl = (x - xh.astype(jnp.float32)).astype(jnp.bfloat16)
    rh = r.astype(jnp.bfloat16)
    rl = (r - rh.astype(jnp.float32)).astype(jnp.bfloat16)
    r2 = jnp.concatenate([rh, rl], axis=1)
    p = _dot(xh, r2) + _dot(xl, r2)
    logits = p[:, :LANES] + p[:, LANES:]
    tm = x.shape[0]
    lane = lax.broadcasted_iota(jnp.int32, (tm, LANES), 1)
    logits = jnp.where(lane < N_EXPERTS, logits, NEG_BIG)
    m1 = jnp.max(logits, axis=-1, keepdims=True)
    i1 = jnp.min(jnp.where(logits == m1, lane, LANES), axis=-1, keepdims=True)
    rest = jnp.where(lane == i1, NEG_BIG, logits)
    m2 = jnp.max(rest, axis=-1, keepdims=True)
    i2 = jnp.min(jnp.where(rest == m2, lane, LANES), axis=-1, keepdims=True)
    e = jnp.exp(m2 - m1)
    w1 = 1.0 / (1.0 + e)
    w2 = e / (1.0 + e)
    sel1 = lane == i1
    sel2 = lane == i2
    member = jnp.where(sel1 | sel2, 1.0, 0.0)
    before = _dot(tri_ref[...], member.astype(jnp.bfloat16)) + carry_ref[...]
    rank1 = jnp.sum(jnp.where(sel1, before, 0.0), axis=-1, keepdims=True).astype(jnp.int32)
    rank2 = jnp.sum(jnp.where(sel2, before, 0.0), axis=-1, keepdims=True).astype(jnp.int32)
    carry_ref[...] += jnp.sum(member, axis=0, keepdims=True)
    oi = jnp.where(lane == 0, i1, jnp.where(lane == 1, i2, jnp.where(lane == 2, rank1, rank2)))
    oi_ref[...] = oi.T[0:SUBLANES, :]
    ow_ref[...] = jnp.where(lane == 0, w1, w2)
    cnt_ref[...] = carry_ref[...]
    xp_ref[...] = _pack_rows(x)


def _router(xf, router, tm=512):
    t, d = xf.shape
    r_pad = jnp.zeros((d, LANES), jnp.float32).at[:, :N_EXPERTS].set(router.astype(jnp.float32))
    tri = jnp.asarray(np.tril(np.ones((tm, tm), np.float32), -1), jnp.bfloat16)
    return pl.pallas_call(
        _router_kernel,
        out_shape=(jax.ShapeDtypeStruct((SUBLANES, t), jnp.int32),
                   jax.ShapeDtypeStruct((t, LANES), jnp.float32),
                   jax.ShapeDtypeStruct((1, LANES), jnp.float32),
                   jax.ShapeDtypeStruct((t, d // 2), jnp.uint32)),
        grid=(t // tm,),
        in_specs=[pl.BlockSpec((tm, d), lambda i: (i, 0)),
                  pl.BlockSpec((d, LANES), lambda i: (0, 0)),
                  pl.BlockSpec((tm, tm), lambda i: (0, 0))],
        out_specs=(pl.BlockSpec((SUBLANES, tm), lambda i: (0, i)),
                   pl.BlockSpec((tm, LANES), lambda i: (i, 0)),
                   pl.BlockSpec((1, LANES), lambda i: (0, 0)),
                   pl.BlockSpec((tm, d // 2), lambda i: (i, 0))),
        scratch_shapes=[pltpu.VMEM((1, LANES), jnp.float32)],
        compiler_params=_cparams(("arbitrary",)),
    )(xf, r_pad, tri)


SC_CORES = 2
SC_SUBCORES = 16
SC_WINDOW = 64


def _sc_gather_rows(table, idx):
    m = idx.shape[0]
    d = table.shape[1]
    n_workers = SC_CORES * SC_SUBCORES
    per_w = m // n_workers
    n_win = per_w // SC_WINDOW
    assert per_w * n_workers == m and n_win * SC_WINDOW == per_w
    mesh = plsc.VectorSubcoreMesh(core_axis_name="c", subcore_axis_name="s")

    @functools.partial(
        pl.kernel, mesh=mesh,
        out_type=jax.ShapeDtypeStruct((m, d), table.dtype),
        scratch_types=[pltpu.VMEM((per_w,), jnp.int32),
                       pltpu.VMEM((SC_WINDOW, d), table.dtype),
                       pltpu.SemaphoreType.DMA],
    )
    def gather(table_hbm, idx_hbm, out_hbm, idx_v, rows_v, sem):
        wid = lax.axis_index("s") * SC_CORES + lax.axis_index("c")
        base = wid * per_w
        pltpu.sync_copy(idx_hbm.at[pl.ds(base, per_w)], idx_v)

        @pl.loop(0, n_win)
        def _(w):
            off = pl.multiple_of(w * SC_WINDOW, SC_WINDOW)
            pltpu.async_copy(table_hbm.at[idx_v.at[pl.ds(off, SC_WINDOW)]], rows_v, sem).wait()
            pltpu.sync_copy(rows_v, out_hbm.at[pl.ds(base + off, SC_WINDOW)])

    return gather(table, idx)


def _sc_scatter_rows(rows, idx, n_out):
    n_copies, m = idx.shape
    d = rows.shape[1]
    n_workers = SC_CORES * SC_SUBCORES
    per_w = m // n_workers
    n_win = per_w // SC_WINDOW
    assert per_w * n_workers == m and n_win * SC_WINDOW == per_w
    idx_w = idx.reshape(n_copies, n_workers, n_win, SC_WINDOW).transpose(1, 0, 2, 3)
    idx_w = idx_w.reshape(n_workers, n_copies * n_win, SC_WINDOW)
    mesh = plsc.VectorSubcoreMesh(core_axis_name="c", subcore_axis_name="s")

    @functools.partial(
        pl.kernel, mesh=mesh,
        out_type=jax.ShapeDtypeStruct((n_out, d), rows.dtype),
        scratch_types=[pltpu.VMEM((n_copies * n_win, SC_WINDOW), jnp.int32),
                       pltpu.VMEM((SC_WINDOW, d), rows.dtype)],
    )
    def scatter(rows_hbm, idx_hbm, out_hbm, idx_v, rows_v):
        wid = lax.axis_index("s") * SC_CORES + lax.axis_index("c")
        base = wid * per_w
        pltpu.sync_copy(idx_hbm.at[wid], idx_v)

        @pl.loop(0, n_win)
        def _(w):
            off = pl.multiple_of(w * SC_WINDOW, SC_WINDOW)
            pltpu.sync_copy(rows_hbm.at[pl.ds(base + off, SC_WINDOW)], rows_v)
            for k in range(n_copies):
                pltpu.sync_copy(rows_v, out_hbm.at[idx_v.at[k * n_win + w]])

    return scatter(rows, idx_w)


def _expert_kernel(te_ref, nv_ref, x_ref, w1_ref, w3_ref, w2_ref, o_ref,
                   w13s_ref, w2s_ref, xb_ref, acc_ref):
    i = pl.program_id(0)
    j = pl.program_id(1)
    last = pl.num_programs(1) - 1
    n_valid = nv_ref[i]
    n_piece = MOE_TILE // MOE_SUB
    piece_rows = [pl.ds(s * MOE_SUB, MOE_SUB) for s in range(n_piece)]

    for s, rows in enumerate(piece_rows):
        @pl.when((n_valid > s * MOE_SUB) & (j == 0))
        def _():
            row = lax.broadcasted_iota(jnp.int32, (MOE_SUB, 1), 0) + s * MOE_SUB
            x = jnp.where(row < n_valid, _unpack_rows(x_ref[rows, :]), 0.0)
            xb_ref[rows, :] = x.astype(jnp.bfloat16)
            acc_ref[rows, :] = jnp.zeros((MOE_SUB, acc_ref.shape[1]), jnp.float32)

    def compute(n_live):
        w13s_ref[:, 0:MOE_FCHUNK] = w1_ref[0].astype(jnp.bfloat16)
        w13s_ref[:, MOE_FCHUNK:2 * MOE_FCHUNK] = w3_ref[0].astype(jnp.bfloat16)
        w2s_ref[...] = w2_ref[0].astype(jnp.bfloat16)
        for rows in piece_rows[:n_live]:
            ab = _dot(xb_ref[rows, :], w13s_ref[...])
            a = ab[:, :MOE_FCHUNK]
            hid = (a * _sigmoid(a) * ab[:, MOE_FCHUNK:]).astype(jnp.bfloat16)
            acc_ref[rows, :] += _dot(hid, w2s_ref[...])

    for n_live in range(1, n_piece + 1):
        upper = n_live * MOE_SUB if n_live < n_piece else MOE_TILE
        pl.when((n_valid > (n_live - 1) * MOE_SUB) & (n_valid <= upper))(functools.partial(compute, n_live))

    for s, rows in enumerate(piece_rows):
        @pl.when((n_valid > s * MOE_SUB) & (j == last))
        def _():
            o_ref[rows, :] = _pack_rows(acc_ref[rows, :])

        @pl.when((n_valid <= s * MOE_SUB) & (j == last))
        def _():
            o_ref[rows, :] = jnp.zeros((MOE_SUB, o_ref.shape[1]), o_ref.dtype)


def _experts(tile_expert, tile_valid, xs, w1, w3, w2):
    n_rows = xs.shape[0]
    d, f = w1.shape[1], w1.shape[2]
    nj = f // MOE_FCHUNK

    def chunk(i, j, nv):
        return jnp.where(nv[i] > 0, j, nj - 1)

    return pl.pallas_call(
        _expert_kernel,
        out_shape=jax.ShapeDtypeStruct((n_rows, d // 2), jnp.uint32),
        grid_spec=pltpu.PrefetchScalarGridSpec(
            num_scalar_prefetch=2,
            grid=(n_rows // MOE_TILE, nj),
            in_specs=[pl.BlockSpec((MOE_TILE, d // 2), lambda i, j, te, nv: (i, 0)),
                      pl.BlockSpec((1, d, MOE_FCHUNK), lambda i, j, te, nv: (te[i], 0, chunk(i, j, nv))),
                      pl.BlockSpec((1, d, MOE_FCHUNK), lambda i, j, te, nv: (te[i], 0, chunk(i, j, nv))),
                      pl.BlockSpec((1, MOE_FCHUNK, d), lambda i, j, te, nv: (te[i], chunk(i, j, nv), 0))],
            out_specs=pl.BlockSpec((MOE_TILE, d // 2), lambda i, j, te, nv: (i, 0)),
            scratch_shapes=[pltpu.VMEM((d, 2 * MOE_FCHUNK), jnp.bfloat16),
                            pltpu.VMEM((MOE_FCHUNK, d), jnp.bfloat16),
                            pltpu.VMEM((MOE_TILE, d), jnp.bfloat16),
                            pltpu.VMEM((MOE_TILE, d), jnp.float32)],
        ),
        compiler_params=pltpu.CompilerParams(dimension_semantics=("arbitrary", "arbitrary"),
                                             vmem_limit_bytes=MOE_VMEM_LIMIT),
    )(tile_expert, tile_valid, xs, w1, w3, w2)


def _combine_kernel(ya_ref, yb_ref, x_ref, w_ref, g_ref, b_ref, *rest):
    o_ref = rest[-1]
    w = w_ref[...]
    f = w[:, 0:1] * _unpack_rows(ya_ref[...]) + w[:, 1:2] * _unpack_rows(yb_ref[...])
    o_ref[...] = _layer_norm_rows(ALPHA * x_ref[...] + f, g_ref[...], b_ref[...])


def _combine(yg, xf, wts, g, b, part, n_parts, prev=None, tm=512):
    t, d = xf.shape
    nt = t // n_parts // tm
    base = part * nt
    in_specs = [pl.BlockSpec((tm, d // 2), lambda i: (i, 0)),
                pl.BlockSpec((tm, d // 2), lambda i: (i + nt, 0)),
                pl.BlockSpec((tm, d), lambda i: (i + base, 0)),
                pl.BlockSpec((tm, LANES), lambda i: (i + base, 0)),
                pl.BlockSpec((1, d), lambda i: (0, 0)),
                pl.BlockSpec((1, d), lambda i: (0, 0))]
    args = [yg, yg, xf, wts, g.reshape(1, d), b.reshape(1, d)]
    aliases = {}
    if prev is not None:
        in_specs.append(pl.BlockSpec(memory_space=pl.ANY))
        args.append(prev)
        aliases = {len(args) - 1: 0}
    return pl.pallas_call(
        _combine_kernel,
        out_shape=jax.ShapeDtypeStruct((t, d), jnp.float32),
        grid=(nt,),
        in_specs=in_specs,
        out_specs=pl.BlockSpec((tm, d), lambda i: (i + base, 0)),
        input_output_aliases=aliases,
        compiler_params=_cparams(("parallel",)),
    )(*args)


def _moe_ffn(xf, router, w1, w3, w2, g, b):
    t, d = xf.shape
    oi, ow, cnt, xp = _router(xf, router)
    e1, e2, rank1, rank2 = oi[0], oi[1], oi[2], oi[3]
    counts = cnt[0, :N_EXPERTS].astype(jnp.int32)
    tiles = (counts + MOE_TILE - 1) // MOE_TILE
    tile_end = jnp.cumsum(tiles)
    tile_start = tile_end - tiles
    offs = tile_start * MOE_TILE
    dest = jnp.concatenate([offs[e1] + rank1, offs[e2] + rank2])
    n_rows = TOP_K * t + N_EXPERTS * MOE_TILE
    n_tiles = n_rows // MOE_TILE
    tile_id = jnp.arange(n_tiles, dtype=jnp.int32)
    tile_expert = jnp.sum(tile_id[:, None] >= tile_end[None, :], axis=1)
    used = tile_expert < N_EXPERTS
    last_expert = jnp.sum(tile_end[-1] - 1 >= tile_end)
    tile_expert = jnp.where(used, tile_expert, last_expert).astype(jnp.int32)
    tile_valid = jnp.clip(counts[tile_expert] - (tile_id - tile_start[tile_expert]) * MOE_TILE, 0, MOE_TILE)
    tile_valid = jnp.where(used, tile_valid, 0).astype(jnp.int32)
    xs = _sc_scatter_rows(xp, dest.reshape(TOP_K, t), n_rows)
    y = _experts(tile_expert, tile_valid, xs, w1, w3, w2)
    n_parts = 2
    tp = t // n_parts
    out = None
    for part in range(n_parts):
        idx = jnp.concatenate([dest[part * tp:(part + 1) * tp], dest[t + part * tp:t + (part + 1) * tp]])
        out = _combine(_sc_gather_rows(y, idx), xf, ow, g, b, part, n_parts, prev=out)
    return out


def _prep_layer(l, p, in_cols):
    bf = jnp.bfloat16
    out = {}
    n_gate = N_BRANCH * D_MODEL
    w_l = p["w_in_bf16"][l]
    out["w_gate"] = 0.5 * w_l[:, int(in_cols[0]):int(in_cols[0]) + n_gate]
    out["w_rest"] = _take_cols(w_l, in_cols[n_gate:])
    deint = np.concatenate([np.arange(0, HEAD_DIM, 2), np.arange(1, HEAD_DIM, 2)])
    lane_dim = np.concatenate([deint[:32], deint[:32], deint[32:], deint[32:]])
    out["gq_a"] = p["a_q_norm"][l][lane_dim].reshape(1, LANES)
    out["gk_a"] = p["a_k_norm"][l][lane_dim].reshape(1, LANES)
    per_head = C_NOPE + C_ROPE
    uq_cols = []
    for h in range(C_HEADS):
        base = h * per_head
        uq_cols += [base + np.arange(32), base + C_NOPE + np.arange(0, C_ROPE, 2), np.full(16, -1),
                    base + 32 + np.arange(32), base + C_NOPE + np.arange(1, C_ROPE, 2), np.full(16, -1)]
    wuq = _take_cols(p["c_w_uq"][l], np.concatenate(uq_cols))
    out["wuq"] = jnp.concatenate([wuq, jnp.zeros((64, wuq.shape[1]), wuq.dtype)], axis=0).astype(bf)
    uk_cols, uv_cols = [], []
    for h in range(C_HEADS):
        base = h * (C_NOPE + C_V)
        uk_cols += [base + np.arange(32), np.full(32, -1), base + 32 + np.arange(32), np.full(32, -1)]
        uv_cols += [base + C_NOPE + np.arange(C_V)]
    out["wuk"] = _take_cols(p["c_w_ukv"][l], np.concatenate(uk_cols)).astype(bf)
    out["wuv"] = _take_cols(p["c_w_ukv"][l], np.concatenate(uv_cols)).astype(bf)
    out["gq_c"] = jnp.concatenate([p["c_q_norm"][l], jnp.zeros((64,), jnp.float32)]).reshape(1, 256)
    out["gkv_c"] = p["c_kv_norm"][l].reshape(1, C_KV_RANK)
    out["conv_w"] = jnp.concatenate([p["d_conv_w"][l][:, 0, :], jnp.zeros((1, D_CH), jnp.float32)], axis=0)
    out["conv_b"] = p["d_conv_b"][l].reshape(1, D_CH)
    out["d_ln_g"] = p["d_ln_g"][l].reshape(1, D_CH)
    out["d_ln_b"] = p["d_ln_b"][l].reshape(1, D_CH)
    a_rows = np.concatenate([(kvh * 2 + g) * HEAD_DIM + np.arange(HEAD_DIM)
                             for g in range(2) for kvh in range(A_KV_HEADS)])
    wb = p["w_branch"][l]
    out["w_branch"] = (0.5 * jnp.stack([wb[0][a_rows], wb[1], wb[2], wb[3]], axis=0)).astype(bf)
    out["w_out"] = p["w_out"][l].astype(bf)
    return out


def kernel(x, ln_in_g, ln_in_b, w_in, a_q_norm, a_k_norm, b_rpb, c_q_norm, c_kv_norm, c_w_uq, c_w_ukv,
           d_conv_w, d_conv_b, d_ln_g, d_ln_b, w_branch, w_out, ln_mix_g, ln_mix_b,
           ffn_w1, ffn_w3, ffn_w2, moe_router, moe_w1, moe_w3, moe_w2, ln_ffn_g, ln_ffn_b):
    batch, seq, d = x.shape
    t = batch * seq
    bf = jnp.bfloat16
    params = dict(w_in_bf16=w_in.astype(bf), a_q_norm=a_q_norm, a_k_norm=a_k_norm, c_q_norm=c_q_norm, c_kv_norm=c_kv_norm,
                  c_w_uq=c_w_uq, c_w_ukv=c_w_ukv, d_conv_w=d_conv_w, d_conv_b=d_conv_b, d_ln_g=d_ln_g,
                  d_ln_b=d_ln_b, w_branch=w_branch, w_out=w_out)
    in_cols = _in_proj_columns()
    cos_a, sin_a, cos_c, sin_c = _rope_tables(seq)
    na_bias = _na_bias_tables(b_rpb, seq // GRID_W)
    xf = xb = None
    for l in range(DEPTH):
        p = _prep_layer(l, params, in_cols)
        if l == 0:
            h, xf = _in_proj(x.reshape(t, d), p["w_gate"], p["w_rest"], ln=(ln_in_g, ln_in_b))
        else:
            h = _in_proj(xb, p["w_gate"], p["w_rest"])
        ya = _attn_a(h, cos_a, sin_a, p["gq_a"], p["gk_a"], batch, seq)
        yb = _attn_b(h, na_bias, l, batch, seq)
        yc = _attn_c(h, p["wuq"], p["wuk"], p["wuv"], p["gq_c"], p["gkv_c"], cos_c, sin_c, batch, seq)
        yd = _conv_branch(h, p["conv_w"], p["conv_b"], p["d_ln_g"], p["d_ln_b"], batch, seq)
        xf, xb = _merge(h, ya, yb, yc, yd, xf, p["w_branch"], p["w_out"], ln_mix_g[l], ln_mix_b[l])
        if l % 2 == 0:
            i = l // 2
            xf, xb = _ffn(xb, xf, ffn_w1[i].astype(bf), ffn_w3[i].astype(bf), ffn_w2[i].astype(bf),
                          ln_ffn_g[l], ln_ffn_b[l])
        else:
            i = l // 2
            xf = _moe_ffn(xf, moe_router[i], moe_w1[i], moe_w3[i], moe_w2[i], ln_ffn_g[l], ln_ffn_b[l])
            xb = xf.astype(bf)
    return xf.reshape(batch, seq, d)
```

```python
import functools

import numpy as np
import jax
import jax.numpy as jnp
from jax import lax
from jax.experimental import pallas as pl
from jax.experimental.pallas import tpu as pltpu
from jax.experimental.pallas import tpu_sc as plsc

D_MODEL = 1024
DEPTH = 2
GRID_W = 64
HEAD_DIM = 64
ROPE_THETA = 10000.0
RMS_EPS = 1e-6
LN_EPS = 1e-5
A_HEADS = 4
A_KV_HEADS = 2
B_HEADS = 4
NA_WIN_H = 8
NA_WIN_W = 16
C_HEADS = 4
C_NOPE = 64
C_ROPE = 32
C_V = 64
C_Q_RANK = 192
C_KV_RANK = 128
D_CH = 256
CONV_W = 31
N_BRANCH = 4
MIX_W = 256
D_FF = 2816
N_EXPERTS = 8
TOP_K = 2
D_FF_EXPERT = 3584
ALPHA = (2 * DEPTH) ** 0.25

LANES = 128
SUBLANES = 8
MXU_TILE = 256
VMEM_LIMIT = 48 * 1024 * 1024

LOG2E = 1.4426950408889634
NEG_BIG = -1e30

COL_GATES = 0
COL_AQ = 4096
COL_AKV = 4352
COL_C = 4608
COL_D = 5120
COL_BQ = 5632
COL_BK = 5888
COL_BV = 6144
D_IN_PAD = 6400

NA_QROWS = 4
NA_KROWS = 12

MOE_TILE = 2048
MOE_SUB = 512
MOE_FCHUNK = 512
MOE_VMEM_LIMIT = 56 * 1024 * 1024


def _in_proj_columns():
    o_aq, o_ak, o_av = 0, 256, 384
    o_bq, o_bk, o_bv = 512, 768, 1024
    o_cq, o_ckv, o_kpe = 1280, 1472, 1600
    o_d, o_g = 1632, 2144
    deint = np.concatenate([np.arange(0, HEAD_DIM, 2), np.arange(1, HEAD_DIM, 2)])
    cols = [o_g + np.arange(N_BRANCH * D_MODEL)]
    half = HEAD_DIM // 2
    for g in range(2):
        for part in range(2):
            for kvh in range(A_KV_HEADS):
                cols.append(o_aq + (kvh * 2 + g) * HEAD_DIM + deint[part * half:(part + 1) * half])
    for part in range(2):
        for kvh in range(A_KV_HEADS):
            cols.append(o_ak + kvh * HEAD_DIM + deint[part * half:(part + 1) * half])
    cols.append(o_av + np.arange(A_KV_HEADS * HEAD_DIM))
    cols.append(o_cq + np.arange(C_Q_RANK))
    cols.append(np.full(64, -1))
    cols.append(o_ckv + np.arange(C_KV_RANK))
    cols.append(np.full(32, -1))
    cols.append(o_kpe + np.arange(0, C_ROPE, 2))
    cols.append(np.full(48, -1))
    cols.append(o_kpe + np.arange(1, C_ROPE, 2))
    cols.append(np.full(16, -1))
    cols.append(o_d + np.arange(2 * D_CH))
    cols.append(o_bq + np.arange(3 * B_HEADS * HEAD_DIM))
    cols = np.concatenate(cols)
    assert cols.shape[0] == D_IN_PAD
    return cols


def _take_cols(w, cols):
    cols = np.asarray(cols)
    runs, i, n = [], 0, len(cols)
    while i < n:
        j = i + 1
        if cols[i] < 0:
            while j < n and cols[j] < 0:
                j += 1
            kind = "zero"
        else:
            while j < n and cols[j] == cols[j - 1] + 1:
                j += 1
            kind = "slice" if j - i >= 64 else "shuffle"
        if kind == "shuffle" and runs and runs[-1][0] == "shuffle":
            runs[-1] = ("shuffle", runs[-1][1], j)
        else:
            runs.append((kind, i, j))
        i = j
    parts = []
    for kind, i, j in runs:
        if kind == "zero":
            parts.append(jnp.zeros((w.shape[0], j - i), w.dtype))
        elif kind == "slice":
            parts.append(w[:, int(cols[i]):int(cols[i]) + j - i])
        else:
            lo, hi = int(cols[i:j].min()), int(cols[i:j].max()) + 1
            onehot = np.zeros((hi - lo, j - i), np.float32)
            onehot[cols[i:j] - lo, np.arange(j - i)] = 1.0
            parts.append(jnp.dot(w[:, lo:hi], jnp.asarray(onehot, w.dtype), precision=lax.Precision.HIGHEST))
    return jnp.concatenate(parts, axis=1)


def _rope_tables(seq):
    t = np.arange(seq)
    row = (t // GRID_W).astype(np.float32)
    col = (t % GRID_W).astype(np.float32)

    def angles(rot_dim):
        n_freq = rot_dim // 4
        inv = jnp.asarray(ROPE_THETA, jnp.float32) ** (-jnp.arange(n_freq, dtype=jnp.float32) / n_freq)
        return jnp.concatenate([jnp.asarray(row)[:, None] * inv, jnp.asarray(col)[:, None] * inv], axis=-1)

    ang_a = angles(HEAD_DIM)
    ca, sa = jnp.cos(ang_a), jnp.sin(ang_a)
    cos_a = jnp.concatenate([ca, ca, ca, ca], axis=-1)
    sin_a = jnp.concatenate([-sa, -sa, sa, sa], axis=-1)
    ang_c = angles(C_ROPE)
    cc, sc = jnp.cos(ang_c), jnp.sin(ang_c)
    one = jnp.ones((seq, 64), jnp.float32)
    zero = jnp.zeros((seq, 64), jnp.float32)
    cos_c = jnp.concatenate([one[:, :32], cc, one[:, :48], cc, one[:, :16]], axis=-1)
    sin_c = jnp.concatenate([zero[:, :32], -sc, zero[:, :48], sc, zero[:, :16]], axis=-1)
    return cos_a, sin_a, cos_c, sin_c


def _na_bias_tables(rpb, rows):
    wh = min(NA_WIN_H, rows)
    n_dr, n_dc = 2 * NA_WIN_H - 1, 2 * NA_WIN_W - 1
    qc = np.arange(GRID_W)[:, None]
    kc = np.arange(GRID_W)[None, :]
    col_start = np.clip(qc - NA_WIN_W // 2, 0, GRID_W - NA_WIN_W)
    ok_c = (kc >= col_start) & (kc < col_start + NA_WIN_W)
    dc = np.clip(kc - qc + (NA_WIN_W - 1), 0, n_dc - 1)
    oh_c = np.eye(n_dc, dtype=np.float32)[dc.reshape(-1)]
    oh_r, ok_r = [], []
    for r0 in (0, NA_QROWS, rows - NA_QROWS):
        ws = int(np.clip(r0 - 4, 0, rows - NA_KROWS))
        qr = r0 + np.arange(NA_QROWS)[:, None]
        kr = ws + np.arange(NA_KROWS)[None, :]
        row_start = np.clip(qr - wh // 2, 0, rows - wh)
        ok_r.append((kr >= row_start) & (kr < row_start + wh))
        dr = np.clip(kr - qr + (NA_WIN_H - 1), 0, n_dr - 1)
        oh_r.append(np.eye(n_dr, dtype=np.float32)[dr.reshape(-1)])
    oh_r = np.concatenate(oh_r, axis=0)
    ok = np.stack(ok_r)[:, :, None, :, None] & ok_c[None, None, :, None, :]
    ok = ok.reshape(3, 1, NA_QROWS * GRID_W, NA_KROWS * GRID_W)
    hi = lax.Precision.HIGHEST
    t1 = jnp.einsum("ma,lhab->lhmb", jnp.asarray(oh_r), rpb.astype(jnp.float32) * LOG2E, precision=hi)
    t2 = jnp.einsum("lhmb,nb->lhmn", t1, jnp.asarray(oh_c), precision=hi).astype(jnp.bfloat16)
    nl, nh = rpb.shape[0], rpb.shape[1]
    t2 = t2.reshape(nl, nh, 3, NA_QROWS, NA_KROWS, GRID_W, GRID_W).transpose(0, 2, 1, 3, 5, 4, 6)
    bias = t2.reshape(nl, 3, nh, NA_QROWS * GRID_W, NA_KROWS * GRID_W)
    return jnp.where(jnp.asarray(ok), bias, jnp.asarray(NEG_BIG, jnp.bfloat16))


def _cparams(sem):
    return pltpu.CompilerParams(dimension_semantics=sem, vmem_limit_bytes=VMEM_LIMIT)


def _layer_norm_rows(z, g, b):
    mu = jnp.mean(z, axis=-1, keepdims=True)
    zc = z - mu
    var = jnp.mean(zc * zc, axis=-1, keepdims=True)
    return zc * lax.rsqrt(var + LN_EPS) * g + b


def _sigmoid(x):
    return 0.5 * jnp.tanh(0.5 * x) + 0.5


def _dot(a, b):
    return jnp.dot(a, b, preferred_element_type=jnp.float32)


def _dot_nt(a, b):
    return lax.dot_general(a, b, (((1,), (1,)), ((), ())), preferred_element_type=jnp.float32)


IN_PROJ_CHUNK = 1280


def _project_chunks(xb, w_refs, h_ref):
    off = 0
    for w_ref in w_refs:
        n_tiles = w_ref.shape[1] // MXU_TILE
        n_chunks = -(-w_ref.shape[1] // IN_PROJ_CHUNK)
        lo = 0
        for c in range(n_chunks):
            hi = (n_tiles * (c + 1) // n_chunks) * MXU_TILE
            h_ref[:, off + lo:off + hi] = _dot(xb, w_ref[:, lo:hi]).astype(h_ref.dtype)
            lo = hi
        off += w_ref.shape[1]


def _in_proj_kernel(x_ref, wg_ref, wr_ref, h_ref):
    _project_chunks(x_ref[...], (wg_ref, wr_ref), h_ref)


def _in_proj_ln_kernel(x_ref, g_ref, b_ref, wg_ref, wr_ref, h_ref, xf_ref):
    y = _layer_norm_rows(x_ref[...], g_ref[...], b_ref[...])
    xf_ref[...] = y
    _project_chunks(y.astype(jnp.bfloat16), (wg_ref, wr_ref), h_ref)


def _in_proj(x, w_gate, w_rest, ln=None, tm=512):
    t, d = x.shape
    n = w_gate.shape[1] + w_rest.shape[1]
    row = lambda width: pl.BlockSpec((tm, width), lambda i: (i, 0))
    vec = pl.BlockSpec((1, d), lambda i: (0, 0))
    resident = lambda w: pl.BlockSpec(w.shape, lambda i: (0, 0), pipeline_mode=pl.Buffered(1))
    h_shape = jax.ShapeDtypeStruct((t, n), jnp.bfloat16)
    if ln is None:
        return pl.pallas_call(
            _in_proj_kernel, out_shape=h_shape, grid=(t // tm,),
            in_specs=[row(d), resident(w_gate), resident(w_rest)], out_specs=row(n),
            compiler_params=_cparams(("parallel",)),
        )(x, w_gate, w_rest)
    g, b = ln
    return pl.pallas_call(
        _in_proj_ln_kernel,
        out_shape=(h_shape, jax.ShapeDtypeStruct((t, d), jnp.float32)),
        grid=(t // tm,),
        in_specs=[row(d), vec, vec, resident(w_gate), resident(w_rest)], out_specs=(row(n), row(d)),
        compiler_params=_cparams(("parallel",)),
    )(x, g.reshape(1, d), b.reshape(1, d), w_gate, w_rest)


def _ones_halves(vblk):
    lane = lax.broadcasted_iota(jnp.int32, vblk.shape, 1)
    ones = jnp.ones_like(vblk)
    return jnp.where(lane < HEAD_DIM, vblk, ones), jnp.where(lane < HEAD_DIM, ones, vblk)


def _attend_pair(s_lo, s_hi, v_lo, v_hi, lo_q):
    outs = []
    for s, v in ((s_lo, v_lo), (s_hi, v_hi)):
        p = jnp.exp2(s - jnp.max(s, axis=-1, keepdims=True)).astype(jnp.bfloat16)
        outs.append(_dot(p, v))
    num = jnp.where(lo_q, outs[0], outs[1])
    den = jnp.where(lo_q, pltpu.roll(outs[0], HEAD_DIM, 1), pltpu.roll(outs[1], HEAD_DIM, 1))
    return num / den


def _attn_a_kernel(q_ref, kv_ref, cos_ref, sin_ref, gq_ref, gk_ref, o_ref, ks_ref, va_ref, *, tq):
    s_len = q_ref.shape[0]

    def first_head(shape):
        lane = lax.broadcasted_iota(jnp.int32, shape, 1)
        return (lane % HEAD_DIM) < HEAD_DIM // 2

    def prep(x, g, cos, sin):
        head_a = first_head(x.shape)
        x2 = x * x
        ss_a = jnp.sum(jnp.where(head_a, x2, 0.0), axis=-1, keepdims=True)
        ss_b = jnp.sum(jnp.where(head_a, 0.0, x2), axis=-1, keepdims=True)
        ms = jnp.where(head_a, ss_a, ss_b) * (1.0 / HEAD_DIM)
        xn = x * lax.rsqrt(ms + RMS_EPS) * g
        return xn * cos + pltpu.roll(xn, LANES // 2, 1) * sin

    ks_ref[...] = prep(kv_ref[:, 0:LANES].astype(jnp.float32), gk_ref[...],
                       cos_ref[...], sin_ref[...]).astype(jnp.bfloat16)
    scale = HEAD_DIM ** -0.5 * LOG2E
    lane_q = lax.broadcasted_iota(jnp.int32, (tq, LANES), 1)
    lo_q = lane_q < HEAD_DIM
    head_a_q = first_head((tq, LANES))

    va_ref[0], va_ref[1] = _ones_halves(kv_ref[:, LANES:2 * LANES])

    def body(i, carry):
        r = pl.multiple_of(i * tq, tq)
        k = ks_ref[...]
        cos = cos_ref[pl.ds(r, tq), :]
        sin = sin_ref[pl.ds(r, tq), :]
        for g in range(2):
            qt = prep(q_ref[pl.ds(r, tq), g * LANES:(g + 1) * LANES].astype(jnp.float32), gq_ref[...], cos, sin)
            qt = (qt * scale).astype(jnp.bfloat16)
            zero = jnp.zeros_like(qt)
            out = _attend_pair(_dot_nt(jnp.where(head_a_q, qt, zero), k), _dot_nt(jnp.where(head_a_q, zero, qt), k),
                               va_ref[0], va_ref[1], lo_q)
            o_ref[pl.ds(r, tq), g * LANES:(g + 1) * LANES] = out.astype(o_ref.dtype)
        return carry

    lax.fori_loop(0, s_len // tq, body, 0, unroll=4)


def _attn_a(h, cos_a, sin_a, gq, gk, batch, seq, tq=256):
    kern = functools.partial(_attn_a_kernel, tq=tq)
    return pl.pallas_call(
        kern,
        out_shape=jax.ShapeDtypeStruct((batch * seq, MIX_W), jnp.bfloat16),
        grid=(batch,),
        in_specs=[pl.BlockSpec((seq, 256), lambda b: (b, COL_AQ // 256)),
                  pl.BlockSpec((seq, 256), lambda b: (b, COL_AKV // 256)),
                  pl.BlockSpec((seq, LANES), lambda b: (0, 0)),
                  pl.BlockSpec((seq, LANES), lambda b: (0, 0)),
                  pl.BlockSpec((1, LANES), lambda b: (0, 0)),
                  pl.BlockSpec((1, LANES), lambda b: (0, 0))],
        out_specs=pl.BlockSpec((seq, MIX_W), lambda b: (b, 0)),
        scratch_shapes=[pltpu.VMEM((seq, LANES), jnp.bfloat16), pltpu.VMEM((2, seq, LANES), jnp.bfloat16)],
        compiler_params=_cparams(("parallel",)),
    )(h, h, cos_a, sin_a, gq, gk)


def _attn_b_kernel(q_ref, k_ref, v_ref, bias_ref, o_ref, va_ref, *, rows):
    nq = NA_QROWS * GRID_W
    nk = NA_KROWS * GRID_W
    lane_q = lax.broadcasted_iota(jnp.int32, (nq, LANES), 1)
    lo_q = lane_q < HEAD_DIM
    n_groups = rows // NA_QROWS
    scale = HEAD_DIM ** -0.5 * LOG2E
    for blk in range(B_HEADS // 2):
        va_ref[2 * blk], va_ref[2 * blk + 1] = _ones_halves(v_ref[:, blk * LANES:(blk + 1) * LANES])

    def body(gi, carry):
        ws = jnp.clip(gi * NA_QROWS - 4, 0, rows - NA_KROWS)
        cls = jnp.where(gi == 0, 0, jnp.where(gi == n_groups - 1, 2, 1))
        qr = pl.multiple_of(gi * nq, nq)
        kr = pl.multiple_of(ws * GRID_W, GRID_W)
        for blk in range(B_HEADS // 2):
            sl = slice(blk * LANES, (blk + 1) * LANES)
            qt = q_ref[pl.ds(qr, nq), sl] * scale
            kw = k_ref[pl.ds(kr, nk), sl]
            zero = jnp.zeros_like(qt)
            s_lo = _dot_nt(jnp.where(lo_q, qt, zero), kw) + bias_ref[cls, 2 * blk].astype(jnp.float32)
            s_hi = _dot_nt(jnp.where(lo_q, zero, qt), kw) + bias_ref[cls, 2 * blk + 1].astype(jnp.float32)
            out = _attend_pair(s_lo, s_hi, va_ref[2 * blk, pl.ds(kr, nk), :], va_ref[2 * blk + 1, pl.ds(kr, nk), :], lo_q)
            o_ref[pl.ds(qr, nq), sl] = out.astype(o_ref.dtype)
        return carry

    lax.fori_loop(0, n_groups, body, 0, unroll=True)


def _attn_b(h, bias, layer, batch, seq):
    rows = seq // GRID_W
    kern = functools.partial(_attn_b_kernel, rows=rows)
    return pl.pallas_call(
        kern,
        out_shape=jax.ShapeDtypeStruct((batch * seq, MIX_W), jnp.bfloat16),
        grid=(batch,),
        in_specs=[pl.BlockSpec((seq, 256), lambda b: (b, COL_BQ // 256)),
                  pl.BlockSpec((seq, 256), lambda b: (b, COL_BK // 256)),
                  pl.BlockSpec((seq, 256), lambda b: (b, COL_BV // 256)),
                  pl.BlockSpec((None,) + bias.shape[1:], lambda b: (layer, 0, 0, 0, 0))],
        out_specs=pl.BlockSpec((seq, MIX_W), lambda b: (b, 0)),
        scratch_shapes=[pltpu.VMEM((B_HEADS, seq, LANES), jnp.bfloat16)],
        compiler_params=_cparams(("parallel",)),
    )(h, h, h, bias)


def _attn_c_kernel(hc_ref, wuq_ref, wuk_ref, wuv_ref, gq_ref, gkv_ref, cos_ref, sin_ref, o_ref,
                   qs_ref, ks_ref, vs_ref, *, tq):
    s_len = hc_ref.shape[0]
    cos = cos_ref[...]
    sin = sin_ref[...]

    def rotate(x):
        return x * cos + pltpu.roll(x, LANES // 2, 1) * sin

    cq = hc_ref[:, 0:256].astype(jnp.float32)
    ms = jnp.sum(cq * cq, axis=-1, keepdims=True) * (1.0 / C_Q_RANK)
    cqn = (cq * lax.rsqrt(ms + RMS_EPS) * gq_ref[...]).astype(jnp.bfloat16)
    ckv = hc_ref[:, 256:384].astype(jnp.float32)
    ms = jnp.mean(ckv * ckv, axis=-1, keepdims=True)
    kvn = (ckv * lax.rsqrt(ms + RMS_EPS) * gkv_ref[...]).astype(jnp.bfloat16)
    kpe = rotate(hc_ref[:, 384:512].astype(jnp.float32))
    v_all = _dot(kvn, wuv_ref[...]).astype(jnp.bfloat16)
    for blk in range(C_HEADS // 2):
        vs_ref[2 * blk], vs_ref[2 * blk + 1] = _ones_halves(v_all[:, blk * LANES:(blk + 1) * LANES])
    scale = (C_NOPE + C_ROPE) ** -0.5 * LOG2E
    for h in range(C_HEADS):
        sl = slice(h * LANES, (h + 1) * LANES)
        qh = rotate(_dot(cqn, wuq_ref[:, sl])) * scale
        qs_ref[:, sl] = qh.astype(jnp.bfloat16)
        ks_ref[:, sl] = (_dot(kvn, wuk_ref[:, sl]) + kpe).astype(jnp.bfloat16)

    lane_q = lax.broadcasted_iota(jnp.int32, (tq, LANES), 1)
    lo_q = lane_q < C_V

    def body(i, carry):
        r = pl.multiple_of(i * tq, tq)
        for blk in range(C_HEADS // 2):
            s = []
            for par in range(2):
                sl = slice((2 * blk + par) * LANES, (2 * blk + par + 1) * LANES)
                s.append(_dot_nt(qs_ref[pl.ds(r, tq), sl], ks_ref[:, sl]))
            out = _attend_pair(s[0], s[1], vs_ref[2 * blk], vs_ref[2 * blk + 1], lo_q)
            o_ref[pl.ds(r, tq), blk * LANES:(blk + 1) * LANES] = out.astype(o_ref.dtype)
        return carry

    lax.fori_loop(0, s_len // tq, body, 0, unroll=4)


def _attn_c(h, wuq, wuk, wuv, gq, gkv, cos_c, sin_c, batch, seq, tq=256):
    kern = functools.partial(_attn_c_kernel, tq=tq)
    const = lambda shape: pl.BlockSpec(shape, lambda b: (0,) * len(shape))
    return pl.pallas_call(
        kern,
        out_shape=jax.ShapeDtypeStruct((batch * seq, MIX_W), jnp.bfloat16),
        grid=(batch,),
        in_specs=[pl.BlockSpec((seq, 512), lambda b: (b, COL_C // 512)),
                  const(wuq.shape), const(wuk.shape), const(wuv.shape),
                  const(gq.shape), const(gkv.shape), const(cos_c.shape), const(sin_c.shape)],
        out_specs=pl.BlockSpec((seq, MIX_W), lambda b: (b, 0)),
        scratch_shapes=[pltpu.VMEM((seq, C_HEADS * LANES), jnp.bfloat16),
                        pltpu.VMEM((seq, C_HEADS * LANES), jnp.bfloat16),
                        pltpu.VMEM((C_HEADS, seq, LANES), jnp.bfloat16)],
        compiler_params=_cparams(("parallel",)),
    )(h, wuq, wuk, wuv, gq, gkv, cos_c, sin_c)


CONV_PAD = 16
CONV_ROWS = 128


def _conv_kernel(hd_ref, w_ref, cb_ref, g_ref, b_ref, o_ref, up_ref):
    s_len = hd_ref.shape[0]
    a = hd_ref[:, 0:D_CH].astype(jnp.float32)
    gate = hd_ref[:, D_CH:2 * D_CH].astype(jnp.float32)
    zeros = jnp.zeros((CONV_PAD, D_CH), jnp.float32)
    up_ref[0, 0:CONV_PAD, :] = zeros
    up_ref[0, CONV_PAD + s_len:2 * CONV_PAD + s_len, :] = zeros
    up_ref[0, CONV_PAD:CONV_PAD + s_len, :] = a * _sigmoid(gate)
    n_shifted = s_len + 2 * CONV_PAD - SUBLANES
    for s in range(1, SUBLANES):
        up_ref[s, 0:n_shifted, :] = up_ref[0, s:s + n_shifted, :]
    shift = CONV_PAD - CONV_W // 2
    for c in range(s_len // CONV_ROWS):
        acc = jnp.zeros((CONV_ROWS, D_CH), jnp.float32)
        for j in range(CONV_W):
            s = (j + shift) % SUBLANES
            start = c * CONV_ROWS + j + shift - s
            acc = acc + up_ref[s, start:start + CONV_ROWS, :] * w_ref[j:j + 1, :]
        u = acc + cb_ref[...]
        y = _layer_norm_rows(u, g_ref[...], b_ref[...])
        o_ref[c * CONV_ROWS:(c + 1) * CONV_ROWS, :] = (y * _sigmoid(y)).astype(o_ref.dtype)


def _conv_branch(h, w, cb, g, b, batch, seq):
    const = lambda shape: pl.BlockSpec(shape, lambda i: (0,) * len(shape))
    return pl.pallas_call(
        _conv_kernel,
        out_shape=jax.ShapeDtypeStruct((batch * seq, MIX_W), jnp.bfloat16),
        grid=(batch,),
        in_specs=[pl.BlockSpec((seq, 512), lambda i: (i, COL_D // 512)),
                  const(w.shape), const(cb.shape), const(g.shape), const(b.shape)],
        out_specs=pl.BlockSpec((seq, MIX_W), lambda i: (i, 0)),
        scratch_shapes=[pltpu.VMEM((SUBLANES, seq + 2 * CONV_PAD, D_CH), jnp.float32)],
        compiler_params=_cparams(("parallel",)),
    )(h, w, cb, g, b)


def _merge_kernel(gl_ref, ya_ref, yb_ref, yc_ref, yd_ref, x_ref, wb_ref, wo_ref, g_ref, b_ref,
                  of_ref, ob_ref):
    merged = None
    for n, y_ref in enumerate((ya_ref, yb_ref, yc_ref, yd_ref)):
        t = jnp.tanh(gl_ref[:, n * D_MODEL:(n + 1) * D_MODEL].astype(jnp.float32))
        term = (t + 1.0) * _dot(y_ref[...], wb_ref[n])
        merged = term if merged is None else merged + term
    mix = _dot(merged.astype(jnp.bfloat16), wo_ref[...])
    y = _layer_norm_rows(ALPHA * x_ref[...] + mix, g_ref[...], b_ref[...])
    of_ref[...] = y
    ob_ref[...] = y.astype(jnp.bfloat16)


def _merge(h, ya, yb, yc, yd, x, wb, wo, g, b, tm=512):
    t, d = x.shape
    row = lambda w: pl.BlockSpec((tm, w), lambda i: (i, 0))
    const = lambda shape: pl.BlockSpec(shape, lambda i: (0,) * len(shape))
    return pl.pallas_call(
        _merge_kernel,
        out_shape=(jax.ShapeDtypeStruct((t, d), jnp.float32), jax.ShapeDtypeStruct((t, d), jnp.bfloat16)),
        grid=(t // tm,),
        in_specs=[pl.BlockSpec((tm, N_BRANCH * d), lambda i: (i, COL_GATES)),
                  row(MIX_W), row(MIX_W), row(MIX_W), row(MIX_W), row(d),
                  const(wb.shape), const(wo.shape), const((1, d)), const((1, d))],
        out_specs=(row(d), row(d)),
        compiler_params=_cparams(("parallel",)),
    )(h, ya, yb, yc, yd, x, wb, wo, g.reshape(1, d), b.reshape(1, d))


def _ffn_kernel(xb_ref, xf_ref, w1_ref, w3_ref, w2_ref, g_ref, b_ref, of_ref, ob_ref):
    x = xb_ref[...]
    f = w2_ref.shape[0]
    split = (f // MXU_TILE + 1) // 2 * MXU_TILE
    acc = None
    for lo, hi in ((0, split), (split, f)):
        a = _dot(x, w1_ref[:, lo:hi])
        hid = (a * _sigmoid(a) * _dot(x, w3_ref[:, lo:hi])).astype(jnp.bfloat16)
        part = _dot(hid, w2_ref[lo:hi, :])
        acc = part if acc is None else acc + part
    y = _layer_norm_rows(ALPHA * xf_ref[...] + acc, g_ref[...], b_ref[...])
    of_ref[...] = y
    ob_ref[...] = y.astype(jnp.bfloat16)


def _ffn(xb, xf, w1, w3, w2, g, b, tm=512):
    t, d = xf.shape
    row = pl.BlockSpec((tm, d), lambda i: (i, 0))
    vec = pl.BlockSpec((1, d), lambda i: (0, 0))
    resident = lambda shape: pl.BlockSpec(shape, lambda i: (0, 0), pipeline_mode=pl.Buffered(1))
    return pl.pallas_call(
        _ffn_kernel,
        out_shape=(jax.ShapeDtypeStruct((t, d), jnp.float32), jax.ShapeDtypeStruct((t, d), jnp.bfloat16)),
        grid=(t // tm,),
        in_specs=[row, row, resident(w1.shape), resident(w3.shape), resident(w2.shape), vec, vec],
        out_specs=(row, row),
        compiler_params=_cparams(("parallel",)),
    )(xb, xf, w1, w3, w2, g.reshape(1, d), b.reshape(1, d))


def _pack_rows(x):
    half = x.shape[1] // 2
    bits = lax.bitcast_convert_type(x.astype(jnp.bfloat16).astype(jnp.float32), jnp.uint32)
    return (bits[:, :half] >> 16) | (bits[:, half:] & jnp.uint32(0xFFFF0000))


def _unpack_rows(p):
    lo = lax.bitcast_convert_type(p << 16, jnp.float32)
    hi = lax.bitcast_convert_type(p & jnp.uint32(0xFFFF0000), jnp.float32)
    return jnp.concatenate([lo, hi], axis=1)


def _router_kernel(x_ref, r_ref, tri_ref, oi_ref, ow_ref, cnt_ref, xp_ref, carry_ref):
    i = pl.program_id(0)

    @pl.when(i == 0)
    def _():
        carry_ref[...] = jnp.zeros_like(carry_ref)

    x = x_ref[...]
    r = r_ref[...]
    xh = x.astype(jnp.bfloat16)
    xl = (x - xh.astype(jnp.float32)).astype(jnp.bfloat16)
    rh = r.astype(jnp.bfloat16)
    rl = (r - rh.astype(jnp.float32)).astype(jnp.bfloat16)
    r2 = jnp.concatenate([rh, rl], axis=1)
    p = _dot(xh, r2) + _dot(xl, r2)
    logits = p[:, :LANES] + p[:, LANES:]
    tm = x.shape[0]
    lane = lax.broadcasted_iota(jnp.int32, (tm, LANES), 1)
    logits = jnp.where(lane < N_EXPERTS, logits, NEG_BIG)
    m1 = jnp.max(logits, axis=-1, keepdims=True)
    i1 = jnp.min(jnp.where(logits == m1, lane, LANES), axis=-1, keepdims=True)
    rest = jnp.where(lane == i1, NEG_BIG, logits)
    m2 = jnp.max(rest, axis=-1, keepdims=True)
    i2 = jnp.min(jnp.where(rest == m2, lane, LANES), axis=-1, keepdims=True)
    e = jnp.exp(m2 - m1)
    w1 = 1.0 / (1.0 + e)
    w2 = e / (1.0 + e)
    sel1 = lane == i1
    sel2 = lane == i2
    member = jnp.where(sel1 | sel2, 1.0, 0.0)
    before = _dot(tri_ref[...], member.astype(jnp.bfloat16)) + carry_ref[...]
    rank1 = jnp.sum(jnp.where(sel1, before, 0.0), axis=-1, keepdims=True).astype(jnp.int32)
    rank2 = jnp.sum(jnp.where(sel2, before, 0.0), axis=-1, keepdims=True).astype(jnp.int32)
    carry_ref[...] += jnp.sum(member, axis=0, keepdims=True)
    oi = jnp.where(lane == 0, i1, jnp.where(lane == 1, i2, jnp.where(lane == 2, rank1, rank2)))
    oi_ref[...] = oi.T[0:SUBLANES, :]
    ow_ref[...] = jnp.where(lane == 0, w1, w2)
    cnt_ref[...] = carry_ref[...]
    xp_ref[...] = _pack_rows(x)


def _router(xf, router, tm=512):
    t, d = xf.shape
    r_pad = jnp.zeros((d, LANES), jnp.float32).at[:, :N_EXPERTS].set(router.astype(jnp.float32))
    tri = jnp.asarray(np.tril(np.ones((tm, tm), np.float32), -1), jnp.bfloat16)
    return pl.pallas_call(
        _router_kernel,
        out_shape=(jax.ShapeDtypeStruct((SUBLANES, t), jnp.int32),
                   jax.ShapeDtypeStruct((t, LANES), jnp.float32),
                   jax.ShapeDtypeStruct((1, LANES), jnp.float32),
                   jax.ShapeDtypeStruct((t, d // 2), jnp.uint32)),
        grid=(t // tm,),
        in_specs=[pl.BlockSpec((tm, d), lambda i: (i, 0)),
                  pl.BlockSpec((d, LANES), lambda i: (0, 0)),
                  pl.BlockSpec((tm, tm), lambda i: (0, 0))],
        out_specs=(pl.BlockSpec((SUBLANES, tm), lambda i: (0, i)),
                   pl.BlockSpec((tm, LANES), lambda i: (i, 0)),
                   pl.BlockSpec((1, LANES), lambda i: (0, 0)),
                   pl.BlockSpec((tm, d // 2), lambda i: (i, 0))),
        scratch_shapes=[pltpu.VMEM((1, LANES), jnp.float32)],
        compiler_params=_cparams(("arbitrary",)),
    )(xf, r_pad, tri)


SC_CORES = 2
SC_SUBCORES = 16
SC_WINDOW = 64


def _sc_gather_rows(table, idx):
    m = idx.shape[0]
    d = table.shape[1]
    n_workers = SC_CORES * SC_SUBCORES
    per_w = m // n_workers
    n_win = per_w // SC_WINDOW
    assert per_w * n_workers == m and n_win * SC_WINDOW == per_w
    mesh = plsc.VectorSubcoreMesh(core_axis_name="c", subcore_axis_name="s")

    @functools.partial(
        pl.kernel, mesh=mesh,
        out_type=jax.ShapeDtypeStruct((m, d), table.dtype),
        scratch_types=[pltpu.VMEM((per_w,), jnp.int32),
                       pltpu.VMEM((SC_WINDOW, d), table.dtype),
                       pltpu.SemaphoreType.DMA],
    )
    def gather(table_hbm, idx_hbm, out_hbm, idx_v, rows_v, sem):
        wid = lax.axis_index("s") * SC_CORES + lax.axis_index("c")
        base = wid * per_w
        pltpu.sync_copy(idx_hbm.at[pl.ds(base, per_w)], idx_v)

        @pl.loop(0, n_win)
        def _(w):
            off = pl.multiple_of(w * SC_WINDOW, SC_WINDOW)
            pltpu.async_copy(table_hbm.at[idx_v.at[pl.ds(off, SC_WINDOW)]], rows_v, sem).wait()
            pltpu.sync_copy(rows_v, out_hbm.at[pl.ds(base + off, SC_WINDOW)])

    return gather(table, idx)


def _sc_scatter_rows(rows, idx, n_out):
    n_copies, m = idx.shape
    d = rows.shape[1]
    n_workers = SC_CORES * SC_SUBCORES
    per_w = m // n_workers
    n_win = per_w // SC_WINDOW
    assert per_w * n_workers == m and n_win * SC_WINDOW == per_w
    idx_w = idx.reshape(n_copies, n_workers, n_win, SC_WINDOW).transpose(1, 0, 2, 3)
    idx_w = idx_w.reshape(n_workers, n_copies * n_win, SC_WINDOW)
    mesh = plsc.VectorSubcoreMesh(core_axis_name="c", subcore_axis_name="s")

    @functools.partial(
        pl.kernel, mesh=mesh,
        out_type=jax.ShapeDtypeStruct((n_out, d), rows.dtype),
        scratch_types=[pltpu.VMEM((n_copies * n_win, SC_WINDOW), jnp.int32),
                       pltpu.VMEM((SC_WINDOW, d), rows.dtype)],
    )
    def scatter(rows_hbm, idx_hbm, out_hbm, idx_v, rows_v):
        wid = lax.axis_index("s") * SC_CORES + lax.axis_index("c")
        base = wid * per_w
        pltpu.sync_copy(idx_hbm.at[wid], idx_v)

        @pl.loop(0, n_win)
        def _(w):
            off = pl.multiple_of(w * SC_WINDOW, SC_WINDOW)
            pltpu.sync_copy(rows_hbm.at[pl.ds(base + off, SC_WINDOW)], rows_v)
            for k in range(n_copies):
                pltpu.sync_copy(rows_v, out_hbm.at[idx_v.at[k * n_win + w]])

    return scatter(rows, idx_w)


def _expert_kernel(te_ref, nv_ref, x_ref, w1_ref, w3_ref, w2_ref, o_ref,
                   w13s_ref, w2s_ref, xb_ref, acc_ref):
    i = pl.program_id(0)
    j = pl.program_id(1)
    last = pl.num_programs(1) - 1
    n_valid = nv_ref[i]
    n_piece = MOE_TILE // MOE_SUB
    piece_rows = [pl.ds(s * MOE_SUB, MOE_SUB) for s in range(n_piece)]

    for s, rows in enumerate(piece_rows):
        @pl.when((n_valid > s * MOE_SUB) & (j == 0))
        def _():
            row = lax.broadcasted_iota(jnp.int32, (MOE_SUB, 1), 0) + s * MOE_SUB
            x = jnp.where(row < n_valid, _unpack_rows(x_ref[rows, :]), 0.0)
            xb_ref[rows, :] = x.astype(jnp.bfloat16)
            acc_ref[rows, :] = jnp.zeros((MOE_SUB, acc_ref.shape[1]), jnp.float32)

    def compute(n_live):
        w13s_ref[:, 0:MOE_FCHUNK] = w1_ref[0].astype(jnp.bfloat16)
        w13s_ref[:, MOE_FCHUNK:2 * MOE_FCHUNK] = w3_ref[0].astype(jnp.bfloat16)
        w2s_ref[...] = w2_ref[0].astype(jnp.bfloat16)
        for rows in piece_rows[:n_live]:
            ab = _dot(xb_ref[rows, :], w13s_ref[...])
            a = ab[:, :MOE_FCHUNK]
            hid = (a * _sigmoid(a) * ab[:, MOE_FCHUNK:]).astype(jnp.bfloat16)
            acc_ref[rows, :] += _dot(hid, w2s_ref[...])

    for n_live in range(1, n_piece + 1):
        upper = n_live * MOE_SUB if n_live < n_piece else MOE_TILE
        pl.when((n_valid > (n_live - 1) * MOE_SUB) & (n_valid <= upper))(functools.partial(compute, n_live))

    for s, rows in enumerate(piece_rows):
        @pl.when((n_valid > s * MOE_SUB) & (j == last))
        def _():
            o_ref[rows, :] = _pack_rows(acc_ref[rows, :])

        @pl.when((n_valid <= s * MOE_SUB) & (j == last))
        def _():
            o_ref[rows, :] = jnp.zeros((MOE_SUB, o_ref.shape[1]), o_ref.dtype)


def _experts(tile_expert, tile_valid, xs, w1, w3, w2):
    n_rows = xs.shape[0]
    d, f = w1.shape[1], w1.shape[2]
    nj = f // MOE_FCHUNK

    def chunk(i, j, nv):
        return jnp.where(nv[i] > 0, j, nj - 1)

    return pl.pallas_call(
        _expert_kernel,
        out_shape=jax.ShapeDtypeStruct((n_rows, d // 2), jnp.uint32),
        grid_spec=pltpu.PrefetchScalarGridSpec(
            num_scalar_prefetch=2,
            grid=(n_rows // MOE_TILE, nj),
            in_specs=[pl.BlockSpec((MOE_TILE, d // 2), lambda i, j, te, nv: (i, 0)),
                      pl.BlockSpec((1, d, MOE_FCHUNK), lambda i, j, te, nv: (te[i], 0, chunk(i, j, nv))),
                      pl.BlockSpec((1, d, MOE_FCHUNK), lambda i, j, te, nv: (te[i], 0, chunk(i, j, nv))),
                      pl.BlockSpec((1, MOE_FCHUNK, d), lambda i, j, te, nv: (te[i], chunk(i, j, nv), 0))],
            out_specs=pl.BlockSpec((MOE_TILE, d // 2), lambda i, j, te, nv: (i, 0)),
            scratch_shapes=[pltpu.VMEM((d, 2 * MOE_FCHUNK), jnp.bfloat16),
                            pltpu.VMEM((MOE_FCHUNK, d), jnp.bfloat16),
                            pltpu.VMEM((MOE_TILE, d), jnp.bfloat16),
                            pltpu.VMEM((MOE_TILE, d), jnp.float32)],
        ),
        compiler_params=pltpu.CompilerParams(dimension_semantics=("arbitrary", "arbitrary"),
                                             vmem_limit_bytes=MOE_VMEM_LIMIT),
    )(tile_expert, tile_valid, xs, w1, w3, w2)


def _combine_kernel(ya_ref, yb_ref, x_ref, w_ref, g_ref, b_ref, *rest):
    o_ref = rest[-1]
    w = w_ref[...]
    f = w[:, 0:1] * _unpack_rows(ya_ref[...]) + w[:, 1:2] * _unpack_rows(yb_ref[...])
    o_ref[...] = _layer_norm_rows(ALPHA * x_ref[...] + f, g_ref[...], b_ref[...])


def _combine(yg, xf, wts, g, b, part, n_parts, prev=None, tm=512):
    t, d = xf.shape
    nt = t // n_parts // tm
    base = part * nt
    in_specs = [pl.BlockSpec((tm, d // 2), lambda i: (i, 0)),
                pl.BlockSpec((tm, d // 2), lambda i: (i + nt, 0)),
                pl.BlockSpec((tm, d), lambda i: (i + base, 0)),
                pl.BlockSpec((tm, LANES), lambda i: (i + base, 0)),
                pl.BlockSpec((1, d), lambda i: (0, 0)),
                pl.BlockSpec((1, d), lambda i: (0, 0))]
    args = [yg, yg, xf, wts, g.reshape(1, d), b.reshape(1, d)]
    aliases = {}
    if prev is not None:
        in_specs.append(pl.BlockSpec(memory_space=pl.ANY))
        args.append(prev)
        aliases = {len(args) - 1: 0}
    return pl.pallas_call(
        _combine_kernel,
        out_shape=jax.ShapeDtypeStruct((t, d), jnp.float32),
        grid=(nt,),
        in_specs=in_specs,
        out_specs=pl.BlockSpec((tm, d), lambda i: (i + base, 0)),
        input_output_aliases=aliases,
        compiler_params=_cparams(("parallel",)),
    )(*args)


def _moe_ffn(xf, router, w1, w3, w2, g, b):
    t, d = xf.shape
    oi, ow, cnt, xp = _router(xf, router)
    e1, e2, rank1, rank2 = oi[0], oi[1], oi[2], oi[3]
    counts = cnt[0, :N_EXPERTS].astype(jnp.int32)
    tiles = (counts + MOE_TILE - 1) // MOE_TILE
    tile_end = jnp.cumsum(tiles)
    tile_start = tile_end - tiles
    offs = tile_start * MOE_TILE
    dest = jnp.concatenate([offs[e1] + rank1, offs[e2] + rank2])
    n_rows = TOP_K * t + N_EXPERTS * MOE_TILE
    n_tiles = n_rows // MOE_TILE
    tile_id = jnp.arange(n_tiles, dtype=jnp.int32)
    tile_expert = jnp.sum(tile_id[:, None] >= tile_end[None, :], axis=1)
    used = tile_expert < N_EXPERTS
    last_expert = jnp.sum(tile_end[-1] - 1 >= tile_end)
    tile_expert = jnp.where(used, tile_expert, last_expert).astype(jnp.int32)
    tile_valid = jnp.clip(counts[tile_expert] - (tile_id - tile_start[tile_expert]) * MOE_TILE, 0, MOE_TILE)
    tile_valid = jnp.where(used, tile_valid, 0).astype(jnp.int32)
    xs = _sc_scatter_rows(xp, dest.reshape(TOP_K, t), n_rows)
    y = _experts(tile_expert, tile_valid, xs, w1, w3, w2)
    n_parts = 4
    tp = t // n_parts
    out = None
    for part in range(n_parts):
        idx = jnp.concatenate([dest[part * tp:(part + 1) * tp], dest[t + part * tp:t + (part + 1) * tp]])
        out = _combine(_sc_gather_rows(y, idx), xf, ow, g, b, part, n_parts, prev=out)
    return out


def _prep_layer(l, p, in_cols):
    bf = jnp.bfloat16
    out = {}
    n_gate = N_BRANCH * D_MODEL
    w_l = p["w_in_bf16"][l]
    out["w_gate"] = 0.5 * w_l[:, int(in_cols[0]):int(in_cols[0]) + n_gate]
    out["w_rest"] = _take_cols(w_l, in_cols[n_gate:])
    deint = np.concatenate([np.arange(0, HEAD_DIM, 2), np.arange(1, HEAD_DIM, 2)])
    lane_dim = np.concatenate([deint[:32], deint[:32], deint[32:], deint[32:]])
    out["gq_a"] = p["a_q_norm"][l][lane_dim].reshape(1, LANES)
    out["gk_a"] = p["a_k_norm"][l][lane_dim].reshape(1, LANES)
    per_head = C_NOPE + C_ROPE
    uq_cols = []
    for h in range(C_HEADS):
        base = h * per_head
        uq_cols += [base + np.arange(32), base + C_NOPE + np.arange(0, C_ROPE, 2), np.full(16, -1),
                    base + 32 + np.arange(32), base + C_NOPE + np.arange(1, C_ROPE, 2), np.full(16, -1)]
    wuq = _take_cols(p["c_w_uq"][l], np.concatenate(uq_cols))
    out["wuq"] = jnp.concatenate([wuq, jnp.zeros((64, wuq.shape[1]), wuq.dtype)], axis=0).astype(bf)
    uk_cols, uv_cols = [], []
    for h in range(C_HEADS):
        base = h * (C_NOPE + C_V)
        uk_cols += [base + np.arange(32), np.full(32, -1), base + 32 + np.arange(32), np.full(32, -1)]
        uv_cols += [base + C_NOPE + np.arange(C_V)]
    out["wuk"] = _take_cols(p["c_w_ukv"][l], np.concatenate(uk_cols)).astype(bf)
    out["wuv"] = _take_cols(p["c_w_ukv"][l], np.concatenate(uv_cols)).astype(bf)
    out["gq_c"] = jnp.concatenate([p["c_q_norm"][l], jnp.zeros((64,), jnp.float32)]).reshape(1, 256)
    out["gkv_c"] = p["c_kv_norm"][l].reshape(1, C_KV_RANK)
    out["conv_w"] = jnp.concatenate([p["d_conv_w"][l][:, 0, :], jnp.zeros((1, D_CH), jnp.float32)], axis=0)
    out["conv_b"] = p["d_conv_b"][l].reshape(1, D_CH)
    out["d_ln_g"] = p["d_ln_g"][l].reshape(1, D_CH)
    out["d_ln_b"] = p["d_ln_b"][l].reshape(1, D_CH)
    a_rows = np.concatenate([(kvh * 2 + g) * HEAD_DIM + np.arange(HEAD_DIM)
                             for g in range(2) for kvh in range(A_KV_HEADS)])
    wb = p["w_branch"][l]
    out["w_branch"] = (0.5 * jnp.stack([wb[0][a_rows], wb[1], wb[2], wb[3]], axis=0)).astype(bf)
    out["w_out"] = p["w_out"][l].astype(bf)
    return out


def kernel(x, ln_in_g, ln_in_b, w_in, a_q_norm, a_k_norm, b_rpb, c_q_norm, c_kv_norm, c_w_uq, c_w_ukv,
           d_conv_w, d_conv_b, d_ln_g, d_ln_b, w_branch, w_out, ln_mix_g, ln_mix_b,
           ffn_w1, ffn_w3, ffn_w2, moe_router, moe_w1, moe_w3, moe_w2, ln_ffn_g, ln_ffn_b):
    batch, seq, d = x.shape
    t = batch * seq
    bf = jnp.bfloat16
    params = dict(w_in_bf16=w_in.astype(bf), a_q_norm=a_q_norm, a_k_norm=a_k_norm, c_q_norm=c_q_norm, c_kv_norm=c_kv_norm,
                  c_w_uq=c_w_uq, c_w_ukv=c_w_ukv, d_conv_w=d_conv_w, d_conv_b=d_conv_b, d_ln_g=d_ln_g,
                  d_ln_b=d_ln_b, w_branch=w_branch, w_out=w_out)
    in_cols = _in_proj_columns()
    cos_a, sin_a, cos_c, sin_c = _rope_tables(seq)
    na_bias = _na_bias_tables(b_rpb, seq // GRID_W)
    xf = xb = None
    for l in range(DEPTH):
        p = _prep_layer(l, params, in_cols)
        if l == 0:
            h, xf = _in_proj(x.reshape(t, d), p["w_gate"], p["w_rest"], ln=(ln_in_g, ln_in_b))
        else:
            h = _in_proj(xb, p["w_gate"], p["w_rest"])
        ya = _attn_a(h, cos_a, sin_a, p["gq_a"], p["gk_a"], batch, seq)
        yb = _attn_b(h, na_bias, l, batch, seq)
        yc = _attn_c(h, p["wuq"], p["wuk"], p["wuv"], p["gq_c"], p["gkv_c"], cos_c, sin_c, batch, seq)
        yd = _conv_branch(h, p["conv_w"], p["conv_b"], p["d_ln_g"], p["d_ln_b"], batch, seq)
        xf, xb = _merge(h, ya, yb, yc, yd, xf, p["w_branch"], p["w_out"], ln_mix_g[l], ln_mix_b[l])
        if l % 2 == 0:
            i = l // 2
            xf, xb = _ffn(xb, xf, ffn_w1[i].astype(bf), ffn_w3[i].astype(bf), ffn_w2[i].astype(bf),
                          ln_ffn_g[l], ln_ffn_b[l])
        else:
            i = l // 2
            xf = _moe_ffn(xf, moe_router[i], moe_w1[i], moe_w3[i], moe_w2[i], ln_ffn_g[l], ln_ffn_b[l])
            xb = xf.astype(bf)
    return xf.reshape(batch, seq, d)
```

```python
import functools

import numpy as np
import jax
import jax.numpy as jnp
from jax import lax
from jax.experimental import pallas as pl
from jax.experimental.pallas import tpu as pltpu
from jax.experimental.pallas import tpu_sc as plsc

D_MODEL = 1024
DEPTH = 2
GRID_W = 64
HEAD_DIM = 64
ROPE_THETA = 10000.0
RMS_EPS = 1e-6
LN_EPS = 1e-5
A_HEADS = 4
A_KV_HEADS = 2
B_HEADS = 4
NA_WIN_H = 8
NA_WIN_W = 16
C_HEADS = 4
C_NOPE = 64
C_ROPE = 32
C_V = 64
C_Q_RANK = 192
C_KV_RANK = 128
D_CH = 256
CONV_W = 31
N_BRANCH = 4
MIX_W = 256
D_FF = 2816
N_EXPERTS = 8
TOP_K = 2
D_FF_EXPERT = 3584
ALPHA = (2 * DEPTH) ** 0.25

LANES = 128
SUBLANES = 8
MXU_TILE = 256
VMEM_LIMIT = 48 * 1024 * 1024

LOG2E = 1.4426950408889634
NEG_BIG = -1e30

COL_GATES = 0
COL_AQ = 4096
COL_AKV = 4352
COL_C = 4608
COL_D = 5120
COL_BQ = 5632
COL_BK = 5888
COL_BV = 6144
D_IN_PAD = 6400

NA_QROWS = 4
NA_KROWS = 12

MOE_TILE = 2048
MOE_SUB = 512
MOE_FCHUNK = 512
MOE_VMEM_LIMIT = 56 * 1024 * 1024


def _in_proj_columns():
    o_aq, o_ak, o_av = 0, 256, 384
    o_bq, o_bk, o_bv = 512, 768, 1024
    o_cq, o_ckv, o_kpe = 1280, 1472, 1600
    o_d, o_g = 1632, 2144
    deint = np.concatenate([np.arange(0, HEAD_DIM, 2), np.arange(1, HEAD_DIM, 2)])
    cols = [o_g + np.arange(N_BRANCH * D_MODEL)]
    half = HEAD_DIM // 2
    for g in range(2):
        for part in range(2):
            for kvh in range(A_KV_HEADS):
                cols.append(o_aq + (kvh * 2 + g) * HEAD_DIM + deint[part * half:(part + 1) * half])
    for part in range(2):
        for kvh in range(A_KV_HEADS):
            cols.append(o_ak + kvh * HEAD_DIM + deint[part * half:(part + 1) * half])
    cols.append(o_av + np.arange(A_KV_HEADS * HEAD_DIM))
    cols.append(o_cq + np.arange(C_Q_RANK))
    cols.append(np.full(64, -1))
    cols.append(o_ckv + np.arange(C_KV_RANK))
    cols.append(np.full(32, -1))
    cols.append(o_kpe + np.arange(0, C_ROPE, 2))
    cols.append(np.full(48, -1))
    cols.append(o_kpe + np.arange(1, C_ROPE, 2))
    cols.append(np.full(16, -1))
    cols.append(o_d + np.arange(2 * D_CH))
    cols.append(o_bq + np.arange(3 * B_HEADS * HEAD_DIM))
    cols = np.concatenate(cols)
    assert cols.shape[0] == D_IN_PAD
    return cols


def _take_cols(w, cols):
    cols = np.asarray(cols)
    runs, i, n = [], 0, len(cols)
    while i < n:
        j = i + 1
        if cols[i] < 0:
            while j < n and cols[j] < 0:
                j += 1
            kind = "zero"
        else:
            while j < n and cols[j] == cols[j - 1] + 1:
                j += 1
            kind = "slice" if j - i >= 64 else "shuffle"
        if kind == "shuffle" and runs and runs[-1][0] == "shuffle":
            runs[-1] = ("shuffle", runs[-1][1], j)
        else:
            runs.append((kind, i, j))
        i = j
    parts = []
    for kind, i, j in runs:
        if kind == "zero":
            parts.append(jnp.zeros((w.shape[0], j - i), w.dtype))
        elif kind == "slice":
            parts.append(w[:, int(cols[i]):int(cols[i]) + j - i])
        else:
            lo, hi = int(cols[i:j].min()), int(cols[i:j].max()) + 1
            onehot = np.zeros((hi - lo, j - i), np.float32)
            onehot[cols[i:j] - lo, np.arange(j - i)] = 1.0
            parts.append(jnp.dot(w[:, lo:hi], jnp.asarray(onehot, w.dtype), precision=lax.Precision.HIGHEST))
    return jnp.concatenate(parts, axis=1)


def _rope_tables(seq):
    t = np.arange(seq)
    row = (t // GRID_W).astype(np.float32)
    col = (t % GRID_W).astype(np.float32)

    def angles(rot_dim):
        n_freq = rot_dim // 4
        inv = jnp.asarray(ROPE_THETA, jnp.float32) ** (-jnp.arange(n_freq, dtype=jnp.float32) / n_freq)
        return jnp.concatenate([jnp.asarray(row)[:, None] * inv, jnp.asarray(col)[:, None] * inv], axis=-1)

    ang_a = angles(HEAD_DIM)
    ca, sa = jnp.cos(ang_a), jnp.sin(ang_a)
    cos_a = jnp.concatenate([ca, ca, ca, ca], axis=-1)
    sin_a = jnp.concatenate([-sa, -sa, sa, sa], axis=-1)
    ang_c = angles(C_ROPE)
    cc, sc = jnp.cos(ang_c), jnp.sin(ang_c)
    one = jnp.ones((seq, 64), jnp.float32)
    zero = jnp.zeros((seq, 64), jnp.float32)
    cos_c = jnp.concatenate([one[:, :32], cc, one[:, :48], cc, one[:, :16]], axis=-1)
    sin_c = jnp.concatenate([zero[:, :32], -sc, zero[:, :48], sc, zero[:, :16]], axis=-1)
    return cos_a, sin_a, cos_c, sin_c


def _na_bias_tables(rpb, rows):
    wh = min(NA_WIN_H, rows)
    n_dr, n_dc = 2 * NA_WIN_H - 1, 2 * NA_WIN_W - 1
    qc = np.arange(GRID_W)[:, None]
    kc = np.arange(GRID_W)[None, :]
    col_start = np.clip(qc - NA_WIN_W // 2, 0, GRID_W - NA_WIN_W)
    ok_c = (kc >= col_start) & (kc < col_start + NA_WIN_W)
    dc = np.clip(kc - qc + (NA_WIN_W - 1), 0, n_dc - 1)
    oh_c = np.eye(n_dc, dtype=np.float32)[dc.reshape(-1)]
    oh_r, ok_r = [], []
    for r0 in (0, NA_QROWS, rows - NA_QROWS):
        ws = int(np.clip(r0 - 4, 0, rows - NA_KROWS))
        qr = r0 + np.arange(NA_QROWS)[:, None]
        kr = ws + np.arange(NA_KROWS)[None, :]
        row_start = np.clip(qr - wh // 2, 0, rows - wh)
        ok_r.append((kr >= row_start) & (kr < row_start + wh))
        dr = np.clip(kr - qr + (NA_WIN_H - 1), 0, n_dr - 1)
        oh_r.append(np.eye(n_dr, dtype=np.float32)[dr.reshape(-1)])
    oh_r = np.concatenate(oh_r, axis=0)
    ok = np.stack(ok_r)[:, :, None, :, None] & ok_c[None, None, :, None, :]
    ok = ok.reshape(3, 1, NA_QROWS * GRID_W, NA_KROWS * GRID_W)
    hi = lax.Precision.HIGHEST
    t1 = jnp.einsum("ma,lhab->lhmb", jnp.asarray(oh_r), rpb.astype(jnp.float32) * LOG2E, precision=hi)
    t2 = jnp.einsum("lhmb,nb->lhmn", t1, jnp.asarray(oh_c), precision=hi).astype(jnp.bfloat16)
    nl, nh = rpb.shape[0], rpb.shape[1]
    t2 = t2.reshape(nl, nh, 3, NA_QROWS, NA_KROWS, GRID_W, GRID_W).transpose(0, 2, 1, 3, 5, 4, 6)
    bias = t2.reshape(nl, 3, nh, NA_QROWS * GRID_W, NA_KROWS * GRID_W)
    return jnp.where(jnp.asarray(ok), bias, jnp.asarray(NEG_BIG, jnp.bfloat16))


def _cparams(sem):
    return pltpu.CompilerParams(dimension_semantics=sem, vmem_limit_bytes=VMEM_LIMIT)


def _layer_norm_rows(z, g, b):
    mu = jnp.mean(z, axis=-1, keepdims=True)
    zc = z - mu
    var = jnp.mean(zc * zc, axis=-1, keepdims=True)
    return zc * lax.rsqrt(var + LN_EPS) * g + b


def _sigmoid(x):
    return 0.5 * jnp.tanh(0.5 * x) + 0.5


def _dot(a, b):
    return jnp.dot(a, b, preferred_element_type=jnp.float32)


def _dot_nt(a, b):
    return lax.dot_general(a, b, (((1,), (1,)), ((), ())), preferred_element_type=jnp.float32)


IN_PROJ_CHUNK = 1280


def _project_chunks(xb, w_refs, h_ref):
    off = 0
    for w_ref in w_refs:
        n_tiles = w_ref.shape[1] // MXU_TILE
        n_chunks = -(-w_ref.shape[1] // IN_PROJ_CHUNK)
        lo = 0
        for c in range(n_chunks):
            hi = (n_tiles * (c + 1) // n_chunks) * MXU_TILE
            h_ref[:, off + lo:off + hi] = _dot(xb, w_ref[:, lo:hi]).astype(h_ref.dtype)
            lo = hi
        off += w_ref.shape[1]


def _in_proj_kernel(x_ref, wg_ref, wr_ref, h_ref):
    _project_chunks(x_ref[...], (wg_ref, wr_ref), h_ref)


def _in_proj_ln_kernel(x_ref, g_ref, b_ref, wg_ref, wr_ref, h_ref, xf_ref):
    y = _layer_norm_rows(x_ref[...], g_ref[...], b_ref[...])
    xf_ref[...] = y
    _project_chunks(y.astype(jnp.bfloat16), (wg_ref, wr_ref), h_ref)


def _in_proj(x, w_gate, w_rest, ln=None, tm=512):
    t, d = x.shape
    n = w_gate.shape[1] + w_rest.shape[1]
    row = lambda width: pl.BlockSpec((tm, width), lambda i: (i, 0))
    vec = pl.BlockSpec((1, d), lambda i: (0, 0))
    resident = lambda w: pl.BlockSpec(w.shape, lambda i: (0, 0), pipeline_mode=pl.Buffered(1))
    h_shape = jax.ShapeDtypeStruct((t, n), jnp.bfloat16)
    if ln is None:
        return pl.pallas_call(
            _in_proj_kernel, out_shape=h_shape, grid=(t // tm,),
            in_specs=[row(d), resident(w_gate), resident(w_rest)], out_specs=row(n),
            compiler_params=_cparams(("parallel",)),
        )(x, w_gate, w_rest)
    g, b = ln
    return pl.pallas_call(
        _in_proj_ln_kernel,
        out_shape=(h_shape, jax.ShapeDtypeStruct((t, d), jnp.float32)),
        grid=(t // tm,),
        in_specs=[row(d), vec, vec, resident(w_gate), resident(w_rest)], out_specs=(row(n), row(d)),
        compiler_params=_cparams(("parallel",)),
    )(x, g.reshape(1, d), b.reshape(1, d), w_gate, w_rest)


def _ones_halves(vblk):
    lane = lax.broadcasted_iota(jnp.int32, vblk.shape, 1)
    ones = jnp.ones_like(vblk)
    return jnp.where(lane < HEAD_DIM, vblk, ones), jnp.where(lane < HEAD_DIM, ones, vblk)


def _attend_pair(s_lo, s_hi, v_lo, v_hi, lo_q):
    outs = []
    for s, v in ((s_lo, v_lo), (s_hi, v_hi)):
        p = jnp.exp2(s - jnp.max(s, axis=-1, keepdims=True)).astype(jnp.bfloat16)
        outs.append(_dot(p, v))
    num = jnp.where(lo_q, outs[0], outs[1])
    den = jnp.where(lo_q, pltpu.roll(outs[0], HEAD_DIM, 1), pltpu.roll(outs[1], HEAD_DIM, 1))
    return num / den


def _attn_a_kernel(q_ref, kv_ref, cos_ref, sin_ref, gq_ref, gk_ref, o_ref, ks_ref, va_ref, *, tq):
    s_len = q_ref.shape[0]

    def first_head(shape):
        lane = lax.broadcasted_iota(jnp.int32, shape, 1)
        return (lane % HEAD_DIM) < HEAD_DIM // 2

    def prep(x, g, cos, sin):
        head_a = first_head(x.shape)
        x2 = x * x
        ss_a = jnp.sum(jnp.where(head_a, x2, 0.0), axis=-1, keepdims=True)
        ss_b = jnp.sum(jnp.where(head_a, 0.0, x2), axis=-1, keepdims=True)
        ms = jnp.where(head_a, ss_a, ss_b) * (1.0 / HEAD_DIM)
        xn = x * lax.rsqrt(ms + RMS_EPS) * g
        return xn * cos + pltpu.roll(xn, LANES // 2, 1) * sin

    ks_ref[...] = prep(kv_ref[:, 0:LANES].astype(jnp.float32), gk_ref[...],
                       cos_ref[...], sin_ref[...]).astype(jnp.bfloat16)
    scale = HEAD_DIM ** -0.5 * LOG2E
    lane_q = lax.broadcasted_iota(jnp.int32, (tq, LANES), 1)
    lo_q = lane_q < HEAD_DIM
    head_a_q = first_head((tq, LANES))

    va_ref[0], va_ref[1] = _ones_halves(kv_ref[:, LANES:2 * LANES])

    def body(i, carry):
        r = pl.multiple_of(i * tq, tq)
        k = ks_ref[...]
        cos = cos_ref[pl.ds(r, tq), :]
        sin = sin_ref[pl.ds(r, tq), :]
        for g in range(2):
            qt = prep(q_ref[pl.ds(r, tq), g * LANES:(g + 1) * LANES].astype(jnp.float32), gq_ref[...], cos, sin)
            qt = (qt * scale).astype(jnp.bfloat16)
            zero = jnp.zeros_like(qt)
            out = _attend_pair(_dot_nt(jnp.where(head_a_q, qt, zero), k), _dot_nt(jnp.where(head_a_q, zero, qt), k),
                               va_ref[0], va_ref[1], lo_q)
            o_ref[pl.ds(r, tq), g * LANES:(g + 1) * LANES] = out.astype(o_ref.dtype)
        return carry

    lax.fori_loop(0, s_len // tq, body, 0, unroll=4)


def _attn_a(h, cos_a, sin_a, gq, gk, batch, seq, tq=256):
    kern = functools.partial(_attn_a_kernel, tq=tq)
    return pl.pallas_call(
        kern,
        out_shape=jax.ShapeDtypeStruct((batch * seq, MIX_W), jnp.bfloat16),
        grid=(batch,),
        in_specs=[pl.BlockSpec((seq, 256), lambda b: (b, COL_AQ // 256)),
                  pl.BlockSpec((seq, 256), lambda b: (b, COL_AKV // 256)),
                  pl.BlockSpec((seq, LANES), lambda b: (0, 0)),
                  pl.BlockSpec((seq, LANES), lambda b: (0, 0)),
                  pl.BlockSpec((1, LANES), lambda b: (0, 0)),
                  pl.BlockSpec((1, LANES), lambda b: (0, 0))],
        out_specs=pl.BlockSpec((seq, MIX_W), lambda b: (b, 0)),
        scratch_shapes=[pltpu.VMEM((seq, LANES), jnp.bfloat16), pltpu.VMEM((2, seq, LANES), jnp.bfloat16)],
        compiler_params=_cparams(("parallel",)),
    )(h, h, cos_a, sin_a, gq, gk)


def _attn_b_kernel(q_ref, k_ref, v_ref, bias_ref, o_ref, va_ref, *, rows):
    nq = NA_QROWS * GRID_W
    nk = NA_KROWS * GRID_W
    lane_q = lax.broadcasted_iota(jnp.int32, (nq, LANES), 1)
    lo_q = lane_q < HEAD_DIM
    n_groups = rows // NA_QROWS
    scale = HEAD_DIM ** -0.5 * LOG2E
    for blk in range(B_HEADS // 2):
        va_ref[2 * blk], va_ref[2 * blk + 1] = _ones_halves(v_ref[:, blk * LANES:(blk + 1) * LANES])

    def body(gi, carry):
        ws = jnp.clip(gi * NA_QROWS - 4, 0, rows - NA_KROWS)
        cls = jnp.where(gi == 0, 0, jnp.where(gi == n_groups - 1, 2, 1))
        qr = pl.multiple_of(gi * nq, nq)
        kr = pl.multiple_of(ws * GRID_W, GRID_W)
        for blk in range(B_HEADS // 2):
            sl = slice(blk * LANES, (blk + 1) * LANES)
            qt = q_ref[pl.ds(qr, nq), sl] * scale
            kw = k_ref[pl.ds(kr, nk), sl]
            zero = jnp.zeros_like(qt)
            s_lo = _dot_nt(jnp.where(lo_q, qt, zero), kw) + bias_ref[cls, 2 * blk].astype(jnp.float32)
            s_hi = _dot_nt(jnp.where(lo_q, zero, qt), kw) + bias_ref[cls, 2 * blk + 1].astype(jnp.float32)
            out = _attend_pair(s_lo, s_hi, va_ref[2 * blk, pl.ds(kr, nk), :], va_ref[2 * blk + 1, pl.ds(kr, nk), :], lo_q)
            o_ref[pl.ds(qr, nq), sl] = out.astype(o_ref.dtype)
        return carry

    lax.fori_loop(0, n_groups, body, 0, unroll=True)


def _attn_b(h, bias, layer, batch, seq):
    rows = seq // GRID_W
    kern = functools.partial(_attn_b_kernel, rows=rows)
    return pl.pallas_call(
        kern,
        out_shape=jax.ShapeDtypeStruct((batch * seq, MIX_W), jnp.bfloat16),
        grid=(batch,),
        in_specs=[pl.BlockSpec((seq, 256), lambda b: (b, COL_BQ // 256)),
                  pl.BlockSpec((seq, 256), lambda b: (b, COL_BK // 256)),
                  pl.BlockSpec((seq, 256), lambda b: (b, COL_BV // 256)),
                  pl.BlockSpec((None,) + bias.shape[1:], lambda b: (layer, 0, 0, 0, 0))],
        out_specs=pl.BlockSpec((seq, MIX_W), lambda b: (b, 0)),
        scratch_shapes=[pltpu.VMEM((B_HEADS, seq, LANES), jnp.bfloat16)],
        compiler_params=_cparams(("parallel",)),
    )(h, h, h, bias)


def _attn_c_kernel(hc_ref, wuq_ref, wuk_ref, wuv_ref, gq_ref, gkv_ref, cos_ref, sin_ref, o_ref,
                   qs_ref, ks_ref, vs_ref, *, tq):
    s_len = hc_ref.shape[0]
    cos = cos_ref[...]
    sin = sin_ref[...]

    def rotate(x):
        return x * cos + pltpu.roll(x, LANES // 2, 1) * sin

    cq = hc_ref[:, 0:256].astype(jnp.float32)
    ms = jnp.sum(cq * cq, axis=-1, keepdims=True) * (1.0 / C_Q_RANK)
    cqn = (cq * lax.rsqrt(ms + RMS_EPS) * gq_ref[...]).astype(jnp.bfloat16)
    ckv = hc_ref[:, 256:384].astype(jnp.float32)
    ms = jnp.mean(ckv * ckv, axis=-1, keepdims=True)
    kvn = (ckv * lax.rsqrt(ms + RMS_EPS) * gkv_ref[...]).astype(jnp.bfloat16)
    kpe = rotate(hc_ref[:, 384:512].astype(jnp.float32))
    v_all = _dot(kvn, wuv_ref[...]).astype(jnp.bfloat16)
    for blk in range(C_HEADS // 2):
        vs_ref[2 * blk], vs_ref[2 * blk + 1] = _ones_halves(v_all[:, blk * LANES:(blk + 1) * LANES])
    scale = (C_NOPE + C_ROPE) ** -0.5 * LOG2E
    for h in range(C_HEADS):
        sl = slice(h * LANES, (h + 1) * LANES)
        qh = rotate(_dot(cqn, wuq_ref[:, sl])) * scale
        qs_ref[:, sl] = qh.astype(jnp.bfloat16)
        ks_ref[:, sl] = (_dot(kvn, wuk_ref[:, sl]) + kpe).astype(jnp.bfloat16)

    lane_q = lax.broadcasted_iota(jnp.int32, (tq, LANES), 1)
    lo_q = lane_q < C_V

    def body(i, carry):
        r = pl.multiple_of(i * tq, tq)
        for blk in range(C_HEADS // 2):
            s = []
            for par in range(2):
                sl = slice((2 * blk + par) * LANES, (2 * blk + par + 1) * LANES)
                s.append(_dot_nt(qs_ref[pl.ds(r, tq), sl], ks_ref[:, sl]))
            out = _attend_pair(s[0], s[1], vs_ref[2 * blk], vs_ref[2 * blk + 1], lo_q)
            o_ref[pl.ds(r, tq), blk * LANES:(blk + 1) * LANES] = out.astype(o_ref.dtype)
        return carry

    lax.fori_loop(0, s_len // tq, body, 0, unroll=4)


def _attn_c(h, wuq, wuk, wuv, gq, gkv, cos_c, sin_c, batch, seq, tq=256):
    kern = functools.partial(_attn_c_kernel, tq=tq)
    const = lambda shape: pl.BlockSpec(shape, lambda b: (0,) * len(shape))
    return pl.pallas_call(
        kern,
        out_shape=jax.ShapeDtypeStruct((batch * seq, MIX_W), jnp.bfloat16),
        grid=(batch,),
        in_specs=[pl.BlockSpec((seq, 512), lambda b: (b, COL_C // 512)),
                  const(wuq.shape), const(wuk.shape), const(wuv.shape),
                  const(gq.shape), const(gkv.shape), const(cos_c.shape), const(sin_c.shape)],
        out_specs=pl.BlockSpec((seq, MIX_W), lambda b: (b, 0)),
        scratch_shapes=[pltpu.VMEM((seq, C_HEADS * LANES), jnp.bfloat16),
                        pltpu.VMEM((seq, C_HEADS * LANES), jnp.bfloat16),
                        pltpu.VMEM((C_HEADS, seq, LANES), jnp.bfloat16)],
        compiler_params=_cparams(("parallel",)),
    )(h, wuq, wuk, wuv, gq, gkv, cos_c, sin_c)


CONV_PAD = 16
CONV_ROWS = 128


def _conv_kernel(hd_ref, w_ref, cb_ref, g_ref, b_ref, o_ref, up_ref):
    s_len = hd_ref.shape[0]
    a = hd_ref[:, 0:D_CH].astype(jnp.float32)
    gate = hd_ref[:, D_CH:2 * D_CH].astype(jnp.float32)
    zeros = jnp.zeros((CONV_PAD, D_CH), jnp.float32)
    up_ref[0, 0:CONV_PAD, :] = zeros
    up_ref[0, CONV_PAD + s_len:2 * CONV_PAD + s_len, :] = zeros
    up_ref[0, CONV_PAD:CONV_PAD + s_len, :] = a * _sigmoid(gate)
    n_shifted = s_len + 2 * CONV_PAD - SUBLANES
    for s in range(1, SUBLANES):
        up_ref[s, 0:n_shifted, :] = up_ref[0, s:s + n_shifted, :]
    shift = CONV_PAD - CONV_W // 2
    for c in range(s_len // CONV_ROWS):
        acc = jnp.zeros((CONV_ROWS, D_CH), jnp.float32)
        for j in range(CONV_W):
            s = (j + shift) % SUBLANES
            start = c * CONV_ROWS + j + shift - s
            acc = acc + up_ref[s, start:start + CONV_ROWS, :] * w_ref[j:j + 1, :]
        u = acc + cb_ref[...]
        y = _layer_norm_rows(u, g_ref[...], b_ref[...])
        o_ref[c * CONV_ROWS:(c + 1) * CONV_ROWS, :] = (y * _sigmoid(y)).astype(o_ref.dtype)


def _conv_branch(h, w, cb, g, b, batch, seq):
    const = lambda shape: pl.BlockSpec(shape, lambda i: (0,) * len(shape))
    return pl.pallas_call(
        _conv_kernel,
        out_shape=jax.ShapeDtypeStruct((batch * seq, MIX_W), jnp.bfloat16),
        grid=(batch,),
        in_specs=[pl.BlockSpec((seq, 512), lambda i: (i, COL_D // 512)),
                  const(w.shape), const(cb.shape), const(g.shape), const(b.shape)],
        out_specs=pl.BlockSpec((seq, MIX_W), lambda i: (i, 0)),
        scratch_shapes=[pltpu.VMEM((SUBLANES, seq + 2 * CONV_PAD, D_CH), jnp.float32)],
        compiler_params=_cparams(("parallel",)),
    )(h, w, cb, g, b)


def _merge_kernel(gl_ref, ya_ref, yb_ref, yc_ref, yd_ref, x_ref, wb_ref, wo_ref, g_ref, b_ref,
                  of_ref, ob_ref):
    merged = None
    for n, y_ref in enumerate((ya_ref, yb_ref, yc_ref, yd_ref)):
        t = jnp.tanh(gl_ref[:, n * D_MODEL:(n + 1) * D_MODEL].astype(jnp.float32))
        term = (t + 1.0) * _dot(y_ref[...], wb_ref[n])
        merged = term if merged is None else merged + term
    mix = _dot(merged.astype(jnp.bfloat16), wo_ref[...])
    y = _layer_norm_rows(ALPHA * x_ref[...] + mix, g_ref[...], b_ref[...])
    of_ref[...] = y
    ob_ref[...] = y.astype(jnp.bfloat16)


def _merge(h, ya, yb, yc, yd, x, wb, wo, g, b, tm=512):
    t, d = x.shape
    row = lambda w: pl.BlockSpec((tm, w), lambda i: (i, 0))
    const = lambda shape: pl.BlockSpec(shape, lambda i: (0,) * len(shape))
    return pl.pallas_call(
        _merge_kernel,
        out_shape=(jax.ShapeDtypeStruct((t, d), jnp.float32), jax.ShapeDtypeStruct((t, d), jnp.bfloat16)),
        grid=(t // tm,),
        in_specs=[pl.BlockSpec((tm, N_BRANCH * d), lambda i: (i, COL_GATES)),
                  row(MIX_W), row(MIX_W), row(MIX_W), row(MIX_W), row(d),
                  const(wb.shape), const(wo.shape), const((1, d)), const((1, d))],
        out_specs=(row(d), row(d)),
        compiler_params=_cparams(("parallel",)),
    )(h, ya, yb, yc, yd, x, wb, wo, g.reshape(1, d), b.reshape(1, d))


def _ffn_kernel(xb_ref, xf_ref, w1_ref, w3_ref, w2_ref, g_ref, b_ref, of_ref, ob_ref):
    x = xb_ref[...]
    f = w2_ref.shape[0]
    split = (f // MXU_TILE + 1) // 2 * MXU_TILE
    acc = None
    for lo, hi in ((0, split), (split, f)):
        a = _dot(x, w1_ref[:, lo:hi])
        hid = (a * _sigmoid(a) * _dot(x, w3_ref[:, lo:hi])).astype(jnp.bfloat16)
        part = _dot(hid, w2_ref[lo:hi, :])
        acc = part if acc is None else acc + part
    y = _layer_norm_rows(ALPHA * xf_ref[...] + acc, g_ref[...], b_ref[...])
    of_ref[...] = y
    ob_ref[...] = y.astype(jnp.bfloat16)


def _ffn(xb, xf, w1, w3, w2, g, b, tm=512):
    t, d = xf.shape
    row = pl.BlockSpec((tm, d), lambda i: (i, 0))
    vec = pl.BlockSpec((1, d), lambda i: (0, 0))
    resident = lambda shape: pl.BlockSpec(shape, lambda i: (0, 0), pipeline_mode=pl.Buffered(1))
    return pl.pallas_call(
        _ffn_kernel,
        out_shape=(jax.ShapeDtypeStruct((t, d), jnp.float32), jax.ShapeDtypeStruct((t, d), jnp.bfloat16)),
        grid=(t // tm,),
        in_specs=[row, row, resident(w1.shape), resident(w3.shape), resident(w2.shape), vec, vec],
        out_specs=(row, row),
        compiler_params=_cparams(("parallel",)),
    )(xb, xf, w1, w3, w2, g.reshape(1, d), b.reshape(1, d))


def _pack_rows(x):
    half = x.shape[1] // 2
    bits = lax.bitcast_convert_type(x.astype(jnp.bfloat16).astype(jnp.float32), jnp.uint32)
    return (bits[:, :half] >> 16) | (bits[:, half:] & jnp.uint32(0xFFFF0000))


def _unpack_rows(p):
    lo = lax.bitcast_convert_type(p << 16, jnp.float32)
    hi = lax.bitcast_convert_type(p & jnp.uint32(0xFFFF0000), jnp.float32)
    return jnp.concatenate([lo, hi], axis=1)


def _router_kernel(x_ref, r_ref, tri_ref, oi_ref, ow_ref, cnt_ref, xp_ref, carry_ref):
    i = pl.program_id(0)

    @pl.when(i == 0)
    def _():
        carry_ref[...] = jnp.zeros_like(carry_ref)

    x = x_ref[...]
    r = r_ref[...]
    xh = x.astype(jnp.bfloat16)
    xl = (x - xh.astype(jnp.float32)).astype(jnp.bfloat16)
    rh = r.astype(jnp.bfloat16)
    rl = (r - rh.astype(jnp.float32)).astype(jnp.bfloat16)
    r2 = jnp.concatenate([rh, rl], axis=1)
    p = _dot(xh, r2) + _dot(xl, r2)
    logits = p[:, :LANES] + p[:, LANES:]
    tm = x.shape[0]
    lane = lax.broadcasted_iota(jnp.int32, (tm, LANES), 1)
    logits = jnp.where(lane < N_EXPERTS, logits, NEG_BIG)
    m1 = jnp.max(logits, axis=-1, keepdims=True)
    i1 = jnp.min(jnp.where(logits == m1, lane, LANES), axis=-1, keepdims=True)
    rest = jnp.where(lane == i1, NEG_BIG, logits)
    m2 = jnp.max(rest, axis=-1, keepdims=True)
    i2 = jnp.min(jnp.where(rest == m2, lane, LANES), axis=-1, keepdims=True)
    e = jnp.exp(m2 - m1)
    w1 = 1.0 / (1.0 + e)
    w2 = e / (1.0 + e)
    sel1 = lane == i1
    sel2 = lane == i2
    member = jnp.where(sel1 | sel2, 1.0, 0.0)
    before = _dot(tri_ref[...], member.astype(jnp.bfloat16)) + carry_ref[...]
    rank1 = jnp.sum(jnp.where(sel1, before, 0.0), axis=-1, keepdims=True).astype(jnp.int32)
    rank2 = jnp.sum(jnp.where(sel2, before, 0.0), axis=-1, keepdims=True).astype(jnp.int32)
    carry_ref[...] += jnp.sum(member, axis=0, keepdims=True)
    oi = jnp.where(lane == 0, i1, jnp.where(lane == 1, i2, jnp.where(lane == 2, rank1, rank2)))
    oi_ref[...] = oi.T[0:SUBLANES, :]
    ow_ref[...] = jnp.where(lane == 0, w1, w2)
    cnt_ref[...] = carry_ref[...]
    xp_ref[...] = _pack_rows(x)


def _router(xf, router, tm=512):
    t, d = xf.shape
    r_pad = jnp.zeros((d, LANES), jnp.float32).at[:, :N_EXPERTS].set(router.astype(jnp.float32))
    tri = jnp.asarray(np.tril(np.ones((tm, tm), np.float32), -1), jnp.bfloat16)
    return pl.pallas_call(
        _router_kernel,
        out_shape=(jax.ShapeDtypeStruct((SUBLANES, t), jnp.int32),
                   jax.ShapeDtypeStruct((t, LANES), jnp.float32),
                   jax.ShapeDtypeStruct((1, LANES), jnp.float32),
                   jax.ShapeDtypeStruct((t, d // 2), jnp.uint32)),
        grid=(t // tm,),
        in_specs=[pl.BlockSpec((tm, d), lambda i: (i, 0)),
                  pl.BlockSpec((d, LANES), lambda i: (0, 0)),
                  pl.BlockSpec((tm, tm), lambda i: (0, 0))],
        out_specs=(pl.BlockSpec((SUBLANES, tm), lambda i: (0, i)),
                   pl.BlockSpec((tm, LANES), lambda i: (i, 0)),
                   pl.BlockSpec((1, LANES), lambda i: (0, 0)),
                   pl.BlockSpec((tm, d // 2), lambda i: (i, 0))),
        scratch_shapes=[pltpu.VMEM((1, LANES), jnp.float32)],
        compiler_params=_cparams(("arbitrary",)),
    )(xf, r_pad, tri)


SC_CORES = 2
SC_SUBCORES = 16
SC_WINDOW = 128


def _sc_gather_rows(table, idx):
    m = idx.shape[0]
    d = table.shape[1]
    n_workers = SC_CORES * SC_SUBCORES
    per_w = m // n_workers
    n_win = per_w // SC_WINDOW
    assert per_w * n_workers == m and n_win * SC_WINDOW == per_w
    mesh = plsc.VectorSubcoreMesh(core_axis_name="c", subcore_axis_name="s")

    @functools.partial(
        pl.kernel, mesh=mesh,
        out_type=jax.ShapeDtypeStruct((m, d), table.dtype),
        scratch_types=[pltpu.VMEM((per_w,), jnp.int32),
                       pltpu.VMEM((SC_WINDOW, d), table.dtype),
                       pltpu.SemaphoreType.DMA],
    )
    def gather(table_hbm, idx_hbm, out_hbm, idx_v, rows_v, sem):
        wid = lax.axis_index("s") * SC_CORES + lax.axis_index("c")
        base = wid * per_w
        pltpu.sync_copy(idx_hbm.at[pl.ds(base, per_w)], idx_v)

        @pl.loop(0, n_win)
        def _(w):
            off = pl.multiple_of(w * SC_WINDOW, SC_WINDOW)
            pltpu.async_copy(table_hbm.at[idx_v.at[pl.ds(off, SC_WINDOW)]], rows_v, sem).wait()
            pltpu.sync_copy(rows_v, out_hbm.at[pl.ds(base + off, SC_WINDOW)])

    return gather(table, idx)


def _sc_scatter_rows(rows, idx, n_out):
    n_copies, m = idx.shape
    d = rows.shape[1]
    n_workers = SC_CORES * SC_SUBCORES
    per_w = m // n_workers
    n_win = per_w // SC_WINDOW
    assert per_w * n_workers == m and n_win * SC_WINDOW == per_w
    idx_w = idx.reshape(n_copies, n_workers, n_win, SC_WINDOW).transpose(1, 0, 2, 3)
    idx_w = idx_w.reshape(n_workers, n_copies * n_win, SC_WINDOW)
    mesh = plsc.VectorSubcoreMesh(core_axis_name="c", subcore_axis_name="s")

    @functools.partial(
        pl.kernel, mesh=mesh,
        out_type=jax.ShapeDtypeStruct((n_out, d), rows.dtype),
        scratch_types=[pltpu.VMEM((n_copies * n_win, SC_WINDOW), jnp.int32),
                       pltpu.VMEM((SC_WINDOW, d), rows.dtype)],
    )
    def scatter(rows_hbm, idx_hbm, out_hbm, idx_v, rows_v):
        wid = lax.axis_index("s") * SC_CORES + lax.axis_index("c")
        base = wid * per_w
        pltpu.sync_copy(idx_hbm.at[wid], idx_v)

        @pl.loop(0, n_win)
        def _(w):
            off = pl.multiple_of(w * SC_WINDOW, SC_WINDOW)
            pltpu.sync_copy(rows_hbm.at[pl.ds(base + off, SC_WINDOW)], rows_v)
            for k in range(n_copies):
                pltpu.sync_copy(rows_v, out_hbm.at[idx_v.at[k * n_win + w]])

    return scatter(rows, idx_w)


def _expert_kernel(te_ref, nv_ref, x_ref, w1_ref, w3_ref, w2_ref, o_ref,
                   w13s_ref, w2s_ref, xb_ref, acc_ref):
    i = pl.program_id(0)
    j = pl.program_id(1)
    last = pl.num_programs(1) - 1
    n_valid = nv_ref[i]
    n_piece = MOE_TILE // MOE_SUB
    piece_rows = [pl.ds(s * MOE_SUB, MOE_SUB) for s in range(n_piece)]

    for s, rows in enumerate(piece_rows):
        @pl.when((n_valid > s * MOE_SUB) & (j == 0))
        def _():
            row = lax.broadcasted_iota(jnp.int32, (MOE_SUB, 1), 0) + s * MOE_SUB
            x = jnp.where(row < n_valid, _unpack_rows(x_ref[rows, :]), 0.0)
            xb_ref[rows, :] = x.astype(jnp.bfloat16)
            acc_ref[rows, :] = jnp.zeros((MOE_SUB, acc_ref.shape[1]), jnp.float32)

    def compute(n_live):
        w13s_ref[:, 0:MOE_FCHUNK] = w1_ref[0].astype(jnp.bfloat16)
        w13s_ref[:, MOE_FCHUNK:2 * MOE_FCHUNK] = w3_ref[0].astype(jnp.bfloat16)
        w2s_ref[...] = w2_ref[0].astype(jnp.bfloat16)
        for rows in piece_rows[:n_live]:
            ab = _dot(xb_ref[rows, :], w13s_ref[...])
            a = ab[:, :MOE_FCHUNK]
            hid = (a * _sigmoid(a) * ab[:, MOE_FCHUNK:]).astype(jnp.bfloat16)
            acc_ref[rows, :] += _dot(hid, w2s_ref[...])

    for n_live in range(1, n_piece + 1):
        upper = n_live * MOE_SUB if n_live < n_piece else MOE_TILE
        pl.when((n_valid > (n_live - 1) * MOE_SUB) & (n_valid <= upper))(functools.partial(compute, n_live))

    for s, rows in enumerate(piece_rows):
        @pl.when((n_valid > s * MOE_SUB) & (j == last))
        def _():
            o_ref[rows, :] = _pack_rows(acc_ref[rows, :])

        @pl.when((n_valid <= s * MOE_SUB) & (j == last))
        def _():
            o_ref[rows, :] = jnp.zeros((MOE_SUB, o_ref.shape[1]), o_ref.dtype)


def _experts(tile_expert, tile_valid, xs, w1, w3, w2):
    n_rows = xs.shape[0]
    d, f = w1.shape[1], w1.shape[2]
    nj = f // MOE_FCHUNK

    def chunk(i, j, nv):
        return jnp.where(nv[i] > 0, j, nj - 1)

    return pl.pallas_call(
        _expert_kernel,
        out_shape=jax.ShapeDtypeStruct((n_rows, d // 2), jnp.uint32),
        grid_spec=pltpu.PrefetchScalarGridSpec(
            num_scalar_prefetch=2,
            grid=(n_rows // MOE_TILE, nj),
            in_specs=[pl.BlockSpec((MOE_TILE, d // 2), lambda i, j, te, nv: (i, 0)),
                      pl.BlockSpec((1, d, MOE_FCHUNK), lambda i, j, te, nv: (te[i], 0, chunk(i, j, nv))),
                      pl.BlockSpec((1, d, MOE_FCHUNK), lambda i, j, te, nv: (te[i], 0, chunk(i, j, nv))),
                      pl.BlockSpec((1, MOE_FCHUNK, d), lambda i, j, te, nv: (te[i], chunk(i, j, nv), 0))],
            out_specs=pl.BlockSpec((MOE_TILE, d // 2), lambda i, j, te, nv: (i, 0)),
            scratch_shapes=[pltpu.VMEM((d, 2 * MOE_FCHUNK), jnp.bfloat16),
                            pltpu.VMEM((MOE_FCHUNK, d), jnp.bfloat16),
                            pltpu.VMEM((MOE_TILE, d), jnp.bfloat16),
                            pltpu.VMEM((MOE_TILE, d), jnp.float32)],
        ),
        compiler_params=pltpu.CompilerParams(dimension_semantics=("arbitrary", "arbitrary"),
                                             vmem_limit_bytes=MOE_VMEM_LIMIT),
    )(tile_expert, tile_valid, xs, w1, w3, w2)


def _combine_kernel(ya_ref, yb_ref, x_ref, w_ref, g_ref, b_ref, *rest):
    o_ref = rest[-1]
    w = w_ref[...]
    f = w[:, 0:1] * _unpack_rows(ya_ref[...]) + w[:, 1:2] * _unpack_rows(yb_ref[...])
    o_ref[...] = _layer_norm_rows(ALPHA * x_ref[...] + f, g_ref[...], b_ref[...])


def _combine(yg, xf, wts, g, b, part, n_parts, prev=None, tm=512):
    t, d = xf.shape
    nt = t // n_parts // tm
    base = part * nt
    in_specs = [pl.BlockSpec((tm, d // 2), lambda i: (i, 0)),
                pl.BlockSpec((tm, d // 2), lambda i: (i + nt, 0)),
                pl.BlockSpec((tm, d), lambda i: (i + base, 0)),
                pl.BlockSpec((tm, LANES), lambda i: (i + base, 0)),
                pl.BlockSpec((1, d), lambda i: (0, 0)),
                pl.BlockSpec((1, d), lambda i: (0, 0))]
    args = [yg, yg, xf, wts, g.reshape(1, d), b.reshape(1, d)]
    aliases = {}
    if prev is not None:
        in_specs.append(pl.BlockSpec(memory_space=pl.ANY))
        args.append(prev)
        aliases = {len(args) - 1: 0}
    return pl.pallas_call(
        _combine_kernel,
        out_shape=jax.ShapeDtypeStruct((t, d), jnp.float32),
        grid=(nt,),
        in_specs=in_specs,
        out_specs=pl.BlockSpec((tm, d), lambda i: (i + base, 0)),
        input_output_aliases=aliases,
        compiler_params=_cparams(("parallel",)),
    )(*args)


def _moe_ffn(xf, router, w1, w3, w2, g, b):
    t, d = xf.shape
    oi, ow, cnt, xp = _router(xf, router)
    e1, e2, rank1, rank2 = oi[0], oi[1], oi[2], oi[3]
    counts = cnt[0, :N_EXPERTS].astype(jnp.int32)
    tiles = (counts + MOE_TILE - 1) // MOE_TILE
    tile_end = jnp.cumsum(tiles)
    tile_start = tile_end - tiles
    offs = tile_start * MOE_TILE
    dest = jnp.concatenate([offs[e1] + rank1, offs[e2] + rank2])
    n_rows = TOP_K * t + N_EXPERTS * MOE_TILE
    n_tiles = n_rows // MOE_TILE
    tile_id = jnp.arange(n_tiles, dtype=jnp.int32)
    tile_expert = jnp.sum(tile_id[:, None] >= tile_end[None, :], axis=1)
    used = tile_expert < N_EXPERTS
    last_expert = jnp.sum(tile_end[-1] - 1 >= tile_end)
    tile_expert = jnp.where(used, tile_expert, last_expert).astype(jnp.int32)
    tile_valid = jnp.clip(counts[tile_expert] - (tile_id - tile_start[tile_expert]) * MOE_TILE, 0, MOE_TILE)
    tile_valid = jnp.where(used, tile_valid, 0).astype(jnp.int32)
    xs = _sc_scatter_rows(xp, dest.reshape(TOP_K, t), n_rows)
    y = _experts(tile_expert, tile_valid, xs, w1, w3, w2)
    n_parts = 2
    tp = t // n_parts
    out = None
    for part in range(n_parts):
        idx = jnp.concatenate([dest[part * tp:(part + 1) * tp], dest[t + part * tp:t + (part + 1) * tp]])
        out = _combine(_sc_gather_rows(y, idx), xf, ow, g, b, part, n_parts, prev=out)
    return out


def _prep_layer(l, p, in_cols):
    bf = jnp.bfloat16
    out = {}
    n_gate = N_BRANCH * D_MODEL
    w_l = p["w_in_bf16"][l]
    out["w_gate"] = 0.5 * w_l[:, int(in_cols[0]):int(in_cols[0]) + n_gate]
    out["w_rest"] = _take_cols(w_l, in_cols[n_gate:])
    deint = np.concatenate([np.arange(0, HEAD_DIM, 2), np.arange(1, HEAD_DIM, 2)])
    lane_dim = np.concatenate([deint[:32], deint[:32], deint[32:], deint[32:]])
    out["gq_a"] = p["a_q_norm"][l][lane_dim].reshape(1, LANES)
    out["gk_a"] = p["a_k_norm"][l][lane_dim].reshape(1, LANES)
    per_head = C_NOPE + C_ROPE
    uq_cols = []
    for h in range(C_HEADS):
        base = h * per_head
        uq_cols += [base + np.arange(32), base + C_NOPE + np.arange(0, C_ROPE, 2), np.full(16, -1),
                    base + 32 + np.arange(32), base + C_NOPE + np.arange(1, C_ROPE, 2), np.full(16, -1)]
    wuq = _take_cols(p["c_w_uq"][l], np.concatenate(uq_cols))
    out["wuq"] = jnp.concatenate([wuq, jnp.zeros((64, wuq.shape[1]), wuq.dtype)], axis=0).astype(bf)
    uk_cols, uv_cols = [], []
    for h in range(C_HEADS):
        base = h * (C_NOPE + C_V)
        uk_cols += [base + np.arange(32), np.full(32, -1), base + 32 + np.arange(32), np.full(32, -1)]
        uv_cols += [base + C_NOPE + np.arange(C_V)]
    out["wuk"] = _take_cols(p["c_w_ukv"][l], np.concatenate(uk_cols)).astype(bf)
    out["wuv"] = _take_cols(p["c_w_ukv"][l], np.concatenate(uv_cols)).astype(bf)
    out["gq_c"] = jnp.concatenate([p["c_q_norm"][l], jnp.zeros((64,), jnp.float32)]).reshape(1, 256)
    out["gkv_c"] = p["c_kv_norm"][l].reshape(1, C_KV_RANK)
    out["conv_w"] = jnp.concatenate([p["d_conv_w"][l][:, 0, :], jnp.zeros((1, D_CH), jnp.float32)], axis=0)
    out["conv_b"] = p["d_conv_b"][l].reshape(1, D_CH)
    out["d_ln_g"] = p["d_ln_g"][l].reshape(1, D_CH)
    out["d_ln_b"] = p["d_ln_b"][l].reshape(1, D_CH)
    a_rows = np.concatenate([(kvh * 2 + g) * HEAD_DIM + np.arange(HEAD_DIM)
                             for g in range(2) for kvh in range(A_KV_HEADS)])
    wb = p["w_branch"][l]
    out["w_branch"] = (0.5 * jnp.stack([wb[0][a_rows], wb[1], wb[2], wb[3]], axis=0)).astype(bf)
    out["w_out"] = p["w_out"][l].astype(bf)
    return out


def kernel(x, ln_in_g, ln_in_b, w_in, a_q_norm, a_k_norm, b_rpb, c_q_norm, c_kv_norm, c_w_uq, c_w_ukv,
           d_conv_w, d_conv_b, d_ln_g, d_ln_b, w_branch, w_out, ln_mix_g, ln_mix_b,
           ffn_w1, ffn_w3, ffn_w2, moe_router, moe_w1, moe_w3, moe_w2, ln_ffn_g, ln_ffn_b):
    batch, seq, d = x.shape
    t = batch * seq
    bf = jnp.bfloat16
    params = dict(w_in_bf16=w_in.astype(bf), a_q_norm=a_q_norm, a_k_norm=a_k_norm, c_q_norm=c_q_norm, c_kv_norm=c_kv_norm,
                  c_w_uq=c_w_uq, c_w_ukv=c_w_ukv, d_conv_w=d_conv_w, d_conv_b=d_conv_b, d_ln_g=d_ln_g,
                  d_ln_b=d_ln_b, w_branch=w_branch, w_out=w_out)
    in_cols = _in_proj_columns()
    cos_a, sin_a, cos_c, sin_c = _rope_tables(seq)
    na_bias = _na_bias_tables(b_rpb, seq // GRID_W)
    xf = xb = None
    for l in range(DEPTH):
        p = _prep_layer(l, params, in_cols)
        if l == 0:
            h, xf = _in_proj(x.reshape(t, d), p["w_gate"], p["w_rest"], ln=(ln_in_g, ln_in_b))
        else:
            h = _in_proj(xb, p["w_gate"], p["w_rest"])
        ya = _attn_a(h, cos_a, sin_a, p["gq_a"], p["gk_a"], batch, seq)
        yb = _attn_b(h, na_bias, l, batch, seq)
        yc = _attn_c(h, p["wuq"], p["wuk"], p["wuv"], p["gq_c"], p["gkv_c"], cos_c, sin_c, batch, seq)
        yd = _conv_branch(h, p["conv_w"], p["conv_b"], p["d_ln_g"], p["d_ln_b"], batch, seq)
        xf, xb = _merge(h, ya, yb, yc, yd, xf, p["w_branch"], p["w_out"], ln_mix_g[l], ln_mix_b[l])
        if l % 2 == 0:
            i = l // 2
            xf, xb = _ffn(xb, xf, ffn_w1[i].astype(bf), ffn_w3[i].astype(bf), ffn_w2[i].astype(bf),
                          ln_ffn_g[l], ln_ffn_b[l])
        else:
            i = l // 2
            xf = _moe_ffn(xf, moe_router[i], moe_w1[i], moe_w3[i], moe_w2[i], ln_ffn_g[l], ln_ffn_b[l])
            xb = xf.astype(bf)
    return xf.reshape(batch, seq, d)
```
